```python
import jax, jax.numpy as jnp
from jax import lax
import numpy as np

D_MODEL = 1024
BATCH = 8
SEQ = 16384
DEPTH = 2

N_MIXERS = 2
EXPAND = 2
D_INNER = EXPAND * D_MODEL
RMS_EPS = 1e-6

GLA_HEADS = 4
GLA_DK = D_MODEL // 2
GLA_DV = D_INNER
GLA_HEAD_K = GLA_DK // GLA_HEADS
GLA_HEAD_V = GLA_DV // GLA_HEADS
GLA_GATE_RANK = 16
GLA_GATE_NORMALIZER = 16.0
GLA_CHUNK = 64
GLA_PROJ = 2 * GLA_DK + 2 * GLA_DV + GLA_GATE_RANK

SSD_HEAD_DIM = 64
SSD_HEADS = D_INNER // SSD_HEAD_DIM
SSD_GROUPS = 8
SSD_HEADS_PER_GROUP = SSD_HEADS // SSD_GROUPS
SSD_STATE = 128
SSD_CONV = 4
SSD_CHUNK = 64
SSD_CONV_DIM = D_INNER + 2 * SSD_GROUPS * SSD_STATE
SSD_PROJ = D_INNER + SSD_CONV_DIM + SSD_HEADS

N_GLA = (DEPTH + 1) // 2
N_SSD = DEPTH // 2

kernel_name = "hybrid_gla_mamba2_interleaved"


def rmsnorm(x, w):
    xf = x.astype(jnp.float32)
    return xf * lax.rsqrt(jnp.mean(xf * xf, axis=-1, keepdims=True) + RMS_EPS) * w.astype(jnp.float32)


def gla_mixer(h, w_in, w_gate_up, b_gate_up, w_head_norm, w_out):
    bsz, L, _ = h.shape
    nc = L // GLA_CHUNK
    proj = h @ w_in
    q, k, v, g, gk_low = jnp.split(
        proj, [GLA_DK, 2 * GLA_DK, 2 * GLA_DK + GLA_DV, 2 * GLA_DK + 2 * GLA_DV], axis=-1)
    log_a = jax.nn.log_sigmoid((gk_low @ w_gate_up + b_gate_up).astype(jnp.float32)) / GLA_GATE_NORMALIZER

    def to_chunks(t, d):
        return t.astype(jnp.float32).reshape(bsz, nc, GLA_CHUNK, GLA_HEADS, d).transpose(1, 0, 3, 2, 4)

    qc = to_chunks(q, GLA_HEAD_K) * (GLA_HEAD_K ** -0.5)
    kc = to_chunks(k, GLA_HEAD_K)
    vc = to_chunks(v, GLA_HEAD_V)
    ac = to_chunks(log_a, GLA_HEAD_K)
    causal = jnp.tril(jnp.ones((GLA_CHUNK, GLA_CHUNK), dtype=bool))[None, None, :, :, None]

    def step(S, inp):
        qi, ki, vi, ai = inp
        b = jnp.cumsum(ai, axis=-2)
        o_inter = jnp.einsum('bhcd,bhdv->bhcv', qi * jnp.exp(b), S)
        rel = jnp.where(causal, b[:, :, :, None, :] - b[:, :, None, :, :], -jnp.inf)
        attn = jnp.sum(qi[:, :, :, None, :] * ki[:, :, None, :, :] * jnp.exp(rel), axis=-1)
        o = o_inter + jnp.einsum('bhij,bhjv->bhiv', attn, vi)
        b_last = b[:, :, -1:, :]
        S = jnp.exp(b_last[:, :, 0, :, None]) * S + jnp.einsum(
            'bhjd,bhjv->bhdv', ki * jnp.exp(b_last - b), vi)
        return S, o

    S0 = jnp.zeros((bsz, GLA_HEADS, GLA_HEAD_K, GLA_HEAD_V), jnp.float32)
    _, o = lax.scan(step, S0, (qc, kc, vc, ac))
    o = o.transpose(1, 0, 3, 2, 4).reshape(bsz, L, GLA_HEADS, GLA_HEAD_V)
    o = rmsnorm(o, w_head_norm).reshape(bsz, L, GLA_DV) * jax.nn.silu(g.astype(jnp.float32))
    return o.astype(h.dtype) @ w_out


def ssd_mixer(h, w_in, conv_w, conv_b, dt_bias, a_log, d_skip, w_gate_norm, w_out):
    bsz, L, _ = h.shape
    nc = L // SSD_CHUNK
    G, HG, P, N, C = SSD_GROUPS, SSD_HEADS_PER_GROUP, SSD_HEAD_DIM, SSD_STATE, SSD_CHUNK
    proj = h @ w_in
    z, xbc, dt = jnp.split(proj, [D_INNER, D_INNER + SSD_CONV_DIM], axis=-1)
    xbc = lax.conv_general_dilated(
        xbc, conv_w[:, None, :].astype(xbc.dtype), window_strides=(1,), padding=[(SSD_CONV - 1, 0)],
        dimension_numbers=('NWC', 'WIO', 'NWC'), feature_group_count=SSD_CONV_DIM)
    xbc = jax.nn.silu(xbc + conv_b)
    xs, Bm, Cm = jnp.split(xbc, [D_INNER, D_INNER + G * N], axis=-1)
    xs = xs.astype(jnp.float32).reshape(bsz, nc, C, G, HG, P)
    Bm = Bm.astype(jnp.float32).reshape(bsz, nc, C, G, N)
    Cm = Cm.astype(jnp.float32).reshape(bsz, nc, C, G, N)
    dt = jax.nn.softplus(dt.astype(jnp.float32) + dt_bias.astype(jnp.float32)).reshape(bsz, nc, C, G, HG)
    A = -jnp.exp(a_log.astype(jnp.float32)).reshape(G, HG)
    a_cum = jnp.cumsum(jnp.moveaxis(dt * A, 2, -1), axis=-1)
    xdt = xs * dt[..., None]

    causal = jnp.tril(jnp.ones((C, C), dtype=bool))
    Lmat = jnp.exp(jnp.where(causal, a_cum[..., :, None] - a_cum[..., None, :], -jnp.inf))
    cb = jnp.einsum('bzlgn,bzsgn->bzgls', Cm, Bm)
    y_diag = jnp.einsum('bzghls,bzsghp->bzlghp', cb[:, :, :, None] * Lmat, xdt)

    decay_to_end = jnp.moveaxis(jnp.exp(a_cum[..., -1:] - a_cum), -1, 2)
    states = jnp.einsum('bzsgn,bzsghp->bzghpn', Bm, xdt * decay_to_end[..., None])
    chunk_decay = jnp.exp(a_cum[..., -1])

    def step(hs, inp):
        st, dec = inp
        return dec[..., None, None] * hs + st, hs

    h0 = jnp.zeros((bsz, G, HG, P, N), jnp.float32)
    _, h_in = lax.scan(step, h0, (jnp.moveaxis(states, 1, 0), jnp.moveaxis(chunk_decay, 1, 0)))
    h_in = jnp.moveaxis(h_in, 0, 1)
    decay_from_start = jnp.moveaxis(jnp.exp(a_cum), -1, 2)
    y_off = jnp.einsum('bzlgn,bzghpn->bzlghp', Cm, h_in) * decay_from_start[..., None]

    y = y_diag + y_off + xs * d_skip.astype(jnp.float32).reshape(G, HG)[..., None]
    y = y.reshape(bsz, L, D_INNER) * jax.nn.silu(z.astype(jnp.float32))
    y = rmsnorm(y, w_gate_norm)
    return y.astype(h.dtype) @ w_out


def _fwd_setup_inputs(seed: int = 0) -> dict:
    key = jax.random.key(seed)
    ks = jax.random.split(key, 17)
    f32 = jnp.float32

    def normal(k, shape, scale):
        return jax.random.normal(k, shape, f32) * scale

    x = jax.random.normal(ks[0], (BATCH, SEQ, D_MODEL), f32)
    norm_w = 1.0 + normal(ks[1], (DEPTH, D_MODEL), 0.02)
    gla_in_proj = normal(ks[2], (N_GLA, D_MODEL, GLA_PROJ), D_MODEL ** -0.5)
    gla_gate_up = normal(ks[3], (N_GLA, GLA_GATE_RANK, GLA_DK), GLA_GATE_RANK ** -0.5)
    gla_gate_bias = normal(ks[4], (N_GLA, GLA_DK), 0.1)
    gla_head_norm = 1.0 + normal(ks[5], (N_GLA, GLA_HEAD_V), 0.02)
    gla_out_proj = normal(ks[6], (N_GLA, GLA_DV, D_MODEL), GLA_DV ** -0.5)
    ssd_in_proj = normal(ks[7], (N_SSD, D_MODEL, SSD_PROJ), D_MODEL ** -0.5)
    ssd_conv_w = normal(ks[8], (N_SSD, SSD_CONV, SSD_CONV_DIM), SSD_CONV ** -0.5)
    ssd_conv_b = normal(ks[9], (N_SSD, SSD_CONV_DIM), 0.02)
    dt0 = jnp.exp(jax.random.uniform(ks[10], (N_SSD, SSD_HEADS), f32, np.log(1e-3), np.log(1e-1)))
    ssd_dt_bias = dt0 + jnp.log(-jnp.expm1(-dt0))
    ssd_a_log = jnp.log(jax.random.uniform(ks[11], (N_SSD, SSD_HEADS), f32, 1.0, 16.0))
    ssd_d = 1.0 + normal(ks[12], (N_SSD, SSD_HEADS), 0.02)
    ssd_gate_norm = 1.0 + normal(ks[13], (N_SSD, D_INNER), 0.02)
    ssd_out_proj = normal(ks[14], (N_SSD, D_INNER, D_MODEL), D_INNER ** -0.5)
    final_norm = 1.0 + normal(ks[15], (D_MODEL,), 0.02)
    return {"x": x, "norm_w": norm_w,
            "gla_in_proj": gla_in_proj, "gla_gate_up": gla_gate_up, "gla_gate_bias": gla_gate_bias,
            "gla_head_norm": gla_head_norm, "gla_out_proj": gla_out_proj,
            "ssd_in_proj": ssd_in_proj, "ssd_conv_w": ssd_conv_w, "ssd_conv_b": ssd_conv_b,
            "ssd_dt_bias": ssd_dt_bias, "ssd_a_log": ssd_a_log, "ssd_d": ssd_d,
            "ssd_gate_norm": ssd_gate_norm, "ssd_out_proj": ssd_out_proj,
            "final_norm": final_norm}


def _fwd_reference(x, norm_w, gla_in_proj, gla_gate_up, gla_gate_bias, gla_head_norm, gla_out_proj,
              ssd_in_proj, ssd_conv_w, ssd_conv_b, ssd_dt_bias, ssd_a_log, ssd_d,
              ssd_gate_norm, ssd_out_proj, final_norm):
    for i in range(DEPTH):
        hn = rmsnorm(x, norm_w[i]).astype(x.dtype)
        j = i // N_MIXERS
        if i % N_MIXERS == 0:
            y = gla_mixer(hn, gla_in_proj[j], gla_gate_up[j], gla_gate_bias[j], gla_head_norm[j],
                          gla_out_proj[j])
        else:
            y = ssd_mixer(hn, ssd_in_proj[j], ssd_conv_w[j], ssd_conv_b[j], ssd_dt_bias[j],
                          ssd_a_log[j], ssd_d[j], ssd_gate_norm[j], ssd_out_proj[j])
        x = x + y
    return rmsnorm(x, final_norm).astype(x.dtype)


import jax as _jax
import jax.numpy as _jnp

TWIN_FORMAT = 'train_step'
FWD_PARAMS = ['x', 'norm_w', 'gla_in_proj', 'gla_gate_up', 'gla_gate_bias', 'gla_head_norm', 'gla_out_proj', 'ssd_in_proj', 'ssd_conv_w', 'ssd_conv_b', 'ssd_dt_bias', 'ssd_a_log', 'ssd_d', 'ssd_gate_norm', 'ssd_out_proj', 'final_norm']
TWIN_WEIGHTS = ['norm_w', 'gla_in_proj', 'gla_gate_up', 'gla_gate_bias', 'gla_head_norm', 'gla_out_proj', 'ssd_in_proj', 'ssd_conv_w', 'ssd_conv_b', 'ssd_dt_bias', 'ssd_a_log', 'ssd_d', 'ssd_gate_norm', 'ssd_out_proj', 'final_norm']
TWIN_DIFF_INPUT = 'x'
TWIN_INPUTS = ['x', 'norm_w', 'gla_in_proj', 'gla_gate_up', 'gla_gate_bias', 'gla_head_norm', 'gla_out_proj', 'ssd_in_proj', 'ssd_conv_w', 'ssd_conv_b', 'ssd_dt_bias', 'ssd_a_log', 'ssd_d', 'ssd_gate_norm', 'ssd_out_proj', 'final_norm', 'loss_target', 'm_norm_w', 'm_gla_in_proj', 'm_gla_gate_up', 'm_gla_gate_bias', 'm_gla_head_norm', 'm_gla_out_proj', 'm_ssd_in_proj', 'm_ssd_conv_w', 'm_ssd_conv_b', 'm_ssd_dt_bias', 'm_ssd_a_log', 'm_ssd_d', 'm_ssd_gate_norm', 'm_ssd_out_proj', 'm_final_norm', 'v_norm_w', 'v_gla_in_proj', 'v_gla_gate_up', 'v_gla_gate_bias', 'v_gla_head_norm', 'v_gla_out_proj', 'v_ssd_in_proj', 'v_ssd_conv_w', 'v_ssd_conv_b', 'v_ssd_dt_bias', 'v_ssd_a_log', 'v_ssd_d', 'v_ssd_gate_norm', 'v_ssd_out_proj', 'v_final_norm']
TWIN_OUTPUTS = ['loss', 'grad_x', 'grad_norm_w', 'grad_gla_in_proj', 'grad_gla_gate_up', 'grad_gla_gate_bias', 'grad_gla_head_norm', 'grad_gla_out_proj', 'grad_ssd_in_proj', 'grad_ssd_conv_w', 'grad_ssd_conv_b', 'grad_ssd_dt_bias', 'grad_ssd_a_log', 'grad_ssd_d', 'grad_ssd_gate_norm', 'grad_ssd_out_proj', 'grad_final_norm', 'delta_norm_w', 'delta_gla_in_proj', 'delta_gla_gate_up', 'delta_gla_gate_bias', 'delta_gla_head_norm', 'delta_gla_out_proj', 'delta_ssd_in_proj', 'delta_ssd_conv_w', 'delta_ssd_conv_b', 'delta_ssd_dt_bias', 'delta_ssd_a_log', 'delta_ssd_d', 'delta_ssd_gate_norm', 'delta_ssd_out_proj', 'delta_final_norm', 'new_m_norm_w', 'new_m_gla_in_proj', 'new_m_gla_gate_up', 'new_m_gla_gate_bias', 'new_m_gla_head_norm', 'new_m_gla_out_proj', 'new_m_ssd_in_proj', 'new_m_ssd_conv_w', 'new_m_ssd_conv_b', 'new_m_ssd_dt_bias', 'new_m_ssd_a_log', 'new_m_ssd_d', 'new_m_ssd_gate_norm', 'new_m_ssd_out_proj', 'new_m_final_norm', 'new_v_norm_w', 'new_v_gla_in_proj', 'new_v_gla_gate_up', 'new_v_gla_gate_bias', 'new_v_gla_head_norm', 'new_v_gla_out_proj', 'new_v_ssd_in_proj', 'new_v_ssd_conv_w', 'new_v_ssd_conv_b', 'new_v_ssd_dt_bias', 'new_v_ssd_a_log', 'new_v_ssd_d', 'new_v_ssd_gate_norm', 'new_v_ssd_out_proj', 'new_v_final_norm']
TWIN_LEAF_KINDS = {'loss': 'loss', 'grad_x': 'grad_x', 'grad_norm_w': 'grad_w', 'grad_gla_in_proj': 'grad_w', 'grad_gla_gate_up': 'grad_w', 'grad_gla_gate_bias': 'grad_w', 'grad_gla_head_norm': 'grad_w', 'grad_gla_out_proj': 'grad_w', 'grad_ssd_in_proj': 'grad_w', 'grad_ssd_conv_w': 'grad_w', 'grad_ssd_conv_b': 'grad_w', 'grad_ssd_dt_bias': 'grad_w', 'grad_ssd_a_log': 'grad_w', 'grad_ssd_d': 'grad_w', 'grad_ssd_gate_norm': 'grad_w', 'grad_ssd_out_proj': 'grad_w', 'grad_final_norm': 'grad_w', 'delta_norm_w': 'delta_w', 'delta_gla_in_proj': 'delta_w', 'delta_gla_gate_up': 'delta_w', 'delta_gla_gate_bias': 'delta_w', 'delta_gla_head_norm': 'delta_w', 'delta_gla_out_proj': 'delta_w', 'delta_ssd_in_proj': 'delta_w', 'delta_ssd_conv_w': 'delta_w', 'delta_ssd_conv_b': 'delta_w', 'delta_ssd_dt_bias': 'delta_w', 'delta_ssd_a_log': 'delta_w', 'delta_ssd_d': 'delta_w', 'delta_ssd_gate_norm': 'delta_w', 'delta_ssd_out_proj': 'delta_w', 'delta_final_norm': 'delta_w', 'new_m_norm_w': 'new_m', 'new_m_gla_in_proj': 'new_m', 'new_m_gla_gate_up': 'new_m', 'new_m_gla_gate_bias': 'new_m', 'new_m_gla_head_norm': 'new_m', 'new_m_gla_out_proj': 'new_m', 'new_m_ssd_in_proj': 'new_m', 'new_m_ssd_conv_w': 'new_m', 'new_m_ssd_conv_b': 'new_m', 'new_m_ssd_dt_bias': 'new_m', 'new_m_ssd_a_log': 'new_m', 'new_m_ssd_d': 'new_m', 'new_m_ssd_gate_norm': 'new_m', 'new_m_ssd_out_proj': 'new_m', 'new_m_final_norm': 'new_m', 'new_v_norm_w': 'new_v', 'new_v_gla_in_proj': 'new_v', 'new_v_gla_gate_up': 'new_v', 'new_v_gla_gate_bias': 'new_v', 'new_v_gla_head_norm': 'new_v', 'new_v_gla_out_proj': 'new_v', 'new_v_ssd_in_proj': 'new_v', 'new_v_ssd_conv_w': 'new_v', 'new_v_ssd_conv_b': 'new_v', 'new_v_ssd_dt_bias': 'new_v', 'new_v_ssd_a_log': 'new_v', 'new_v_ssd_d': 'new_v', 'new_v_ssd_gate_norm': 'new_v', 'new_v_ssd_out_proj': 'new_v', 'new_v_final_norm': 'new_v'}


def _forward(args):
    return _fwd_reference(*[args[k] for k in FWD_PARAMS])


def _output_shape():
    def fwd():
        inp = _fwd_setup_inputs(0)
        return _fwd_reference(*[inp[k] for k in FWD_PARAMS])
    out = _jax.eval_shape(fwd)
    return out.shape, out.dtype

N_MICROBATCH = 1
ADAM_LR = 0.001
ADAM_B1 = 0.9
ADAM_B2 = 0.999
ADAM_EPS = 1e-08
ADAM_WD = 0.01
ADAM_STEP = 10
PER_EXAMPLE_BATCH_AXIS = {'x': 0, 'loss_target': 0}
SHARED_INPUTS = []
_WEIGHT_DTYPES = {'norm_w': _jnp.float32, 'gla_in_proj': _jnp.float32, 'gla_gate_up': _jnp.float32, 'gla_gate_bias': _jnp.float32, 'gla_head_norm': _jnp.float32, 'gla_out_proj': _jnp.float32, 'ssd_in_proj': _jnp.float32, 'ssd_conv_w': _jnp.float32, 'ssd_conv_b': _jnp.float32, 'ssd_dt_bias': _jnp.float32, 'ssd_a_log': _jnp.float32, 'ssd_d': _jnp.float32, 'ssd_gate_norm': _jnp.float32, 'ssd_out_proj': _jnp.float32, 'final_norm': _jnp.float32}
MOMENT_SCALE = {'norm_w': 4.338240e-01, 'gla_in_proj': 2.081487e-01, 'gla_gate_up': 3.940001e-02, 'gla_gate_bias': 1.641704e-01, 'gla_head_norm': 3.535430e-01, 'gla_out_proj': 2.256852e-01, 'ssd_in_proj': 1.426687e-01, 'ssd_conv_w': 1.262820e-01, 'ssd_conv_b': 1.822499e-01, 'ssd_dt_bias': 4.568644e-01, 'ssd_a_log': 5.970184e-01, 'ssd_d': 1.149828e+00, 'ssd_gate_norm': 1.908622e-01, 'ssd_out_proj': 2.351091e-01, 'final_norm': 1.279349e+02}


def _to_microbatches(a, axis):
    t = _jnp.moveaxis(a, axis, 0)
    t = t.reshape((N_MICROBATCH, t.shape[0] // N_MICROBATCH) + t.shape[1:])
    return _jnp.moveaxis(t, 1, axis + 1)


def setup_inputs(seed: int = 0) -> dict:
    inp = _fwd_setup_inputs(seed)
    key = _jax.random.fold_in(_jax.random.key(seed), 7919)
    shape, _ = _output_shape()
    out = dict(inp)
    out["loss_target"] = _jax.random.normal(_jax.random.fold_in(key, 0), shape, _jnp.float32)
    for i, name in enumerate(TWIN_WEIGHTS):
        w = inp[name].astype(_jnp.float32)
        if MOMENT_SCALE is None:
            s = _jnp.sqrt(_jnp.mean(_jnp.square(w)) + 1e-30)
        else:
            s = MOMENT_SCALE[name]
        km, kv = _jax.random.split(_jax.random.fold_in(key, i + 1))
        out[name] = w
        out["m_" + name] = s * _jax.random.normal(km, w.shape, _jnp.float32)
        out["v_" + name] = (s * s) * _jax.random.uniform(kv, w.shape, _jnp.float32, 0.5, 1.5)
    if N_MICROBATCH > 1:
        for name, axis in PER_EXAMPLE_BATCH_AXIS.items():
            out[name] = _to_microbatches(out[name], axis)
    return {'x': out['x'], 'norm_w': out['norm_w'], 'gla_in_proj': out['gla_in_proj'], 'gla_gate_up': out['gla_gate_up'], 'gla_gate_bias': out['gla_gate_bias'], 'gla_head_norm': out['gla_head_norm'], 'gla_out_proj': out['gla_out_proj'], 'ssd_in_proj': out['ssd_in_proj'], 'ssd_conv_w': out['ssd_conv_w'], 'ssd_conv_b': out['ssd_conv_b'], 'ssd_dt_bias': out['ssd_dt_bias'], 'ssd_a_log': out['ssd_a_log'], 'ssd_d': out['ssd_d'], 'ssd_gate_norm': out['ssd_gate_norm'], 'ssd_out_proj': out['ssd_out_proj'], 'final_norm': out['final_norm'], 'loss_target': out['loss_target'], 'm_norm_w': out['m_norm_w'], 'm_gla_in_proj': out['m_gla_in_proj'], 'm_gla_gate_up': out['m_gla_gate_up'], 'm_gla_gate_bias': out['m_gla_gate_bias'], 'm_gla_head_norm': out['m_gla_head_norm'], 'm_gla_out_proj': out['m_gla_out_proj'], 'm_ssd_in_proj': out['m_ssd_in_proj'], 'm_ssd_conv_w': out['m_ssd_conv_w'], 'm_ssd_conv_b': out['m_ssd_conv_b'], 'm_ssd_dt_bias': out['m_ssd_dt_bias'], 'm_ssd_a_log': out['m_ssd_a_log'], 'm_ssd_d': out['m_ssd_d'], 'm_ssd_gate_norm': out['m_ssd_gate_norm'], 'm_ssd_out_proj': out['m_ssd_out_proj'], 'm_final_norm': out['m_final_norm'], 'v_norm_w': out['v_norm_w'], 'v_gla_in_proj': out['v_gla_in_proj'], 'v_gla_gate_up': out['v_gla_gate_up'], 'v_gla_gate_bias': out['v_gla_gate_bias'], 'v_gla_head_norm': out['v_gla_head_norm'], 'v_gla_out_proj': out['v_gla_out_proj'], 'v_ssd_in_proj': out['v_ssd_in_proj'], 'v_ssd_conv_w': out['v_ssd_conv_w'], 'v_ssd_conv_b': out['v_ssd_conv_b'], 'v_ssd_dt_bias': out['v_ssd_dt_bias'], 'v_ssd_a_log': out['v_ssd_a_log'], 'v_ssd_d': out['v_ssd_d'], 'v_ssd_gate_norm': out['v_ssd_gate_norm'], 'v_ssd_out_proj': out['v_ssd_out_proj'], 'v_final_norm': out['v_final_norm']}


def _loss(weights, diff, rest, loss_target):
    with _jax.named_scope("forward"):
        args = {**rest, TWIN_DIFF_INPUT: diff, **{k: w.astype(_WEIGHT_DTYPES[k]) for k, w in weights.items()}}
        y = _forward(args)
    with _jax.named_scope("loss_head"):
        err = _jnp.square(y.astype(_jnp.float32) - loss_target)
        return 0.5 * _jnp.sum(_jnp.mean(err, axis=-1)) if err.ndim else 0.5 * err


def _adamw(w, g, m, v):
    m = ADAM_B1 * m + (1.0 - ADAM_B1) * g
    v = ADAM_B2 * v + (1.0 - ADAM_B2) * _jnp.square(g)
    m_hat = m / (1.0 - ADAM_B1 ** ADAM_STEP)
    v_hat = v / (1.0 - ADAM_B2 ** ADAM_STEP)
    delta = -ADAM_LR * (m_hat / (_jnp.sqrt(v_hat) + ADAM_EPS) + ADAM_WD * w)
    return delta, m, v


def reference(x, norm_w, gla_in_proj, gla_gate_up, gla_gate_bias, gla_head_norm, gla_out_proj, ssd_in_proj, ssd_conv_w, ssd_conv_b, ssd_dt_bias, ssd_a_log, ssd_d, ssd_gate_norm, ssd_out_proj, final_norm, loss_target, m_norm_w, m_gla_in_proj, m_gla_gate_up, m_gla_gate_bias, m_gla_head_norm, m_gla_out_proj, m_ssd_in_proj, m_ssd_conv_w, m_ssd_conv_b, m_ssd_dt_bias, m_ssd_a_log, m_ssd_d, m_ssd_gate_norm, m_ssd_out_proj, m_final_norm, v_norm_w, v_gla_in_proj, v_gla_gate_up, v_gla_gate_bias, v_gla_head_norm, v_gla_out_proj, v_ssd_in_proj, v_ssd_conv_w, v_ssd_conv_b, v_ssd_dt_bias, v_ssd_a_log, v_ssd_d, v_ssd_gate_norm, v_ssd_out_proj, v_final_norm):
    given = dict(x=x, norm_w=norm_w, gla_in_proj=gla_in_proj, gla_gate_up=gla_gate_up, gla_gate_bias=gla_gate_bias, gla_head_norm=gla_head_norm, gla_out_proj=gla_out_proj, ssd_in_proj=ssd_in_proj, ssd_conv_w=ssd_conv_w, ssd_conv_b=ssd_conv_b, ssd_dt_bias=ssd_dt_bias, ssd_a_log=ssd_a_log, ssd_d=ssd_d, ssd_gate_norm=ssd_gate_norm, ssd_out_proj=ssd_out_proj, final_norm=final_norm, loss_target=loss_target, m_norm_w=m_norm_w, m_gla_in_proj=m_gla_in_proj, m_gla_gate_up=m_gla_gate_up, m_gla_gate_bias=m_gla_gate_bias, m_gla_head_norm=m_gla_head_norm, m_gla_out_proj=m_gla_out_proj, m_ssd_in_proj=m_ssd_in_proj, m_ssd_conv_w=m_ssd_conv_w, m_ssd_conv_b=m_ssd_conv_b, m_ssd_dt_bias=m_ssd_dt_bias, m_ssd_a_log=m_ssd_a_log, m_ssd_d=m_ssd_d, m_ssd_gate_norm=m_ssd_gate_norm, m_ssd_out_proj=m_ssd_out_proj, m_final_norm=m_final_norm, v_norm_w=v_norm_w, v_gla_in_proj=v_gla_in_proj, v_gla_gate_up=v_gla_gate_up, v_gla_gate_bias=v_gla_gate_bias, v_gla_head_norm=v_gla_head_norm, v_gla_out_proj=v_gla_out_proj, v_ssd_in_proj=v_ssd_in_proj, v_ssd_conv_w=v_ssd_conv_w, v_ssd_conv_b=v_ssd_conv_b, v_ssd_dt_bias=v_ssd_dt_bias, v_ssd_a_log=v_ssd_a_log, v_ssd_d=v_ssd_d, v_ssd_gate_norm=v_ssd_gate_norm, v_ssd_out_proj=v_ssd_out_proj, v_final_norm=v_final_norm)
    weights = {n: given[n] for n in TWIN_WEIGHTS}
    shared = {n: given[n] for n in SHARED_INPUTS}
    per_example = {n: given[n] for n in ['x']}
    grad_fn = _jax.value_and_grad(_loss, argnums=(0, 1))

    def one_microbatch(ex, loss_target):
        ex = dict(ex)
        diff = ex.pop(TWIN_DIFF_INPUT)
        return grad_fn(weights, diff, {**shared, **ex}, loss_target)

    if N_MICROBATCH == 1:
        loss, (grad_w, grad_x) = one_microbatch(per_example, given["loss_target"])
    else:
        def body(carry, xs):
            loss_sum, grad_sum = carry
            l_k, (gw_k, gx_k) = one_microbatch(xs[0], xs[1])
            with _jax.named_scope("update"):
                return (loss_sum + l_k, _jax.tree.map(_jnp.add, grad_sum, gw_k)), gx_k

        init = (_jnp.zeros((), _jnp.float32), _jax.tree.map(_jnp.zeros_like, weights))
        (loss, grad_w), grad_x = _jax.lax.scan(body, init, (per_example, given["loss_target"]))
    with _jax.named_scope("update"):
        delta_w, new_m, new_v = {}, {}, {}
        for n in TWIN_WEIGHTS:
            delta_w[n], new_m[n], new_v[n] = _adamw(weights[n], grad_w[n], given["m_" + n], given["v_" + n])
    return (loss, grad_x, *[grad_w[n] for n in TWIN_WEIGHTS], *[delta_w[n] for n in TWIN_WEIGHTS],
            *[new_m[n] for n in TWIN_WEIGHTS], *[new_v[n] for n in TWIN_WEIGHTS])
```

```python
import functools

import jax
import jax.numpy as jnp
from jax import lax
from jax.experimental import pallas as pl
from jax.experimental.pallas import tpu as pltpu

F32 = jnp.float32
BF16 = jnp.bfloat16
HI = lax.Precision.HIGHEST

D_MODEL = 1024
D_INNER = 2048
RMS_EPS = 1e-6
GLA_HEADS = 4
GLA_DK = 512
GLA_HEAD_K = 128
GLA_HEAD_V = 512
GLA_RANK = 16
GLA_NORMALIZER = 16.0
CHUNK = 64
SUB = 16
GLA_PROJ = 5136
GLA_PROJ_PAD = 5376
GLA_GK_COL = 5120
SSD_HEADS = 32
SSD_GROUPS = 8
SSD_HPG = 4
SSD_P = 64
SSD_N = 128
SSD_CONV = 4
SSD_CONV_DIM = 4096
SSD_PROJ = 6176
SSD_PROJ_PAD = 6272
SSD_DT_COL = 6144
N_DEV = 8

ADAM_LR = 0.001
ADAM_B1 = 0.9
ADAM_B2 = 0.999
ADAM_EPS = 1e-08
ADAM_WD = 0.01
ADAM_STEP = 10

VMEM_LIMIT = 56 * 1024 * 1024


def _dot(a, b, prec=None):
    return jnp.dot(a, b, preferred_element_type=F32, precision=prec)


def _dot_nt(a, b, prec=None):
    return lax.dot_general(a, b, (((1,), (1,)), ((), ())), preferred_element_type=F32, precision=prec)


def _dot_tn(a, b, prec=None):
    return lax.dot_general(a, b, (((0,), (0,)), ((), ())), preferred_element_type=F32, precision=prec)


def _bf(a):
    return a.astype(BF16)


@jax.custom_vjp
def _mxu(a, b):
    return _dot(_bf(a), _bf(b))


def _mxu_fwd(a, b):
    return _mxu(a, b), (a, b)


def _mxu_bwd(res, g):
    a, b = res
    return _dot_nt(_bf(g), _bf(b)), _dot_tn(_bf(a), _bf(g))


_mxu.defvjp(_mxu_fwd, _mxu_bwd)


@jax.custom_vjp
def _mxu_nt(a, b):
    return _dot_nt(_bf(a), _bf(b))


def _mxu_nt_fwd(a, b):
    return _mxu_nt(a, b), (a, b)


def _mxu_nt_bwd(res, g):
    a, b = res
    return _dot(_bf(g), _bf(b)), _dot_tn(_bf(g), _bf(a))


_mxu_nt.defvjp(_mxu_nt_fwd, _mxu_nt_bwd)


@jax.custom_vjp
def _mxu_tn(a, b):
    return _dot_tn(_bf(a), _bf(b))


def _mxu_tn_fwd(a, b):
    return _mxu_tn(a, b), (a, b)


def _mxu_tn_bwd(res, g):
    a, b = res
    return _dot_nt(_bf(b), _bf(g)), _dot(_bf(a), _bf(g))


_mxu_tn.defvjp(_mxu_tn_fwd, _mxu_tn_bwd)


def _silu(x):
    return x / (1.0 + jnp.exp(-x))


def _log_sigmoid(z):
    return jnp.minimum(z, 0.0) - jnp.log(1.0 + jnp.exp(-jnp.abs(z)))


def _softplus(z):
    return jnp.maximum(z, 0.0) + jnp.log(1.0 + jnp.exp(-jnp.abs(z)))


def _iota(shape, dim):
    return lax.broadcasted_iota(jnp.int32, shape, dim)


def _gla_chunk(q, k, v, gk, wup, bias, St):
    nb = CHUNK // SUB
    z = _mxu(gk, wup) + bias
    la = _log_sigmoid(z) * (1.0 / GLA_NORMALIZER)
    qs = q * (GLA_HEAD_K ** -0.5)
    r = _iota((CHUNK, CHUNK), 0)
    c = _iota((CHUNK, CHUNK), 1)
    blk_tri = jnp.where((r // SUB == c // SUB) & (c <= r), 1.0, 0.0).astype(F32)
    blk_one = jnp.where(r // SUB == c // SUB, 1.0, 0.0).astype(F32)
    bl = _dot(blk_tri, la, HI)
    suf = _dot(blk_one, la, HI) - bl
    pre = [jnp.zeros((1, GLA_HEAD_K), F32)]
    for i in range(1, nb + 1):
        pre.append(jnp.sum(la[: i * SUB], axis=0, keepdims=True))
    b_last = pre[nb]
    rs = _iota((SUB, GLA_HEAD_K), 0)
    cs = _iota((SUB, SUB), 1)
    outs = []
    kdec = []
    for i in range(nb):
        sl = slice(i * SUB, (i + 1) * SUB)
        q_i, k_i, bl_i = qs[sl], k[sl], bl[sl]
        b_i = bl_i + pre[i]
        o_i = _mxu_nt(q_i * jnp.exp(b_i), St)
        qa = q_i * jnp.exp(bl_i)
        if i > 0:
            kp = jnp.concatenate(
                [k[j * SUB:(j + 1) * SUB] * jnp.exp(suf[j * SUB:(j + 1) * SUB] + (pre[i] - pre[j + 1])) for j in range(i)], axis=0)
            a_off = _dot_nt(qa, kp, HI)
            o_i = o_i + _mxu(a_off, v[: i * SUB])
        a_dg = jnp.zeros((SUB, SUB), F32)
        for j in range(SUB):
            e = jnp.exp(jnp.minimum(bl_i - bl_i[j:j + 1], 0.0))
            t = jnp.where(rs >= j, q_i * e * k_i[j:j + 1], 0.0)
            a_dg = a_dg + jnp.where(cs == j, jnp.sum(t, axis=-1, keepdims=True), 0.0)
        o_i = o_i + _mxu(a_dg, v[sl])
        outs.append(o_i)
        kdec.append(k_i * jnp.exp(suf[sl] + (b_last - pre[i + 1])))
    o = jnp.concatenate(outs, axis=0)
    St_new = St * jnp.exp(b_last) + _mxu_tn(v, jnp.concatenate(kdec, axis=0))
    return o, St_new


def _cparams(sem):
    return pltpu.CompilerParams(dimension_semantics=sem, vmem_limit_bytes=VMEM_LIMIT)


def _gla_fwd(proj, wup, bias):
    L = proj.shape[0]
    nc = L // CHUNK

    def body(q_ref, k_ref, v_ref, gk_ref, wup_ref, b_ref, o_ref, s_ref, st):
        h = pl.program_id(1)

        @pl.when(pl.program_id(0) == 0)
        def _():
            st[h] = jnp.zeros((GLA_HEAD_V, GLA_HEAD_K), F32)

        s_in = st[h]
        s_ref[...] = s_in
        o, s_new = _gla_chunk(q_ref[...], k_ref[...], v_ref[...], gk_ref[...], wup_ref[...], b_ref[...], s_in)
        o_ref[...] = o
        st[h] = s_new

    return pl.pallas_call(
        body,
        grid=(nc, GLA_HEADS),
        in_specs=[
            pl.BlockSpec((CHUNK, GLA_HEAD_K), lambda c, h: (c, h)),
            pl.BlockSpec((CHUNK, GLA_HEAD_K), lambda c, h: (c, GLA_HEADS + h)),
            pl.BlockSpec((CHUNK, GLA_HEAD_V), lambda c, h: (c, 2 + h)),
            pl.BlockSpec((CHUNK, 128), lambda c, h: (c, GLA_GK_COL // 128)),
            pl.BlockSpec((128, GLA_HEAD_K), lambda c, h: (0, h)),
            pl.BlockSpec((1, GLA_HEAD_K), lambda c, h: (0, h)),
        ],
        out_specs=[
            pl.BlockSpec((CHUNK, GLA_HEAD_V), lambda c, h: (c, h)),
            pl.BlockSpec((None, None, GLA_HEAD_V, GLA_HEAD_K), lambda c, h: (c, h, 0, 0)),
        ],
        out_shape=[
            jax.ShapeDtypeStruct((L, D_INNER), F32),
            jax.ShapeDtypeStruct((nc, GLA_HEADS, GLA_HEAD_V, GLA_HEAD_K), F32),
        ],
        scratch_shapes=[pltpu.VMEM((GLA_HEADS, GLA_HEAD_V, GLA_HEAD_K), F32)],
        compiler_params=_cparams(("arbitrary", "arbitrary")),
        name="gla_fwd",
    )(proj, proj, proj, proj, wup, bias)


def _gla_bwd(proj, wup, bias, s_in, do):
    L = proj.shape[0]
    nc = L // CHUNK

    def body(q_ref, k_ref, v_ref, gk_ref, wup_ref, b_ref, s_ref, do_ref,
             dq_ref, dk_ref, dv_ref, dgk_ref, dwup_ref, db_ref, dst):
        h = pl.program_id(1)
        first = pl.program_id(0) == 0

        @pl.when(first)
        def _():
            dst[h] = jnp.zeros((GLA_HEAD_V, GLA_HEAD_K), F32)
            dwup_ref[h] = jnp.zeros((128, GLA_HEAD_K), F32)
            db_ref[h] = jnp.zeros((1, GLA_HEAD_K), F32)

        _, vjp = jax.vjp(_gla_chunk, q_ref[...], k_ref[...], v_ref[...], gk_ref[...], wup_ref[...], b_ref[...], s_ref[...])
        dq, dk, dv, dgk, dwup, db, ds = vjp((do_ref[...], dst[h]))
        dq_ref[...] = dq
        dk_ref[...] = dk
        dv_ref[...] = dv
        dst[h] = ds
        dwup_ref[h] += dwup
        db_ref[h] += db

        @pl.when(h == 0)
        def _():
            dgk_ref[...] = dgk

        @pl.when(h != 0)
        def _():
            dgk_ref[...] += dgk

    rc = lambda c: nc - 1 - c
    return pl.pallas_call(
        body,
        grid=(nc, GLA_HEADS),
        in_specs=[
            pl.BlockSpec((CHUNK, GLA_HEAD_K), lambda c, h: (rc(c), h)),
            pl.BlockSpec((CHUNK, GLA_HEAD_K), lambda c, h: (rc(c), GLA_HEADS + h)),
            pl.BlockSpec((CHUNK, GLA_HEAD_V), lambda c, h: (rc(c), 2 + h)),
            pl.BlockSpec((CHUNK, 128), lambda c, h: (rc(c), GLA_GK_COL // 128)),
            pl.BlockSpec((128, GLA_HEAD_K), lambda c, h: (0, h)),
            pl.BlockSpec((1, GLA_HEAD_K), lambda c, h: (0, h)),
            pl.BlockSpec((None, None, GLA_HEAD_V, GLA_HEAD_K), lambda c, h: (rc(c), h, 0, 0)),
            pl.BlockSpec((CHUNK, GLA_HEAD_V), lambda c, h: (rc(c), h)),
        ],
        out_specs=[
            pl.BlockSpec((CHUNK, GLA_HEAD_K), lambda c, h: (rc(c), h)),
            pl.BlockSpec((CHUNK, GLA_HEAD_K), lambda c, h: (rc(c), h)),
            pl.BlockSpec((CHUNK, GLA_HEAD_V), lambda c, h: (rc(c), h)),
            pl.BlockSpec((CHUNK, 128), lambda c, h: (rc(c), 0)),
            pl.BlockSpec((GLA_HEADS, 128, GLA_HEAD_K), lambda c, h: (0, 0, 0)),
            pl.BlockSpec((GLA_HEADS, 1, GLA_HEAD_K), lambda c, h: (0, 0, 0)),
        ],
        out_shape=[
            jax.ShapeDtypeStruct((L, GLA_DK), F32),
            jax.ShapeDtypeStruct((L, GLA_DK), F32),
            jax.ShapeDtypeStruct((L, D_INNER), F32),
            jax.ShapeDtypeStruct((L, 128), F32),
            jax.ShapeDtypeStruct((GLA_HEADS, 128, GLA_HEAD_K), F32),
            jax.ShapeDtypeStruct((GLA_HEADS, 1, GLA_HEAD_K), F32),
        ],
        scratch_shapes=[pltpu.VMEM((GLA_HEADS, GLA_HEAD_V, GLA_HEAD_K), F32)],
        compiler_params=_cparams(("arbitrary", "arbitrary")),
        name="gla_bwd",
    )(proj, proj, proj, proj, wup, bias, s_in, do)


def _ssd_chunk(g, xs, Bm, Cm, dtp, dtb, alog, dsk, h_in):
    gw = SSD_HPG * SSD_P
    dt = _softplus(dtp + dtb)
    a = dt * (-jnp.exp(alog))
    r = _iota((CHUNK, CHUNK), 0)
    c = _iota((CHUNK, CHUNK), 1)
    tri = jnp.where(c <= r, 1.0, 0.0).astype(F32)
    acum = _dot(tri, a, HI)
    a_last = jnp.sum(a, axis=0, keepdims=True)
    e_g = jnp.where(_iota((128, gw), 0) == SSD_HPG * g + _iota((128, gw), 1) // SSD_P, 1.0, 0.0).astype(F32)
    dt_b = _dot(dt, e_g, HI)
    ac_b = _dot(acum, e_g, HI)
    al_b = _dot(jnp.broadcast_to(a_last, (8, 128)), e_g, HI)[0:1]
    d_b = _dot(jnp.broadcast_to(dsk, (8, 128)), e_g, HI)[0:1]
    xdt = xs * dt_b
    cb = _mxu_nt(Cm, Bm)
    lane_head = _iota((CHUNK, gw), 1) // SSD_P
    y = jnp.zeros((CHUNK, gw), F32)
    for j in range(SSD_HPG):
        eh = jnp.where(_iota((128, CHUNK), 0) == SSD_HPG * g + j, 1.0, 0.0).astype(F32)
        eht = jnp.where(_iota((CHUNK, 128), 1) == SSD_HPG * g + j, 1.0, 0.0).astype(F32)
        ac_c = _dot(acum, eh, HI)
        ac_r = _dot_nt(eht, acum, HI)
        lmat = jnp.where(c <= r, jnp.exp(jnp.minimum(ac_c - ac_r, 0.0)), 0.0)
        yj = _mxu(cb * lmat, xdt)
        y = y + jnp.where(lane_head == j, yj, 0.0)
    st = _mxu_tn(xdt * jnp.exp(al_b - ac_b), Bm)
    e_r = jnp.where(_iota((gw, 128), 1) == SSD_HPG * g + _iota((gw, 128), 0) // SSD_P, 1.0, 0.0).astype(F32)
    cd_rows = _dot_nt(e_r, jnp.broadcast_to(jnp.exp(a_last), (SSD_N, 128)), HI)
    h_out = cd_rows * h_in + st
    y = y + _mxu_nt(Cm, h_in) * jnp.exp(ac_b) + xs * d_b
    return y, h_out


def _ssd_fwd(xbc, proj, dtb, alog, dsk):
    L = xbc.shape[0]
    nc = L // CHUNK
    gw = SSD_HPG * SSD_P

    def body(xs_ref, b_ref, c_ref, dt_ref, dtb_ref, alog_ref, dsk_ref, y_ref, hs_ref, hst):
        g = pl.program_id(1)

        @pl.when(pl.program_id(0) == 0)
        def _():
            hst[g] = jnp.zeros((gw, SSD_N), F32)

        h_in = hst[g]
        hs_ref[...] = h_in
        y, h_out = _ssd_chunk(g, xs_ref[...], b_ref[...], c_ref[...], dt_ref[...], dtb_ref[...], alog_ref[...], dsk_ref[...], h_in)
        y_ref[...] = y
        hst[g] = h_out

    vec = pl.BlockSpec((1, 128), lambda c, g: (0, 0))
    return pl.pallas_call(
        body,
        grid=(nc, SSD_GROUPS),
        in_specs=[
            pl.BlockSpec((CHUNK, gw), lambda c, g: (c, g)),
            pl.BlockSpec((CHUNK, SSD_N), lambda c, g: (c, D_INNER // SSD_N + g)),
            pl.BlockSpec((CHUNK, SSD_N), lambda c, g: (c, D_INNER // SSD_N + SSD_GROUPS + g)),
            pl.BlockSpec((CHUNK, 128), lambda c, g: (c, SSD_DT_COL // 128)),
            vec, vec, vec,
        ],
        out_specs=[
            pl.BlockSpec((CHUNK, gw), lambda c, g: (c, g)),
            pl.BlockSpec((None, None, gw, SSD_N), lambda c, g: (c, g, 0, 0)),
        ],
        out_shape=[
            jax.ShapeDtypeStruct((L, D_INNER), F32),
            jax.ShapeDtypeStruct((nc, SSD_GROUPS, gw, SSD_N), F32),
        ],
        scratch_shapes=[pltpu.VMEM((SSD_GROUPS, gw, SSD_N), F32)],
        compiler_params=_cparams(("arbitrary", "arbitrary")),
        name="ssd_fwd",
    )(xbc, xbc, xbc, proj, dtb, alog, dsk)


def _ssd_bwd(xbc, proj, dtb, alog, dsk, h_saved, dy):
    L = xbc.shape[0]
    nc = L // CHUNK
    gw = SSD_HPG * SSD_P

    def body(xs_ref, b_ref, c_ref, dt_ref, dtb_ref, alog_ref, dsk_ref, hs_ref, dy_ref,
             dxs_ref, db_ref, dc_ref, ddt_ref, ddtb_ref, dalog_ref, ddsk_ref, dhst):
        g = pl.program_id(1)
        first = pl.program_id(0) == 0

        @pl.when(first)
        def _():
            dhst[g] = jnp.zeros((gw, SSD_N), F32)

        @pl.when(first & (g == 0))
        def _():
            ddtb_ref[...] = jnp.zeros((1, 128), F32)
            dalog_ref[...] = jnp.zeros((1, 128), F32)
            ddsk_ref[...] = jnp.zeros((1, 128), F32)

        _, vjp = jax.vjp(functools.partial(_ssd_chunk, g), xs_ref[...], b_ref[...], c_ref[...], dt_ref[...],
                         dtb_ref[...], alog_ref[...], dsk_ref[...], hs_ref[...])
        dxs, db, dc, ddt, ddtb, dalog, ddsk, dh = vjp((dy_ref[...], dhst[g]))
        dxs_ref[...] = dxs
        db_ref[...] = db
        dc_ref[...] = dc
        dhst[g] = dh
        ddtb_ref[...] += ddtb
        dalog_ref[...] += dalog
        ddsk_ref[...] += ddsk

        @pl.when(g == 0)
        def _():
            ddt_ref[...] = ddt

        @pl.when(g != 0)
        def _():
            ddt_ref[...] += ddt

    rc = lambda c: nc - 1 - c
    vec = pl.BlockSpec((1, 128), lambda c, g: (0, 0))
    vshape = jax.ShapeDtypeStruct((1, 128), F32)
    return pl.pallas_call(
        body,
        grid=(nc, SSD_GROUPS),
        in_specs=[
            pl.BlockSpec((CHUNK, gw), lambda c, g: (rc(c), g)),
            pl.BlockSpec((CHUNK, SSD_N), lambda c, g: (rc(c), D_INNER // SSD_N + g)),
            pl.BlockSpec((CHUNK, SSD_N), lambda c, g: (rc(c), D_INNER // SSD_N + SSD_GROUPS + g)),
            pl.BlockSpec((CHUNK, 128), lambda c, g: (rc(c), SSD_DT_COL // 128)),
            vec, vec, vec,
            pl.BlockSpec((None, None, gw, SSD_N), lambda c, g: (rc(c), g, 0, 0)),
            pl.BlockSpec((CHUNK, gw), lambda c, g: (rc(c), g)),
        ],
        out_specs=[
            pl.BlockSpec((CHUNK, gw), lambda c, g: (rc(c), g)),
            pl.BlockSpec((CHUNK, SSD_N), lambda c, g: (rc(c), g)),
            pl.BlockSpec((CHUNK, SSD_N), lambda c, g: (rc(c), g)),
            pl.BlockSpec((CHUNK, 128), lambda c, g: (rc(c), 0)),
            vec, vec, vec,
        ],
        out_shape=[
            jax.ShapeDtypeStruct((L, D_INNER), F32),
            jax.ShapeDtypeStruct((L, SSD_GROUPS * SSD_N), F32),
            jax.ShapeDtypeStruct((L, SSD_GROUPS * SSD_N), F32),
            jax.ShapeDtypeStruct((L, 128), F32),
            vshape, vshape, vshape,
        ],
        scratch_shapes=[pltpu.VMEM((SSD_GROUPS, gw, SSD_N), F32)],
        compiler_params=_cparams(("arbitrary", "arbitrary")),
        name="ssd_bwd",
    )(xbc, xbc, xbc, proj, dtb, alog, dsk, h_saved, dy)


def _pick(n, options):
    for t in options:
        if n % t == 0:
            return t
    return n


TOKEN_TILE = 512


def _mm(a, b, *, name, out_dtype=F32, add=None):
    M, K = a.shape
    N = b.shape[1]
    tm = min(TOKEN_TILE, M)
    tn = _pick(N, (1024, 896, 512, 256, 128))
    tk = _pick(K, (1024, 896, 512, 256, 128))
    nk = K // tk

    def body(*refs):
        if add is None:
            a_ref, b_ref, o_ref, acc = refs
        else:
            a_ref, b_ref, add_ref, o_ref, acc = refs
        k = pl.program_id(2)
        p = _dot(_bf(a_ref[...]), _bf(b_ref[...]))

        @pl.when(k == 0)
        def _():
            acc[...] = p

        @pl.when(k > 0)
        def _():
            acc[...] += p

        @pl.when(k == nk - 1)
        def _():
            r = acc[...]
            if add is not None:
                r = r + add_ref[...]
            o_ref[...] = r.astype(out_dtype)

    in_specs = [pl.BlockSpec((tm, tk), lambda i, j, k: (i, k)), pl.BlockSpec((tk, tn), lambda i, j, k: (k, j))]
    args = [a, b]
    if add is not None:
        in_specs.append(pl.BlockSpec((tm, tn), lambda i, j, k: (i, j)))
        args.append(add)
    return pl.pallas_call(
        body,
        grid=(M // tm, N // tn, nk),
        in_specs=in_specs,
        out_specs=pl.BlockSpec((tm, tn), lambda i, j, k: (i, j)),
        out_shape=jax.ShapeDtypeStruct((M, N), out_dtype),
        scratch_shapes=[pltpu.VMEM((tm, tn), F32)],
        compiler_params=_cparams(("parallel", "parallel", "arbitrary")),
        name=name,
    )(*args)


def _mm_tn(a, b, *, name):
    M, K = a.shape
    N = b.shape[1]
    tm = min(TOKEN_TILE, M)
    tn = _pick(N, (896, 512, 256, 128))

    def body(a_ref, b_ref, o_ref):
        p = _dot_tn(_bf(a_ref[...]), _bf(b_ref[...]))

        @pl.when(pl.program_id(1) == 0)
        def _():
            o_ref[...] = p

        @pl.when(pl.program_id(1) > 0)
        def _():
            o_ref[...] += p

    return pl.pallas_call(
        body,
        grid=(N // tn, M // tm),
        in_specs=[pl.BlockSpec((tm, K), lambda j, i: (i, 0)), pl.BlockSpec((tm, tn), lambda j, i: (i, j))],
        out_specs=pl.BlockSpec((K, tn), lambda j, i: (0, j)),
        out_shape=jax.ShapeDtypeStruct((K, N), F32),
        compiler_params=_cparams(("parallel", "arbitrary")),
        name=name,
    )(a, b)


def _rms(x, w):
    return x * lax.rsqrt(jnp.mean(x * x, axis=-1, keepdims=True) + RMS_EPS) * w


def _acc_out(ref, val, first):
    @pl.when(first)
    def _():
        ref[...] = val

    @pl.when(jnp.logical_not(first))
    def _():
        ref[...] += val


def _rms_fwd(x, w, *, name):
    L, D = x.shape
    tm = min(TOKEN_TILE, L)

    def body(x_ref, w_ref, o_ref):
        o_ref[...] = _rms(x_ref[...], w_ref[...]).astype(BF16)

    return pl.pallas_call(
        body, grid=(L // tm,),
        in_specs=[pl.BlockSpec((tm, D), lambda i: (i, 0)), pl.BlockSpec((1, D), lambda i: (0, 0))],
        out_specs=pl.BlockSpec((tm, D), lambda i: (i, 0)),
        out_shape=jax.ShapeDtypeStruct((L, D), BF16),
        compiler_params=_cparams(("parallel",)), name=name,
    )(x, w)


def _rms_bwd(x, w, dhn, dres, *, name):
    L, D = x.shape
    tm = min(TOKEN_TILE, L)

    def body(x_ref, w_ref, dhn_ref, dres_ref, dx_ref, dw_ref):
        _, vjp = jax.vjp(_rms, x_ref[...], w_ref[...])
        dx, dw = vjp(dhn_ref[...])
        dx_ref[...] = dx + dres_ref[...]
        _acc_out(dw_ref, dw, pl.program_id(0) == 0)

    row = pl.BlockSpec((tm, D), lambda i: (i, 0))
    vec = pl.BlockSpec((1, D), lambda i: (0, 0))
    return pl.pallas_call(
        body, grid=(L // tm,),
        in_specs=[row, vec, row, row],
        out_specs=[row, vec],
        out_shape=[jax.ShapeDtypeStruct((L, D), F32), jax.ShapeDtypeStruct((1, D), F32)],
        compiler_params=_cparams(("arbitrary",)), name=name,
    )(x, w, dhn, dres)


def _gla_post(o, g, wn):
    return _rms(o, wn) * _silu(g)


def _gla_post_fwd(o, proj, wn):
    L = o.shape[0]
    tm = min(TOKEN_TILE, L)
    gcol = (2 * GLA_DK + D_INNER) // GLA_HEAD_V

    def body(o_ref, g_ref, w_ref, y_ref):
        y_ref[...] = _gla_post(o_ref[...], g_ref[...], w_ref[...]).astype(BF16)

    blk = pl.BlockSpec((tm, GLA_HEAD_V), lambda i, h: (i, h))
    return pl.pallas_call(
        body, grid=(L // tm, GLA_HEADS),
        in_specs=[blk, pl.BlockSpec((tm, GLA_HEAD_V), lambda i, h: (i, gcol + h)), pl.BlockSpec((1, GLA_HEAD_V), lambda i, h: (0, 0))],
        out_specs=blk,
        out_shape=jax.ShapeDtypeStruct((L, D_INNER), BF16),
        compiler_params=_cparams(("parallel", "parallel")), name="gla_post_fwd",
    )(o, proj, wn)


def _gla_post_bwd(o, proj, wn, dy):
    L = o.shape[0]
    tm = min(TOKEN_TILE, L)
    gcol = (2 * GLA_DK + D_INNER) // GLA_HEAD_V

    def body(o_ref, g_ref, w_ref, dy_ref, do_ref, dg_ref, dw_ref):
        _, vjp = jax.vjp(_gla_post, o_ref[...], g_ref[...], w_ref[...])
        do, dg, dw = vjp(dy_ref[...])
        do_ref[...] = do
        dg_ref[...] = dg
        _acc_out(dw_ref, dw, (pl.program_id(0) == 0) & (pl.program_id(1) == 0))

    blk = pl.BlockSpec((tm, GLA_HEAD_V), lambda i, h: (i, h))
    vec = pl.BlockSpec((1, GLA_HEAD_V), lambda i, h: (0, 0))
    return pl.pallas_call(
        body, grid=(L // tm, GLA_HEADS),
        in_specs=[blk, pl.BlockSpec((tm, GLA_HEAD_V), lambda i, h: (i, gcol + h)), vec, blk],
        out_specs=[blk, blk, vec],
        out_shape=[jax.ShapeDtypeStruct((L, D_INNER), F32), jax.ShapeDtypeStruct((L, D_INNER), F32),
                   jax.ShapeDtypeStruct((1, GLA_HEAD_V), F32)],
        compiler_params=_cparams(("arbitrary", "arbitrary")), name="gla_post_bwd",
    )(o, proj, wn, dy)


def _ssd_post(y, z, wn):
    return _rms(y * _silu(z), wn)


def _ssd_post_fwd(y, proj, wn):
    L = y.shape[0]
    tm = min(TOKEN_TILE // 2, L)

    def body(y_ref, z_ref, w_ref, o_ref):
        o_ref[...] = _ssd_post(y_ref[...], z_ref[...], w_ref[...]).astype(BF16)

    blk = pl.BlockSpec((tm, D_INNER), lambda i: (i, 0))
    return pl.pallas_call(
        body, grid=(L // tm,),
        in_specs=[blk, blk, pl.BlockSpec((1, D_INNER), lambda i: (0, 0))],
        out_specs=blk,
        out_shape=jax.ShapeDtypeStruct((L, D_INNER), BF16),
        compiler_params=_cparams(("parallel",)), name="ssd_post_fwd",
    )(y, proj, wn)


def _ssd_post_bwd(y, proj, wn, dyn):
    L = y.shape[0]
    tm = min(TOKEN_TILE // 2, L)

    def body(y_ref, z_ref, w_ref, dyn_ref, dy_ref, dz_ref, dw_ref):
        _, vjp = jax.vjp(_ssd_post, y_ref[...], z_ref[...], w_ref[...])
        dy, dz, dw = vjp(dyn_ref[...])
        dy_ref[...] = dy
        dz_ref[...] = dz
        _acc_out(dw_ref, dw, pl.program_id(0) == 0)

    blk = pl.BlockSpec((tm, D_INNER), lambda i: (i, 0))
    vec = pl.BlockSpec((1, D_INNER), lambda i: (0, 0))
    return pl.pallas_call(
        body, grid=(L // tm,),
        in_specs=[blk, blk, vec, blk],
        out_specs=[blk, blk, vec],
        out_shape=[jax.ShapeDtypeStruct((L, D_INNER), F32), jax.ShapeDtypeStruct((L, D_INNER), F32),
                   jax.ShapeDtypeStruct((1, D_INNER), F32)],
        compiler_params=_cparams(("arbitrary",)), name="ssd_post_bwd",
    )(y, proj, wn, dyn)


CONV_HALO = 8
CONV_COLS = 1024


def _conv_tile(xin, halo, w, b):
    tm = xin.shape[0]
    xx = jnp.concatenate([halo, xin], axis=0)
    u = b
    for k in range(SSD_CONV):
        off = CONV_HALO - (SSD_CONV - 1) + k
        u = u + w[k:k + 1] * xx[off:off + tm]
    return _silu(u)


def _conv_fwd(proj, w, b):
    L = proj.shape[0]
    tm = min(TOKEN_TILE, L)
    c0 = D_INNER // CONV_COLS
    hb = tm // CONV_HALO

    def body(x_ref, h_ref, w_ref, b_ref, o_ref):
        halo = jnp.where(pl.program_id(1) == 0, 0.0, h_ref[...])
        o_ref[...] = _conv_tile(x_ref[...], halo, w_ref[...], b_ref[...])

    return pl.pallas_call(
        body, grid=(SSD_CONV_DIM // CONV_COLS, L // tm),
        in_specs=[
            pl.BlockSpec((tm, CONV_COLS), lambda j, i: (i, c0 + j)),
            pl.BlockSpec((CONV_HALO, CONV_COLS), lambda j, i: (jnp.maximum(i * hb - 1, 0), c0 + j)),
            pl.BlockSpec((SSD_CONV, CONV_COLS), lambda j, i: (0, j)),
            pl.BlockSpec((1, CONV_COLS), lambda j, i: (0, j)),
        ],
        out_specs=pl.BlockSpec((tm, CONV_COLS), lambda j, i: (i, j)),
        out_shape=jax.ShapeDtypeStruct((L, SSD_CONV_DIM), F32),
        compiler_params=_cparams(("parallel", "parallel")), name="conv_fwd",
    )(proj, proj, w, b)


def _conv_bwd(proj, w, b, dxs, dbm, dcm):
    L = proj.shape[0]
    tm = min(TOKEN_TILE, L)
    nt = L // tm
    c0 = D_INNER // CONV_COLS
    hb = tm // CONV_HALO
    nxs = D_INNER // CONV_COLS
    nb = SSD_GROUPS * SSD_N // CONV_COLS

    def body(x_ref, h_ref, w_ref, b_ref, dxs_ref, dbm_ref, dcm_ref, dx_ref, dw_ref, db_ref, carry):
        j = pl.program_id(0)
        i = pl.program_id(1)
        first = i == 0

        @pl.when(first)
        def _():
            carry[...] = jnp.zeros((CONV_HALO, CONV_COLS), F32)

        dy = jnp.where(j < nxs, dxs_ref[...], jnp.where(j < nxs + nb, dbm_ref[...], dcm_ref[...]))
        halo = jnp.where(i == nt - 1, 0.0, h_ref[...])
        _, vjp = jax.vjp(_conv_tile, x_ref[...], halo, w_ref[...], b_ref[...])
        dx, dhalo, dw, db = vjp(dy)
        dx_ref[...] = jnp.concatenate([dx[:tm - CONV_HALO], dx[tm - CONV_HALO:] + carry[...]], axis=0)
        carry[...] = dhalo
        _acc_out(dw_ref, dw, first)
        _acc_out(db_ref, db, first)

    rt = lambda i: nt - 1 - i
    return pl.pallas_call(
        body, grid=(SSD_CONV_DIM // CONV_COLS, nt),
        in_specs=[
            pl.BlockSpec((tm, CONV_COLS), lambda j, i: (rt(i), c0 + j)),
            pl.BlockSpec((CONV_HALO, CONV_COLS), lambda j, i: (jnp.maximum(rt(i) * hb - 1, 0), c0 + j)),
            pl.BlockSpec((SSD_CONV, CONV_COLS), lambda j, i: (0, j)),
            pl.BlockSpec((1, CONV_COLS), lambda j, i: (0, j)),
            pl.BlockSpec((tm, CONV_COLS), lambda j, i: (rt(i), jnp.minimum(j, nxs - 1))),
            pl.BlockSpec((tm, CONV_COLS), lambda j, i: (rt(i), jnp.clip(j - nxs, 0, nb - 1))),
            pl.BlockSpec((tm, CONV_COLS), lambda j, i: (rt(i), jnp.clip(j - nxs - nb, 0, nb - 1))),
        ],
        out_specs=[
            pl.BlockSpec((tm, CONV_COLS), lambda j, i: (rt(i), j)),
            pl.BlockSpec((SSD_CONV, CONV_COLS), lambda j, i: (0, j)),
            pl.BlockSpec((1, CONV_COLS), lambda j, i: (0, j)),
        ],
        out_shape=[jax.ShapeDtypeStruct((L, SSD_CONV_DIM), F32), jax.ShapeDtypeStruct((SSD_CONV, SSD_CONV_DIM), F32),
                   jax.ShapeDtypeStruct((1, SSD_CONV_DIM), F32)],
        scratch_shapes=[pltpu.VMEM((CONV_HALO, CONV_COLS), F32)],
        compiler_params=_cparams(("arbitrary", "arbitrary")), name="conv_bwd",
    )(proj, proj, w, b, dxs, dbm, dcm)


def _loss_bwd(x, tgt, w):
    L, D = x.shape
    tm = min(TOKEN_TILE, L)

    def body(x_ref, t_ref, w_ref, l_ref, dx_ref, dw_ref):
        xv = x_ref[...]
        wv = w_ref[...]
        r = lax.rsqrt(jnp.mean(xv * xv, axis=-1, keepdims=True) + RMS_EPS)
        xh = xv * r
        e = xh * wv - t_ref[...]
        lsum = 0.5 * jnp.sum(jnp.mean(e * e, axis=-1, keepdims=True), axis=0, keepdims=True)
        dout = e * (1.0 / D)
        gx = dout * wv
        dx_ref[...] = r * (gx - xh * jnp.mean(gx * xh, axis=-1, keepdims=True))
        first = pl.program_id(0) == 0
        _acc_out(dw_ref, jnp.sum(dout * xh, axis=0, keepdims=True), first)
        _acc_out(l_ref, jnp.broadcast_to(lsum, (8, 128)), first)

    row = pl.BlockSpec((tm, D), lambda i: (i, 0))
    vec = pl.BlockSpec((1, D), lambda i: (0, 0))
    return pl.pallas_call(
        body, grid=(L // tm,),
        in_specs=[row, row, vec],
        out_specs=[pl.BlockSpec((8, 128), lambda i: (0, 0)), row, vec],
        out_shape=[jax.ShapeDtypeStruct((8, 128), F32), jax.ShapeDtypeStruct((L, D), F32), jax.ShapeDtypeStruct((1, D), F32)],
        compiler_params=_cparams(("arbitrary",)), name="loss_bwd",
    )(x, tgt, w)


MESH = pl.DeviceIdType.MESH
ANY = pl.BlockSpec(memory_space=pl.ANY)


def _all_gather(xs, *, name):
    R, C = xs.shape

    def body(x_ref, out_ref, send_sems, recv_sems, local_sem):
        x, y, c = lax.axis_index("x"), lax.axis_index("y"), lax.axis_index("c")
        me, sibling = (x, y, c), (x, y, 1 - c)
        chips = [(1 - x, y), (x, 1 - y), (1 - x, 1 - y)]

        def slot(px, py, pc):
            return out_ref.at[4 * px + 2 * py + pc]

        def copy(k, block, to, src=None):
            return pltpu.make_async_remote_copy(
                src_ref=slot(*block) if src is None else src, dst_ref=slot(*block),
                send_sem=send_sems.at[k], recv_sem=recv_sems.at[k], device_id=to, device_id_type=MESH)

        mine = pltpu.make_async_copy(x_ref, slot(*me), local_sem)
        mine.start()
        first = [copy(0, me, sibling, src=x_ref)]
        first += [copy(1 + j, me, (*chip, c), src=x_ref) for j, chip in enumerate(chips)]
        for cp in first:
            cp.start()
        passed = [copy(4 + j, (*chip, c), sibling) for j, chip in enumerate(chips)]
        for j, chip in enumerate(chips):
            copy(1 + j, (*chip, c), me).wait_recv()
            passed[j].start()
        copy(0, sibling, me).wait_recv()
        for j, chip in enumerate(chips):
            copy(4 + j, (*chip, 1 - c), me).wait_recv()
        for cp in first + passed:
            cp.wait_send()
        mine.wait()

    return pl.pallas_call(
        body,
        out_shape=jax.ShapeDtypeStruct((N_DEV, R, C), xs.dtype),
        in_specs=[ANY], out_specs=ANY,
        scratch_shapes=[pltpu.SemaphoreType.DMA((7,)), pltpu.SemaphoreType.DMA((7,)), pltpu.SemaphoreType.DMA],
        name=name,
    )(xs)


def _exchange(parts, *, name):
    _, R, C = parts.shape

    def body(p_ref, out_ref, send_sems, recv_sems, local_sem):
        x, y, c = lax.axis_index("x"), lax.axis_index("y"), lax.axis_index("c")
        my = 4 * x + 2 * y + c
        mine = pltpu.make_async_copy(p_ref.at[my], out_ref.at[my], local_sem)
        mine.start()
        copies = []
        for k in range(1, N_DEV):
            fx, fy, fc = (k >> 2) & 1, (k >> 1) & 1, k & 1
            px, py, pc = (1 - x if fx else x), (1 - y if fy else y), (1 - c if fc else c)
            pid = 4 * px + 2 * py + pc
            copies.append(pltpu.make_async_remote_copy(
                src_ref=p_ref.at[pid], dst_ref=out_ref.at[my], send_sem=send_sems.at[k - 1], recv_sem=recv_sems.at[k - 1],
                device_id=(px, py, pc), device_id_type=MESH))
            copies[-1].start()
        for k in range(1, N_DEV):
            fx, fy, fc = (k >> 2) & 1, (k >> 1) & 1, k & 1
            px, py, pc = (1 - x if fx else x), (1 - y if fy else y), (1 - c if fc else c)
            pid = 4 * px + 2 * py + pc
            pltpu.make_async_remote_copy(
                src_ref=p_ref.at[pid], dst_ref=out_ref.at[pid], send_sem=send_sems.at[k - 1], recv_sem=recv_sems.at[k - 1],
                device_id=(px, py, pc), device_id_type=MESH).wait()
        mine.wait()

    return pl.pallas_call(
        body,
        out_shape=jax.ShapeDtypeStruct((N_DEV, R, C), parts.dtype),
        in_specs=[ANY], out_specs=ANY,
        scratch_shapes=[pltpu.SemaphoreType.DMA((7,)), pltpu.SemaphoreType.DMA((7,)), pltpu.SemaphoreType.DMA],
        name=name,
    )(parts)


def _adamw(parts, w, m, v, *, name):
    R = w.shape[0]
    tr = _pick(R, (512, 256, 128, 64, 40, 32, 16, 8))

    def body(p_ref, w_ref, m_ref, v_ref, g_ref, d_ref, mo_ref, vo_ref):
        g = p_ref[0]
        for s in range(1, N_DEV):
            g = g + p_ref[s]
        mn = ADAM_B1 * m_ref[...] + (1.0 - ADAM_B1) * g
        vn = ADAM_B2 * v_ref[...] + (1.0 - ADAM_B2) * jnp.square(g)
        m_hat = mn / (1.0 - ADAM_B1 ** ADAM_STEP)
        v_hat = vn / (1.0 - ADAM_B2 ** ADAM_STEP)
        g_ref[...] = g
        d_ref[...] = -ADAM_LR * (m_hat / (jnp.sqrt(v_hat) + ADAM_EPS) + ADAM_WD * w_ref[...])
        mo_ref[...] = mn
        vo_ref[...] = vn

    blk = pl.BlockSpec((tr, 128), lambda i: (i, 0))
    shp = jax.ShapeDtypeStruct((R, 128), F32)
    return pl.pallas_call(
        body, grid=(R // tr,),
        in_specs=[pl.BlockSpec((N_DEV, tr, 128), lambda i: (0, i, 0)), blk, blk, blk],
        out_specs=[blk, blk, blk, blk],
        out_shape=[shp, shp, shp, shp],
        compiler_params=_cparams(("parallel",)), name=name,
    )(parts, w, m, v)


def _rows(a):
    return a.reshape(-1, 128)


def _pad_rows(a, rows):
    return jnp.pad(a, ((0, rows - a.shape[0]), (0, 0)))


def _pad_cols(a, cols):
    return jnp.pad(a, ((0, 0), (0, cols - a.shape[1])))


def _col_shards(a, n):
    r = a.shape[0]
    return a.reshape(r, N_DEV, n).transpose(1, 0, 2).reshape(N_DEV, -1, 128)


def _from_col_shards(g, r, n):
    return g.reshape(N_DEV, r, n).transpose(1, 0, 2).reshape(r, N_DEV * n)


def kernel(x, norm_w, gla_in_proj, gla_gate_up, gla_gate_bias, gla_head_norm, gla_out_proj, ssd_in_proj, ssd_conv_w, ssd_conv_b, ssd_dt_bias, ssd_a_log, ssd_d, ssd_gate_norm, ssd_out_proj, final_norm, loss_target, m_norm_w, m_gla_in_proj, m_gla_gate_up, m_gla_gate_bias, m_gla_head_norm, m_gla_out_proj, m_ssd_in_proj, m_ssd_conv_w, m_ssd_conv_b, m_ssd_dt_bias, m_ssd_a_log, m_ssd_d, m_ssd_gate_norm, m_ssd_out_proj, m_final_norm, v_norm_w, v_gla_in_proj, v_gla_gate_up, v_gla_gate_bias, v_gla_head_norm, v_gla_out_proj, v_ssd_in_proj, v_ssd_conv_w, v_ssd_conv_b, v_ssd_dt_bias, v_ssd_a_log, v_ssd_d, v_ssd_gate_norm, v_ssd_out_proj, v_final_norm):
    x0 = x[0]
    tgt = loss_target[0]
    n_gin = GLA_PROJ // N_DEV
    n_sin = SSD_PROJ // N_DEV
    n_up = GLA_DK // N_DEV
    n_cv = SSD_CONV_DIM // N_DEV

    big = [_rows(gla_in_proj[0]), _pad_rows(_rows(gla_gate_up[0]), 16), _rows(gla_out_proj[0]), _rows(ssd_in_proj[0]),
           _rows(ssd_out_proj[0])]
    big_rows = [a.shape[0] for a in big]
    wg = _all_gather(jnp.concatenate(big, axis=0).astype(BF16), name="gather_weights")
    small = [_rows(ssd_conv_w[0]), _rows(ssd_conv_b[0]), _rows(ssd_gate_norm[0])]
    small_rows = [a.shape[0] for a in small]
    ws = _all_gather(_pad_rows(jnp.concatenate(small, axis=0), 24), name="gather_small")

    def seg(g, rows, i):
        o = sum(rows[:i])
        return g[:, o:o + rows[i]]

    w_gin = _pad_cols(_from_col_shards(seg(wg, big_rows, 0), D_MODEL, n_gin), GLA_PROJ_PAD)
    wup = _pad_rows(_from_col_shards(seg(wg, big_rows, 1)[:, :GLA_RANK * n_up // 128], GLA_RANK, n_up), 128).astype(F32)
    w_gout = seg(wg, big_rows, 2).reshape(D_INNER, D_MODEL)
    w_sin = _pad_cols(_from_col_shards(seg(wg, big_rows, 3), D_MODEL, n_sin), SSD_PROJ_PAD)
    w_sout = seg(wg, big_rows, 4).reshape(D_INNER, D_MODEL)
    conv_w = _from_col_shards(seg(ws, small_rows, 0), SSD_CONV, n_cv)
    conv_b = seg(ws, small_rows, 1).reshape(1, SSD_CONV_DIM)
    gate_norm = seg(ws, small_rows, 2).reshape(1, D_INNER)
    vec128 = lambda a: _pad_cols(a.reshape(1, -1), 128)
    dtb, alog, dsk = vec128(ssd_dt_bias), vec128(ssd_a_log), vec128(ssd_d)
    nw0, nw1 = norm_w[0:1], norm_w[1:2]

    hn1 = _rms_fwd(x0, nw0, name="rms1_fwd")
    proj1 = _mm(hn1, w_gin, name="gla_in_proj")
    o, s_saved = _gla_fwd(proj1, wup, gla_gate_bias)
    og = _gla_post_fwd(o, proj1, gla_head_norm)
    x1 = _mm(og, w_gout, add=x0, name="gla_out_proj")
    hn2 = _rms_fwd(x1, nw1, name="rms2_fwd")
    proj2 = _mm(hn2, w_sin, name="ssd_in_proj")
    xbc = _conv_fwd(proj2, conv_w, conv_b)
    y, h_saved = _ssd_fwd(xbc, proj2, dtb, alog, dsk)
    yn = _ssd_post_fwd(y, proj2, gate_norm)
    x2 = _mm(yn, w_sout, add=x1, name="ssd_out_proj")
    lsum, dx2, d_final = _loss_bwd(x2, tgt, final_norm.reshape(1, D_MODEL))
    loss = lax.psum(lsum[0, 0], ("x", "y", "c"))

    d_sout = _mm_tn(yn, dx2, name="ssd_out_proj_dw")
    dyn = _mm(dx2, w_sout.T, name="ssd_out_proj_dx")
    dy, dz, d_gate_norm = _ssd_post_bwd(y, proj2, gate_norm, dyn)
    dxs, dbm, dcm, ddt, d_dtb, d_alog, d_dsk = _ssd_bwd(xbc, proj2, dtb, alog, dsk, h_saved, dy)
    dxbc, d_conv_w, d_conv_b = _conv_bwd(proj2, conv_w, conv_b, dxs, dbm, dcm)
    dproj2 = jnp.concatenate([dz, dxbc, ddt], axis=1).astype(BF16)
    d_sin = _mm_tn(hn2, dproj2, name="ssd_in_proj_dw")
    dhn2 = _mm(dproj2, w_sin.T, name="ssd_in_proj_dx")
    dx1, d_nw1 = _rms_bwd(x1, nw1, dhn2, dx2, name="rms2_bwd")
    d_gout = _mm_tn(og, dx1, name="gla_out_proj_dw")
    dog = _mm(dx1, w_gout.T, name="gla_out_proj_dx")
    do, dg, d_head_norm = _gla_post_bwd(o, proj1, gla_head_norm, dog)
    dq, dk, dv, dgk, d_wup, d_gbias = _gla_bwd(proj1, wup, gla_gate_bias, s_saved, do)
    dproj1 = jnp.concatenate([dq, dk, dv, dg, dgk, jnp.zeros_like(dgk)], axis=1).astype(BF16)
    d_gin = _mm_tn(hn1, dproj1, name="gla_in_proj_dw")
    dhn1 = _mm(dproj1, w_gin.T, name="gla_in_proj_dx")
    dx0, d_nw0 = _rms_bwd(x0, nw0, dhn1, dx1, name="rms1_bwd")

    d_wup_full = jnp.concatenate([d_wup[h] for h in range(GLA_HEADS)], axis=1)[:GLA_RANK]
    sharded = [
        ("gla_in_proj", _col_shards(d_gin[:, :GLA_PROJ], n_gin)),
        ("gla_gate_up", _col_shards(d_wup_full, n_up)),
        ("gla_out_proj", d_gout.reshape(N_DEV, -1, 128)),
        ("ssd_in_proj", _col_shards(d_sin[:, :SSD_PROJ], n_sin)),
        ("ssd_conv_w", _col_shards(d_conv_w, n_cv)),
        ("ssd_conv_b", d_conv_b.reshape(N_DEV, -1, 128)),
        ("ssd_gate_norm", d_gate_norm.reshape(N_DEV, -1, 128)),
        ("ssd_out_proj", d_sout.reshape(N_DEV, -1, 128)),
    ]
    given = dict(
        gla_in_proj=(gla_in_proj, m_gla_in_proj, v_gla_in_proj), gla_gate_up=(gla_gate_up, m_gla_gate_up, v_gla_gate_up),
        gla_out_proj=(gla_out_proj, m_gla_out_proj, v_gla_out_proj), ssd_in_proj=(ssd_in_proj, m_ssd_in_proj, v_ssd_in_proj),
        ssd_conv_w=(ssd_conv_w, m_ssd_conv_w, v_ssd_conv_w), ssd_conv_b=(ssd_conv_b, m_ssd_conv_b, v_ssd_conv_b),
        ssd_gate_norm=(ssd_gate_norm, m_ssd_gate_norm, v_ssd_gate_norm), ssd_out_proj=(ssd_out_proj, m_ssd_out_proj, v_ssd_out_proj),
        norm_w=(norm_w, m_norm_w, v_norm_w), gla_gate_bias=(gla_gate_bias, m_gla_gate_bias, v_gla_gate_bias),
        gla_head_norm=(gla_head_norm, m_gla_head_norm, v_gla_head_norm), ssd_dt_bias=(ssd_dt_bias, m_ssd_dt_bias, v_ssd_dt_bias),
        ssd_a_log=(ssd_a_log, m_ssd_a_log, v_ssd_a_log), ssd_d=(ssd_d, m_ssd_d, v_ssd_d),
        final_norm=(final_norm, m_final_norm, v_final_norm),
    )
    results = {}

    def update(group, parts_rows, gather, name):
        rows = [p.shape[1] for _, p in group]
        total = -(-sum(rows) // parts_rows) * parts_rows
        parts = jnp.concatenate([p for _, p in group], axis=1)
        parts = jnp.pad(parts, ((0, 0), (0, total - parts.shape[1]), (0, 0)))
        if gather:
            recv = _all_gather(parts[0], name=name + "_gather")
        else:
            recv = _exchange(parts, name=name + "_exchange")

        def flat(i):
            rs = []
            for (nm, _), r in zip(group, rows):
                a = given[nm][i]
                a = a.reshape(1, -1)
                a = _pad_cols(a, r * 128).reshape(r, 128)
                rs.append(a)
            return _pad_rows(jnp.concatenate(rs, axis=0), total)

        outs = _adamw(recv, flat(0), flat(1), flat(2), name=name + "_adamw")
        off = 0
        for (nm, _), r in zip(group, rows):
            shape = given[nm][0].shape
            n = 1
            for d in shape:
                n *= d
            results[nm] = tuple(t[off:off + r].reshape(-1)[:n].reshape(shape) for t in outs)
            off += r

    update(sharded, 512, False, "sharded")
    replicated = [
        ("norm_w", jnp.concatenate([d_nw0, d_nw1], axis=0).reshape(1, -1, 128)),
        ("gla_gate_bias", d_gbias.reshape(1, -1, 128)),
        ("gla_head_norm", d_head_norm.reshape(1, -1, 128)),
        ("ssd_dt_bias", d_dtb.reshape(1, 1, 128)),
        ("ssd_a_log", d_alog.reshape(1, 1, 128)),
        ("ssd_d", d_dsk.reshape(1, 1, 128)),
        ("final_norm", d_final.reshape(1, -1, 128)),
    ]
    update(replicated, 8, True, "replicated")

    order = ["norm_w", "gla_in_proj", "gla_gate_up", "gla_gate_bias", "gla_head_norm", "gla_out_proj", "ssd_in_proj",
             "ssd_conv_w", "ssd_conv_b", "ssd_dt_bias", "ssd_a_log", "ssd_d", "ssd_gate_norm", "ssd_out_proj", "final_norm"]
    out = [loss, dx0[None]]
    for i in range(4):
        out += [results[n][i] for n in order]
    return tuple(out)
```

```python
import functools

import jax
import jax.numpy as jnp
from jax import lax
from jax.experimental import pallas as pl
from jax.experimental.pallas import tpu as pltpu

F32 = jnp.float32
BF16 = jnp.bfloat16
HI = lax.Precision.HIGHEST

D_MODEL = 1024
D_INNER = 2048
RMS_EPS = 1e-6
GLA_HEADS = 4
GLA_DK = 512
GLA_HEAD_K = 128
GLA_HEAD_V = 512
GLA_RANK = 16
GLA_NORMALIZER = 16.0
CHUNK = 64
SUB = 16
GLA_PROJ = 5136
GLA_PROJ_PAD = 5376
GLA_GK_COL = 5120
SSD_HEADS = 32
SSD_GROUPS = 8
SSD_HPG = 4
SSD_P = 64
SSD_N = 128
SSD_CONV = 4
SSD_CONV_DIM = 4096
SSD_PROJ = 6176
SSD_PROJ_PAD = 6272
SSD_DT_COL = 6144
N_DEV = 8

ADAM_LR = 0.001
ADAM_B1 = 0.9
ADAM_B2 = 0.999
ADAM_EPS = 1e-08
ADAM_WD = 0.01
ADAM_STEP = 10

VMEM_LIMIT = 56 * 1024 * 1024


def _dot(a, b, prec=None):
    return jnp.dot(a, b, preferred_element_type=F32, precision=prec)


def _dot_nt(a, b, prec=None):
    return lax.dot_general(a, b, (((1,), (1,)), ((), ())), preferred_element_type=F32, precision=prec)


def _dot_tn(a, b, prec=None):
    return lax.dot_general(a, b, (((0,), (0,)), ((), ())), preferred_element_type=F32, precision=prec)


def _bf(a):
    return a.astype(BF16)


@jax.custom_vjp
def _mxu(a, b):
    return _dot(_bf(a), _bf(b))


def _mxu_fwd(a, b):
    return _mxu(a, b), (a, b)


def _mxu_bwd(res, g):
    a, b = res
    return _dot_nt(_bf(g), _bf(b)), _dot_tn(_bf(a), _bf(g))


_mxu.defvjp(_mxu_fwd, _mxu_bwd)


@jax.custom_vjp
def _mxu_nt(a, b):
    return _dot_nt(_bf(a), _bf(b))


def _mxu_nt_fwd(a, b):
    return _mxu_nt(a, b), (a, b)


def _mxu_nt_bwd(res, g):
    a, b = res
    return _dot(_bf(g), _bf(b)), _dot_tn(_bf(g), _bf(a))


_mxu_nt.defvjp(_mxu_nt_fwd, _mxu_nt_bwd)


@jax.custom_vjp
def _mxu_tn(a, b):
    return _dot_tn(_bf(a), _bf(b))


def _mxu_tn_fwd(a, b):
    return _mxu_tn(a, b), (a, b)


def _mxu_tn_bwd(res, g):
    a, b = res
    return _dot_nt(_bf(b), _bf(g)), _dot(_bf(a), _bf(g))


_mxu_tn.defvjp(_mxu_tn_fwd, _mxu_tn_bwd)


def _silu(x):
    return x / (1.0 + jnp.exp(-x))


def _log_sigmoid(z):
    return jnp.minimum(z, 0.0) - jnp.log(1.0 + jnp.exp(-jnp.abs(z)))


def _softplus(z):
    return jnp.maximum(z, 0.0) + jnp.log(1.0 + jnp.exp(-jnp.abs(z)))


def _iota(shape, dim):
    return lax.broadcasted_iota(jnp.int32, shape, dim)


def _gla_chunk(q, k, v, gk, wup, bias, St):
    nb = CHUNK // SUB
    z = _mxu(gk, wup) + bias
    la = _log_sigmoid(z) * (1.0 / GLA_NORMALIZER)
    qs = q * (GLA_HEAD_K ** -0.5)
    r = _iota((CHUNK, CHUNK), 0)
    c = _iota((CHUNK, CHUNK), 1)
    blk_tri = jnp.where((r // SUB == c // SUB) & (c <= r), 1.0, 0.0).astype(F32)
    blk_one = jnp.where(r // SUB == c // SUB, 1.0, 0.0).astype(F32)
    bl = _dot(blk_tri, la, HI)
    suf = _dot(blk_one, la, HI) - bl
    pre = [jnp.zeros((1, GLA_HEAD_K), F32)]
    for i in range(1, nb + 1):
        pre.append(jnp.sum(la[: i * SUB], axis=0, keepdims=True))
    b_last = pre[nb]
    rs = _iota((SUB, GLA_HEAD_K), 0)
    cs = _iota((SUB, SUB), 1)
    outs = []
    kdec = []
    for i in range(nb):
        sl = slice(i * SUB, (i + 1) * SUB)
        q_i, k_i, bl_i = qs[sl], k[sl], bl[sl]
        b_i = bl_i + pre[i]
        o_i = _mxu_nt(q_i * jnp.exp(b_i), St)
        qa = q_i * jnp.exp(bl_i)
        if i > 0:
            kp = jnp.concatenate(
                [k[j * SUB:(j + 1) * SUB] * jnp.exp(suf[j * SUB:(j + 1) * SUB] + (pre[i] - pre[j + 1])) for j in range(i)], axis=0)
            a_off = _dot_nt(qa, kp, HI)
            o_i = o_i + _mxu(a_off, v[: i * SUB])
        a_dg = jnp.zeros((SUB, SUB), F32)
        for j in range(SUB):
            e = jnp.exp(jnp.minimum(bl_i - bl_i[j:j + 1], 0.0))
            t = jnp.where(rs >= j, q_i * e * k_i[j:j + 1], 0.0)
            a_dg = a_dg + jnp.where(cs == j, jnp.sum(t, axis=-1, keepdims=True), 0.0)
        o_i = o_i + _mxu(a_dg, v[sl])
        outs.append(o_i)
        kdec.append(k_i * jnp.exp(suf[sl] + (b_last - pre[i + 1])))
    o = jnp.concatenate(outs, axis=0)
    St_new = St * jnp.exp(b_last) + _mxu_tn(v, jnp.concatenate(kdec, axis=0))
    return o, St_new


def _cparams(sem):
    return pltpu.CompilerParams(dimension_semantics=sem, vmem_limit_bytes=VMEM_LIMIT)


def _gla_fwd(proj, wup, bias):
    L = proj.shape[0]
    nc = L // CHUNK

    def body(q_ref, k_ref, v_ref, gk_ref, wup_ref, b_ref, o_ref, s_ref, st):
        h = pl.program_id(1)

        @pl.when(pl.program_id(0) == 0)
        def _():
            st[h] = jnp.zeros((GLA_HEAD_V, GLA_HEAD_K), F32)

        s_in = st[h]
        s_ref[...] = s_in
        o, s_new = _gla_chunk(q_ref[...], k_ref[...], v_ref[...], gk_ref[...], wup_ref[...], b_ref[...], s_in)
        o_ref[...] = o
        st[h] = s_new

    return pl.pallas_call(
        body,
        grid=(nc, GLA_HEADS),
        in_specs=[
            pl.BlockSpec((CHUNK, GLA_HEAD_K), lambda c, h: (c, h)),
            pl.BlockSpec((CHUNK, GLA_HEAD_K), lambda c, h: (c, GLA_HEADS + h)),
            pl.BlockSpec((CHUNK, GLA_HEAD_V), lambda c, h: (c, 2 + h)),
            pl.BlockSpec((CHUNK, 128), lambda c, h: (c, GLA_GK_COL // 128)),
            pl.BlockSpec((128, GLA_HEAD_K), lambda c, h: (0, h)),
            pl.BlockSpec((1, GLA_HEAD_K), lambda c, h: (0, h)),
        ],
        out_specs=[
            pl.BlockSpec((CHUNK, GLA_HEAD_V), lambda c, h: (c, h)),
            pl.BlockSpec((None, None, GLA_HEAD_V, GLA_HEAD_K), lambda c, h: (c, h, 0, 0)),
        ],
        out_shape=[
            jax.ShapeDtypeStruct((L, D_INNER), F32),
            jax.ShapeDtypeStruct((nc, GLA_HEADS, GLA_HEAD_V, GLA_HEAD_K), F32),
        ],
        scratch_shapes=[pltpu.VMEM((GLA_HEADS, GLA_HEAD_V, GLA_HEAD_K), F32)],
        compiler_params=_cparams(("arbitrary", "arbitrary")),
        name="gla_fwd",
    )(proj, proj, proj, proj, wup, bias)


def _gla_bwd(proj, wup, bias, s_in, do):
    L = proj.shape[0]
    nc = L // CHUNK

    def body(q_ref, k_ref, v_ref, gk_ref, wup_ref, b_ref, s_ref, do_ref,
             dq_ref, dk_ref, dv_ref, dgk_ref, dwup_ref, db_ref, dst):
        h = pl.program_id(1)
        first = pl.program_id(0) == 0

        @pl.when(first)
        def _():
            dst[h] = jnp.zeros((GLA_HEAD_V, GLA_HEAD_K), F32)
            dwup_ref[h] = jnp.zeros((128, GLA_HEAD_K), F32)
            db_ref[h] = jnp.zeros((1, GLA_HEAD_K), F32)

        _, vjp = jax.vjp(_gla_chunk, q_ref[...], k_ref[...], v_ref[...], gk_ref[...], wup_ref[...], b_ref[...], s_ref[...])
        dq, dk, dv, dgk, dwup, db, ds = vjp((do_ref[...], dst[h]))
        dq_ref[...] = dq
        dk_ref[...] = dk
        dv_ref[...] = dv
        dst[h] = ds
        dwup_ref[h] += dwup
        db_ref[h] += db

        @pl.when(h == 0)
        def _():
            dgk_ref[...] = dgk

        @pl.when(h != 0)
        def _():
            dgk_ref[...] += dgk

    rc = lambda c: nc - 1 - c
    return pl.pallas_call(
        body,
        grid=(nc, GLA_HEADS),
        in_specs=[
            pl.BlockSpec((CHUNK, GLA_HEAD_K), lambda c, h: (rc(c), h)),
            pl.BlockSpec((CHUNK, GLA_HEAD_K), lambda c, h: (rc(c), GLA_HEADS + h)),
            pl.BlockSpec((CHUNK, GLA_HEAD_V), lambda c, h: (rc(c), 2 + h)),
            pl.BlockSpec((CHUNK, 128), lambda c, h: (rc(c), GLA_GK_COL // 128)),
            pl.BlockSpec((128, GLA_HEAD_K), lambda c, h: (0, h)),
            pl.BlockSpec((1, GLA_HEAD_K), lambda c, h: (0, h)),
            pl.BlockSpec((None, None, GLA_HEAD_V, GLA_HEAD_K), lambda c, h: (rc(c), h, 0, 0)),
            pl.BlockSpec((CHUNK, GLA_HEAD_V), lambda c, h: (rc(c), h)),
        ],
        out_specs=[
            pl.BlockSpec((CHUNK, GLA_HEAD_K), lambda c, h: (rc(c), h)),
            pl.BlockSpec((CHUNK, GLA_HEAD_K), lambda c, h: (rc(c), h)),
            pl.BlockSpec((CHUNK, GLA_HEAD_V), lambda c, h: (rc(c), h)),
            pl.BlockSpec((CHUNK, 128), lambda c, h: (rc(c), 0)),
            pl.BlockSpec((GLA_HEADS, 128, GLA_HEAD_K), lambda c, h: (0, 0, 0)),
            pl.BlockSpec((GLA_HEADS, 1, GLA_HEAD_K), lambda c, h: (0, 0, 0)),
        ],
        out_shape=[
            jax.ShapeDtypeStruct((L, GLA_DK), F32),
            jax.ShapeDtypeStruct((L, GLA_DK), F32),
            jax.ShapeDtypeStruct((L, D_INNER), F32),
            jax.ShapeDtypeStruct((L, 128), F32),
            jax.ShapeDtypeStruct((GLA_HEADS, 128, GLA_HEAD_K), F32),
            jax.ShapeDtypeStruct((GLA_HEADS, 1, GLA_HEAD_K), F32),
        ],
        scratch_shapes=[pltpu.VMEM((GLA_HEADS, GLA_HEAD_V, GLA_HEAD_K), F32)],
        compiler_params=_cparams(("arbitrary", "arbitrary")),
        name="gla_bwd",
    )(proj, proj, proj, proj, wup, bias, s_in, do)


def _ssd_chunk(g, xs, Bm, Cm, dtp, dtb, alog, dsk, h_in):
    gw = SSD_HPG * SSD_P
    dt = _softplus(dtp + dtb)
    a = dt * (-jnp.exp(alog))
    r = _iota((CHUNK, CHUNK), 0)
    c = _iota((CHUNK, CHUNK), 1)
    tri = jnp.where(c <= r, 1.0, 0.0).astype(F32)
    acum = _dot(tri, a, HI)
    a_last = jnp.sum(a, axis=0, keepdims=True)
    e_g = jnp.where(_iota((128, gw), 0) == SSD_HPG * g + _iota((128, gw), 1) // SSD_P, 1.0, 0.0).astype(F32)
    dt_b = _dot(dt, e_g, HI)
    ac_b = _dot(acum, e_g, HI)
    al_b = _dot(jnp.broadcast_to(a_last, (8, 128)), e_g, HI)[0:1]
    d_b = _dot(jnp.broadcast_to(dsk, (8, 128)), e_g, HI)[0:1]
    xdt = xs * dt_b
    cb = _mxu_nt(Cm, Bm)
    lane_head = _iota((CHUNK, gw), 1) // SSD_P
    y = jnp.zeros((CHUNK, gw), F32)
    for j in range(SSD_HPG):
        eh = jnp.where(_iota((128, CHUNK), 0) == SSD_HPG * g + j, 1.0, 0.0).astype(F32)
        eht = jnp.where(_iota((CHUNK, 128), 1) == SSD_HPG * g + j, 1.0, 0.0).astype(F32)
        ac_c = _dot(acum, eh, HI)
        ac_r = _dot_nt(eht, acum, HI)
        lmat = jnp.where(c <= r, jnp.exp(jnp.minimum(ac_c - ac_r, 0.0)), 0.0)
        yj = _mxu(cb * lmat, xdt)
        y = y + jnp.where(lane_head == j, yj, 0.0)
    st = _mxu_tn(xdt * jnp.exp(al_b - ac_b), Bm)
    e_r = jnp.where(_iota((gw, 128), 1) == SSD_HPG * g + _iota((gw, 128), 0) // SSD_P, 1.0, 0.0).astype(F32)
    cd_rows = _dot_nt(e_r, jnp.broadcast_to(jnp.exp(a_last), (SSD_N, 128)), HI)
    h_out = cd_rows * h_in + st
    y = y + _mxu_nt(Cm, h_in) * jnp.exp(ac_b) + xs * d_b
    return y, h_out


def _ssd_fwd(xbc, proj, dtb, alog, dsk):
    L = xbc.shape[0]
    nc = L // CHUNK
    gw = SSD_HPG * SSD_P

    def body(xs_ref, b_ref, c_ref, dt_ref, dtb_ref, alog_ref, dsk_ref, y_ref, hs_ref, hst):
        g = pl.program_id(1)

        @pl.when(pl.program_id(0) == 0)
        def _():
            hst[g] = jnp.zeros((gw, SSD_N), F32)

        h_in = hst[g]
        hs_ref[...] = h_in
        y, h_out = _ssd_chunk(g, xs_ref[...], b_ref[...], c_ref[...], dt_ref[...], dtb_ref[...], alog_ref[...], dsk_ref[...], h_in)
        y_ref[...] = y
        hst[g] = h_out

    vec = pl.BlockSpec((1, 128), lambda c, g: (0, 0))
    return pl.pallas_call(
        body,
        grid=(nc, SSD_GROUPS),
        in_specs=[
            pl.BlockSpec((CHUNK, gw), lambda c, g: (c, g)),
            pl.BlockSpec((CHUNK, SSD_N), lambda c, g: (c, D_INNER // SSD_N + g)),
            pl.BlockSpec((CHUNK, SSD_N), lambda c, g: (c, D_INNER // SSD_N + SSD_GROUPS + g)),
            pl.BlockSpec((CHUNK, 128), lambda c, g: (c, SSD_DT_COL // 128)),
            vec, vec, vec,
        ],
        out_specs=[
            pl.BlockSpec((CHUNK, gw), lambda c, g: (c, g)),
            pl.BlockSpec((None, None, gw, SSD_N), lambda c, g: (c, g, 0, 0)),
        ],
        out_shape=[
            jax.ShapeDtypeStruct((L, D_INNER), F32),
            jax.ShapeDtypeStruct((nc, SSD_GROUPS, gw, SSD_N), F32),
        ],
        scratch_shapes=[pltpu.VMEM((SSD_GROUPS, gw, SSD_N), F32)],
        compiler_params=_cparams(("arbitrary", "arbitrary")),
        name="ssd_fwd",
    )(xbc, xbc, xbc, proj, dtb, alog, dsk)


def _ssd_bwd(xbc, proj, dtb, alog, dsk, h_saved, dy):
    L = xbc.shape[0]
    nc = L // CHUNK
    gw = SSD_HPG * SSD_P

    def body(xs_ref, b_ref, c_ref, dt_ref, dtb_ref, alog_ref, dsk_ref, hs_ref, dy_ref,
             dxs_ref, db_ref, dc_ref, ddt_ref, ddtb_ref, dalog_ref, ddsk_ref, dhst):
        g = pl.program_id(1)
        first = pl.program_id(0) == 0

        @pl.when(first)
        def _():
            dhst[g] = jnp.zeros((gw, SSD_N), F32)

        @pl.when(first & (g == 0))
        def _():
            ddtb_ref[...] = jnp.zeros((1, 128), F32)
            dalog_ref[...] = jnp.zeros((1, 128), F32)
            ddsk_ref[...] = jnp.zeros((1, 128), F32)

        _, vjp = jax.vjp(functools.partial(_ssd_chunk, g), xs_ref[...], b_ref[...], c_ref[...], dt_ref[...],
                         dtb_ref[...], alog_ref[...], dsk_ref[...], hs_ref[...])
        dxs, db, dc, ddt, ddtb, dalog, ddsk, dh = vjp((dy_ref[...], dhst[g]))
        dxs_ref[...] = dxs
        db_ref[...] = db
        dc_ref[...] = dc
        dhst[g] = dh
        ddtb_ref[...] += ddtb
        dalog_ref[...] += dalog
        ddsk_ref[...] += ddsk

        @pl.when(g == 0)
        def _():
            ddt_ref[...] = ddt

        @pl.when(g != 0)
        def _():
            ddt_ref[...] += ddt

    rc = lambda c: nc - 1 - c
    vec = pl.BlockSpec((1, 128), lambda c, g: (0, 0))
    vshape = jax.ShapeDtypeStruct((1, 128), F32)
    return pl.pallas_call(
        body,
        grid=(nc, SSD_GROUPS),
        in_specs=[
            pl.BlockSpec((CHUNK, gw), lambda c, g: (rc(c), g)),
            pl.BlockSpec((CHUNK, SSD_N), lambda c, g: (rc(c), D_INNER // SSD_N + g)),
            pl.BlockSpec((CHUNK, SSD_N), lambda c, g: (rc(c), D_INNER // SSD_N + SSD_GROUPS + g)),
            pl.BlockSpec((CHUNK, 128), lambda c, g: (rc(c), SSD_DT_COL // 128)),
            vec, vec, vec,
            pl.BlockSpec((None, None, gw, SSD_N), lambda c, g: (rc(c), g, 0, 0)),
            pl.BlockSpec((CHUNK, gw), lambda c, g: (rc(c), g)),
        ],
        out_specs=[
            pl.BlockSpec((CHUNK, gw), lambda c, g: (rc(c), g)),
            pl.BlockSpec((CHUNK, SSD_N), lambda c, g: (rc(c), g)),
            pl.BlockSpec((CHUNK, SSD_N), lambda c, g: (rc(c), g)),
            pl.BlockSpec((CHUNK, 128), lambda c, g: (rc(c), 0)),
            vec, vec, vec,
        ],
        out_shape=[
            jax.ShapeDtypeStruct((L, D_INNER), F32),
            jax.ShapeDtypeStruct((L, SSD_GROUPS * SSD_N), F32),
            jax.ShapeDtypeStruct((L, SSD_GROUPS * SSD_N), F32),
            jax.ShapeDtypeStruct((L, 128), F32),
            vshape, vshape, vshape,
        ],
        scratch_shapes=[pltpu.VMEM((SSD_GROUPS, gw, SSD_N), F32)],
        compiler_params=_cparams(("arbitrary", "arbitrary")),
        name="ssd_bwd",
    )(xbc, xbc, xbc, proj, dtb, alog, dsk, h_saved, dy)


def _rms(x, w):
    return x * lax.rsqrt(jnp.mean(x * x, axis=-1, keepdims=True) + RMS_EPS) * w


def _gla_post(o, g, wn):
    return _rms(o, wn) * _silu(g)


GLA_HALF = 2 * GLA_HEAD_V


def _gla_specs(nc, rev):
    ci = (lambda c: nc - 1 - c) if rev else (lambda c: c)
    v0 = 2 * GLA_DK // GLA_HALF
    g0 = (2 * GLA_DK + D_INNER) // GLA_HALF
    return [
        pl.BlockSpec((CHUNK, GLA_DK), lambda c: (ci(c), 0)),
        pl.BlockSpec((CHUNK, GLA_DK), lambda c: (ci(c), 1)),
        pl.BlockSpec((CHUNK, GLA_HALF), lambda c: (ci(c), v0)),
        pl.BlockSpec((CHUNK, GLA_HALF), lambda c: (ci(c), v0 + 1)),
        pl.BlockSpec((CHUNK, GLA_HALF), lambda c: (ci(c), g0)),
        pl.BlockSpec((CHUNK, GLA_HALF), lambda c: (ci(c), g0 + 1)),
        pl.BlockSpec((CHUNK, 128), lambda c: (ci(c), GLA_GK_COL // 128)),
        pl.BlockSpec((128, GLA_DK), lambda c: (0, 0)),
        pl.BlockSpec((1, GLA_DK), lambda c: (0, 0)),
        pl.BlockSpec((1, GLA_HEAD_V), lambda c: (0, 0)),
    ]


def _head_cols(ref_a, ref_b, h):
    ref = ref_a if h < 2 else ref_b
    return ref[:, (h % 2) * GLA_HEAD_V:(h % 2 + 1) * GLA_HEAD_V]


def _gla_layer_fwd(proj, wup, bias, wn):
    L = proj.shape[0]
    nc = L // CHUNK

    def body(q_ref, k_ref, va_ref, vb_ref, ga_ref, gb_ref, gk_ref, wup_ref, b_ref, wn_ref, o_ref, og_ref, s_ref, st):
        @pl.when(pl.program_id(0) == 0)
        def _():
            st[...] = jnp.zeros(st.shape, F32)

        gk = gk_ref[...]
        for h in range(GLA_HEADS):
            kc = slice(h * GLA_HEAD_K, (h + 1) * GLA_HEAD_K)
            vc = slice(h * GLA_HEAD_V, (h + 1) * GLA_HEAD_V)
            s_in = st[h]
            s_ref[h] = s_in
            o, s_new = _gla_chunk(q_ref[:, kc], k_ref[:, kc], _head_cols(va_ref, vb_ref, h), gk, wup_ref[:, kc], b_ref[:, kc], s_in)
            st[h] = s_new
            o_ref[:, vc] = o
            og_ref[:, vc] = _gla_post(o, _head_cols(ga_ref, gb_ref, h), wn_ref[...]).astype(BF16)

    return pl.pallas_call(
        body,
        grid=(nc,),
        in_specs=_gla_specs(nc, False),
        out_specs=[
            pl.BlockSpec((CHUNK, D_INNER), lambda c: (c, 0)),
            pl.BlockSpec((CHUNK, D_INNER), lambda c: (c, 0)),
            pl.BlockSpec((None, GLA_HEADS, GLA_HEAD_V, GLA_HEAD_K), lambda c: (c, 0, 0, 0)),
        ],
        out_shape=[
            jax.ShapeDtypeStruct((L, D_INNER), F32),
            jax.ShapeDtypeStruct((L, D_INNER), BF16),
            jax.ShapeDtypeStruct((nc, GLA_HEADS, GLA_HEAD_V, GLA_HEAD_K), F32),
        ],
        scratch_shapes=[pltpu.VMEM((GLA_HEADS, GLA_HEAD_V, GLA_HEAD_K), F32)],
        compiler_params=_cparams(("arbitrary",)),
        name="gla_layer_fwd",
    )(proj, proj, proj, proj, proj, proj, proj, wup, bias, wn)


def _gla_layer_bwd(proj, wup, bias, wn, o, s_in, dog):
    L = proj.shape[0]
    nc = L // CHUNK

    def body(q_ref, k_ref, va_ref, vb_ref, ga_ref, gb_ref, gk_ref, wup_ref, b_ref, wn_ref, o_ref, s_ref, dog_ref,
             dp_ref, dwup_ref, db_ref, dwn_ref, dst):
        @pl.when(pl.program_id(0) == 0)
        def _():
            dst[...] = jnp.zeros(dst.shape, F32)
            dwup_ref[...] = jnp.zeros(dwup_ref.shape, F32)
            db_ref[...] = jnp.zeros(db_ref.shape, F32)
            dwn_ref[...] = jnp.zeros(dwn_ref.shape, F32)

        gk = gk_ref[...]
        dgk_sum = jnp.zeros((CHUNK, 128), F32)
        for h in range(GLA_HEADS):
            kc = slice(h * GLA_HEAD_K, (h + 1) * GLA_HEAD_K)
            vc = slice(h * GLA_HEAD_V, (h + 1) * GLA_HEAD_V)
            _, post_vjp = jax.vjp(_gla_post, o_ref[:, vc], _head_cols(ga_ref, gb_ref, h), wn_ref[...])
            do, dg, dwn = post_vjp(dog_ref[:, vc])
            _, vjp = jax.vjp(_gla_chunk, q_ref[:, kc], k_ref[:, kc], _head_cols(va_ref, vb_ref, h), gk, wup_ref[:, kc],
                             b_ref[:, kc], s_ref[h])
            dq, dk, dv, dgk, dwup, db, ds = vjp((do, dst[h]))
            dst[h] = ds
            dp_ref[:, kc] = dq.astype(BF16)
            dp_ref[:, GLA_DK + h * GLA_HEAD_K:GLA_DK + (h + 1) * GLA_HEAD_K] = dk.astype(BF16)
            dp_ref[:, 2 * GLA_DK + h * GLA_HEAD_V:2 * GLA_DK + (h + 1) * GLA_HEAD_V] = dv.astype(BF16)
            dp_ref[:, 2 * GLA_DK + D_INNER + h * GLA_HEAD_V:2 * GLA_DK + D_INNER + (h + 1) * GLA_HEAD_V] = dg.astype(BF16)
            dwup_ref[:, kc] += dwup
            db_ref[:, kc] += db
            dwn_ref[...] += dwn
            dgk_sum = dgk_sum + dgk
        dp_ref[:, GLA_GK_COL:GLA_GK_COL + 128] = dgk_sum.astype(BF16)
        dp_ref[:, GLA_GK_COL + 128:] = jnp.zeros((CHUNK, GLA_PROJ_PAD - GLA_GK_COL - 128), BF16)

    rc = lambda c: nc - 1 - c
    return pl.pallas_call(
        body,
        grid=(nc,),
        in_specs=_gla_specs(nc, True) + [
            pl.BlockSpec((CHUNK, D_INNER), lambda c: (rc(c), 0)),
            pl.BlockSpec((None, GLA_HEADS, GLA_HEAD_V, GLA_HEAD_K), lambda c: (rc(c), 0, 0, 0)),
            pl.BlockSpec((CHUNK, D_INNER), lambda c: (rc(c), 0)),
        ],
        out_specs=[
            pl.BlockSpec((CHUNK, GLA_PROJ_PAD), lambda c: (rc(c), 0)),
            pl.BlockSpec((128, GLA_DK), lambda c: (0, 0)),
            pl.BlockSpec((1, GLA_DK), lambda c: (0, 0)),
            pl.BlockSpec((1, GLA_HEAD_V), lambda c: (0, 0)),
        ],
        out_shape=[
            jax.ShapeDtypeStruct((L, GLA_PROJ_PAD), BF16),
            jax.ShapeDtypeStruct((128, GLA_DK), F32),
            jax.ShapeDtypeStruct((1, GLA_DK), F32),
            jax.ShapeDtypeStruct((1, GLA_HEAD_V), F32),
        ],
        scratch_shapes=[pltpu.VMEM((GLA_HEADS, GLA_HEAD_V, GLA_HEAD_K), F32)],
        compiler_params=_cparams(("arbitrary",)),
        name="gla_layer_bwd",
    )(proj, proj, proj, proj, proj, proj, proj, wup, bias, wn, o, s_in, dog)


@jax.custom_vjp
def _expand(v):
    r = v.shape[0]
    left = _iota((r, 128), 1) < SSD_P
    slabs = []
    for p in range(SSD_HEADS // 2):
        a = jnp.broadcast_to(v[:, 2 * p:2 * p + 1], (r, 128))
        b = jnp.broadcast_to(v[:, 2 * p + 1:2 * p + 2], (r, 128))
        slabs.append(jnp.where(left, a, b))
    return jnp.concatenate(slabs, axis=1)


def _expand_fwd(v):
    return _expand(v), None


def _expand_bwd(_, g):
    r = g.shape[0]
    lane = _iota((r, 128), 1)
    left = lane < SSD_P
    dv = jnp.zeros((r, 128), F32)
    for p in range(SSD_HEADS // 2):
        gs = g[:, 128 * p:128 * (p + 1)]
        sa = jnp.sum(jnp.where(left, gs, 0.0), axis=-1, keepdims=True)
        sb = jnp.sum(jnp.where(left, 0.0, gs), axis=-1, keepdims=True)
        dv = dv + jnp.where(lane == 2 * p, sa, 0.0) + jnp.where(lane == 2 * p + 1, sb, 0.0)
    return (dv,)


_expand.defvjp(_expand_fwd, _expand_bwd)


def _scan_rows(a, reverse):
    n = a.shape[0]
    row = _iota(a.shape, 0)
    sh = 1
    while sh < n:
        if reverse:
            a = a + jnp.where(row < n - sh, pltpu.roll(a, n - sh, 0), 0.0)
        else:
            a = a + jnp.where(row >= sh, pltpu.roll(a, sh, 0), 0.0)
        sh *= 2
    return a


@jax.custom_vjp
def _cumsum_rows(a):
    return _scan_rows(a, False)


def _cumsum_rows_fwd(a):
    return _scan_rows(a, False), None


def _cumsum_rows_bwd(_, g):
    return (_scan_rows(g, True),)


_cumsum_rows.defvjp(_cumsum_rows_fwd, _cumsum_rows_bwd)

SSD_GW = SSD_HPG * SSD_P
SSD_BC = SSD_GROUPS * SSD_N


def _ssd_chunk_all(xs, Bm, Cm, dtp, dtb, alog, dsk, h_in):
    dt = _softplus(dtp + dtb)
    acum = _cumsum_rows(dt * (-jnp.exp(alog)))
    a_last = acum[CHUNK - 1:CHUNK]
    acum_b = _expand(acum)
    w_end = _expand(dt * jnp.exp(a_last - acum))
    d_b = _expand(jnp.broadcast_to(dsk, (8, 128)))[0:1]
    ac_t = jnp.concatenate([acum, acum], axis=0).T
    dt_t = jnp.concatenate([dt, dt], axis=0).T
    lane = _iota((CHUNK, 128), 1)
    left = lane < SSD_P
    causal = (lane & (SSD_P - 1)) <= _iota((CHUNK, 128), 0)
    ys, sts = [], []
    for g in range(SSD_GROUPS):
        Bg = Bm[:, g * SSD_N:(g + 1) * SSD_N]
        Cg = Cm[:, g * SSD_N:(g + 1) * SSD_N]
        cb2 = _mxu_nt(Cg, jnp.concatenate([Bg, Bg], axis=0))
        y_off = _mxu_nt(Cg, h_in[g * SSD_GW:(g + 1) * SSD_GW])
        for k in range(SSD_HPG // 2):
            p = g * (SSD_HPG // 2) + k
            sl = slice(128 * p, 128 * (p + 1))
            ac_c = acum_b[:, sl]
            ac_r = jnp.where(left, ac_t[2 * p:2 * p + 1], ac_t[2 * p + 1:2 * p + 2])
            dt_r = jnp.where(left, dt_t[2 * p:2 * p + 1], dt_t[2 * p + 1:2 * p + 2])
            m2 = cb2 * jnp.where(causal, jnp.exp(jnp.minimum(ac_c - ac_r, 0.0)), 0.0) * dt_r
            xsl = xs[:, sl]
            x2 = jnp.concatenate([jnp.where(left, xsl, 0.0), jnp.where(left, 0.0, xsl)], axis=0)
            ys.append(_mxu(m2, x2) + y_off[:, 128 * k:128 * (k + 1)] * jnp.exp(ac_c) + xsl * d_b[:, sl])
        gs = slice(g * SSD_GW, (g + 1) * SSD_GW)
        sts.append(_mxu_tn(xs[:, gs] * w_end[:, gs], Bg))
    cd = jnp.exp(ac_t[:, CHUNK - 1:CHUNK])
    hs = [h_in[h * SSD_P:(h + 1) * SSD_P] * cd[h:h + 1] for h in range(SSD_HEADS)]
    h_out = jnp.concatenate(hs, axis=0) + jnp.concatenate(sts, axis=0)
    return jnp.concatenate(ys, axis=1), h_out


def _ssd_specs(nc, rev):
    ci = (lambda c: nc - 1 - c) if rev else (lambda c: c)
    vec = pl.BlockSpec((1, 128), lambda c: (0, 0))
    return [
        pl.BlockSpec((CHUNK, D_INNER), lambda c: (ci(c), 0)),
        pl.BlockSpec((CHUNK, SSD_BC), lambda c: (ci(c), D_INNER // SSD_BC)),
        pl.BlockSpec((CHUNK, SSD_BC), lambda c: (ci(c), D_INNER // SSD_BC + 1)),
        pl.BlockSpec((CHUNK, 128), lambda c: (ci(c), SSD_DT_COL // 128)),
        vec, vec, vec,
    ]


def _ssd_layer_fwd(xbc, proj, dtb, alog, dsk):
    L = xbc.shape[0]
    nc = L // CHUNK

    def body(xs_ref, b_ref, c_ref, dt_ref, dtb_ref, alog_ref, dsk_ref, y_ref, hs_ref, hst):
        @pl.when(pl.program_id(0) == 0)
        def _():
            hst[...] = jnp.zeros(hst.shape, F32)

        h_in = hst[...]
        hs_ref[...] = h_in
        y, h_out = _ssd_chunk_all(xs_ref[...], b_ref[...], c_ref[...], dt_ref[...], dtb_ref[...], alog_ref[...], dsk_ref[...], h_in)
        y_ref[...] = y
        hst[...] = h_out

    return pl.pallas_call(
        body,
        grid=(nc,),
        in_specs=_ssd_specs(nc, False),
        out_specs=[
            pl.BlockSpec((CHUNK, D_INNER), lambda c: (c, 0)),
            pl.BlockSpec((None, D_INNER, SSD_N), lambda c: (c, 0, 0)),
        ],
        out_shape=[
            jax.ShapeDtypeStruct((L, D_INNER), F32),
            jax.ShapeDtypeStruct((nc, D_INNER, SSD_N), F32),
        ],
        scratch_shapes=[pltpu.VMEM((D_INNER, SSD_N), F32)],
        compiler_params=_cparams(("arbitrary",)),
        name="ssd_layer_fwd",
    )(xbc, xbc, xbc, proj, dtb, alog, dsk)


def _ssd_layer_bwd(xbc, proj, dtb, alog, dsk, h_saved, dy):
    L = xbc.shape[0]
    nc = L // CHUNK

    def body(xs_ref, b_ref, c_ref, dt_ref, dtb_ref, alog_ref, dsk_ref, hs_ref, dy_ref,
             dx_ref, ddt_ref, ddtb_ref, dalog_ref, ddsk_ref, dhst):
        @pl.when(pl.program_id(0) == 0)
        def _():
            dhst[...] = jnp.zeros(dhst.shape, F32)
            ddtb_ref[...] = jnp.zeros((1, 128), F32)
            dalog_ref[...] = jnp.zeros((1, 128), F32)
            ddsk_ref[...] = jnp.zeros((1, 128), F32)

        _, vjp = jax.vjp(_ssd_chunk_all, xs_ref[...], b_ref[...], c_ref[...], dt_ref[...], dtb_ref[...], alog_ref[...],
                         dsk_ref[...], hs_ref[...])
        dxs, db, dc, ddt, ddtb, dalog, ddsk, dh = vjp((dy_ref[...], dhst[...]))
        dx_ref[:, :D_INNER] = dxs
        dx_ref[:, D_INNER:D_INNER + SSD_BC] = db
        dx_ref[:, D_INNER + SSD_BC:] = dc
        ddt_ref[...] = ddt
        dhst[...] = dh
        ddtb_ref[...] += ddtb
        dalog_ref[...] += dalog
        ddsk_ref[...] += ddsk

    rc = lambda c: nc - 1 - c
    vec = pl.BlockSpec((1, 128), lambda c: (0, 0))
    vshape = jax.ShapeDtypeStruct((1, 128), F32)
    return pl.pallas_call(
        body,
        grid=(nc,),
        in_specs=_ssd_specs(nc, True) + [
            pl.BlockSpec((None, D_INNER, SSD_N), lambda c: (rc(c), 0, 0)),
            pl.BlockSpec((CHUNK, D_INNER), lambda c: (rc(c), 0)),
        ],
        out_specs=[
            pl.BlockSpec((CHUNK, SSD_CONV_DIM), lambda c: (rc(c), 0)),
            pl.BlockSpec((CHUNK, 128), lambda c: (rc(c), 0)),
            vec, vec, vec,
        ],
        out_shape=[
            jax.ShapeDtypeStruct((L, SSD_CONV_DIM), F32),
            jax.ShapeDtypeStruct((L, 128), F32),
            vshape, vshape, vshape,
        ],
        scratch_shapes=[pltpu.VMEM((D_INNER, SSD_N), F32)],
        compiler_params=_cparams(("arbitrary",)),
        name="ssd_layer_bwd",
    )(xbc, xbc, xbc, proj, dtb, alog, dsk, h_saved, dy)


def _pick(n, options):
    for t in options:
        if n % t == 0:
            return t
    return n


TOKEN_TILE = 512


def _mm(a, b, *, name, out_dtype=F32, add=None):
    M, K = a.shape
    N = b.shape[1]
    tm = min(TOKEN_TILE, M)
    tn = _pick(N, (1024, 896, 512, 256, 128))
    tk = _pick(K, (1024, 896, 512, 256, 128))
    nk = K // tk

    def body(*refs):
        if add is None:
            a_ref, b_ref, o_ref, acc = refs
        else:
            a_ref, b_ref, add_ref, o_ref, acc = refs
        k = pl.program_id(2)
        p = _dot(_bf(a_ref[...]), _bf(b_ref[...]))

        @pl.when(k == 0)
        def _():
            acc[...] = p

        @pl.when(k > 0)
        def _():
            acc[...] += p

        @pl.when(k == nk - 1)
        def _():
            r = acc[...]
            if add is not None:
                r = r + add_ref[...]
            o_ref[...] = r.astype(out_dtype)

    in_specs = [pl.BlockSpec((tm, tk), lambda i, j, k: (i, k)), pl.BlockSpec((tk, tn), lambda i, j, k: (k, j))]
    args = [a, b]
    if add is not None:
        in_specs.append(pl.BlockSpec((tm, tn), lambda i, j, k: (i, j)))
        args.append(add)
    return pl.pallas_call(
        body,
        grid=(M // tm, N // tn, nk),
        in_specs=in_specs,
        out_specs=pl.BlockSpec((tm, tn), lambda i, j, k: (i, j)),
        out_shape=jax.ShapeDtypeStruct((M, N), out_dtype),
        scratch_shapes=[pltpu.VMEM((tm, tn), F32)],
        compiler_params=_cparams(("parallel", "parallel", "arbitrary")),
        name=name,
    )(*args)


def _mm_tn(a, b, *, name):
    M, K = a.shape
    N = b.shape[1]
    tm = min(TOKEN_TILE, M)
    tn = _pick(N, (896, 512, 256, 128))

    def body(a_ref, b_ref, o_ref):
        p = _dot_tn(_bf(a_ref[...]), _bf(b_ref[...]))

        @pl.when(pl.program_id(1) == 0)
        def _():
            o_ref[...] = p

        @pl.when(pl.program_id(1) > 0)
        def _():
            o_ref[...] += p

    return pl.pallas_call(
        body,
        grid=(N // tn, M // tm),
        in_specs=[pl.BlockSpec((tm, K), lambda j, i: (i, 0)), pl.BlockSpec((tm, tn), lambda j, i: (i, j))],
        out_specs=pl.BlockSpec((K, tn), lambda j, i: (0, j)),
        out_shape=jax.ShapeDtypeStruct((K, N), F32),
        compiler_params=_cparams(("parallel", "arbitrary")),
        name=name,
    )(a, b)


def _rms(x, w):
    return x * lax.rsqrt(jnp.mean(x * x, axis=-1, keepdims=True) + RMS_EPS) * w


def _acc_out(ref, val, first):
    @pl.when(first)
    def _():
        ref[...] = val

    @pl.when(jnp.logical_not(first))
    def _():
        ref[...] += val


def _rms_fwd(x, w, *, name):
    L, D = x.shape
    tm = min(TOKEN_TILE, L)

    def body(x_ref, w_ref, o_ref):
        o_ref[...] = _rms(x_ref[...], w_ref[...]).astype(BF16)

    return pl.pallas_call(
        body, grid=(L // tm,),
        in_specs=[pl.BlockSpec((tm, D), lambda i: (i, 0)), pl.BlockSpec((1, D), lambda i: (0, 0))],
        out_specs=pl.BlockSpec((tm, D), lambda i: (i, 0)),
        out_shape=jax.ShapeDtypeStruct((L, D), BF16),
        compiler_params=_cparams(("parallel",)), name=name,
    )(x, w)


def _rms_bwd(x, w, dhn, dres, *, name):
    L, D = x.shape
    tm = min(TOKEN_TILE, L)

    def body(x_ref, w_ref, dhn_ref, dres_ref, dx_ref, dw_ref):
        _, vjp = jax.vjp(_rms, x_ref[...], w_ref[...])
        dx, dw = vjp(dhn_ref[...])
        dx_ref[...] = dx + dres_ref[...]
        _acc_out(dw_ref, dw, pl.program_id(0) == 0)

    row = pl.BlockSpec((tm, D), lambda i: (i, 0))
    vec = pl.BlockSpec((1, D), lambda i: (0, 0))
    return pl.pallas_call(
        body, grid=(L // tm,),
        in_specs=[row, vec, row, row],
        out_specs=[row, vec],
        out_shape=[jax.ShapeDtypeStruct((L, D), F32), jax.ShapeDtypeStruct((1, D), F32)],
        compiler_params=_cparams(("arbitrary",)), name=name,
    )(x, w, dhn, dres)


def _gla_post(o, g, wn):
    return _rms(o, wn) * _silu(g)


def _gla_post_fwd(o, proj, wn):
    L = o.shape[0]
    tm = min(TOKEN_TILE, L)
    gcol = (2 * GLA_DK + D_INNER) // GLA_HEAD_V

    def body(o_ref, g_ref, w_ref, y_ref):
        y_ref[...] = _gla_post(o_ref[...], g_ref[...], w_ref[...]).astype(BF16)

    blk = pl.BlockSpec((tm, GLA_HEAD_V), lambda i, h: (i, h))
    return pl.pallas_call(
        body, grid=(L // tm, GLA_HEADS),
        in_specs=[blk, pl.BlockSpec((tm, GLA_HEAD_V), lambda i, h: (i, gcol + h)), pl.BlockSpec((1, GLA_HEAD_V), lambda i, h: (0, 0))],
        out_specs=blk,
        out_shape=jax.ShapeDtypeStruct((L, D_INNER), BF16),
        compiler_params=_cparams(("parallel", "parallel")), name="gla_post_fwd",
    )(o, proj, wn)


def _gla_post_bwd(o, proj, wn, dy):
    L = o.shape[0]
    tm = min(TOKEN_TILE, L)
    gcol = (2 * GLA_DK + D_INNER) // GLA_HEAD_V

    def body(o_ref, g_ref, w_ref, dy_ref, do_ref, dg_ref, dw_ref):
        _, vjp = jax.vjp(_gla_post, o_ref[...], g_ref[...], w_ref[...])
        do, dg, dw = vjp(dy_ref[...])
        do_ref[...] = do
        dg_ref[...] = dg
        _acc_out(dw_ref, dw, (pl.program_id(0) == 0) & (pl.program_id(1) == 0))

    blk = pl.BlockSpec((tm, GLA_HEAD_V), lambda i, h: (i, h))
    vec = pl.BlockSpec((1, GLA_HEAD_V), lambda i, h: (0, 0))
    return pl.pallas_call(
        body, grid=(L // tm, GLA_HEADS),
        in_specs=[blk, pl.BlockSpec((tm, GLA_HEAD_V), lambda i, h: (i, gcol + h)), vec, blk],
        out_specs=[blk, blk, vec],
        out_shape=[jax.ShapeDtypeStruct((L, D_INNER), F32), jax.ShapeDtypeStruct((L, D_INNER), F32),
                   jax.ShapeDtypeStruct((1, GLA_HEAD_V), F32)],
        compiler_params=_cparams(("arbitrary", "arbitrary")), name="gla_post_bwd",
    )(o, proj, wn, dy)


def _ssd_post(y, z, wn):
    return _rms(y * _silu(z), wn)


def _ssd_post_fwd(y, proj, wn):
    L = y.shape[0]
    tm = min(TOKEN_TILE // 2, L)

    def body(y_ref, z_ref, w_ref, o_ref):
        o_ref[...] = _ssd_post(y_ref[...], z_ref[...], w_ref[...]).astype(BF16)

    blk = pl.BlockSpec((tm, D_INNER), lambda i: (i, 0))
    return pl.pallas_call(
        body, grid=(L // tm,),
        in_specs=[blk, blk, pl.BlockSpec((1, D_INNER), lambda i: (0, 0))],
        out_specs=blk,
        out_shape=jax.ShapeDtypeStruct((L, D_INNER), BF16),
        compiler_params=_cparams(("parallel",)), name="ssd_post_fwd",
    )(y, proj, wn)


def _ssd_post_bwd(y, proj, wn, dyn):
    L = y.shape[0]
    tm = min(TOKEN_TILE // 2, L)

    def body(y_ref, z_ref, w_ref, dyn_ref, dy_ref, dz_ref, dw_ref):
        _, vjp = jax.vjp(_ssd_post, y_ref[...], z_ref[...], w_ref[...])
        dy, dz, dw = vjp(dyn_ref[...])
        dy_ref[...] = dy
        dz_ref[...] = dz
        _acc_out(dw_ref, dw, pl.program_id(0) == 0)

    blk = pl.BlockSpec((tm, D_INNER), lambda i: (i, 0))
    vec = pl.BlockSpec((1, D_INNER), lambda i: (0, 0))
    return pl.pallas_call(
        body, grid=(L // tm,),
        in_specs=[blk, blk, vec, blk],
        out_specs=[blk, blk, vec],
        out_shape=[jax.ShapeDtypeStruct((L, D_INNER), F32), jax.ShapeDtypeStruct((L, D_INNER), F32),
                   jax.ShapeDtypeStruct((1, D_INNER), F32)],
        compiler_params=_cparams(("arbitrary",)), name="ssd_post_bwd",
    )(y, proj, wn, dyn)


CONV_HALO = 8
CONV_COLS = 1024


def _conv_tile(xin, halo, w, b):
    tm = xin.shape[0]
    xx = jnp.concatenate([halo, xin], axis=0)
    u = b
    for k in range(SSD_CONV):
        off = CONV_HALO - (SSD_CONV - 1) + k
        u = u + w[k:k + 1] * xx[off:off + tm]
    return _silu(u)


def _conv_fwd(proj, w, b):
    L = proj.shape[0]
    tm = min(TOKEN_TILE, L)
    c0 = D_INNER // CONV_COLS
    hb = tm // CONV_HALO

    def body(x_ref, h_ref, w_ref, b_ref, o_ref):
        halo = jnp.where(pl.program_id(1) == 0, 0.0, h_ref[...])
        o_ref[...] = _conv_tile(x_ref[...], halo, w_ref[...], b_ref[...])

    return pl.pallas_call(
        body, grid=(SSD_CONV_DIM // CONV_COLS, L // tm),
        in_specs=[
            pl.BlockSpec((tm, CONV_COLS), lambda j, i: (i, c0 + j)),
            pl.BlockSpec((CONV_HALO, CONV_COLS), lambda j, i: (jnp.maximum(i * hb - 1, 0), c0 + j)),
            pl.BlockSpec((SSD_CONV, CONV_COLS), lambda j, i: (0, j)),
            pl.BlockSpec((1, CONV_COLS), lambda j, i: (0, j)),
        ],
        out_specs=pl.BlockSpec((tm, CONV_COLS), lambda j, i: (i, j)),
        out_shape=jax.ShapeDtypeStruct((L, SSD_CONV_DIM), F32),
        compiler_params=_cparams(("parallel", "parallel")), name="conv_fwd",
    )(proj, proj, w, b)


def _conv_bwd(proj, w, b, dxbc):
    L = proj.shape[0]
    tm = min(TOKEN_TILE, L)
    nt = L // tm
    c0 = D_INNER // CONV_COLS
    hb = tm // CONV_HALO

    def body(x_ref, h_ref, w_ref, b_ref, dy_ref, dx_ref, dw_ref, db_ref, carry):
        i = pl.program_id(1)
        first = i == 0

        @pl.when(first)
        def _():
            carry[...] = jnp.zeros((CONV_HALO, CONV_COLS), F32)

        dy = dy_ref[...]
        halo = jnp.where(i == nt - 1, 0.0, h_ref[...])
        _, vjp = jax.vjp(_conv_tile, x_ref[...], halo, w_ref[...], b_ref[...])
        dx, dhalo, dw, db = vjp(dy)
        dx_ref[...] = jnp.concatenate([dx[:tm - CONV_HALO], dx[tm - CONV_HALO:] + carry[...]], axis=0)
        carry[...] = dhalo
        _acc_out(dw_ref, dw, first)
        _acc_out(db_ref, db, first)

    rt = lambda i: nt - 1 - i
    return pl.pallas_call(
        body, grid=(SSD_CONV_DIM // CONV_COLS, nt),
        in_specs=[
            pl.BlockSpec((tm, CONV_COLS), lambda j, i: (rt(i), c0 + j)),
            pl.BlockSpec((CONV_HALO, CONV_COLS), lambda j, i: (jnp.maximum(rt(i) * hb - 1, 0), c0 + j)),
            pl.BlockSpec((SSD_CONV, CONV_COLS), lambda j, i: (0, j)),
            pl.BlockSpec((1, CONV_COLS), lambda j, i: (0, j)),
            pl.BlockSpec((tm, CONV_COLS), lambda j, i: (rt(i), j)),
        ],
        out_specs=[
            pl.BlockSpec((tm, CONV_COLS), lambda j, i: (rt(i), j)),
            pl.BlockSpec((SSD_CONV, CONV_COLS), lambda j, i: (0, j)),
            pl.BlockSpec((1, CONV_COLS), lambda j, i: (0, j)),
        ],
        out_shape=[jax.ShapeDtypeStruct((L, SSD_CONV_DIM), F32), jax.ShapeDtypeStruct((SSD_CONV, SSD_CONV_DIM), F32),
                   jax.ShapeDtypeStruct((1, SSD_CONV_DIM), F32)],
        scratch_shapes=[pltpu.VMEM((CONV_HALO, CONV_COLS), F32)],
        compiler_params=_cparams(("arbitrary", "arbitrary")), name="conv_bwd",
    )(proj, proj, w, b, dxbc)


def _loss_bwd(x, tgt, w):
    L, D = x.shape
    tm = min(TOKEN_TILE, L)

    def body(x_ref, t_ref, w_ref, l_ref, dx_ref, dw_ref):
        xv = x_ref[...]
        wv = w_ref[...]
        r = lax.rsqrt(jnp.mean(xv * xv, axis=-1, keepdims=True) + RMS_EPS)
        xh = xv * r
        e = xh * wv - t_ref[...]
        lsum = 0.5 * jnp.sum(jnp.mean(e * e, axis=-1, keepdims=True), axis=0, keepdims=True)
        dout = e * (1.0 / D)
        gx = dout * wv
        dx_ref[...] = r * (gx - xh * jnp.mean(gx * xh, axis=-1, keepdims=True))
        first = pl.program_id(0) == 0
        _acc_out(dw_ref, jnp.sum(dout * xh, axis=0, keepdims=True), first)
        _acc_out(l_ref, jnp.broadcast_to(lsum, (8, 128)), first)

    row = pl.BlockSpec((tm, D), lambda i: (i, 0))
    vec = pl.BlockSpec((1, D), lambda i: (0, 0))
    return pl.pallas_call(
        body, grid=(L // tm,),
        in_specs=[row, row, vec],
        out_specs=[pl.BlockSpec((8, 128), lambda i: (0, 0)), row, vec],
        out_shape=[jax.ShapeDtypeStruct((8, 128), F32), jax.ShapeDtypeStruct((L, D), F32), jax.ShapeDtypeStruct((1, D), F32)],
        compiler_params=_cparams(("arbitrary",)), name="loss_bwd",
    )(x, tgt, w)


MESH = pl.DeviceIdType.MESH
ANY = pl.BlockSpec(memory_space=pl.ANY)


def _all_gather(xs, *, name):
    R, C = xs.shape

    def body(x_ref, out_ref, send_sems, recv_sems, local_sem):
        x, y, c = lax.axis_index("x"), lax.axis_index("y"), lax.axis_index("c")
        me, sibling = (x, y, c), (x, y, 1 - c)
        chips = [(1 - x, y), (x, 1 - y), (1 - x, 1 - y)]

        def slot(px, py, pc):
            return out_ref.at[4 * px + 2 * py + pc]

        def copy(k, block, to, src=None):
            return pltpu.make_async_remote_copy(
                src_ref=slot(*block) if src is None else src, dst_ref=slot(*block),
                send_sem=send_sems.at[k], recv_sem=recv_sems.at[k], device_id=to, device_id_type=MESH)

        mine = pltpu.make_async_copy(x_ref, slot(*me), local_sem)
        mine.start()
        first = [copy(0, me, sibling, src=x_ref)]
        first += [copy(1 + j, me, (*chip, c), src=x_ref) for j, chip in enumerate(chips)]
        for cp in first:
            cp.start()
        passed = [copy(4 + j, (*chip, c), sibling) for j, chip in enumerate(chips)]
        for j, chip in enumerate(chips):
            copy(1 + j, (*chip, c), me).wait_recv()
            passed[j].start()
        copy(0, sibling, me).wait_recv()
        for j, chip in enumerate(chips):
            copy(4 + j, (*chip, 1 - c), me).wait_recv()
        for cp in first + passed:
            cp.wait_send()
        mine.wait()

    return pl.pallas_call(
        body,
        out_shape=jax.ShapeDtypeStruct((N_DEV, R, C), xs.dtype),
        in_specs=[ANY], out_specs=ANY,
        scratch_shapes=[pltpu.SemaphoreType.DMA((7,)), pltpu.SemaphoreType.DMA((7,)), pltpu.SemaphoreType.DMA],
        name=name,
    )(xs)


def _exchange(parts, *, name):
    _, R, C = parts.shape

    def body(p_ref, out_ref, send_sems, recv_sems, local_sem):
        x, y, c = lax.axis_index("x"), lax.axis_index("y"), lax.axis_index("c")
        my = 4 * x + 2 * y + c
        mine = pltpu.make_async_copy(p_ref.at[my], out_ref.at[my], local_sem)
        mine.start()
        copies = []
        for k in range(1, N_DEV):
            fx, fy, fc = (k >> 2) & 1, (k >> 1) & 1, k & 1
            px, py, pc = (1 - x if fx else x), (1 - y if fy else y), (1 - c if fc else c)
            pid = 4 * px + 2 * py + pc
            copies.append(pltpu.make_async_remote_copy(
                src_ref=p_ref.at[pid], dst_ref=out_ref.at[my], send_sem=send_sems.at[k - 1], recv_sem=recv_sems.at[k - 1],
                device_id=(px, py, pc), device_id_type=MESH))
            copies[-1].start()
        for k in range(1, N_DEV):
            fx, fy, fc = (k >> 2) & 1, (k >> 1) & 1, k & 1
            px, py, pc = (1 - x if fx else x), (1 - y if fy else y), (1 - c if fc else c)
            pid = 4 * px + 2 * py + pc
            pltpu.make_async_remote_copy(
                src_ref=p_ref.at[pid], dst_ref=out_ref.at[pid], send_sem=send_sems.at[k - 1], recv_sem=recv_sems.at[k - 1],
                device_id=(px, py, pc), device_id_type=MESH).wait()
        mine.wait()

    return pl.pallas_call(
        body,
        out_shape=jax.ShapeDtypeStruct((N_DEV, R, C), parts.dtype),
        in_specs=[ANY], out_specs=ANY,
        scratch_shapes=[pltpu.SemaphoreType.DMA((7,)), pltpu.SemaphoreType.DMA((7,)), pltpu.SemaphoreType.DMA],
        name=name,
    )(parts)


def _adamw(parts, w, m, v, *, name):
    R = w.shape[0]
    tr = _pick(R, (512, 256, 128, 64, 40, 32, 16, 8))

    def body(p_ref, w_ref, m_ref, v_ref, g_ref, d_ref, mo_ref, vo_ref):
        g = p_ref[0]
        for s in range(1, N_DEV):
            g = g + p_ref[s]
        mn = ADAM_B1 * m_ref[...] + (1.0 - ADAM_B1) * g
        vn = ADAM_B2 * v_ref[...] + (1.0 - ADAM_B2) * jnp.square(g)
        m_hat = mn / (1.0 - ADAM_B1 ** ADAM_STEP)
        v_hat = vn / (1.0 - ADAM_B2 ** ADAM_STEP)
        g_ref[...] = g
        d_ref[...] = -ADAM_LR * (m_hat / (jnp.sqrt(v_hat) + ADAM_EPS) + ADAM_WD * w_ref[...])
        mo_ref[...] = mn
        vo_ref[...] = vn

    blk = pl.BlockSpec((tr, 128), lambda i: (i, 0))
    shp = jax.ShapeDtypeStruct((R, 128), F32)
    return pl.pallas_call(
        body, grid=(R // tr,),
        in_specs=[pl.BlockSpec((N_DEV, tr, 128), lambda i: (0, i, 0)), blk, blk, blk],
        out_specs=[blk, blk, blk, blk],
        out_shape=[shp, shp, shp, shp],
        compiler_params=_cparams(("parallel",)), name=name,
    )(parts, w, m, v)


def _rows(a):
    return a.reshape(-1, 128)


def _pad_rows(a, rows):
    return jnp.pad(a, ((0, rows - a.shape[0]), (0, 0)))


def _pad_cols(a, cols):
    return jnp.pad(a, ((0, 0), (0, cols - a.shape[1])))


def _col_shards(a, n):
    r = a.shape[0]
    return a.reshape(r, N_DEV, n).transpose(1, 0, 2).reshape(N_DEV, -1, 128)


def _from_col_shards(g, r, n):
    return g.reshape(N_DEV, r, n).transpose(1, 0, 2).reshape(r, N_DEV * n)


def kernel(x, norm_w, gla_in_proj, gla_gate_up, gla_gate_bias, gla_head_norm, gla_out_proj, ssd_in_proj, ssd_conv_w, ssd_conv_b, ssd_dt_bias, ssd_a_log, ssd_d, ssd_gate_norm, ssd_out_proj, final_norm, loss_target, m_norm_w, m_gla_in_proj, m_gla_gate_up, m_gla_gate_bias, m_gla_head_norm, m_gla_out_proj, m_ssd_in_proj, m_ssd_conv_w, m_ssd_conv_b, m_ssd_dt_bias, m_ssd_a_log, m_ssd_d, m_ssd_gate_norm, m_ssd_out_proj, m_final_norm, v_norm_w, v_gla_in_proj, v_gla_gate_up, v_gla_gate_bias, v_gla_head_norm, v_gla_out_proj, v_ssd_in_proj, v_ssd_conv_w, v_ssd_conv_b, v_ssd_dt_bias, v_ssd_a_log, v_ssd_d, v_ssd_gate_norm, v_ssd_out_proj, v_final_norm):
    x0 = x[0]
    tgt = loss_target[0]
    n_gin = GLA_PROJ // N_DEV
    n_sin = SSD_PROJ // N_DEV
    n_up = GLA_DK // N_DEV
    n_cv = SSD_CONV_DIM // N_DEV

    big = [_rows(gla_in_proj[0]), _pad_rows(_rows(gla_gate_up[0]), 16), _rows(gla_out_proj[0]), _rows(ssd_in_proj[0]),
           _rows(ssd_out_proj[0])]
    big_rows = [a.shape[0] for a in big]
    wg = _all_gather(jnp.concatenate(big, axis=0).astype(BF16), name="gather_weights")
    small = [_rows(ssd_conv_w[0]), _rows(ssd_conv_b[0]), _rows(ssd_gate_norm[0])]
    small_rows = [a.shape[0] for a in small]
    ws = _all_gather(_pad_rows(jnp.concatenate(small, axis=0), 24), name="gather_small")

    def seg(g, rows, i):
        o = sum(rows[:i])
        return g[:, o:o + rows[i]]

    w_gin = _pad_cols(_from_col_shards(seg(wg, big_rows, 0), D_MODEL, n_gin), GLA_PROJ_PAD)
    wup = _pad_rows(_from_col_shards(seg(wg, big_rows, 1)[:, :GLA_RANK * n_up // 128], GLA_RANK, n_up), 128).astype(F32)
    w_gout = seg(wg, big_rows, 2).reshape(D_INNER, D_MODEL)
    w_sin = _pad_cols(_from_col_shards(seg(wg, big_rows, 3), D_MODEL, n_sin), SSD_PROJ_PAD)
    w_sout = seg(wg, big_rows, 4).reshape(D_INNER, D_MODEL)
    conv_w = _from_col_shards(seg(ws, small_rows, 0), SSD_CONV, n_cv)
    conv_b = seg(ws, small_rows, 1).reshape(1, SSD_CONV_DIM)
    gate_norm = seg(ws, small_rows, 2).reshape(1, D_INNER)
    vec128 = lambda a: _pad_cols(a.reshape(1, -1), 128)
    dtb, alog, dsk = vec128(ssd_dt_bias), vec128(ssd_a_log), vec128(ssd_d)
    nw0, nw1 = norm_w[0:1], norm_w[1:2]

    hn1 = _rms_fwd(x0, nw0, name="rms1_fwd")
    proj1 = _mm(hn1, w_gin, name="gla_in_proj")
    o, og, s_saved = _gla_layer_fwd(proj1, wup, gla_gate_bias, gla_head_norm)
    x1 = _mm(og, w_gout, add=x0, name="gla_out_proj")
    hn2 = _rms_fwd(x1, nw1, name="rms2_fwd")
    proj2 = _mm(hn2, w_sin, name="ssd_in_proj")
    xbc = _conv_fwd(proj2, conv_w, conv_b)
    y, h_saved = _ssd_layer_fwd(xbc, proj2, dtb, alog, dsk)
    yn = _ssd_post_fwd(y, proj2, gate_norm)
    x2 = _mm(yn, w_sout, add=x1, name="ssd_out_proj")
    lsum, dx2, d_final = _loss_bwd(x2, tgt, final_norm.reshape(1, D_MODEL))
    loss = lax.psum(lsum[0, 0], ("x", "y", "c"))

    d_sout = _mm_tn(yn, dx2, name="ssd_out_proj_dw")
    dyn = _mm(dx2, w_sout.T, name="ssd_out_proj_dx")
    dy, dz, d_gate_norm = _ssd_post_bwd(y, proj2, gate_norm, dyn)
    dxbc_post, ddt, d_dtb, d_alog, d_dsk = _ssd_layer_bwd(xbc, proj2, dtb, alog, dsk, h_saved, dy)
    dxbc, d_conv_w, d_conv_b = _conv_bwd(proj2, conv_w, conv_b, dxbc_post)
    dproj2 = jnp.concatenate([dz, dxbc, ddt], axis=1).astype(BF16)
    d_sin = _mm_tn(hn2, dproj2, name="ssd_in_proj_dw")
    dhn2 = _mm(dproj2, w_sin.T, name="ssd_in_proj_dx")
    dx1, d_nw1 = _rms_bwd(x1, nw1, dhn2, dx2, name="rms2_bwd")
    d_gout = _mm_tn(og, dx1, name="gla_out_proj_dw")
    dog = _mm(dx1, w_gout.T, name="gla_out_proj_dx")
    dproj1, d_wup, d_gbias, d_head_norm = _gla_layer_bwd(proj1, wup, gla_gate_bias, gla_head_norm, o, s_saved, dog)
    d_gin = _mm_tn(hn1, dproj1, name="gla_in_proj_dw")
    dhn1 = _mm(dproj1, w_gin.T, name="gla_in_proj_dx")
    dx0, d_nw0 = _rms_bwd(x0, nw0, dhn1, dx1, name="rms1_bwd")

    d_wup_full = d_wup[:GLA_RANK]
    sharded = [
        ("gla_in_proj", _col_shards(d_gin[:, :GLA_PROJ], n_gin)),
        ("gla_gate_up", _col_shards(d_wup_full, n_up)),
        ("gla_out_proj", d_gout.reshape(N_DEV, -1, 128)),
        ("ssd_in_proj", _col_shards(d_sin[:, :SSD_PROJ], n_sin)),
        ("ssd_conv_w", _col_shards(d_conv_w, n_cv)),
        ("ssd_conv_b", d_conv_b.reshape(N_DEV, -1, 128)),
        ("ssd_gate_norm", d_gate_norm.reshape(N_DEV, -1, 128)),
        ("ssd_out_proj", d_sout.reshape(N_DEV, -1, 128)),
    ]
    given = dict(
        gla_in_proj=(gla_in_proj, m_gla_in_proj, v_gla_in_proj), gla_gate_up=(gla_gate_up, m_gla_gate_up, v_gla_gate_up),
        gla_out_proj=(gla_out_proj, m_gla_out_proj, v_gla_out_proj), ssd_in_proj=(ssd_in_proj, m_ssd_in_proj, v_ssd_in_proj),
        ssd_conv_w=(ssd_conv_w, m_ssd_conv_w, v_ssd_conv_w), ssd_conv_b=(ssd_conv_b, m_ssd_conv_b, v_ssd_conv_b),
        ssd_gate_norm=(ssd_gate_norm, m_ssd_gate_norm, v_ssd_gate_norm), ssd_out_proj=(ssd_out_proj, m_ssd_out_proj, v_ssd_out_proj),
        norm_w=(norm_w, m_norm_w, v_norm_w), gla_gate_bias=(gla_gate_bias, m_gla_gate_bias, v_gla_gate_bias),
        gla_head_norm=(gla_head_norm, m_gla_head_norm, v_gla_head_norm), ssd_dt_bias=(ssd_dt_bias, m_ssd_dt_bias, v_ssd_dt_bias),
        ssd_a_log=(ssd_a_log, m_ssd_a_log, v_ssd_a_log), ssd_d=(ssd_d, m_ssd_d, v_ssd_d),
        final_norm=(final_norm, m_final_norm, v_final_norm),
    )
    results = {}

    def update(group, parts_rows, gather, name):
        rows = [p.shape[1] for _, p in group]
        total = -(-sum(rows) // parts_rows) * parts_rows
        parts = jnp.concatenate([p for _, p in group], axis=1)
        parts = jnp.pad(parts, ((0, 0), (0, total - parts.shape[1]), (0, 0)))
        if gather:
            recv = _all_gather(parts[0], name=name + "_gather")
        else:
            recv = _exchange(parts, name=name + "_exchange")

        def flat(i):
            rs = []
            for (nm, _), r in zip(group, rows):
                a = given[nm][i]
                a = a.reshape(1, -1)
                a = _pad_cols(a, r * 128).reshape(r, 128)
                rs.append(a)
            return _pad_rows(jnp.concatenate(rs, axis=0), total)

        outs = _adamw(recv, flat(0), flat(1), flat(2), name=name + "_adamw")
        off = 0
        for (nm, _), r in zip(group, rows):
            shape = given[nm][0].shape
            n = 1
            for d in shape:
                n *= d
            results[nm] = tuple(t[off:off + r].reshape(-1)[:n].reshape(shape) for t in outs)
            off += r

    update(sharded, 512, False, "sharded")
    replicated = [
        ("norm_w", jnp.concatenate([d_nw0, d_nw1], axis=0).reshape(1, -1, 128)),
        ("gla_gate_bias", d_gbias.reshape(1, -1, 128)),
        ("gla_head_norm", d_head_norm.reshape(1, -1, 128)),
        ("ssd_dt_bias", d_dtb.reshape(1, 1, 128)),
        ("ssd_a_log", d_alog.reshape(1, 1, 128)),
        ("ssd_d", d_dsk.reshape(1, 1, 128)),
        ("final_norm", d_final.reshape(1, -1, 128)),
    ]
    update(replicated, 8, True, "replicated")

    order = ["norm_w", "gla_in_proj", "gla_gate_up", "gla_gate_bias", "gla_head_norm", "gla_out_proj", "ssd_in_proj",
             "ssd_conv_w", "ssd_conv_b", "ssd_dt_bias", "ssd_a_log", "ssd_d", "ssd_gate_norm", "ssd_out_proj", "final_norm"]
    out = [loss, dx0[None]]
    for i in range(4):
        out += [results[n][i] for n in order]
    return tuple(out)
```

```python
import functools

import jax
import jax.numpy as jnp
from jax import lax
from jax.experimental import pallas as pl
from jax.experimental.pallas import tpu as pltpu

F32 = jnp.float32
BF16 = jnp.bfloat16
HI = lax.Precision.HIGHEST

D_MODEL = 1024
D_INNER = 2048
RMS_EPS = 1e-6
GLA_HEADS = 4
GLA_DK = 512
GLA_HEAD_K = 128
GLA_HEAD_V = 512
GLA_RANK = 16
GLA_NORMALIZER = 16.0
CHUNK = 64
SUB = 16
GLA_PROJ = 5136
GLA_PROJ_PAD = 5376
GLA_GK_COL = 5120
SSD_HEADS = 32
SSD_GROUPS = 8
SSD_HPG = 4
SSD_P = 64
SSD_N = 128
SSD_CONV = 4
SSD_CONV_DIM = 4096
SSD_PROJ = 6176
SSD_PROJ_PAD = 6400
SSD_DT_COL = 6144
N_DEV = 8

ADAM_LR = 0.001
ADAM_B1 = 0.9
ADAM_B2 = 0.999
ADAM_EPS = 1e-08
ADAM_WD = 0.01
ADAM_STEP = 10

VMEM_LIMIT = 56 * 1024 * 1024


def _dot(a, b, prec=None):
    return jnp.dot(a, b, preferred_element_type=F32, precision=prec)


def _dot_nt(a, b, prec=None):
    return lax.dot_general(a, b, (((1,), (1,)), ((), ())), preferred_element_type=F32, precision=prec)


def _dot_tn(a, b, prec=None):
    return lax.dot_general(a, b, (((0,), (0,)), ((), ())), preferred_element_type=F32, precision=prec)


def _bf(a):
    return a.astype(BF16)


@jax.custom_vjp
def _mxu(a, b):
    return _dot(_bf(a), _bf(b))


def _mxu_fwd(a, b):
    return _mxu(a, b), (a, b)


def _mxu_bwd(res, g):
    a, b = res
    return _dot_nt(_bf(g), _bf(b)), _dot_tn(_bf(a), _bf(g))


_mxu.defvjp(_mxu_fwd, _mxu_bwd)


@jax.custom_vjp
def _mxu_nt(a, b):
    return _dot_nt(_bf(a), _bf(b))


def _mxu_nt_fwd(a, b):
    return _mxu_nt(a, b), (a, b)


def _mxu_nt_bwd(res, g):
    a, b = res
    return _dot(_bf(g), _bf(b)), _dot_tn(_bf(g), _bf(a))


_mxu_nt.defvjp(_mxu_nt_fwd, _mxu_nt_bwd)


@jax.custom_vjp
def _mxu_tn(a, b):
    return _dot_tn(_bf(a), _bf(b))


def _mxu_tn_fwd(a, b):
    return _mxu_tn(a, b), (a, b)


def _mxu_tn_bwd(res, g):
    a, b = res
    return _dot_nt(_bf(b), _bf(g)), _dot(_bf(a), _bf(g))


_mxu_tn.defvjp(_mxu_tn_fwd, _mxu_tn_bwd)


def _silu(x):
    return x / (1.0 + jnp.exp(-x))


def _log_sigmoid(z):
    return jnp.minimum(z, 0.0) - jnp.log(1.0 + jnp.exp(-jnp.abs(z)))


def _softplus(z):
    return jnp.maximum(z, 0.0) + jnp.log(1.0 + jnp.exp(-jnp.abs(z)))


def _iota(shape, dim):
    return lax.broadcasted_iota(jnp.int32, shape, dim)


def _scan_rows(a, reverse, seg):
    n = a.shape[0]
    pos = _iota(a.shape, 0) & (seg - 1)
    sh = 1
    while sh < seg:
        if reverse:
            a = a + jnp.where(pos < seg - sh, pltpu.roll(a, n - sh, 0), 0.0)
        else:
            a = a + jnp.where(pos >= sh, pltpu.roll(a, sh, 0), 0.0)
        sh *= 2
    return a


def _make_cumsum(seg):
    @jax.custom_vjp
    def cumsum(a):
        return _scan_rows(a, False, seg)

    cumsum.defvjp(lambda a: (_scan_rows(a, False, seg), None), lambda _, g: (_scan_rows(g, True, seg),))
    return cumsum


_cumsum_sub = _make_cumsum(SUB)
_cumsum_rows = _make_cumsum(CHUNK)


def _split2(a):
    hi = _bf(a)
    return hi, _bf(a - hi.astype(F32))


def _three_pass(dot, a, b):
    ah, al = _split2(a)
    bh, bl = _split2(b)
    return dot(ah, bh) + (dot(ah, bl) + dot(al, bh))


@jax.custom_vjp
def _dot3_nt(a, b):
    return _three_pass(_dot_nt, a, b)


def _dot3_nt_fwd(a, b):
    return _dot3_nt(a, b), (a, b)


def _dot3_nt_bwd(res, g):
    a, b = res
    return _three_pass(_dot, g, b), _three_pass(_dot_tn, g, a)


_dot3_nt.defvjp(_dot3_nt_fwd, _dot3_nt_bwd)


def _gla_chunk(q, k, v, gk, wup, bias, St):
    nb = CHUNK // SUB
    z = _mxu(gk, wup) + bias
    la = _log_sigmoid(z) * (1.0 / GLA_NORMALIZER)
    qs = q * (GLA_HEAD_K ** -0.5)
    bl = _cumsum_sub(la)
    tot = [jnp.sum(la[i * SUB:(i + 1) * SUB], axis=0, keepdims=True) for i in range(nb)]
    pre = [jnp.zeros((1, GLA_HEAD_K), F32)]
    for i in range(nb):
        pre.append(pre[i] + tot[i])
    b_last = pre[nb]
    rows_of = lambda vals: jnp.concatenate([jnp.broadcast_to(t, (SUB, GLA_HEAD_K)) for t in vals], axis=0)
    suf = rows_of(tot) - bl
    nxt = rows_of(pre[1:])
    o = _mxu_nt(qs * jnp.exp(bl + rows_of(pre[:nb])), St)
    St_new = St * jnp.exp(b_last) + _mxu_tn(v, k * jnp.exp(suf + (b_last - nxt)))
    qa = qs * jnp.exp(bl)
    row = _iota((CHUNK, GLA_HEAD_K), 0)
    rs = _iota((SUB, GLA_HEAD_K), 0)
    cs = _iota((SUB, CHUNK), 1)
    a_rows = []
    for i in range(nb):
        sl = slice(i * SUB, (i + 1) * SUB)
        q_i, k_i, bl_i = qs[sl], k[sl], bl[sl]
        if i > 0:
            kp = jnp.where(row < i * SUB, k * jnp.exp(jnp.minimum(suf + (pre[i] - nxt), 0.0)), 0.0)
            a_i = _dot3_nt(qa[sl], kp)
        else:
            a_i = jnp.zeros((SUB, CHUNK), F32)
        for j in range(SUB):
            e = jnp.exp(jnp.minimum(bl_i - bl_i[j:j + 1], 0.0))
            t = jnp.where(rs >= j, q_i * e * k_i[j:j + 1], 0.0)
            a_i = a_i + jnp.where(cs == i * SUB + j, jnp.sum(t, axis=-1, keepdims=True), 0.0)
        a_rows.append(a_i)
    o = o + _mxu(jnp.concatenate(a_rows, axis=0), v)
    return o, St_new


def _cparams(sem):
    return pltpu.CompilerParams(dimension_semantics=sem, vmem_limit_bytes=VMEM_LIMIT)


def _gla_fwd(proj, wup, bias):
    L = proj.shape[0]
    nc = L // CHUNK

    def body(q_ref, k_ref, v_ref, gk_ref, wup_ref, b_ref, o_ref, s_ref, st):
        h = pl.program_id(1)

        @pl.when(pl.program_id(0) == 0)
        def _():
            st[h] = jnp.zeros((GLA_HEAD_V, GLA_HEAD_K), F32)

        s_in = st[h]
        s_ref[...] = s_in
        o, s_new = _gla_chunk(q_ref[...], k_ref[...], v_ref[...], gk_ref[...], wup_ref[...], b_ref[...], s_in)
        o_ref[...] = o
        st[h] = s_new

    return pl.pallas_call(
        body,
        grid=(nc, GLA_HEADS),
        in_specs=[
            pl.BlockSpec((CHUNK, GLA_HEAD_K), lambda c, h: (c, h)),
            pl.BlockSpec((CHUNK, GLA_HEAD_K), lambda c, h: (c, GLA_HEADS + h)),
            pl.BlockSpec((CHUNK, GLA_HEAD_V), lambda c, h: (c, 2 + h)),
            pl.BlockSpec((CHUNK, 128), lambda c, h: (c, GLA_GK_COL // 128)),
            pl.BlockSpec((128, GLA_HEAD_K), lambda c, h: (0, h)),
            pl.BlockSpec((1, GLA_HEAD_K), lambda c, h: (0, h)),
        ],
        out_specs=[
            pl.BlockSpec((CHUNK, GLA_HEAD_V), lambda c, h: (c, h)),
            pl.BlockSpec((None, None, GLA_HEAD_V, GLA_HEAD_K), lambda c, h: (c, h, 0, 0)),
        ],
        out_shape=[
            jax.ShapeDtypeStruct((L, D_INNER), F32),
            jax.ShapeDtypeStruct((nc, GLA_HEADS, GLA_HEAD_V, GLA_HEAD_K), F32),
        ],
        scratch_shapes=[pltpu.VMEM((GLA_HEADS, GLA_HEAD_V, GLA_HEAD_K), F32)],
        compiler_params=_cparams(("arbitrary", "arbitrary")),
        name="gla_fwd",
    )(proj, proj, proj, proj, wup, bias)


def _gla_bwd(proj, wup, bias, s_in, do):
    L = proj.shape[0]
    nc = L // CHUNK

    def body(q_ref, k_ref, v_ref, gk_ref, wup_ref, b_ref, s_ref, do_ref,
             dq_ref, dk_ref, dv_ref, dgk_ref, dwup_ref, db_ref, dst):
        h = pl.program_id(1)
        first = pl.program_id(0) == 0

        @pl.when(first)
        def _():
            dst[h] = jnp.zeros((GLA_HEAD_V, GLA_HEAD_K), F32)
            dwup_ref[h] = jnp.zeros((128, GLA_HEAD_K), F32)
            db_ref[h] = jnp.zeros((1, GLA_HEAD_K), F32)

        _, vjp = jax.vjp(_gla_chunk, q_ref[...], k_ref[...], v_ref[...], gk_ref[...], wup_ref[...], b_ref[...], s_ref[...])
        dq, dk, dv, dgk, dwup, db, ds = vjp((do_ref[...], dst[h]))
        dq_ref[...] = dq
        dk_ref[...] = dk
        dv_ref[...] = dv
        dst[h] = ds
        dwup_ref[h] += dwup
        db_ref[h] += db

        @pl.when(h == 0)
        def _():
            dgk_ref[...] = dgk

        @pl.when(h != 0)
        def _():
            dgk_ref[...] += dgk

    rc = lambda c: nc - 1 - c
    return pl.pallas_call(
        body,
        grid=(nc, GLA_HEADS),
        in_specs=[
            pl.BlockSpec((CHUNK, GLA_HEAD_K), lambda c, h: (rc(c), h)),
            pl.BlockSpec((CHUNK, GLA_HEAD_K), lambda c, h: (rc(c), GLA_HEADS + h)),
            pl.BlockSpec((CHUNK, GLA_HEAD_V), lambda c, h: (rc(c), 2 + h)),
            pl.BlockSpec((CHUNK, 128), lambda c, h: (rc(c), GLA_GK_COL // 128)),
            pl.BlockSpec((128, GLA_HEAD_K), lambda c, h: (0, h)),
            pl.BlockSpec((1, GLA_HEAD_K), lambda c, h: (0, h)),
            pl.BlockSpec((None, None, GLA_HEAD_V, GLA_HEAD_K), lambda c, h: (rc(c), h, 0, 0)),
            pl.BlockSpec((CHUNK, GLA_HEAD_V), lambda c, h: (rc(c), h)),
        ],
        out_specs=[
            pl.BlockSpec((CHUNK, GLA_HEAD_K), lambda c, h: (rc(c), h)),
            pl.BlockSpec((CHUNK, GLA_HEAD_K), lambda c, h: (rc(c), h)),
            pl.BlockSpec((CHUNK, GLA_HEAD_V), lambda c, h: (rc(c), h)),
            pl.BlockSpec((CHUNK, 128), lambda c, h: (rc(c), 0)),
            pl.BlockSpec((GLA_HEADS, 128, GLA_HEAD_K), lambda c, h: (0, 0, 0)),
            pl.BlockSpec((GLA_HEADS, 1, GLA_HEAD_K), lambda c, h: (0, 0, 0)),
        ],
        out_shape=[
            jax.ShapeDtypeStruct((L, GLA_DK), F32),
            jax.ShapeDtypeStruct((L, GLA_DK), F32),
            jax.ShapeDtypeStruct((L, D_INNER), F32),
            jax.ShapeDtypeStruct((L, 128), F32),
            jax.ShapeDtypeStruct((GLA_HEADS, 128, GLA_HEAD_K), F32),
            jax.ShapeDtypeStruct((GLA_HEADS, 1, GLA_HEAD_K), F32),
        ],
        scratch_shapes=[pltpu.VMEM((GLA_HEADS, GLA_HEAD_V, GLA_HEAD_K), F32)],
        compiler_params=_cparams(("arbitrary", "arbitrary")),
        name="gla_bwd",
    )(proj, proj, proj, proj, wup, bias, s_in, do)


def _ssd_chunk(g, xs, Bm, Cm, dtp, dtb, alog, dsk, h_in):
    gw = SSD_HPG * SSD_P
    dt = _softplus(dtp + dtb)
    a = dt * (-jnp.exp(alog))
    r = _iota((CHUNK, CHUNK), 0)
    c = _iota((CHUNK, CHUNK), 1)
    tri = jnp.where(c <= r, 1.0, 0.0).astype(F32)
    acum = _dot(tri, a, HI)
    a_last = jnp.sum(a, axis=0, keepdims=True)
    e_g = jnp.where(_iota((128, gw), 0) == SSD_HPG * g + _iota((128, gw), 1) // SSD_P, 1.0, 0.0).astype(F32)
    dt_b = _dot(dt, e_g, HI)
    ac_b = _dot(acum, e_g, HI)
    al_b = _dot(jnp.broadcast_to(a_last, (8, 128)), e_g, HI)[0:1]
    d_b = _dot(jnp.broadcast_to(dsk, (8, 128)), e_g, HI)[0:1]
    xdt = xs * dt_b
    cb = _mxu_nt(Cm, Bm)
    lane_head = _iota((CHUNK, gw), 1) // SSD_P
    y = jnp.zeros((CHUNK, gw), F32)
    for j in range(SSD_HPG):
        eh = jnp.where(_iota((128, CHUNK), 0) == SSD_HPG * g + j, 1.0, 0.0).astype(F32)
        eht = jnp.where(_iota((CHUNK, 128), 1) == SSD_HPG * g + j, 1.0, 0.0).astype(F32)
        ac_c = _dot(acum, eh, HI)
        ac_r = _dot_nt(eht, acum, HI)
        lmat = jnp.where(c <= r, jnp.exp(jnp.minimum(ac_c - ac_r, 0.0)), 0.0)
        yj = _mxu(cb * lmat, xdt)
        y = y + jnp.where(lane_head == j, yj, 0.0)
    st = _mxu_tn(xdt * jnp.exp(al_b - ac_b), Bm)
    e_r = jnp.where(_iota((gw, 128), 1) == SSD_HPG * g + _iota((gw, 128), 0) // SSD_P, 1.0, 0.0).astype(F32)
    cd_rows = _dot_nt(e_r, jnp.broadcast_to(jnp.exp(a_last), (SSD_N, 128)), HI)
    h_out = cd_rows * h_in + st
    y = y + _mxu_nt(Cm, h_in) * jnp.exp(ac_b) + xs * d_b
    return y, h_out


def _ssd_fwd(xbc, proj, dtb, alog, dsk):
    L = xbc.shape[0]
    nc = L // CHUNK
    gw = SSD_HPG * SSD_P

    def body(xs_ref, b_ref, c_ref, dt_ref, dtb_ref, alog_ref, dsk_ref, y_ref, hs_ref, hst):
        g = pl.program_id(1)

        @pl.when(pl.program_id(0) == 0)
        def _():
            hst[g] = jnp.zeros((gw, SSD_N), F32)

        h_in = hst[g]
        hs_ref[...] = h_in
        y, h_out = _ssd_chunk(g, xs_ref[...], b_ref[...], c_ref[...], dt_ref[...], dtb_ref[...], alog_ref[...], dsk_ref[...], h_in)
        y_ref[...] = y
        hst[g] = h_out

    vec = pl.BlockSpec((1, 128), lambda c, g: (0, 0))
    return pl.pallas_call(
        body,
        grid=(nc, SSD_GROUPS),
        in_specs=[
            pl.BlockSpec((CHUNK, gw), lambda c, g: (c, g)),
            pl.BlockSpec((CHUNK, SSD_N), lambda c, g: (c, D_INNER // SSD_N + g)),
            pl.BlockSpec((CHUNK, SSD_N), lambda c, g: (c, D_INNER // SSD_N + SSD_GROUPS + g)),
            pl.BlockSpec((CHUNK, 128), lambda c, g: (c, SSD_DT_COL // 128)),
            vec, vec, vec,
        ],
        out_specs=[
            pl.BlockSpec((CHUNK, gw), lambda c, g: (c, g)),
            pl.BlockSpec((None, None, gw, SSD_N), lambda c, g: (c, g, 0, 0)),
        ],
        out_shape=[
            jax.ShapeDtypeStruct((L, D_INNER), F32),
            jax.ShapeDtypeStruct((nc, SSD_GROUPS, gw, SSD_N), F32),
        ],
        scratch_shapes=[pltpu.VMEM((SSD_GROUPS, gw, SSD_N), F32)],
        compiler_params=_cparams(("arbitrary", "arbitrary")),
        name="ssd_fwd",
    )(xbc, xbc, xbc, proj, dtb, alog, dsk)


def _ssd_bwd(xbc, proj, dtb, alog, dsk, h_saved, dy):
    L = xbc.shape[0]
    nc = L // CHUNK
    gw = SSD_HPG * SSD_P

    def body(xs_ref, b_ref, c_ref, dt_ref, dtb_ref, alog_ref, dsk_ref, hs_ref, dy_ref,
             dxs_ref, db_ref, dc_ref, ddt_ref, ddtb_ref, dalog_ref, ddsk_ref, dhst):
        g = pl.program_id(1)
        first = pl.program_id(0) == 0

        @pl.when(first)
        def _():
            dhst[g] = jnp.zeros((gw, SSD_N), F32)

        @pl.when(first & (g == 0))
        def _():
            ddtb_ref[...] = jnp.zeros((1, 128), F32)
            dalog_ref[...] = jnp.zeros((1, 128), F32)
            ddsk_ref[...] = jnp.zeros((1, 128), F32)

        _, vjp = jax.vjp(functools.partial(_ssd_chunk, g), xs_ref[...], b_ref[...], c_ref[...], dt_ref[...],
                         dtb_ref[...], alog_ref[...], dsk_ref[...], hs_ref[...])
        dxs, db, dc, ddt, ddtb, dalog, ddsk, dh = vjp((dy_ref[...], dhst[g]))
        dxs_ref[...] = dxs
        db_ref[...] = db
        dc_ref[...] = dc
        dhst[g] = dh
        ddtb_ref[...] += ddtb
        dalog_ref[...] += dalog
        ddsk_ref[...] += ddsk

        @pl.when(g == 0)
        def _():
            ddt_ref[...] = ddt

        @pl.when(g != 0)
        def _():
            ddt_ref[...] += ddt

    rc = lambda c: nc - 1 - c
    vec = pl.BlockSpec((1, 128), lambda c, g: (0, 0))
    vshape = jax.ShapeDtypeStruct((1, 128), F32)
    return pl.pallas_call(
        body,
        grid=(nc, SSD_GROUPS),
        in_specs=[
            pl.BlockSpec((CHUNK, gw), lambda c, g: (rc(c), g)),
            pl.BlockSpec((CHUNK, SSD_N), lambda c, g: (rc(c), D_INNER // SSD_N + g)),
            pl.BlockSpec((CHUNK, SSD_N), lambda c, g: (rc(c), D_INNER // SSD_N + SSD_GROUPS + g)),
            pl.BlockSpec((CHUNK, 128), lambda c, g: (rc(c), SSD_DT_COL // 128)),
            vec, vec, vec,
            pl.BlockSpec((None, None, gw, SSD_N), lambda c, g: (rc(c), g, 0, 0)),
            pl.BlockSpec((CHUNK, gw), lambda c, g: (rc(c), g)),
        ],
        out_specs=[
            pl.BlockSpec((CHUNK, gw), lambda c, g: (rc(c), g)),
            pl.BlockSpec((CHUNK, SSD_N), lambda c, g: (rc(c), g)),
            pl.BlockSpec((CHUNK, SSD_N), lambda c, g: (rc(c), g)),
            pl.BlockSpec((CHUNK, 128), lambda c, g: (rc(c), 0)),
            vec, vec, vec,
        ],
        out_shape=[
            jax.ShapeDtypeStruct((L, D_INNER), F32),
            jax.ShapeDtypeStruct((L, SSD_GROUPS * SSD_N), F32),
            jax.ShapeDtypeStruct((L, SSD_GROUPS * SSD_N), F32),
            jax.ShapeDtypeStruct((L, 128), F32),
            vshape, vshape, vshape,
        ],
        scratch_shapes=[pltpu.VMEM((SSD_GROUPS, gw, SSD_N), F32)],
        compiler_params=_cparams(("arbitrary", "arbitrary")),
        name="ssd_bwd",
    )(xbc, xbc, xbc, proj, dtb, alog, dsk, h_saved, dy)


def _rms(x, w):
    return x * lax.rsqrt(jnp.mean(x * x, axis=-1, keepdims=True) + RMS_EPS) * w


def _gla_post(o, g, wn):
    return _rms(o, wn) * _silu(g)


GLA_HALF = 2 * GLA_HEAD_V


def _gla_specs(nc, rev):
    ci = (lambda c: nc - 1 - c) if rev else (lambda c: c)
    v0 = 2 * GLA_DK // GLA_HALF
    g0 = (2 * GLA_DK + D_INNER) // GLA_HALF
    return [
        pl.BlockSpec((CHUNK, GLA_DK), lambda c: (ci(c), 0)),
        pl.BlockSpec((CHUNK, GLA_DK), lambda c: (ci(c), 1)),
        pl.BlockSpec((CHUNK, GLA_HALF), lambda c: (ci(c), v0)),
        pl.BlockSpec((CHUNK, GLA_HALF), lambda c: (ci(c), v0 + 1)),
        pl.BlockSpec((CHUNK, GLA_HALF), lambda c: (ci(c), g0)),
        pl.BlockSpec((CHUNK, GLA_HALF), lambda c: (ci(c), g0 + 1)),
        pl.BlockSpec((CHUNK, 128), lambda c: (ci(c), GLA_GK_COL // 128)),
        pl.BlockSpec((128, GLA_DK), lambda c: (0, 0)),
        pl.BlockSpec((1, GLA_DK), lambda c: (0, 0)),
        pl.BlockSpec((1, GLA_HEAD_V), lambda c: (0, 0)),
    ]


def _head_cols(ref_a, ref_b, h):
    ref = ref_a if h < 2 else ref_b
    return ref[:, (h % 2) * GLA_HEAD_V:(h % 2 + 1) * GLA_HEAD_V]


def _gla_layer_fwd(proj, wup, bias, wn):
    L = proj.shape[0]
    nc = L // CHUNK

    def body(q_ref, k_ref, va_ref, vb_ref, ga_ref, gb_ref, gk_ref, wup_ref, b_ref, wn_ref, o_ref, og_ref, s_ref, st):
        @pl.when(pl.program_id(0) == 0)
        def _():
            st[...] = jnp.zeros(st.shape, F32)

        gk = gk_ref[...]
        for h in range(GLA_HEADS):
            kc = slice(h * GLA_HEAD_K, (h + 1) * GLA_HEAD_K)
            vc = slice(h * GLA_HEAD_V, (h + 1) * GLA_HEAD_V)
            s_in = st[h]
            s_ref[h] = s_in
            o, s_new = _gla_chunk(q_ref[:, kc], k_ref[:, kc], _head_cols(va_ref, vb_ref, h), gk, wup_ref[:, kc], b_ref[:, kc], s_in)
            st[h] = s_new
            o_ref[:, vc] = o
            og_ref[:, vc] = _gla_post(o, _head_cols(ga_ref, gb_ref, h), wn_ref[...]).astype(BF16)

    return pl.pallas_call(
        body,
        grid=(nc,),
        in_specs=_gla_specs(nc, False),
        out_specs=[
            pl.BlockSpec((CHUNK, D_INNER), lambda c: (c, 0)),
            pl.BlockSpec((CHUNK, D_INNER), lambda c: (c, 0)),
            pl.BlockSpec((None, GLA_HEADS, GLA_HEAD_V, GLA_HEAD_K), lambda c: (c, 0, 0, 0)),
        ],
        out_shape=[
            jax.ShapeDtypeStruct((L, D_INNER), F32),
            jax.ShapeDtypeStruct((L, D_INNER), BF16),
            jax.ShapeDtypeStruct((nc, GLA_HEADS, GLA_HEAD_V, GLA_HEAD_K), F32),
        ],
        scratch_shapes=[pltpu.VMEM((GLA_HEADS, GLA_HEAD_V, GLA_HEAD_K), F32)],
        compiler_params=_cparams(("arbitrary",)),
        name="gla_layer_fwd",
    )(proj, proj, proj, proj, proj, proj, proj, wup, bias, wn)


def _gla_layer_bwd(proj, wup, bias, wn, o, s_in, dog):
    L = proj.shape[0]
    nc = L // CHUNK

    def body(q_ref, k_ref, va_ref, vb_ref, ga_ref, gb_ref, gk_ref, wup_ref, b_ref, wn_ref, o_ref, s_ref, dog_ref,
             dp_ref, dwup_ref, db_ref, dwn_ref, dst):
        @pl.when(pl.program_id(0) == 0)
        def _():
            dst[...] = jnp.zeros(dst.shape, F32)
            dwup_ref[...] = jnp.zeros(dwup_ref.shape, F32)
            db_ref[...] = jnp.zeros(db_ref.shape, F32)
            dwn_ref[...] = jnp.zeros(dwn_ref.shape, F32)

        gk = gk_ref[...]
        dgk_sum = jnp.zeros((CHUNK, 128), F32)
        for h in range(GLA_HEADS):
            kc = slice(h * GLA_HEAD_K, (h + 1) * GLA_HEAD_K)
            vc = slice(h * GLA_HEAD_V, (h + 1) * GLA_HEAD_V)
            _, post_vjp = jax.vjp(_gla_post, o_ref[:, vc], _head_cols(ga_ref, gb_ref, h), wn_ref[...])
            do, dg, dwn = post_vjp(dog_ref[:, vc])
            _, vjp = jax.vjp(_gla_chunk, q_ref[:, kc], k_ref[:, kc], _head_cols(va_ref, vb_ref, h), gk, wup_ref[:, kc],
                             b_ref[:, kc], s_ref[h])
            dq, dk, dv, dgk, dwup, db, ds = vjp((do, dst[h]))
            dst[h] = ds
            dp_ref[:, kc] = dq.astype(BF16)
            dp_ref[:, GLA_DK + h * GLA_HEAD_K:GLA_DK + (h + 1) * GLA_HEAD_K] = dk.astype(BF16)
            dp_ref[:, 2 * GLA_DK + h * GLA_HEAD_V:2 * GLA_DK + (h + 1) * GLA_HEAD_V] = dv.astype(BF16)
            dp_ref[:, 2 * GLA_DK + D_INNER + h * GLA_HEAD_V:2 * GLA_DK + D_INNER + (h + 1) * GLA_HEAD_V] = dg.astype(BF16)
            dwup_ref[:, kc] += dwup
            db_ref[:, kc] += db
            dwn_ref[...] += dwn
            dgk_sum = dgk_sum + dgk
        dp_ref[:, GLA_GK_COL:GLA_GK_COL + 128] = dgk_sum.astype(BF16)
        dp_ref[:, GLA_GK_COL + 128:] = jnp.zeros((CHUNK, GLA_PROJ_PAD - GLA_GK_COL - 128), BF16)

    rc = lambda c: nc - 1 - c
    return pl.pallas_call(
        body,
        grid=(nc,),
        in_specs=_gla_specs(nc, True) + [
            pl.BlockSpec((CHUNK, D_INNER), lambda c: (rc(c), 0)),
            pl.BlockSpec((None, GLA_HEADS, GLA_HEAD_V, GLA_HEAD_K), lambda c: (rc(c), 0, 0, 0)),
            pl.BlockSpec((CHUNK, D_INNER), lambda c: (rc(c), 0)),
        ],
        out_specs=[
            pl.BlockSpec((CHUNK, GLA_PROJ_PAD), lambda c: (rc(c), 0)),
            pl.BlockSpec((128, GLA_DK), lambda c: (0, 0)),
            pl.BlockSpec((1, GLA_DK), lambda c: (0, 0)),
            pl.BlockSpec((1, GLA_HEAD_V), lambda c: (0, 0)),
        ],
        out_shape=[
            jax.ShapeDtypeStruct((L, GLA_PROJ_PAD), BF16),
            jax.ShapeDtypeStruct((128, GLA_DK), F32),
            jax.ShapeDtypeStruct((1, GLA_DK), F32),
            jax.ShapeDtypeStruct((1, GLA_HEAD_V), F32),
        ],
        scratch_shapes=[pltpu.VMEM((GLA_HEADS, GLA_HEAD_V, GLA_HEAD_K), F32)],
        compiler_params=_cparams(("arbitrary",)),
        name="gla_layer_bwd",
    )(proj, proj, proj, proj, proj, proj, proj, wup, bias, wn, o, s_in, dog)


@jax.custom_vjp
def _expand(v):
    r = v.shape[0]
    left = _iota((r, 128), 1) < SSD_P
    slabs = []
    for p in range(SSD_HEADS // 2):
        a = jnp.broadcast_to(v[:, 2 * p:2 * p + 1], (r, 128))
        b = jnp.broadcast_to(v[:, 2 * p + 1:2 * p + 2], (r, 128))
        slabs.append(jnp.where(left, a, b))
    return jnp.concatenate(slabs, axis=1)


def _expand_fwd(v):
    return _expand(v), None


def _expand_bwd(_, g):
    r = g.shape[0]
    lane = _iota((r, 128), 1)
    left = lane < SSD_P
    dv = jnp.zeros((r, 128), F32)
    for p in range(SSD_HEADS // 2):
        gs = g[:, 128 * p:128 * (p + 1)]
        sa = jnp.sum(jnp.where(left, gs, 0.0), axis=-1, keepdims=True)
        sb = jnp.sum(jnp.where(left, 0.0, gs), axis=-1, keepdims=True)
        dv = dv + jnp.where(lane == 2 * p, sa, 0.0) + jnp.where(lane == 2 * p + 1, sb, 0.0)
    return (dv,)


_expand.defvjp(_expand_fwd, _expand_bwd)


SSD_GW = SSD_HPG * SSD_P
SSD_BC = SSD_GROUPS * SSD_N


def _ssd_chunk_all(xs, Bm, Cm, dtp, dtb, alog, dsk, h_in):
    dt = _softplus(dtp + dtb)
    acum = _cumsum_rows(dt * (-jnp.exp(alog)))
    a_last = acum[CHUNK - 1:CHUNK]
    acum_b = _expand(acum)
    w_end = _expand(dt * jnp.exp(a_last - acum))
    d_b = _expand(jnp.broadcast_to(dsk, (8, 128)))[0:1]
    ac_t = jnp.concatenate([acum, acum], axis=0).T
    dt_t = jnp.concatenate([dt, dt], axis=0).T
    lane = _iota((CHUNK, 128), 1)
    left = lane < SSD_P
    causal = (lane & (SSD_P - 1)) <= _iota((CHUNK, 128), 0)
    ys, sts = [], []
    for g in range(SSD_GROUPS):
        Bg = Bm[:, g * SSD_N:(g + 1) * SSD_N]
        Cg = Cm[:, g * SSD_N:(g + 1) * SSD_N]
        cb2 = _mxu_nt(Cg, jnp.concatenate([Bg, Bg], axis=0))
        y_off = _mxu_nt(Cg, h_in[g * SSD_GW:(g + 1) * SSD_GW])
        for k in range(SSD_HPG // 2):
            p = g * (SSD_HPG // 2) + k
            sl = slice(128 * p, 128 * (p + 1))
            ac_c = acum_b[:, sl]
            ac_r = jnp.where(left, ac_t[2 * p:2 * p + 1], ac_t[2 * p + 1:2 * p + 2])
            dt_r = jnp.where(left, dt_t[2 * p:2 * p + 1], dt_t[2 * p + 1:2 * p + 2])
            m2 = cb2 * jnp.where(causal, jnp.exp(jnp.minimum(ac_c - ac_r, 0.0)), 0.0) * dt_r
            xsl = xs[:, sl]
            x2 = jnp.concatenate([jnp.where(left, xsl, 0.0), jnp.where(left, 0.0, xsl)], axis=0)
            ys.append(_mxu(m2, x2) + y_off[:, 128 * k:128 * (k + 1)] * jnp.exp(ac_c) + xsl * d_b[:, sl])
        gs = slice(g * SSD_GW, (g + 1) * SSD_GW)
        sts.append(_mxu_tn(xs[:, gs] * w_end[:, gs], Bg))
    cd = jnp.exp(ac_t[:, CHUNK - 1:CHUNK])
    hs = [h_in[h * SSD_P:(h + 1) * SSD_P] * cd[h:h + 1] for h in range(SSD_HEADS)]
    h_out = jnp.concatenate(hs, axis=0) + jnp.concatenate(sts, axis=0)
    return jnp.concatenate(ys, axis=1), h_out


def _ssd_specs(nc, rev):
    ci = (lambda c: nc - 1 - c) if rev else (lambda c: c)
    vec = pl.BlockSpec((1, 128), lambda c: (0, 0))
    return [
        pl.BlockSpec((CHUNK, D_INNER), lambda c: (ci(c), 0)),
        pl.BlockSpec((CHUNK, SSD_BC), lambda c: (ci(c), D_INNER // SSD_BC)),
        pl.BlockSpec((CHUNK, SSD_BC), lambda c: (ci(c), D_INNER // SSD_BC + 1)),
        pl.BlockSpec((CHUNK, 128), lambda c: (ci(c), SSD_DT_COL // 128)),
        vec, vec, vec,
    ]


def _ssd_layer_fwd(xbc, proj, dtb, alog, dsk):
    L = xbc.shape[0]
    nc = L // CHUNK

    def body(xs_ref, b_ref, c_ref, dt_ref, dtb_ref, alog_ref, dsk_ref, y_ref, hs_ref, hst):
        @pl.when(pl.program_id(0) == 0)
        def _():
            hst[...] = jnp.zeros(hst.shape, F32)

        h_in = hst[...]
        hs_ref[...] = h_in
        y, h_out = _ssd_chunk_all(xs_ref[...], b_ref[...], c_ref[...], dt_ref[...], dtb_ref[...], alog_ref[...], dsk_ref[...], h_in)
        y_ref[...] = y
        hst[...] = h_out

    return pl.pallas_call(
        body,
        grid=(nc,),
        in_specs=_ssd_specs(nc, False),
        out_specs=[
            pl.BlockSpec((CHUNK, D_INNER), lambda c: (c, 0)),
            pl.BlockSpec((None, D_INNER, SSD_N), lambda c: (c, 0, 0)),
        ],
        out_shape=[
            jax.ShapeDtypeStruct((L, D_INNER), F32),
            jax.ShapeDtypeStruct((nc, D_INNER, SSD_N), F32),
        ],
        scratch_shapes=[pltpu.VMEM((D_INNER, SSD_N), F32)],
        compiler_params=_cparams(("arbitrary",)),
        name="ssd_layer_fwd",
    )(xbc, xbc, xbc, proj, dtb, alog, dsk)


def _ssd_layer_bwd(xbc, proj, dtb, alog, dsk, h_saved, dy):
    L = xbc.shape[0]
    nc = L // CHUNK

    def body(xs_ref, b_ref, c_ref, dt_ref, dtb_ref, alog_ref, dsk_ref, hs_ref, dy_ref,
             dx_ref, ddt_ref, ddtb_ref, dalog_ref, ddsk_ref, dhst):
        @pl.when(pl.program_id(0) == 0)
        def _():
            dhst[...] = jnp.zeros(dhst.shape, F32)
            ddtb_ref[...] = jnp.zeros((1, 128), F32)
            dalog_ref[...] = jnp.zeros((1, 128), F32)
            ddsk_ref[...] = jnp.zeros((1, 128), F32)

        _, vjp = jax.vjp(_ssd_chunk_all, xs_ref[...], b_ref[...], c_ref[...], dt_ref[...], dtb_ref[...], alog_ref[...],
                         dsk_ref[...], hs_ref[...])
        dxs, db, dc, ddt, ddtb, dalog, ddsk, dh = vjp((dy_ref[...], dhst[...]))
        dx_ref[:, :D_INNER] = dxs
        dx_ref[:, D_INNER:D_INNER + SSD_BC] = db
        dx_ref[:, D_INNER + SSD_BC:] = dc
        ddt_ref[...] = ddt
        dhst[...] = dh
        ddtb_ref[...] += ddtb
        dalog_ref[...] += dalog
        ddsk_ref[...] += ddsk

    rc = lambda c: nc - 1 - c
    vec = pl.BlockSpec((1, 128), lambda c: (0, 0))
    vshape = jax.ShapeDtypeStruct((1, 128), F32)
    return pl.pallas_call(
        body,
        grid=(nc,),
        in_specs=_ssd_specs(nc, True) + [
            pl.BlockSpec((None, D_INNER, SSD_N), lambda c: (rc(c), 0, 0)),
            pl.BlockSpec((CHUNK, D_INNER), lambda c: (rc(c), 0)),
        ],
        out_specs=[
            pl.BlockSpec((CHUNK, SSD_CONV_DIM), lambda c: (rc(c), 0)),
            pl.BlockSpec((CHUNK, 128), lambda c: (rc(c), 0)),
            vec, vec, vec,
        ],
        out_shape=[
            jax.ShapeDtypeStruct((L, SSD_CONV_DIM), F32),
            jax.ShapeDtypeStruct((L, 128), F32),
            vshape, vshape, vshape,
        ],
        scratch_shapes=[pltpu.VMEM((D_INNER, SSD_N), F32)],
        compiler_params=_cparams(("arbitrary",)),
        name="ssd_layer_bwd",
    )(xbc, xbc, xbc, proj, dtb, alog, dsk, h_saved, dy)


def _pick(n, options):
    for t in options:
        if n % t == 0:
            return t
    return n


TOKEN_TILE = 512
MM_TOKEN_TILE = 1024
MM_TILES = (1792, 1280, 1024, 768, 512, 256, 128)


def _mm(a, b, *, name, out_dtype=F32, add=None):
    M, K = a.shape
    N = b.shape[1]
    tm = min(MM_TOKEN_TILE, M)
    tn = _pick(N, MM_TILES)
    tk = _pick(K, MM_TILES)
    nk = K // tk

    def body(*refs):
        if add is None:
            a_ref, b_ref, o_ref, acc = refs
        else:
            a_ref, b_ref, add_ref, o_ref, acc = refs
        k = pl.program_id(2)
        p = _dot(_bf(a_ref[...]), _bf(b_ref[...]))

        def finish(r):
            if add is not None:
                r = r + add_ref[...]
            o_ref[...] = r.astype(out_dtype)

        if nk == 1:
            finish(p)
            return

        @pl.when(k == 0)
        def _():
            acc[...] = p

        @pl.when((k > 0) & (k < nk - 1))
        def _():
            acc[...] += p

        @pl.when(k == nk - 1)
        def _():
            finish(acc[...] + p)

    in_specs = [pl.BlockSpec((tm, tk), lambda i, j, k: (i, k)), pl.BlockSpec((tk, tn), lambda i, j, k: (k, j))]
    args = [a, b]
    if add is not None:
        in_specs.append(pl.BlockSpec((tm, tn), lambda i, j, k: (i, j)))
        args.append(add)
    return pl.pallas_call(
        body,
        grid=(M // tm, N // tn, nk),
        in_specs=in_specs,
        out_specs=pl.BlockSpec((tm, tn), lambda i, j, k: (i, j)),
        out_shape=jax.ShapeDtypeStruct((M, N), out_dtype),
        scratch_shapes=[pltpu.VMEM((tm, tn), F32)],
        compiler_params=_cparams(("parallel", "parallel", "arbitrary")),
        name=name,
    )(*args)


def _mm_tn(a, b, *, name):
    M, K = a.shape
    N = b.shape[1]
    tm = min(MM_TOKEN_TILE, M)
    tn = _pick(N, MM_TILES)

    def body(a_ref, b_ref, o_ref):
        p = _dot_tn(_bf(a_ref[...]), _bf(b_ref[...]))

        @pl.when(pl.program_id(1) == 0)
        def _():
            o_ref[...] = p

        @pl.when(pl.program_id(1) > 0)
        def _():
            o_ref[...] += p

    return pl.pallas_call(
        body,
        grid=(N // tn, M // tm),
        in_specs=[pl.BlockSpec((tm, K), lambda j, i: (i, 0)), pl.BlockSpec((tm, tn), lambda j, i: (i, j))],
        out_specs=pl.BlockSpec((K, tn), lambda j, i: (0, j)),
        out_shape=jax.ShapeDtypeStruct((K, N), F32),
        compiler_params=_cparams(("parallel", "arbitrary")),
        name=name,
    )(a, b)


def _rms(x, w):
    return x * lax.rsqrt(jnp.mean(x * x, axis=-1, keepdims=True) + RMS_EPS) * w


def _acc_out(ref, val, first):
    @pl.when(first)
    def _():
        ref[...] = val

    @pl.when(jnp.logical_not(first))
    def _():
        ref[...] += val


def _rms_fwd(x, w, *, name):
    L, D = x.shape
    tm = min(TOKEN_TILE, L)

    def body(x_ref, w_ref, o_ref):
        o_ref[...] = _rms(x_ref[...], w_ref[...]).astype(BF16)

    return pl.pallas_call(
        body, grid=(L // tm,),
        in_specs=[pl.BlockSpec((tm, D), lambda i: (i, 0)), pl.BlockSpec((1, D), lambda i: (0, 0))],
        out_specs=pl.BlockSpec((tm, D), lambda i: (i, 0)),
        out_shape=jax.ShapeDtypeStruct((L, D), BF16),
        compiler_params=_cparams(("parallel",)), name=name,
    )(x, w)


def _rms_bwd(x, w, dhn, dres, *, name):
    L, D = x.shape
    tm = min(TOKEN_TILE, L)

    def body(x_ref, w_ref, dhn_ref, dres_ref, dx_ref, dw_ref):
        _, vjp = jax.vjp(_rms, x_ref[...], w_ref[...])
        dx, dw = vjp(dhn_ref[...])
        dx_ref[...] = dx + dres_ref[...]
        _acc_out(dw_ref, dw, pl.program_id(0) == 0)

    row = pl.BlockSpec((tm, D), lambda i: (i, 0))
    vec = pl.BlockSpec((1, D), lambda i: (0, 0))
    return pl.pallas_call(
        body, grid=(L // tm,),
        in_specs=[row, vec, row, row],
        out_specs=[row, vec],
        out_shape=[jax.ShapeDtypeStruct((L, D), F32), jax.ShapeDtypeStruct((1, D), F32)],
        compiler_params=_cparams(("arbitrary",)), name=name,
    )(x, w, dhn, dres)


def _gla_post(o, g, wn):
    return _rms(o, wn) * _silu(g)


def _gla_post_fwd(o, proj, wn):
    L = o.shape[0]
    tm = min(TOKEN_TILE, L)
    gcol = (2 * GLA_DK + D_INNER) // GLA_HEAD_V

    def body(o_ref, g_ref, w_ref, y_ref):
        y_ref[...] = _gla_post(o_ref[...], g_ref[...], w_ref[...]).astype(BF16)

    blk = pl.BlockSpec((tm, GLA_HEAD_V), lambda i, h: (i, h))
    return pl.pallas_call(
        body, grid=(L // tm, GLA_HEADS),
        in_specs=[blk, pl.BlockSpec((tm, GLA_HEAD_V), lambda i, h: (i, gcol + h)), pl.BlockSpec((1, GLA_HEAD_V), lambda i, h: (0, 0))],
        out_specs=blk,
        out_shape=jax.ShapeDtypeStruct((L, D_INNER), BF16),
        compiler_params=_cparams(("parallel", "parallel")), name="gla_post_fwd",
    )(o, proj, wn)


def _gla_post_bwd(o, proj, wn, dy):
    L = o.shape[0]
    tm = min(TOKEN_TILE, L)
    gcol = (2 * GLA_DK + D_INNER) // GLA_HEAD_V

    def body(o_ref, g_ref, w_ref, dy_ref, do_ref, dg_ref, dw_ref):
        _, vjp = jax.vjp(_gla_post, o_ref[...], g_ref[...], w_ref[...])
        do, dg, dw = vjp(dy_ref[...])
        do_ref[...] = do
        dg_ref[...] = dg
        _acc_out(dw_ref, dw, (pl.program_id(0) == 0) & (pl.program_id(1) == 0))

    blk = pl.BlockSpec((tm, GLA_HEAD_V), lambda i, h: (i, h))
    vec = pl.BlockSpec((1, GLA_HEAD_V), lambda i, h: (0, 0))
    return pl.pallas_call(
        body, grid=(L // tm, GLA_HEADS),
        in_specs=[blk, pl.BlockSpec((tm, GLA_HEAD_V), lambda i, h: (i, gcol + h)), vec, blk],
        out_specs=[blk, blk, vec],
        out_shape=[jax.ShapeDtypeStruct((L, D_INNER), F32), jax.ShapeDtypeStruct((L, D_INNER), F32),
                   jax.ShapeDtypeStruct((1, GLA_HEAD_V), F32)],
        compiler_params=_cparams(("arbitrary", "arbitrary")), name="gla_post_bwd",
    )(o, proj, wn, dy)


def _ssd_post(y, z, wn):
    return _rms(y * _silu(z), wn)


def _ssd_post_fwd(y, proj, wn):
    L = y.shape[0]
    tm = min(TOKEN_TILE // 2, L)

    def body(y_ref, z_ref, w_ref, o_ref):
        o_ref[...] = _ssd_post(y_ref[...], z_ref[...], w_ref[...]).astype(BF16)

    blk = pl.BlockSpec((tm, D_INNER), lambda i: (i, 0))
    return pl.pallas_call(
        body, grid=(L // tm,),
        in_specs=[blk, blk, pl.BlockSpec((1, D_INNER), lambda i: (0, 0))],
        out_specs=blk,
        out_shape=jax.ShapeDtypeStruct((L, D_INNER), BF16),
        compiler_params=_cparams(("parallel",)), name="ssd_post_fwd",
    )(y, proj, wn)


def _ssd_post_bwd(y, proj, wn, dyn):
    L = y.shape[0]
    tm = min(TOKEN_TILE // 2, L)

    def body(y_ref, z_ref, w_ref, dyn_ref, dy_ref, dz_ref, dw_ref):
        _, vjp = jax.vjp(_ssd_post, y_ref[...], z_ref[...], w_ref[...])
        dy, dz, dw = vjp(dyn_ref[...])
        dy_ref[...] = dy
        dz_ref[...] = dz
        _acc_out(dw_ref, dw, pl.program_id(0) == 0)

    blk = pl.BlockSpec((tm, D_INNER), lambda i: (i, 0))
    vec = pl.BlockSpec((1, D_INNER), lambda i: (0, 0))
    return pl.pallas_call(
        body, grid=(L // tm,),
        in_specs=[blk, blk, vec, blk],
        out_specs=[blk, blk, vec],
        out_shape=[jax.ShapeDtypeStruct((L, D_INNER), F32), jax.ShapeDtypeStruct((L, D_INNER), F32),
                   jax.ShapeDtypeStruct((1, D_INNER), F32)],
        compiler_params=_cparams(("arbitrary",)), name="ssd_post_bwd",
    )(y, proj, wn, dyn)


CONV_HALO = 8
CONV_COLS = 1024


def _conv_tile(xin, halo, w, b):
    tm = xin.shape[0]
    xx = jnp.concatenate([halo, xin], axis=0)
    u = b
    for k in range(SSD_CONV):
        off = CONV_HALO - (SSD_CONV - 1) + k
        u = u + w[k:k + 1] * xx[off:off + tm]
    return _silu(u)


def _conv_fwd(proj, w, b):
    L = proj.shape[0]
    tm = min(TOKEN_TILE, L)
    c0 = D_INNER // CONV_COLS
    hb = tm // CONV_HALO

    def body(x_ref, h_ref, w_ref, b_ref, o_ref):
        halo = jnp.where(pl.program_id(1) == 0, 0.0, h_ref[...])
        o_ref[...] = _conv_tile(x_ref[...], halo, w_ref[...], b_ref[...])

    return pl.pallas_call(
        body, grid=(SSD_CONV_DIM // CONV_COLS, L // tm),
        in_specs=[
            pl.BlockSpec((tm, CONV_COLS), lambda j, i: (i, c0 + j)),
            pl.BlockSpec((CONV_HALO, CONV_COLS), lambda j, i: (jnp.maximum(i * hb - 1, 0), c0 + j)),
            pl.BlockSpec((SSD_CONV, CONV_COLS), lambda j, i: (0, j)),
            pl.BlockSpec((1, CONV_COLS), lambda j, i: (0, j)),
        ],
        out_specs=pl.BlockSpec((tm, CONV_COLS), lambda j, i: (i, j)),
        out_shape=jax.ShapeDtypeStruct((L, SSD_CONV_DIM), F32),
        compiler_params=_cparams(("parallel", "parallel")), name="conv_fwd",
    )(proj, proj, w, b)


def _conv_bwd(proj, w, b, dxbc):
    L = proj.shape[0]
    tm = min(TOKEN_TILE, L)
    nt = L // tm
    c0 = D_INNER // CONV_COLS
    hb = tm // CONV_HALO

    def body(x_ref, h_ref, w_ref, b_ref, dy_ref, dx_ref, dw_ref, db_ref, carry):
        i = pl.program_id(1)
        first = i == 0

        @pl.when(first)
        def _():
            carry[...] = jnp.zeros((CONV_HALO, CONV_COLS), F32)

        dy = dy_ref[...]
        halo = jnp.where(i == nt - 1, 0.0, h_ref[...])
        _, vjp = jax.vjp(_conv_tile, x_ref[...], halo, w_ref[...], b_ref[...])
        dx, dhalo, dw, db = vjp(dy)
        dx_ref[...] = jnp.concatenate([dx[:tm - CONV_HALO], dx[tm - CONV_HALO:] + carry[...]], axis=0)
        carry[...] = dhalo
        _acc_out(dw_ref, dw, first)
        _acc_out(db_ref, db, first)

    rt = lambda i: nt - 1 - i
    return pl.pallas_call(
        body, grid=(SSD_CONV_DIM // CONV_COLS, nt),
        in_specs=[
            pl.BlockSpec((tm, CONV_COLS), lambda j, i: (rt(i), c0 + j)),
            pl.BlockSpec((CONV_HALO, CONV_COLS), lambda j, i: (jnp.maximum(rt(i) * hb - 1, 0), c0 + j)),
            pl.BlockSpec((SSD_CONV, CONV_COLS), lambda j, i: (0, j)),
            pl.BlockSpec((1, CONV_COLS), lambda j, i: (0, j)),
            pl.BlockSpec((tm, CONV_COLS), lambda j, i: (rt(i), j)),
        ],
        out_specs=[
            pl.BlockSpec((tm, CONV_COLS), lambda j, i: (rt(i), j)),
            pl.BlockSpec((SSD_CONV, CONV_COLS), lambda j, i: (0, j)),
            pl.BlockSpec((1, CONV_COLS), lambda j, i: (0, j)),
        ],
        out_shape=[jax.ShapeDtypeStruct((L, SSD_CONV_DIM), F32), jax.ShapeDtypeStruct((SSD_CONV, SSD_CONV_DIM), F32),
                   jax.ShapeDtypeStruct((1, SSD_CONV_DIM), F32)],
        scratch_shapes=[pltpu.VMEM((CONV_HALO, CONV_COLS), F32)],
        compiler_params=_cparams(("arbitrary", "arbitrary")), name="conv_bwd",
    )(proj, proj, w, b, dxbc)


def _loss_bwd(x, tgt, w):
    L, D = x.shape
    tm = min(TOKEN_TILE, L)

    def body(x_ref, t_ref, w_ref, l_ref, dx_ref, dw_ref):
        xv = x_ref[...]
        wv = w_ref[...]
        r = lax.rsqrt(jnp.mean(xv * xv, axis=-1, keepdims=True) + RMS_EPS)
        xh = xv * r
        e = xh * wv - t_ref[...]
        lsum = 0.5 * jnp.sum(jnp.mean(e * e, axis=-1, keepdims=True), axis=0, keepdims=True)
        dout = e * (1.0 / D)
        gx = dout * wv
        dx_ref[...] = r * (gx - xh * jnp.mean(gx * xh, axis=-1, keepdims=True))
        first = pl.program_id(0) == 0
        _acc_out(dw_ref, jnp.sum(dout * xh, axis=0, keepdims=True), first)
        _acc_out(l_ref, jnp.broadcast_to(lsum, (8, 128)), first)

    row = pl.BlockSpec((tm, D), lambda i: (i, 0))
    vec = pl.BlockSpec((1, D), lambda i: (0, 0))
    return pl.pallas_call(
        body, grid=(L // tm,),
        in_specs=[row, row, vec],
        out_specs=[pl.BlockSpec((8, 128), lambda i: (0, 0)), row, vec],
        out_shape=[jax.ShapeDtypeStruct((8, 128), F32), jax.ShapeDtypeStruct((L, D), F32), jax.ShapeDtypeStruct((1, D), F32)],
        compiler_params=_cparams(("arbitrary",)), name="loss_bwd",
    )(x, tgt, w)


MESH = pl.DeviceIdType.MESH
ANY = pl.BlockSpec(memory_space=pl.ANY)


def _all_gather(xs, *, name):
    R, C = xs.shape

    def body(x_ref, out_ref, send_sems, recv_sems, local_sem):
        x, y, c = lax.axis_index("x"), lax.axis_index("y"), lax.axis_index("c")
        me, sibling = (x, y, c), (x, y, 1 - c)
        chips = [(1 - x, y), (x, 1 - y), (1 - x, 1 - y)]

        def slot(px, py, pc):
            return out_ref.at[4 * px + 2 * py + pc]

        def copy(k, block, to, src=None):
            return pltpu.make_async_remote_copy(
                src_ref=slot(*block) if src is None else src, dst_ref=slot(*block),
                send_sem=send_sems.at[k], recv_sem=recv_sems.at[k], device_id=to, device_id_type=MESH)

        mine = pltpu.make_async_copy(x_ref, slot(*me), local_sem)
        mine.start()
        first = [copy(0, me, sibling, src=x_ref)]
        first += [copy(1 + j, me, (*chip, c), src=x_ref) for j, chip in enumerate(chips)]
        for cp in first:
            cp.start()
        passed = [copy(4 + j, (*chip, c), sibling) for j, chip in enumerate(chips)]
        for j, chip in enumerate(chips):
            copy(1 + j, (*chip, c), me).wait_recv()
            passed[j].start()
        copy(0, sibling, me).wait_recv()
        for j, chip in enumerate(chips):
            copy(4 + j, (*chip, 1 - c), me).wait_recv()
        for cp in first + passed:
            cp.wait_send()
        mine.wait()

    return pl.pallas_call(
        body,
        out_shape=jax.ShapeDtypeStruct((N_DEV, R, C), xs.dtype),
        in_specs=[ANY], out_specs=ANY,
        scratch_shapes=[pltpu.SemaphoreType.DMA((7,)), pltpu.SemaphoreType.DMA((7,)), pltpu.SemaphoreType.DMA],
        name=name,
    )(xs)


def _exchange(parts, *, name):
    _, R, C = parts.shape

    def body(p_ref, out_ref, send_sems, recv_sems, local_sem):
        x, y, c = lax.axis_index("x"), lax.axis_index("y"), lax.axis_index("c")
        my = 4 * x + 2 * y + c
        mine = pltpu.make_async_copy(p_ref.at[my], out_ref.at[my], local_sem)
        mine.start()
        copies = []
        for k in range(1, N_DEV):
            fx, fy, fc = (k >> 2) & 1, (k >> 1) & 1, k & 1
            px, py, pc = (1 - x if fx else x), (1 - y if fy else y), (1 - c if fc else c)
            pid = 4 * px + 2 * py + pc
            copies.append(pltpu.make_async_remote_copy(
                src_ref=p_ref.at[pid], dst_ref=out_ref.at[my], send_sem=send_sems.at[k - 1], recv_sem=recv_sems.at[k - 1],
                device_id=(px, py, pc), device_id_type=MESH))
            copies[-1].start()
        for k in range(1, N_DEV):
            fx, fy, fc = (k >> 2) & 1, (k >> 1) & 1, k & 1
            px, py, pc = (1 - x if fx else x), (1 - y if fy else y), (1 - c if fc else c)
            pid = 4 * px + 2 * py + pc
            pltpu.make_async_remote_copy(
                src_ref=p_ref.at[pid], dst_ref=out_ref.at[pid], send_sem=send_sems.at[k - 1], recv_sem=recv_sems.at[k - 1],
                device_id=(px, py, pc), device_id_type=MESH).wait()
        mine.wait()

    return pl.pallas_call(
        body,
        out_shape=jax.ShapeDtypeStruct((N_DEV, R, C), parts.dtype),
        in_specs=[ANY], out_specs=ANY,
        scratch_shapes=[pltpu.SemaphoreType.DMA((7,)), pltpu.SemaphoreType.DMA((7,)), pltpu.SemaphoreType.DMA],
        name=name,
    )(parts)


def _adamw(parts, w, m, v, *, name):
    R = w.shape[0]
    tr = _pick(R, (512, 256, 128, 64, 40, 32, 16, 8))

    def body(p_ref, w_ref, m_ref, v_ref, g_ref, d_ref, mo_ref, vo_ref):
        g = p_ref[0]
        for s in range(1, N_DEV):
            g = g + p_ref[s]
        mn = ADAM_B1 * m_ref[...] + (1.0 - ADAM_B1) * g
        vn = ADAM_B2 * v_ref[...] + (1.0 - ADAM_B2) * jnp.square(g)
        m_hat = mn / (1.0 - ADAM_B1 ** ADAM_STEP)
        v_hat = vn / (1.0 - ADAM_B2 ** ADAM_STEP)
        g_ref[...] = g
        d_ref[...] = -ADAM_LR * (m_hat / (jnp.sqrt(v_hat) + ADAM_EPS) + ADAM_WD * w_ref[...])
        mo_ref[...] = mn
        vo_ref[...] = vn

    blk = pl.BlockSpec((tr, 128), lambda i: (i, 0))
    shp = jax.ShapeDtypeStruct((R, 128), F32)
    return pl.pallas_call(
        body, grid=(R // tr,),
        in_specs=[pl.BlockSpec((N_DEV, tr, 128), lambda i: (0, i, 0)), blk, blk, blk],
        out_specs=[blk, blk, blk, blk],
        out_shape=[shp, shp, shp, shp],
        compiler_params=_cparams(("parallel",)), name=name,
    )(parts, w, m, v)


def _rows(a):
    return a.reshape(-1, 128)


def _pad_rows(a, rows):
    return jnp.pad(a, ((0, rows - a.shape[0]), (0, 0)))


def _pad_cols(a, cols):
    return jnp.pad(a, ((0, 0), (0, cols - a.shape[1])))


def _col_shards(a, n):
    r = a.shape[0]
    return a.reshape(r, N_DEV, n).transpose(1, 0, 2).reshape(N_DEV, -1, 128)


def _from_col_shards(g, r, n):
    return g.reshape(N_DEV, r, n).transpose(1, 0, 2).reshape(r, N_DEV * n)


def kernel(x, norm_w, gla_in_proj, gla_gate_up, gla_gate_bias, gla_head_norm, gla_out_proj, ssd_in_proj, ssd_conv_w, ssd_conv_b, ssd_dt_bias, ssd_a_log, ssd_d, ssd_gate_norm, ssd_out_proj, final_norm, loss_target, m_norm_w, m_gla_in_proj, m_gla_gate_up, m_gla_gate_bias, m_gla_head_norm, m_gla_out_proj, m_ssd_in_proj, m_ssd_conv_w, m_ssd_conv_b, m_ssd_dt_bias, m_ssd_a_log, m_ssd_d, m_ssd_gate_norm, m_ssd_out_proj, m_final_norm, v_norm_w, v_gla_in_proj, v_gla_gate_up, v_gla_gate_bias, v_gla_head_norm, v_gla_out_proj, v_ssd_in_proj, v_ssd_conv_w, v_ssd_conv_b, v_ssd_dt_bias, v_ssd_a_log, v_ssd_d, v_ssd_gate_norm, v_ssd_out_proj, v_final_norm):
    x0 = x[0]
    tgt = loss_target[0]
    n_gin = GLA_PROJ // N_DEV
    n_sin = SSD_PROJ // N_DEV
    n_up = GLA_DK // N_DEV
    n_cv = SSD_CONV_DIM // N_DEV

    big = [_rows(gla_in_proj[0]), _pad_rows(_rows(gla_gate_up[0]), 16), _rows(gla_out_proj[0]), _rows(ssd_in_proj[0]),
           _rows(ssd_out_proj[0])]
    big_rows = [a.shape[0] for a in big]
    wg = _all_gather(jnp.concatenate(big, axis=0).astype(BF16), name="gather_weights")
    small = [_rows(ssd_conv_w[0]), _rows(ssd_conv_b[0]), _rows(ssd_gate_norm[0])]
    small_rows = [a.shape[0] for a in small]
    ws = _all_gather(_pad_rows(jnp.concatenate(small, axis=0), 24), name="gather_small")

    def seg(g, rows, i):
        o = sum(rows[:i])
        return g[:, o:o + rows[i]]

    w_gin = _pad_cols(_from_col_shards(seg(wg, big_rows, 0), D_MODEL, n_gin), GLA_PROJ_PAD)
    wup = _pad_rows(_from_col_shards(seg(wg, big_rows, 1)[:, :GLA_RANK * n_up // 128], GLA_RANK, n_up), 128).astype(F32)
    w_gout = seg(wg, big_rows, 2).reshape(D_INNER, D_MODEL)
    w_sin = _pad_cols(_from_col_shards(seg(wg, big_rows, 3), D_MODEL, n_sin), SSD_PROJ_PAD)
    w_sout = seg(wg, big_rows, 4).reshape(D_INNER, D_MODEL)
    conv_w = _from_col_shards(seg(ws, small_rows, 0), SSD_CONV, n_cv)
    conv_b = seg(ws, small_rows, 1).reshape(1, SSD_CONV_DIM)
    gate_norm = seg(ws, small_rows, 2).reshape(1, D_INNER)
    vec128 = lambda a: _pad_cols(a.reshape(1, -1), 128)
    dtb, alog, dsk = vec128(ssd_dt_bias), vec128(ssd_a_log), vec128(ssd_d)
    nw0, nw1 = norm_w[0:1], norm_w[1:2]

    hn1 = _rms_fwd(x0, nw0, name="rms1_fwd")
    proj1 = _mm(hn1, w_gin, name="gla_in_proj")
    o, og, s_saved = _gla_layer_fwd(proj1, wup, gla_gate_bias, gla_head_norm)
    x1 = _mm(og, w_gout, add=x0, name="gla_out_proj")
    hn2 = _rms_fwd(x1, nw1, name="rms2_fwd")
    proj2 = _mm(hn2, w_sin, name="ssd_in_proj")
    xbc = _conv_fwd(proj2, conv_w, conv_b)
    y, h_saved = _ssd_layer_fwd(xbc, proj2, dtb, alog, dsk)
    yn = _ssd_post_fwd(y, proj2, gate_norm)
    x2 = _mm(yn, w_sout, add=x1, name="ssd_out_proj")
    lsum, dx2, d_final = _loss_bwd(x2, tgt, final_norm.reshape(1, D_MODEL))
    loss = lax.psum(lsum[0, 0], ("x", "y", "c"))

    d_sout = _mm_tn(yn, dx2, name="ssd_out_proj_dw")
    dyn = _mm(dx2, w_sout.T, name="ssd_out_proj_dx")
    dy, dz, d_gate_norm = _ssd_post_bwd(y, proj2, gate_norm, dyn)
    dxbc_post, ddt, d_dtb, d_alog, d_dsk = _ssd_layer_bwd(xbc, proj2, dtb, alog, dsk, h_saved, dy)
    dxbc, d_conv_w, d_conv_b = _conv_bwd(proj2, conv_w, conv_b, dxbc_post)
    dproj2 = jnp.concatenate([dz, dxbc, ddt, jnp.zeros((ddt.shape[0], SSD_PROJ_PAD - SSD_DT_COL - 128), F32)], axis=1).astype(BF16)
    d_sin = _mm_tn(hn2, dproj2, name="ssd_in_proj_dw")
    dhn2 = _mm(dproj2, w_sin.T, name="ssd_in_proj_dx")
    dx1, d_nw1 = _rms_bwd(x1, nw1, dhn2, dx2, name="rms2_bwd")
    d_gout = _mm_tn(og, dx1, name="gla_out_proj_dw")
    dog = _mm(dx1, w_gout.T, name="gla_out_proj_dx")
    dproj1, d_wup, d_gbias, d_head_norm = _gla_layer_bwd(proj1, wup, gla_gate_bias, gla_head_norm, o, s_saved, dog)
    d_gin = _mm_tn(hn1, dproj1, name="gla_in_proj_dw")
    dhn1 = _mm(dproj1, w_gin.T, name="gla_in_proj_dx")
    dx0, d_nw0 = _rms_bwd(x0, nw0, dhn1, dx1, name="rms1_bwd")

    d_wup_full = d_wup[:GLA_RANK]
    sharded = [
        ("gla_in_proj", _col_shards(d_gin[:, :GLA_PROJ], n_gin)),
        ("gla_gate_up", _col_shards(d_wup_full, n_up)),
        ("gla_out_proj", d_gout.reshape(N_DEV, -1, 128)),
        ("ssd_in_proj", _col_shards(d_sin[:, :SSD_PROJ], n_sin)),
        ("ssd_conv_w", _col_shards(d_conv_w, n_cv)),
        ("ssd_conv_b", d_conv_b.reshape(N_DEV, -1, 128)),
        ("ssd_gate_norm", d_gate_norm.reshape(N_DEV, -1, 128)),
        ("ssd_out_proj", d_sout.reshape(N_DEV, -1, 128)),
    ]
    given = dict(
        gla_in_proj=(gla_in_proj, m_gla_in_proj, v_gla_in_proj), gla_gate_up=(gla_gate_up, m_gla_gate_up, v_gla_gate_up),
        gla_out_proj=(gla_out_proj, m_gla_out_proj, v_gla_out_proj), ssd_in_proj=(ssd_in_proj, m_ssd_in_proj, v_ssd_in_proj),
        ssd_conv_w=(ssd_conv_w, m_ssd_conv_w, v_ssd_conv_w), ssd_conv_b=(ssd_conv_b, m_ssd_conv_b, v_ssd_conv_b),
        ssd_gate_norm=(ssd_gate_norm, m_ssd_gate_norm, v_ssd_gate_norm), ssd_out_proj=(ssd_out_proj, m_ssd_out_proj, v_ssd_out_proj),
        norm_w=(norm_w, m_norm_w, v_norm_w), gla_gate_bias=(gla_gate_bias, m_gla_gate_bias, v_gla_gate_bias),
        gla_head_norm=(gla_head_norm, m_gla_head_norm, v_gla_head_norm), ssd_dt_bias=(ssd_dt_bias, m_ssd_dt_bias, v_ssd_dt_bias),
        ssd_a_log=(ssd_a_log, m_ssd_a_log, v_ssd_a_log), ssd_d=(ssd_d, m_ssd_d, v_ssd_d),
        final_norm=(final_norm, m_final_norm, v_final_norm),
    )
    results = {}

    def update(group, parts_rows, gather, name):
        rows = [p.shape[1] for _, p in group]
        total = -(-sum(rows) // parts_rows) * parts_rows
        parts = jnp.concatenate([p for _, p in group], axis=1)
        parts = jnp.pad(parts, ((0, 0), (0, total - parts.shape[1]), (0, 0)))
        if gather:
            recv = _all_gather(parts[0], name=name + "_gather")
        else:
            recv = _exchange(parts, name=name + "_exchange")

        def flat(i):
            rs = []
            for (nm, _), r in zip(group, rows):
                a = given[nm][i]
                a = a.reshape(1, -1)
                a = _pad_cols(a, r * 128).reshape(r, 128)
                rs.append(a)
            return _pad_rows(jnp.concatenate(rs, axis=0), total)

        outs = _adamw(recv, flat(0), flat(1), flat(2), name=name + "_adamw")
        off = 0
        for (nm, _), r in zip(group, rows):
            shape = given[nm][0].shape
            n = 1
            for d in shape:
                n *= d
            results[nm] = tuple(t[off:off + r].reshape(-1)[:n].reshape(shape) for t in outs)
            off += r

    update(sharded, 512, False, "sharded")
    replicated = [
        ("norm_w", jnp.concatenate([d_nw0, d_nw1], axis=0).reshape(1, -1, 128)),
        ("gla_gate_bias", d_gbias.reshape(1, -1, 128)),
        ("gla_head_norm", d_head_norm.reshape(1, -1, 128)),
        ("ssd_dt_bias", d_dtb.reshape(1, 1, 128)),
        ("ssd_a_log", d_alog.reshape(1, 1, 128)),
        ("ssd_d", d_dsk.reshape(1, 1, 128)),
        ("final_norm", d_final.reshape(1, -1, 128)),
    ]
    update(replicated, 8, True, "replicated")

    order = ["norm_w", "gla_in_proj", "gla_gate_up", "gla_gate_bias", "gla_head_norm", "gla_out_proj", "ssd_in_proj",
             "ssd_conv_w", "ssd_conv_b", "ssd_dt_bias", "ssd_a_log", "ssd_d", "ssd_gate_norm", "ssd_out_proj", "final_norm"]
    out = [loss, dx0[None]]
    for i in range(4):
        out += [results[n][i] for n in order]
    return tuple(out)
```

```python
import jax
import jax.numpy as jnp
from jax import lax
from jax.experimental import pallas as pl
from jax.experimental.pallas import tpu as pltpu

F32 = jnp.float32
BF16 = jnp.bfloat16

D_MODEL = 1024
D_INNER = 2048
RMS_EPS = 1e-6
GLA_HEADS = 4
GLA_DK = 512
GLA_HEAD_K = 128
GLA_HEAD_V = 512
GLA_RANK = 16
GLA_NORMALIZER = 16.0
CHUNK = 64
SUB = 16
GLA_PROJ = 5136
GLA_PROJ_PAD = 5376
GLA_GK_COL = 5120
SSD_HEADS = 32
SSD_GROUPS = 8
SSD_HPG = 4
SSD_P = 64
SSD_N = 128
SSD_CONV = 4
SSD_CONV_DIM = 4096
SSD_PROJ = 6176
SSD_PROJ_PAD = 6400
SSD_DT_COL = 6144
N_DEV = 8

ADAM_LR = 0.001
ADAM_B1 = 0.9
ADAM_B2 = 0.999
ADAM_EPS = 1e-08
ADAM_WD = 0.01
ADAM_STEP = 10

VMEM_LIMIT = 56 * 1024 * 1024
TOKEN_TILE = 512
MM_TOKEN_TILE = 1024
MM_TILES = (1792, 1280, 1024, 768, 512, 256, 128)


def _dot(a, b):
    return jnp.dot(a, b, preferred_element_type=F32)


def _dot_nt(a, b):
    return lax.dot_general(a, b, (((1,), (1,)), ((), ())), preferred_element_type=F32)


def _dot_tn(a, b):
    return lax.dot_general(a, b, (((0,), (0,)), ((), ())), preferred_element_type=F32)


def _bf(a):
    return a.astype(BF16)


@jax.custom_vjp
def _mxu(a, b):
    return _dot(_bf(a), _bf(b))


def _mxu_fwd(a, b):
    return _mxu(a, b), (a, b)


def _mxu_bwd(res, g):
    a, b = res
    return _dot_nt(_bf(g), _bf(b)), _dot_tn(_bf(a), _bf(g))


_mxu.defvjp(_mxu_fwd, _mxu_bwd)


@jax.custom_vjp
def _mxu_nt(a, b):
    return _dot_nt(_bf(a), _bf(b))


def _mxu_nt_fwd(a, b):
    return _mxu_nt(a, b), (a, b)


def _mxu_nt_bwd(res, g):
    a, b = res
    return _dot(_bf(g), _bf(b)), _dot_tn(_bf(g), _bf(a))


_mxu_nt.defvjp(_mxu_nt_fwd, _mxu_nt_bwd)


@jax.custom_vjp
def _mxu_tn(a, b):
    return _dot_tn(_bf(a), _bf(b))


def _mxu_tn_fwd(a, b):
    return _mxu_tn(a, b), (a, b)


def _mxu_tn_bwd(res, g):
    a, b = res
    return _dot_nt(_bf(b), _bf(g)), _dot(_bf(a), _bf(g))


_mxu_tn.defvjp(_mxu_tn_fwd, _mxu_tn_bwd)


def _split2(a):
    hi = _bf(a)
    return hi, _bf(a - hi.astype(F32))


def _three_pass(dot, a, b):
    ah, al = _split2(a)
    bh, bl = _split2(b)
    return dot(ah, bh) + (dot(ah, bl) + dot(al, bh))


@jax.custom_vjp
def _dot3_nt(a, b):
    return _three_pass(_dot_nt, a, b)


def _dot3_nt_fwd(a, b):
    return _dot3_nt(a, b), (a, b)


def _dot3_nt_bwd(res, g):
    a, b = res
    return _three_pass(_dot, g, b), _three_pass(_dot_tn, g, a)


_dot3_nt.defvjp(_dot3_nt_fwd, _dot3_nt_bwd)


def _silu(x):
    return x / (1.0 + jnp.exp(-x))


def _log_sigmoid(z):
    return jnp.minimum(z, 0.0) - jnp.log(1.0 + jnp.exp(-jnp.abs(z)))


def _softplus(z):
    return jnp.maximum(z, 0.0) + jnp.log(1.0 + jnp.exp(-jnp.abs(z)))


def _iota(shape, dim):
    return lax.broadcasted_iota(jnp.int32, shape, dim)


def _rms(x, w):
    return x * lax.rsqrt(jnp.mean(x * x, axis=-1, keepdims=True) + RMS_EPS) * w


def _scan_rows(a, reverse, seg):
    n = a.shape[0]
    pos = _iota(a.shape, 0) & (seg - 1)
    sh = 1
    while sh < seg:
        if reverse:
            a = a + jnp.where(pos < seg - sh, pltpu.roll(a, n - sh, 0), 0.0)
        else:
            a = a + jnp.where(pos >= sh, pltpu.roll(a, sh, 0), 0.0)
        sh *= 2
    return a


def _make_cumsum(seg):
    @jax.custom_vjp
    def cumsum(a):
        return _scan_rows(a, False, seg)

    cumsum.defvjp(lambda a: (_scan_rows(a, False, seg), None), lambda _, g: (_scan_rows(g, True, seg),))
    return cumsum


_cumsum_sub = _make_cumsum(SUB)
_cumsum_rows = _make_cumsum(CHUNK)


def _cparams(sem):
    return pltpu.CompilerParams(dimension_semantics=sem, vmem_limit_bytes=VMEM_LIMIT)


def _acc_out(ref, val, first):
    @pl.when(first)
    def _():
        ref[...] = val

    @pl.when(jnp.logical_not(first))
    def _():
        ref[...] += val


def _gla_chunk(q, k, v, gk, wup, bias, St):
    nb = CHUNK // SUB
    z = _mxu(gk, wup) + bias
    la = _log_sigmoid(z) * (1.0 / GLA_NORMALIZER)
    qs = q * (GLA_HEAD_K ** -0.5)
    bl = _cumsum_sub(la)
    tot = [jnp.sum(la[i * SUB:(i + 1) * SUB], axis=0, keepdims=True) for i in range(nb)]
    pre = [jnp.zeros((1, GLA_HEAD_K), F32)]
    for i in range(nb):
        pre.append(pre[i] + tot[i])
    b_last = pre[nb]
    rows_of = lambda vals: jnp.concatenate([jnp.broadcast_to(t, (SUB, GLA_HEAD_K)) for t in vals], axis=0)
    suf = rows_of(tot) - bl
    nxt = rows_of(pre[1:])
    o = _mxu_nt(qs * jnp.exp(bl + rows_of(pre[:nb])), St)
    St_new = St * jnp.exp(b_last) + _mxu_tn(v, k * jnp.exp(suf + (b_last - nxt)))
    qa = qs * jnp.exp(bl)
    row = _iota((CHUNK, GLA_HEAD_K), 0)
    rs = _iota((SUB, GLA_HEAD_K), 0)
    cs = _iota((SUB, CHUNK), 1)
    a_rows = []
    for i in range(nb):
        sl = slice(i * SUB, (i + 1) * SUB)
        q_i, k_i, bl_i = qs[sl], k[sl], bl[sl]
        if i > 0:
            kp = jnp.where(row < i * SUB, k * jnp.exp(jnp.minimum(suf + (pre[i] - nxt), 0.0)), 0.0)
            a_i = _dot3_nt(qa[sl], kp)
        else:
            a_i = jnp.zeros((SUB, CHUNK), F32)
        for j in range(SUB):
            e = jnp.exp(jnp.minimum(bl_i - bl_i[j:j + 1], 0.0))
            t = jnp.where(rs >= j, q_i * e * k_i[j:j + 1], 0.0)
            a_i = a_i + jnp.where(cs == i * SUB + j, jnp.sum(t, axis=-1, keepdims=True), 0.0)
        a_rows.append(a_i)
    o = o + _mxu(jnp.concatenate(a_rows, axis=0), v)
    return o, St_new


def _gla_post(o, g, wn):
    return _rms(o, wn) * _silu(g)


GLA_HALF = 2 * GLA_HEAD_V


def _gla_specs(nc, rev):
    ci = (lambda c: nc - 1 - c) if rev else (lambda c: c)
    v0 = 2 * GLA_DK // GLA_HALF
    g0 = (2 * GLA_DK + D_INNER) // GLA_HALF
    return [
        pl.BlockSpec((CHUNK, GLA_DK), lambda c: (ci(c), 0)),
        pl.BlockSpec((CHUNK, GLA_DK), lambda c: (ci(c), 1)),
        pl.BlockSpec((CHUNK, GLA_HALF), lambda c: (ci(c), v0)),
        pl.BlockSpec((CHUNK, GLA_HALF), lambda c: (ci(c), v0 + 1)),
        pl.BlockSpec((CHUNK, GLA_HALF), lambda c: (ci(c), g0)),
        pl.BlockSpec((CHUNK, GLA_HALF), lambda c: (ci(c), g0 + 1)),
        pl.BlockSpec((CHUNK, 128), lambda c: (ci(c), GLA_GK_COL // 128)),
        pl.BlockSpec((128, GLA_DK), lambda c: (0, 0)),
        pl.BlockSpec((1, GLA_DK), lambda c: (0, 0)),
        pl.BlockSpec((1, GLA_HEAD_V), lambda c: (0, 0)),
    ]


def _head_cols(ref_a, ref_b, h):
    ref = ref_a if h < 2 else ref_b
    return ref[:, (h % 2) * GLA_HEAD_V:(h % 2 + 1) * GLA_HEAD_V]


def _gla_layer_fwd(proj, wup, bias, wn):
    L = proj.shape[0]
    nc = L // CHUNK

    def body(q_ref, k_ref, va_ref, vb_ref, ga_ref, gb_ref, gk_ref, wup_ref, b_ref, wn_ref, o_ref, og_ref, s_ref, st):
        @pl.when(pl.program_id(0) == 0)
        def _():
            st[...] = jnp.zeros(st.shape, F32)

        gk = gk_ref[...]
        for h in range(GLA_HEADS):
            kc = slice(h * GLA_HEAD_K, (h + 1) * GLA_HEAD_K)
            vc = slice(h * GLA_HEAD_V, (h + 1) * GLA_HEAD_V)
            s_in = st[h]
            s_ref[h] = s_in
            o, s_new = _gla_chunk(q_ref[:, kc], k_ref[:, kc], _head_cols(va_ref, vb_ref, h), gk, wup_ref[:, kc], b_ref[:, kc], s_in)
            st[h] = s_new
            o_ref[:, vc] = o
            og_ref[:, vc] = _gla_post(o, _head_cols(ga_ref, gb_ref, h), wn_ref[...]).astype(BF16)

    return pl.pallas_call(
        body,
        grid=(nc,),
        in_specs=_gla_specs(nc, False),
        out_specs=[
            pl.BlockSpec((CHUNK, D_INNER), lambda c: (c, 0)),
            pl.BlockSpec((CHUNK, D_INNER), lambda c: (c, 0)),
            pl.BlockSpec((None, GLA_HEADS, GLA_HEAD_V, GLA_HEAD_K), lambda c: (c, 0, 0, 0)),
        ],
        out_shape=[
            jax.ShapeDtypeStruct((L, D_INNER), F32),
            jax.ShapeDtypeStruct((L, D_INNER), BF16),
            jax.ShapeDtypeStruct((nc, GLA_HEADS, GLA_HEAD_V, GLA_HEAD_K), F32),
        ],
        scratch_shapes=[pltpu.VMEM((GLA_HEADS, GLA_HEAD_V, GLA_HEAD_K), F32)],
        compiler_params=_cparams(("arbitrary",)),
        name="gla_layer_fwd",
    )(proj, proj, proj, proj, proj, proj, proj, wup, bias, wn)


def _gla_layer_bwd(proj, wup, bias, wn, o, s_in, dog):
    L = proj.shape[0]
    nc = L // CHUNK

    def body(q_ref, k_ref, va_ref, vb_ref, ga_ref, gb_ref, gk_ref, wup_ref, b_ref, wn_ref, o_ref, s_ref, dog_ref,
             dp_ref, dwup_ref, db_ref, dwn_ref, dst):
        @pl.when(pl.program_id(0) == 0)
        def _():
            dst[...] = jnp.zeros(dst.shape, F32)
            dwup_ref[...] = jnp.zeros(dwup_ref.shape, F32)
            db_ref[...] = jnp.zeros(db_ref.shape, F32)
            dwn_ref[...] = jnp.zeros(dwn_ref.shape, F32)

        gk = gk_ref[...]
        dgk_sum = jnp.zeros((CHUNK, 128), F32)
        for h in range(GLA_HEADS):
            kc = slice(h * GLA_HEAD_K, (h + 1) * GLA_HEAD_K)
            vc = slice(h * GLA_HEAD_V, (h + 1) * GLA_HEAD_V)
            _, post_vjp = jax.vjp(_gla_post, o_ref[:, vc], _head_cols(ga_ref, gb_ref, h), wn_ref[...])
            do, dg, dwn = post_vjp(dog_ref[:, vc])
            _, vjp = jax.vjp(_gla_chunk, q_ref[:, kc], k_ref[:, kc], _head_cols(va_ref, vb_ref, h), gk, wup_ref[:, kc],
                             b_ref[:, kc], s_ref[h])
            dq, dk, dv, dgk, dwup, db, ds = vjp((do, dst[h]))
            dst[h] = ds
            dp_ref[:, kc] = dq.astype(BF16)
            dp_ref[:, GLA_DK + h * GLA_HEAD_K:GLA_DK + (h + 1) * GLA_HEAD_K] = dk.astype(BF16)
            dp_ref[:, 2 * GLA_DK + h * GLA_HEAD_V:2 * GLA_DK + (h + 1) * GLA_HEAD_V] = dv.astype(BF16)
            dp_ref[:, 2 * GLA_DK + D_INNER + h * GLA_HEAD_V:2 * GLA_DK + D_INNER + (h + 1) * GLA_HEAD_V] = dg.astype(BF16)
            dwup_ref[:, kc] += dwup
            db_ref[:, kc] += db
            dwn_ref[...] += dwn
            dgk_sum = dgk_sum + dgk
        dp_ref[:, GLA_GK_COL:GLA_GK_COL + 128] = dgk_sum.astype(BF16)
        dp_ref[:, GLA_GK_COL + 128:] = jnp.zeros((CHUNK, GLA_PROJ_PAD - GLA_GK_COL - 128), BF16)

    rc = lambda c: nc - 1 - c
    return pl.pallas_call(
        body,
        grid=(nc,),
        in_specs=_gla_specs(nc, True) + [
            pl.BlockSpec((CHUNK, D_INNER), lambda c: (rc(c), 0)),
            pl.BlockSpec((None, GLA_HEADS, GLA_HEAD_V, GLA_HEAD_K), lambda c: (rc(c), 0, 0, 0)),
            pl.BlockSpec((CHUNK, D_INNER), lambda c: (rc(c), 0)),
        ],
        out_specs=[
            pl.BlockSpec((CHUNK, GLA_PROJ_PAD), lambda c: (rc(c), 0)),
            pl.BlockSpec((128, GLA_DK), lambda c: (0, 0)),
            pl.BlockSpec((1, GLA_DK), lambda c: (0, 0)),
            pl.BlockSpec((1, GLA_HEAD_V), lambda c: (0, 0)),
        ],
        out_shape=[
            jax.ShapeDtypeStruct((L, GLA_PROJ_PAD), BF16),
            jax.ShapeDtypeStruct((128, GLA_DK), F32),
            jax.ShapeDtypeStruct((1, GLA_DK), F32),
            jax.ShapeDtypeStruct((1, GLA_HEAD_V), F32),
        ],
        scratch_shapes=[pltpu.VMEM((GLA_HEADS, GLA_HEAD_V, GLA_HEAD_K), F32)],
        compiler_params=_cparams(("arbitrary",)),
        name="gla_layer_bwd",
    )(proj, proj, proj, proj, proj, proj, proj, wup, bias, wn, o, s_in, dog)


@jax.custom_vjp
def _expand(v):
    r = v.shape[0]
    left = _iota((r, 128), 1) < SSD_P
    slabs = []
    for p in range(SSD_HEADS // 2):
        a = jnp.broadcast_to(v[:, 2 * p:2 * p + 1], (r, 128))
        b = jnp.broadcast_to(v[:, 2 * p + 1:2 * p + 2], (r, 128))
        slabs.append(jnp.where(left, a, b))
    return jnp.concatenate(slabs, axis=1)


def _expand_fwd(v):
    return _expand(v), None


def _expand_bwd(_, g):
    r = g.shape[0]
    lane = _iota((r, 128), 1)
    left = lane < SSD_P
    dv = jnp.zeros((r, 128), F32)
    for p in range(SSD_HEADS // 2):
        gs = g[:, 128 * p:128 * (p + 1)]
        sa = jnp.sum(jnp.where(left, gs, 0.0), axis=-1, keepdims=True)
        sb = jnp.sum(jnp.where(left, 0.0, gs), axis=-1, keepdims=True)
        dv = dv + jnp.where(lane == 2 * p, sa, 0.0) + jnp.where(lane == 2 * p + 1, sb, 0.0)
    return (dv,)


_expand.defvjp(_expand_fwd, _expand_bwd)

SSD_GW = SSD_HPG * SSD_P
SSD_BC = SSD_GROUPS * SSD_N


def _ssd_chunk(xs, Bm, Cm, dtp, dtb, alog, dsk, h_in):
    dt = _softplus(dtp + dtb)
    acum = _cumsum_rows(dt * (-jnp.exp(alog)))
    a_last = acum[CHUNK - 1:CHUNK]
    acum_b = _expand(acum)
    w_end = _expand(dt * jnp.exp(a_last - acum))
    d_b = _expand(jnp.broadcast_to(dsk, (8, 128)))[0:1]
    ac_t = jnp.concatenate([acum, acum], axis=0).T
    dt_t = jnp.concatenate([dt, dt], axis=0).T
    lane = _iota((CHUNK, 128), 1)
    left = lane < SSD_P
    causal = (lane & (SSD_P - 1)) <= _iota((CHUNK, 128), 0)
    ys, sts = [], []
    for g in range(SSD_GROUPS):
        Bg = Bm[:, g * SSD_N:(g + 1) * SSD_N]
        Cg = Cm[:, g * SSD_N:(g + 1) * SSD_N]
        cb2 = _mxu_nt(Cg, jnp.concatenate([Bg, Bg], axis=0))
        y_off = _mxu_nt(Cg, h_in[g * SSD_GW:(g + 1) * SSD_GW])
        for k in range(SSD_HPG // 2):
            p = g * (SSD_HPG // 2) + k
            sl = slice(128 * p, 128 * (p + 1))
            ac_c = acum_b[:, sl]
            ac_r = jnp.where(left, ac_t[2 * p:2 * p + 1], ac_t[2 * p + 1:2 * p + 2])
            dt_r = jnp.where(left, dt_t[2 * p:2 * p + 1], dt_t[2 * p + 1:2 * p + 2])
            m2 = cb2 * jnp.where(causal, jnp.exp(jnp.minimum(ac_c - ac_r, 0.0)), 0.0) * dt_r
            xsl = xs[:, sl]
            x2 = jnp.concatenate([jnp.where(left, xsl, 0.0), jnp.where(left, 0.0, xsl)], axis=0)
            ys.append(_mxu(m2, x2) + y_off[:, 128 * k:128 * (k + 1)] * jnp.exp(ac_c) + xsl * d_b[:, sl])
        gs = slice(g * SSD_GW, (g + 1) * SSD_GW)
        sts.append(_mxu_tn(xs[:, gs] * w_end[:, gs], Bg))
    cd = jnp.exp(ac_t[:, CHUNK - 1:CHUNK])
    hs = [h_in[h * SSD_P:(h + 1) * SSD_P] * cd[h:h + 1] for h in range(SSD_HEADS)]
    h_out = jnp.concatenate(hs, axis=0) + jnp.concatenate(sts, axis=0)
    return jnp.concatenate(ys, axis=1), h_out


def _ssd_step(xs, Bm, Cm, dtp, dtb, alog, dsk, h_in, z, wn):
    y, h_out = _ssd_chunk(xs, Bm, Cm, dtp, dtb, alog, dsk, h_in)
    return _rms(y * _silu(z), wn), h_out


def _ssd_specs(nc, rev):
    ci = (lambda c: nc - 1 - c) if rev else (lambda c: c)
    vec = pl.BlockSpec((1, 128), lambda c: (0, 0))
    return [
        pl.BlockSpec((CHUNK, D_INNER), lambda c: (ci(c), 0)),
        pl.BlockSpec((CHUNK, SSD_BC), lambda c: (ci(c), D_INNER // SSD_BC)),
        pl.BlockSpec((CHUNK, SSD_BC), lambda c: (ci(c), D_INNER // SSD_BC + 1)),
        pl.BlockSpec((CHUNK, 128), lambda c: (ci(c), SSD_DT_COL // 128)),
        vec, vec, vec,
        pl.BlockSpec((CHUNK, D_INNER), lambda c: (ci(c), 0)),
        pl.BlockSpec((1, D_INNER), lambda c: (0, 0)),
    ]


def _ssd_layer_fwd(xbc, proj, dtb, alog, dsk, wn):
    L = xbc.shape[0]
    nc = L // CHUNK

    def body(xs_ref, b_ref, c_ref, dt_ref, dtb_ref, alog_ref, dsk_ref, z_ref, wn_ref, y_ref, hs_ref, hst):
        @pl.when(pl.program_id(0) == 0)
        def _():
            hst[...] = jnp.zeros(hst.shape, F32)

        h_in = hst[...]
        hs_ref[...] = h_in
        yn, h_out = _ssd_step(xs_ref[...], b_ref[...], c_ref[...], dt_ref[...], dtb_ref[...], alog_ref[...], dsk_ref[...],
                              h_in, z_ref[...], wn_ref[...])
        y_ref[...] = yn.astype(BF16)
        hst[...] = h_out

    return pl.pallas_call(
        body,
        grid=(nc,),
        in_specs=_ssd_specs(nc, False),
        out_specs=[
            pl.BlockSpec((CHUNK, D_INNER), lambda c: (c, 0)),
            pl.BlockSpec((None, D_INNER, SSD_N), lambda c: (c, 0, 0)),
        ],
        out_shape=[
            jax.ShapeDtypeStruct((L, D_INNER), BF16),
            jax.ShapeDtypeStruct((nc, D_INNER, SSD_N), F32),
        ],
        scratch_shapes=[pltpu.VMEM((D_INNER, SSD_N), F32)],
        compiler_params=_cparams(("arbitrary",)),
        name="ssd_layer_fwd",
    )(xbc, xbc, xbc, proj, dtb, alog, dsk, proj, wn)


def _ssd_layer_bwd(xbc, proj, dtb, alog, dsk, wn, h_saved, dyn):
    L = xbc.shape[0]
    nc = L // CHUNK

    def body(xs_ref, b_ref, c_ref, dt_ref, dtb_ref, alog_ref, dsk_ref, z_ref, wn_ref, hs_ref, dyn_ref,
             dp_ref, dx_ref, ddtb_ref, dalog_ref, ddsk_ref, dwn_ref, dhst):
        @pl.when(pl.program_id(0) == 0)
        def _():
            dhst[...] = jnp.zeros(dhst.shape, F32)
            ddtb_ref[...] = jnp.zeros((1, 128), F32)
            dalog_ref[...] = jnp.zeros((1, 128), F32)
            ddsk_ref[...] = jnp.zeros((1, 128), F32)
            dwn_ref[...] = jnp.zeros((1, D_INNER), F32)

        _, vjp = jax.vjp(_ssd_step, xs_ref[...], b_ref[...], c_ref[...], dt_ref[...], dtb_ref[...], alog_ref[...],
                         dsk_ref[...], hs_ref[...], z_ref[...], wn_ref[...])
        dxs, db, dc, ddt, ddtb, dalog, ddsk, dh, dz, dwn = vjp((dyn_ref[...], dhst[...]))
        dx_ref[:, :D_INNER] = dxs
        dx_ref[:, D_INNER:D_INNER + SSD_BC] = db
        dx_ref[:, D_INNER + SSD_BC:] = dc
        dp_ref[:, :D_INNER] = dz.astype(BF16)
        dp_ref[:, D_INNER:SSD_DT_COL] = jnp.zeros((CHUNK, SSD_CONV_DIM), BF16)
        dp_ref[:, SSD_DT_COL:SSD_DT_COL + 128] = ddt.astype(BF16)
        dp_ref[:, SSD_DT_COL + 128:] = jnp.zeros((CHUNK, SSD_PROJ_PAD - SSD_DT_COL - 128), BF16)
        dhst[...] = dh
        ddtb_ref[...] += ddtb
        dalog_ref[...] += dalog
        ddsk_ref[...] += ddsk
        dwn_ref[...] += dwn

    rc = lambda c: nc - 1 - c
    vec = pl.BlockSpec((1, 128), lambda c: (0, 0))
    vshape = jax.ShapeDtypeStruct((1, 128), F32)
    return pl.pallas_call(
        body,
        grid=(nc,),
        in_specs=_ssd_specs(nc, True) + [
            pl.BlockSpec((None, D_INNER, SSD_N), lambda c: (rc(c), 0, 0)),
            pl.BlockSpec((CHUNK, D_INNER), lambda c: (rc(c), 0)),
        ],
        out_specs=[
            pl.BlockSpec((CHUNK, SSD_PROJ_PAD), lambda c: (rc(c), 0)),
            pl.BlockSpec((CHUNK, SSD_CONV_DIM), lambda c: (rc(c), 0)),
            vec, vec, vec,
            pl.BlockSpec((1, D_INNER), lambda c: (0, 0)),
        ],
        out_shape=[
            jax.ShapeDtypeStruct((L, SSD_PROJ_PAD), BF16),
            jax.ShapeDtypeStruct((L, SSD_CONV_DIM), F32),
            vshape, vshape, vshape,
            jax.ShapeDtypeStruct((1, D_INNER), F32),
        ],
        scratch_shapes=[pltpu.VMEM((D_INNER, SSD_N), F32)],
        compiler_params=_cparams(("arbitrary",)),
        name="ssd_layer_bwd",
    )(xbc, xbc, xbc, proj, dtb, alog, dsk, proj, wn, h_saved, dyn)


def _pick(n, options):
    for t in options:
        if n % t == 0:
            return t
    return n


def _mm(a, b, *, name, out_dtype=F32, add=None):
    M, K = a.shape
    N = b.shape[1]
    tm = min(MM_TOKEN_TILE, M)
    tn = _pick(N, MM_TILES)
    tk = _pick(K, MM_TILES)
    nk = K // tk

    def body(*refs):
        if add is None:
            a_ref, b_ref, o_ref, acc = refs
        else:
            a_ref, b_ref, add_ref, o_ref, acc = refs
        k = pl.program_id(2)
        p = _dot(_bf(a_ref[...]), _bf(b_ref[...]))

        def finish(r):
            if add is not None:
                r = r + add_ref[...]
            o_ref[...] = r.astype(out_dtype)

        if nk == 1:
            finish(p)
            return

        @pl.when(k == 0)
        def _():
            acc[...] = p

        @pl.when((k > 0) & (k < nk - 1))
        def _():
            acc[...] += p

        @pl.when(k == nk - 1)
        def _():
            finish(acc[...] + p)

    in_specs = [pl.BlockSpec((tm, tk), lambda i, j, k: (i, k)), pl.BlockSpec((tk, tn), lambda i, j, k: (k, j))]
    args = [a, b]
    if add is not None:
        in_specs.append(pl.BlockSpec((tm, tn), lambda i, j, k: (i, j)))
        args.append(add)
    return pl.pallas_call(
        body,
        grid=(M // tm, N // tn, nk),
        in_specs=in_specs,
        out_specs=pl.BlockSpec((tm, tn), lambda i, j, k: (i, j)),
        out_shape=jax.ShapeDtypeStruct((M, N), out_dtype),
        scratch_shapes=[pltpu.VMEM((tm, tn), F32)],
        compiler_params=_cparams(("parallel", "parallel", "arbitrary")),
        name=name,
    )(*args)


def _mm_tn(a, b, *, name):
    M, K = a.shape
    N = b.shape[1]
    tm = min(MM_TOKEN_TILE, M)
    tn = _pick(N, MM_TILES)

    def body(a_ref, b_ref, o_ref):
        _acc_out(o_ref, _dot_tn(_bf(a_ref[...]), _bf(b_ref[...])), pl.program_id(1) == 0)

    return pl.pallas_call(
        body,
        grid=(N // tn, M // tm),
        in_specs=[pl.BlockSpec((tm, K), lambda j, i: (i, 0)), pl.BlockSpec((tm, tn), lambda j, i: (i, j))],
        out_specs=pl.BlockSpec((K, tn), lambda j, i: (0, j)),
        out_shape=jax.ShapeDtypeStruct((K, N), F32),
        compiler_params=_cparams(("parallel", "arbitrary")),
        name=name,
    )(a, b)


def _rms_fwd(x, w, *, name):
    L, D = x.shape
    tm = min(TOKEN_TILE, L)

    def body(x_ref, w_ref, o_ref):
        o_ref[...] = _rms(x_ref[...], w_ref[...]).astype(BF16)

    return pl.pallas_call(
        body, grid=(L // tm,),
        in_specs=[pl.BlockSpec((tm, D), lambda i: (i, 0)), pl.BlockSpec((1, D), lambda i: (0, 0))],
        out_specs=pl.BlockSpec((tm, D), lambda i: (i, 0)),
        out_shape=jax.ShapeDtypeStruct((L, D), BF16),
        compiler_params=_cparams(("parallel",)), name=name,
    )(x, w)


def _rms_bwd(x, w, dhn, dres, *, name):
    L, D = x.shape
    tm = min(TOKEN_TILE, L)

    def body(x_ref, w_ref, dhn_ref, dres_ref, dx_ref, dw_ref):
        _, vjp = jax.vjp(_rms, x_ref[...], w_ref[...])
        dx, dw = vjp(dhn_ref[...])
        dx_ref[...] = dx + dres_ref[...]
        _acc_out(dw_ref, dw, pl.program_id(0) == 0)

    row = pl.BlockSpec((tm, D), lambda i: (i, 0))
    vec = pl.BlockSpec((1, D), lambda i: (0, 0))
    return pl.pallas_call(
        body, grid=(L // tm,),
        in_specs=[row, vec, row, row],
        out_specs=[row, vec],
        out_shape=[jax.ShapeDtypeStruct((L, D), F32), jax.ShapeDtypeStruct((1, D), F32)],
        compiler_params=_cparams(("arbitrary",)), name=name,
    )(x, w, dhn, dres)


CONV_HALO = 8
CONV_COLS = 1024


def _conv_tile(xin, halo, w, b):
    tm = xin.shape[0]
    xx = jnp.concatenate([halo, xin], axis=0)
    u = b
    for k in range(SSD_CONV):
        off = CONV_HALO - (SSD_CONV - 1) + k
        u = u + w[k:k + 1] * xx[off:off + tm]
    return _silu(u)


def _conv_fwd(proj, w, b):
    L = proj.shape[0]
    tm = min(TOKEN_TILE, L)
    c0 = D_INNER // CONV_COLS
    hb = tm // CONV_HALO

    def body(x_ref, h_ref, w_ref, b_ref, o_ref):
        halo = jnp.where(pl.program_id(1) == 0, 0.0, h_ref[...])
        o_ref[...] = _conv_tile(x_ref[...], halo, w_ref[...], b_ref[...])

    return pl.pallas_call(
        body, grid=(SSD_CONV_DIM // CONV_COLS, L // tm),
        in_specs=[
            pl.BlockSpec((tm, CONV_COLS), lambda j, i: (i, c0 + j)),
            pl.BlockSpec((CONV_HALO, CONV_COLS), lambda j, i: (jnp.maximum(i * hb - 1, 0), c0 + j)),
            pl.BlockSpec((SSD_CONV, CONV_COLS), lambda j, i: (0, j)),
            pl.BlockSpec((1, CONV_COLS), lambda j, i: (0, j)),
        ],
        out_specs=pl.BlockSpec((tm, CONV_COLS), lambda j, i: (i, j)),
        out_shape=jax.ShapeDtypeStruct((L, SSD_CONV_DIM), F32),
        compiler_params=_cparams(("parallel", "parallel")), name="conv_fwd",
    )(proj, proj, w, b)


def _conv_bwd(proj, w, b, dxbc, dproj):
    L = proj.shape[0]
    tm = min(TOKEN_TILE, L)
    nt = L // tm
    c0 = D_INNER // CONV_COLS
    hb = tm // CONV_HALO

    def body(x_ref, h_ref, w_ref, b_ref, dy_ref, dp_in_ref, dp_ref, dw_ref, db_ref, carry):
        del dp_in_ref
        i = pl.program_id(1)
        first = i == 0

        @pl.when(first)
        def _():
            carry[...] = jnp.zeros((CONV_HALO, CONV_COLS), F32)

        halo = jnp.where(i == nt - 1, 0.0, h_ref[...])
        _, vjp = jax.vjp(_conv_tile, x_ref[...], halo, w_ref[...], b_ref[...])
        dx, dhalo, dw, db = vjp(dy_ref[...])
        dp_ref[...] = jnp.concatenate([dx[:tm - CONV_HALO], dx[tm - CONV_HALO:] + carry[...]], axis=0).astype(BF16)
        carry[...] = dhalo
        _acc_out(dw_ref, dw, first)
        _acc_out(db_ref, db, first)

    rt = lambda i: nt - 1 - i
    return pl.pallas_call(
        body, grid=(SSD_CONV_DIM // CONV_COLS, nt),
        in_specs=[
            pl.BlockSpec((tm, CONV_COLS), lambda j, i: (rt(i), c0 + j)),
            pl.BlockSpec((CONV_HALO, CONV_COLS), lambda j, i: (jnp.maximum(rt(i) * hb - 1, 0), c0 + j)),
            pl.BlockSpec((SSD_CONV, CONV_COLS), lambda j, i: (0, j)),
            pl.BlockSpec((1, CONV_COLS), lambda j, i: (0, j)),
            pl.BlockSpec((tm, CONV_COLS), lambda j, i: (rt(i), j)),
            pl.BlockSpec(memory_space=pl.ANY),
        ],
        out_specs=[
            pl.BlockSpec((tm, CONV_COLS), lambda j, i: (rt(i), c0 + j)),
            pl.BlockSpec((SSD_CONV, CONV_COLS), lambda j, i: (0, j)),
            pl.BlockSpec((1, CONV_COLS), lambda j, i: (0, j)),
        ],
        out_shape=[jax.ShapeDtypeStruct((L, SSD_PROJ_PAD), BF16), jax.ShapeDtypeStruct((SSD_CONV, SSD_CONV_DIM), F32),
                   jax.ShapeDtypeStruct((1, SSD_CONV_DIM), F32)],
        scratch_shapes=[pltpu.VMEM((CONV_HALO, CONV_COLS), F32)],
        input_output_aliases={5: 0},
        compiler_params=_cparams(("arbitrary", "arbitrary")), name="conv_bwd",
    )(proj, proj, w, b, dxbc, dproj)


def _loss_bwd(x, tgt, w):
    L, D = x.shape
    tm = min(TOKEN_TILE, L)

    def body(x_ref, t_ref, w_ref, l_ref, dx_ref, dw_ref):
        xv = x_ref[...]
        wv = w_ref[...]
        r = lax.rsqrt(jnp.mean(xv * xv, axis=-1, keepdims=True) + RMS_EPS)
        xh = xv * r
        e = xh * wv - t_ref[...]
        lsum = 0.5 * jnp.sum(jnp.mean(e * e, axis=-1, keepdims=True), axis=0, keepdims=True)
        dout = e * (1.0 / D)
        gx = dout * wv
        dx_ref[...] = r * (gx - xh * jnp.mean(gx * xh, axis=-1, keepdims=True))
        first = pl.program_id(0) == 0
        _acc_out(dw_ref, jnp.sum(dout * xh, axis=0, keepdims=True), first)
        _acc_out(l_ref, jnp.broadcast_to(lsum, (8, 128)), first)

    row = pl.BlockSpec((tm, D), lambda i: (i, 0))
    vec = pl.BlockSpec((1, D), lambda i: (0, 0))
    return pl.pallas_call(
        body, grid=(L // tm,),
        in_specs=[row, row, vec],
        out_specs=[pl.BlockSpec((8, 128), lambda i: (0, 0)), row, vec],
        out_shape=[jax.ShapeDtypeStruct((8, 128), F32), jax.ShapeDtypeStruct((L, D), F32), jax.ShapeDtypeStruct((1, D), F32)],
        compiler_params=_cparams(("arbitrary",)), name="loss_bwd",
    )(x, tgt, w)


MESH = pl.DeviceIdType.MESH
ANY = pl.BlockSpec(memory_space=pl.ANY)


def _all_gather(xs, *, name):
    n = len(xs)

    def body(*refs):
        x_refs, out_refs = refs[:n], refs[n:2 * n]
        send_sems, recv_sems, local_sems = refs[2 * n:]
        x, y, c = lax.axis_index("x"), lax.axis_index("y"), lax.axis_index("c")
        me, sibling = (x, y, c), (x, y, 1 - c)
        chips = [(1 - x, y), (x, 1 - y), (1 - x, 1 - y)]

        def slot(i, px, py, pc):
            return out_refs[i].at[4 * px + 2 * py + pc]

        def copy(i, k, block, to, src=None):
            return pltpu.make_async_remote_copy(
                src_ref=slot(i, *block) if src is None else src, dst_ref=slot(i, *block),
                send_sem=send_sems.at[i, k], recv_sem=recv_sems.at[i, k], device_id=to, device_id_type=MESH)

        mine = [pltpu.make_async_copy(x_refs[i], slot(i, *me), local_sems.at[i]) for i in range(n)]
        for cp in mine:
            cp.start()
        first = [copy(i, 0, me, sibling, src=x_refs[i]) for i in range(n)]
        first += [copy(i, 1 + j, me, (*chip, c), src=x_refs[i]) for j, chip in enumerate(chips) for i in range(n)]
        for cp in first:
            cp.start()
        passed = []
        for j, chip in enumerate(chips):
            for i in range(n):
                copy(i, 1 + j, (*chip, c), me).wait_recv()
                passed.append(copy(i, 4 + j, (*chip, c), sibling))
                passed[-1].start()
        for i in range(n):
            copy(i, 0, sibling, me).wait_recv()
        for j, chip in enumerate(chips):
            for i in range(n):
                copy(i, 4 + j, (*chip, 1 - c), me).wait_recv()
        for cp in first + passed:
            cp.wait_send()
        for cp in mine:
            cp.wait()

    return pl.pallas_call(
        body,
        out_shape=[jax.ShapeDtypeStruct((N_DEV,) + a.shape, a.dtype) for a in xs],
        in_specs=[ANY] * n, out_specs=[ANY] * n,
        scratch_shapes=[pltpu.SemaphoreType.DMA((n, 7)), pltpu.SemaphoreType.DMA((n, 7)), pltpu.SemaphoreType.DMA((n,))],
        name=name,
    )(*xs)


def _exchange(parts, *, name):
    n = len(parts)

    def body(*refs):
        p_refs, out_refs = refs[:n], refs[n:2 * n]
        send_sems, recv_sems, local_sems = refs[2 * n:]
        x, y, c = lax.axis_index("x"), lax.axis_index("y"), lax.axis_index("c")
        my = 4 * x + 2 * y + c
        mine = [pltpu.make_async_copy(p_refs[i].at[my], out_refs[i].at[my], local_sems.at[i]) for i in range(n)]
        for cp in mine:
            cp.start()

        def peer(k):
            fx, fy, fc = (k >> 2) & 1, (k >> 1) & 1, k & 1
            px, py, pc = (1 - x if fx else x), (1 - y if fy else y), (1 - c if fc else c)
            return (px, py, pc), 4 * px + 2 * py + pc

        for k in range(1, N_DEV):
            to, pid = peer(k)
            for i in range(n):
                pltpu.make_async_remote_copy(
                    src_ref=p_refs[i].at[pid], dst_ref=out_refs[i].at[my], send_sem=send_sems.at[i, k - 1],
                    recv_sem=recv_sems.at[i, k - 1], device_id=to, device_id_type=MESH).start()
        for k in range(1, N_DEV):
            to, pid = peer(k)
            for i in range(n):
                pltpu.make_async_remote_copy(
                    src_ref=p_refs[i].at[pid], dst_ref=out_refs[i].at[pid], send_sem=send_sems.at[i, k - 1],
                    recv_sem=recv_sems.at[i, k - 1], device_id=to, device_id_type=MESH).wait()
        for cp in mine:
            cp.wait()

    return pl.pallas_call(
        body,
        out_shape=[jax.ShapeDtypeStruct(a.shape, a.dtype) for a in parts],
        in_specs=[ANY] * n, out_specs=[ANY] * n,
        scratch_shapes=[pltpu.SemaphoreType.DMA((n, 7)), pltpu.SemaphoreType.DMA((n, 7)), pltpu.SemaphoreType.DMA((n,))],
        name=name,
    )(*parts)


def _adamw(parts, w, m, v, *, name):
    a, b = w.shape
    tr = _pick(a, (256, 128, 64, 32, 16, 8))

    def body(p_ref, w_ref, m_ref, v_ref, g_ref, d_ref, mo_ref, vo_ref):
        g = p_ref[0]
        for s in range(1, N_DEV):
            g = g + p_ref[s]
        mn = ADAM_B1 * m_ref[...] + (1.0 - ADAM_B1) * g
        vn = ADAM_B2 * v_ref[...] + (1.0 - ADAM_B2) * jnp.square(g)
        m_hat = mn / (1.0 - ADAM_B1 ** ADAM_STEP)
        v_hat = vn / (1.0 - ADAM_B2 ** ADAM_STEP)
        g_ref[...] = g
        d_ref[...] = -ADAM_LR * (m_hat / (jnp.sqrt(v_hat) + ADAM_EPS) + ADAM_WD * w_ref[...])
        mo_ref[...] = mn
        vo_ref[...] = vn

    blk = pl.BlockSpec((tr, b), lambda i: (i, 0))
    shp = jax.ShapeDtypeStruct((a, b), F32)
    return pl.pallas_call(
        body, grid=(a // tr,),
        in_specs=[pl.BlockSpec((N_DEV, tr, b), lambda i: (0, i, 0)), blk, blk, blk],
        out_specs=[blk, blk, blk, blk],
        out_shape=[shp, shp, shp, shp],
        compiler_params=_cparams(("parallel",)), name=name,
    )(parts, w, m, v)


def _col_shards(a, n):
    return a.reshape(a.shape[0], N_DEV, n).transpose(1, 0, 2)


def _from_col_shards(g, cols):
    r = g.shape[1]
    full = g.transpose(1, 0, 2).reshape(r, -1)
    return jnp.pad(full, ((0, 0), (0, cols - full.shape[1])))


def kernel(x, norm_w, gla_in_proj, gla_gate_up, gla_gate_bias, gla_head_norm, gla_out_proj, ssd_in_proj, ssd_conv_w, ssd_conv_b, ssd_dt_bias, ssd_a_log, ssd_d, ssd_gate_norm, ssd_out_proj, final_norm, loss_target, m_norm_w, m_gla_in_proj, m_gla_gate_up, m_gla_gate_bias, m_gla_head_norm, m_gla_out_proj, m_ssd_in_proj, m_ssd_conv_w, m_ssd_conv_b, m_ssd_dt_bias, m_ssd_a_log, m_ssd_d, m_ssd_gate_norm, m_ssd_out_proj, m_final_norm, v_norm_w, v_gla_in_proj, v_gla_gate_up, v_gla_gate_bias, v_gla_head_norm, v_gla_out_proj, v_ssd_in_proj, v_ssd_conv_w, v_ssd_conv_b, v_ssd_dt_bias, v_ssd_a_log, v_ssd_d, v_ssd_gate_norm, v_ssd_out_proj, v_final_norm):
    x0 = x[0]
    tgt = loss_target[0]
    n_gin = GLA_PROJ // N_DEV
    n_sin = SSD_PROJ // N_DEV
    n_up = GLA_DK // N_DEV
    n_cv = SSD_CONV_DIM // N_DEV

    g_gin, g_up, g_gout, g_sin, g_sout, g_cw, g_cb, g_gn = _all_gather(
        [gla_in_proj[0].astype(BF16), gla_gate_up[0].astype(BF16), gla_out_proj[0].astype(BF16),
         ssd_in_proj[0].astype(BF16), ssd_out_proj[0].astype(BF16), ssd_conv_w[0], ssd_conv_b, ssd_gate_norm],
        name="gather_weights")
    w_gin = _from_col_shards(g_gin, GLA_PROJ_PAD)
    wup = jnp.pad(_from_col_shards(g_up, GLA_DK), ((0, 128 - GLA_RANK), (0, 0))).astype(F32)
    w_gout = g_gout.reshape(D_INNER, D_MODEL)
    w_sin = _from_col_shards(g_sin, SSD_PROJ_PAD)
    w_sout = g_sout.reshape(D_INNER, D_MODEL)
    conv_w = _from_col_shards(g_cw, SSD_CONV_DIM)
    conv_b = g_cb.reshape(1, SSD_CONV_DIM)
    gate_norm = g_gn.reshape(1, D_INNER)
    vec128 = lambda a: jnp.pad(a.reshape(1, -1), ((0, 0), (0, 128 - a.size)))
    dtb, alog, dsk = vec128(ssd_dt_bias), vec128(ssd_a_log), vec128(ssd_d)
    nw0, nw1 = norm_w[0:1], norm_w[1:2]

    hn1 = _rms_fwd(x0, nw0, name="rms1_fwd")
    proj1 = _mm(hn1, w_gin, name="gla_in_proj")
    o, og, s_saved = _gla_layer_fwd(proj1, wup, gla_gate_bias, gla_head_norm)
    x1 = _mm(og, w_gout, add=x0, name="gla_out_proj")
    hn2 = _rms_fwd(x1, nw1, name="rms2_fwd")
    proj2 = _mm(hn2, w_sin, name="ssd_in_proj")
    xbc = _conv_fwd(proj2, conv_w, conv_b)
    yn, h_saved = _ssd_layer_fwd(xbc, proj2, dtb, alog, dsk, gate_norm)
    x2 = _mm(yn, w_sout, add=x1, name="ssd_out_proj")
    lsum, dx2, d_final = _loss_bwd(x2, tgt, final_norm.reshape(1, D_MODEL))
    loss = lax.psum(lsum[0, 0], ("x", "y", "c"))

    d_sout = _mm_tn(yn, dx2, name="ssd_out_proj_dw")
    dyn = _mm(dx2, w_sout.T, name="ssd_out_proj_dx")
    dproj2, dxbc, d_dtb, d_alog, d_dsk, d_gate_norm = _ssd_layer_bwd(xbc, proj2, dtb, alog, dsk, gate_norm, h_saved, dyn)
    dproj2, d_conv_w, d_conv_b = _conv_bwd(proj2, conv_w, conv_b, dxbc, dproj2)
    d_sin = _mm_tn(hn2, dproj2, name="ssd_in_proj_dw")
    dhn2 = _mm(dproj2, w_sin.T, name="ssd_in_proj_dx")
    dx1, d_nw1 = _rms_bwd(x1, nw1, dhn2, dx2, name="rms2_bwd")
    d_gout = _mm_tn(og, dx1, name="gla_out_proj_dw")
    dog = _mm(dx1, w_gout.T, name="gla_out_proj_dx")
    dproj1, d_wup, d_gbias, d_head_norm = _gla_layer_bwd(proj1, wup, gla_gate_bias, gla_head_norm, o, s_saved, dog)
    d_gin = _mm_tn(hn1, dproj1, name="gla_in_proj_dw")
    dhn1 = _mm(dproj1, w_gin.T, name="gla_in_proj_dx")
    dx0, d_nw0 = _rms_bwd(x0, nw0, dhn1, dx1, name="rms1_bwd")

    sharded = {
        "gla_in_proj": ((gla_in_proj[0], m_gla_in_proj[0], v_gla_in_proj[0]), _col_shards(d_gin[:, :GLA_PROJ], n_gin)),
        "gla_gate_up": ((gla_gate_up[0], m_gla_gate_up[0], v_gla_gate_up[0]), _col_shards(d_wup[:GLA_RANK], n_up)),
        "gla_out_proj": ((gla_out_proj[0], m_gla_out_proj[0], v_gla_out_proj[0]), d_gout.reshape(N_DEV, -1, D_MODEL)),
        "ssd_in_proj": ((ssd_in_proj[0], m_ssd_in_proj[0], v_ssd_in_proj[0]), _col_shards(d_sin[:, :SSD_PROJ], n_sin)),
        "ssd_conv_w": ((ssd_conv_w[0], m_ssd_conv_w[0], v_ssd_conv_w[0]), _col_shards(d_conv_w, n_cv)),
        "ssd_conv_b": ((ssd_conv_b, m_ssd_conv_b, v_ssd_conv_b), d_conv_b.reshape(N_DEV, 1, n_cv)),
        "ssd_gate_norm": ((ssd_gate_norm, m_ssd_gate_norm, v_ssd_gate_norm), d_gate_norm.reshape(N_DEV, 1, -1)),
        "ssd_out_proj": ((ssd_out_proj[0], m_ssd_out_proj[0], v_ssd_out_proj[0]), d_sout.reshape(N_DEV, -1, D_MODEL)),
    }
    heads = SSD_HEADS
    replicated = {
        "norm_w": ((norm_w, m_norm_w, v_norm_w), jnp.concatenate([d_nw0, d_nw1], axis=0)),
        "gla_gate_bias": ((gla_gate_bias, m_gla_gate_bias, v_gla_gate_bias), d_gbias),
        "gla_head_norm": ((gla_head_norm, m_gla_head_norm, v_gla_head_norm), d_head_norm),
        "ssd_dt_bias": ((ssd_dt_bias, m_ssd_dt_bias, v_ssd_dt_bias), d_dtb[:, :heads]),
        "ssd_a_log": ((ssd_a_log, m_ssd_a_log, v_ssd_a_log), d_alog[:, :heads]),
        "ssd_d": ((ssd_d, m_ssd_d, v_ssd_d), d_dsk[:, :heads]),
        "final_norm": (tuple(t.reshape(1, D_MODEL) for t in (final_norm, m_final_norm, v_final_norm)), d_final),
    }
    results = {}
    for group, recv in ((sharded, _exchange([p for _, p in sharded.values()], name="sharded_exchange")),
                        (replicated, _all_gather([p for _, p in replicated.values()], name="replicated_gather"))):
        for (nm, ((w, m, v), _)), r in zip(group.items(), recv):
            results[nm] = _adamw(r, w, m, v, name=nm + "_adamw")

    order = [("norm_w", norm_w), ("gla_in_proj", gla_in_proj), ("gla_gate_up", gla_gate_up), ("gla_gate_bias", gla_gate_bias),
             ("gla_head_norm", gla_head_norm), ("gla_out_proj", gla_out_proj), ("ssd_in_proj", ssd_in_proj),
             ("ssd_conv_w", ssd_conv_w), ("ssd_conv_b", ssd_conv_b), ("ssd_dt_bias", ssd_dt_bias), ("ssd_a_log", ssd_a_log),
             ("ssd_d", ssd_d), ("ssd_gate_norm", ssd_gate_norm), ("ssd_out_proj", ssd_out_proj), ("final_norm", final_norm)]
    out = [loss, dx0[None]]
    for i in range(4):
        out += [results[nm][i].reshape(ref.shape) for nm, ref in order]
    return tuple(out)
```

```python
import jax
import jax.numpy as jnp
from jax import lax
from jax.experimental import pallas as pl
from jax.experimental.pallas import tpu as pltpu

F32 = jnp.float32
BF16 = jnp.bfloat16

D_MODEL = 1024
D_INNER = 2048
RMS_EPS = 1e-6
GLA_HEADS = 4
GLA_DK = 512
GLA_HEAD_K = 128
GLA_HEAD_V = 512
GLA_RANK = 16
GLA_NORMALIZER = 16.0
CHUNK = 64
SUB = 16
GLA_PROJ = 5136
GLA_PROJ_PAD = 5376
GLA_GK_COL = 5120
SSD_HEADS = 32
SSD_GROUPS = 8
SSD_HPG = 4
SSD_P = 64
SSD_N = 128
SSD_CONV = 4
SSD_CONV_DIM = 4096
SSD_PROJ = 6176
SSD_PROJ_PAD = 6400
SSD_DT_COL = 6144
N_DEV = 8

ADAM_LR = 0.001
ADAM_B1 = 0.9
ADAM_B2 = 0.999
ADAM_EPS = 1e-08
ADAM_WD = 0.01
ADAM_STEP = 10

VMEM_LIMIT = 56 * 1024 * 1024
TOKEN_TILE = 512
MM_TOKEN_TILE = 1024
MM_TILES = (1792, 1280, 1024, 768, 512, 256, 128)


def _dot(a, b):
    return jnp.dot(a, b, preferred_element_type=F32)


def _dot_nt(a, b):
    return lax.dot_general(a, b, (((1,), (1,)), ((), ())), preferred_element_type=F32)


def _dot_tn(a, b):
    return lax.dot_general(a, b, (((0,), (0,)), ((), ())), preferred_element_type=F32)


def _bf(a):
    return a.astype(BF16)


@jax.custom_vjp
def _mxu(a, b):
    return _dot(_bf(a), _bf(b))


def _mxu_fwd(a, b):
    return _mxu(a, b), (a, b)


def _mxu_bwd(res, g):
    a, b = res
    return _dot_nt(_bf(g), _bf(b)), _dot_tn(_bf(a), _bf(g))


_mxu.defvjp(_mxu_fwd, _mxu_bwd)


@jax.custom_vjp
def _mxu_nt(a, b):
    return _dot_nt(_bf(a), _bf(b))


def _mxu_nt_fwd(a, b):
    return _mxu_nt(a, b), (a, b)


def _mxu_nt_bwd(res, g):
    a, b = res
    return _dot(_bf(g), _bf(b)), _dot_tn(_bf(g), _bf(a))


_mxu_nt.defvjp(_mxu_nt_fwd, _mxu_nt_bwd)


@jax.custom_vjp
def _mxu_tn(a, b):
    return _dot_tn(_bf(a), _bf(b))


def _mxu_tn_fwd(a, b):
    return _mxu_tn(a, b), (a, b)


def _mxu_tn_bwd(res, g):
    a, b = res
    return _dot_nt(_bf(b), _bf(g)), _dot(_bf(a), _bf(g))


_mxu_tn.defvjp(_mxu_tn_fwd, _mxu_tn_bwd)


def _split2(a):
    hi = _bf(a)
    return hi, _bf(a - hi.astype(F32))


def _three_pass(dot, a, b):
    ah, al = _split2(a)
    bh, bl = _split2(b)
    return dot(ah, bh) + (dot(ah, bl) + dot(al, bh))


@jax.custom_vjp
def _dot3_nt(a, b):
    return _three_pass(_dot_nt, a, b)


def _dot3_nt_fwd(a, b):
    return _dot3_nt(a, b), (a, b)


def _dot3_nt_bwd(res, g):
    a, b = res
    return _three_pass(_dot, g, b), _three_pass(_dot_tn, g, a)


_dot3_nt.defvjp(_dot3_nt_fwd, _dot3_nt_bwd)


def _silu(x):
    return x / (1.0 + jnp.exp(-x))


def _log_sigmoid(z):
    return jnp.minimum(z, 0.0) - jnp.log(1.0 + jnp.exp(-jnp.abs(z)))


def _softplus(z):
    return jnp.maximum(z, 0.0) + jnp.log(1.0 + jnp.exp(-jnp.abs(z)))


def _iota(shape, dim):
    return lax.broadcasted_iota(jnp.int32, shape, dim)


def _rms(x, w):
    return x * lax.rsqrt(jnp.mean(x * x, axis=-1, keepdims=True) + RMS_EPS) * w


def _scan_rows(a, reverse, seg):
    n = a.shape[0]
    pos = _iota(a.shape, 0) & (seg - 1)
    sh = 1
    while sh < seg:
        if reverse:
            a = a + jnp.where(pos < seg - sh, pltpu.roll(a, n - sh, 0), 0.0)
        else:
            a = a + jnp.where(pos >= sh, pltpu.roll(a, sh, 0), 0.0)
        sh *= 2
    return a


def _make_cumsum(seg):
    @jax.custom_vjp
    def cumsum(a):
        return _scan_rows(a, False, seg)

    cumsum.defvjp(lambda a: (_scan_rows(a, False, seg), None), lambda _, g: (_scan_rows(g, True, seg),))
    return cumsum


_cumsum_sub = _make_cumsum(SUB)
_cumsum_rows = _make_cumsum(CHUNK)


def _cparams(sem):
    return pltpu.CompilerParams(dimension_semantics=sem, vmem_limit_bytes=VMEM_LIMIT)


def _acc_out(ref, val, first):
    @pl.when(first)
    def _():
        ref[...] = val

    @pl.when(jnp.logical_not(first))
    def _():
        ref[...] += val


def _gla_chunk(q, k, v, gk, wup, bias, St):
    nb = CHUNK // SUB
    z = _mxu(gk, wup) + bias
    la = _log_sigmoid(z) * (1.0 / GLA_NORMALIZER)
    qs = q * (GLA_HEAD_K ** -0.5)
    bl = _cumsum_sub(la)
    tot = [jnp.sum(la[i * SUB:(i + 1) * SUB], axis=0, keepdims=True) for i in range(nb)]
    pre = [jnp.zeros((1, GLA_HEAD_K), F32)]
    for i in range(nb):
        pre.append(pre[i] + tot[i])
    b_last = pre[nb]
    rows_of = lambda vals: jnp.concatenate([jnp.broadcast_to(t, (SUB, GLA_HEAD_K)) for t in vals], axis=0)
    suf = rows_of(tot) - bl
    nxt = rows_of(pre[1:])
    o = _mxu_nt(qs * jnp.exp(bl + rows_of(pre[:nb])), St)
    St_new = St * jnp.exp(b_last) + _mxu_tn(v, k * jnp.exp(suf + (b_last - nxt)))
    qa = qs * jnp.exp(bl)
    row = _iota((CHUNK, GLA_HEAD_K), 0)
    rs = _iota((SUB, GLA_HEAD_K), 0)
    cs = _iota((SUB, CHUNK), 1)
    a_rows = []
    for i in range(nb):
        sl = slice(i * SUB, (i + 1) * SUB)
        q_i, k_i, bl_i = qs[sl], k[sl], bl[sl]
        if i > 0:
            kp = jnp.where(row < i * SUB, k * jnp.exp(jnp.minimum(suf + (pre[i] - nxt), 0.0)), 0.0)
            a_i = _dot3_nt(qa[sl], kp)
        else:
            a_i = jnp.zeros((SUB, CHUNK), F32)
        for j in range(SUB):
            e = jnp.exp(jnp.minimum(bl_i - bl_i[j:j + 1], 0.0))
            t = jnp.where(rs >= j, q_i * e * k_i[j:j + 1], 0.0)
            a_i = a_i + jnp.where(cs == i * SUB + j, jnp.sum(t, axis=-1, keepdims=True), 0.0)
        a_rows.append(a_i)
    o = o + _mxu(jnp.concatenate(a_rows, axis=0), v)
    return o, St_new


def _gla_post(o, g, wn):
    return _rms(o, wn) * _silu(g)


GLA_HALF = 2 * GLA_HEAD_V


def _gla_specs(nc, rev):
    ci = (lambda c: nc - 1 - c) if rev else (lambda c: c)
    v0 = 2 * GLA_DK // GLA_HALF
    g0 = (2 * GLA_DK + D_INNER) // GLA_HALF
    return [
        pl.BlockSpec((CHUNK, GLA_DK), lambda c: (ci(c), 0)),
        pl.BlockSpec((CHUNK, GLA_DK), lambda c: (ci(c), 1)),
        pl.BlockSpec((CHUNK, GLA_HALF), lambda c: (ci(c), v0)),
        pl.BlockSpec((CHUNK, GLA_HALF), lambda c: (ci(c), v0 + 1)),
        pl.BlockSpec((CHUNK, GLA_HALF), lambda c: (ci(c), g0)),
        pl.BlockSpec((CHUNK, GLA_HALF), lambda c: (ci(c), g0 + 1)),
        pl.BlockSpec((CHUNK, 128), lambda c: (ci(c), GLA_GK_COL // 128)),
        pl.BlockSpec((128, GLA_DK), lambda c: (0, 0)),
        pl.BlockSpec((1, GLA_DK), lambda c: (0, 0)),
        pl.BlockSpec((1, GLA_HEAD_V), lambda c: (0, 0)),
    ]


def _head_cols(ref_a, ref_b, h):
    ref = ref_a if h < 2 else ref_b
    return ref[:, (h % 2) * GLA_HEAD_V:(h % 2 + 1) * GLA_HEAD_V]


def _gla_layer_fwd(proj, wup, bias, wn, gather):
    L = proj.shape[0]
    nc = L // CHUNK
    n = len(gather)

    def body(*refs):
        q_ref, k_ref, va_ref, vb_ref, ga_ref, gb_ref, gk_ref, wup_ref, b_ref, wn_ref = refs[:10]
        x_refs = refs[10:10 + n]
        o_ref, og_ref, s_ref = refs[10 + n:13 + n]
        out_refs = refs[13 + n:13 + 2 * n]
        st, send_sems, recv_sems, local_sems = refs[13 + 2 * n:]
        start, finish = _gather_ops(x_refs, out_refs, send_sems, recv_sems, local_sems)

        @pl.when(pl.program_id(0) == 0)
        def _():
            st[...] = jnp.zeros(st.shape, F32)
            start()

        gk = gk_ref[...]
        for h in range(GLA_HEADS):
            kc = slice(h * GLA_HEAD_K, (h + 1) * GLA_HEAD_K)
            vc = slice(h * GLA_HEAD_V, (h + 1) * GLA_HEAD_V)
            s_in = st[h]
            s_ref[h] = s_in
            o, s_new = _gla_chunk(q_ref[:, kc], k_ref[:, kc], _head_cols(va_ref, vb_ref, h), gk, wup_ref[:, kc], b_ref[:, kc], s_in)
            st[h] = s_new
            o_ref[:, vc] = o
            og_ref[:, vc] = _gla_post(o, _head_cols(ga_ref, gb_ref, h), wn_ref[...]).astype(BF16)

        @pl.when(pl.program_id(0) == nc - 1)
        def _():
            finish()

    res = pl.pallas_call(
        body,
        grid=(nc,),
        in_specs=_gla_specs(nc, False) + [ANY] * n,
        out_specs=[
            pl.BlockSpec((CHUNK, D_INNER), lambda c: (c, 0)),
            pl.BlockSpec((CHUNK, D_INNER), lambda c: (c, 0)),
            pl.BlockSpec((None, GLA_HEADS, GLA_HEAD_V, GLA_HEAD_K), lambda c: (c, 0, 0, 0)),
        ] + [ANY] * n,
        out_shape=[
            jax.ShapeDtypeStruct((L, D_INNER), F32),
            jax.ShapeDtypeStruct((L, D_INNER), BF16),
            jax.ShapeDtypeStruct((nc, GLA_HEADS, GLA_HEAD_V, GLA_HEAD_K), F32),
        ] + [jax.ShapeDtypeStruct((N_DEV,) + a.shape, a.dtype) for a in gather],
        scratch_shapes=[pltpu.VMEM((GLA_HEADS, GLA_HEAD_V, GLA_HEAD_K), F32)] + _comm_sems(n),
        compiler_params=_cparams(("arbitrary",)),
        name="gla_layer_fwd",
    )(proj, proj, proj, proj, proj, proj, proj, wup, bias, wn, *gather)
    return res[:3], res[3:]


def _gla_layer_bwd(proj, wup, bias, wn, o, s_in, dog, exchange):
    L = proj.shape[0]
    nc = L // CHUNK
    n = len(exchange)

    def body(*refs):
        (q_ref, k_ref, va_ref, vb_ref, ga_ref, gb_ref, gk_ref, wup_ref, b_ref, wn_ref, o_ref, s_ref, dog_ref) = refs[:13]
        p_refs = refs[13:13 + n]
        dp_ref, dwup_ref, db_ref, dwn_ref = refs[13 + n:17 + n]
        out_refs = refs[17 + n:17 + 2 * n]
        dst, send_sems, recv_sems, local_sems = refs[17 + 2 * n:]
        start, finish = _exchange_ops(p_refs, out_refs, send_sems, recv_sems, local_sems)

        @pl.when(pl.program_id(0) == 0)
        def _():
            dst[...] = jnp.zeros(dst.shape, F32)
            dwup_ref[...] = jnp.zeros(dwup_ref.shape, F32)
            db_ref[...] = jnp.zeros(db_ref.shape, F32)
            dwn_ref[...] = jnp.zeros(dwn_ref.shape, F32)
            start()

        gk = gk_ref[...]
        dgk_sum = jnp.zeros((CHUNK, 128), F32)
        for h in range(GLA_HEADS):
            kc = slice(h * GLA_HEAD_K, (h + 1) * GLA_HEAD_K)
            vc = slice(h * GLA_HEAD_V, (h + 1) * GLA_HEAD_V)
            _, post_vjp = jax.vjp(_gla_post, o_ref[:, vc], _head_cols(ga_ref, gb_ref, h), wn_ref[...])
            do, dg, dwn = post_vjp(dog_ref[:, vc])
            _, vjp = jax.vjp(_gla_chunk, q_ref[:, kc], k_ref[:, kc], _head_cols(va_ref, vb_ref, h), gk, wup_ref[:, kc],
                             b_ref[:, kc], s_ref[h])
            dq, dk, dv, dgk, dwup, db, ds = vjp((do, dst[h]))
            dst[h] = ds
            dp_ref[:, kc] = dq.astype(BF16)
            dp_ref[:, GLA_DK + h * GLA_HEAD_K:GLA_DK + (h + 1) * GLA_HEAD_K] = dk.astype(BF16)
            dp_ref[:, 2 * GLA_DK + h * GLA_HEAD_V:2 * GLA_DK + (h + 1) * GLA_HEAD_V] = dv.astype(BF16)
            dp_ref[:, 2 * GLA_DK + D_INNER + h * GLA_HEAD_V:2 * GLA_DK + D_INNER + (h + 1) * GLA_HEAD_V] = dg.astype(BF16)
            dwup_ref[:, kc] += dwup
            db_ref[:, kc] += db
            dwn_ref[...] += dwn
            dgk_sum = dgk_sum + dgk
        dp_ref[:, GLA_GK_COL:GLA_GK_COL + 128] = dgk_sum.astype(BF16)
        dp_ref[:, GLA_GK_COL + 128:] = jnp.zeros((CHUNK, GLA_PROJ_PAD - GLA_GK_COL - 128), BF16)

        @pl.when(pl.program_id(0) == nc - 1)
        def _():
            finish()

    rc = lambda c: nc - 1 - c
    res = pl.pallas_call(
        body,
        grid=(nc,),
        in_specs=_gla_specs(nc, True) + [
            pl.BlockSpec((CHUNK, D_INNER), lambda c: (rc(c), 0)),
            pl.BlockSpec((None, GLA_HEADS, GLA_HEAD_V, GLA_HEAD_K), lambda c: (rc(c), 0, 0, 0)),
            pl.BlockSpec((CHUNK, D_INNER), lambda c: (rc(c), 0)),
        ] + [ANY] * n,
        out_specs=[
            pl.BlockSpec((CHUNK, GLA_PROJ_PAD), lambda c: (rc(c), 0)),
            pl.BlockSpec((128, GLA_DK), lambda c: (0, 0)),
            pl.BlockSpec((1, GLA_DK), lambda c: (0, 0)),
            pl.BlockSpec((1, GLA_HEAD_V), lambda c: (0, 0)),
        ] + [ANY] * n,
        out_shape=[
            jax.ShapeDtypeStruct((L, GLA_PROJ_PAD), BF16),
            jax.ShapeDtypeStruct((128, GLA_DK), F32),
            jax.ShapeDtypeStruct((1, GLA_DK), F32),
            jax.ShapeDtypeStruct((1, GLA_HEAD_V), F32),
        ] + [jax.ShapeDtypeStruct(a.shape, a.dtype) for a in exchange],
        scratch_shapes=[pltpu.VMEM((GLA_HEADS, GLA_HEAD_V, GLA_HEAD_K), F32)] + _comm_sems(n),
        compiler_params=_cparams(("arbitrary",)),
        name="gla_layer_bwd",
    )(proj, proj, proj, proj, proj, proj, proj, wup, bias, wn, o, s_in, dog, *exchange)
    return res[:4], res[4:]


@jax.custom_vjp
def _expand(v):
    r = v.shape[0]
    left = _iota((r, 128), 1) < SSD_P
    slabs = []
    for p in range(SSD_HEADS // 2):
        a = jnp.broadcast_to(v[:, 2 * p:2 * p + 1], (r, 128))
        b = jnp.broadcast_to(v[:, 2 * p + 1:2 * p + 2], (r, 128))
        slabs.append(jnp.where(left, a, b))
    return jnp.concatenate(slabs, axis=1)


def _expand_fwd(v):
    return _expand(v), None


def _expand_bwd(_, g):
    r = g.shape[0]
    lane = _iota((r, 128), 1)
    left = lane < SSD_P
    dv = jnp.zeros((r, 128), F32)
    for p in range(SSD_HEADS // 2):
        gs = g[:, 128 * p:128 * (p + 1)]
        sa = jnp.sum(jnp.where(left, gs, 0.0), axis=-1, keepdims=True)
        sb = jnp.sum(jnp.where(left, 0.0, gs), axis=-1, keepdims=True)
        dv = dv + jnp.where(lane == 2 * p, sa, 0.0) + jnp.where(lane == 2 * p + 1, sb, 0.0)
    return (dv,)


_expand.defvjp(_expand_fwd, _expand_bwd)

SSD_GW = SSD_HPG * SSD_P
SSD_BC = SSD_GROUPS * SSD_N


def _ssd_chunk(xs, Bm, Cm, dtp, dtb, alog, dsk, h_in):
    dt = _softplus(dtp + dtb)
    acum = _cumsum_rows(dt * (-jnp.exp(alog)))
    a_last = acum[CHUNK - 1:CHUNK]
    acum_b = _expand(acum)
    w_end = _expand(dt * jnp.exp(a_last - acum))
    d_b = _expand(jnp.broadcast_to(dsk, (8, 128)))[0:1]
    ac_t = jnp.concatenate([acum, acum], axis=0).T
    dt_t = jnp.concatenate([dt, dt], axis=0).T
    lane = _iota((CHUNK, 128), 1)
    left = lane < SSD_P
    causal = (lane & (SSD_P - 1)) <= _iota((CHUNK, 128), 0)
    ys, sts = [], []
    for g in range(SSD_GROUPS):
        Bg = Bm[:, g * SSD_N:(g + 1) * SSD_N]
        Cg = Cm[:, g * SSD_N:(g + 1) * SSD_N]
        cb2 = _mxu_nt(Cg, jnp.concatenate([Bg, Bg], axis=0))
        y_off = _mxu_nt(Cg, h_in[g * SSD_GW:(g + 1) * SSD_GW])
        for k in range(SSD_HPG // 2):
            p = g * (SSD_HPG // 2) + k
            sl = slice(128 * p, 128 * (p + 1))
            ac_c = acum_b[:, sl]
            ac_r = jnp.where(left, ac_t[2 * p:2 * p + 1], ac_t[2 * p + 1:2 * p + 2])
            dt_r = jnp.where(left, dt_t[2 * p:2 * p + 1], dt_t[2 * p + 1:2 * p + 2])
            m2 = cb2 * jnp.where(causal, jnp.exp(jnp.minimum(ac_c - ac_r, 0.0)), 0.0) * dt_r
            xsl = xs[:, sl]
            x2 = jnp.concatenate([jnp.where(left, xsl, 0.0), jnp.where(left, 0.0, xsl)], axis=0)
            ys.append(_mxu(m2, x2) + y_off[:, 128 * k:128 * (k + 1)] * jnp.exp(ac_c) + xsl * d_b[:, sl])
        gs = slice(g * SSD_GW, (g + 1) * SSD_GW)
        sts.append(_mxu_tn(xs[:, gs] * w_end[:, gs], Bg))
    cd = jnp.exp(ac_t[:, CHUNK - 1:CHUNK])
    hs = [h_in[h * SSD_P:(h + 1) * SSD_P] * cd[h:h + 1] for h in range(SSD_HEADS)]
    h_out = jnp.concatenate(hs, axis=0) + jnp.concatenate(sts, axis=0)
    return jnp.concatenate(ys, axis=1), h_out


def _ssd_step(xs, Bm, Cm, dtp, dtb, alog, dsk, h_in, z, wn):
    y, h_out = _ssd_chunk(xs, Bm, Cm, dtp, dtb, alog, dsk, h_in)
    return _rms(y * _silu(z), wn), h_out


def _ssd_specs(nc, rev):
    ci = (lambda c: nc - 1 - c) if rev else (lambda c: c)
    vec = pl.BlockSpec((1, 128), lambda c: (0, 0))
    return [
        pl.BlockSpec((CHUNK, D_INNER), lambda c: (ci(c), 0)),
        pl.BlockSpec((CHUNK, SSD_BC), lambda c: (ci(c), D_INNER // SSD_BC)),
        pl.BlockSpec((CHUNK, SSD_BC), lambda c: (ci(c), D_INNER // SSD_BC + 1)),
        pl.BlockSpec((CHUNK, 128), lambda c: (ci(c), SSD_DT_COL // 128)),
        vec, vec, vec,
        pl.BlockSpec((CHUNK, D_INNER), lambda c: (ci(c), 0)),
        pl.BlockSpec((1, D_INNER), lambda c: (0, 0)),
    ]


def _ssd_layer_fwd(xbc, proj, dtb, alog, dsk, wn):
    L = xbc.shape[0]
    nc = L // CHUNK

    def body(xs_ref, b_ref, c_ref, dt_ref, dtb_ref, alog_ref, dsk_ref, z_ref, wn_ref, y_ref, hs_ref, hst):
        @pl.when(pl.program_id(0) == 0)
        def _():
            hst[...] = jnp.zeros(hst.shape, F32)

        h_in = hst[...]
        hs_ref[...] = h_in
        yn, h_out = _ssd_step(xs_ref[...], b_ref[...], c_ref[...], dt_ref[...], dtb_ref[...], alog_ref[...], dsk_ref[...],
                              h_in, z_ref[...], wn_ref[...])
        y_ref[...] = yn.astype(BF16)
        hst[...] = h_out

    return pl.pallas_call(
        body,
        grid=(nc,),
        in_specs=_ssd_specs(nc, False),
        out_specs=[
            pl.BlockSpec((CHUNK, D_INNER), lambda c: (c, 0)),
            pl.BlockSpec((None, D_INNER, SSD_N), lambda c: (c, 0, 0)),
        ],
        out_shape=[
            jax.ShapeDtypeStruct((L, D_INNER), BF16),
            jax.ShapeDtypeStruct((nc, D_INNER, SSD_N), F32),
        ],
        scratch_shapes=[pltpu.VMEM((D_INNER, SSD_N), F32)],
        compiler_params=_cparams(("arbitrary",)),
        name="ssd_layer_fwd",
    )(xbc, xbc, xbc, proj, dtb, alog, dsk, proj, wn)


def _ssd_layer_bwd(xbc, proj, dtb, alog, dsk, wn, h_saved, dyn):
    L = xbc.shape[0]
    nc = L // CHUNK

    def body(xs_ref, b_ref, c_ref, dt_ref, dtb_ref, alog_ref, dsk_ref, z_ref, wn_ref, hs_ref, dyn_ref,
             dp_ref, dx_ref, ddtb_ref, dalog_ref, ddsk_ref, dwn_ref, dhst):
        @pl.when(pl.program_id(0) == 0)
        def _():
            dhst[...] = jnp.zeros(dhst.shape, F32)
            ddtb_ref[...] = jnp.zeros((1, 128), F32)
            dalog_ref[...] = jnp.zeros((1, 128), F32)
            ddsk_ref[...] = jnp.zeros((1, 128), F32)
            dwn_ref[...] = jnp.zeros((1, D_INNER), F32)

        _, vjp = jax.vjp(_ssd_step, xs_ref[...], b_ref[...], c_ref[...], dt_ref[...], dtb_ref[...], alog_ref[...],
                         dsk_ref[...], hs_ref[...], z_ref[...], wn_ref[...])
        dxs, db, dc, ddt, ddtb, dalog, ddsk, dh, dz, dwn = vjp((dyn_ref[...], dhst[...]))
        dx_ref[:, :D_INNER] = dxs
        dx_ref[:, D_INNER:D_INNER + SSD_BC] = db
        dx_ref[:, D_INNER + SSD_BC:] = dc
        dp_ref[:, :D_INNER] = dz.astype(BF16)
        dp_ref[:, D_INNER:SSD_DT_COL] = jnp.zeros((CHUNK, SSD_CONV_DIM), BF16)
        dp_ref[:, SSD_DT_COL:SSD_DT_COL + 128] = ddt.astype(BF16)
        dp_ref[:, SSD_DT_COL + 128:] = jnp.zeros((CHUNK, SSD_PROJ_PAD - SSD_DT_COL - 128), BF16)
        dhst[...] = dh
        ddtb_ref[...] += ddtb
        dalog_ref[...] += dalog
        ddsk_ref[...] += ddsk
        dwn_ref[...] += dwn

    rc = lambda c: nc - 1 - c
    vec = pl.BlockSpec((1, 128), lambda c: (0, 0))
    vshape = jax.ShapeDtypeStruct((1, 128), F32)
    return pl.pallas_call(
        body,
        grid=(nc,),
        in_specs=_ssd_specs(nc, True) + [
            pl.BlockSpec((None, D_INNER, SSD_N), lambda c: (rc(c), 0, 0)),
            pl.BlockSpec((CHUNK, D_INNER), lambda c: (rc(c), 0)),
        ],
        out_specs=[
            pl.BlockSpec((CHUNK, SSD_PROJ_PAD), lambda c: (rc(c), 0)),
            pl.BlockSpec((CHUNK, SSD_CONV_DIM), lambda c: (rc(c), 0)),
            vec, vec, vec,
            pl.BlockSpec((1, D_INNER), lambda c: (0, 0)),
        ],
        out_shape=[
            jax.ShapeDtypeStruct((L, SSD_PROJ_PAD), BF16),
            jax.ShapeDtypeStruct((L, SSD_CONV_DIM), F32),
            vshape, vshape, vshape,
            jax.ShapeDtypeStruct((1, D_INNER), F32),
        ],
        scratch_shapes=[pltpu.VMEM((D_INNER, SSD_N), F32)],
        compiler_params=_cparams(("arbitrary",)),
        name="ssd_layer_bwd",
    )(xbc, xbc, xbc, proj, dtb, alog, dsk, proj, wn, h_saved, dyn)


def _pick(n, options):
    for t in options:
        if n % t == 0:
            return t
    return n


def _mm(a, b, *, name, out_dtype=F32, add=None):
    M, K = a.shape
    N = b.shape[1]
    tm = min(MM_TOKEN_TILE, M)
    tn = _pick(N, MM_TILES)
    tk = _pick(K, MM_TILES)
    nk = K // tk

    def body(*refs):
        if add is None:
            a_ref, b_ref, o_ref, acc = refs
        else:
            a_ref, b_ref, add_ref, o_ref, acc = refs
        k = pl.program_id(2)
        p = _dot(_bf(a_ref[...]), _bf(b_ref[...]))

        def finish(r):
            if add is not None:
                r = r + add_ref[...]
            o_ref[...] = r.astype(out_dtype)

        if nk == 1:
            finish(p)
            return

        @pl.when(k == 0)
        def _():
            acc[...] = p

        @pl.when((k > 0) & (k < nk - 1))
        def _():
            acc[...] += p

        @pl.when(k == nk - 1)
        def _():
            finish(acc[...] + p)

    in_specs = [pl.BlockSpec((tm, tk), lambda i, j, k: (i, k)), pl.BlockSpec((tk, tn), lambda i, j, k: (k, j))]
    args = [a, b]
    if add is not None:
        in_specs.append(pl.BlockSpec((tm, tn), lambda i, j, k: (i, j)))
        args.append(add)
    return pl.pallas_call(
        body,
        grid=(M // tm, N // tn, nk),
        in_specs=in_specs,
        out_specs=pl.BlockSpec((tm, tn), lambda i, j, k: (i, j)),
        out_shape=jax.ShapeDtypeStruct((M, N), out_dtype),
        scratch_shapes=[pltpu.VMEM((tm, tn), F32)],
        compiler_params=_cparams(("parallel", "parallel", "arbitrary")),
        name=name,
    )(*args)


def _mm_tn(a, b, *, name):
    M, K = a.shape
    N = b.shape[1]
    tm = min(MM_TOKEN_TILE, M)
    tn = _pick(N, MM_TILES)

    def body(a_ref, b_ref, o_ref):
        _acc_out(o_ref, _dot_tn(_bf(a_ref[...]), _bf(b_ref[...])), pl.program_id(1) == 0)

    return pl.pallas_call(
        body,
        grid=(N // tn, M // tm),
        in_specs=[pl.BlockSpec((tm, K), lambda j, i: (i, 0)), pl.BlockSpec((tm, tn), lambda j, i: (i, j))],
        out_specs=pl.BlockSpec((K, tn), lambda j, i: (0, j)),
        out_shape=jax.ShapeDtypeStruct((K, N), F32),
        compiler_params=_cparams(("parallel", "arbitrary")),
        name=name,
    )(a, b)


def _rms_fwd(x, w, *, name):
    L, D = x.shape
    tm = min(TOKEN_TILE, L)

    def body(x_ref, w_ref, o_ref):
        o_ref[...] = _rms(x_ref[...], w_ref[...]).astype(BF16)

    return pl.pallas_call(
        body, grid=(L // tm,),
        in_specs=[pl.BlockSpec((tm, D), lambda i: (i, 0)), pl.BlockSpec((1, D), lambda i: (0, 0))],
        out_specs=pl.BlockSpec((tm, D), lambda i: (i, 0)),
        out_shape=jax.ShapeDtypeStruct((L, D), BF16),
        compiler_params=_cparams(("parallel",)), name=name,
    )(x, w)


def _rms_bwd(x, w, dhn, dres, *, name):
    L, D = x.shape
    tm = min(TOKEN_TILE, L)

    def body(x_ref, w_ref, dhn_ref, dres_ref, dx_ref, dw_ref):
        _, vjp = jax.vjp(_rms, x_ref[...], w_ref[...])
        dx, dw = vjp(dhn_ref[...])
        dx_ref[...] = dx + dres_ref[...]
        _acc_out(dw_ref, dw, pl.program_id(0) == 0)

    row = pl.BlockSpec((tm, D), lambda i: (i, 0))
    vec = pl.BlockSpec((1, D), lambda i: (0, 0))
    return pl.pallas_call(
        body, grid=(L // tm,),
        in_specs=[row, vec, row, row],
        out_specs=[row, vec],
        out_shape=[jax.ShapeDtypeStruct((L, D), F32), jax.ShapeDtypeStruct((1, D), F32)],
        compiler_params=_cparams(("arbitrary",)), name=name,
    )(x, w, dhn, dres)


CONV_HALO = 8
CONV_COLS = 1024


CONV_RB = 64
CONV_CB = 256


def _conv_pieces(tm):
    return [(r0, c0) for c0 in range(0, CONV_COLS, CONV_CB) for r0 in range(0, tm, min(CONV_RB, tm))]


def _conv_fwd(proj, w, b):
    L = proj.shape[0]
    tm = min(TOKEN_TILE, L)
    rb = min(CONV_RB, tm)
    c0 = D_INNER // CONV_COLS
    hb = tm // CONV_HALO

    def body(x_ref, h_ref, w_ref, b_ref, o_ref, sp_ref, xx):
        xx[0:CONV_HALO] = jnp.where(pl.program_id(1) == 0, 0.0, h_ref[...])
        xx[CONV_HALO:CONV_HALO + tm] = x_ref[...]
        for r0, cc in _conv_pieces(tm):
            cs = slice(cc, cc + CONV_CB)
            u = b_ref[:, cs]
            for k in range(SSD_CONV):
                off = CONV_HALO - (SSD_CONV - 1) + k + r0
                u = u + w_ref[k:k + 1, cs] * xx[off:off + rb, cs]
            s = 1.0 / (1.0 + jnp.exp(-u))
            o_ref[r0:r0 + rb, cs] = u * s
            sp_ref[r0:r0 + rb, cs] = s * (1.0 + u * (1.0 - s))

    blk = pl.BlockSpec((tm, CONV_COLS), lambda j, i: (i, j))
    shp = jax.ShapeDtypeStruct((L, SSD_CONV_DIM), F32)
    return pl.pallas_call(
        body, grid=(SSD_CONV_DIM // CONV_COLS, L // tm),
        in_specs=[
            pl.BlockSpec((tm, CONV_COLS), lambda j, i: (i, c0 + j)),
            pl.BlockSpec((CONV_HALO, CONV_COLS), lambda j, i: (jnp.maximum(i * hb - 1, 0), c0 + j)),
            pl.BlockSpec((SSD_CONV, CONV_COLS), lambda j, i: (0, j)),
            pl.BlockSpec((1, CONV_COLS), lambda j, i: (0, j)),
        ],
        out_specs=[blk, blk],
        out_shape=[shp, shp],
        scratch_shapes=[pltpu.VMEM((CONV_HALO + tm, CONV_COLS), F32)],
        compiler_params=_cparams(("parallel", "parallel")), name="conv_fwd",
    )(proj, proj, w, b)


def _conv_bwd(proj, w, sp, dxbc, dproj):
    L = proj.shape[0]
    tm = min(TOKEN_TILE, L)
    rb = min(CONV_RB, tm)
    nt = L // tm
    c0 = D_INNER // CONV_COLS
    hb = tm // CONV_HALO

    def fold(a):
        return jnp.sum(a.reshape(rb // 8, 8, CONV_CB), axis=0)

    def body(x_ref, h_ref, w_ref, sp_ref, dy_ref, dp_in_ref, dp_ref, dw_ref, db_ref, xx, dd):
        del dp_in_ref
        i = pl.program_id(1)
        first = i == 0

        @pl.when(first)
        def _():
            dd[tm:tm + CONV_HALO] = jnp.zeros((CONV_HALO, CONV_COLS), F32)

        xx[0:CONV_HALO] = jnp.where(i == nt - 1, 0.0, h_ref[...])
        xx[CONV_HALO:CONV_HALO + tm] = x_ref[...]
        dws, dbs = [], []
        for cc in range(0, CONV_COLS, CONV_CB):
            cs = slice(cc, cc + CONV_CB)
            acc = [jnp.zeros((8, CONV_CB), F32) for _ in range(SSD_CONV + 1)]
            for r0 in range(0, tm, rb):
                du = dy_ref[r0:r0 + rb, cs] * sp_ref[r0:r0 + rb, cs]
                dd[r0:r0 + rb, cs] = du
                for k in range(SSD_CONV):
                    off = CONV_HALO - (SSD_CONV - 1) + k + r0
                    acc[k] = acc[k] + fold(du * xx[off:off + rb, cs])
                acc[SSD_CONV] = acc[SSD_CONV] + fold(du)
            dws.append(jnp.concatenate([jnp.sum(a, axis=0, keepdims=True) for a in acc[:SSD_CONV]], axis=0))
            dbs.append(jnp.sum(acc[SSD_CONV], axis=0, keepdims=True))
        for r0, cc in _conv_pieces(tm):
            cs = slice(cc, cc + CONV_CB)
            dx = jnp.zeros((rb, CONV_CB), F32)
            for k in range(SSD_CONV):
                off = SSD_CONV - 1 - k + r0
                dx = dx + w_ref[k:k + 1, cs] * dd[off:off + rb, cs]
            dp_ref[r0:r0 + rb, cs] = dx.astype(BF16)
        _acc_out(dw_ref, jnp.concatenate(dws, axis=1), first)
        _acc_out(db_ref, jnp.concatenate(dbs, axis=1), first)
        dd[tm:tm + CONV_HALO] = dd[0:CONV_HALO]

    rt = lambda i: nt - 1 - i
    return pl.pallas_call(
        body, grid=(SSD_CONV_DIM // CONV_COLS, nt),
        in_specs=[
            pl.BlockSpec((tm, CONV_COLS), lambda j, i: (rt(i), c0 + j)),
            pl.BlockSpec((CONV_HALO, CONV_COLS), lambda j, i: (jnp.maximum(rt(i) * hb - 1, 0), c0 + j)),
            pl.BlockSpec((SSD_CONV, CONV_COLS), lambda j, i: (0, j)),
            pl.BlockSpec((tm, CONV_COLS), lambda j, i: (rt(i), j)),
            pl.BlockSpec((tm, CONV_COLS), lambda j, i: (rt(i), j)),
            pl.BlockSpec(memory_space=pl.ANY),
        ],
        out_specs=[
            pl.BlockSpec((tm, CONV_COLS), lambda j, i: (rt(i), c0 + j)),
            pl.BlockSpec((SSD_CONV, CONV_COLS), lambda j, i: (0, j)),
            pl.BlockSpec((1, CONV_COLS), lambda j, i: (0, j)),
        ],
        out_shape=[jax.ShapeDtypeStruct((L, SSD_PROJ_PAD), BF16), jax.ShapeDtypeStruct((SSD_CONV, SSD_CONV_DIM), F32),
                   jax.ShapeDtypeStruct((1, SSD_CONV_DIM), F32)],
        scratch_shapes=[pltpu.VMEM((CONV_HALO + tm, CONV_COLS), F32), pltpu.VMEM((tm + CONV_HALO, CONV_COLS), F32)],
        input_output_aliases={5: 0},
        compiler_params=_cparams(("arbitrary", "arbitrary")), name="conv_bwd",
    )(proj, proj, w, sp, dxbc, dproj)


def _loss_bwd(x, tgt, w):
    L, D = x.shape
    tm = min(TOKEN_TILE, L)

    def body(x_ref, t_ref, w_ref, l_ref, dx_ref, dw_ref):
        xv = x_ref[...]
        wv = w_ref[...]
        r = lax.rsqrt(jnp.mean(xv * xv, axis=-1, keepdims=True) + RMS_EPS)
        xh = xv * r
        e = xh * wv - t_ref[...]
        lsum = 0.5 * jnp.sum(jnp.mean(e * e, axis=-1, keepdims=True), axis=0, keepdims=True)
        dout = e * (1.0 / D)
        gx = dout * wv
        dx_ref[...] = r * (gx - xh * jnp.mean(gx * xh, axis=-1, keepdims=True))
        first = pl.program_id(0) == 0
        _acc_out(dw_ref, jnp.sum(dout * xh, axis=0, keepdims=True), first)
        _acc_out(l_ref, jnp.broadcast_to(lsum, (8, 128)), first)

    row = pl.BlockSpec((tm, D), lambda i: (i, 0))
    vec = pl.BlockSpec((1, D), lambda i: (0, 0))
    return pl.pallas_call(
        body, grid=(L // tm,),
        in_specs=[row, row, vec],
        out_specs=[pl.BlockSpec((8, 128), lambda i: (0, 0)), row, vec],
        out_shape=[jax.ShapeDtypeStruct((8, 128), F32), jax.ShapeDtypeStruct((L, D), F32), jax.ShapeDtypeStruct((1, D), F32)],
        compiler_params=_cparams(("arbitrary",)), name="loss_bwd",
    )(x, tgt, w)


MESH = pl.DeviceIdType.MESH
ANY = pl.BlockSpec(memory_space=pl.ANY)


def _comm_sems(n):
    return [pltpu.SemaphoreType.DMA((n, 7)), pltpu.SemaphoreType.DMA((n, 7)), pltpu.SemaphoreType.DMA((n,))]


def _gather_ops(x_refs, out_refs, send_sems, recv_sems, local_sems):
    n = len(x_refs)
    x, y, c = lax.axis_index("x"), lax.axis_index("y"), lax.axis_index("c")
    me, sibling = (x, y, c), (x, y, 1 - c)
    chips = [(1 - x, y), (x, 1 - y), (1 - x, 1 - y)]

    def slot(i, px, py, pc):
        return out_refs[i].at[4 * px + 2 * py + pc]

    def copy(i, k, block, to, src=None):
        return pltpu.make_async_remote_copy(
            src_ref=slot(i, *block) if src is None else src, dst_ref=slot(i, *block),
            send_sem=send_sems.at[i, k], recv_sem=recv_sems.at[i, k], device_id=to, device_id_type=MESH)

    def own():
        mine = [pltpu.make_async_copy(x_refs[i], slot(i, *me), local_sems.at[i]) for i in range(n)]
        first = [copy(i, 0, me, sibling, src=x_refs[i]) for i in range(n)]
        first += [copy(i, 1 + j, me, (*chip, c), src=x_refs[i]) for j, chip in enumerate(chips) for i in range(n)]
        return mine, first

    def start():
        mine, first = own()
        for cp in mine + first:
            cp.start()

    def finish():
        mine, first = own()
        passed = []
        for j, chip in enumerate(chips):
            for i in range(n):
                copy(i, 1 + j, (*chip, c), me).wait_recv()
                passed.append(copy(i, 4 + j, (*chip, c), sibling))
                passed[-1].start()
        for i in range(n):
            copy(i, 0, sibling, me).wait_recv()
        for j, chip in enumerate(chips):
            for i in range(n):
                copy(i, 4 + j, (*chip, 1 - c), me).wait_recv()
        for cp in first + passed:
            cp.wait_send()
        for cp in mine:
            cp.wait()

    return start, finish


def _exchange_ops(p_refs, out_refs, send_sems, recv_sems, local_sems):
    n = len(p_refs)
    x, y, c = lax.axis_index("x"), lax.axis_index("y"), lax.axis_index("c")
    my = 4 * x + 2 * y + c

    def peer(k):
        fx, fy, fc = (k >> 2) & 1, (k >> 1) & 1, k & 1
        px, py, pc = (1 - x if fx else x), (1 - y if fy else y), (1 - c if fc else c)
        return (px, py, pc), 4 * px + 2 * py + pc

    def mine():
        return [pltpu.make_async_copy(p_refs[i].at[my], out_refs[i].at[my], local_sems.at[i]) for i in range(n)]

    def start():
        for cp in mine():
            cp.start()
        for k in range(1, N_DEV):
            to, pid = peer(k)
            for i in range(n):
                pltpu.make_async_remote_copy(
                    src_ref=p_refs[i].at[pid], dst_ref=out_refs[i].at[my], send_sem=send_sems.at[i, k - 1],
                    recv_sem=recv_sems.at[i, k - 1], device_id=to, device_id_type=MESH).start()

    def finish():
        for k in range(1, N_DEV):
            to, pid = peer(k)
            for i in range(n):
                pltpu.make_async_remote_copy(
                    src_ref=p_refs[i].at[pid], dst_ref=out_refs[i].at[pid], send_sem=send_sems.at[i, k - 1],
                    recv_sem=recv_sems.at[i, k - 1], device_id=to, device_id_type=MESH).wait()
        for cp in mine():
            cp.wait()

    return start, finish


def _all_gather(xs, *, name):
    n = len(xs)

    def body(*refs):
        start, finish = _gather_ops(refs[:n], refs[n:2 * n], *refs[2 * n:])
        start()
        finish()

    return pl.pallas_call(
        body,
        out_shape=[jax.ShapeDtypeStruct((N_DEV,) + a.shape, a.dtype) for a in xs],
        in_specs=[ANY] * n, out_specs=[ANY] * n, scratch_shapes=_comm_sems(n), name=name,
    )(*xs)


def _exchange(parts, *, name):
    n = len(parts)

    def body(*refs):
        start, finish = _exchange_ops(refs[:n], refs[n:2 * n], *refs[2 * n:])
        start()
        finish()

    return pl.pallas_call(
        body,
        out_shape=[jax.ShapeDtypeStruct(a.shape, a.dtype) for a in parts],
        in_specs=[ANY] * n, out_specs=[ANY] * n, scratch_shapes=_comm_sems(n), name=name,
    )(*parts)


def _adamw(parts, w, m, v, *, name):
    a, b = w.shape
    tr = _pick(a, (256, 128, 64, 32, 16, 8))

    def body(p_ref, w_ref, m_ref, v_ref, g_ref, d_ref, mo_ref, vo_ref):
        g = p_ref[0]
        for s in range(1, N_DEV):
            g = g + p_ref[s]
        mn = ADAM_B1 * m_ref[...] + (1.0 - ADAM_B1) * g
        vn = ADAM_B2 * v_ref[...] + (1.0 - ADAM_B2) * jnp.square(g)
        m_hat = mn / (1.0 - ADAM_B1 ** ADAM_STEP)
        v_hat = vn / (1.0 - ADAM_B2 ** ADAM_STEP)
        g_ref[...] = g
        d_ref[...] = -ADAM_LR * (m_hat / (jnp.sqrt(v_hat) + ADAM_EPS) + ADAM_WD * w_ref[...])
        mo_ref[...] = mn
        vo_ref[...] = vn

    blk = pl.BlockSpec((tr, b), lambda i: (i, 0))
    shp = jax.ShapeDtypeStruct((a, b), F32)
    return pl.pallas_call(
        body, grid=(a // tr,),
        in_specs=[pl.BlockSpec((N_DEV, tr, b), lambda i: (0, i, 0)), blk, blk, blk],
        out_specs=[blk, blk, blk, blk],
        out_shape=[shp, shp, shp, shp],
        compiler_params=_cparams(("parallel",)), name=name,
    )(parts, w, m, v)


def _col_shards(a, n):
    return a.reshape(a.shape[0], N_DEV, n).transpose(1, 0, 2)


def _from_col_shards(g, cols):
    r = g.shape[1]
    full = g.transpose(1, 0, 2).reshape(r, -1)
    return jnp.pad(full, ((0, 0), (0, cols - full.shape[1])))


def kernel(x, norm_w, gla_in_proj, gla_gate_up, gla_gate_bias, gla_head_norm, gla_out_proj, ssd_in_proj, ssd_conv_w, ssd_conv_b, ssd_dt_bias, ssd_a_log, ssd_d, ssd_gate_norm, ssd_out_proj, final_norm, loss_target, m_norm_w, m_gla_in_proj, m_gla_gate_up, m_gla_gate_bias, m_gla_head_norm, m_gla_out_proj, m_ssd_in_proj, m_ssd_conv_w, m_ssd_conv_b, m_ssd_dt_bias, m_ssd_a_log, m_ssd_d, m_ssd_gate_norm, m_ssd_out_proj, m_final_norm, v_norm_w, v_gla_in_proj, v_gla_gate_up, v_gla_gate_bias, v_gla_head_norm, v_gla_out_proj, v_ssd_in_proj, v_ssd_conv_w, v_ssd_conv_b, v_ssd_dt_bias, v_ssd_a_log, v_ssd_d, v_ssd_gate_norm, v_ssd_out_proj, v_final_norm):
    x0 = x[0]
    tgt = loss_target[0]
    n_gin = GLA_PROJ // N_DEV
    n_sin = SSD_PROJ // N_DEV
    n_up = GLA_DK // N_DEV
    n_cv = SSD_CONV_DIM // N_DEV

    g_gin, g_up, g_gout = _all_gather(
        [gla_in_proj[0].astype(BF16), gla_gate_up[0].astype(BF16), gla_out_proj[0].astype(BF16)], name="gather_weights")
    w_gin = _from_col_shards(g_gin, GLA_PROJ_PAD)
    wup = jnp.pad(_from_col_shards(g_up, GLA_DK), ((0, 128 - GLA_RANK), (0, 0))).astype(F32)
    w_gout = g_gout.reshape(D_INNER, D_MODEL)
    vec128 = lambda a: jnp.pad(a.reshape(1, -1), ((0, 0), (0, 128 - a.size)))
    dtb, alog, dsk = vec128(ssd_dt_bias), vec128(ssd_a_log), vec128(ssd_d)
    nw0, nw1 = norm_w[0:1], norm_w[1:2]

    hn1 = _rms_fwd(x0, nw0, name="rms1_fwd")
    proj1 = _mm(hn1, w_gin, name="gla_in_proj")
    (o, og, s_saved), (g_sin, g_sout, g_cw, g_cb, g_gn) = _gla_layer_fwd(
        proj1, wup, gla_gate_bias, gla_head_norm,
        [ssd_in_proj[0].astype(BF16), ssd_out_proj[0].astype(BF16), ssd_conv_w[0], ssd_conv_b, ssd_gate_norm])
    w_sin = _from_col_shards(g_sin, SSD_PROJ_PAD)
    w_sout = g_sout.reshape(D_INNER, D_MODEL)
    conv_w = _from_col_shards(g_cw, SSD_CONV_DIM)
    conv_b = g_cb.reshape(1, SSD_CONV_DIM)
    gate_norm = g_gn.reshape(1, D_INNER)
    x1 = _mm(og, w_gout, add=x0, name="gla_out_proj")
    hn2 = _rms_fwd(x1, nw1, name="rms2_fwd")
    proj2 = _mm(hn2, w_sin, name="ssd_in_proj")
    xbc, conv_sp = _conv_fwd(proj2, conv_w, conv_b)
    yn, h_saved = _ssd_layer_fwd(xbc, proj2, dtb, alog, dsk, gate_norm)
    x2 = _mm(yn, w_sout, add=x1, name="ssd_out_proj")
    lsum, dx2, d_final = _loss_bwd(x2, tgt, final_norm.reshape(1, D_MODEL))
    loss = lax.psum(lsum[0, 0], ("x", "y", "c"))

    d_sout = _mm_tn(yn, dx2, name="ssd_out_proj_dw")
    dyn = _mm(dx2, w_sout.T, name="ssd_out_proj_dx")
    dproj2, dxbc, d_dtb, d_alog, d_dsk, d_gate_norm = _ssd_layer_bwd(xbc, proj2, dtb, alog, dsk, gate_norm, h_saved, dyn)
    dproj2, d_conv_w, d_conv_b = _conv_bwd(proj2, conv_w, conv_sp, dxbc, dproj2)
    d_sin = _mm_tn(hn2, dproj2, name="ssd_in_proj_dw")
    dhn2 = _mm(dproj2, w_sin.T, name="ssd_in_proj_dx")
    dx1, d_nw1 = _rms_bwd(x1, nw1, dhn2, dx2, name="rms2_bwd")
    d_gout = _mm_tn(og, dx1, name="gla_out_proj_dw")
    dog = _mm(dx1, w_gout.T, name="gla_out_proj_dx")
    early = {
        "gla_out_proj": ((gla_out_proj[0], m_gla_out_proj[0], v_gla_out_proj[0]), d_gout.reshape(N_DEV, -1, D_MODEL)),
        "ssd_in_proj": ((ssd_in_proj[0], m_ssd_in_proj[0], v_ssd_in_proj[0]), _col_shards(d_sin[:, :SSD_PROJ], n_sin)),
        "ssd_conv_w": ((ssd_conv_w[0], m_ssd_conv_w[0], v_ssd_conv_w[0]), _col_shards(d_conv_w, n_cv)),
        "ssd_conv_b": ((ssd_conv_b, m_ssd_conv_b, v_ssd_conv_b), d_conv_b.reshape(N_DEV, 1, n_cv)),
        "ssd_gate_norm": ((ssd_gate_norm, m_ssd_gate_norm, v_ssd_gate_norm), d_gate_norm.reshape(N_DEV, 1, -1)),
        "ssd_out_proj": ((ssd_out_proj[0], m_ssd_out_proj[0], v_ssd_out_proj[0]), d_sout.reshape(N_DEV, -1, D_MODEL)),
    }
    (dproj1, d_wup, d_gbias, d_head_norm), early_recv = _gla_layer_bwd(
        proj1, wup, gla_gate_bias, gla_head_norm, o, s_saved, dog, [p for _, p in early.values()])
    d_gin = _mm_tn(hn1, dproj1, name="gla_in_proj_dw")
    dhn1 = _mm(dproj1, w_gin.T, name="gla_in_proj_dx")
    dx0, d_nw0 = _rms_bwd(x0, nw0, dhn1, dx1, name="rms1_bwd")
    late = {
        "gla_in_proj": ((gla_in_proj[0], m_gla_in_proj[0], v_gla_in_proj[0]), _col_shards(d_gin[:, :GLA_PROJ], n_gin)),
        "gla_gate_up": ((gla_gate_up[0], m_gla_gate_up[0], v_gla_gate_up[0]), _col_shards(d_wup[:GLA_RANK], n_up)),
    }
    heads = SSD_HEADS
    replicated = {
        "norm_w": ((norm_w, m_norm_w, v_norm_w), jnp.concatenate([d_nw0, d_nw1], axis=0)),
        "gla_gate_bias": ((gla_gate_bias, m_gla_gate_bias, v_gla_gate_bias), d_gbias),
        "gla_head_norm": ((gla_head_norm, m_gla_head_norm, v_gla_head_norm), d_head_norm),
        "ssd_dt_bias": ((ssd_dt_bias, m_ssd_dt_bias, v_ssd_dt_bias), d_dtb[:, :heads]),
        "ssd_a_log": ((ssd_a_log, m_ssd_a_log, v_ssd_a_log), d_alog[:, :heads]),
        "ssd_d": ((ssd_d, m_ssd_d, v_ssd_d), d_dsk[:, :heads]),
        "final_norm": (tuple(t.reshape(1, D_MODEL) for t in (final_norm, m_final_norm, v_final_norm)), d_final),
    }
    results = {}
    for group, recv in ((early, early_recv),
                        (late, _exchange([p for _, p in late.values()], name="sharded_exchange")),
                        (replicated, _all_gather([p for _, p in replicated.values()], name="replicated_gather"))):
        for (nm, ((w, m, v), _)), r in zip(group.items(), recv):
            results[nm] = _adamw(r, w, m, v, name=nm + "_adamw")

    order = [("norm_w", norm_w), ("gla_in_proj", gla_in_proj), ("gla_gate_up", gla_gate_up), ("gla_gate_bias", gla_gate_bias),
             ("gla_head_norm", gla_head_norm), ("gla_out_proj", gla_out_proj), ("ssd_in_proj", ssd_in_proj),
             ("ssd_conv_w", ssd_conv_w), ("ssd_conv_b", ssd_conv_b), ("ssd_dt_bias", ssd_dt_bias), ("ssd_a_log", ssd_a_log),
             ("ssd_d", ssd_d), ("ssd_gate_norm", ssd_gate_norm), ("ssd_out_proj", ssd_out_proj), ("final_norm", final_norm)]
    out = [loss, dx0[None]]
    for i in range(4):
        out += [results[nm][i].reshape(ref.shape) for nm, ref in order]
    return tuple(out)
```

```python
import jax
import jax.numpy as jnp
from jax import lax
from jax.experimental import pallas as pl
from jax.experimental.pallas import tpu as pltpu

F32 = jnp.float32
BF16 = jnp.bfloat16

D_MODEL = 1024
D_INNER = 2048
RMS_EPS = 1e-6
GLA_HEADS = 4
GLA_DK = 512
GLA_HEAD_K = 128
GLA_HEAD_V = 512
GLA_RANK = 16
GLA_NORMALIZER = 16.0
CHUNK = 64
SUB = 16
GLA_PROJ = 5136
GLA_PROJ_PAD = 5376
GLA_GK_COL = 5120
SSD_HEADS = 32
SSD_GROUPS = 8
SSD_HPG = 4
SSD_P = 64
SSD_N = 128
SSD_CONV = 4
SSD_CONV_DIM = 4096
SSD_PROJ = 6176
SSD_PROJ_PAD = 6400
SSD_DT_COL = 6144
N_DEV = 8

ADAM_LR = 0.001
ADAM_B1 = 0.9
ADAM_B2 = 0.999
ADAM_EPS = 1e-08
ADAM_WD = 0.01
ADAM_STEP = 10

VMEM_LIMIT = 56 * 1024 * 1024
TOKEN_TILE = 512
MM_TOKEN_TILE = 2048
MM_VMEM_BUDGET = 44 * 1024 * 1024
MM_TILES = (1792, 1280, 1024, 768, 512, 256, 128)


def _dot(a, b):
    return jnp.dot(a, b, preferred_element_type=F32)


def _dot_nt(a, b):
    return lax.dot_general(a, b, (((1,), (1,)), ((), ())), preferred_element_type=F32)


def _dot_tn(a, b):
    return lax.dot_general(a, b, (((0,), (0,)), ((), ())), preferred_element_type=F32)


def _bf(a):
    return a.astype(BF16)


@jax.custom_vjp
def _mxu(a, b):
    return _dot(_bf(a), _bf(b))


def _mxu_fwd(a, b):
    return _mxu(a, b), (a, b)


def _mxu_bwd(res, g):
    a, b = res
    return _dot_nt(_bf(g), _bf(b)), _dot_tn(_bf(a), _bf(g))


_mxu.defvjp(_mxu_fwd, _mxu_bwd)


@jax.custom_vjp
def _mxu_nt(a, b):
    return _dot_nt(_bf(a), _bf(b))


def _mxu_nt_fwd(a, b):
    return _mxu_nt(a, b), (a, b)


def _mxu_nt_bwd(res, g):
    a, b = res
    return _dot(_bf(g), _bf(b)), _dot_tn(_bf(g), _bf(a))


_mxu_nt.defvjp(_mxu_nt_fwd, _mxu_nt_bwd)


@jax.custom_vjp
def _mxu_tn(a, b):
    return _dot_tn(_bf(a), _bf(b))


def _mxu_tn_fwd(a, b):
    return _mxu_tn(a, b), (a, b)


def _mxu_tn_bwd(res, g):
    a, b = res
    return _dot_nt(_bf(b), _bf(g)), _dot(_bf(a), _bf(g))


_mxu_tn.defvjp(_mxu_tn_fwd, _mxu_tn_bwd)


def _split2(a):
    hi = _bf(a)
    return hi, _bf(a - hi.astype(F32))


def _three_pass(dot, a, b):
    ah, al = _split2(a)
    bh, bl = _split2(b)
    return dot(ah, bh) + (dot(ah, bl) + dot(al, bh))


@jax.custom_vjp
def _dot3_nt(a, b):
    return _three_pass(_dot_nt, a, b)


def _dot3_nt_fwd(a, b):
    return _dot3_nt(a, b), (a, b)


def _dot3_nt_bwd(res, g):
    a, b = res
    return _three_pass(_dot, g, b), _three_pass(_dot_tn, g, a)


_dot3_nt.defvjp(_dot3_nt_fwd, _dot3_nt_bwd)


def _silu(x):
    return x / (1.0 + jnp.exp(-x))


def _log_sigmoid(z):
    return jnp.minimum(z, 0.0) - jnp.log(1.0 + jnp.exp(-jnp.abs(z)))


def _softplus(z):
    return jnp.maximum(z, 0.0) + jnp.log(1.0 + jnp.exp(-jnp.abs(z)))


def _iota(shape, dim):
    return lax.broadcasted_iota(jnp.int32, shape, dim)


def _rms(x, w):
    return x * lax.rsqrt(jnp.mean(x * x, axis=-1, keepdims=True) + RMS_EPS) * w


def _scan_rows(a, reverse, seg):
    n = a.shape[0]
    pos = _iota(a.shape, 0) & (seg - 1)
    sh = 1
    while sh < seg:
        if reverse:
            a = a + jnp.where(pos < seg - sh, pltpu.roll(a, n - sh, 0), 0.0)
        else:
            a = a + jnp.where(pos >= sh, pltpu.roll(a, sh, 0), 0.0)
        sh *= 2
    return a


def _make_cumsum(seg):
    @jax.custom_vjp
    def cumsum(a):
        return _scan_rows(a, False, seg)

    cumsum.defvjp(lambda a: (_scan_rows(a, False, seg), None), lambda _, g: (_scan_rows(g, True, seg),))
    return cumsum


_cumsum_sub = _make_cumsum(SUB)
_cumsum_rows = _make_cumsum(CHUNK)


def _cparams(sem):
    return pltpu.CompilerParams(dimension_semantics=sem, vmem_limit_bytes=VMEM_LIMIT)


def _acc_out(ref, val, first):
    @pl.when(first)
    def _():
        ref[...] = val

    @pl.when(jnp.logical_not(first))
    def _():
        ref[...] += val


def _gla_chunk(q, k, v, gk, wup, bias, St):
    nb = CHUNK // SUB
    z = _mxu(gk, wup) + bias
    la = _log_sigmoid(z) * (1.0 / GLA_NORMALIZER)
    qs = q * (GLA_HEAD_K ** -0.5)
    bl = _cumsum_sub(la)
    tot = [jnp.sum(la[i * SUB:(i + 1) * SUB], axis=0, keepdims=True) for i in range(nb)]
    pre = [jnp.zeros((1, GLA_HEAD_K), F32)]
    for i in range(nb):
        pre.append(pre[i] + tot[i])
    b_last = pre[nb]
    rows_of = lambda vals: jnp.concatenate([jnp.broadcast_to(t, (SUB, GLA_HEAD_K)) for t in vals], axis=0)
    suf = rows_of(tot) - bl
    nxt = rows_of(pre[1:])
    o = _mxu_nt(qs * jnp.exp(bl + rows_of(pre[:nb])), St)
    St_new = St * jnp.exp(b_last) + _mxu_tn(v, k * jnp.exp(suf + (b_last - nxt)))
    qa = qs * jnp.exp(bl)
    row = _iota((CHUNK, GLA_HEAD_K), 0)
    rs = _iota((SUB, GLA_HEAD_K), 0)
    cs = _iota((SUB, CHUNK), 1)
    a_rows = []
    for i in range(nb):
        sl = slice(i * SUB, (i + 1) * SUB)
        q_i, k_i, bl_i = qs[sl], k[sl], bl[sl]
        if i > 0:
            kp = jnp.where(row < i * SUB, k * jnp.exp(jnp.minimum(suf + (pre[i] - nxt), 0.0)), 0.0)
            a_i = _dot3_nt(qa[sl], kp)
        else:
            a_i = jnp.zeros((SUB, CHUNK), F32)
        for j in range(SUB):
            e = jnp.exp(jnp.minimum(bl_i - bl_i[j:j + 1], 0.0))
            t = jnp.where(rs >= j, q_i * e * k_i[j:j + 1], 0.0)
            a_i = a_i + jnp.where(cs == i * SUB + j, jnp.sum(t, axis=-1, keepdims=True), 0.0)
        a_rows.append(a_i)
    o = o + _mxu(jnp.concatenate(a_rows, axis=0), v)
    return o, St_new


def _gla_post(o, g, wn):
    return _rms(o, wn) * _silu(g)


GLA_HALF = 2 * GLA_HEAD_V


def _gla_specs(nc, rev):
    ci = (lambda c: nc - 1 - c) if rev else (lambda c: c)
    v0 = 2 * GLA_DK // GLA_HALF
    g0 = (2 * GLA_DK + D_INNER) // GLA_HALF
    return [
        pl.BlockSpec((CHUNK, GLA_DK), lambda c: (ci(c), 0)),
        pl.BlockSpec((CHUNK, GLA_DK), lambda c: (ci(c), 1)),
        pl.BlockSpec((CHUNK, GLA_HALF), lambda c: (ci(c), v0)),
        pl.BlockSpec((CHUNK, GLA_HALF), lambda c: (ci(c), v0 + 1)),
        pl.BlockSpec((CHUNK, GLA_HALF), lambda c: (ci(c), g0)),
        pl.BlockSpec((CHUNK, GLA_HALF), lambda c: (ci(c), g0 + 1)),
        pl.BlockSpec((CHUNK, 128), lambda c: (ci(c), GLA_GK_COL // 128)),
        pl.BlockSpec((128, GLA_DK), lambda c: (0, 0)),
        pl.BlockSpec((1, GLA_DK), lambda c: (0, 0)),
        pl.BlockSpec((1, GLA_HEAD_V), lambda c: (0, 0)),
    ]


def _head_cols(ref_a, ref_b, h):
    ref = ref_a if h < 2 else ref_b
    return ref[:, (h % 2) * GLA_HEAD_V:(h % 2 + 1) * GLA_HEAD_V]


def _gla_layer_fwd(proj, wup, bias, wn, gather):
    L = proj.shape[0]
    nc = L // CHUNK
    n = len(gather)

    def body(*refs):
        q_ref, k_ref, va_ref, vb_ref, ga_ref, gb_ref, gk_ref, wup_ref, b_ref, wn_ref = refs[:10]
        x_refs = refs[10:10 + n]
        o_ref, og_ref, s_ref = refs[10 + n:13 + n]
        out_refs = refs[13 + n:13 + 2 * n]
        st, send_sems, recv_sems, local_sems = refs[13 + 2 * n:]
        start, finish = _gather_ops(x_refs, out_refs, send_sems, recv_sems, local_sems)

        @pl.when(pl.program_id(0) == 0)
        def _():
            st[...] = jnp.zeros(st.shape, F32)
            start()

        gk = gk_ref[...]
        for h in range(GLA_HEADS):
            kc = slice(h * GLA_HEAD_K, (h + 1) * GLA_HEAD_K)
            vc = slice(h * GLA_HEAD_V, (h + 1) * GLA_HEAD_V)
            s_in = st[h]
            s_ref[h] = s_in
            o, s_new = _gla_chunk(q_ref[:, kc], k_ref[:, kc], _head_cols(va_ref, vb_ref, h), gk, wup_ref[:, kc], b_ref[:, kc], s_in)
            st[h] = s_new
            o_ref[:, vc] = o
            og_ref[:, vc] = _gla_post(o, _head_cols(ga_ref, gb_ref, h), wn_ref[...]).astype(BF16)

        @pl.when(pl.program_id(0) == nc - 1)
        def _():
            finish()

    res = pl.pallas_call(
        body,
        grid=(nc,),
        in_specs=_gla_specs(nc, False) + [ANY] * n,
        out_specs=[
            pl.BlockSpec((CHUNK, D_INNER), lambda c: (c, 0)),
            pl.BlockSpec((CHUNK, D_INNER), lambda c: (c, 0)),
            pl.BlockSpec((None, GLA_HEADS, GLA_HEAD_V, GLA_HEAD_K), lambda c: (c, 0, 0, 0)),
        ] + [ANY] * n,
        out_shape=[
            jax.ShapeDtypeStruct((L, D_INNER), F32),
            jax.ShapeDtypeStruct((L, D_INNER), BF16),
            jax.ShapeDtypeStruct((nc, GLA_HEADS, GLA_HEAD_V, GLA_HEAD_K), F32),
        ] + [jax.ShapeDtypeStruct((N_DEV,) + a.shape, a.dtype) for a in gather],
        scratch_shapes=[pltpu.VMEM((GLA_HEADS, GLA_HEAD_V, GLA_HEAD_K), F32)] + _comm_sems(n),
        compiler_params=_cparams(("arbitrary",)),
        name="gla_layer_fwd",
    )(proj, proj, proj, proj, proj, proj, proj, wup, bias, wn, *gather)
    return res[:3], res[3:]


def _gla_layer_bwd(proj, wup, bias, wn, o, s_in, dog, exchange):
    L = proj.shape[0]
    nc = L // CHUNK
    n = len(exchange)

    def body(*refs):
        (q_ref, k_ref, va_ref, vb_ref, ga_ref, gb_ref, gk_ref, wup_ref, b_ref, wn_ref, o_ref, s_ref, dog_ref) = refs[:13]
        p_refs = refs[13:13 + n]
        dp_ref, dwup_ref, db_ref, dwn_ref = refs[13 + n:17 + n]
        out_refs = refs[17 + n:17 + 2 * n]
        dst, send_sems, recv_sems, local_sems = refs[17 + 2 * n:]
        start, finish = _exchange_ops(p_refs, out_refs, send_sems, recv_sems, local_sems)

        @pl.when(pl.program_id(0) == 0)
        def _():
            dst[...] = jnp.zeros(dst.shape, F32)
            dwup_ref[...] = jnp.zeros(dwup_ref.shape, F32)
            db_ref[...] = jnp.zeros(db_ref.shape, F32)
            dwn_ref[...] = jnp.zeros(dwn_ref.shape, F32)
            start()

        gk = gk_ref[...]
        dgk_sum = jnp.zeros((CHUNK, 128), F32)
        for h in range(GLA_HEADS):
            kc = slice(h * GLA_HEAD_K, (h + 1) * GLA_HEAD_K)
            vc = slice(h * GLA_HEAD_V, (h + 1) * GLA_HEAD_V)
            _, post_vjp = jax.vjp(_gla_post, o_ref[:, vc], _head_cols(ga_ref, gb_ref, h), wn_ref[...])
            do, dg, dwn = post_vjp(dog_ref[:, vc])
            _, vjp = jax.vjp(_gla_chunk, q_ref[:, kc], k_ref[:, kc], _head_cols(va_ref, vb_ref, h), gk, wup_ref[:, kc],
                             b_ref[:, kc], s_ref[h])
            dq, dk, dv, dgk, dwup, db, ds = vjp((do, dst[h]))
            dst[h] = ds
            dp_ref[:, kc] = dq.astype(BF16)
            dp_ref[:, GLA_DK + h * GLA_HEAD_K:GLA_DK + (h + 1) * GLA_HEAD_K] = dk.astype(BF16)
            dp_ref[:, 2 * GLA_DK + h * GLA_HEAD_V:2 * GLA_DK + (h + 1) * GLA_HEAD_V] = dv.astype(BF16)
            dp_ref[:, 2 * GLA_DK + D_INNER + h * GLA_HEAD_V:2 * GLA_DK + D_INNER + (h + 1) * GLA_HEAD_V] = dg.astype(BF16)
            dwup_ref[:, kc] += dwup
            db_ref[:, kc] += db
            dwn_ref[...] += dwn
            dgk_sum = dgk_sum + dgk
        dp_ref[:, GLA_GK_COL:GLA_GK_COL + 128] = dgk_sum.astype(BF16)
        dp_ref[:, GLA_GK_COL + 128:] = jnp.zeros((CHUNK, GLA_PROJ_PAD - GLA_GK_COL - 128), BF16)

        @pl.when(pl.program_id(0) == nc - 1)
        def _():
            finish()

    rc = lambda c: nc - 1 - c
    res = pl.pallas_call(
        body,
        grid=(nc,),
        in_specs=_gla_specs(nc, True) + [
            pl.BlockSpec((CHUNK, D_INNER), lambda c: (rc(c), 0)),
            pl.BlockSpec((None, GLA_HEADS, GLA_HEAD_V, GLA_HEAD_K), lambda c: (rc(c), 0, 0, 0)),
            pl.BlockSpec((CHUNK, D_INNER), lambda c: (rc(c), 0)),
        ] + [ANY] * n,
        out_specs=[
            pl.BlockSpec((CHUNK, GLA_PROJ_PAD), lambda c: (rc(c), 0)),
            pl.BlockSpec((128, GLA_DK), lambda c: (0, 0)),
            pl.BlockSpec((1, GLA_DK), lambda c: (0, 0)),
            pl.BlockSpec((1, GLA_HEAD_V), lambda c: (0, 0)),
        ] + [ANY] * n,
        out_shape=[
            jax.ShapeDtypeStruct((L, GLA_PROJ_PAD), BF16),
            jax.ShapeDtypeStruct((128, GLA_DK), F32),
            jax.ShapeDtypeStruct((1, GLA_DK), F32),
            jax.ShapeDtypeStruct((1, GLA_HEAD_V), F32),
        ] + [jax.ShapeDtypeStruct(a.shape, a.dtype) for a in exchange],
        scratch_shapes=[pltpu.VMEM((GLA_HEADS, GLA_HEAD_V, GLA_HEAD_K), F32)] + _comm_sems(n),
        compiler_params=_cparams(("arbitrary",)),
        name="gla_layer_bwd",
    )(proj, proj, proj, proj, proj, proj, proj, wup, bias, wn, o, s_in, dog, *exchange)
    return res[:4], res[4:]


@jax.custom_vjp
def _expand(v):
    r = v.shape[0]
    left = _iota((r, 128), 1) < SSD_P
    slabs = []
    for p in range(SSD_HEADS // 2):
        a = jnp.broadcast_to(v[:, 2 * p:2 * p + 1], (r, 128))
        b = jnp.broadcast_to(v[:, 2 * p + 1:2 * p + 2], (r, 128))
        slabs.append(jnp.where(left, a, b))
    return jnp.concatenate(slabs, axis=1)


def _expand_fwd(v):
    return _expand(v), None


def _expand_bwd(_, g):
    r = g.shape[0]
    lane = _iota((r, 128), 1)
    left = lane < SSD_P
    dv = jnp.zeros((r, 128), F32)
    for p in range(SSD_HEADS // 2):
        gs = g[:, 128 * p:128 * (p + 1)]
        sa = jnp.sum(jnp.where(left, gs, 0.0), axis=-1, keepdims=True)
        sb = jnp.sum(jnp.where(left, 0.0, gs), axis=-1, keepdims=True)
        dv = dv + jnp.where(lane == 2 * p, sa, 0.0) + jnp.where(lane == 2 * p + 1, sb, 0.0)
    return (dv,)


_expand.defvjp(_expand_fwd, _expand_bwd)

SSD_GW = SSD_HPG * SSD_P
SSD_BC = SSD_GROUPS * SSD_N


def _ssd_chunk(xs, Bm, Cm, dtp, dtb, alog, dsk, h_in):
    dt = _softplus(dtp + dtb)
    acum = _cumsum_rows(dt * (-jnp.exp(alog)))
    a_last = acum[CHUNK - 1:CHUNK]
    acum_b = _expand(acum)
    w_end = _expand(dt * jnp.exp(a_last - acum))
    d_b = _expand(jnp.broadcast_to(dsk, (8, 128)))[0:1]
    ac_t = jnp.concatenate([acum, acum], axis=0).T
    dt_t = jnp.concatenate([dt, dt], axis=0).T
    lane = _iota((CHUNK, 128), 1)
    left = lane < SSD_P
    causal = (lane & (SSD_P - 1)) <= _iota((CHUNK, 128), 0)
    ys, sts = [], []
    for g in range(SSD_GROUPS):
        Bg = Bm[:, g * SSD_N:(g + 1) * SSD_N]
        Cg = Cm[:, g * SSD_N:(g + 1) * SSD_N]
        cb2 = _mxu_nt(Cg, jnp.concatenate([Bg, Bg], axis=0))
        y_off = _mxu_nt(Cg, h_in[g * SSD_GW:(g + 1) * SSD_GW])
        for k in range(SSD_HPG // 2):
            p = g * (SSD_HPG // 2) + k
            sl = slice(128 * p, 128 * (p + 1))
            ac_c = acum_b[:, sl]
            ac_r = jnp.where(left, ac_t[2 * p:2 * p + 1], ac_t[2 * p + 1:2 * p + 2])
            dt_r = jnp.where(left, dt_t[2 * p:2 * p + 1], dt_t[2 * p + 1:2 * p + 2])
            m2 = cb2 * jnp.where(causal, jnp.exp(jnp.minimum(ac_c - ac_r, 0.0)), 0.0) * dt_r
            xsl = xs[:, sl]
            x2 = jnp.concatenate([jnp.where(left, xsl, 0.0), jnp.where(left, 0.0, xsl)], axis=0)
            ys.append(_mxu(m2, x2) + y_off[:, 128 * k:128 * (k + 1)] * jnp.exp(ac_c) + xsl * d_b[:, sl])
        gs = slice(g * SSD_GW, (g + 1) * SSD_GW)
        sts.append(_mxu_tn(xs[:, gs] * w_end[:, gs], Bg))
    cd = jnp.exp(ac_t[:, CHUNK - 1:CHUNK])
    hs = [h_in[h * SSD_P:(h + 1) * SSD_P] * cd[h:h + 1] for h in range(SSD_HEADS)]
    h_out = jnp.concatenate(hs, axis=0) + jnp.concatenate(sts, axis=0)
    return jnp.concatenate(ys, axis=1), h_out


def _ssd_step(xs, Bm, Cm, dtp, dtb, alog, dsk, h_in, z, wn):
    y, h_out = _ssd_chunk(xs, Bm, Cm, dtp, dtb, alog, dsk, h_in)
    return _rms(y * _silu(z), wn), h_out


def _ssd_specs(nc, rev):
    ci = (lambda c: nc - 1 - c) if rev else (lambda c: c)
    vec = pl.BlockSpec((1, 128), lambda c: (0, 0))
    return [
        pl.BlockSpec((CHUNK, D_INNER), lambda c: (ci(c), 0)),
        pl.BlockSpec((CHUNK, SSD_BC), lambda c: (ci(c), D_INNER // SSD_BC)),
        pl.BlockSpec((CHUNK, SSD_BC), lambda c: (ci(c), D_INNER // SSD_BC + 1)),
        pl.BlockSpec((CHUNK, 128), lambda c: (ci(c), SSD_DT_COL // 128)),
        vec, vec, vec,
        pl.BlockSpec((CHUNK, D_INNER), lambda c: (ci(c), 0)),
        pl.BlockSpec((1, D_INNER), lambda c: (0, 0)),
    ]


def _ssd_layer_fwd(xbc, proj, dtb, alog, dsk, wn):
    L = xbc.shape[0]
    nc = L // CHUNK

    def body(xs_ref, b_ref, c_ref, dt_ref, dtb_ref, alog_ref, dsk_ref, z_ref, wn_ref, y_ref, hs_ref, hst):
        @pl.when(pl.program_id(0) == 0)
        def _():
            hst[...] = jnp.zeros(hst.shape, F32)

        h_in = hst[...]
        hs_ref[...] = h_in
        yn, h_out = _ssd_step(xs_ref[...], b_ref[...], c_ref[...], dt_ref[...], dtb_ref[...], alog_ref[...], dsk_ref[...],
                              h_in, z_ref[...], wn_ref[...])
        y_ref[...] = yn.astype(BF16)
        hst[...] = h_out

    return pl.pallas_call(
        body,
        grid=(nc,),
        in_specs=_ssd_specs(nc, False),
        out_specs=[
            pl.BlockSpec((CHUNK, D_INNER), lambda c: (c, 0)),
            pl.BlockSpec((None, D_INNER, SSD_N), lambda c: (c, 0, 0)),
        ],
        out_shape=[
            jax.ShapeDtypeStruct((L, D_INNER), BF16),
            jax.ShapeDtypeStruct((nc, D_INNER, SSD_N), F32),
        ],
        scratch_shapes=[pltpu.VMEM((D_INNER, SSD_N), F32)],
        compiler_params=_cparams(("arbitrary",)),
        name="ssd_layer_fwd",
    )(xbc, xbc, xbc, proj, dtb, alog, dsk, proj, wn)


def _ssd_layer_bwd(xbc, proj, dtb, alog, dsk, wn, h_saved, dyn):
    L = xbc.shape[0]
    nc = L // CHUNK

    def body(xs_ref, b_ref, c_ref, dt_ref, dtb_ref, alog_ref, dsk_ref, z_ref, wn_ref, hs_ref, dyn_ref,
             dp_ref, dx_ref, ddtb_ref, dalog_ref, ddsk_ref, dwn_ref, dhst):
        @pl.when(pl.program_id(0) == 0)
        def _():
            dhst[...] = jnp.zeros(dhst.shape, F32)
            ddtb_ref[...] = jnp.zeros((1, 128), F32)
            dalog_ref[...] = jnp.zeros((1, 128), F32)
            ddsk_ref[...] = jnp.zeros((1, 128), F32)
            dwn_ref[...] = jnp.zeros((1, D_INNER), F32)

        _, vjp = jax.vjp(_ssd_step, xs_ref[...], b_ref[...], c_ref[...], dt_ref[...], dtb_ref[...], alog_ref[...],
                         dsk_ref[...], hs_ref[...], z_ref[...], wn_ref[...])
        dxs, db, dc, ddt, ddtb, dalog, ddsk, dh, dz, dwn = vjp((dyn_ref[...], dhst[...]))
        dx_ref[:, :D_INNER] = dxs
        dx_ref[:, D_INNER:D_INNER + SSD_BC] = db
        dx_ref[:, D_INNER + SSD_BC:] = dc
        dp_ref[:, :D_INNER] = dz.astype(BF16)
        dp_ref[:, D_INNER:SSD_DT_COL] = jnp.zeros((CHUNK, SSD_CONV_DIM), BF16)
        dp_ref[:, SSD_DT_COL:SSD_DT_COL + 128] = ddt.astype(BF16)
        dp_ref[:, SSD_DT_COL + 128:] = jnp.zeros((CHUNK, SSD_PROJ_PAD - SSD_DT_COL - 128), BF16)
        dhst[...] = dh
        ddtb_ref[...] += ddtb
        dalog_ref[...] += dalog
        ddsk_ref[...] += ddsk
        dwn_ref[...] += dwn

    rc = lambda c: nc - 1 - c
    vec = pl.BlockSpec((1, 128), lambda c: (0, 0))
    vshape = jax.ShapeDtypeStruct((1, 128), F32)
    return pl.pallas_call(
        body,
        grid=(nc,),
        in_specs=_ssd_specs(nc, True) + [
            pl.BlockSpec((None, D_INNER, SSD_N), lambda c: (rc(c), 0, 0)),
            pl.BlockSpec((CHUNK, D_INNER), lambda c: (rc(c), 0)),
        ],
        out_specs=[
            pl.BlockSpec((CHUNK, SSD_PROJ_PAD), lambda c: (rc(c), 0)),
            pl.BlockSpec((CHUNK, SSD_CONV_DIM), lambda c: (rc(c), 0)),
            vec, vec, vec,
            pl.BlockSpec((1, D_INNER), lambda c: (0, 0)),
        ],
        out_shape=[
            jax.ShapeDtypeStruct((L, SSD_PROJ_PAD), BF16),
            jax.ShapeDtypeStruct((L, SSD_CONV_DIM), F32),
            vshape, vshape, vshape,
            jax.ShapeDtypeStruct((1, D_INNER), F32),
        ],
        scratch_shapes=[pltpu.VMEM((D_INNER, SSD_N), F32)],
        compiler_params=_cparams(("arbitrary",)),
        name="ssd_layer_bwd",
    )(xbc, xbc, xbc, proj, dtb, alog, dsk, proj, wn, h_saved, dyn)


def _pick(n, options):
    for t in options:
        if n % t == 0:
            return t
    return n


def _token_tile(m, row_bytes, fixed_bytes):
    for t in (MM_TOKEN_TILE, MM_TOKEN_TILE // 2, MM_TOKEN_TILE // 4):
        if m % t == 0 and t * row_bytes + fixed_bytes <= MM_VMEM_BUDGET:
            return t
    return min(m, MM_TOKEN_TILE // 4)


def _mm(a, b, *, name, out_dtype=F32, add=None, exchange=()):
    M, K = a.shape
    N = b.shape[1]
    tn = _pick(N, MM_TILES)
    tk = _pick(K, MM_TILES)
    nk = K // tk
    row_bytes = 2 * (tk * a.dtype.itemsize + tn * jnp.dtype(out_dtype).itemsize + (tn * 4 if add is not None else 0)) \
        + (tn * 4 if nk > 1 else 0)
    tm = _token_tile(M, row_bytes, 2 * tk * tn * b.dtype.itemsize)
    grid = (M // tm, N // tn, nk)
    n = len(exchange)
    n_in = 2 + (add is not None)

    def body(*refs):
        a_ref, b_ref = refs[:2]
        add_ref = refs[2] if add is not None else None
        p_refs = refs[n_in:n_in + n]
        o_ref = refs[n_in + n]
        out_refs = refs[n_in + n + 1:n_in + 2 * n + 1]
        acc = refs[n_in + 2 * n + 1]
        ids = [pl.program_id(d) for d in range(3)]
        k = ids[2]
        if n:
            start, finish_exchange = _exchange_ops(p_refs, out_refs, *refs[n_in + 2 * n + 2:])

            @pl.when((ids[0] == 0) & (ids[1] == 0) & (k == 0))
            def _():
                start()

        p = _dot(_bf(a_ref[...]), _bf(b_ref[...]))

        def finish(r):
            if add is not None:
                r = r + add_ref[...]
            o_ref[...] = r.astype(out_dtype)

        if nk == 1:
            finish(p)
        else:
            @pl.when(k == 0)
            def _():
                acc[...] = p

            @pl.when((k > 0) & (k < nk - 1))
            def _():
                acc[...] += p

            @pl.when(k == nk - 1)
            def _():
                finish(acc[...] + p)

        if n:
            @pl.when((ids[0] == grid[0] - 1) & (ids[1] == grid[1] - 1) & (k == nk - 1))
            def _():
                finish_exchange()

    in_specs = [pl.BlockSpec((tm, tk), lambda i, j, k: (i, k)), pl.BlockSpec((tk, tn), lambda i, j, k: (k, j))]
    args = [a, b]
    if add is not None:
        in_specs.append(pl.BlockSpec((tm, tn), lambda i, j, k: (i, j)))
        args.append(add)
    res = pl.pallas_call(
        body,
        grid=grid,
        in_specs=in_specs + [ANY] * n,
        out_specs=[pl.BlockSpec((tm, tn), lambda i, j, k: (i, j))] + [ANY] * n,
        out_shape=[jax.ShapeDtypeStruct((M, N), out_dtype)] + [jax.ShapeDtypeStruct(p.shape, p.dtype) for p in exchange],
        scratch_shapes=[pltpu.VMEM((tm, tn) if nk > 1 else (8, 128), F32)] + (_comm_sems(n) if n else []),
        compiler_params=_cparams(("arbitrary",) * 3 if n else ("parallel", "parallel", "arbitrary")),
        name=name,
    )(*args, *exchange)
    return (res[0], res[1:]) if n else res[0]


def _mm_tn(a, b, *, name):
    M, K = a.shape
    N = b.shape[1]
    tn = _pick(N, MM_TILES)
    tm = _token_tile(M, 2 * (K * a.dtype.itemsize + tn * b.dtype.itemsize), 2 * K * tn * 4)

    def body(a_ref, b_ref, o_ref):
        _acc_out(o_ref, _dot_tn(_bf(a_ref[...]), _bf(b_ref[...])), pl.program_id(1) == 0)

    return pl.pallas_call(
        body,
        grid=(N // tn, M // tm),
        in_specs=[pl.BlockSpec((tm, K), lambda j, i: (i, 0)), pl.BlockSpec((tm, tn), lambda j, i: (i, j))],
        out_specs=pl.BlockSpec((K, tn), lambda j, i: (0, j)),
        out_shape=jax.ShapeDtypeStruct((K, N), F32),
        compiler_params=_cparams(("parallel", "arbitrary")),
        name=name,
    )(a, b)


def _rms_fwd(x, w, *, name):
    L, D = x.shape
    tm = min(TOKEN_TILE, L)

    def body(x_ref, w_ref, o_ref):
        o_ref[...] = _rms(x_ref[...], w_ref[...]).astype(BF16)

    return pl.pallas_call(
        body, grid=(L // tm,),
        in_specs=[pl.BlockSpec((tm, D), lambda i: (i, 0)), pl.BlockSpec((1, D), lambda i: (0, 0))],
        out_specs=pl.BlockSpec((tm, D), lambda i: (i, 0)),
        out_shape=jax.ShapeDtypeStruct((L, D), BF16),
        compiler_params=_cparams(("parallel",)), name=name,
    )(x, w)


def _rms_bwd(x, w, dhn, dres, *, name):
    L, D = x.shape
    tm = min(TOKEN_TILE, L)

    def body(x_ref, w_ref, dhn_ref, dres_ref, dx_ref, dw_ref):
        _, vjp = jax.vjp(_rms, x_ref[...], w_ref[...])
        dx, dw = vjp(dhn_ref[...])
        dx_ref[...] = dx + dres_ref[...]
        _acc_out(dw_ref, dw, pl.program_id(0) == 0)

    row = pl.BlockSpec((tm, D), lambda i: (i, 0))
    vec = pl.BlockSpec((1, D), lambda i: (0, 0))
    return pl.pallas_call(
        body, grid=(L // tm,),
        in_specs=[row, vec, row, row],
        out_specs=[row, vec],
        out_shape=[jax.ShapeDtypeStruct((L, D), F32), jax.ShapeDtypeStruct((1, D), F32)],
        compiler_params=_cparams(("arbitrary",)), name=name,
    )(x, w, dhn, dres)


CONV_HALO = 8
CONV_COLS = 1024


CONV_RB = 64
CONV_CB = 256


def _conv_pieces(tm):
    return [(r0, c0) for c0 in range(0, CONV_COLS, CONV_CB) for r0 in range(0, tm, min(CONV_RB, tm))]


def _conv_fwd(proj, w, b):
    L = proj.shape[0]
    tm = min(TOKEN_TILE, L)
    rb = min(CONV_RB, tm)
    c0 = D_INNER // CONV_COLS
    hb = tm // CONV_HALO

    def body(x_ref, h_ref, w_ref, b_ref, o_ref, sp_ref, xx):
        xx[0:CONV_HALO] = jnp.where(pl.program_id(1) == 0, 0.0, h_ref[...])
        xx[CONV_HALO:CONV_HALO + tm] = x_ref[...]
        for r0, cc in _conv_pieces(tm):
            cs = slice(cc, cc + CONV_CB)
            u = b_ref[:, cs]
            for k in range(SSD_CONV):
                off = CONV_HALO - (SSD_CONV - 1) + k + r0
                u = u + w_ref[k:k + 1, cs] * xx[off:off + rb, cs]
            s = 1.0 / (1.0 + jnp.exp(-u))
            o_ref[r0:r0 + rb, cs] = u * s
            sp_ref[r0:r0 + rb, cs] = s * (1.0 + u * (1.0 - s))

    blk = pl.BlockSpec((tm, CONV_COLS), lambda j, i: (i, j))
    shp = jax.ShapeDtypeStruct((L, SSD_CONV_DIM), F32)
    return pl.pallas_call(
        body, grid=(SSD_CONV_DIM // CONV_COLS, L // tm),
        in_specs=[
            pl.BlockSpec((tm, CONV_COLS), lambda j, i: (i, c0 + j)),
            pl.BlockSpec((CONV_HALO, CONV_COLS), lambda j, i: (jnp.maximum(i * hb - 1, 0), c0 + j)),
            pl.BlockSpec((SSD_CONV, CONV_COLS), lambda j, i: (0, j)),
            pl.BlockSpec((1, CONV_COLS), lambda j, i: (0, j)),
        ],
        out_specs=[blk, blk],
        out_shape=[shp, shp],
        scratch_shapes=[pltpu.VMEM((CONV_HALO + tm, CONV_COLS), F32)],
        compiler_params=_cparams(("parallel", "parallel")), name="conv_fwd",
    )(proj, proj, w, b)


def _conv_bwd(proj, w, sp, dxbc, dproj):
    L = proj.shape[0]
    tm = min(TOKEN_TILE, L)
    rb = min(CONV_RB, tm)
    nt = L // tm
    c0 = D_INNER // CONV_COLS
    hb = tm // CONV_HALO

    def fold(a):
        return jnp.sum(a.reshape(rb // 8, 8, CONV_CB), axis=0)

    def body(x_ref, h_ref, w_ref, sp_ref, dy_ref, dp_in_ref, dp_ref, dw_ref, db_ref, xx, dd):
        del dp_in_ref
        i = pl.program_id(1)
        first = i == 0

        @pl.when(first)
        def _():
            dd[tm:tm + CONV_HALO] = jnp.zeros((CONV_HALO, CONV_COLS), F32)

        xx[0:CONV_HALO] = jnp.where(i == nt - 1, 0.0, h_ref[...])
        xx[CONV_HALO:CONV_HALO + tm] = x_ref[...]
        dws, dbs = [], []
        for cc in range(0, CONV_COLS, CONV_CB):
            cs = slice(cc, cc + CONV_CB)
            acc = [jnp.zeros((8, CONV_CB), F32) for _ in range(SSD_CONV + 1)]
            for r0 in range(0, tm, rb):
                du = dy_ref[r0:r0 + rb, cs] * sp_ref[r0:r0 + rb, cs]
                dd[r0:r0 + rb, cs] = du
                for k in range(SSD_CONV):
                    off = CONV_HALO - (SSD_CONV - 1) + k + r0
                    acc[k] = acc[k] + fold(du * xx[off:off + rb, cs])
                acc[SSD_CONV] = acc[SSD_CONV] + fold(du)
            dws.append(jnp.concatenate([jnp.sum(a, axis=0, keepdims=True) for a in acc[:SSD_CONV]], axis=0))
            dbs.append(jnp.sum(acc[SSD_CONV], axis=0, keepdims=True))
        for r0, cc in _conv_pieces(tm):
            cs = slice(cc, cc + CONV_CB)
            dx = jnp.zeros((rb, CONV_CB), F32)
            for k in range(SSD_CONV):
                off = SSD_CONV - 1 - k + r0
                dx = dx + w_ref[k:k + 1, cs] * dd[off:off + rb, cs]
            dp_ref[r0:r0 + rb, cs] = dx.astype(BF16)
        _acc_out(dw_ref, jnp.concatenate(dws, axis=1), first)
        _acc_out(db_ref, jnp.concatenate(dbs, axis=1), first)
        dd[tm:tm + CONV_HALO] = dd[0:CONV_HALO]

    rt = lambda i: nt - 1 - i
    return pl.pallas_call(
        body, grid=(SSD_CONV_DIM // CONV_COLS, nt),
        in_specs=[
            pl.BlockSpec((tm, CONV_COLS), lambda j, i: (rt(i), c0 + j)),
            pl.BlockSpec((CONV_HALO, CONV_COLS), lambda j, i: (jnp.maximum(rt(i) * hb - 1, 0), c0 + j)),
            pl.BlockSpec((SSD_CONV, CONV_COLS), lambda j, i: (0, j)),
            pl.BlockSpec((tm, CONV_COLS), lambda j, i: (rt(i), j)),
            pl.BlockSpec((tm, CONV_COLS), lambda j, i: (rt(i), j)),
            pl.BlockSpec(memory_space=pl.ANY),
        ],
        out_specs=[
            pl.BlockSpec((tm, CONV_COLS), lambda j, i: (rt(i), c0 + j)),
            pl.BlockSpec((SSD_CONV, CONV_COLS), lambda j, i: (0, j)),
            pl.BlockSpec((1, CONV_COLS), lambda j, i: (0, j)),
        ],
        out_shape=[jax.ShapeDtypeStruct((L, SSD_PROJ_PAD), BF16), jax.ShapeDtypeStruct((SSD_CONV, SSD_CONV_DIM), F32),
                   jax.ShapeDtypeStruct((1, SSD_CONV_DIM), F32)],
        scratch_shapes=[pltpu.VMEM((CONV_HALO + tm, CONV_COLS), F32), pltpu.VMEM((tm + CONV_HALO, CONV_COLS), F32)],
        input_output_aliases={5: 0},
        compiler_params=_cparams(("arbitrary", "arbitrary")), name="conv_bwd",
    )(proj, proj, w, sp, dxbc, dproj)


def _loss_bwd(x, tgt, w):
    L, D = x.shape
    tm = min(TOKEN_TILE, L)

    def body(x_ref, t_ref, w_ref, l_ref, dx_ref, dw_ref):
        xv = x_ref[...]
        wv = w_ref[...]
        r = lax.rsqrt(jnp.mean(xv * xv, axis=-1, keepdims=True) + RMS_EPS)
        xh = xv * r
        e = xh * wv - t_ref[...]
        lsum = 0.5 * jnp.sum(jnp.mean(e * e, axis=-1, keepdims=True), axis=0, keepdims=True)
        dout = e * (1.0 / D)
        gx = dout * wv
        dx_ref[...] = r * (gx - xh * jnp.mean(gx * xh, axis=-1, keepdims=True))
        first = pl.program_id(0) == 0
        _acc_out(dw_ref, jnp.sum(dout * xh, axis=0, keepdims=True), first)
        _acc_out(l_ref, jnp.broadcast_to(lsum, (8, 128)), first)

    row = pl.BlockSpec((tm, D), lambda i: (i, 0))
    vec = pl.BlockSpec((1, D), lambda i: (0, 0))
    return pl.pallas_call(
        body, grid=(L // tm,),
        in_specs=[row, row, vec],
        out_specs=[pl.BlockSpec((8, 128), lambda i: (0, 0)), row, vec],
        out_shape=[jax.ShapeDtypeStruct((8, 128), F32), jax.ShapeDtypeStruct((L, D), F32), jax.ShapeDtypeStruct((1, D), F32)],
        compiler_params=_cparams(("arbitrary",)), name="loss_bwd",
    )(x, tgt, w)


MESH = pl.DeviceIdType.MESH
ANY = pl.BlockSpec(memory_space=pl.ANY)


def _comm_sems(n):
    return [pltpu.SemaphoreType.DMA((n, 7)), pltpu.SemaphoreType.DMA((n, 7)), pltpu.SemaphoreType.DMA((n,))]


def _gather_ops(x_refs, out_refs, send_sems, recv_sems, local_sems):
    n = len(x_refs)
    x, y, c = lax.axis_index("x"), lax.axis_index("y"), lax.axis_index("c")
    me, sibling = (x, y, c), (x, y, 1 - c)
    chips = [(1 - x, y), (x, 1 - y), (1 - x, 1 - y)]

    def slot(i, px, py, pc):
        return out_refs[i].at[4 * px + 2 * py + pc]

    def copy(i, k, block, to, src=None):
        return pltpu.make_async_remote_copy(
            src_ref=slot(i, *block) if src is None else src, dst_ref=slot(i, *block),
            send_sem=send_sems.at[i, k], recv_sem=recv_sems.at[i, k], device_id=to, device_id_type=MESH)

    def own():
        mine = [pltpu.make_async_copy(x_refs[i], slot(i, *me), local_sems.at[i]) for i in range(n)]
        first = [copy(i, 0, me, sibling, src=x_refs[i]) for i in range(n)]
        first += [copy(i, 1 + j, me, (*chip, c), src=x_refs[i]) for j, chip in enumerate(chips) for i in range(n)]
        return mine, first

    def start():
        mine, first = own()
        for cp in mine + first:
            cp.start()

    def finish():
        mine, first = own()
        passed = []
        for j, chip in enumerate(chips):
            for i in range(n):
                copy(i, 1 + j, (*chip, c), me).wait_recv()
                passed.append(copy(i, 4 + j, (*chip, c), sibling))
                passed[-1].start()
        for i in range(n):
            copy(i, 0, sibling, me).wait_recv()
        for j, chip in enumerate(chips):
            for i in range(n):
                copy(i, 4 + j, (*chip, 1 - c), me).wait_recv()
        for cp in first + passed:
            cp.wait_send()
        for cp in mine:
            cp.wait()

    return start, finish


def _exchange_ops(p_refs, out_refs, send_sems, recv_sems, local_sems):
    n = len(p_refs)
    x, y, c = lax.axis_index("x"), lax.axis_index("y"), lax.axis_index("c")
    my = 4 * x + 2 * y + c

    def peer(k):
        fx, fy, fc = (k >> 2) & 1, (k >> 1) & 1, k & 1
        px, py, pc = (1 - x if fx else x), (1 - y if fy else y), (1 - c if fc else c)
        return (px, py, pc), 4 * px + 2 * py + pc

    def mine():
        return [pltpu.make_async_copy(p_refs[i].at[my], out_refs[i].at[my], local_sems.at[i]) for i in range(n)]

    def start():
        for cp in mine():
            cp.start()
        for k in range(1, N_DEV):
            to, pid = peer(k)
            for i in range(n):
                pltpu.make_async_remote_copy(
                    src_ref=p_refs[i].at[pid], dst_ref=out_refs[i].at[my], send_sem=send_sems.at[i, k - 1],
                    recv_sem=recv_sems.at[i, k - 1], device_id=to, device_id_type=MESH).start()

    def finish():
        for k in range(1, N_DEV):
            to, pid = peer(k)
            for i in range(n):
                pltpu.make_async_remote_copy(
                    src_ref=p_refs[i].at[pid], dst_ref=out_refs[i].at[pid], send_sem=send_sems.at[i, k - 1],
                    recv_sem=recv_sems.at[i, k - 1], device_id=to, device_id_type=MESH).wait()
        for cp in mine():
            cp.wait()

    return start, finish


def _all_gather(xs, *, name):
    n = len(xs)

    def body(*refs):
        start, finish = _gather_ops(refs[:n], refs[n:2 * n], *refs[2 * n:])
        start()
        finish()

    return pl.pallas_call(
        body,
        out_shape=[jax.ShapeDtypeStruct((N_DEV,) + a.shape, a.dtype) for a in xs],
        in_specs=[ANY] * n, out_specs=[ANY] * n, scratch_shapes=_comm_sems(n), name=name,
    )(*xs)


def _adamw(parts, w, m, v, *, name):
    a, b = w.shape
    tr = _pick(a, (256, 128, 64, 32, 16, 8))

    def body(p_ref, w_ref, m_ref, v_ref, g_ref, d_ref, mo_ref, vo_ref):
        g = p_ref[0]
        for s in range(1, N_DEV):
            g = g + p_ref[s]
        mn = ADAM_B1 * m_ref[...] + (1.0 - ADAM_B1) * g
        vn = ADAM_B2 * v_ref[...] + (1.0 - ADAM_B2) * jnp.square(g)
        m_hat = mn / (1.0 - ADAM_B1 ** ADAM_STEP)
        v_hat = vn / (1.0 - ADAM_B2 ** ADAM_STEP)
        g_ref[...] = g
        d_ref[...] = -ADAM_LR * (m_hat / (jnp.sqrt(v_hat) + ADAM_EPS) + ADAM_WD * w_ref[...])
        mo_ref[...] = mn
        vo_ref[...] = vn

    blk = pl.BlockSpec((tr, b), lambda i: (i, 0))
    shp = jax.ShapeDtypeStruct((a, b), F32)
    return pl.pallas_call(
        body, grid=(a // tr,),
        in_specs=[pl.BlockSpec((N_DEV, tr, b), lambda i: (0, i, 0)), blk, blk, blk],
        out_specs=[blk, blk, blk, blk],
        out_shape=[shp, shp, shp, shp],
        compiler_params=_cparams(("parallel",)), name=name,
    )(parts, w, m, v)


def _col_shards(a, n):
    return a.reshape(a.shape[0], N_DEV, n).transpose(1, 0, 2)


def _from_col_shards(g, cols):
    r = g.shape[1]
    full = g.transpose(1, 0, 2).reshape(r, -1)
    return jnp.pad(full, ((0, 0), (0, cols - full.shape[1])))


def kernel(x, norm_w, gla_in_proj, gla_gate_up, gla_gate_bias, gla_head_norm, gla_out_proj, ssd_in_proj, ssd_conv_w, ssd_conv_b, ssd_dt_bias, ssd_a_log, ssd_d, ssd_gate_norm, ssd_out_proj, final_norm, loss_target, m_norm_w, m_gla_in_proj, m_gla_gate_up, m_gla_gate_bias, m_gla_head_norm, m_gla_out_proj, m_ssd_in_proj, m_ssd_conv_w, m_ssd_conv_b, m_ssd_dt_bias, m_ssd_a_log, m_ssd_d, m_ssd_gate_norm, m_ssd_out_proj, m_final_norm, v_norm_w, v_gla_in_proj, v_gla_gate_up, v_gla_gate_bias, v_gla_head_norm, v_gla_out_proj, v_ssd_in_proj, v_ssd_conv_w, v_ssd_conv_b, v_ssd_dt_bias, v_ssd_a_log, v_ssd_d, v_ssd_gate_norm, v_ssd_out_proj, v_final_norm):
    x0 = x[0]
    tgt = loss_target[0]
    n_gin = GLA_PROJ // N_DEV
    n_sin = SSD_PROJ // N_DEV
    n_up = GLA_DK // N_DEV
    n_cv = SSD_CONV_DIM // N_DEV

    g_gin, g_up, g_gout = _all_gather(
        [gla_in_proj[0].astype(BF16), gla_gate_up[0].astype(BF16), gla_out_proj[0].astype(BF16)], name="gather_weights")
    w_gin = _from_col_shards(g_gin, GLA_PROJ_PAD)
    wup = jnp.pad(_from_col_shards(g_up, GLA_DK), ((0, 128 - GLA_RANK), (0, 0))).astype(F32)
    w_gout = g_gout.reshape(D_INNER, D_MODEL)
    vec128 = lambda a: jnp.pad(a.reshape(1, -1), ((0, 0), (0, 128 - a.size)))
    dtb, alog, dsk = vec128(ssd_dt_bias), vec128(ssd_a_log), vec128(ssd_d)
    nw0, nw1 = norm_w[0:1], norm_w[1:2]

    hn1 = _rms_fwd(x0, nw0, name="rms1_fwd")
    proj1 = _mm(hn1, w_gin, name="gla_in_proj")
    (o, og, s_saved), (g_sin, g_sout, g_cw, g_cb, g_gn) = _gla_layer_fwd(
        proj1, wup, gla_gate_bias, gla_head_norm,
        [ssd_in_proj[0].astype(BF16), ssd_out_proj[0].astype(BF16), ssd_conv_w[0], ssd_conv_b, ssd_gate_norm])
    w_sin = _from_col_shards(g_sin, SSD_PROJ_PAD)
    w_sout = g_sout.reshape(D_INNER, D_MODEL)
    conv_w = _from_col_shards(g_cw, SSD_CONV_DIM)
    conv_b = g_cb.reshape(1, SSD_CONV_DIM)
    gate_norm = g_gn.reshape(1, D_INNER)
    x1 = _mm(og, w_gout, add=x0, name="gla_out_proj")
    hn2 = _rms_fwd(x1, nw1, name="rms2_fwd")
    proj2 = _mm(hn2, w_sin, name="ssd_in_proj")
    xbc, conv_sp = _conv_fwd(proj2, conv_w, conv_b)
    yn, h_saved = _ssd_layer_fwd(xbc, proj2, dtb, alog, dsk, gate_norm)
    x2 = _mm(yn, w_sout, add=x1, name="ssd_out_proj")
    lsum, dx2, d_final = _loss_bwd(x2, tgt, final_norm.reshape(1, D_MODEL))
    loss = lax.psum(lsum[0, 0], ("x", "y", "c"))

    d_sout = _mm_tn(yn, dx2, name="ssd_out_proj_dw")
    dyn = _mm(dx2, w_sout.T, name="ssd_out_proj_dx")
    dproj2, dxbc, d_dtb, d_alog, d_dsk, d_gate_norm = _ssd_layer_bwd(xbc, proj2, dtb, alog, dsk, gate_norm, h_saved, dyn)
    dproj2, d_conv_w, d_conv_b = _conv_bwd(proj2, conv_w, conv_sp, dxbc, dproj2)
    d_sin = _mm_tn(hn2, dproj2, name="ssd_in_proj_dw")
    dhn2 = _mm(dproj2, w_sin.T, name="ssd_in_proj_dx")
    dx1, d_nw1 = _rms_bwd(x1, nw1, dhn2, dx2, name="rms2_bwd")
    d_gout = _mm_tn(og, dx1, name="gla_out_proj_dw")
    dog = _mm(dx1, w_gout.T, name="gla_out_proj_dx")
    early = {
        "gla_out_proj": ((gla_out_proj[0], m_gla_out_proj[0], v_gla_out_proj[0]), d_gout.reshape(N_DEV, -1, D_MODEL)),
        "ssd_in_proj": ((ssd_in_proj[0], m_ssd_in_proj[0], v_ssd_in_proj[0]), _col_shards(d_sin[:, :SSD_PROJ], n_sin)),
        "ssd_conv_w": ((ssd_conv_w[0], m_ssd_conv_w[0], v_ssd_conv_w[0]), _col_shards(d_conv_w, n_cv)),
        "ssd_conv_b": ((ssd_conv_b, m_ssd_conv_b, v_ssd_conv_b), d_conv_b.reshape(N_DEV, 1, n_cv)),
        "ssd_gate_norm": ((ssd_gate_norm, m_ssd_gate_norm, v_ssd_gate_norm), d_gate_norm.reshape(N_DEV, 1, -1)),
        "ssd_out_proj": ((ssd_out_proj[0], m_ssd_out_proj[0], v_ssd_out_proj[0]), d_sout.reshape(N_DEV, -1, D_MODEL)),
    }
    (dproj1, d_wup, d_gbias, d_head_norm), early_recv = _gla_layer_bwd(
        proj1, wup, gla_gate_bias, gla_head_norm, o, s_saved, dog, [p for _, p in early.values()])
    d_gin = _mm_tn(hn1, dproj1, name="gla_in_proj_dw")
    late = {
        "gla_in_proj": ((gla_in_proj[0], m_gla_in_proj[0], v_gla_in_proj[0]), _col_shards(d_gin[:, :GLA_PROJ], n_gin)),
        "gla_gate_up": ((gla_gate_up[0], m_gla_gate_up[0], v_gla_gate_up[0]), _col_shards(d_wup[:GLA_RANK], n_up)),
    }
    dhn1, late_recv = _mm(dproj1, w_gin.T, name="gla_in_proj_dx", exchange=[p for _, p in late.values()])
    dx0, d_nw0 = _rms_bwd(x0, nw0, dhn1, dx1, name="rms1_bwd")
    heads = SSD_HEADS
    replicated = {
        "norm_w": ((norm_w, m_norm_w, v_norm_w), jnp.concatenate([d_nw0, d_nw1], axis=0)),
        "gla_gate_bias": ((gla_gate_bias, m_gla_gate_bias, v_gla_gate_bias), d_gbias),
        "gla_head_norm": ((gla_head_norm, m_gla_head_norm, v_gla_head_norm), d_head_norm),
        "ssd_dt_bias": ((ssd_dt_bias, m_ssd_dt_bias, v_ssd_dt_bias), d_dtb[:, :heads]),
        "ssd_a_log": ((ssd_a_log, m_ssd_a_log, v_ssd_a_log), d_alog[:, :heads]),
        "ssd_d": ((ssd_d, m_ssd_d, v_ssd_d), d_dsk[:, :heads]),
        "final_norm": (tuple(t.reshape(1, D_MODEL) for t in (final_norm, m_final_norm, v_final_norm)), d_final),
    }
    results = {}
    for group, recv in ((early, early_recv),
                        (late, late_recv),
                        (replicated, _all_gather([p for _, p in replicated.values()], name="replicated_gather"))):
        for (nm, ((w, m, v), _)), r in zip(group.items(), recv):
            results[nm] = _adamw(r, w, m, v, name=nm + "_adamw")

    order = [("norm_w", norm_w), ("gla_in_proj", gla_in_proj), ("gla_gate_up", gla_gate_up), ("gla_gate_bias", gla_gate_bias),
             ("gla_head_norm", gla_head_norm), ("gla_out_proj", gla_out_proj), ("ssd_in_proj", ssd_in_proj),
             ("ssd_conv_w", ssd_conv_w), ("ssd_conv_b", ssd_conv_b), ("ssd_dt_bias", ssd_dt_bias), ("ssd_a_log", ssd_a_log),
             ("ssd_d", ssd_d), ("ssd_gate_norm", ssd_gate_norm), ("ssd_out_proj", ssd_out_proj), ("final_norm", final_norm)]
    out = [loss, dx0[None]]
    for i in range(4):
        out += [results[nm][i].reshape(ref.shape) for nm, ref in order]
    return tuple(out)
```

```python
import jax
import jax.numpy as jnp
from jax import lax
from jax.experimental import pallas as pl
from jax.experimental.pallas import tpu as pltpu

F32 = jnp.float32
BF16 = jnp.bfloat16

D_MODEL = 1024
D_INNER = 2048
RMS_EPS = 1e-6
GLA_HEADS = 4
GLA_DK = 512
GLA_HEAD_K = 128
GLA_HEAD_V = 512
GLA_RANK = 16
GLA_NORMALIZER = 16.0
CHUNK = 64
SUB = 16
GLA_PROJ = 5136
GLA_PROJ_PAD = 5376
GLA_GK_COL = 5120
SSD_HEADS = 32
SSD_GROUPS = 8
SSD_HPG = 4
SSD_P = 64
SSD_N = 128
SSD_CONV = 4
SSD_CONV_DIM = 4096
SSD_PROJ = 6176
SSD_PROJ_PAD = 6400
SSD_DT_COL = 6144
N_DEV = 8

ADAM_LR = 0.001
ADAM_B1 = 0.9
ADAM_B2 = 0.999
ADAM_EPS = 1e-08
ADAM_WD = 0.01
ADAM_STEP = 10

VMEM_LIMIT = 56 * 1024 * 1024
TOKEN_TILE = 512
MM_TOKEN_TILE = 2048
MM_VMEM_BUDGET = 44 * 1024 * 1024
MM_TILES = (1792, 1280, 1024, 768, 512, 256, 128)


def _dot(a, b):
    return jnp.dot(a, b, preferred_element_type=F32)


def _dot_nt(a, b):
    return lax.dot_general(a, b, (((1,), (1,)), ((), ())), preferred_element_type=F32)


def _dot_tn(a, b):
    return lax.dot_general(a, b, (((0,), (0,)), ((), ())), preferred_element_type=F32)


def _bf(a):
    return a.astype(BF16)


@jax.custom_vjp
def _mxu(a, b):
    return _dot(_bf(a), _bf(b))


def _mxu_fwd(a, b):
    return _mxu(a, b), (a, b)


def _mxu_bwd(res, g):
    a, b = res
    return _dot_nt(_bf(g), _bf(b)), _dot_tn(_bf(a), _bf(g))


_mxu.defvjp(_mxu_fwd, _mxu_bwd)


@jax.custom_vjp
def _mxu_nt(a, b):
    return _dot_nt(_bf(a), _bf(b))


def _mxu_nt_fwd(a, b):
    return _mxu_nt(a, b), (a, b)


def _mxu_nt_bwd(res, g):
    a, b = res
    return _dot(_bf(g), _bf(b)), _dot_tn(_bf(g), _bf(a))


_mxu_nt.defvjp(_mxu_nt_fwd, _mxu_nt_bwd)


@jax.custom_vjp
def _mxu_tn(a, b):
    return _dot_tn(_bf(a), _bf(b))


def _mxu_tn_fwd(a, b):
    return _mxu_tn(a, b), (a, b)


def _mxu_tn_bwd(res, g):
    a, b = res
    return _dot_nt(_bf(b), _bf(g)), _dot(_bf(a), _bf(g))


_mxu_tn.defvjp(_mxu_tn_fwd, _mxu_tn_bwd)


def _split2(a):
    hi = _bf(a)
    return hi, _bf(a - hi.astype(F32))


def _three_pass(dot, a, b):
    ah, al = _split2(a)
    bh, bl = _split2(b)
    return dot(ah, bh) + (dot(ah, bl) + dot(al, bh))


@jax.custom_vjp
def _dot3_nt(a, b):
    return _three_pass(_dot_nt, a, b)


def _dot3_nt_fwd(a, b):
    return _dot3_nt(a, b), (a, b)


def _dot3_nt_bwd(res, g):
    a, b = res
    return _three_pass(_dot, g, b), _three_pass(_dot_tn, g, a)


_dot3_nt.defvjp(_dot3_nt_fwd, _dot3_nt_bwd)


def _silu(x):
    return x / (1.0 + jnp.exp(-x))


def _log_sigmoid(z):
    return jnp.minimum(z, 0.0) - jnp.log(1.0 + jnp.exp(-jnp.abs(z)))


def _softplus(z):
    return jnp.maximum(z, 0.0) + jnp.log(1.0 + jnp.exp(-jnp.abs(z)))


def _iota(shape, dim):
    return lax.broadcasted_iota(jnp.int32, shape, dim)


def _rms(x, w):
    return x * lax.rsqrt(jnp.mean(x * x, axis=-1, keepdims=True) + RMS_EPS) * w


def _scan_rows(a, reverse, seg):
    n = a.shape[0]
    pos = _iota(a.shape, 0) & (seg - 1)
    sh = 1
    while sh < seg:
        if reverse:
            a = a + jnp.where(pos < seg - sh, pltpu.roll(a, n - sh, 0), 0.0)
        else:
            a = a + jnp.where(pos >= sh, pltpu.roll(a, sh, 0), 0.0)
        sh *= 2
    return a


def _make_cumsum(seg):
    @jax.custom_vjp
    def cumsum(a):
        return _scan_rows(a, False, seg)

    cumsum.defvjp(lambda a: (_scan_rows(a, False, seg), None), lambda _, g: (_scan_rows(g, True, seg),))
    return cumsum


_cumsum_sub = _make_cumsum(SUB)
_cumsum_rows = _make_cumsum(CHUNK)


def _cparams(sem):
    return pltpu.CompilerParams(dimension_semantics=sem, vmem_limit_bytes=VMEM_LIMIT)


def _acc_out(ref, val, first):
    @pl.when(first)
    def _():
        ref[...] = val

    @pl.when(jnp.logical_not(first))
    def _():
        ref[...] += val


def _gla_chunk(q, k, v, gk, wup, bias, St):
    nb = CHUNK // SUB
    z = _mxu(gk, wup) + bias
    la = _log_sigmoid(z) * (1.0 / GLA_NORMALIZER)
    qs = q * (GLA_HEAD_K ** -0.5)
    bl = _cumsum_sub(la)
    tot = [jnp.sum(la[i * SUB:(i + 1) * SUB], axis=0, keepdims=True) for i in range(nb)]
    pre = [jnp.zeros((1, GLA_HEAD_K), F32)]
    for i in range(nb):
        pre.append(pre[i] + tot[i])
    b_last = pre[nb]
    rows_of = lambda vals: jnp.concatenate([jnp.broadcast_to(t, (SUB, GLA_HEAD_K)) for t in vals], axis=0)
    suf = rows_of(tot) - bl
    nxt = rows_of(pre[1:])
    o = _mxu_nt(qs * jnp.exp(bl + rows_of(pre[:nb])), St)
    St_new = St * jnp.exp(b_last) + _mxu_tn(v, k * jnp.exp(suf + (b_last - nxt)))
    qa = qs * jnp.exp(bl)
    half = SUB // 2
    rs = _iota((SUB, GLA_HEAD_K), 0)
    cs = _iota((half, CHUNK), 1)
    a_rows = []
    for i in range(nb):
        sl = slice(i * SUB, (i + 1) * SUB)
        q_i, k_i, bl_i = qs[sl], k[sl], bl[sl]
        if i > 0:
            n = i * SUB
            kp = k[:n] * jnp.exp(suf[:n] + (pre[i] - nxt[:n]))
            a_i = _dot3_nt(qa[sl], jnp.concatenate([kp, jnp.zeros((CHUNK - n, GLA_HEAD_K), F32)], axis=0))
            a_top, a_bot = a_i[:half], a_i[half:]
        else:
            a_top = a_bot = jnp.zeros((half, CHUNK), F32)
        for j in range(SUB):
            lo = 0 if j < half else half
            e = jnp.exp(jnp.minimum(bl_i[lo:] - bl_i[j:j + 1], 0.0))
            t = jnp.where(rs[lo:] >= j, q_i[lo:] * e * k_i[j:j + 1], 0.0)
            rsum = jnp.sum(t, axis=-1, keepdims=True)
            hit = cs == i * SUB + j
            if lo == 0:
                a_top = a_top + jnp.where(hit, rsum[:half], 0.0)
            a_bot = a_bot + jnp.where(hit, rsum[half - lo:], 0.0)
        a_rows += [a_top, a_bot]
    o = o + _mxu(jnp.concatenate(a_rows, axis=0), v)
    return o, St_new


def _gla_post(o, g, wn):
    return _rms(o, wn) * _silu(g)


GLA_HALF = 2 * GLA_HEAD_V


def _gla_specs(nc, rev):
    ci = (lambda c: nc - 1 - c) if rev else (lambda c: c)
    v0 = 2 * GLA_DK // GLA_HALF
    g0 = (2 * GLA_DK + D_INNER) // GLA_HALF
    return [
        pl.BlockSpec((CHUNK, GLA_DK), lambda c: (ci(c), 0)),
        pl.BlockSpec((CHUNK, GLA_DK), lambda c: (ci(c), 1)),
        pl.BlockSpec((CHUNK, GLA_HALF), lambda c: (ci(c), v0)),
        pl.BlockSpec((CHUNK, GLA_HALF), lambda c: (ci(c), v0 + 1)),
        pl.BlockSpec((CHUNK, GLA_HALF), lambda c: (ci(c), g0)),
        pl.BlockSpec((CHUNK, GLA_HALF), lambda c: (ci(c), g0 + 1)),
        pl.BlockSpec((CHUNK, 128), lambda c: (ci(c), GLA_GK_COL // 128)),
        pl.BlockSpec((128, GLA_DK), lambda c: (0, 0)),
        pl.BlockSpec((1, GLA_DK), lambda c: (0, 0)),
        pl.BlockSpec((1, GLA_HEAD_V), lambda c: (0, 0)),
    ]


def _head_cols(ref_a, ref_b, h):
    ref = ref_a if h < 2 else ref_b
    return ref[:, (h % 2) * GLA_HEAD_V:(h % 2 + 1) * GLA_HEAD_V]


def _gla_layer_fwd(proj, wup, bias, wn, gather):
    L = proj.shape[0]
    nc = L // CHUNK
    n = len(gather)

    def body(*refs):
        q_ref, k_ref, va_ref, vb_ref, ga_ref, gb_ref, gk_ref, wup_ref, b_ref, wn_ref = refs[:10]
        x_refs = refs[10:10 + n]
        o_ref, og_ref, s_ref = refs[10 + n:13 + n]
        out_refs = refs[13 + n:13 + 2 * n]
        st, send_sems, recv_sems, local_sems = refs[13 + 2 * n:]
        start, finish = _gather_ops(x_refs, out_refs, send_sems, recv_sems, local_sems)

        @pl.when(pl.program_id(0) == 0)
        def _():
            st[...] = jnp.zeros(st.shape, F32)
            start()

        gk = gk_ref[...]
        for h in range(GLA_HEADS):
            kc = slice(h * GLA_HEAD_K, (h + 1) * GLA_HEAD_K)
            vc = slice(h * GLA_HEAD_V, (h + 1) * GLA_HEAD_V)
            s_in = st[h]
            s_ref[h] = s_in
            o, s_new = _gla_chunk(q_ref[:, kc], k_ref[:, kc], _head_cols(va_ref, vb_ref, h), gk, wup_ref[:, kc], b_ref[:, kc], s_in)
            st[h] = s_new
            o_ref[:, vc] = o
            og_ref[:, vc] = _gla_post(o, _head_cols(ga_ref, gb_ref, h), wn_ref[...]).astype(BF16)

        @pl.when(pl.program_id(0) == nc - 1)
        def _():
            finish()

    res = pl.pallas_call(
        body,
        grid=(nc,),
        in_specs=_gla_specs(nc, False) + [ANY] * n,
        out_specs=[
            pl.BlockSpec((CHUNK, D_INNER), lambda c: (c, 0)),
            pl.BlockSpec((CHUNK, D_INNER), lambda c: (c, 0)),
            pl.BlockSpec((None, GLA_HEADS, GLA_HEAD_V, GLA_HEAD_K), lambda c: (c, 0, 0, 0)),
        ] + [ANY] * n,
        out_shape=[
            jax.ShapeDtypeStruct((L, D_INNER), F32),
            jax.ShapeDtypeStruct((L, D_INNER), BF16),
            jax.ShapeDtypeStruct((nc, GLA_HEADS, GLA_HEAD_V, GLA_HEAD_K), F32),
        ] + [jax.ShapeDtypeStruct((N_DEV,) + a.shape, a.dtype) for a in gather],
        scratch_shapes=[pltpu.VMEM((GLA_HEADS, GLA_HEAD_V, GLA_HEAD_K), F32)] + _comm_sems(n),
        compiler_params=_cparams(("arbitrary",)),
        name="gla_layer_fwd",
    )(proj, proj, proj, proj, proj, proj, proj, wup, bias, wn, *gather)
    return res[:3], res[3:]


def _gla_layer_bwd(proj, wup, bias, wn, o, s_in, dog, exchange):
    L = proj.shape[0]
    nc = L // CHUNK
    n = len(exchange)

    def body(*refs):
        (q_ref, k_ref, va_ref, vb_ref, ga_ref, gb_ref, gk_ref, wup_ref, b_ref, wn_ref, o_ref, s_ref, dog_ref) = refs[:13]
        p_refs = refs[13:13 + n]
        dp_ref, dwup_ref, db_ref, dwn_ref = refs[13 + n:17 + n]
        out_refs = refs[17 + n:17 + 2 * n]
        dst, send_sems, recv_sems, local_sems = refs[17 + 2 * n:]
        start, finish = _exchange_ops(p_refs, out_refs, send_sems, recv_sems, local_sems)

        @pl.when(pl.program_id(0) == 0)
        def _():
            dst[...] = jnp.zeros(dst.shape, F32)
            dwup_ref[...] = jnp.zeros(dwup_ref.shape, F32)
            db_ref[...] = jnp.zeros(db_ref.shape, F32)
            dwn_ref[...] = jnp.zeros(dwn_ref.shape, F32)
            start()

        gk = gk_ref[...]
        dgk_sum = jnp.zeros((CHUNK, 128), F32)
        for h in range(GLA_HEADS):
            kc = slice(h * GLA_HEAD_K, (h + 1) * GLA_HEAD_K)
            vc = slice(h * GLA_HEAD_V, (h + 1) * GLA_HEAD_V)
            _, post_vjp = jax.vjp(_gla_post, o_ref[:, vc], _head_cols(ga_ref, gb_ref, h), wn_ref[...])
            do, dg, dwn = post_vjp(dog_ref[:, vc])
            _, vjp = jax.vjp(_gla_chunk, q_ref[:, kc], k_ref[:, kc], _head_cols(va_ref, vb_ref, h), gk, wup_ref[:, kc],
                             b_ref[:, kc], s_ref[h])
            dq, dk, dv, dgk, dwup, db, ds = vjp((do, dst[h]))
            dst[h] = ds
            dp_ref[:, kc] = dq.astype(BF16)
            dp_ref[:, GLA_DK + h * GLA_HEAD_K:GLA_DK + (h + 1) * GLA_HEAD_K] = dk.astype(BF16)
            dp_ref[:, 2 * GLA_DK + h * GLA_HEAD_V:2 * GLA_DK + (h + 1) * GLA_HEAD_V] = dv.astype(BF16)
            dp_ref[:, 2 * GLA_DK + D_INNER + h * GLA_HEAD_V:2 * GLA_DK + D_INNER + (h + 1) * GLA_HEAD_V] = dg.astype(BF16)
            dwup_ref[:, kc] += dwup
            db_ref[:, kc] += db
            dwn_ref[...] += dwn
            dgk_sum = dgk_sum + dgk
        dp_ref[:, GLA_GK_COL:GLA_GK_COL + 128] = dgk_sum.astype(BF16)
        dp_ref[:, GLA_GK_COL + 128:] = jnp.zeros((CHUNK, GLA_PROJ_PAD - GLA_GK_COL - 128), BF16)

        @pl.when(pl.program_id(0) == nc - 1)
        def _():
            finish()

    rc = lambda c: nc - 1 - c
    res = pl.pallas_call(
        body,
        grid=(nc,),
        in_specs=_gla_specs(nc, True) + [
            pl.BlockSpec((CHUNK, D_INNER), lambda c: (rc(c), 0)),
            pl.BlockSpec((None, GLA_HEADS, GLA_HEAD_V, GLA_HEAD_K), lambda c: (rc(c), 0, 0, 0)),
            pl.BlockSpec((CHUNK, D_INNER), lambda c: (rc(c), 0)),
        ] + [ANY] * n,
        out_specs=[
            pl.BlockSpec((CHUNK, GLA_PROJ_PAD), lambda c: (rc(c), 0)),
            pl.BlockSpec((128, GLA_DK), lambda c: (0, 0)),
            pl.BlockSpec((1, GLA_DK), lambda c: (0, 0)),
            pl.BlockSpec((1, GLA_HEAD_V), lambda c: (0, 0)),
        ] + [ANY] * n,
        out_shape=[
            jax.ShapeDtypeStruct((L, GLA_PROJ_PAD), BF16),
            jax.ShapeDtypeStruct((128, GLA_DK), F32),
            jax.ShapeDtypeStruct((1, GLA_DK), F32),
            jax.ShapeDtypeStruct((1, GLA_HEAD_V), F32),
        ] + [jax.ShapeDtypeStruct(a.shape, a.dtype) for a in exchange],
        scratch_shapes=[pltpu.VMEM((GLA_HEADS, GLA_HEAD_V, GLA_HEAD_K), F32)] + _comm_sems(n),
        compiler_params=_cparams(("arbitrary",)),
        name="gla_layer_bwd",
    )(proj, proj, proj, proj, proj, proj, proj, wup, bias, wn, o, s_in, dog, *exchange)
    return res[:4], res[4:]


@jax.custom_vjp
def _expand(v):
    r = v.shape[0]
    left = _iota((r, 128), 1) < SSD_P
    slabs = []
    for p in range(SSD_HEADS // 2):
        a = jnp.broadcast_to(v[:, 2 * p:2 * p + 1], (r, 128))
        b = jnp.broadcast_to(v[:, 2 * p + 1:2 * p + 2], (r, 128))
        slabs.append(jnp.where(left, a, b))
    return jnp.concatenate(slabs, axis=1)


def _expand_fwd(v):
    return _expand(v), None


def _expand_bwd(_, g):
    r = g.shape[0]
    lane = _iota((r, 128), 1)
    left = lane < SSD_P
    dv = jnp.zeros((r, 128), F32)
    for p in range(SSD_HEADS // 2):
        gs = g[:, 128 * p:128 * (p + 1)]
        sa = jnp.sum(jnp.where(left, gs, 0.0), axis=-1, keepdims=True)
        sb = jnp.sum(jnp.where(left, 0.0, gs), axis=-1, keepdims=True)
        dv = dv + jnp.where(lane == 2 * p, sa, 0.0) + jnp.where(lane == 2 * p + 1, sb, 0.0)
    return (dv,)


_expand.defvjp(_expand_fwd, _expand_bwd)

SSD_GW = SSD_HPG * SSD_P
SSD_BC = SSD_GROUPS * SSD_N
SSD_PHASE = 4


def _ssd_chunk(xs, Bm, Cm, dtp, dtb, alog, dsk, h_in):
    dt = _softplus(dtp + dtb)
    acum = _cumsum_rows(dt * (-jnp.exp(alog)))
    a_last = acum[CHUNK - 1:CHUNK]
    acum_b = _expand(acum)
    w_end = _expand(dt * jnp.exp(a_last - acum))
    d_b = _expand(jnp.broadcast_to(dsk, (8, 128)))[0:1]
    ac_t = jnp.concatenate([acum, acum], axis=0).T
    dt_t = jnp.concatenate([dt, dt], axis=0).T
    lane = _iota((CHUNK, 128), 1)
    left = lane < SSD_P
    causal = (lane & (SSD_P - 1)) <= _iota((CHUNK, 128), 0)
    cd = jnp.exp(ac_t[:, CHUNK - 1:CHUNK])
    ys, h_out = [], []
    for g0 in range(0, SSD_GROUPS, SSD_PHASE):
        cb2, y_off = {}, {}
        for g in range(g0, g0 + SSD_PHASE):
            Bg = Bm[:, g * SSD_N:(g + 1) * SSD_N]
            Cg = Cm[:, g * SSD_N:(g + 1) * SSD_N]
            gs = slice(g * SSD_GW, (g + 1) * SSD_GW)
            cb2[g] = _mxu_nt(Cg, jnp.concatenate([Bg, Bg], axis=0))
            y_off[g] = _mxu_nt(Cg, h_in[gs])
            st = _mxu_tn(xs[:, gs] * w_end[:, gs], Bg)
            hs = [h_in[h * SSD_P:(h + 1) * SSD_P] * cd[h:h + 1] for h in range(g * SSD_HPG, (g + 1) * SSD_HPG)]
            h_out.append(jnp.concatenate(hs, axis=0) + st)
        for p in range(g0 * (SSD_HPG // 2), (g0 + SSD_PHASE) * (SSD_HPG // 2)):
            g, k = divmod(p, SSD_HPG // 2)
            sl = slice(128 * p, 128 * (p + 1))
            ac_c = acum_b[:, sl]
            ac_r = jnp.where(left, ac_t[2 * p:2 * p + 1], ac_t[2 * p + 1:2 * p + 2])
            dt_r = jnp.where(left, dt_t[2 * p:2 * p + 1], dt_t[2 * p + 1:2 * p + 2])
            m2 = cb2[g] * jnp.where(causal, jnp.exp(jnp.minimum(ac_c - ac_r, 0.0)), 0.0) * dt_r
            xsl = xs[:, sl]
            x2 = jnp.concatenate([jnp.where(left, xsl, 0.0), jnp.where(left, 0.0, xsl)], axis=0)
            ys.append(_mxu(m2, x2) + y_off[g][:, 128 * k:128 * (k + 1)] * jnp.exp(ac_c) + xsl * d_b[:, sl])
    return jnp.concatenate(ys, axis=1), jnp.concatenate(h_out, axis=0)


def _ssd_step(xs, Bm, Cm, dtp, dtb, alog, dsk, h_in, z, wn):
    y, h_out = _ssd_chunk(xs, Bm, Cm, dtp, dtb, alog, dsk, h_in)
    return _rms(y * _silu(z), wn), h_out


def _ssd_specs(nc, rev):
    ci = (lambda c: nc - 1 - c) if rev else (lambda c: c)
    vec = pl.BlockSpec((1, 128), lambda c: (0, 0))
    return [
        pl.BlockSpec((CHUNK, D_INNER), lambda c: (ci(c), 0)),
        pl.BlockSpec((CHUNK, SSD_BC), lambda c: (ci(c), D_INNER // SSD_BC)),
        pl.BlockSpec((CHUNK, SSD_BC), lambda c: (ci(c), D_INNER // SSD_BC + 1)),
        pl.BlockSpec((CHUNK, 128), lambda c: (ci(c), SSD_DT_COL // 128)),
        vec, vec, vec,
        pl.BlockSpec((CHUNK, D_INNER), lambda c: (ci(c), 0)),
        pl.BlockSpec((1, D_INNER), lambda c: (0, 0)),
    ]


def _ssd_layer_fwd(xbc, proj, dtb, alog, dsk, wn):
    L = xbc.shape[0]
    nc = L // CHUNK

    def body(xs_ref, b_ref, c_ref, dt_ref, dtb_ref, alog_ref, dsk_ref, z_ref, wn_ref, y_ref, hs_ref, hst):
        @pl.when(pl.program_id(0) == 0)
        def _():
            hst[...] = jnp.zeros(hst.shape, F32)

        h_in = hst[...]
        hs_ref[...] = h_in
        yn, h_out = _ssd_step(xs_ref[...], b_ref[...], c_ref[...], dt_ref[...], dtb_ref[...], alog_ref[...], dsk_ref[...],
                              h_in, z_ref[...], wn_ref[...])
        y_ref[...] = yn.astype(BF16)
        hst[...] = h_out

    return pl.pallas_call(
        body,
        grid=(nc,),
        in_specs=_ssd_specs(nc, False),
        out_specs=[
            pl.BlockSpec((CHUNK, D_INNER), lambda c: (c, 0)),
            pl.BlockSpec((None, D_INNER, SSD_N), lambda c: (c, 0, 0)),
        ],
        out_shape=[
            jax.ShapeDtypeStruct((L, D_INNER), BF16),
            jax.ShapeDtypeStruct((nc, D_INNER, SSD_N), F32),
        ],
        scratch_shapes=[pltpu.VMEM((D_INNER, SSD_N), F32)],
        compiler_params=_cparams(("arbitrary",)),
        name="ssd_layer_fwd",
    )(xbc, xbc, xbc, proj, dtb, alog, dsk, proj, wn)


def _ssd_layer_bwd(xbc, proj, dtb, alog, dsk, wn, h_saved, dyn):
    L = xbc.shape[0]
    nc = L // CHUNK

    def body(xs_ref, b_ref, c_ref, dt_ref, dtb_ref, alog_ref, dsk_ref, z_ref, wn_ref, hs_ref, dyn_ref,
             dp_ref, dx_ref, ddtb_ref, dalog_ref, ddsk_ref, dwn_ref, dhst):
        @pl.when(pl.program_id(0) == 0)
        def _():
            dhst[...] = jnp.zeros(dhst.shape, F32)
            ddtb_ref[...] = jnp.zeros((1, 128), F32)
            dalog_ref[...] = jnp.zeros((1, 128), F32)
            ddsk_ref[...] = jnp.zeros((1, 128), F32)
            dwn_ref[...] = jnp.zeros((1, D_INNER), F32)

        _, vjp = jax.vjp(_ssd_step, xs_ref[...], b_ref[...], c_ref[...], dt_ref[...], dtb_ref[...], alog_ref[...],
                         dsk_ref[...], hs_ref[...], z_ref[...], wn_ref[...])
        dxs, db, dc, ddt, ddtb, dalog, ddsk, dh, dz, dwn = vjp((dyn_ref[...], dhst[...]))
        dx_ref[:, :D_INNER] = dxs
        dx_ref[:, D_INNER:D_INNER + SSD_BC] = db
        dx_ref[:, D_INNER + SSD_BC:] = dc
        dp_ref[:, :D_INNER] = dz.astype(BF16)
        dp_ref[:, D_INNER:SSD_DT_COL] = jnp.zeros((CHUNK, SSD_CONV_DIM), BF16)
        dp_ref[:, SSD_DT_COL:SSD_DT_COL + 128] = ddt.astype(BF16)
        dp_ref[:, SSD_DT_COL + 128:] = jnp.zeros((CHUNK, SSD_PROJ_PAD - SSD_DT_COL - 128), BF16)
        dhst[...] = dh
        ddtb_ref[...] += ddtb
        dalog_ref[...] += dalog
        ddsk_ref[...] += ddsk
        dwn_ref[...] += dwn

    rc = lambda c: nc - 1 - c
    vec = pl.BlockSpec((1, 128), lambda c: (0, 0))
    vshape = jax.ShapeDtypeStruct((1, 128), F32)
    return pl.pallas_call(
        body,
        grid=(nc,),
        in_specs=_ssd_specs(nc, True) + [
            pl.BlockSpec((None, D_INNER, SSD_N), lambda c: (rc(c), 0, 0)),
            pl.BlockSpec((CHUNK, D_INNER), lambda c: (rc(c), 0)),
        ],
        out_specs=[
            pl.BlockSpec((CHUNK, SSD_PROJ_PAD), lambda c: (rc(c), 0)),
            pl.BlockSpec((CHUNK, SSD_CONV_DIM), lambda c: (rc(c), 0)),
            vec, vec, vec,
            pl.BlockSpec((1, D_INNER), lambda c: (0, 0)),
        ],
        out_shape=[
            jax.ShapeDtypeStruct((L, SSD_PROJ_PAD), BF16),
            jax.ShapeDtypeStruct((L, SSD_CONV_DIM), F32),
            vshape, vshape, vshape,
            jax.ShapeDtypeStruct((1, D_INNER), F32),
        ],
        scratch_shapes=[pltpu.VMEM((D_INNER, SSD_N), F32)],
        compiler_params=_cparams(("arbitrary",)),
        name="ssd_layer_bwd",
    )(xbc, xbc, xbc, proj, dtb, alog, dsk, proj, wn, h_saved, dyn)


def _pick(n, options):
    for t in options:
        if n % t == 0:
            return t
    return n


def _token_tile(m, row_bytes, fixed_bytes):
    for t in (MM_TOKEN_TILE, MM_TOKEN_TILE // 2, MM_TOKEN_TILE // 4):
        if m % t == 0 and t * row_bytes + fixed_bytes <= MM_VMEM_BUDGET:
            return t
    return min(m, MM_TOKEN_TILE // 4)


def _mm(a, b, *, name, out_dtype=F32, add=None, exchange=()):
    M, K = a.shape
    N = b.shape[1]
    tn = _pick(N, MM_TILES)
    tk = _pick(K, MM_TILES)
    nk = K // tk
    row_bytes = 2 * (tk * a.dtype.itemsize + tn * jnp.dtype(out_dtype).itemsize + (tn * 4 if add is not None else 0)) \
        + (tn * 4 if nk > 1 else 0)
    tm = _token_tile(M, row_bytes, 2 * tk * tn * b.dtype.itemsize)
    grid = (M // tm, N // tn, nk)
    n = len(exchange)
    n_in = 2 + (add is not None)

    def body(*refs):
        a_ref, b_ref = refs[:2]
        add_ref = refs[2] if add is not None else None
        p_refs = refs[n_in:n_in + n]
        o_ref = refs[n_in + n]
        out_refs = refs[n_in + n + 1:n_in + 2 * n + 1]
        acc = refs[n_in + 2 * n + 1]
        ids = [pl.program_id(d) for d in range(3)]
        k = ids[2]
        if n:
            start, finish_exchange = _exchange_ops(p_refs, out_refs, *refs[n_in + 2 * n + 2:])

            @pl.when((ids[0] == 0) & (ids[1] == 0) & (k == 0))
            def _():
                start()

        p = _dot(_bf(a_ref[...]), _bf(b_ref[...]))

        def finish(r):
            if add is not None:
                r = r + add_ref[...]
            o_ref[...] = r.astype(out_dtype)

        if nk == 1:
            finish(p)
        else:
            @pl.when(k == 0)
            def _():
                acc[...] = p

            @pl.when((k > 0) & (k < nk - 1))
            def _():
                acc[...] += p

            @pl.when(k == nk - 1)
            def _():
                finish(acc[...] + p)

        if n:
            @pl.when((ids[0] == grid[0] - 1) & (ids[1] == grid[1] - 1) & (k == nk - 1))
            def _():
                finish_exchange()

    in_specs = [pl.BlockSpec((tm, tk), lambda i, j, k: (i, k)), pl.BlockSpec((tk, tn), lambda i, j, k: (k, j))]
    args = [a, b]
    if add is not None:
        in_specs.append(pl.BlockSpec((tm, tn), lambda i, j, k: (i, j)))
        args.append(add)
    res = pl.pallas_call(
        body,
        grid=grid,
        in_specs=in_specs + [ANY] * n,
        out_specs=[pl.BlockSpec((tm, tn), lambda i, j, k: (i, j))] + [ANY] * n,
        out_shape=[jax.ShapeDtypeStruct((M, N), out_dtype)] + [jax.ShapeDtypeStruct(p.shape, p.dtype) for p in exchange],
        scratch_shapes=[pltpu.VMEM((tm, tn) if nk > 1 else (8, 128), F32)] + (_comm_sems(n) if n else []),
        compiler_params=_cparams(("arbitrary",) * 3 if n else ("parallel", "parallel", "arbitrary")),
        name=name,
    )(*args, *exchange)
    return (res[0], res[1:]) if n else res[0]


def _mm_tn(a, b, *, name):
    M, K = a.shape
    N = b.shape[1]
    tn = _pick(N, MM_TILES)
    tm = _token_tile(M, 2 * (K * a.dtype.itemsize + tn * b.dtype.itemsize), 2 * K * tn * 4)

    def body(a_ref, b_ref, o_ref):
        _acc_out(o_ref, _dot_tn(_bf(a_ref[...]), _bf(b_ref[...])), pl.program_id(1) == 0)

    return pl.pallas_call(
        body,
        grid=(N // tn, M // tm),
        in_specs=[pl.BlockSpec((tm, K), lambda j, i: (i, 0)), pl.BlockSpec((tm, tn), lambda j, i: (i, j))],
        out_specs=pl.BlockSpec((K, tn), lambda j, i: (0, j)),
        out_shape=jax.ShapeDtypeStruct((K, N), F32),
        compiler_params=_cparams(("parallel", "arbitrary")),
        name=name,
    )(a, b)


def _rms_fwd(x, w, *, name):
    L, D = x.shape
    tm = min(TOKEN_TILE, L)

    def body(x_ref, w_ref, o_ref):
        o_ref[...] = _rms(x_ref[...], w_ref[...]).astype(BF16)

    return pl.pallas_call(
        body, grid=(L // tm,),
        in_specs=[pl.BlockSpec((tm, D), lambda i: (i, 0)), pl.BlockSpec((1, D), lambda i: (0, 0))],
        out_specs=pl.BlockSpec((tm, D), lambda i: (i, 0)),
        out_shape=jax.ShapeDtypeStruct((L, D), BF16),
        compiler_params=_cparams(("parallel",)), name=name,
    )(x, w)


def _rms_bwd(x, w, dhn, dres, *, name):
    L, D = x.shape
    tm = min(TOKEN_TILE, L)

    def body(x_ref, w_ref, dhn_ref, dres_ref, dx_ref, dw_ref):
        _, vjp = jax.vjp(_rms, x_ref[...], w_ref[...])
        dx, dw = vjp(dhn_ref[...])
        dx_ref[...] = dx + dres_ref[...]
        _acc_out(dw_ref, dw, pl.program_id(0) == 0)

    row = pl.BlockSpec((tm, D), lambda i: (i, 0))
    vec = pl.BlockSpec((1, D), lambda i: (0, 0))
    return pl.pallas_call(
        body, grid=(L // tm,),
        in_specs=[row, vec, row, row],
        out_specs=[row, vec],
        out_shape=[jax.ShapeDtypeStruct((L, D), F32), jax.ShapeDtypeStruct((1, D), F32)],
        compiler_params=_cparams(("arbitrary",)), name=name,
    )(x, w, dhn, dres)


CONV_HALO = 8
CONV_COLS = 1024


CONV_RB = 64
CONV_CB = 256


def _conv_pieces(tm):
    return [(r0, c0) for c0 in range(0, CONV_COLS, CONV_CB) for r0 in range(0, tm, min(CONV_RB, tm))]


def _conv_fwd(proj, w, b):
    L = proj.shape[0]
    tm = min(TOKEN_TILE, L)
    rb = min(CONV_RB, tm)
    c0 = D_INNER // CONV_COLS
    hb = tm // CONV_HALO

    def body(x_ref, h_ref, w_ref, b_ref, o_ref, sp_ref, xx):
        xx[0:CONV_HALO] = jnp.where(pl.program_id(1) == 0, 0.0, h_ref[...])
        xx[CONV_HALO:CONV_HALO + tm] = x_ref[...]
        for r0, cc in _conv_pieces(tm):
            cs = slice(cc, cc + CONV_CB)
            u = b_ref[:, cs]
            for k in range(SSD_CONV):
                off = CONV_HALO - (SSD_CONV - 1) + k + r0
                u = u + w_ref[k:k + 1, cs] * xx[off:off + rb, cs]
            s = 1.0 / (1.0 + jnp.exp(-u))
            o_ref[r0:r0 + rb, cs] = u * s
            sp_ref[r0:r0 + rb, cs] = s * (1.0 + u * (1.0 - s))

    blk = pl.BlockSpec((tm, CONV_COLS), lambda j, i: (i, j))
    shp = jax.ShapeDtypeStruct((L, SSD_CONV_DIM), F32)
    return pl.pallas_call(
        body, grid=(SSD_CONV_DIM // CONV_COLS, L // tm),
        in_specs=[
            pl.BlockSpec((tm, CONV_COLS), lambda j, i: (i, c0 + j)),
            pl.BlockSpec((CONV_HALO, CONV_COLS), lambda j, i: (jnp.maximum(i * hb - 1, 0), c0 + j)),
            pl.BlockSpec((SSD_CONV, CONV_COLS), lambda j, i: (0, j)),
            pl.BlockSpec((1, CONV_COLS), lambda j, i: (0, j)),
        ],
        out_specs=[blk, blk],
        out_shape=[shp, shp],
        scratch_shapes=[pltpu.VMEM((CONV_HALO + tm, CONV_COLS), F32)],
        compiler_params=_cparams(("parallel", "parallel")), name="conv_fwd",
    )(proj, proj, w, b)


def _conv_bwd(proj, w, sp, dxbc, dproj):
    L = proj.shape[0]
    tm = min(TOKEN_TILE, L)
    rb = min(CONV_RB, tm)
    nt = L // tm
    c0 = D_INNER // CONV_COLS
    hb = tm // CONV_HALO

    def fold(a):
        return jnp.sum(a.reshape(rb // 8, 8, CONV_CB), axis=0)

    def body(x_ref, h_ref, w_ref, sp_ref, dy_ref, dp_in_ref, dp_ref, dw_ref, db_ref, xx, dd):
        del dp_in_ref
        i = pl.program_id(1)
        first = i == 0

        @pl.when(first)
        def _():
            dd[tm:tm + CONV_HALO] = jnp.zeros((CONV_HALO, CONV_COLS), F32)

        xx[0:CONV_HALO] = jnp.where(i == nt - 1, 0.0, h_ref[...])
        xx[CONV_HALO:CONV_HALO + tm] = x_ref[...]
        dws, dbs = [], []
        for cc in range(0, CONV_COLS, CONV_CB):
            cs = slice(cc, cc + CONV_CB)
            acc = [jnp.zeros((8, CONV_CB), F32) for _ in range(SSD_CONV + 1)]
            for r0 in range(0, tm, rb):
                du = dy_ref[r0:r0 + rb, cs] * sp_ref[r0:r0 + rb, cs]
                dd[r0:r0 + rb, cs] = du
                for k in range(SSD_CONV):
                    off = CONV_HALO - (SSD_CONV - 1) + k + r0
                    acc[k] = acc[k] + fold(du * xx[off:off + rb, cs])
                acc[SSD_CONV] = acc[SSD_CONV] + fold(du)
            dws.append(jnp.concatenate([jnp.sum(a, axis=0, keepdims=True) for a in acc[:SSD_CONV]], axis=0))
            dbs.append(jnp.sum(acc[SSD_CONV], axis=0, keepdims=True))
        for r0, cc in _conv_pieces(tm):
            cs = slice(cc, cc + CONV_CB)
            dx = jnp.zeros((rb, CONV_CB), F32)
            for k in range(SSD_CONV):
                off = SSD_CONV - 1 - k + r0
                dx = dx + w_ref[k:k + 1, cs] * dd[off:off + rb, cs]
            dp_ref[r0:r0 + rb, cs] = dx.astype(BF16)
        _acc_out(dw_ref, jnp.concatenate(dws, axis=1), first)
        _acc_out(db_ref, jnp.concatenate(dbs, axis=1), first)
        dd[tm:tm + CONV_HALO] = dd[0:CONV_HALO]

    rt = lambda i: nt - 1 - i
    return pl.pallas_call(
        body, grid=(SSD_CONV_DIM // CONV_COLS, nt),
        in_specs=[
            pl.BlockSpec((tm, CONV_COLS), lambda j, i: (rt(i), c0 + j)),
            pl.BlockSpec((CONV_HALO, CONV_COLS), lambda j, i: (jnp.maximum(rt(i) * hb - 1, 0), c0 + j)),
            pl.BlockSpec((SSD_CONV, CONV_COLS), lambda j, i: (0, j)),
            pl.BlockSpec((tm, CONV_COLS), lambda j, i: (rt(i), j)),
            pl.BlockSpec((tm, CONV_COLS), lambda j, i: (rt(i), j)),
            pl.BlockSpec(memory_space=pl.ANY),
        ],
        out_specs=[
            pl.BlockSpec((tm, CONV_COLS), lambda j, i: (rt(i), c0 + j)),
            pl.BlockSpec((SSD_CONV, CONV_COLS), lambda j, i: (0, j)),
            pl.BlockSpec((1, CONV_COLS), lambda j, i: (0, j)),
        ],
        out_shape=[jax.ShapeDtypeStruct((L, SSD_PROJ_PAD), BF16), jax.ShapeDtypeStruct((SSD_CONV, SSD_CONV_DIM), F32),
                   jax.ShapeDtypeStruct((1, SSD_CONV_DIM), F32)],
        scratch_shapes=[pltpu.VMEM((CONV_HALO + tm, CONV_COLS), F32), pltpu.VMEM((tm + CONV_HALO, CONV_COLS), F32)],
        input_output_aliases={5: 0},
        compiler_params=_cparams(("arbitrary", "arbitrary")), name="conv_bwd",
    )(proj, proj, w, sp, dxbc, dproj)


def _loss_bwd(x, tgt, w):
    L, D = x.shape
    tm = min(TOKEN_TILE, L)

    def body(x_ref, t_ref, w_ref, l_ref, dx_ref, dw_ref):
        xv = x_ref[...]
        wv = w_ref[...]
        r = lax.rsqrt(jnp.mean(xv * xv, axis=-1, keepdims=True) + RMS_EPS)
        xh = xv * r
        e = xh * wv - t_ref[...]
        lsum = 0.5 * jnp.sum(jnp.mean(e * e, axis=-1, keepdims=True), axis=0, keepdims=True)
        dout = e * (1.0 / D)
        gx = dout * wv
        dx_ref[...] = r * (gx - xh * jnp.mean(gx * xh, axis=-1, keepdims=True))
        first = pl.program_id(0) == 0
        _acc_out(dw_ref, jnp.sum(dout * xh, axis=0, keepdims=True), first)
        _acc_out(l_ref, jnp.broadcast_to(lsum, (8, 128)), first)

    row = pl.BlockSpec((tm, D), lambda i: (i, 0))
    vec = pl.BlockSpec((1, D), lambda i: (0, 0))
    return pl.pallas_call(
        body, grid=(L // tm,),
        in_specs=[row, row, vec],
        out_specs=[pl.BlockSpec((8, 128), lambda i: (0, 0)), row, vec],
        out_shape=[jax.ShapeDtypeStruct((8, 128), F32), jax.ShapeDtypeStruct((L, D), F32), jax.ShapeDtypeStruct((1, D), F32)],
        compiler_params=_cparams(("arbitrary",)), name="loss_bwd",
    )(x, tgt, w)


MESH = pl.DeviceIdType.MESH
ANY = pl.BlockSpec(memory_space=pl.ANY)


def _comm_sems(n):
    return [pltpu.SemaphoreType.DMA((n, 7)), pltpu.SemaphoreType.DMA((n, 7)), pltpu.SemaphoreType.DMA((n,))]


def _gather_ops(x_refs, out_refs, send_sems, recv_sems, local_sems):
    n = len(x_refs)
    x, y, c = lax.axis_index("x"), lax.axis_index("y"), lax.axis_index("c")
    me, sibling = (x, y, c), (x, y, 1 - c)
    chips = [(1 - x, y), (x, 1 - y), (1 - x, 1 - y)]

    def slot(i, px, py, pc):
        return out_refs[i].at[4 * px + 2 * py + pc]

    def copy(i, k, block, to, src=None):
        return pltpu.make_async_remote_copy(
            src_ref=slot(i, *block) if src is None else src, dst_ref=slot(i, *block),
            send_sem=send_sems.at[i, k], recv_sem=recv_sems.at[i, k], device_id=to, device_id_type=MESH)

    def own():
        mine = [pltpu.make_async_copy(x_refs[i], slot(i, *me), local_sems.at[i]) for i in range(n)]
        first = [copy(i, 0, me, sibling, src=x_refs[i]) for i in range(n)]
        first += [copy(i, 1 + j, me, (*chip, c), src=x_refs[i]) for j, chip in enumerate(chips) for i in range(n)]
        return mine, first

    def start():
        mine, first = own()
        for cp in mine + first:
            cp.start()

    def finish():
        mine, first = own()
        passed = []
        for j, chip in enumerate(chips):
            for i in range(n):
                copy(i, 1 + j, (*chip, c), me).wait_recv()
                passed.append(copy(i, 4 + j, (*chip, c), sibling))
                passed[-1].start()
        for i in range(n):
            copy(i, 0, sibling, me).wait_recv()
        for j, chip in enumerate(chips):
            for i in range(n):
                copy(i, 4 + j, (*chip, 1 - c), me).wait_recv()
        for cp in first + passed:
            cp.wait_send()
        for cp in mine:
            cp.wait()

    return start, finish


def _exchange_ops(p_refs, out_refs, send_sems, recv_sems, local_sems):
    n = len(p_refs)
    x, y, c = lax.axis_index("x"), lax.axis_index("y"), lax.axis_index("c")
    my = 4 * x + 2 * y + c

    def peer(k):
        fx, fy, fc = (k >> 2) & 1, (k >> 1) & 1, k & 1
        px, py, pc = (1 - x if fx else x), (1 - y if fy else y), (1 - c if fc else c)
        return (px, py, pc), 4 * px + 2 * py + pc

    def mine():
        return [pltpu.make_async_copy(p_refs[i].at[my], out_refs[i].at[my], local_sems.at[i]) for i in range(n)]

    def start():
        for cp in mine():
            cp.start()
        for k in range(1, N_DEV):
            to, pid = peer(k)
            for i in range(n):
                pltpu.make_async_remote_copy(
                    src_ref=p_refs[i].at[pid], dst_ref=out_refs[i].at[my], send_sem=send_sems.at[i, k - 1],
                    recv_sem=recv_sems.at[i, k - 1], device_id=to, device_id_type=MESH).start()

    def finish():
        for k in range(1, N_DEV):
            to, pid = peer(k)
            for i in range(n):
                pltpu.make_async_remote_copy(
                    src_ref=p_refs[i].at[pid], dst_ref=out_refs[i].at[pid], send_sem=send_sems.at[i, k - 1],
                    recv_sem=recv_sems.at[i, k - 1], device_id=to, device_id_type=MESH).wait()
        for cp in mine():
            cp.wait()

    return start, finish


def _all_gather(xs, *, name):
    n = len(xs)

    def body(*refs):
        start, finish = _gather_ops(refs[:n], refs[n:2 * n], *refs[2 * n:])
        start()
        finish()

    return pl.pallas_call(
        body,
        out_shape=[jax.ShapeDtypeStruct((N_DEV,) + a.shape, a.dtype) for a in xs],
        in_specs=[ANY] * n, out_specs=[ANY] * n, scratch_shapes=_comm_sems(n), name=name,
    )(*xs)


def _adamw(parts, w, m, v, *, name):
    a, b = w.shape
    tr = _pick(a, (256, 128, 64, 32, 16, 8))

    def body(p_ref, w_ref, m_ref, v_ref, g_ref, d_ref, mo_ref, vo_ref):
        g = p_ref[0]
        for s in range(1, N_DEV):
            g = g + p_ref[s]
        mn = ADAM_B1 * m_ref[...] + (1.0 - ADAM_B1) * g
        vn = ADAM_B2 * v_ref[...] + (1.0 - ADAM_B2) * jnp.square(g)
        m_hat = mn / (1.0 - ADAM_B1 ** ADAM_STEP)
        v_hat = vn / (1.0 - ADAM_B2 ** ADAM_STEP)
        g_ref[...] = g
        d_ref[...] = -ADAM_LR * (m_hat / (jnp.sqrt(v_hat) + ADAM_EPS) + ADAM_WD * w_ref[...])
        mo_ref[...] = mn
        vo_ref[...] = vn

    blk = pl.BlockSpec((tr, b), lambda i: (i, 0))
    shp = jax.ShapeDtypeStruct((a, b), F32)
    return pl.pallas_call(
        body, grid=(a // tr,),
        in_specs=[pl.BlockSpec((N_DEV, tr, b), lambda i: (0, i, 0)), blk, blk, blk],
        out_specs=[blk, blk, blk, blk],
        out_shape=[shp, shp, shp, shp],
        compiler_params=_cparams(("parallel",)), name=name,
    )(parts, w, m, v)


def _col_shards(a, n):
    return a.reshape(a.shape[0], N_DEV, n).transpose(1, 0, 2)


def _from_col_shards(g, cols):
    r = g.shape[1]
    full = g.transpose(1, 0, 2).reshape(r, -1)
    return jnp.pad(full, ((0, 0), (0, cols - full.shape[1])))


def kernel(x, norm_w, gla_in_proj, gla_gate_up, gla_gate_bias, gla_head_norm, gla_out_proj, ssd_in_proj, ssd_conv_w, ssd_conv_b, ssd_dt_bias, ssd_a_log, ssd_d, ssd_gate_norm, ssd_out_proj, final_norm, loss_target, m_norm_w, m_gla_in_proj, m_gla_gate_up, m_gla_gate_bias, m_gla_head_norm, m_gla_out_proj, m_ssd_in_proj, m_ssd_conv_w, m_ssd_conv_b, m_ssd_dt_bias, m_ssd_a_log, m_ssd_d, m_ssd_gate_norm, m_ssd_out_proj, m_final_norm, v_norm_w, v_gla_in_proj, v_gla_gate_up, v_gla_gate_bias, v_gla_head_norm, v_gla_out_proj, v_ssd_in_proj, v_ssd_conv_w, v_ssd_conv_b, v_ssd_dt_bias, v_ssd_a_log, v_ssd_d, v_ssd_gate_norm, v_ssd_out_proj, v_final_norm):
    x0 = x[0]
    tgt = loss_target[0]
    n_gin = GLA_PROJ // N_DEV
    n_sin = SSD_PROJ // N_DEV
    n_up = GLA_DK // N_DEV
    n_cv = SSD_CONV_DIM // N_DEV

    g_gin, g_up, g_gout = _all_gather(
        [gla_in_proj[0].astype(BF16), gla_gate_up[0].astype(BF16), gla_out_proj[0].astype(BF16)], name="gather_weights")
    w_gin = _from_col_shards(g_gin, GLA_PROJ_PAD)
    wup = jnp.pad(_from_col_shards(g_up, GLA_DK), ((0, 128 - GLA_RANK), (0, 0))).astype(F32)
    w_gout = g_gout.reshape(D_INNER, D_MODEL)
    vec128 = lambda a: jnp.pad(a.reshape(1, -1), ((0, 0), (0, 128 - a.size)))
    dtb, alog, dsk = vec128(ssd_dt_bias), vec128(ssd_a_log), vec128(ssd_d)
    nw0, nw1 = norm_w[0:1], norm_w[1:2]

    hn1 = _rms_fwd(x0, nw0, name="rms1_fwd")
    proj1 = _mm(hn1, w_gin, name="gla_in_proj")
    (o, og, s_saved), (g_sin, g_sout, g_cw, g_cb, g_gn) = _gla_layer_fwd(
        proj1, wup, gla_gate_bias, gla_head_norm,
        [ssd_in_proj[0].astype(BF16), ssd_out_proj[0].astype(BF16), ssd_conv_w[0], ssd_conv_b, ssd_gate_norm])
    w_sin = _from_col_shards(g_sin, SSD_PROJ_PAD)
    w_sout = g_sout.reshape(D_INNER, D_MODEL)
    conv_w = _from_col_shards(g_cw, SSD_CONV_DIM)
    conv_b = g_cb.reshape(1, SSD_CONV_DIM)
    gate_norm = g_gn.reshape(1, D_INNER)
    x1 = _mm(og, w_gout, add=x0, name="gla_out_proj")
    hn2 = _rms_fwd(x1, nw1, name="rms2_fwd")
    proj2 = _mm(hn2, w_sin, name="ssd_in_proj")
    xbc, conv_sp = _conv_fwd(proj2, conv_w, conv_b)
    yn, h_saved = _ssd_layer_fwd(xbc, proj2, dtb, alog, dsk, gate_norm)
    x2 = _mm(yn, w_sout, add=x1, name="ssd_out_proj")
    lsum, dx2, d_final = _loss_bwd(x2, tgt, final_norm.reshape(1, D_MODEL))
    loss = lax.psum(lsum[0, 0], ("x", "y", "c"))

    d_sout = _mm_tn(yn, dx2, name="ssd_out_proj_dw")
    dyn = _mm(dx2, w_sout.T, name="ssd_out_proj_dx")
    dproj2, dxbc, d_dtb, d_alog, d_dsk, d_gate_norm = _ssd_layer_bwd(xbc, proj2, dtb, alog, dsk, gate_norm, h_saved, dyn)
    dproj2, d_conv_w, d_conv_b = _conv_bwd(proj2, conv_w, conv_sp, dxbc, dproj2)
    d_sin = _mm_tn(hn2, dproj2, name="ssd_in_proj_dw")
    dhn2 = _mm(dproj2, w_sin.T, name="ssd_in_proj_dx")
    dx1, d_nw1 = _rms_bwd(x1, nw1, dhn2, dx2, name="rms2_bwd")
    d_gout = _mm_tn(og, dx1, name="gla_out_proj_dw")
    dog = _mm(dx1, w_gout.T, name="gla_out_proj_dx")
    early = {
        "gla_out_proj": ((gla_out_proj[0], m_gla_out_proj[0], v_gla_out_proj[0]), d_gout.reshape(N_DEV, -1, D_MODEL)),
        "ssd_in_proj": ((ssd_in_proj[0], m_ssd_in_proj[0], v_ssd_in_proj[0]), _col_shards(d_sin[:, :SSD_PROJ], n_sin)),
        "ssd_conv_w": ((ssd_conv_w[0], m_ssd_conv_w[0], v_ssd_conv_w[0]), _col_shards(d_conv_w, n_cv)),
        "ssd_conv_b": ((ssd_conv_b, m_ssd_conv_b, v_ssd_conv_b), d_conv_b.reshape(N_DEV, 1, n_cv)),
        "ssd_gate_norm": ((ssd_gate_norm, m_ssd_gate_norm, v_ssd_gate_norm), d_gate_norm.reshape(N_DEV, 1, -1)),
        "ssd_out_proj": ((ssd_out_proj[0], m_ssd_out_proj[0], v_ssd_out_proj[0]), d_sout.reshape(N_DEV, -1, D_MODEL)),
    }
    (dproj1, d_wup, d_gbias, d_head_norm), early_recv = _gla_layer_bwd(
        proj1, wup, gla_gate_bias, gla_head_norm, o, s_saved, dog, [p for _, p in early.values()])
    d_gin = _mm_tn(hn1, dproj1, name="gla_in_proj_dw")
    late = {
        "gla_in_proj": ((gla_in_proj[0], m_gla_in_proj[0], v_gla_in_proj[0]), _col_shards(d_gin[:, :GLA_PROJ], n_gin)),
        "gla_gate_up": ((gla_gate_up[0], m_gla_gate_up[0], v_gla_gate_up[0]), _col_shards(d_wup[:GLA_RANK], n_up)),
    }
    dhn1, late_recv = _mm(dproj1, w_gin.T, name="gla_in_proj_dx", exchange=[p for _, p in late.values()])
    dx0, d_nw0 = _rms_bwd(x0, nw0, dhn1, dx1, name="rms1_bwd")
    heads = SSD_HEADS
    replicated = {
        "norm_w": ((norm_w, m_norm_w, v_norm_w), jnp.concatenate([d_nw0, d_nw1], axis=0)),
        "gla_gate_bias": ((gla_gate_bias, m_gla_gate_bias, v_gla_gate_bias), d_gbias),
        "gla_head_norm": ((gla_head_norm, m_gla_head_norm, v_gla_head_norm), d_head_norm),
        "ssd_dt_bias": ((ssd_dt_bias, m_ssd_dt_bias, v_ssd_dt_bias), d_dtb[:, :heads]),
        "ssd_a_log": ((ssd_a_log, m_ssd_a_log, v_ssd_a_log), d_alog[:, :heads]),
        "ssd_d": ((ssd_d, m_ssd_d, v_ssd_d), d_dsk[:, :heads]),
        "final_norm": (tuple(t.reshape(1, D_MODEL) for t in (final_norm, m_final_norm, v_final_norm)), d_final),
    }
    results = {}
    for group, recv in ((early, early_recv),
                        (late, late_recv),
                        (replicated, _all_gather([p for _, p in replicated.values()], name="replicated_gather"))):
        for (nm, ((w, m, v), _)), r in zip(group.items(), recv):
            results[nm] = _adamw(r, w, m, v, name=nm + "_adamw")

    order = [("norm_w", norm_w), ("gla_in_proj", gla_in_proj), ("gla_gate_up", gla_gate_up), ("gla_gate_bias", gla_gate_bias),
             ("gla_head_norm", gla_head_norm), ("gla_out_proj", gla_out_proj), ("ssd_in_proj", ssd_in_proj),
             ("ssd_conv_w", ssd_conv_w), ("ssd_conv_b", ssd_conv_b), ("ssd_dt_bias", ssd_dt_bias), ("ssd_a_log", ssd_a_log),
             ("ssd_d", ssd_d), ("ssd_gate_norm", ssd_gate_norm), ("ssd_out_proj", ssd_out_proj), ("final_norm", final_norm)]
    out = [loss, dx0[None]]
    for i in range(4):
        out += [results[nm][i].reshape(ref.shape) for nm, ref in order]
    return tuple(out)
```

```python
import jax
import jax.numpy as jnp
from jax import lax
from jax.experimental import pallas as pl
from jax.experimental.pallas import tpu as pltpu

F32 = jnp.float32
BF16 = jnp.bfloat16

D_MODEL = 1024
D_INNER = 2048
RMS_EPS = 1e-6
GLA_HEADS = 4
GLA_DK = 512
GLA_HEAD_K = 128
GLA_HEAD_V = 512
GLA_RANK = 16
GLA_NORMALIZER = 16.0
CHUNK = 64
SUB = 16
GLA_PROJ = 5136
GLA_PROJ_PAD = 5376
GLA_GK_COL = 5120
SSD_HEADS = 32
SSD_GROUPS = 8
SSD_HPG = 4
SSD_P = 64
SSD_N = 128
SSD_CONV = 4
SSD_CONV_DIM = 4096
SSD_PROJ = 6176
SSD_PROJ_PAD = 6400
SSD_DT_COL = 6144
N_DEV = 8

ADAM_LR = 0.001
ADAM_B1 = 0.9
ADAM_B2 = 0.999
ADAM_EPS = 1e-08
ADAM_WD = 0.01
ADAM_STEP = 10

VMEM_LIMIT = 56 * 1024 * 1024
TOKEN_TILE = 512
MM_TOKEN_TILE = 2048
MM_VMEM_BUDGET = 44 * 1024 * 1024
MM_TILES = (1792, 1280, 1024, 768, 512, 256, 128)


def _dot(a, b):
    return jnp.dot(a, b, preferred_element_type=F32)


def _dot_nt(a, b):
    return lax.dot_general(a, b, (((1,), (1,)), ((), ())), preferred_element_type=F32)


def _dot_tn(a, b):
    return lax.dot_general(a, b, (((0,), (0,)), ((), ())), preferred_element_type=F32)


def _bf(a):
    return a.astype(BF16)


@jax.custom_vjp
def _mxu(a, b):
    return _dot(_bf(a), _bf(b))


def _mxu_fwd(a, b):
    return _mxu(a, b), (a, b)


def _mxu_bwd(res, g):
    a, b = res
    return _dot_nt(_bf(g), _bf(b)), _dot_tn(_bf(a), _bf(g))


_mxu.defvjp(_mxu_fwd, _mxu_bwd)


@jax.custom_vjp
def _mxu_nt(a, b):
    return _dot_nt(_bf(a), _bf(b))


def _mxu_nt_fwd(a, b):
    return _mxu_nt(a, b), (a, b)


def _mxu_nt_bwd(res, g):
    a, b = res
    return _dot(_bf(g), _bf(b)), _dot_tn(_bf(g), _bf(a))


_mxu_nt.defvjp(_mxu_nt_fwd, _mxu_nt_bwd)


@jax.custom_vjp
def _mxu_tn(a, b):
    return _dot_tn(_bf(a), _bf(b))


def _mxu_tn_fwd(a, b):
    return _mxu_tn(a, b), (a, b)


def _mxu_tn_bwd(res, g):
    a, b = res
    return _dot_nt(_bf(b), _bf(g)), _dot(_bf(a), _bf(g))


_mxu_tn.defvjp(_mxu_tn_fwd, _mxu_tn_bwd)


def _split2(a):
    hi = _bf(a)
    return hi, _bf(a - hi.astype(F32))


def _three_pass(dot, a, b):
    ah, al = _split2(a)
    bh, bl = _split2(b)
    return dot(ah, bh) + (dot(ah, bl) + dot(al, bh))


@jax.custom_vjp
def _dot3_nt(a, b):
    return _three_pass(_dot_nt, a, b)


def _dot3_nt_fwd(a, b):
    return _dot3_nt(a, b), (a, b)


def _dot3_nt_bwd(res, g):
    a, b = res
    return _three_pass(_dot, g, b), _three_pass(_dot_tn, g, a)


_dot3_nt.defvjp(_dot3_nt_fwd, _dot3_nt_bwd)


def _silu(x):
    return x / (1.0 + jnp.exp(-x))


def _log_sigmoid(z):
    return jnp.minimum(z, 0.0) - jnp.log(1.0 + jnp.exp(-jnp.abs(z)))


def _softplus(z):
    return jnp.maximum(z, 0.0) + jnp.log(1.0 + jnp.exp(-jnp.abs(z)))


def _iota(shape, dim):
    return lax.broadcasted_iota(jnp.int32, shape, dim)


def _rms(x, w):
    return x * lax.rsqrt(jnp.mean(x * x, axis=-1, keepdims=True) + RMS_EPS) * w


def _scan_rows(a, reverse, seg):
    n = a.shape[0]
    pos = _iota(a.shape, 0) & (seg - 1)
    sh = 1
    while sh < seg:
        if reverse:
            a = a + jnp.where(pos < seg - sh, pltpu.roll(a, n - sh, 0), 0.0)
        else:
            a = a + jnp.where(pos >= sh, pltpu.roll(a, sh, 0), 0.0)
        sh *= 2
    return a


def _make_cumsum(seg):
    @jax.custom_vjp
    def cumsum(a):
        return _scan_rows(a, False, seg)

    cumsum.defvjp(lambda a: (_scan_rows(a, False, seg), None), lambda _, g: (_scan_rows(g, True, seg),))
    return cumsum


_cumsum_sub = _make_cumsum(SUB)
_cumsum_rows = _make_cumsum(CHUNK)


def _cparams(sem):
    return pltpu.CompilerParams(dimension_semantics=sem, vmem_limit_bytes=VMEM_LIMIT)


def _acc_out(ref, val, first):
    @pl.when(first)
    def _():
        ref[...] = val

    @pl.when(jnp.logical_not(first))
    def _():
        ref[...] += val


def _gla_chunk(q, k, va, vb, gk, wup, bias, sts):
    nb = CHUNK // SUB
    heads = range(GLA_HEADS)
    hc = lambda a, h: a[:, h * GLA_HEAD_K:(h + 1) * GLA_HEAD_K]
    v = [(va if h < 2 else vb)[:, (h % 2) * GLA_HEAD_V:(h % 2 + 1) * GLA_HEAD_V] for h in heads]
    z = _mxu(gk, wup) + bias
    la = _log_sigmoid(z) * (1.0 / GLA_NORMALIZER)
    qs = q * (GLA_HEAD_K ** -0.5)
    bl = _cumsum_sub(la)
    tot = [jnp.sum(la[i * SUB:(i + 1) * SUB], axis=0, keepdims=True) for i in range(nb)]
    pre = [jnp.zeros((1, GLA_DK), F32)]
    for i in range(nb):
        pre.append(pre[i] + tot[i])
    b_last = pre[nb]
    rows_of = lambda vals: jnp.concatenate([jnp.broadcast_to(t, (SUB, GLA_DK)) for t in vals], axis=0)
    suf = rows_of(tot) - bl
    nxt = rows_of(pre[1:])
    q_in = qs * jnp.exp(bl + rows_of(pre[:nb]))
    k_end = k * jnp.exp(suf + (b_last - nxt))
    dec = jnp.exp(b_last)
    qa = qs * jnp.exp(bl)
    o_inter = [_mxu_nt(hc(q_in, h), sts[h]) for h in heads]
    sts_new = tuple(sts[h] * hc(dec, h) + _mxu_tn(v[h], hc(k_end, h)) for h in heads)
    half = SUB // 2
    rs = _iota((SUB, GLA_HEAD_K), 0)
    cs = _iota((half, CHUNK), 1)
    a_rows = [[] for _ in heads]
    for i in range(nb):
        sl = slice(i * SUB, (i + 1) * SUB)
        n = i * SUB
        if i > 0:
            kp = jnp.concatenate([k[:n] * jnp.exp(suf[:n] + (pre[i] - nxt[:n])), jnp.zeros((CHUNK - n, GLA_DK), F32)], axis=0)
        for h in heads:
            q_i, k_i, bl_i = hc(qs[sl], h), hc(k[sl], h), hc(bl[sl], h)
            if i > 0:
                a_i = _dot3_nt(hc(qa[sl], h), hc(kp, h))
                a_top, a_bot = a_i[:half], a_i[half:]
            else:
                a_top = a_bot = jnp.zeros((half, CHUNK), F32)
            for j in range(SUB):
                lo = 0 if j < half else half
                e = jnp.exp(jnp.minimum(bl_i[lo:] - bl_i[j:j + 1], 0.0))
                t = jnp.where(rs[lo:] >= j, q_i[lo:] * e * k_i[j:j + 1], 0.0)
                rsum = jnp.sum(t, axis=-1, keepdims=True)
                hit = cs == i * SUB + j
                if lo == 0:
                    a_top = a_top + jnp.where(hit, rsum[:half], 0.0)
                a_bot = a_bot + jnp.where(hit, rsum[half - lo:], 0.0)
            a_rows[h] += [a_top, a_bot]
    o = [o_inter[h] + _mxu(jnp.concatenate(a_rows[h], axis=0), v[h]) for h in heads]
    return jnp.concatenate(o, axis=1), sts_new


def _gla_post(o, g, wn):
    return _rms(o, wn) * _silu(g)


GLA_HALF = 2 * GLA_HEAD_V


def _gla_specs(nc, rev):
    ci = (lambda c: nc - 1 - c) if rev else (lambda c: c)
    v0 = 2 * GLA_DK // GLA_HALF
    g0 = (2 * GLA_DK + D_INNER) // GLA_HALF
    return [
        pl.BlockSpec((CHUNK, GLA_DK), lambda c: (ci(c), 0)),
        pl.BlockSpec((CHUNK, GLA_DK), lambda c: (ci(c), 1)),
        pl.BlockSpec((CHUNK, GLA_HALF), lambda c: (ci(c), v0)),
        pl.BlockSpec((CHUNK, GLA_HALF), lambda c: (ci(c), v0 + 1)),
        pl.BlockSpec((CHUNK, GLA_HALF), lambda c: (ci(c), g0)),
        pl.BlockSpec((CHUNK, GLA_HALF), lambda c: (ci(c), g0 + 1)),
        pl.BlockSpec((CHUNK, 128), lambda c: (ci(c), GLA_GK_COL // 128)),
        pl.BlockSpec((128, GLA_DK), lambda c: (0, 0)),
        pl.BlockSpec((1, GLA_DK), lambda c: (0, 0)),
        pl.BlockSpec((1, GLA_HEAD_V), lambda c: (0, 0)),
    ]


def _head_cols(ref_a, ref_b, h):
    ref = ref_a if h < 2 else ref_b
    return ref[:, (h % 2) * GLA_HEAD_V:(h % 2 + 1) * GLA_HEAD_V]


def _gla_layer_fwd(proj, wup, bias, wn, gather):
    L = proj.shape[0]
    nc = L // CHUNK
    n = len(gather)

    def body(*refs):
        q_ref, k_ref, va_ref, vb_ref, ga_ref, gb_ref, gk_ref, wup_ref, b_ref, wn_ref = refs[:10]
        x_refs = refs[10:10 + n]
        o_ref, og_ref, s_ref = refs[10 + n:13 + n]
        out_refs = refs[13 + n:13 + 2 * n]
        st, send_sems, recv_sems, local_sems = refs[13 + 2 * n:]
        start, finish = _gather_ops(x_refs, out_refs, send_sems, recv_sems, local_sems)

        @pl.when(pl.program_id(0) == 0)
        def _():
            st[...] = jnp.zeros(st.shape, F32)
            start()

        s_in = tuple(st[h] for h in range(GLA_HEADS))
        o, s_new = _gla_chunk(q_ref[...], k_ref[...], va_ref[...], vb_ref[...], gk_ref[...], wup_ref[...], b_ref[...], s_in)
        o_ref[...] = o
        for h in range(GLA_HEADS):
            vc = slice(h * GLA_HEAD_V, (h + 1) * GLA_HEAD_V)
            s_ref[h] = s_in[h]
            st[h] = s_new[h]
            og_ref[:, vc] = _gla_post(o[:, vc], _head_cols(ga_ref, gb_ref, h), wn_ref[...]).astype(BF16)

        @pl.when(pl.program_id(0) == nc - 1)
        def _():
            finish()

    res = pl.pallas_call(
        body,
        grid=(nc,),
        in_specs=_gla_specs(nc, False) + [ANY] * n,
        out_specs=[
            pl.BlockSpec((CHUNK, D_INNER), lambda c: (c, 0)),
            pl.BlockSpec((CHUNK, D_INNER), lambda c: (c, 0)),
            pl.BlockSpec((None, GLA_HEADS, GLA_HEAD_V, GLA_HEAD_K), lambda c: (c, 0, 0, 0)),
        ] + [ANY] * n,
        out_shape=[
            jax.ShapeDtypeStruct((L, D_INNER), F32),
            jax.ShapeDtypeStruct((L, D_INNER), BF16),
            jax.ShapeDtypeStruct((nc, GLA_HEADS, GLA_HEAD_V, GLA_HEAD_K), F32),
        ] + [jax.ShapeDtypeStruct((N_DEV,) + a.shape, a.dtype) for a in gather],
        scratch_shapes=[pltpu.VMEM((GLA_HEADS, GLA_HEAD_V, GLA_HEAD_K), F32)] + _comm_sems(n),
        compiler_params=_cparams(("arbitrary",)),
        name="gla_layer_fwd",
    )(proj, proj, proj, proj, proj, proj, proj, wup, bias, wn, *gather)
    return res[:3], res[3:]


def _gla_layer_bwd(proj, wup, bias, wn, o, s_in, dog, exchange):
    L = proj.shape[0]
    nc = L // CHUNK
    n = len(exchange)

    def body(*refs):
        (q_ref, k_ref, va_ref, vb_ref, ga_ref, gb_ref, gk_ref, wup_ref, b_ref, wn_ref, o_ref, s_ref, dog_ref) = refs[:13]
        p_refs = refs[13:13 + n]
        dp_ref, dwup_ref, db_ref, dwn_ref = refs[13 + n:17 + n]
        out_refs = refs[17 + n:17 + 2 * n]
        dst, send_sems, recv_sems, local_sems = refs[17 + 2 * n:]
        start, finish = _exchange_ops(p_refs, out_refs, send_sems, recv_sems, local_sems)

        @pl.when(pl.program_id(0) == 0)
        def _():
            dst[...] = jnp.zeros(dst.shape, F32)
            dwup_ref[...] = jnp.zeros(dwup_ref.shape, F32)
            db_ref[...] = jnp.zeros(db_ref.shape, F32)
            dwn_ref[...] = jnp.zeros(dwn_ref.shape, F32)
            start()

        dos = []
        for h in range(GLA_HEADS):
            vc = slice(h * GLA_HEAD_V, (h + 1) * GLA_HEAD_V)
            _, post_vjp = jax.vjp(_gla_post, o_ref[:, vc], _head_cols(ga_ref, gb_ref, h), wn_ref[...])
            do, dg, dwn = post_vjp(dog_ref[:, vc])
            dos.append(do)
            dp_ref[:, 2 * GLA_DK + D_INNER + h * GLA_HEAD_V:2 * GLA_DK + D_INNER + (h + 1) * GLA_HEAD_V] = dg.astype(BF16)
            dwn_ref[...] += dwn
        _, vjp = jax.vjp(_gla_chunk, q_ref[...], k_ref[...], va_ref[...], vb_ref[...], gk_ref[...], wup_ref[...], b_ref[...],
                         tuple(s_ref[h] for h in range(GLA_HEADS)))
        dq, dk, dva, dvb, dgk, dwup, db, ds = vjp((jnp.concatenate(dos, axis=1), tuple(dst[h] for h in range(GLA_HEADS))))
        for h in range(GLA_HEADS):
            dst[h] = ds[h]
        dp_ref[:, :GLA_DK] = dq.astype(BF16)
        dp_ref[:, GLA_DK:2 * GLA_DK] = dk.astype(BF16)
        dp_ref[:, 2 * GLA_DK:2 * GLA_DK + GLA_HALF] = dva.astype(BF16)
        dp_ref[:, 2 * GLA_DK + GLA_HALF:2 * GLA_DK + D_INNER] = dvb.astype(BF16)
        dwup_ref[...] += dwup
        db_ref[...] += db
        dp_ref[:, GLA_GK_COL:GLA_GK_COL + 128] = dgk.astype(BF16)
        dp_ref[:, GLA_GK_COL + 128:] = jnp.zeros((CHUNK, GLA_PROJ_PAD - GLA_GK_COL - 128), BF16)

        @pl.when(pl.program_id(0) == nc - 1)
        def _():
            finish()

    rc = lambda c: nc - 1 - c
    res = pl.pallas_call(
        body,
        grid=(nc,),
        in_specs=_gla_specs(nc, True) + [
            pl.BlockSpec((CHUNK, D_INNER), lambda c: (rc(c), 0)),
            pl.BlockSpec((None, GLA_HEADS, GLA_HEAD_V, GLA_HEAD_K), lambda c: (rc(c), 0, 0, 0)),
            pl.BlockSpec((CHUNK, D_INNER), lambda c: (rc(c), 0)),
        ] + [ANY] * n,
        out_specs=[
            pl.BlockSpec((CHUNK, GLA_PROJ_PAD), lambda c: (rc(c), 0)),
            pl.BlockSpec((128, GLA_DK), lambda c: (0, 0)),
            pl.BlockSpec((1, GLA_DK), lambda c: (0, 0)),
            pl.BlockSpec((1, GLA_HEAD_V), lambda c: (0, 0)),
        ] + [ANY] * n,
        out_shape=[
            jax.ShapeDtypeStruct((L, GLA_PROJ_PAD), BF16),
            jax.ShapeDtypeStruct((128, GLA_DK), F32),
            jax.ShapeDtypeStruct((1, GLA_DK), F32),
            jax.ShapeDtypeStruct((1, GLA_HEAD_V), F32),
        ] + [jax.ShapeDtypeStruct(a.shape, a.dtype) for a in exchange],
        scratch_shapes=[pltpu.VMEM((GLA_HEADS, GLA_HEAD_V, GLA_HEAD_K), F32)] + _comm_sems(n),
        compiler_params=_cparams(("arbitrary",)),
        name="gla_layer_bwd",
    )(proj, proj, proj, proj, proj, proj, proj, wup, bias, wn, o, s_in, dog, *exchange)
    return res[:4], res[4:]


@jax.custom_vjp
def _expand(v):
    r = v.shape[0]
    left = _iota((r, 128), 1) < SSD_P
    slabs = []
    for p in range(SSD_HEADS // 2):
        a = jnp.broadcast_to(v[:, 2 * p:2 * p + 1], (r, 128))
        b = jnp.broadcast_to(v[:, 2 * p + 1:2 * p + 2], (r, 128))
        slabs.append(jnp.where(left, a, b))
    return jnp.concatenate(slabs, axis=1)


def _expand_fwd(v):
    return _expand(v), None


def _expand_bwd(_, g):
    r = g.shape[0]
    lane = _iota((r, 128), 1)
    left = lane < SSD_P
    dv = jnp.zeros((r, 128), F32)
    for p in range(SSD_HEADS // 2):
        gs = g[:, 128 * p:128 * (p + 1)]
        sa = jnp.sum(jnp.where(left, gs, 0.0), axis=-1, keepdims=True)
        sb = jnp.sum(jnp.where(left, 0.0, gs), axis=-1, keepdims=True)
        dv = dv + jnp.where(lane == 2 * p, sa, 0.0) + jnp.where(lane == 2 * p + 1, sb, 0.0)
    return (dv,)


_expand.defvjp(_expand_fwd, _expand_bwd)

SSD_GW = SSD_HPG * SSD_P
SSD_BC = SSD_GROUPS * SSD_N
SSD_PHASE = 4


def _ssd_chunk(xs, Bm, Cm, dtp, dtb, alog, dsk, h_in):
    dt = _softplus(dtp + dtb)
    acum = _cumsum_rows(dt * (-jnp.exp(alog)))
    a_last = acum[CHUNK - 1:CHUNK]
    acum_b = _expand(acum)
    w_end = _expand(dt * jnp.exp(a_last - acum))
    d_b = _expand(jnp.broadcast_to(dsk, (8, 128)))[0:1]
    ac_t = jnp.concatenate([acum, acum], axis=0).T
    dt_t = jnp.concatenate([dt, dt], axis=0).T
    lane = _iota((CHUNK, 128), 1)
    left = lane < SSD_P
    causal = (lane & (SSD_P - 1)) <= _iota((CHUNK, 128), 0)
    cd = jnp.exp(ac_t[:, CHUNK - 1:CHUNK])
    ys, h_out = [], []
    for g0 in range(0, SSD_GROUPS, SSD_PHASE):
        cb2, y_off = {}, {}
        for g in range(g0, g0 + SSD_PHASE):
            Bg = Bm[:, g * SSD_N:(g + 1) * SSD_N]
            Cg = Cm[:, g * SSD_N:(g + 1) * SSD_N]
            gs = slice(g * SSD_GW, (g + 1) * SSD_GW)
            cb2[g] = _mxu_nt(Cg, jnp.concatenate([Bg, Bg], axis=0))
            y_off[g] = _mxu_nt(Cg, h_in[gs])
            st = _mxu_tn(xs[:, gs] * w_end[:, gs], Bg)
            hs = [h_in[h * SSD_P:(h + 1) * SSD_P] * cd[h:h + 1] for h in range(g * SSD_HPG, (g + 1) * SSD_HPG)]
            h_out.append(jnp.concatenate(hs, axis=0) + st)
        for p in range(g0 * (SSD_HPG // 2), (g0 + SSD_PHASE) * (SSD_HPG // 2)):
            g, k = divmod(p, SSD_HPG // 2)
            sl = slice(128 * p, 128 * (p + 1))
            ac_c = acum_b[:, sl]
            ac_r = jnp.where(left, ac_t[2 * p:2 * p + 1], ac_t[2 * p + 1:2 * p + 2])
            dt_r = jnp.where(left, dt_t[2 * p:2 * p + 1], dt_t[2 * p + 1:2 * p + 2])
            m2 = cb2[g] * jnp.where(causal, jnp.exp(jnp.minimum(ac_c - ac_r, 0.0)), 0.0) * dt_r
            xsl = xs[:, sl]
            x2 = jnp.concatenate([jnp.where(left, xsl, 0.0), jnp.where(left, 0.0, xsl)], axis=0)
            ys.append(_mxu(m2, x2) + y_off[g][:, 128 * k:128 * (k + 1)] * jnp.exp(ac_c) + xsl * d_b[:, sl])
    return jnp.concatenate(ys, axis=1), jnp.concatenate(h_out, axis=0)


def _ssd_step(xs, Bm, Cm, dtp, dtb, alog, dsk, h_in, z, wn):
    y, h_out = _ssd_chunk(xs, Bm, Cm, dtp, dtb, alog, dsk, h_in)
    return _rms(y * _silu(z), wn), h_out


def _ssd_specs(nc, rev):
    ci = (lambda c: nc - 1 - c) if rev else (lambda c: c)
    vec = pl.BlockSpec((1, 128), lambda c: (0, 0))
    return [
        pl.BlockSpec((CHUNK, D_INNER), lambda c: (ci(c), 0)),
        pl.BlockSpec((CHUNK, SSD_BC), lambda c: (ci(c), D_INNER // SSD_BC)),
        pl.BlockSpec((CHUNK, SSD_BC), lambda c: (ci(c), D_INNER // SSD_BC + 1)),
        pl.BlockSpec((CHUNK, 128), lambda c: (ci(c), SSD_DT_COL // 128)),
        vec, vec, vec,
        pl.BlockSpec((CHUNK, D_INNER), lambda c: (ci(c), 0)),
        pl.BlockSpec((1, D_INNER), lambda c: (0, 0)),
    ]


def _ssd_layer_fwd(xbc, proj, dtb, alog, dsk, wn):
    L = xbc.shape[0]
    nc = L // CHUNK

    def body(xs_ref, b_ref, c_ref, dt_ref, dtb_ref, alog_ref, dsk_ref, z_ref, wn_ref, y_ref, hs_ref, hst):
        @pl.when(pl.program_id(0) == 0)
        def _():
            hst[...] = jnp.zeros(hst.shape, F32)

        h_in = hst[...]
        hs_ref[...] = h_in
        yn, h_out = _ssd_step(xs_ref[...], b_ref[...], c_ref[...], dt_ref[...], dtb_ref[...], alog_ref[...], dsk_ref[...],
                              h_in, z_ref[...], wn_ref[...])
        y_ref[...] = yn.astype(BF16)
        hst[...] = h_out

    return pl.pallas_call(
        body,
        grid=(nc,),
        in_specs=_ssd_specs(nc, False),
        out_specs=[
            pl.BlockSpec((CHUNK, D_INNER), lambda c: (c, 0)),
            pl.BlockSpec((None, D_INNER, SSD_N), lambda c: (c, 0, 0)),
        ],
        out_shape=[
            jax.ShapeDtypeStruct((L, D_INNER), BF16),
            jax.ShapeDtypeStruct((nc, D_INNER, SSD_N), F32),
        ],
        scratch_shapes=[pltpu.VMEM((D_INNER, SSD_N), F32)],
        compiler_params=_cparams(("arbitrary",)),
        name="ssd_layer_fwd",
    )(xbc, xbc, xbc, proj, dtb, alog, dsk, proj, wn)


def _ssd_layer_bwd(xbc, proj, dtb, alog, dsk, wn, h_saved, dyn):
    L = xbc.shape[0]
    nc = L // CHUNK

    def body(xs_ref, b_ref, c_ref, dt_ref, dtb_ref, alog_ref, dsk_ref, z_ref, wn_ref, hs_ref, dyn_ref,
             dp_ref, dx_ref, ddtb_ref, dalog_ref, ddsk_ref, dwn_ref, dhst):
        @pl.when(pl.program_id(0) == 0)
        def _():
            dhst[...] = jnp.zeros(dhst.shape, F32)
            ddtb_ref[...] = jnp.zeros((1, 128), F32)
            dalog_ref[...] = jnp.zeros((1, 128), F32)
            ddsk_ref[...] = jnp.zeros((1, 128), F32)
            dwn_ref[...] = jnp.zeros((1, D_INNER), F32)

        _, vjp = jax.vjp(_ssd_step, xs_ref[...], b_ref[...], c_ref[...], dt_ref[...], dtb_ref[...], alog_ref[...],
                         dsk_ref[...], hs_ref[...], z_ref[...], wn_ref[...])
        dxs, db, dc, ddt, ddtb, dalog, ddsk, dh, dz, dwn = vjp((dyn_ref[...], dhst[...]))
        dx_ref[:, :D_INNER] = dxs
        dx_ref[:, D_INNER:D_INNER + SSD_BC] = db
        dx_ref[:, D_INNER + SSD_BC:] = dc
        dp_ref[:, :D_INNER] = dz.astype(BF16)
        dp_ref[:, D_INNER:SSD_DT_COL] = jnp.zeros((CHUNK, SSD_CONV_DIM), BF16)
        dp_ref[:, SSD_DT_COL:SSD_DT_COL + 128] = ddt.astype(BF16)
        dp_ref[:, SSD_DT_COL + 128:] = jnp.zeros((CHUNK, SSD_PROJ_PAD - SSD_DT_COL - 128), BF16)
        dhst[...] = dh
        ddtb_ref[...] += ddtb
        dalog_ref[...] += dalog
        ddsk_ref[...] += ddsk
        dwn_ref[...] += dwn

    rc = lambda c: nc - 1 - c
    vec = pl.BlockSpec((1, 128), lambda c: (0, 0))
    vshape = jax.ShapeDtypeStruct((1, 128), F32)
    return pl.pallas_call(
        body,
        grid=(nc,),
        in_specs=_ssd_specs(nc, True) + [
            pl.BlockSpec((None, D_INNER, SSD_N), lambda c: (rc(c), 0, 0)),
            pl.BlockSpec((CHUNK, D_INNER), lambda c: (rc(c), 0)),
        ],
        out_specs=[
            pl.BlockSpec((CHUNK, SSD_PROJ_PAD), lambda c: (rc(c), 0)),
            pl.BlockSpec((CHUNK, SSD_CONV_DIM), lambda c: (rc(c), 0)),
            vec, vec, vec,
            pl.BlockSpec((1, D_INNER), lambda c: (0, 0)),
        ],
        out_shape=[
            jax.ShapeDtypeStruct((L, SSD_PROJ_PAD), BF16),
            jax.ShapeDtypeStruct((L, SSD_CONV_DIM), F32),
            vshape, vshape, vshape,
            jax.ShapeDtypeStruct((1, D_INNER), F32),
        ],
        scratch_shapes=[pltpu.VMEM((D_INNER, SSD_N), F32)],
        compiler_params=_cparams(("arbitrary",)),
        name="ssd_layer_bwd",
    )(xbc, xbc, xbc, proj, dtb, alog, dsk, proj, wn, h_saved, dyn)


def _pick(n, options):
    for t in options:
        if n % t == 0:
            return t
    return n


def _token_tile(m, row_bytes, fixed_bytes):
    for t in (MM_TOKEN_TILE, MM_TOKEN_TILE // 2, MM_TOKEN_TILE // 4):
        if m % t == 0 and t * row_bytes + fixed_bytes <= MM_VMEM_BUDGET:
            return t
    return min(m, MM_TOKEN_TILE // 4)


def _mm(a, b, *, name, out_dtype=F32, add=None, exchange=()):
    M, K = a.shape
    N = b.shape[1]
    tn = _pick(N, MM_TILES)
    tk = _pick(K, MM_TILES)
    nk = K // tk
    row_bytes = 2 * (tk * a.dtype.itemsize + tn * jnp.dtype(out_dtype).itemsize + (tn * 4 if add is not None else 0)) \
        + (tn * 4 if nk > 1 else 0)
    tm = _token_tile(M, row_bytes, 2 * tk * tn * b.dtype.itemsize)
    grid = (M // tm, N // tn, nk)
    n = len(exchange)
    n_in = 2 + (add is not None)

    def body(*refs):
        a_ref, b_ref = refs[:2]
        add_ref = refs[2] if add is not None else None
        p_refs = refs[n_in:n_in + n]
        o_ref = refs[n_in + n]
        out_refs = refs[n_in + n + 1:n_in + 2 * n + 1]
        acc = refs[n_in + 2 * n + 1]
        ids = [pl.program_id(d) for d in range(3)]
        k = ids[2]
        if n:
            start, finish_exchange = _exchange_ops(p_refs, out_refs, *refs[n_in + 2 * n + 2:])

            @pl.when((ids[0] == 0) & (ids[1] == 0) & (k == 0))
            def _():
                start()

        p = _dot(_bf(a_ref[...]), _bf(b_ref[...]))

        def finish(r):
            if add is not None:
                r = r + add_ref[...]
            o_ref[...] = r.astype(out_dtype)

        if nk == 1:
            finish(p)
        else:
            @pl.when(k == 0)
            def _():
                acc[...] = p

            @pl.when((k > 0) & (k < nk - 1))
            def _():
                acc[...] += p

            @pl.when(k == nk - 1)
            def _():
                finish(acc[...] + p)

        if n:
            @pl.when((ids[0] == grid[0] - 1) & (ids[1] == grid[1] - 1) & (k == nk - 1))
            def _():
                finish_exchange()

    in_specs = [pl.BlockSpec((tm, tk), lambda i, j, k: (i, k)), pl.BlockSpec((tk, tn), lambda i, j, k: (k, j))]
    args = [a, b]
    if add is not None:
        in_specs.append(pl.BlockSpec((tm, tn), lambda i, j, k: (i, j)))
        args.append(add)
    res = pl.pallas_call(
        body,
        grid=grid,
        in_specs=in_specs + [ANY] * n,
        out_specs=[pl.BlockSpec((tm, tn), lambda i, j, k: (i, j))] + [ANY] * n,
        out_shape=[jax.ShapeDtypeStruct((M, N), out_dtype)] + [jax.ShapeDtypeStruct(p.shape, p.dtype) for p in exchange],
        scratch_shapes=[pltpu.VMEM((tm, tn) if nk > 1 else (8, 128), F32)] + (_comm_sems(n) if n else []),
        compiler_params=_cparams(("arbitrary",) * 3 if n else ("parallel", "parallel", "arbitrary")),
        name=name,
    )(*args, *exchange)
    return (res[0], res[1:]) if n else res[0]


def _mm_tn(a, b, *, name):
    M, K = a.shape
    N = b.shape[1]
    tn = _pick(N, MM_TILES)
    tm = _token_tile(M, 2 * (K * a.dtype.itemsize + tn * b.dtype.itemsize), 2 * K * tn * 4)

    def body(a_ref, b_ref, o_ref):
        _acc_out(o_ref, _dot_tn(_bf(a_ref[...]), _bf(b_ref[...])), pl.program_id(1) == 0)

    return pl.pallas_call(
        body,
        grid=(N // tn, M // tm),
        in_specs=[pl.BlockSpec((tm, K), lambda j, i: (i, 0)), pl.BlockSpec((tm, tn), lambda j, i: (i, j))],
        out_specs=pl.BlockSpec((K, tn), lambda j, i: (0, j)),
        out_shape=jax.ShapeDtypeStruct((K, N), F32),
        compiler_params=_cparams(("parallel", "arbitrary")),
        name=name,
    )(a, b)


def _rms_fwd(x, w, *, name):
    L, D = x.shape
    tm = min(TOKEN_TILE, L)

    def body(x_ref, w_ref, o_ref):
        o_ref[...] = _rms(x_ref[...], w_ref[...]).astype(BF16)

    return pl.pallas_call(
        body, grid=(L // tm,),
        in_specs=[pl.BlockSpec((tm, D), lambda i: (i, 0)), pl.BlockSpec((1, D), lambda i: (0, 0))],
        out_specs=pl.BlockSpec((tm, D), lambda i: (i, 0)),
        out_shape=jax.ShapeDtypeStruct((L, D), BF16),
        compiler_params=_cparams(("parallel",)), name=name,
    )(x, w)


def _rms_bwd(x, w, dhn, dres, *, name):
    L, D = x.shape
    tm = min(TOKEN_TILE, L)

    def body(x_ref, w_ref, dhn_ref, dres_ref, dx_ref, dw_ref):
        _, vjp = jax.vjp(_rms, x_ref[...], w_ref[...])
        dx, dw = vjp(dhn_ref[...])
        dx_ref[...] = dx + dres_ref[...]
        _acc_out(dw_ref, dw, pl.program_id(0) == 0)

    row = pl.BlockSpec((tm, D), lambda i: (i, 0))
    vec = pl.BlockSpec((1, D), lambda i: (0, 0))
    return pl.pallas_call(
        body, grid=(L // tm,),
        in_specs=[row, vec, row, row],
        out_specs=[row, vec],
        out_shape=[jax.ShapeDtypeStruct((L, D), F32), jax.ShapeDtypeStruct((1, D), F32)],
        compiler_params=_cparams(("arbitrary",)), name=name,
    )(x, w, dhn, dres)


CONV_HALO = 8
CONV_COLS = 1024


CONV_RB = 64
CONV_CB = 256


def _conv_pieces(tm):
    return [(r0, c0) for c0 in range(0, CONV_COLS, CONV_CB) for r0 in range(0, tm, min(CONV_RB, tm))]


def _conv_fwd(proj, w, b):
    L = proj.shape[0]
    tm = min(TOKEN_TILE, L)
    rb = min(CONV_RB, tm)
    c0 = D_INNER // CONV_COLS
    hb = tm // CONV_HALO

    def body(x_ref, h_ref, w_ref, b_ref, o_ref, sp_ref, xx):
        xx[0:CONV_HALO] = jnp.where(pl.program_id(1) == 0, 0.0, h_ref[...])
        xx[CONV_HALO:CONV_HALO + tm] = x_ref[...]
        for r0, cc in _conv_pieces(tm):
            cs = slice(cc, cc + CONV_CB)
            u = b_ref[:, cs]
            for k in range(SSD_CONV):
                off = CONV_HALO - (SSD_CONV - 1) + k + r0
                u = u + w_ref[k:k + 1, cs] * xx[off:off + rb, cs]
            s = 1.0 / (1.0 + jnp.exp(-u))
            o_ref[r0:r0 + rb, cs] = u * s
            sp_ref[r0:r0 + rb, cs] = s * (1.0 + u * (1.0 - s))

    blk = pl.BlockSpec((tm, CONV_COLS), lambda j, i: (i, j))
    shp = jax.ShapeDtypeStruct((L, SSD_CONV_DIM), F32)
    return pl.pallas_call(
        body, grid=(SSD_CONV_DIM // CONV_COLS, L // tm),
        in_specs=[
            pl.BlockSpec((tm, CONV_COLS), lambda j, i: (i, c0 + j)),
            pl.BlockSpec((CONV_HALO, CONV_COLS), lambda j, i: (jnp.maximum(i * hb - 1, 0), c0 + j)),
            pl.BlockSpec((SSD_CONV, CONV_COLS), lambda j, i: (0, j)),
            pl.BlockSpec((1, CONV_COLS), lambda j, i: (0, j)),
        ],
        out_specs=[blk, blk],
        out_shape=[shp, shp],
        scratch_shapes=[pltpu.VMEM((CONV_HALO + tm, CONV_COLS), F32)],
        compiler_params=_cparams(("parallel", "parallel")), name="conv_fwd",
    )(proj, proj, w, b)


def _conv_bwd(proj, w, sp, dxbc, dproj):
    L = proj.shape[0]
    tm = min(TOKEN_TILE, L)
    rb = min(CONV_RB, tm)
    nt = L // tm
    c0 = D_INNER // CONV_COLS
    hb = tm // CONV_HALO

    def fold(a):
        return jnp.sum(a.reshape(rb // 8, 8, CONV_CB), axis=0)

    def body(x_ref, h_ref, w_ref, sp_ref, dy_ref, dp_in_ref, dp_ref, dw_ref, db_ref, xx, dd):
        del dp_in_ref
        i = pl.program_id(1)
        first = i == 0

        @pl.when(first)
        def _():
            dd[tm:tm + CONV_HALO] = jnp.zeros((CONV_HALO, CONV_COLS), F32)

        xx[0:CONV_HALO] = jnp.where(i == nt - 1, 0.0, h_ref[...])
        xx[CONV_HALO:CONV_HALO + tm] = x_ref[...]
        dws, dbs = [], []
        for cc in range(0, CONV_COLS, CONV_CB):
            cs = slice(cc, cc + CONV_CB)
            acc = [jnp.zeros((8, CONV_CB), F32) for _ in range(SSD_CONV + 1)]
            for r0 in range(0, tm, rb):
                du = dy_ref[r0:r0 + rb, cs] * sp_ref[r0:r0 + rb, cs]
                dd[r0:r0 + rb, cs] = du
                for k in range(SSD_CONV):
                    off = CONV_HALO - (SSD_CONV - 1) + k + r0
                    acc[k] = acc[k] + fold(du * xx[off:off + rb, cs])
                acc[SSD_CONV] = acc[SSD_CONV] + fold(du)
            dws.append(jnp.concatenate([jnp.sum(a, axis=0, keepdims=True) for a in acc[:SSD_CONV]], axis=0))
            dbs.append(jnp.sum(acc[SSD_CONV], axis=0, keepdims=True))
        for r0, cc in _conv_pieces(tm):
            cs = slice(cc, cc + CONV_CB)
            dx = jnp.zeros((rb, CONV_CB), F32)
            for k in range(SSD_CONV):
                off = SSD_CONV - 1 - k + r0
                dx = dx + w_ref[k:k + 1, cs] * dd[off:off + rb, cs]
            dp_ref[r0:r0 + rb, cs] = dx.astype(BF16)
        _acc_out(dw_ref, jnp.concatenate(dws, axis=1), first)
        _acc_out(db_ref, jnp.concatenate(dbs, axis=1), first)
        dd[tm:tm + CONV_HALO] = dd[0:CONV_HALO]

    rt = lambda i: nt - 1 - i
    return pl.pallas_call(
        body, grid=(SSD_CONV_DIM // CONV_COLS, nt),
        in_specs=[
            pl.BlockSpec((tm, CONV_COLS), lambda j, i: (rt(i), c0 + j)),
            pl.BlockSpec((CONV_HALO, CONV_COLS), lambda j, i: (jnp.maximum(rt(i) * hb - 1, 0), c0 + j)),
            pl.BlockSpec((SSD_CONV, CONV_COLS), lambda j, i: (0, j)),
            pl.BlockSpec((tm, CONV_COLS), lambda j, i: (rt(i), j)),
            pl.BlockSpec((tm, CONV_COLS), lambda j, i: (rt(i), j)),
            pl.BlockSpec(memory_space=pl.ANY),
        ],
        out_specs=[
            pl.BlockSpec((tm, CONV_COLS), lambda j, i: (rt(i), c0 + j)),
            pl.BlockSpec((SSD_CONV, CONV_COLS), lambda j, i: (0, j)),
            pl.BlockSpec((1, CONV_COLS), lambda j, i: (0, j)),
        ],
        out_shape=[jax.ShapeDtypeStruct((L, SSD_PROJ_PAD), BF16), jax.ShapeDtypeStruct((SSD_CONV, SSD_CONV_DIM), F32),
                   jax.ShapeDtypeStruct((1, SSD_CONV_DIM), F32)],
        scratch_shapes=[pltpu.VMEM((CONV_HALO + tm, CONV_COLS), F32), pltpu.VMEM((tm + CONV_HALO, CONV_COLS), F32)],
        input_output_aliases={5: 0},
        compiler_params=_cparams(("arbitrary", "arbitrary")), name="conv_bwd",
    )(proj, proj, w, sp, dxbc, dproj)


def _loss_bwd(x, tgt, w):
    L, D = x.shape
    tm = min(TOKEN_TILE, L)

    def body(x_ref, t_ref, w_ref, l_ref, dx_ref, dw_ref):
        xv = x_ref[...]
        wv = w_ref[...]
        r = lax.rsqrt(jnp.mean(xv * xv, axis=-1, keepdims=True) + RMS_EPS)
        xh = xv * r
        e = xh * wv - t_ref[...]
        lsum = 0.5 * jnp.sum(jnp.mean(e * e, axis=-1, keepdims=True), axis=0, keepdims=True)
        dout = e * (1.0 / D)
        gx = dout * wv
        dx_ref[...] = r * (gx - xh * jnp.mean(gx * xh, axis=-1, keepdims=True))
        first = pl.program_id(0) == 0
        _acc_out(dw_ref, jnp.sum(dout * xh, axis=0, keepdims=True), first)
        _acc_out(l_ref, jnp.broadcast_to(lsum, (8, 128)), first)

    row = pl.BlockSpec((tm, D), lambda i: (i, 0))
    vec = pl.BlockSpec((1, D), lambda i: (0, 0))
    return pl.pallas_call(
        body, grid=(L // tm,),
        in_specs=[row, row, vec],
        out_specs=[pl.BlockSpec((8, 128), lambda i: (0, 0)), row, vec],
        out_shape=[jax.ShapeDtypeStruct((8, 128), F32), jax.ShapeDtypeStruct((L, D), F32), jax.ShapeDtypeStruct((1, D), F32)],
        compiler_params=_cparams(("arbitrary",)), name="loss_bwd",
    )(x, tgt, w)


MESH = pl.DeviceIdType.MESH
ANY = pl.BlockSpec(memory_space=pl.ANY)


def _comm_sems(n):
    return [pltpu.SemaphoreType.DMA((n, 7)), pltpu.SemaphoreType.DMA((n, 7)), pltpu.SemaphoreType.DMA((n,))]


def _gather_ops(x_refs, out_refs, send_sems, recv_sems, local_sems):
    n = len(x_refs)
    x, y, c = lax.axis_index("x"), lax.axis_index("y"), lax.axis_index("c")
    me, sibling = (x, y, c), (x, y, 1 - c)
    chips = [(1 - x, y), (x, 1 - y), (1 - x, 1 - y)]

    def slot(i, px, py, pc):
        return out_refs[i].at[4 * px + 2 * py + pc]

    def copy(i, k, block, to, src=None):
        return pltpu.make_async_remote_copy(
            src_ref=slot(i, *block) if src is None else src, dst_ref=slot(i, *block),
            send_sem=send_sems.at[i, k], recv_sem=recv_sems.at[i, k], device_id=to, device_id_type=MESH)

    def own():
        mine = [pltpu.make_async_copy(x_refs[i], slot(i, *me), local_sems.at[i]) for i in range(n)]
        first = [copy(i, 0, me, sibling, src=x_refs[i]) for i in range(n)]
        first += [copy(i, 1 + j, me, (*chip, c), src=x_refs[i]) for j, chip in enumerate(chips) for i in range(n)]
        return mine, first

    def start():
        mine, first = own()
        for cp in mine + first:
            cp.start()

    def finish():
        mine, first = own()
        passed = []
        for j, chip in enumerate(chips):
            for i in range(n):
                copy(i, 1 + j, (*chip, c), me).wait_recv()
                passed.append(copy(i, 4 + j, (*chip, c), sibling))
                passed[-1].start()
        for i in range(n):
            copy(i, 0, sibling, me).wait_recv()
        for j, chip in enumerate(chips):
            for i in range(n):
                copy(i, 4 + j, (*chip, 1 - c), me).wait_recv()
        for cp in first + passed:
            cp.wait_send()
        for cp in mine:
            cp.wait()

    return start, finish


def _exchange_ops(p_refs, out_refs, send_sems, recv_sems, local_sems):
    n = len(p_refs)
    x, y, c = lax.axis_index("x"), lax.axis_index("y"), lax.axis_index("c")
    my = 4 * x + 2 * y + c

    def peer(k):
        fx, fy, fc = (k >> 2) & 1, (k >> 1) & 1, k & 1
        px, py, pc = (1 - x if fx else x), (1 - y if fy else y), (1 - c if fc else c)
        return (px, py, pc), 4 * px + 2 * py + pc

    def mine():
        return [pltpu.make_async_copy(p_refs[i].at[my], out_refs[i].at[my], local_sems.at[i]) for i in range(n)]

    def start():
        for cp in mine():
            cp.start()
        for k in range(1, N_DEV):
            to, pid = peer(k)
            for i in range(n):
                pltpu.make_async_remote_copy(
                    src_ref=p_refs[i].at[pid], dst_ref=out_refs[i].at[my], send_sem=send_sems.at[i, k - 1],
                    recv_sem=recv_sems.at[i, k - 1], device_id=to, device_id_type=MESH).start()

    def finish():
        for k in range(1, N_DEV):
            to, pid = peer(k)
            for i in range(n):
                pltpu.make_async_remote_copy(
                    src_ref=p_refs[i].at[pid], dst_ref=out_refs[i].at[pid], send_sem=send_sems.at[i, k - 1],
                    recv_sem=recv_sems.at[i, k - 1], device_id=to, device_id_type=MESH).wait()
        for cp in mine():
            cp.wait()

    return start, finish


def _all_gather(xs, *, name):
    n = len(xs)

    def body(*refs):
        start, finish = _gather_ops(refs[:n], refs[n:2 * n], *refs[2 * n:])
        start()
        finish()

    return pl.pallas_call(
        body,
        out_shape=[jax.ShapeDtypeStruct((N_DEV,) + a.shape, a.dtype) for a in xs],
        in_specs=[ANY] * n, out_specs=[ANY] * n, scratch_shapes=_comm_sems(n), name=name,
    )(*xs)


def _adamw(parts, w, m, v, *, name):
    a, b = w.shape
    tr = _pick(a, (256, 128, 64, 32, 16, 8))

    def body(p_ref, w_ref, m_ref, v_ref, g_ref, d_ref, mo_ref, vo_ref):
        g = p_ref[0]
        for s in range(1, N_DEV):
            g = g + p_ref[s]
        mn = ADAM_B1 * m_ref[...] + (1.0 - ADAM_B1) * g
        vn = ADAM_B2 * v_ref[...] + (1.0 - ADAM_B2) * jnp.square(g)
        m_hat = mn / (1.0 - ADAM_B1 ** ADAM_STEP)
        v_hat = vn / (1.0 - ADAM_B2 ** ADAM_STEP)
        g_ref[...] = g
        d_ref[...] = -ADAM_LR * (m_hat / (jnp.sqrt(v_hat) + ADAM_EPS) + ADAM_WD * w_ref[...])
        mo_ref[...] = mn
        vo_ref[...] = vn

    blk = pl.BlockSpec((tr, b), lambda i: (i, 0))
    shp = jax.ShapeDtypeStruct((a, b), F32)
    return pl.pallas_call(
        body, grid=(a // tr,),
        in_specs=[pl.BlockSpec((N_DEV, tr, b), lambda i: (0, i, 0)), blk, blk, blk],
        out_specs=[blk, blk, blk, blk],
        out_shape=[shp, shp, shp, shp],
        compiler_params=_cparams(("parallel",)), name=name,
    )(parts, w, m, v)


def _col_shards(a, n):
    return a.reshape(a.shape[0], N_DEV, n).transpose(1, 0, 2)


def _from_col_shards(g, cols):
    r = g.shape[1]
    full = g.transpose(1, 0, 2).reshape(r, -1)
    return jnp.pad(full, ((0, 0), (0, cols - full.shape[1])))


def kernel(x, norm_w, gla_in_proj, gla_gate_up, gla_gate_bias, gla_head_norm, gla_out_proj, ssd_in_proj, ssd_conv_w, ssd_conv_b, ssd_dt_bias, ssd_a_log, ssd_d, ssd_gate_norm, ssd_out_proj, final_norm, loss_target, m_norm_w, m_gla_in_proj, m_gla_gate_up, m_gla_gate_bias, m_gla_head_norm, m_gla_out_proj, m_ssd_in_proj, m_ssd_conv_w, m_ssd_conv_b, m_ssd_dt_bias, m_ssd_a_log, m_ssd_d, m_ssd_gate_norm, m_ssd_out_proj, m_final_norm, v_norm_w, v_gla_in_proj, v_gla_gate_up, v_gla_gate_bias, v_gla_head_norm, v_gla_out_proj, v_ssd_in_proj, v_ssd_conv_w, v_ssd_conv_b, v_ssd_dt_bias, v_ssd_a_log, v_ssd_d, v_ssd_gate_norm, v_ssd_out_proj, v_final_norm):
    x0 = x[0]
    tgt = loss_target[0]
    n_gin = GLA_PROJ // N_DEV
    n_sin = SSD_PROJ // N_DEV
    n_up = GLA_DK // N_DEV
    n_cv = SSD_CONV_DIM // N_DEV

    g_gin, g_up, g_gout = _all_gather(
        [gla_in_proj[0].astype(BF16), gla_gate_up[0].astype(BF16), gla_out_proj[0].astype(BF16)], name="gather_weights")
    w_gin = _from_col_shards(g_gin, GLA_PROJ_PAD)
    wup = jnp.pad(_from_col_shards(g_up, GLA_DK), ((0, 128 - GLA_RANK), (0, 0))).astype(F32)
    w_gout = g_gout.reshape(D_INNER, D_MODEL)
    vec128 = lambda a: jnp.pad(a.reshape(1, -1), ((0, 0), (0, 128 - a.size)))
    dtb, alog, dsk = vec128(ssd_dt_bias), vec128(ssd_a_log), vec128(ssd_d)
    nw0, nw1 = norm_w[0:1], norm_w[1:2]

    hn1 = _rms_fwd(x0, nw0, name="rms1_fwd")
    proj1 = _mm(hn1, w_gin, name="gla_in_proj")
    (o, og, s_saved), (g_sin, g_sout, g_cw, g_cb, g_gn) = _gla_layer_fwd(
        proj1, wup, gla_gate_bias, gla_head_norm,
        [ssd_in_proj[0].astype(BF16), ssd_out_proj[0].astype(BF16), ssd_conv_w[0], ssd_conv_b, ssd_gate_norm])
    w_sin = _from_col_shards(g_sin, SSD_PROJ_PAD)
    w_sout = g_sout.reshape(D_INNER, D_MODEL)
    conv_w = _from_col_shards(g_cw, SSD_CONV_DIM)
    conv_b = g_cb.reshape(1, SSD_CONV_DIM)
    gate_norm = g_gn.reshape(1, D_INNER)
    x1 = _mm(og, w_gout, add=x0, name="gla_out_proj")
    hn2 = _rms_fwd(x1, nw1, name="rms2_fwd")
    proj2 = _mm(hn2, w_sin, name="ssd_in_proj")
    xbc, conv_sp = _conv_fwd(proj2, conv_w, conv_b)
    yn, h_saved = _ssd_layer_fwd(xbc, proj2, dtb, alog, dsk, gate_norm)
    x2 = _mm(yn, w_sout, add=x1, name="ssd_out_proj")
    lsum, dx2, d_final = _loss_bwd(x2, tgt, final_norm.reshape(1, D_MODEL))
    loss = lax.psum(lsum[0, 0], ("x", "y", "c"))

    d_sout = _mm_tn(yn, dx2, name="ssd_out_proj_dw")
    dyn = _mm(dx2, w_sout.T, name="ssd_out_proj_dx")
    dproj2, dxbc, d_dtb, d_alog, d_dsk, d_gate_norm = _ssd_layer_bwd(xbc, proj2, dtb, alog, dsk, gate_norm, h_saved, dyn)
    dproj2, d_conv_w, d_conv_b = _conv_bwd(proj2, conv_w, conv_sp, dxbc, dproj2)
    d_sin = _mm_tn(hn2, dproj2, name="ssd_in_proj_dw")
    dhn2 = _mm(dproj2, w_sin.T, name="ssd_in_proj_dx")
    dx1, d_nw1 = _rms_bwd(x1, nw1, dhn2, dx2, name="rms2_bwd")
    d_gout = _mm_tn(og, dx1, name="gla_out_proj_dw")
    dog = _mm(dx1, w_gout.T, name="gla_out_proj_dx")
    early = {
        "gla_out_proj": ((gla_out_proj[0], m_gla_out_proj[0], v_gla_out_proj[0]), d_gout.reshape(N_DEV, -1, D_MODEL)),
        "ssd_in_proj": ((ssd_in_proj[0], m_ssd_in_proj[0], v_ssd_in_proj[0]), _col_shards(d_sin[:, :SSD_PROJ], n_sin)),
        "ssd_conv_w": ((ssd_conv_w[0], m_ssd_conv_w[0], v_ssd_conv_w[0]), _col_shards(d_conv_w, n_cv)),
        "ssd_conv_b": ((ssd_conv_b, m_ssd_conv_b, v_ssd_conv_b), d_conv_b.reshape(N_DEV, 1, n_cv)),
        "ssd_gate_norm": ((ssd_gate_norm, m_ssd_gate_norm, v_ssd_gate_norm), d_gate_norm.reshape(N_DEV, 1, -1)),
        "ssd_out_proj": ((ssd_out_proj[0], m_ssd_out_proj[0], v_ssd_out_proj[0]), d_sout.reshape(N_DEV, -1, D_MODEL)),
    }
    (dproj1, d_wup, d_gbias, d_head_norm), early_recv = _gla_layer_bwd(
        proj1, wup, gla_gate_bias, gla_head_norm, o, s_saved, dog, [p for _, p in early.values()])
    d_gin = _mm_tn(hn1, dproj1, name="gla_in_proj_dw")
    late = {
        "gla_in_proj": ((gla_in_proj[0], m_gla_in_proj[0], v_gla_in_proj[0]), _col_shards(d_gin[:, :GLA_PROJ], n_gin)),
        "gla_gate_up": ((gla_gate_up[0], m_gla_gate_up[0], v_gla_gate_up[0]), _col_shards(d_wup[:GLA_RANK], n_up)),
    }
    dhn1, late_recv = _mm(dproj1, w_gin.T, name="gla_in_proj_dx", exchange=[p for _, p in late.values()])
    dx0, d_nw0 = _rms_bwd(x0, nw0, dhn1, dx1, name="rms1_bwd")
    heads = SSD_HEADS
    replicated = {
        "norm_w": ((norm_w, m_norm_w, v_norm_w), jnp.concatenate([d_nw0, d_nw1], axis=0)),
        "gla_gate_bias": ((gla_gate_bias, m_gla_gate_bias, v_gla_gate_bias), d_gbias),
        "gla_head_norm": ((gla_head_norm, m_gla_head_norm, v_gla_head_norm), d_head_norm),
        "ssd_dt_bias": ((ssd_dt_bias, m_ssd_dt_bias, v_ssd_dt_bias), d_dtb[:, :heads]),
        "ssd_a_log": ((ssd_a_log, m_ssd_a_log, v_ssd_a_log), d_alog[:, :heads]),
        "ssd_d": ((ssd_d, m_ssd_d, v_ssd_d), d_dsk[:, :heads]),
        "final_norm": (tuple(t.reshape(1, D_MODEL) for t in (final_norm, m_final_norm, v_final_norm)), d_final),
    }
    results = {}
    for group, recv in ((early, early_recv),
                        (late, late_recv),
                        (replicated, _all_gather([p for _, p in replicated.values()], name="replicated_gather"))):
        for (nm, ((w, m, v), _)), r in zip(group.items(), recv):
            results[nm] = _adamw(r, w, m, v, name=nm + "_adamw")

    order = [("norm_w", norm_w), ("gla_in_proj", gla_in_proj), ("gla_gate_up", gla_gate_up), ("gla_gate_bias", gla_gate_bias),
             ("gla_head_norm", gla_head_norm), ("gla_out_proj", gla_out_proj), ("ssd_in_proj", ssd_in_proj),
             ("ssd_conv_w", ssd_conv_w), ("ssd_conv_b", ssd_conv_b), ("ssd_dt_bias", ssd_dt_bias), ("ssd_a_log", ssd_a_log),
             ("ssd_d", ssd_d), ("ssd_gate_norm", ssd_gate_norm), ("ssd_out_proj", ssd_out_proj), ("final_norm", final_norm)]
    out = [loss, dx0[None]]
    for i in range(4):
        out += [results[nm][i].reshape(ref.shape) for nm, ref in order]
    return tuple(out)
```

```python
import jax
import jax.numpy as jnp
from jax import lax
from jax.experimental import pallas as pl
from jax.experimental.pallas import tpu as pltpu

F32 = jnp.float32
BF16 = jnp.bfloat16

D_MODEL = 1024
D_INNER = 2048
RMS_EPS = 1e-6
GLA_HEADS = 4
GLA_DK = 512
GLA_HEAD_K = 128
GLA_HEAD_V = 512
GLA_RANK = 16
GLA_NORMALIZER = 16.0
CHUNK = 64
SUB = 16
STEP_CHUNKS = 2
GLA_PROJ = 5136
GLA_PROJ_PAD = 5376
GLA_GK_COL = 5120
SSD_HEADS = 32
SSD_GROUPS = 8
SSD_HPG = 4
SSD_P = 64
SSD_N = 128
SSD_CONV = 4
SSD_CONV_DIM = 4096
SSD_PROJ = 6176
SSD_PROJ_PAD = 6400
SSD_DT_COL = 6144
N_DEV = 8

ADAM_LR = 0.001
ADAM_B1 = 0.9
ADAM_B2 = 0.999
ADAM_EPS = 1e-08
ADAM_WD = 0.01
ADAM_STEP = 10

VMEM_LIMIT = 56 * 1024 * 1024
TOKEN_TILE = 512
MM_TOKEN_TILE = 2048
MM_VMEM_BUDGET = 44 * 1024 * 1024
MM_TILES = (1792, 1280, 1024, 768, 512, 256, 128)


def _dot(a, b):
    return jnp.dot(a, b, preferred_element_type=F32)


def _dot_nt(a, b):
    return lax.dot_general(a, b, (((1,), (1,)), ((), ())), preferred_element_type=F32)


def _dot_tn(a, b):
    return lax.dot_general(a, b, (((0,), (0,)), ((), ())), preferred_element_type=F32)


def _bf(a):
    return a.astype(BF16)


@jax.custom_vjp
def _mxu(a, b):
    return _dot(_bf(a), _bf(b))


def _mxu_fwd(a, b):
    return _mxu(a, b), (a, b)


def _mxu_bwd(res, g):
    a, b = res
    return _dot_nt(_bf(g), _bf(b)), _dot_tn(_bf(a), _bf(g))


_mxu.defvjp(_mxu_fwd, _mxu_bwd)


@jax.custom_vjp
def _mxu_nt(a, b):
    return _dot_nt(_bf(a), _bf(b))


def _mxu_nt_fwd(a, b):
    return _mxu_nt(a, b), (a, b)


def _mxu_nt_bwd(res, g):
    a, b = res
    return _dot(_bf(g), _bf(b)), _dot_tn(_bf(g), _bf(a))


_mxu_nt.defvjp(_mxu_nt_fwd, _mxu_nt_bwd)


@jax.custom_vjp
def _mxu_tn(a, b):
    return _dot_tn(_bf(a), _bf(b))


def _mxu_tn_fwd(a, b):
    return _mxu_tn(a, b), (a, b)


def _mxu_tn_bwd(res, g):
    a, b = res
    return _dot_nt(_bf(b), _bf(g)), _dot(_bf(a), _bf(g))


_mxu_tn.defvjp(_mxu_tn_fwd, _mxu_tn_bwd)


def _split2(a):
    hi = _bf(a)
    return hi, _bf(a - hi.astype(F32))


def _three_pass(dot, a, b):
    ah, al = _split2(a)
    bh, bl = _split2(b)
    return dot(ah, bh) + (dot(ah, bl) + dot(al, bh))


@jax.custom_vjp
def _dot3_nt(a, b):
    return _three_pass(_dot_nt, a, b)


def _dot3_nt_fwd(a, b):
    return _dot3_nt(a, b), (a, b)


def _dot3_nt_bwd(res, g):
    a, b = res
    return _three_pass(_dot, g, b), _three_pass(_dot_tn, g, a)


_dot3_nt.defvjp(_dot3_nt_fwd, _dot3_nt_bwd)


def _silu(x):
    return x / (1.0 + jnp.exp(-x))


def _log_sigmoid(z):
    return jnp.minimum(z, 0.0) - jnp.log(1.0 + jnp.exp(-jnp.abs(z)))


def _softplus(z):
    return jnp.maximum(z, 0.0) + jnp.log(1.0 + jnp.exp(-jnp.abs(z)))


def _iota(shape, dim):
    return lax.broadcasted_iota(jnp.int32, shape, dim)


def _rms(x, w):
    return x * lax.rsqrt(jnp.mean(x * x, axis=-1, keepdims=True) + RMS_EPS) * w


def _scan_rows(a, reverse, seg):
    n = a.shape[0]
    pos = _iota(a.shape, 0) & (seg - 1)
    sh = 1
    while sh < seg:
        if reverse:
            a = a + jnp.where(pos < seg - sh, pltpu.roll(a, n - sh, 0), 0.0)
        else:
            a = a + jnp.where(pos >= sh, pltpu.roll(a, sh, 0), 0.0)
        sh *= 2
    return a


def _make_cumsum(seg):
    @jax.custom_vjp
    def cumsum(a):
        return _scan_rows(a, False, seg)

    cumsum.defvjp(lambda a: (_scan_rows(a, False, seg), None), lambda _, g: (_scan_rows(g, True, seg),))
    return cumsum


_cumsum_sub = _make_cumsum(SUB)
_cumsum_rows = _make_cumsum(CHUNK)


def _cparams(sem):
    return pltpu.CompilerParams(dimension_semantics=sem, vmem_limit_bytes=VMEM_LIMIT)


def _acc_out(ref, val, first):
    @pl.when(first)
    def _():
        ref[...] = val

    @pl.when(jnp.logical_not(first))
    def _():
        ref[...] += val


def _gla_chunk(q, k, va, vb, gk, wup, bias, sts):
    nb = CHUNK // SUB
    heads = range(GLA_HEADS)
    hc = lambda a, h: a[:, h * GLA_HEAD_K:(h + 1) * GLA_HEAD_K]
    v = [(va if h < 2 else vb)[:, (h % 2) * GLA_HEAD_V:(h % 2 + 1) * GLA_HEAD_V] for h in heads]
    z = _mxu(gk, wup) + bias
    la = _log_sigmoid(z) * (1.0 / GLA_NORMALIZER)
    qs = q * (GLA_HEAD_K ** -0.5)
    bl = _cumsum_sub(la)
    tot = [jnp.sum(la[i * SUB:(i + 1) * SUB], axis=0, keepdims=True) for i in range(nb)]
    pre = [jnp.zeros((1, GLA_DK), F32)]
    for i in range(nb):
        pre.append(pre[i] + tot[i])
    b_last = pre[nb]
    rows_of = lambda vals: jnp.concatenate([jnp.broadcast_to(t, (SUB, GLA_DK)) for t in vals], axis=0)
    suf = rows_of(tot) - bl
    nxt = rows_of(pre[1:])
    q_in = qs * jnp.exp(bl + rows_of(pre[:nb]))
    k_end = k * jnp.exp(suf + (b_last - nxt))
    dec = jnp.exp(b_last)
    qa = qs * jnp.exp(bl)
    o_inter = [_mxu_nt(hc(q_in, h), sts[h]) for h in heads]
    sts_new = tuple(sts[h] * hc(dec, h) + _mxu_tn(v[h], hc(k_end, h)) for h in heads)
    half = SUB // 2
    rs = _iota((SUB, GLA_HEAD_K), 0)
    cs = _iota((half, CHUNK), 1)
    a_rows = [[] for _ in heads]
    for i in range(nb):
        sl = slice(i * SUB, (i + 1) * SUB)
        n = i * SUB
        if i > 0:
            kp = jnp.concatenate([k[:n] * jnp.exp(suf[:n] + (pre[i] - nxt[:n])), jnp.zeros((CHUNK - n, GLA_DK), F32)], axis=0)
        for h in heads:
            q_i, k_i, bl_i = hc(qs[sl], h), hc(k[sl], h), hc(bl[sl], h)
            if i > 0:
                a_i = _dot3_nt(hc(qa[sl], h), hc(kp, h))
                a_top, a_bot = a_i[:half], a_i[half:]
            else:
                a_top = a_bot = jnp.zeros((half, CHUNK), F32)
            for j in range(SUB):
                lo = 0 if j < half else half
                e = jnp.exp(jnp.minimum(bl_i[lo:] - bl_i[j:j + 1], 0.0))
                t = jnp.where(rs[lo:] >= j, q_i[lo:] * e * k_i[j:j + 1], 0.0)
                rsum = jnp.sum(t, axis=-1, keepdims=True)
                hit = cs == i * SUB + j
                if lo == 0:
                    a_top = a_top + jnp.where(hit, rsum[:half], 0.0)
                a_bot = a_bot + jnp.where(hit, rsum[half - lo:], 0.0)
            a_rows[h] += [a_top, a_bot]
    o = [o_inter[h] + _mxu(jnp.concatenate(a_rows[h], axis=0), v[h]) for h in heads]
    return jnp.concatenate(o, axis=1), sts_new


def _gla_post(o, g, wn):
    return _rms(o, wn) * _silu(g)


GLA_HALF = 2 * GLA_HEAD_V


def _gla_specs(rows, nc, rev):
    ci = (lambda c: nc - 1 - c) if rev else (lambda c: c)
    v0 = 2 * GLA_DK // GLA_HALF
    g0 = (2 * GLA_DK + D_INNER) // GLA_HALF
    return [
        pl.BlockSpec((rows, GLA_DK), lambda c: (ci(c), 0)),
        pl.BlockSpec((rows, GLA_DK), lambda c: (ci(c), 1)),
        pl.BlockSpec((rows, GLA_HALF), lambda c: (ci(c), v0)),
        pl.BlockSpec((rows, GLA_HALF), lambda c: (ci(c), v0 + 1)),
        pl.BlockSpec((rows, GLA_HALF), lambda c: (ci(c), g0)),
        pl.BlockSpec((rows, GLA_HALF), lambda c: (ci(c), g0 + 1)),
        pl.BlockSpec((rows, 128), lambda c: (ci(c), GLA_GK_COL // 128)),
        pl.BlockSpec((128, GLA_DK), lambda c: (0, 0)),
        pl.BlockSpec((1, GLA_DK), lambda c: (0, 0)),
        pl.BlockSpec((1, GLA_HEAD_V), lambda c: (0, 0)),
    ]


def _head_cols(ref_a, ref_b, h, rows=slice(None)):
    ref = ref_a if h < 2 else ref_b
    return ref[rows, (h % 2) * GLA_HEAD_V:(h % 2 + 1) * GLA_HEAD_V]


def _gla_layer_fwd(proj, wup, bias, wn, gather):
    L = proj.shape[0]
    sc = min(STEP_CHUNKS, L // CHUNK)
    rows = sc * CHUNK
    nc = L // rows
    n = len(gather)

    def body(*refs):
        q_ref, k_ref, va_ref, vb_ref, ga_ref, gb_ref, gk_ref, wup_ref, b_ref, wn_ref = refs[:10]
        x_refs = refs[10:10 + n]
        o_ref, og_ref, s_ref = refs[10 + n:13 + n]
        out_refs = refs[13 + n:13 + 2 * n]
        st, send_sems, recv_sems, local_sems = refs[13 + 2 * n:]
        start, finish = _gather_ops(x_refs, out_refs, send_sems, recv_sems, local_sems)

        @pl.when(pl.program_id(0) == 0)
        def _():
            st[...] = jnp.zeros(st.shape, F32)
            start()

        s_cur = tuple(st[h] for h in range(GLA_HEADS))
        for u in range(sc):
            r = slice(u * CHUNK, (u + 1) * CHUNK)
            for h in range(GLA_HEADS):
                s_ref[u, h] = s_cur[h]
            o, s_cur = _gla_chunk(q_ref[r], k_ref[r], va_ref[r], vb_ref[r], gk_ref[r], wup_ref[...], b_ref[...], s_cur)
            o_ref[r] = o
            for h in range(GLA_HEADS):
                vc = slice(h * GLA_HEAD_V, (h + 1) * GLA_HEAD_V)
                og_ref[r, vc] = _gla_post(o[:, vc], _head_cols(ga_ref, gb_ref, h, r), wn_ref[...]).astype(BF16)
        for h in range(GLA_HEADS):
            st[h] = s_cur[h]

        @pl.when(pl.program_id(0) == nc - 1)
        def _():
            finish()

    res = pl.pallas_call(
        body,
        grid=(nc,),
        in_specs=_gla_specs(rows, nc, False) + [ANY] * n,
        out_specs=[
            pl.BlockSpec((rows, D_INNER), lambda c: (c, 0)),
            pl.BlockSpec((rows, D_INNER), lambda c: (c, 0)),
            pl.BlockSpec((sc, GLA_HEADS, GLA_HEAD_V, GLA_HEAD_K), lambda c: (c, 0, 0, 0)),
        ] + [ANY] * n,
        out_shape=[
            jax.ShapeDtypeStruct((L, D_INNER), F32),
            jax.ShapeDtypeStruct((L, D_INNER), BF16),
            jax.ShapeDtypeStruct((L // CHUNK, GLA_HEADS, GLA_HEAD_V, GLA_HEAD_K), F32),
        ] + [jax.ShapeDtypeStruct((N_DEV,) + a.shape, a.dtype) for a in gather],
        scratch_shapes=[pltpu.VMEM((GLA_HEADS, GLA_HEAD_V, GLA_HEAD_K), F32)] + _comm_sems(n),
        compiler_params=_cparams(("arbitrary",)),
        name="gla_layer_fwd",
    )(proj, proj, proj, proj, proj, proj, proj, wup, bias, wn, *gather)
    return res[:3], res[3:]


def _gla_layer_bwd(proj, wup, bias, wn, o, s_in, dog, exchange):
    L = proj.shape[0]
    nc = L // CHUNK
    n = len(exchange)

    def body(*refs):
        (q_ref, k_ref, va_ref, vb_ref, ga_ref, gb_ref, gk_ref, wup_ref, b_ref, wn_ref, o_ref, s_ref, dog_ref) = refs[:13]
        p_refs = refs[13:13 + n]
        dp_ref, dwup_ref, db_ref, dwn_ref = refs[13 + n:17 + n]
        out_refs = refs[17 + n:17 + 2 * n]
        dst, send_sems, recv_sems, local_sems = refs[17 + 2 * n:]
        start, finish = _exchange_ops(p_refs, out_refs, send_sems, recv_sems, local_sems)

        @pl.when(pl.program_id(0) == 0)
        def _():
            dst[...] = jnp.zeros(dst.shape, F32)
            dwup_ref[...] = jnp.zeros(dwup_ref.shape, F32)
            db_ref[...] = jnp.zeros(db_ref.shape, F32)
            dwn_ref[...] = jnp.zeros(dwn_ref.shape, F32)
            start()

        dos = []
        for h in range(GLA_HEADS):
            vc = slice(h * GLA_HEAD_V, (h + 1) * GLA_HEAD_V)
            _, post_vjp = jax.vjp(_gla_post, o_ref[:, vc], _head_cols(ga_ref, gb_ref, h), wn_ref[...])
            do, dg, dwn = post_vjp(dog_ref[:, vc])
            dos.append(do)
            dp_ref[:, 2 * GLA_DK + D_INNER + h * GLA_HEAD_V:2 * GLA_DK + D_INNER + (h + 1) * GLA_HEAD_V] = dg.astype(BF16)
            dwn_ref[...] += dwn
        _, vjp = jax.vjp(_gla_chunk, q_ref[...], k_ref[...], va_ref[...], vb_ref[...], gk_ref[...], wup_ref[...], b_ref[...],
                         tuple(s_ref[h] for h in range(GLA_HEADS)))
        dq, dk, dva, dvb, dgk, dwup, db, ds = vjp((jnp.concatenate(dos, axis=1), tuple(dst[h] for h in range(GLA_HEADS))))
        for h in range(GLA_HEADS):
            dst[h] = ds[h]
        dp_ref[:, :GLA_DK] = dq.astype(BF16)
        dp_ref[:, GLA_DK:2 * GLA_DK] = dk.astype(BF16)
        dp_ref[:, 2 * GLA_DK:2 * GLA_DK + GLA_HALF] = dva.astype(BF16)
        dp_ref[:, 2 * GLA_DK + GLA_HALF:2 * GLA_DK + D_INNER] = dvb.astype(BF16)
        dwup_ref[...] += dwup
        db_ref[...] += db
        dp_ref[:, GLA_GK_COL:GLA_GK_COL + 128] = dgk.astype(BF16)
        dp_ref[:, GLA_GK_COL + 128:] = jnp.zeros((CHUNK, GLA_PROJ_PAD - GLA_GK_COL - 128), BF16)

        @pl.when(pl.program_id(0) == nc - 1)
        def _():
            finish()

    rc = lambda c: nc - 1 - c
    res = pl.pallas_call(
        body,
        grid=(nc,),
        in_specs=_gla_specs(CHUNK, nc, True) + [
            pl.BlockSpec((CHUNK, D_INNER), lambda c: (rc(c), 0)),
            pl.BlockSpec((None, GLA_HEADS, GLA_HEAD_V, GLA_HEAD_K), lambda c: (rc(c), 0, 0, 0)),
            pl.BlockSpec((CHUNK, D_INNER), lambda c: (rc(c), 0)),
        ] + [ANY] * n,
        out_specs=[
            pl.BlockSpec((CHUNK, GLA_PROJ_PAD), lambda c: (rc(c), 0)),
            pl.BlockSpec((128, GLA_DK), lambda c: (0, 0)),
            pl.BlockSpec((1, GLA_DK), lambda c: (0, 0)),
            pl.BlockSpec((1, GLA_HEAD_V), lambda c: (0, 0)),
        ] + [ANY] * n,
        out_shape=[
            jax.ShapeDtypeStruct((L, GLA_PROJ_PAD), BF16),
            jax.ShapeDtypeStruct((128, GLA_DK), F32),
            jax.ShapeDtypeStruct((1, GLA_DK), F32),
            jax.ShapeDtypeStruct((1, GLA_HEAD_V), F32),
        ] + [jax.ShapeDtypeStruct(a.shape, a.dtype) for a in exchange],
        scratch_shapes=[pltpu.VMEM((GLA_HEADS, GLA_HEAD_V, GLA_HEAD_K), F32)] + _comm_sems(n),
        compiler_params=_cparams(("arbitrary",)),
        name="gla_layer_bwd",
    )(proj, proj, proj, proj, proj, proj, proj, wup, bias, wn, o, s_in, dog, *exchange)
    return res[:4], res[4:]


@jax.custom_vjp
def _expand(v):
    r = v.shape[0]
    left = _iota((r, 128), 1) < SSD_P
    slabs = []
    for p in range(SSD_HEADS // 2):
        a = jnp.broadcast_to(v[:, 2 * p:2 * p + 1], (r, 128))
        b = jnp.broadcast_to(v[:, 2 * p + 1:2 * p + 2], (r, 128))
        slabs.append(jnp.where(left, a, b))
    return jnp.concatenate(slabs, axis=1)


def _expand_fwd(v):
    return _expand(v), None


def _expand_bwd(_, g):
    r = g.shape[0]
    lane = _iota((r, 128), 1)
    left = lane < SSD_P
    dv = jnp.zeros((r, 128), F32)
    for p in range(SSD_HEADS // 2):
        gs = g[:, 128 * p:128 * (p + 1)]
        sa = jnp.sum(jnp.where(left, gs, 0.0), axis=-1, keepdims=True)
        sb = jnp.sum(jnp.where(left, 0.0, gs), axis=-1, keepdims=True)
        dv = dv + jnp.where(lane == 2 * p, sa, 0.0) + jnp.where(lane == 2 * p + 1, sb, 0.0)
    return (dv,)


_expand.defvjp(_expand_fwd, _expand_bwd)

SSD_GW = SSD_HPG * SSD_P
SSD_BC = SSD_GROUPS * SSD_N
SSD_PHASE = 4


def _ssd_chunk(xs, Bm, Cm, dtp, dtb, alog, dsk, h_in):
    dt = _softplus(dtp + dtb)
    acum = _cumsum_rows(dt * (-jnp.exp(alog)))
    a_last = acum[CHUNK - 1:CHUNK]
    acum_b = _expand(acum)
    w_end = _expand(dt * jnp.exp(a_last - acum))
    d_b = _expand(jnp.broadcast_to(dsk, (8, 128)))[0:1]
    ac_t = jnp.concatenate([acum, acum], axis=0).T
    dt_t = jnp.concatenate([dt, dt], axis=0).T
    lane = _iota((CHUNK, 128), 1)
    left = lane < SSD_P
    causal = (lane & (SSD_P - 1)) <= _iota((CHUNK, 128), 0)
    cd = jnp.exp(ac_t[:, CHUNK - 1:CHUNK])
    ys, h_out = [], []
    for g0 in range(0, SSD_GROUPS, SSD_PHASE):
        cb2, y_off = {}, {}
        for g in range(g0, g0 + SSD_PHASE):
            Bg = Bm[:, g * SSD_N:(g + 1) * SSD_N]
            Cg = Cm[:, g * SSD_N:(g + 1) * SSD_N]
            gs = slice(g * SSD_GW, (g + 1) * SSD_GW)
            cb2[g] = _mxu_nt(Cg, jnp.concatenate([Bg, Bg], axis=0))
            y_off[g] = _mxu_nt(Cg, h_in[gs])
            st = _mxu_tn(xs[:, gs] * w_end[:, gs], Bg)
            hs = [h_in[h * SSD_P:(h + 1) * SSD_P] * cd[h:h + 1] for h in range(g * SSD_HPG, (g + 1) * SSD_HPG)]
            h_out.append(jnp.concatenate(hs, axis=0) + st)
        for p in range(g0 * (SSD_HPG // 2), (g0 + SSD_PHASE) * (SSD_HPG // 2)):
            g, k = divmod(p, SSD_HPG // 2)
            sl = slice(128 * p, 128 * (p + 1))
            ac_c = acum_b[:, sl]
            ac_r = jnp.where(left, ac_t[2 * p:2 * p + 1], ac_t[2 * p + 1:2 * p + 2])
            dt_r = jnp.where(left, dt_t[2 * p:2 * p + 1], dt_t[2 * p + 1:2 * p + 2])
            m2 = cb2[g] * jnp.where(causal, jnp.exp(jnp.minimum(ac_c - ac_r, 0.0)), 0.0) * dt_r
            xsl = xs[:, sl]
            x2 = jnp.concatenate([jnp.where(left, xsl, 0.0), jnp.where(left, 0.0, xsl)], axis=0)
            ys.append(_mxu(m2, x2) + y_off[g][:, 128 * k:128 * (k + 1)] * jnp.exp(ac_c) + xsl * d_b[:, sl])
    return jnp.concatenate(ys, axis=1), jnp.concatenate(h_out, axis=0)


def _ssd_step(xs, Bm, Cm, dtp, dtb, alog, dsk, h_in, z, wn):
    ys, h = [], h_in
    for u in range(xs.shape[0] // CHUNK):
        r = slice(u * CHUNK, (u + 1) * CHUNK)
        y, h = _ssd_chunk(xs[r], Bm[r], Cm[r], dtp[r], dtb, alog, dsk, h)
        ys.append(_rms(y * _silu(z[r]), wn))
    return jnp.concatenate(ys, axis=0), h


def _ssd_specs(rows, nc, rev):
    ci = (lambda c: nc - 1 - c) if rev else (lambda c: c)
    vec = pl.BlockSpec((1, 128), lambda c: (0, 0))
    return [
        pl.BlockSpec((rows, D_INNER), lambda c: (ci(c), 0)),
        pl.BlockSpec((rows, SSD_BC), lambda c: (ci(c), D_INNER // SSD_BC)),
        pl.BlockSpec((rows, SSD_BC), lambda c: (ci(c), D_INNER // SSD_BC + 1)),
        pl.BlockSpec((rows, 128), lambda c: (ci(c), SSD_DT_COL // 128)),
        vec, vec, vec,
        pl.BlockSpec((rows, D_INNER), lambda c: (ci(c), 0)),
        pl.BlockSpec((1, D_INNER), lambda c: (0, 0)),
    ]


def _ssd_layer_fwd(xbc, proj, dtb, alog, dsk, wn):
    L = xbc.shape[0]
    sc = min(STEP_CHUNKS, L // CHUNK)
    rows = sc * CHUNK
    nc = L // rows

    def body(xs_ref, b_ref, c_ref, dt_ref, dtb_ref, alog_ref, dsk_ref, z_ref, wn_ref, y_ref, hs_ref, hst):
        @pl.when(pl.program_id(0) == 0)
        def _():
            hst[...] = jnp.zeros(hst.shape, F32)

        h = hst[...]
        for u in range(sc):
            r = slice(u * CHUNK, (u + 1) * CHUNK)
            hs_ref[u] = h
            yn, h = _ssd_step(xs_ref[r], b_ref[r], c_ref[r], dt_ref[r], dtb_ref[...], alog_ref[...], dsk_ref[...], h,
                              z_ref[r], wn_ref[...])
            y_ref[r] = yn.astype(BF16)
        hst[...] = h

    return pl.pallas_call(
        body,
        grid=(nc,),
        in_specs=_ssd_specs(rows, nc, False),
        out_specs=[
            pl.BlockSpec((rows, D_INNER), lambda c: (c, 0)),
            pl.BlockSpec((sc, D_INNER, SSD_N), lambda c: (c, 0, 0)),
        ],
        out_shape=[
            jax.ShapeDtypeStruct((L, D_INNER), BF16),
            jax.ShapeDtypeStruct((L // CHUNK, D_INNER, SSD_N), F32),
        ],
        scratch_shapes=[pltpu.VMEM((D_INNER, SSD_N), F32)],
        compiler_params=_cparams(("arbitrary",)),
        name="ssd_layer_fwd",
    )(xbc, xbc, xbc, proj, dtb, alog, dsk, proj, wn)


def _ssd_layer_bwd(xbc, proj, dtb, alog, dsk, wn, h_saved, dyn):
    L = xbc.shape[0]
    rows = CHUNK
    nc = L // rows

    def body(xs_ref, b_ref, c_ref, dt_ref, dtb_ref, alog_ref, dsk_ref, z_ref, wn_ref, hs_ref, dyn_ref,
             dp_ref, dx_ref, ddtb_ref, dalog_ref, ddsk_ref, dwn_ref, dhst):
        @pl.when(pl.program_id(0) == 0)
        def _():
            dhst[...] = jnp.zeros(dhst.shape, F32)
            ddtb_ref[...] = jnp.zeros((1, 128), F32)
            dalog_ref[...] = jnp.zeros((1, 128), F32)
            ddsk_ref[...] = jnp.zeros((1, 128), F32)
            dwn_ref[...] = jnp.zeros((1, D_INNER), F32)

        _, vjp = jax.vjp(_ssd_step, xs_ref[...], b_ref[...], c_ref[...], dt_ref[...], dtb_ref[...], alog_ref[...],
                         dsk_ref[...], hs_ref[...], z_ref[...], wn_ref[...])
        dxs, db, dc, ddt, ddtb, dalog, ddsk, dh, dz, dwn = vjp((dyn_ref[...], dhst[...]))
        dx_ref[:, :D_INNER] = dxs
        dx_ref[:, D_INNER:D_INNER + SSD_BC] = db
        dx_ref[:, D_INNER + SSD_BC:] = dc
        dp_ref[:, :D_INNER] = dz.astype(BF16)
        dp_ref[:, D_INNER:SSD_DT_COL] = jnp.zeros((rows, SSD_CONV_DIM), BF16)
        dp_ref[:, SSD_DT_COL:SSD_DT_COL + 128] = ddt.astype(BF16)
        dp_ref[:, SSD_DT_COL + 128:] = jnp.zeros((rows, SSD_PROJ_PAD - SSD_DT_COL - 128), BF16)
        dhst[...] = dh
        ddtb_ref[...] += ddtb
        dalog_ref[...] += dalog
        ddsk_ref[...] += ddsk
        dwn_ref[...] += dwn

    rc = lambda c: nc - 1 - c
    vec = pl.BlockSpec((1, 128), lambda c: (0, 0))
    vshape = jax.ShapeDtypeStruct((1, 128), F32)
    return pl.pallas_call(
        body,
        grid=(nc,),
        in_specs=_ssd_specs(rows, nc, True) + [
            pl.BlockSpec((None, D_INNER, SSD_N), lambda c: (rc(c), 0, 0)),
            pl.BlockSpec((rows, D_INNER), lambda c: (rc(c), 0)),
        ],
        out_specs=[
            pl.BlockSpec((rows, SSD_PROJ_PAD), lambda c: (rc(c), 0)),
            pl.BlockSpec((rows, SSD_CONV_DIM), lambda c: (rc(c), 0)),
            vec, vec, vec,
            pl.BlockSpec((1, D_INNER), lambda c: (0, 0)),
        ],
        out_shape=[
            jax.ShapeDtypeStruct((L, SSD_PROJ_PAD), BF16),
            jax.ShapeDtypeStruct((L, SSD_CONV_DIM), F32),
            vshape, vshape, vshape,
            jax.ShapeDtypeStruct((1, D_INNER), F32),
        ],
        scratch_shapes=[pltpu.VMEM((D_INNER, SSD_N), F32)],
        compiler_params=_cparams(("arbitrary",)),
        name="ssd_layer_bwd",
    )(xbc, xbc, xbc, proj, dtb, alog, dsk, proj, wn, h_saved, dyn)


def _pick(n, options):
    for t in options:
        if n % t == 0:
            return t
    return n


def _token_tile(m, row_bytes, fixed_bytes):
    for t in (MM_TOKEN_TILE, MM_TOKEN_TILE // 2, MM_TOKEN_TILE // 4):
        if m % t == 0 and t * row_bytes + fixed_bytes <= MM_VMEM_BUDGET:
            return t
    return min(m, MM_TOKEN_TILE // 4)


def _mm(a, b, *, name, out_dtype=F32, add=None, exchange=()):
    M, K = a.shape
    N = b.shape[1]
    tn = _pick(N, MM_TILES)
    tk = _pick(K, MM_TILES)
    nk = K // tk
    row_bytes = 2 * (tk * a.dtype.itemsize + tn * jnp.dtype(out_dtype).itemsize + (tn * 4 if add is not None else 0)) \
        + (tn * 4 if nk > 1 else 0)
    tm = _token_tile(M, row_bytes, 2 * tk * tn * b.dtype.itemsize)
    grid = (M // tm, N // tn, nk)
    n = len(exchange)
    n_in = 2 + (add is not None)

    def body(*refs):
        a_ref, b_ref = refs[:2]
        add_ref = refs[2] if add is not None else None
        p_refs = refs[n_in:n_in + n]
        o_ref = refs[n_in + n]
        out_refs = refs[n_in + n + 1:n_in + 2 * n + 1]
        acc = refs[n_in + 2 * n + 1]
        ids = [pl.program_id(d) for d in range(3)]
        k = ids[2]
        if n:
            start, finish_exchange = _exchange_ops(p_refs, out_refs, *refs[n_in + 2 * n + 2:])

            @pl.when((ids[0] == 0) & (ids[1] == 0) & (k == 0))
            def _():
                start()

        p = _dot(_bf(a_ref[...]), _bf(b_ref[...]))

        def finish(r):
            if add is not None:
                r = r + add_ref[...]
            o_ref[...] = r.astype(out_dtype)

        if nk == 1:
            finish(p)
        else:
            @pl.when(k == 0)
            def _():
                acc[...] = p

            @pl.when((k > 0) & (k < nk - 1))
            def _():
                acc[...] += p

            @pl.when(k == nk - 1)
            def _():
                finish(acc[...] + p)

        if n:
            @pl.when((ids[0] == grid[0] - 1) & (ids[1] == grid[1] - 1) & (k == nk - 1))
            def _():
                finish_exchange()

    in_specs = [pl.BlockSpec((tm, tk), lambda i, j, k: (i, k)), pl.BlockSpec((tk, tn), lambda i, j, k: (k, j))]
    args = [a, b]
    if add is not None:
        in_specs.append(pl.BlockSpec((tm, tn), lambda i, j, k: (i, j)))
        args.append(add)
    res = pl.pallas_call(
        body,
        grid=grid,
        in_specs=in_specs + [ANY] * n,
        out_specs=[pl.BlockSpec((tm, tn), lambda i, j, k: (i, j))] + [ANY] * n,
        out_shape=[jax.ShapeDtypeStruct((M, N), out_dtype)] + [jax.ShapeDtypeStruct(p.shape, p.dtype) for p in exchange],
        scratch_shapes=[pltpu.VMEM((tm, tn) if nk > 1 else (8, 128), F32)] + (_comm_sems(n) if n else []),
        compiler_params=_cparams(("arbitrary",) * 3 if n else ("parallel", "parallel", "arbitrary")),
        name=name,
    )(*args, *exchange)
    return (res[0], res[1:]) if n else res[0]


def _mm_tn(a, b, *, name):
    M, K = a.shape
    N = b.shape[1]
    tn = _pick(N, MM_TILES)
    tm = _token_tile(M, 2 * (K * a.dtype.itemsize + tn * b.dtype.itemsize), 2 * K * tn * 4)

    def body(a_ref, b_ref, o_ref):
        _acc_out(o_ref, _dot_tn(_bf(a_ref[...]), _bf(b_ref[...])), pl.program_id(1) == 0)

    return pl.pallas_call(
        body,
        grid=(N // tn, M // tm),
        in_specs=[pl.BlockSpec((tm, K), lambda j, i: (i, 0)), pl.BlockSpec((tm, tn), lambda j, i: (i, j))],
        out_specs=pl.BlockSpec((K, tn), lambda j, i: (0, j)),
        out_shape=jax.ShapeDtypeStruct((K, N), F32),
        compiler_params=_cparams(("parallel", "arbitrary")),
        name=name,
    )(a, b)


def _rms_fwd(x, w, *, name):
    L, D = x.shape
    tm = min(TOKEN_TILE, L)

    def body(x_ref, w_ref, o_ref):
        o_ref[...] = _rms(x_ref[...], w_ref[...]).astype(BF16)

    return pl.pallas_call(
        body, grid=(L // tm,),
        in_specs=[pl.BlockSpec((tm, D), lambda i: (i, 0)), pl.BlockSpec((1, D), lambda i: (0, 0))],
        out_specs=pl.BlockSpec((tm, D), lambda i: (i, 0)),
        out_shape=jax.ShapeDtypeStruct((L, D), BF16),
        compiler_params=_cparams(("parallel",)), name=name,
    )(x, w)


def _rms_bwd(x, w, dhn, dres, *, name):
    L, D = x.shape
    tm = min(TOKEN_TILE, L)

    def body(x_ref, w_ref, dhn_ref, dres_ref, dx_ref, dw_ref):
        _, vjp = jax.vjp(_rms, x_ref[...], w_ref[...])
        dx, dw = vjp(dhn_ref[...])
        dx_ref[...] = dx + dres_ref[...]
        _acc_out(dw_ref, dw, pl.program_id(0) == 0)

    row = pl.BlockSpec((tm, D), lambda i: (i, 0))
    vec = pl.BlockSpec((1, D), lambda i: (0, 0))
    return pl.pallas_call(
        body, grid=(L // tm,),
        in_specs=[row, vec, row, row],
        out_specs=[row, vec],
        out_shape=[jax.ShapeDtypeStruct((L, D), F32), jax.ShapeDtypeStruct((1, D), F32)],
        compiler_params=_cparams(("arbitrary",)), name=name,
    )(x, w, dhn, dres)


CONV_HALO = 8
CONV_COLS = 1024


CONV_RB = 64
CONV_CB = 256


def _conv_pieces(tm):
    return [(r0, c0) for c0 in range(0, CONV_COLS, CONV_CB) for r0 in range(0, tm, min(CONV_RB, tm))]


def _conv_fwd(proj, w, b):
    L = proj.shape[0]
    tm = min(TOKEN_TILE, L)
    rb = min(CONV_RB, tm)
    c0 = D_INNER // CONV_COLS
    hb = tm // CONV_HALO

    def body(x_ref, h_ref, w_ref, b_ref, o_ref, sp_ref, xx):
        xx[0:CONV_HALO] = jnp.where(pl.program_id(1) == 0, 0.0, h_ref[...])
        xx[CONV_HALO:CONV_HALO + tm] = x_ref[...]
        for r0, cc in _conv_pieces(tm):
            cs = slice(cc, cc + CONV_CB)
            u = b_ref[:, cs]
            for k in range(SSD_CONV):
                off = CONV_HALO - (SSD_CONV - 1) + k + r0
                u = u + w_ref[k:k + 1, cs] * xx[off:off + rb, cs]
            s = 1.0 / (1.0 + jnp.exp(-u))
            o_ref[r0:r0 + rb, cs] = u * s
            sp_ref[r0:r0 + rb, cs] = s * (1.0 + u * (1.0 - s))

    blk = pl.BlockSpec((tm, CONV_COLS), lambda j, i: (i, j))
    shp = jax.ShapeDtypeStruct((L, SSD_CONV_DIM), F32)
    return pl.pallas_call(
        body, grid=(SSD_CONV_DIM // CONV_COLS, L // tm),
        in_specs=[
            pl.BlockSpec((tm, CONV_COLS), lambda j, i: (i, c0 + j)),
            pl.BlockSpec((CONV_HALO, CONV_COLS), lambda j, i: (jnp.maximum(i * hb - 1, 0), c0 + j)),
            pl.BlockSpec((SSD_CONV, CONV_COLS), lambda j, i: (0, j)),
            pl.BlockSpec((1, CONV_COLS), lambda j, i: (0, j)),
        ],
        out_specs=[blk, blk],
        out_shape=[shp, shp],
        scratch_shapes=[pltpu.VMEM((CONV_HALO + tm, CONV_COLS), F32)],
        compiler_params=_cparams(("parallel", "parallel")), name="conv_fwd",
    )(proj, proj, w, b)


def _conv_bwd(proj, w, sp, dxbc, dproj):
    L = proj.shape[0]
    tm = min(TOKEN_TILE, L)
    rb = min(CONV_RB, tm)
    nt = L // tm
    c0 = D_INNER // CONV_COLS
    hb = tm // CONV_HALO

    def fold(a):
        return jnp.sum(a.reshape(rb // 8, 8, CONV_CB), axis=0)

    def body(x_ref, h_ref, w_ref, sp_ref, dy_ref, dp_in_ref, dp_ref, dw_ref, db_ref, xx, dd):
        del dp_in_ref
        i = pl.program_id(1)
        first = i == 0

        @pl.when(first)
        def _():
            dd[tm:tm + CONV_HALO] = jnp.zeros((CONV_HALO, CONV_COLS), F32)

        xx[0:CONV_HALO] = jnp.where(i == nt - 1, 0.0, h_ref[...])
        xx[CONV_HALO:CONV_HALO + tm] = x_ref[...]
        dws, dbs = [], []
        for cc in range(0, CONV_COLS, CONV_CB):
            cs = slice(cc, cc + CONV_CB)
            acc = [jnp.zeros((8, CONV_CB), F32) for _ in range(SSD_CONV + 1)]
            for r0 in range(0, tm, rb):
                du = dy_ref[r0:r0 + rb, cs] * sp_ref[r0:r0 + rb, cs]
                dd[r0:r0 + rb, cs] = du
                for k in range(SSD_CONV):
                    off = CONV_HALO - (SSD_CONV - 1) + k + r0
                    acc[k] = acc[k] + fold(du * xx[off:off + rb, cs])
                acc[SSD_CONV] = acc[SSD_CONV] + fold(du)
            dws.append(jnp.concatenate([jnp.sum(a, axis=0, keepdims=True) for a in acc[:SSD_CONV]], axis=0))
            dbs.append(jnp.sum(acc[SSD_CONV], axis=0, keepdims=True))
        for r0, cc in _conv_pieces(tm):
            cs = slice(cc, cc + CONV_CB)
            dx = jnp.zeros((rb, CONV_CB), F32)
            for k in range(SSD_CONV):
                off = SSD_CONV - 1 - k + r0
                dx = dx + w_ref[k:k + 1, cs] * dd[off:off + rb, cs]
            dp_ref[r0:r0 + rb, cs] = dx.astype(BF16)
        _acc_out(dw_ref, jnp.concatenate(dws, axis=1), first)
        _acc_out(db_ref, jnp.concatenate(dbs, axis=1), first)
        dd[tm:tm + CONV_HALO] = dd[0:CONV_HALO]

    rt = lambda i: nt - 1 - i
    return pl.pallas_call(
        body, grid=(SSD_CONV_DIM // CONV_COLS, nt),
        in_specs=[
            pl.BlockSpec((tm, CONV_COLS), lambda j, i: (rt(i), c0 + j)),
            pl.BlockSpec((CONV_HALO, CONV_COLS), lambda j, i: (jnp.maximum(rt(i) * hb - 1, 0), c0 + j)),
            pl.BlockSpec((SSD_CONV, CONV_COLS), lambda j, i: (0, j)),
            pl.BlockSpec((tm, CONV_COLS), lambda j, i: (rt(i), j)),
            pl.BlockSpec((tm, CONV_COLS), lambda j, i: (rt(i), j)),
            pl.BlockSpec(memory_space=pl.ANY),
        ],
        out_specs=[
            pl.BlockSpec((tm, CONV_COLS), lambda j, i: (rt(i), c0 + j)),
            pl.BlockSpec((SSD_CONV, CONV_COLS), lambda j, i: (0, j)),
            pl.BlockSpec((1, CONV_COLS), lambda j, i: (0, j)),
        ],
        out_shape=[jax.ShapeDtypeStruct((L, SSD_PROJ_PAD), BF16), jax.ShapeDtypeStruct((SSD_CONV, SSD_CONV_DIM), F32),
                   jax.ShapeDtypeStruct((1, SSD_CONV_DIM), F32)],
        scratch_shapes=[pltpu.VMEM((CONV_HALO + tm, CONV_COLS), F32), pltpu.VMEM((tm + CONV_HALO, CONV_COLS), F32)],
        input_output_aliases={5: 0},
        compiler_params=_cparams(("arbitrary", "arbitrary")), name="conv_bwd",
    )(proj, proj, w, sp, dxbc, dproj)


def _loss_bwd(x, tgt, w):
    L, D = x.shape
    tm = min(TOKEN_TILE, L)

    def body(x_ref, t_ref, w_ref, l_ref, dx_ref, dw_ref):
        xv = x_ref[...]
        wv = w_ref[...]
        r = lax.rsqrt(jnp.mean(xv * xv, axis=-1, keepdims=True) + RMS_EPS)
        xh = xv * r
        e = xh * wv - t_ref[...]
        lsum = 0.5 * jnp.sum(jnp.mean(e * e, axis=-1, keepdims=True), axis=0, keepdims=True)
        dout = e * (1.0 / D)
        gx = dout * wv
        dx_ref[...] = r * (gx - xh * jnp.mean(gx * xh, axis=-1, keepdims=True))
        first = pl.program_id(0) == 0
        _acc_out(dw_ref, jnp.sum(dout * xh, axis=0, keepdims=True), first)
        _acc_out(l_ref, jnp.broadcast_to(lsum, (8, 128)), first)

    row = pl.BlockSpec((tm, D), lambda i: (i, 0))
    vec = pl.BlockSpec((1, D), lambda i: (0, 0))
    return pl.pallas_call(
        body, grid=(L // tm,),
        in_specs=[row, row, vec],
        out_specs=[pl.BlockSpec((8, 128), lambda i: (0, 0)), row, vec],
        out_shape=[jax.ShapeDtypeStruct((8, 128), F32), jax.ShapeDtypeStruct((L, D), F32), jax.ShapeDtypeStruct((1, D), F32)],
        compiler_params=_cparams(("arbitrary",)), name="loss_bwd",
    )(x, tgt, w)


MESH = pl.DeviceIdType.MESH
ANY = pl.BlockSpec(memory_space=pl.ANY)


def _comm_sems(n):
    return [pltpu.SemaphoreType.DMA((n, 7)), pltpu.SemaphoreType.DMA((n, 7)), pltpu.SemaphoreType.DMA((n,))]


def _gather_ops(x_refs, out_refs, send_sems, recv_sems, local_sems):
    n = len(x_refs)
    x, y, c = lax.axis_index("x"), lax.axis_index("y"), lax.axis_index("c")
    me, sibling = (x, y, c), (x, y, 1 - c)
    chips = [(1 - x, y), (x, 1 - y), (1 - x, 1 - y)]

    def slot(i, px, py, pc):
        return out_refs[i].at[4 * px + 2 * py + pc]

    def copy(i, k, block, to, src=None):
        return pltpu.make_async_remote_copy(
            src_ref=slot(i, *block) if src is None else src, dst_ref=slot(i, *block),
            send_sem=send_sems.at[i, k], recv_sem=recv_sems.at[i, k], device_id=to, device_id_type=MESH)

    def own():
        mine = [pltpu.make_async_copy(x_refs[i], slot(i, *me), local_sems.at[i]) for i in range(n)]
        first = [copy(i, 0, me, sibling, src=x_refs[i]) for i in range(n)]
        first += [copy(i, 1 + j, me, (*chip, c), src=x_refs[i]) for j, chip in enumerate(chips) for i in range(n)]
        return mine, first

    def start():
        mine, first = own()
        for cp in mine + first:
            cp.start()

    def finish():
        mine, first = own()
        passed = []
        for j, chip in enumerate(chips):
            for i in range(n):
                copy(i, 1 + j, (*chip, c), me).wait_recv()
                passed.append(copy(i, 4 + j, (*chip, c), sibling))
                passed[-1].start()
        for i in range(n):
            copy(i, 0, sibling, me).wait_recv()
        for j, chip in enumerate(chips):
            for i in range(n):
                copy(i, 4 + j, (*chip, 1 - c), me).wait_recv()
        for cp in first + passed:
            cp.wait_send()
        for cp in mine:
            cp.wait()

    return start, finish


def _exchange_ops(p_refs, out_refs, send_sems, recv_sems, local_sems):
    n = len(p_refs)
    x, y, c = lax.axis_index("x"), lax.axis_index("y"), lax.axis_index("c")
    my = 4 * x + 2 * y + c

    def peer(k):
        fx, fy, fc = (k >> 2) & 1, (k >> 1) & 1, k & 1
        px, py, pc = (1 - x if fx else x), (1 - y if fy else y), (1 - c if fc else c)
        return (px, py, pc), 4 * px + 2 * py + pc

    def mine():
        return [pltpu.make_async_copy(p_refs[i].at[my], out_refs[i].at[my], local_sems.at[i]) for i in range(n)]

    def start():
        for cp in mine():
            cp.start()
        for k in range(1, N_DEV):
            to, pid = peer(k)
            for i in range(n):
                pltpu.make_async_remote_copy(
                    src_ref=p_refs[i].at[pid], dst_ref=out_refs[i].at[my], send_sem=send_sems.at[i, k - 1],
                    recv_sem=recv_sems.at[i, k - 1], device_id=to, device_id_type=MESH).start()

    def finish():
        for k in range(1, N_DEV):
            to, pid = peer(k)
            for i in range(n):
                pltpu.make_async_remote_copy(
                    src_ref=p_refs[i].at[pid], dst_ref=out_refs[i].at[pid], send_sem=send_sems.at[i, k - 1],
                    recv_sem=recv_sems.at[i, k - 1], device_id=to, device_id_type=MESH).wait()
        for cp in mine():
            cp.wait()

    return start, finish


def _all_gather(xs, *, name):
    n = len(xs)

    def body(*refs):
        start, finish = _gather_ops(refs[:n], refs[n:2 * n], *refs[2 * n:])
        start()
        finish()

    return pl.pallas_call(
        body,
        out_shape=[jax.ShapeDtypeStruct((N_DEV,) + a.shape, a.dtype) for a in xs],
        in_specs=[ANY] * n, out_specs=[ANY] * n, scratch_shapes=_comm_sems(n), name=name,
    )(*xs)


def _adamw(parts, w, m, v, *, name):
    a, b = w.shape
    tr = _pick(a, (256, 128, 64, 32, 16, 8))

    def body(p_ref, w_ref, m_ref, v_ref, g_ref, d_ref, mo_ref, vo_ref):
        g = p_ref[0]
        for s in range(1, N_DEV):
            g = g + p_ref[s]
        mn = ADAM_B1 * m_ref[...] + (1.0 - ADAM_B1) * g
        vn = ADAM_B2 * v_ref[...] + (1.0 - ADAM_B2) * jnp.square(g)
        m_hat = mn / (1.0 - ADAM_B1 ** ADAM_STEP)
        v_hat = vn / (1.0 - ADAM_B2 ** ADAM_STEP)
        g_ref[...] = g
        d_ref[...] = -ADAM_LR * (m_hat / (jnp.sqrt(v_hat) + ADAM_EPS) + ADAM_WD * w_ref[...])
        mo_ref[...] = mn
        vo_ref[...] = vn

    blk = pl.BlockSpec((tr, b), lambda i: (i, 0))
    shp = jax.ShapeDtypeStruct((a, b), F32)
    return pl.pallas_call(
        body, grid=(a // tr,),
        in_specs=[pl.BlockSpec((N_DEV, tr, b), lambda i: (0, i, 0)), blk, blk, blk],
        out_specs=[blk, blk, blk, blk],
        out_shape=[shp, shp, shp, shp],
        compiler_params=_cparams(("parallel",)), name=name,
    )(parts, w, m, v)


def _col_shards(a, n):
    return a.reshape(a.shape[0], N_DEV, n).transpose(1, 0, 2)


def _from_col_shards(g, cols):
    r = g.shape[1]
    full = g.transpose(1, 0, 2).reshape(r, -1)
    return jnp.pad(full, ((0, 0), (0, cols - full.shape[1])))


def kernel(x, norm_w, gla_in_proj, gla_gate_up, gla_gate_bias, gla_head_norm, gla_out_proj, ssd_in_proj, ssd_conv_w, ssd_conv_b, ssd_dt_bias, ssd_a_log, ssd_d, ssd_gate_norm, ssd_out_proj, final_norm, loss_target, m_norm_w, m_gla_in_proj, m_gla_gate_up, m_gla_gate_bias, m_gla_head_norm, m_gla_out_proj, m_ssd_in_proj, m_ssd_conv_w, m_ssd_conv_b, m_ssd_dt_bias, m_ssd_a_log, m_ssd_d, m_ssd_gate_norm, m_ssd_out_proj, m_final_norm, v_norm_w, v_gla_in_proj, v_gla_gate_up, v_gla_gate_bias, v_gla_head_norm, v_gla_out_proj, v_ssd_in_proj, v_ssd_conv_w, v_ssd_conv_b, v_ssd_dt_bias, v_ssd_a_log, v_ssd_d, v_ssd_gate_norm, v_ssd_out_proj, v_final_norm):
    x0 = x[0]
    tgt = loss_target[0]
    n_gin = GLA_PROJ // N_DEV
    n_sin = SSD_PROJ // N_DEV
    n_up = GLA_DK // N_DEV
    n_cv = SSD_CONV_DIM // N_DEV

    g_gin, g_up, g_gout = _all_gather(
        [gla_in_proj[0].astype(BF16), gla_gate_up[0].astype(BF16), gla_out_proj[0].astype(BF16)], name="gather_weights")
    w_gin = _from_col_shards(g_gin, GLA_PROJ_PAD)
    wup = jnp.pad(_from_col_shards(g_up, GLA_DK), ((0, 128 - GLA_RANK), (0, 0))).astype(F32)
    w_gout = g_gout.reshape(D_INNER, D_MODEL)
    vec128 = lambda a: jnp.pad(a.reshape(1, -1), ((0, 0), (0, 128 - a.size)))
    dtb, alog, dsk = vec128(ssd_dt_bias), vec128(ssd_a_log), vec128(ssd_d)
    nw0, nw1 = norm_w[0:1], norm_w[1:2]

    hn1 = _rms_fwd(x0, nw0, name="rms1_fwd")
    proj1 = _mm(hn1, w_gin, name="gla_in_proj")
    (o, og, s_saved), (g_sin, g_sout, g_cw, g_cb, g_gn) = _gla_layer_fwd(
        proj1, wup, gla_gate_bias, gla_head_norm,
        [ssd_in_proj[0].astype(BF16), ssd_out_proj[0].astype(BF16), ssd_conv_w[0], ssd_conv_b, ssd_gate_norm])
    w_sin = _from_col_shards(g_sin, SSD_PROJ_PAD)
    w_sout = g_sout.reshape(D_INNER, D_MODEL)
    conv_w = _from_col_shards(g_cw, SSD_CONV_DIM)
    conv_b = g_cb.reshape(1, SSD_CONV_DIM)
    gate_norm = g_gn.reshape(1, D_INNER)
    x1 = _mm(og, w_gout, add=x0, name="gla_out_proj")
    hn2 = _rms_fwd(x1, nw1, name="rms2_fwd")
    proj2 = _mm(hn2, w_sin, name="ssd_in_proj")
    xbc, conv_sp = _conv_fwd(proj2, conv_w, conv_b)
    yn, h_saved = _ssd_layer_fwd(xbc, proj2, dtb, alog, dsk, gate_norm)
    x2 = _mm(yn, w_sout, add=x1, name="ssd_out_proj")
    lsum, dx2, d_final = _loss_bwd(x2, tgt, final_norm.reshape(1, D_MODEL))
    loss = lax.psum(lsum[0, 0], ("x", "y", "c"))

    d_sout = _mm_tn(yn, dx2, name="ssd_out_proj_dw")
    dyn = _mm(dx2, w_sout.T, name="ssd_out_proj_dx")
    dproj2, dxbc, d_dtb, d_alog, d_dsk, d_gate_norm = _ssd_layer_bwd(xbc, proj2, dtb, alog, dsk, gate_norm, h_saved, dyn)
    dproj2, d_conv_w, d_conv_b = _conv_bwd(proj2, conv_w, conv_sp, dxbc, dproj2)
    d_sin = _mm_tn(hn2, dproj2, name="ssd_in_proj_dw")
    dhn2 = _mm(dproj2, w_sin.T, name="ssd_in_proj_dx")
    dx1, d_nw1 = _rms_bwd(x1, nw1, dhn2, dx2, name="rms2_bwd")
    d_gout = _mm_tn(og, dx1, name="gla_out_proj_dw")
    dog = _mm(dx1, w_gout.T, name="gla_out_proj_dx")
    early = {
        "gla_out_proj": ((gla_out_proj[0], m_gla_out_proj[0], v_gla_out_proj[0]), d_gout.reshape(N_DEV, -1, D_MODEL)),
        "ssd_in_proj": ((ssd_in_proj[0], m_ssd_in_proj[0], v_ssd_in_proj[0]), _col_shards(d_sin[:, :SSD_PROJ], n_sin)),
        "ssd_conv_w": ((ssd_conv_w[0], m_ssd_conv_w[0], v_ssd_conv_w[0]), _col_shards(d_conv_w, n_cv)),
        "ssd_conv_b": ((ssd_conv_b, m_ssd_conv_b, v_ssd_conv_b), d_conv_b.reshape(N_DEV, 1, n_cv)),
        "ssd_gate_norm": ((ssd_gate_norm, m_ssd_gate_norm, v_ssd_gate_norm), d_gate_norm.reshape(N_DEV, 1, -1)),
        "ssd_out_proj": ((ssd_out_proj[0], m_ssd_out_proj[0], v_ssd_out_proj[0]), d_sout.reshape(N_DEV, -1, D_MODEL)),
    }
    (dproj1, d_wup, d_gbias, d_head_norm), early_recv = _gla_layer_bwd(
        proj1, wup, gla_gate_bias, gla_head_norm, o, s_saved, dog, [p for _, p in early.values()])
    d_gin = _mm_tn(hn1, dproj1, name="gla_in_proj_dw")
    late = {
        "gla_in_proj": ((gla_in_proj[0], m_gla_in_proj[0], v_gla_in_proj[0]), _col_shards(d_gin[:, :GLA_PROJ], n_gin)),
        "gla_gate_up": ((gla_gate_up[0], m_gla_gate_up[0], v_gla_gate_up[0]), _col_shards(d_wup[:GLA_RANK], n_up)),
    }
    dhn1, late_recv = _mm(dproj1, w_gin.T, name="gla_in_proj_dx", exchange=[p for _, p in late.values()])
    dx0, d_nw0 = _rms_bwd(x0, nw0, dhn1, dx1, name="rms1_bwd")
    heads = SSD_HEADS
    replicated = {
        "norm_w": ((norm_w, m_norm_w, v_norm_w), jnp.concatenate([d_nw0, d_nw1], axis=0)),
        "gla_gate_bias": ((gla_gate_bias, m_gla_gate_bias, v_gla_gate_bias), d_gbias),
        "gla_head_norm": ((gla_head_norm, m_gla_head_norm, v_gla_head_norm), d_head_norm),
        "ssd_dt_bias": ((ssd_dt_bias, m_ssd_dt_bias, v_ssd_dt_bias), d_dtb[:, :heads]),
        "ssd_a_log": ((ssd_a_log, m_ssd_a_log, v_ssd_a_log), d_alog[:, :heads]),
        "ssd_d": ((ssd_d, m_ssd_d, v_ssd_d), d_dsk[:, :heads]),
        "final_norm": (tuple(t.reshape(1, D_MODEL) for t in (final_norm, m_final_norm, v_final_norm)), d_final),
    }
    results = {}
    for group, recv in ((early, early_recv),
                        (late, late_recv),
                        (replicated, _all_gather([p for _, p in replicated.values()], name="replicated_gather"))):
        for (nm, ((w, m, v), _)), r in zip(group.items(), recv):
            results[nm] = _adamw(r, w, m, v, name=nm + "_adamw")

    order = [("norm_w", norm_w), ("gla_in_proj", gla_in_proj), ("gla_gate_up", gla_gate_up), ("gla_gate_bias", gla_gate_bias),
             ("gla_head_norm", gla_head_norm), ("gla_out_proj", gla_out_proj), ("ssd_in_proj", ssd_in_proj),
             ("ssd_conv_w", ssd_conv_w), ("ssd_conv_b", ssd_conv_b), ("ssd_dt_bias", ssd_dt_bias), ("ssd_a_log", ssd_a_log),
             ("ssd_d", ssd_d), ("ssd_gate_norm", ssd_gate_norm), ("ssd_out_proj", ssd_out_proj), ("final_norm", final_norm)]
    out = [loss, dx0[None]]
    for i in range(4):
        out += [results[nm][i].reshape(ref.shape) for nm, ref in order]
    return tuple(out)
```

```python
import jax
import jax.numpy as jnp
from jax import lax
from jax.experimental import pallas as pl
from jax.experimental.pallas import tpu as pltpu

F32 = jnp.float32
BF16 = jnp.bfloat16

D_MODEL = 1024
D_INNER = 2048
RMS_EPS = 1e-6
GLA_HEADS = 4
GLA_DK = 512
GLA_HEAD_K = 128
GLA_HEAD_V = 512
GLA_RANK = 16
GLA_NORMALIZER = 16.0
CHUNK = 64
SUB = 16
STEP_CHUNKS = 4
BWD_STEP_CHUNKS = 1
GLA_PROJ = 5136
GLA_PROJ_PAD = 5376
GLA_GK_COL = 5120
SSD_HEADS = 32
SSD_GROUPS = 8
SSD_HPG = 4
SSD_P = 64
SSD_N = 128
SSD_CONV = 4
SSD_CONV_DIM = 4096
SSD_PROJ = 6176
SSD_PROJ_PAD = 6400
SSD_DT_COL = 6144
N_DEV = 8

ADAM_LR = 0.001
ADAM_B1 = 0.9
ADAM_B2 = 0.999
ADAM_EPS = 1e-08
ADAM_WD = 0.01
ADAM_STEP = 10

VMEM_LIMIT = 56 * 1024 * 1024
TOKEN_TILE = 512
MM_TOKEN_TILE = 2048
MM_VMEM_BUDGET = 44 * 1024 * 1024
MM_TILES = (1792, 1280, 1024, 768, 512, 256, 128)


def _dot(a, b):
    return jnp.dot(a, b, preferred_element_type=F32)


def _dot_nt(a, b):
    return lax.dot_general(a, b, (((1,), (1,)), ((), ())), preferred_element_type=F32)


def _dot_tn(a, b):
    return lax.dot_general(a, b, (((0,), (0,)), ((), ())), preferred_element_type=F32)


def _bf(a):
    return a.astype(BF16)


@jax.custom_vjp
def _mxu(a, b):
    return _dot(_bf(a), _bf(b))


def _mxu_fwd(a, b):
    return _mxu(a, b), (a, b)


def _mxu_bwd(res, g):
    a, b = res
    return _dot_nt(_bf(g), _bf(b)), _dot_tn(_bf(a), _bf(g))


_mxu.defvjp(_mxu_fwd, _mxu_bwd)


@jax.custom_vjp
def _mxu_nt(a, b):
    return _dot_nt(_bf(a), _bf(b))


def _mxu_nt_fwd(a, b):
    return _mxu_nt(a, b), (a, b)


def _mxu_nt_bwd(res, g):
    a, b = res
    return _dot(_bf(g), _bf(b)), _dot_tn(_bf(g), _bf(a))


_mxu_nt.defvjp(_mxu_nt_fwd, _mxu_nt_bwd)


@jax.custom_vjp
def _mxu_tn(a, b):
    return _dot_tn(_bf(a), _bf(b))


def _mxu_tn_fwd(a, b):
    return _mxu_tn(a, b), (a, b)


def _mxu_tn_bwd(res, g):
    a, b = res
    return _dot_nt(_bf(b), _bf(g)), _dot(_bf(a), _bf(g))


_mxu_tn.defvjp(_mxu_tn_fwd, _mxu_tn_bwd)


def _split2(a):
    hi = _bf(a)
    return hi, _bf(a - hi.astype(F32))


def _three_pass(dot, a, b):
    ah, al = _split2(a)
    bh, bl = _split2(b)
    return dot(ah, bh) + (dot(ah, bl) + dot(al, bh))


@jax.custom_vjp
def _dot3_nt(a, b):
    return _three_pass(_dot_nt, a, b)


def _dot3_nt_fwd(a, b):
    return _dot3_nt(a, b), (a, b)


def _dot3_nt_bwd(res, g):
    a, b = res
    return _three_pass(_dot, g, b), _three_pass(_dot_tn, g, a)


_dot3_nt.defvjp(_dot3_nt_fwd, _dot3_nt_bwd)


def _silu(x):
    return x / (1.0 + jnp.exp(-x))


def _log_sigmoid(z):
    return jnp.minimum(z, 0.0) - jnp.log(1.0 + jnp.exp(-jnp.abs(z)))


def _softplus(z):
    return jnp.maximum(z, 0.0) + jnp.log(1.0 + jnp.exp(-jnp.abs(z)))


def _iota(shape, dim):
    return lax.broadcasted_iota(jnp.int32, shape, dim)


def _rms(x, w):
    return x * lax.rsqrt(jnp.mean(x * x, axis=-1, keepdims=True) + RMS_EPS) * w


def _scan_rows(a, reverse, seg):
    n = a.shape[0]
    pos = _iota(a.shape, 0) & (seg - 1)
    sh = 1
    while sh < seg:
        if reverse:
            a = a + jnp.where(pos < seg - sh, pltpu.roll(a, n - sh, 0), 0.0)
        else:
            a = a + jnp.where(pos >= sh, pltpu.roll(a, sh, 0), 0.0)
        sh *= 2
    return a


def _make_cumsum(seg):
    @jax.custom_vjp
    def cumsum(a):
        return _scan_rows(a, False, seg)

    cumsum.defvjp(lambda a: (_scan_rows(a, False, seg), None), lambda _, g: (_scan_rows(g, True, seg),))
    return cumsum


_cumsum_sub = _make_cumsum(SUB)
_cumsum_rows = _make_cumsum(CHUNK)


def _cparams(sem):
    return pltpu.CompilerParams(dimension_semantics=sem, vmem_limit_bytes=VMEM_LIMIT)


def _acc_out(ref, val, first):
    @pl.when(first)
    def _():
        ref[...] = val

    @pl.when(jnp.logical_not(first))
    def _():
        ref[...] += val


def _gla_chunk(q, k, va, vb, gk, wup, bias, sts):
    nb = CHUNK // SUB
    heads = range(GLA_HEADS)
    hc = lambda a, h: a[:, h * GLA_HEAD_K:(h + 1) * GLA_HEAD_K]
    v = [(va if h < 2 else vb)[:, (h % 2) * GLA_HEAD_V:(h % 2 + 1) * GLA_HEAD_V] for h in heads]
    z = _mxu(gk, wup) + bias
    la = _log_sigmoid(z) * (1.0 / GLA_NORMALIZER)
    qs = q * (GLA_HEAD_K ** -0.5)
    bl = _cumsum_sub(la)
    tot = [jnp.sum(la[i * SUB:(i + 1) * SUB], axis=0, keepdims=True) for i in range(nb)]
    pre = [jnp.zeros((1, GLA_DK), F32)]
    for i in range(nb):
        pre.append(pre[i] + tot[i])
    b_last = pre[nb]
    rows_of = lambda vals: jnp.concatenate([jnp.broadcast_to(t, (SUB, GLA_DK)) for t in vals], axis=0)
    suf = rows_of(tot) - bl
    nxt = rows_of(pre[1:])
    q_in = qs * jnp.exp(bl + rows_of(pre[:nb]))
    k_end = k * jnp.exp(suf + (b_last - nxt))
    dec = jnp.exp(b_last)
    qa = qs * jnp.exp(bl)
    o_inter = [_mxu_nt(hc(q_in, h), sts[h]) for h in heads]
    sts_new = tuple(sts[h] * hc(dec, h) + _mxu_tn(v[h], hc(k_end, h)) for h in heads)
    half = SUB // 2
    rs = _iota((SUB, GLA_HEAD_K), 0)
    cs = _iota((half, CHUNK), 1)
    a_rows = [[] for _ in heads]
    for i in range(nb):
        sl = slice(i * SUB, (i + 1) * SUB)
        n = i * SUB
        if i > 0:
            kp = jnp.concatenate([k[:n] * jnp.exp(suf[:n] + (pre[i] - nxt[:n])), jnp.zeros((CHUNK - n, GLA_DK), F32)], axis=0)
        for h in heads:
            q_i, k_i, bl_i = hc(qs[sl], h), hc(k[sl], h), hc(bl[sl], h)
            if i > 0:
                a_i = _dot3_nt(hc(qa[sl], h), hc(kp, h))
                a_top, a_bot = a_i[:half], a_i[half:]
            else:
                a_top = a_bot = jnp.zeros((half, CHUNK), F32)
            for j in range(SUB):
                lo = 0 if j < half else half
                e = jnp.exp(jnp.minimum(bl_i[lo:] - bl_i[j:j + 1], 0.0))
                t = jnp.where(rs[lo:] >= j, q_i[lo:] * e * k_i[j:j + 1], 0.0)
                rsum = jnp.sum(t, axis=-1, keepdims=True)
                hit = cs == i * SUB + j
                if lo == 0:
                    a_top = a_top + jnp.where(hit, rsum[:half], 0.0)
                a_bot = a_bot + jnp.where(hit, rsum[half - lo:], 0.0)
            a_rows[h] += [a_top, a_bot]
    o = [o_inter[h] + _mxu(jnp.concatenate(a_rows[h], axis=0), v[h]) for h in heads]
    return jnp.concatenate(o, axis=1), sts_new


def _gla_post(o, g, wn):
    return _rms(o, wn) * _silu(g)


GLA_HALF = 2 * GLA_HEAD_V


def _gla_specs(rows, nc, rev):
    ci = (lambda c: nc - 1 - c) if rev else (lambda c: c)
    v0 = 2 * GLA_DK // GLA_HALF
    g0 = (2 * GLA_DK + D_INNER) // GLA_HALF
    return [
        pl.BlockSpec((rows, GLA_DK), lambda c: (ci(c), 0)),
        pl.BlockSpec((rows, GLA_DK), lambda c: (ci(c), 1)),
        pl.BlockSpec((rows, GLA_HALF), lambda c: (ci(c), v0)),
        pl.BlockSpec((rows, GLA_HALF), lambda c: (ci(c), v0 + 1)),
        pl.BlockSpec((rows, GLA_HALF), lambda c: (ci(c), g0)),
        pl.BlockSpec((rows, GLA_HALF), lambda c: (ci(c), g0 + 1)),
        pl.BlockSpec((rows, 128), lambda c: (ci(c), GLA_GK_COL // 128)),
        pl.BlockSpec((128, GLA_DK), lambda c: (0, 0)),
        pl.BlockSpec((1, GLA_DK), lambda c: (0, 0)),
        pl.BlockSpec((1, GLA_HEAD_V), lambda c: (0, 0)),
    ]


def _head_cols(ref_a, ref_b, h, rows=slice(None)):
    ref = ref_a if h < 2 else ref_b
    return ref[rows, (h % 2) * GLA_HEAD_V:(h % 2 + 1) * GLA_HEAD_V]


def _gla_layer_fwd(proj, wup, bias, wn, gather):
    L = proj.shape[0]
    sc = min(STEP_CHUNKS, L // CHUNK)
    rows = sc * CHUNK
    nc = L // rows
    n = len(gather)

    def body(*refs):
        q_ref, k_ref, va_ref, vb_ref, ga_ref, gb_ref, gk_ref, wup_ref, b_ref, wn_ref = refs[:10]
        x_refs = refs[10:10 + n]
        o_ref, og_ref, s_ref = refs[10 + n:13 + n]
        out_refs = refs[13 + n:13 + 2 * n]
        st, send_sems, recv_sems, local_sems = refs[13 + 2 * n:]
        start, finish = _gather_ops(x_refs, out_refs, send_sems, recv_sems, local_sems)

        @pl.when(pl.program_id(0) == 0)
        def _():
            st[...] = jnp.zeros(st.shape, F32)
            start()

        s_cur = tuple(st[h] for h in range(GLA_HEADS))
        for u in range(sc):
            r = slice(u * CHUNK, (u + 1) * CHUNK)
            for h in range(GLA_HEADS):
                s_ref[u, h] = s_cur[h]
            o, s_cur = _gla_chunk(q_ref[r], k_ref[r], va_ref[r], vb_ref[r], gk_ref[r], wup_ref[...], b_ref[...], s_cur)
            o_ref[r] = o
            for h in range(GLA_HEADS):
                vc = slice(h * GLA_HEAD_V, (h + 1) * GLA_HEAD_V)
                og_ref[r, vc] = _gla_post(o[:, vc], _head_cols(ga_ref, gb_ref, h, r), wn_ref[...]).astype(BF16)
        for h in range(GLA_HEADS):
            st[h] = s_cur[h]

        @pl.when(pl.program_id(0) == nc - 1)
        def _():
            finish()

    res = pl.pallas_call(
        body,
        grid=(nc,),
        in_specs=_gla_specs(rows, nc, False) + [ANY] * n,
        out_specs=[
            pl.BlockSpec((rows, D_INNER), lambda c: (c, 0)),
            pl.BlockSpec((rows, D_INNER), lambda c: (c, 0)),
            pl.BlockSpec((sc, GLA_HEADS, GLA_HEAD_V, GLA_HEAD_K), lambda c: (c, 0, 0, 0)),
        ] + [ANY] * n,
        out_shape=[
            jax.ShapeDtypeStruct((L, D_INNER), F32),
            jax.ShapeDtypeStruct((L, D_INNER), BF16),
            jax.ShapeDtypeStruct((L // CHUNK, GLA_HEADS, GLA_HEAD_V, GLA_HEAD_K), F32),
        ] + [jax.ShapeDtypeStruct((N_DEV,) + a.shape, a.dtype) for a in gather],
        scratch_shapes=[pltpu.VMEM((GLA_HEADS, GLA_HEAD_V, GLA_HEAD_K), F32)] + _comm_sems(n),
        compiler_params=_cparams(("arbitrary",)),
        name="gla_layer_fwd",
    )(proj, proj, proj, proj, proj, proj, proj, wup, bias, wn, *gather)
    return res[:3], res[3:]


def _gla_layer_bwd(proj, wup, bias, wn, o, s_in, dog, exchange):
    L = proj.shape[0]
    nc = L // CHUNK
    n = len(exchange)

    def body(*refs):
        (q_ref, k_ref, va_ref, vb_ref, ga_ref, gb_ref, gk_ref, wup_ref, b_ref, wn_ref, o_ref, s_ref, dog_ref) = refs[:13]
        p_refs = refs[13:13 + n]
        dp_ref, dwup_ref, db_ref, dwn_ref = refs[13 + n:17 + n]
        out_refs = refs[17 + n:17 + 2 * n]
        dst, send_sems, recv_sems, local_sems = refs[17 + 2 * n:]
        start, finish = _exchange_ops(p_refs, out_refs, send_sems, recv_sems, local_sems)

        @pl.when(pl.program_id(0) == 0)
        def _():
            dst[...] = jnp.zeros(dst.shape, F32)
            dwup_ref[...] = jnp.zeros(dwup_ref.shape, F32)
            db_ref[...] = jnp.zeros(db_ref.shape, F32)
            dwn_ref[...] = jnp.zeros(dwn_ref.shape, F32)
            start()

        dos = []
        for h in range(GLA_HEADS):
            vc = slice(h * GLA_HEAD_V, (h + 1) * GLA_HEAD_V)
            _, post_vjp = jax.vjp(_gla_post, o_ref[:, vc], _head_cols(ga_ref, gb_ref, h), wn_ref[...])
            do, dg, dwn = post_vjp(dog_ref[:, vc])
            dos.append(do)
            dp_ref[:, 2 * GLA_DK + D_INNER + h * GLA_HEAD_V:2 * GLA_DK + D_INNER + (h + 1) * GLA_HEAD_V] = dg.astype(BF16)
            dwn_ref[...] += dwn
        _, vjp = jax.vjp(_gla_chunk, q_ref[...], k_ref[...], va_ref[...], vb_ref[...], gk_ref[...], wup_ref[...], b_ref[...],
                         tuple(s_ref[h] for h in range(GLA_HEADS)))
        dq, dk, dva, dvb, dgk, dwup, db, ds = vjp((jnp.concatenate(dos, axis=1), tuple(dst[h] for h in range(GLA_HEADS))))
        for h in range(GLA_HEADS):
            dst[h] = ds[h]
        dp_ref[:, :GLA_DK] = dq.astype(BF16)
        dp_ref[:, GLA_DK:2 * GLA_DK] = dk.astype(BF16)
        dp_ref[:, 2 * GLA_DK:2 * GLA_DK + GLA_HALF] = dva.astype(BF16)
        dp_ref[:, 2 * GLA_DK + GLA_HALF:2 * GLA_DK + D_INNER] = dvb.astype(BF16)
        dwup_ref[...] += dwup
        db_ref[...] += db
        dp_ref[:, GLA_GK_COL:GLA_GK_COL + 128] = dgk.astype(BF16)
        dp_ref[:, GLA_GK_COL + 128:] = jnp.zeros((CHUNK, GLA_PROJ_PAD - GLA_GK_COL - 128), BF16)

        @pl.when(pl.program_id(0) == nc - 1)
        def _():
            finish()

    rc = lambda c: nc - 1 - c
    res = pl.pallas_call(
        body,
        grid=(nc,),
        in_specs=_gla_specs(CHUNK, nc, True) + [
            pl.BlockSpec((CHUNK, D_INNER), lambda c: (rc(c), 0)),
            pl.BlockSpec((None, GLA_HEADS, GLA_HEAD_V, GLA_HEAD_K), lambda c: (rc(c), 0, 0, 0)),
            pl.BlockSpec((CHUNK, D_INNER), lambda c: (rc(c), 0)),
        ] + [ANY] * n,
        out_specs=[
            pl.BlockSpec((CHUNK, GLA_PROJ_PAD), lambda c: (rc(c), 0)),
            pl.BlockSpec((128, GLA_DK), lambda c: (0, 0)),
            pl.BlockSpec((1, GLA_DK), lambda c: (0, 0)),
            pl.BlockSpec((1, GLA_HEAD_V), lambda c: (0, 0)),
        ] + [ANY] * n,
        out_shape=[
            jax.ShapeDtypeStruct((L, GLA_PROJ_PAD), BF16),
            jax.ShapeDtypeStruct((128, GLA_DK), F32),
            jax.ShapeDtypeStruct((1, GLA_DK), F32),
            jax.ShapeDtypeStruct((1, GLA_HEAD_V), F32),
        ] + [jax.ShapeDtypeStruct(a.shape, a.dtype) for a in exchange],
        scratch_shapes=[pltpu.VMEM((GLA_HEADS, GLA_HEAD_V, GLA_HEAD_K), F32)] + _comm_sems(n),
        compiler_params=_cparams(("arbitrary",)),
        name="gla_layer_bwd",
    )(proj, proj, proj, proj, proj, proj, proj, wup, bias, wn, o, s_in, dog, *exchange)
    return res[:4], res[4:]


@jax.custom_vjp
def _expand(v):
    r = v.shape[0]
    left = _iota((r, 128), 1) < SSD_P
    slabs = []
    for p in range(SSD_HEADS // 2):
        a = jnp.broadcast_to(v[:, 2 * p:2 * p + 1], (r, 128))
        b = jnp.broadcast_to(v[:, 2 * p + 1:2 * p + 2], (r, 128))
        slabs.append(jnp.where(left, a, b))
    return jnp.concatenate(slabs, axis=1)


def _expand_fwd(v):
    return _expand(v), None


def _expand_bwd(_, g):
    r = g.shape[0]
    lane = _iota((r, 128), 1)
    left = lane < SSD_P
    dv = jnp.zeros((r, 128), F32)
    for p in range(SSD_HEADS // 2):
        gs = g[:, 128 * p:128 * (p + 1)]
        sa = jnp.sum(jnp.where(left, gs, 0.0), axis=-1, keepdims=True)
        sb = jnp.sum(jnp.where(left, 0.0, gs), axis=-1, keepdims=True)
        dv = dv + jnp.where(lane == 2 * p, sa, 0.0) + jnp.where(lane == 2 * p + 1, sb, 0.0)
    return (dv,)


_expand.defvjp(_expand_fwd, _expand_bwd)

SSD_GW = SSD_HPG * SSD_P
SSD_BC = SSD_GROUPS * SSD_N
SSD_PHASE = 4


def _ssd_chunk(xs, Bm, Cm, dtp, dtb, alog, dsk, h_in):
    dt = _softplus(dtp + dtb)
    acum = _cumsum_rows(dt * (-jnp.exp(alog)))
    a_last = acum[CHUNK - 1:CHUNK]
    acum_b = _expand(acum)
    w_end = _expand(dt * jnp.exp(a_last - acum))
    d_b = _expand(jnp.broadcast_to(dsk, (8, 128)))[0:1]
    ac_t = jnp.concatenate([acum, acum], axis=0).T
    dt_t = jnp.concatenate([dt, dt], axis=0).T
    lane = _iota((CHUNK, 128), 1)
    left = lane < SSD_P
    causal = (lane & (SSD_P - 1)) <= _iota((CHUNK, 128), 0)
    cd = jnp.exp(ac_t[:, CHUNK - 1:CHUNK])
    ys, h_out = [], []
    for g0 in range(0, SSD_GROUPS, SSD_PHASE):
        cb2, y_off = {}, {}
        for g in range(g0, g0 + SSD_PHASE):
            Bg = Bm[:, g * SSD_N:(g + 1) * SSD_N]
            Cg = Cm[:, g * SSD_N:(g + 1) * SSD_N]
            gs = slice(g * SSD_GW, (g + 1) * SSD_GW)
            cb2[g] = _mxu_nt(Cg, jnp.concatenate([Bg, Bg], axis=0))
            y_off[g] = _mxu_nt(Cg, h_in[gs])
            st = _mxu_tn(xs[:, gs] * w_end[:, gs], Bg)
            hs = [h_in[h * SSD_P:(h + 1) * SSD_P] * cd[h:h + 1] for h in range(g * SSD_HPG, (g + 1) * SSD_HPG)]
            h_out.append(jnp.concatenate(hs, axis=0) + st)
        for p in range(g0 * (SSD_HPG // 2), (g0 + SSD_PHASE) * (SSD_HPG // 2)):
            g, k = divmod(p, SSD_HPG // 2)
            sl = slice(128 * p, 128 * (p + 1))
            ac_c = acum_b[:, sl]
            ac_r = jnp.where(left, ac_t[2 * p:2 * p + 1], ac_t[2 * p + 1:2 * p + 2])
            dt_r = jnp.where(left, dt_t[2 * p:2 * p + 1], dt_t[2 * p + 1:2 * p + 2])
            m2 = cb2[g] * jnp.where(causal, jnp.exp(jnp.minimum(ac_c - ac_r, 0.0)), 0.0) * dt_r
            xsl = xs[:, sl]
            x2 = jnp.concatenate([jnp.where(left, xsl, 0.0), jnp.where(left, 0.0, xsl)], axis=0)
            ys.append(_mxu(m2, x2) + y_off[g][:, 128 * k:128 * (k + 1)] * jnp.exp(ac_c) + xsl * d_b[:, sl])
    return jnp.concatenate(ys, axis=1), jnp.concatenate(h_out, axis=0)


def _ssd_step(xs, Bm, Cm, dtp, dtb, alog, dsk, h_in, z, wn):
    ys, h = [], h_in
    for u in range(xs.shape[0] // CHUNK):
        r = slice(u * CHUNK, (u + 1) * CHUNK)
        y, h = _ssd_chunk(xs[r], Bm[r], Cm[r], dtp[r], dtb, alog, dsk, h)
        ys.append(_rms(y * _silu(z[r]), wn))
    return jnp.concatenate(ys, axis=0), h


def _ssd_specs(rows, nc, rev):
    ci = (lambda c: nc - 1 - c) if rev else (lambda c: c)
    vec = pl.BlockSpec((1, 128), lambda c: (0, 0))
    return [
        pl.BlockSpec((rows, D_INNER), lambda c: (ci(c), 0)),
        pl.BlockSpec((rows, SSD_BC), lambda c: (ci(c), D_INNER // SSD_BC)),
        pl.BlockSpec((rows, SSD_BC), lambda c: (ci(c), D_INNER // SSD_BC + 1)),
        pl.BlockSpec((rows, 128), lambda c: (ci(c), SSD_DT_COL // 128)),
        vec, vec, vec,
        pl.BlockSpec((rows, D_INNER), lambda c: (ci(c), 0)),
        pl.BlockSpec((1, D_INNER), lambda c: (0, 0)),
    ]


def _ssd_layer_fwd(xbc, proj, dtb, alog, dsk, wn):
    L = xbc.shape[0]
    sc = min(STEP_CHUNKS, L // CHUNK)
    rows = sc * CHUNK
    nc = L // rows

    def body(xs_ref, b_ref, c_ref, dt_ref, dtb_ref, alog_ref, dsk_ref, z_ref, wn_ref, y_ref, hs_ref, hst):
        @pl.when(pl.program_id(0) == 0)
        def _():
            hst[...] = jnp.zeros(hst.shape, F32)

        h = hst[...]
        for u in range(sc):
            r = slice(u * CHUNK, (u + 1) * CHUNK)
            hs_ref[u] = h
            yn, h = _ssd_step(xs_ref[r], b_ref[r], c_ref[r], dt_ref[r], dtb_ref[...], alog_ref[...], dsk_ref[...], h,
                              z_ref[r], wn_ref[...])
            y_ref[r] = yn.astype(BF16)
        hst[...] = h

    return pl.pallas_call(
        body,
        grid=(nc,),
        in_specs=_ssd_specs(rows, nc, False),
        out_specs=[
            pl.BlockSpec((rows, D_INNER), lambda c: (c, 0)),
            pl.BlockSpec((sc, D_INNER, SSD_N), lambda c: (c, 0, 0)),
        ],
        out_shape=[
            jax.ShapeDtypeStruct((L, D_INNER), BF16),
            jax.ShapeDtypeStruct((L // CHUNK, D_INNER, SSD_N), F32),
        ],
        scratch_shapes=[pltpu.VMEM((D_INNER, SSD_N), F32)],
        compiler_params=_cparams(("arbitrary",)),
        name="ssd_layer_fwd",
    )(xbc, xbc, xbc, proj, dtb, alog, dsk, proj, wn)


def _ssd_layer_bwd(xbc, proj, dtb, alog, dsk, wn, h_saved, dyn):
    L = xbc.shape[0]
    sc = min(BWD_STEP_CHUNKS, L // CHUNK)
    rows = sc * CHUNK
    nc = L // rows

    def body(xs_ref, b_ref, c_ref, dt_ref, dtb_ref, alog_ref, dsk_ref, z_ref, wn_ref, hs_ref, dyn_ref,
             dp_ref, dx_ref, ddtb_ref, dalog_ref, ddsk_ref, dwn_ref, dhst):
        @pl.when(pl.program_id(0) == 0)
        def _():
            dhst[...] = jnp.zeros(dhst.shape, F32)
            ddtb_ref[...] = jnp.zeros((1, 128), F32)
            dalog_ref[...] = jnp.zeros((1, 128), F32)
            ddsk_ref[...] = jnp.zeros((1, 128), F32)
            dwn_ref[...] = jnp.zeros((1, D_INNER), F32)

        dh = dhst[...]
        for u in reversed(range(sc)):
            r = slice(u * CHUNK, (u + 1) * CHUNK)
            _, vjp = jax.vjp(_ssd_step, xs_ref[r], b_ref[r], c_ref[r], dt_ref[r], dtb_ref[...], alog_ref[...],
                             dsk_ref[...], hs_ref[u], z_ref[r], wn_ref[...])
            dxs, db, dc, ddt, ddtb, dalog, ddsk, dh, dz, dwn = vjp((dyn_ref[r], dh))
            dx_ref[r, :D_INNER] = dxs
            dx_ref[r, D_INNER:D_INNER + SSD_BC] = db
            dx_ref[r, D_INNER + SSD_BC:] = dc
            dp_ref[r, :D_INNER] = dz.astype(BF16)
            dp_ref[r, SSD_DT_COL:SSD_DT_COL + 128] = ddt.astype(BF16)
            ddtb_ref[...] += ddtb
            dalog_ref[...] += dalog
            ddsk_ref[...] += ddsk
            dwn_ref[...] += dwn
        dp_ref[:, D_INNER:SSD_DT_COL] = jnp.zeros((rows, SSD_CONV_DIM), BF16)
        dp_ref[:, SSD_DT_COL + 128:] = jnp.zeros((rows, SSD_PROJ_PAD - SSD_DT_COL - 128), BF16)
        dhst[...] = dh

    rc = lambda c: nc - 1 - c
    vec = pl.BlockSpec((1, 128), lambda c: (0, 0))
    vshape = jax.ShapeDtypeStruct((1, 128), F32)
    return pl.pallas_call(
        body,
        grid=(nc,),
        in_specs=_ssd_specs(rows, nc, True) + [
            pl.BlockSpec((sc, D_INNER, SSD_N), lambda c: (rc(c), 0, 0)),
            pl.BlockSpec((rows, D_INNER), lambda c: (rc(c), 0)),
        ],
        out_specs=[
            pl.BlockSpec((rows, SSD_PROJ_PAD), lambda c: (rc(c), 0)),
            pl.BlockSpec((rows, SSD_CONV_DIM), lambda c: (rc(c), 0)),
            vec, vec, vec,
            pl.BlockSpec((1, D_INNER), lambda c: (0, 0)),
        ],
        out_shape=[
            jax.ShapeDtypeStruct((L, SSD_PROJ_PAD), BF16),
            jax.ShapeDtypeStruct((L, SSD_CONV_DIM), F32),
            vshape, vshape, vshape,
            jax.ShapeDtypeStruct((1, D_INNER), F32),
        ],
        scratch_shapes=[pltpu.VMEM((D_INNER, SSD_N), F32)],
        compiler_params=_cparams(("arbitrary",)),
        name="ssd_layer_bwd",
    )(xbc, xbc, xbc, proj, dtb, alog, dsk, proj, wn, h_saved, dyn)


def _pick(n, options):
    for t in options:
        if n % t == 0:
            return t
    return n


def _token_tile(m, row_bytes, fixed_bytes):
    for t in (MM_TOKEN_TILE, MM_TOKEN_TILE // 2, MM_TOKEN_TILE // 4):
        if m % t == 0 and t * row_bytes + fixed_bytes <= MM_VMEM_BUDGET:
            return t
    return min(m, MM_TOKEN_TILE // 4)


def _mm(a, b, *, name, out_dtype=F32, add=None, exchange=()):
    M, K = a.shape
    N = b.shape[1]
    tn = _pick(N, MM_TILES)
    tk = _pick(K, MM_TILES)
    nk = K // tk
    row_bytes = 2 * (tk * a.dtype.itemsize + tn * jnp.dtype(out_dtype).itemsize + (tn * 4 if add is not None else 0)) \
        + (tn * 4 if nk > 1 else 0)
    tm = _token_tile(M, row_bytes, 2 * tk * tn * b.dtype.itemsize)
    grid = (M // tm, N // tn, nk)
    n = len(exchange)
    n_in = 2 + (add is not None)

    def body(*refs):
        a_ref, b_ref = refs[:2]
        add_ref = refs[2] if add is not None else None
        p_refs = refs[n_in:n_in + n]
        o_ref = refs[n_in + n]
        out_refs = refs[n_in + n + 1:n_in + 2 * n + 1]
        acc = refs[n_in + 2 * n + 1]
        ids = [pl.program_id(d) for d in range(3)]
        k = ids[2]
        if n:
            start, finish_exchange = _exchange_ops(p_refs, out_refs, *refs[n_in + 2 * n + 2:])

            @pl.when((ids[0] == 0) & (ids[1] == 0) & (k == 0))
            def _():
                start()

        p = _dot(_bf(a_ref[...]), _bf(b_ref[...]))

        def finish(r):
            if add is not None:
                r = r + add_ref[...]
            o_ref[...] = r.astype(out_dtype)

        if nk == 1:
            finish(p)
        else:
            @pl.when(k == 0)
            def _():
                acc[...] = p

            @pl.when((k > 0) & (k < nk - 1))
            def _():
                acc[...] += p

            @pl.when(k == nk - 1)
            def _():
                finish(acc[...] + p)

        if n:
            @pl.when((ids[0] == grid[0] - 1) & (ids[1] == grid[1] - 1) & (k == nk - 1))
            def _():
                finish_exchange()

    in_specs = [pl.BlockSpec((tm, tk), lambda i, j, k: (i, k)), pl.BlockSpec((tk, tn), lambda i, j, k: (k, j))]
    args = [a, b]
    if add is not None:
        in_specs.append(pl.BlockSpec((tm, tn), lambda i, j, k: (i, j)))
        args.append(add)
    res = pl.pallas_call(
        body,
        grid=grid,
        in_specs=in_specs + [ANY] * n,
        out_specs=[pl.BlockSpec((tm, tn), lambda i, j, k: (i, j))] + [ANY] * n,
        out_shape=[jax.ShapeDtypeStruct((M, N), out_dtype)] + [jax.ShapeDtypeStruct(p.shape, p.dtype) for p in exchange],
        scratch_shapes=[pltpu.VMEM((tm, tn) if nk > 1 else (8, 128), F32)] + (_comm_sems(n) if n else []),
        compiler_params=_cparams(("arbitrary",) * 3 if n else ("parallel", "parallel", "arbitrary")),
        name=name,
    )(*args, *exchange)
    return (res[0], res[1:]) if n else res[0]


def _mm_tn(a, b, *, name):
    M, K = a.shape
    N = b.shape[1]
    tn = _pick(N, MM_TILES)
    tm = _token_tile(M, 2 * (K * a.dtype.itemsize + tn * b.dtype.itemsize), 2 * K * tn * 4)

    def body(a_ref, b_ref, o_ref):
        _acc_out(o_ref, _dot_tn(_bf(a_ref[...]), _bf(b_ref[...])), pl.program_id(1) == 0)

    return pl.pallas_call(
        body,
        grid=(N // tn, M // tm),
        in_specs=[pl.BlockSpec((tm, K), lambda j, i: (i, 0)), pl.BlockSpec((tm, tn), lambda j, i: (i, j))],
        out_specs=pl.BlockSpec((K, tn), lambda j, i: (0, j)),
        out_shape=jax.ShapeDtypeStruct((K, N), F32),
        compiler_params=_cparams(("parallel", "arbitrary")),
        name=name,
    )(a, b)


def _rms_fwd(x, w, *, name):
    L, D = x.shape
    tm = min(TOKEN_TILE, L)

    def body(x_ref, w_ref, o_ref):
        o_ref[...] = _rms(x_ref[...], w_ref[...]).astype(BF16)

    return pl.pallas_call(
        body, grid=(L // tm,),
        in_specs=[pl.BlockSpec((tm, D), lambda i: (i, 0)), pl.BlockSpec((1, D), lambda i: (0, 0))],
        out_specs=pl.BlockSpec((tm, D), lambda i: (i, 0)),
        out_shape=jax.ShapeDtypeStruct((L, D), BF16),
        compiler_params=_cparams(("parallel",)), name=name,
    )(x, w)


def _rms_bwd(x, w, dhn, dres, *, name):
    L, D = x.shape
    tm = min(TOKEN_TILE, L)

    def body(x_ref, w_ref, dhn_ref, dres_ref, dx_ref, dw_ref):
        _, vjp = jax.vjp(_rms, x_ref[...], w_ref[...])
        dx, dw = vjp(dhn_ref[...])
        dx_ref[...] = dx + dres_ref[...]
        _acc_out(dw_ref, dw, pl.program_id(0) == 0)

    row = pl.BlockSpec((tm, D), lambda i: (i, 0))
    vec = pl.BlockSpec((1, D), lambda i: (0, 0))
    return pl.pallas_call(
        body, grid=(L // tm,),
        in_specs=[row, vec, row, row],
        out_specs=[row, vec],
        out_shape=[jax.ShapeDtypeStruct((L, D), F32), jax.ShapeDtypeStruct((1, D), F32)],
        compiler_params=_cparams(("arbitrary",)), name=name,
    )(x, w, dhn, dres)


CONV_HALO = 8
CONV_COLS = 1024


CONV_RB = 64
CONV_CB = 256


def _conv_pieces(tm):
    return [(r0, c0) for c0 in range(0, CONV_COLS, CONV_CB) for r0 in range(0, tm, min(CONV_RB, tm))]


def _conv_fwd(proj, w, b):
    L = proj.shape[0]
    tm = min(TOKEN_TILE, L)
    rb = min(CONV_RB, tm)
    c0 = D_INNER // CONV_COLS
    hb = tm // CONV_HALO

    def body(x_ref, h_ref, w_ref, b_ref, o_ref, sp_ref, xx):
        xx[0:CONV_HALO] = jnp.where(pl.program_id(1) == 0, 0.0, h_ref[...])
        xx[CONV_HALO:CONV_HALO + tm] = x_ref[...]
        for r0, cc in _conv_pieces(tm):
            cs = slice(cc, cc + CONV_CB)
            u = b_ref[:, cs]
            for k in range(SSD_CONV):
                off = CONV_HALO - (SSD_CONV - 1) + k + r0
                u = u + w_ref[k:k + 1, cs] * xx[off:off + rb, cs]
            s = 1.0 / (1.0 + jnp.exp(-u))
            o_ref[r0:r0 + rb, cs] = u * s
            sp_ref[r0:r0 + rb, cs] = s * (1.0 + u * (1.0 - s))

    blk = pl.BlockSpec((tm, CONV_COLS), lambda j, i: (i, j))
    shp = jax.ShapeDtypeStruct((L, SSD_CONV_DIM), F32)
    return pl.pallas_call(
        body, grid=(SSD_CONV_DIM // CONV_COLS, L // tm),
        in_specs=[
            pl.BlockSpec((tm, CONV_COLS), lambda j, i: (i, c0 + j)),
            pl.BlockSpec((CONV_HALO, CONV_COLS), lambda j, i: (jnp.maximum(i * hb - 1, 0), c0 + j)),
            pl.BlockSpec((SSD_CONV, CONV_COLS), lambda j, i: (0, j)),
            pl.BlockSpec((1, CONV_COLS), lambda j, i: (0, j)),
        ],
        out_specs=[blk, blk],
        out_shape=[shp, shp],
        scratch_shapes=[pltpu.VMEM((CONV_HALO + tm, CONV_COLS), F32)],
        compiler_params=_cparams(("parallel", "parallel")), name="conv_fwd",
    )(proj, proj, w, b)


def _conv_bwd(proj, w, sp, dxbc, dproj):
    L = proj.shape[0]
    tm = min(TOKEN_TILE, L)
    rb = min(CONV_RB, tm)
    nt = L // tm
    c0 = D_INNER // CONV_COLS
    hb = tm // CONV_HALO

    def fold(a):
        return jnp.sum(a.reshape(rb // 8, 8, CONV_CB), axis=0)

    def body(x_ref, h_ref, w_ref, sp_ref, dy_ref, dp_in_ref, dp_ref, dw_ref, db_ref, xx, dd):
        del dp_in_ref
        i = pl.program_id(1)
        first = i == 0

        @pl.when(first)
        def _():
            dd[tm:tm + CONV_HALO] = jnp.zeros((CONV_HALO, CONV_COLS), F32)

        xx[0:CONV_HALO] = jnp.where(i == nt - 1, 0.0, h_ref[...])
        xx[CONV_HALO:CONV_HALO + tm] = x_ref[...]
        dws, dbs = [], []
        for cc in range(0, CONV_COLS, CONV_CB):
            cs = slice(cc, cc + CONV_CB)
            acc = [jnp.zeros((8, CONV_CB), F32) for _ in range(SSD_CONV + 1)]
            for r0 in range(0, tm, rb):
                du = dy_ref[r0:r0 + rb, cs] * sp_ref[r0:r0 + rb, cs]
                dd[r0:r0 + rb, cs] = du
                for k in range(SSD_CONV):
                    off = CONV_HALO - (SSD_CONV - 1) + k + r0
                    acc[k] = acc[k] + fold(du * xx[off:off + rb, cs])
                acc[SSD_CONV] = acc[SSD_CONV] + fold(du)
            dws.append(jnp.concatenate([jnp.sum(a, axis=0, keepdims=True) for a in acc[:SSD_CONV]], axis=0))
            dbs.append(jnp.sum(acc[SSD_CONV], axis=0, keepdims=True))
        for r0, cc in _conv_pieces(tm):
            cs = slice(cc, cc + CONV_CB)
            dx = jnp.zeros((rb, CONV_CB), F32)
            for k in range(SSD_CONV):
                off = SSD_CONV - 1 - k + r0
                dx = dx + w_ref[k:k + 1, cs] * dd[off:off + rb, cs]
            dp_ref[r0:r0 + rb, cs] = dx.astype(BF16)
        _acc_out(dw_ref, jnp.concatenate(dws, axis=1), first)
        _acc_out(db_ref, jnp.concatenate(dbs, axis=1), first)
        dd[tm:tm + CONV_HALO] = dd[0:CONV_HALO]

    rt = lambda i: nt - 1 - i
    return pl.pallas_call(
        body, grid=(SSD_CONV_DIM // CONV_COLS, nt),
        in_specs=[
            pl.BlockSpec((tm, CONV_COLS), lambda j, i: (rt(i), c0 + j)),
            pl.BlockSpec((CONV_HALO, CONV_COLS), lambda j, i: (jnp.maximum(rt(i) * hb - 1, 0), c0 + j)),
            pl.BlockSpec((SSD_CONV, CONV_COLS), lambda j, i: (0, j)),
            pl.BlockSpec((tm, CONV_COLS), lambda j, i: (rt(i), j)),
            pl.BlockSpec((tm, CONV_COLS), lambda j, i: (rt(i), j)),
            pl.BlockSpec(memory_space=pl.ANY),
        ],
        out_specs=[
            pl.BlockSpec((tm, CONV_COLS), lambda j, i: (rt(i), c0 + j)),
            pl.BlockSpec((SSD_CONV, CONV_COLS), lambda j, i: (0, j)),
            pl.BlockSpec((1, CONV_COLS), lambda j, i: (0, j)),
        ],
        out_shape=[jax.ShapeDtypeStruct((L, SSD_PROJ_PAD), BF16), jax.ShapeDtypeStruct((SSD_CONV, SSD_CONV_DIM), F32),
                   jax.ShapeDtypeStruct((1, SSD_CONV_DIM), F32)],
        scratch_shapes=[pltpu.VMEM((CONV_HALO + tm, CONV_COLS), F32), pltpu.VMEM((tm + CONV_HALO, CONV_COLS), F32)],
        input_output_aliases={5: 0},
        compiler_params=_cparams(("arbitrary", "arbitrary")), name="conv_bwd",
    )(proj, proj, w, sp, dxbc, dproj)


def _loss_bwd(x, tgt, w):
    L, D = x.shape
    tm = min(TOKEN_TILE, L)

    def body(x_ref, t_ref, w_ref, l_ref, dx_ref, dw_ref):
        xv = x_ref[...]
        wv = w_ref[...]
        r = lax.rsqrt(jnp.mean(xv * xv, axis=-1, keepdims=True) + RMS_EPS)
        xh = xv * r
        e = xh * wv - t_ref[...]
        lsum = 0.5 * jnp.sum(jnp.mean(e * e, axis=-1, keepdims=True), axis=0, keepdims=True)
        dout = e * (1.0 / D)
        gx = dout * wv
        dx_ref[...] = r * (gx - xh * jnp.mean(gx * xh, axis=-1, keepdims=True))
        first = pl.program_id(0) == 0
        _acc_out(dw_ref, jnp.sum(dout * xh, axis=0, keepdims=True), first)
        _acc_out(l_ref, jnp.broadcast_to(lsum, (8, 128)), first)

    row = pl.BlockSpec((tm, D), lambda i: (i, 0))
    vec = pl.BlockSpec((1, D), lambda i: (0, 0))
    return pl.pallas_call(
        body, grid=(L // tm,),
        in_specs=[row, row, vec],
        out_specs=[pl.BlockSpec((8, 128), lambda i: (0, 0)), row, vec],
        out_shape=[jax.ShapeDtypeStruct((8, 128), F32), jax.ShapeDtypeStruct((L, D), F32), jax.ShapeDtypeStruct((1, D), F32)],
        compiler_params=_cparams(("arbitrary",)), name="loss_bwd",
    )(x, tgt, w)


MESH = pl.DeviceIdType.MESH
ANY = pl.BlockSpec(memory_space=pl.ANY)


def _comm_sems(n):
    return [pltpu.SemaphoreType.DMA((n, 7)), pltpu.SemaphoreType.DMA((n, 7)), pltpu.SemaphoreType.DMA((n,))]


def _gather_ops(x_refs, out_refs, send_sems, recv_sems, local_sems):
    n = len(x_refs)
    x, y, c = lax.axis_index("x"), lax.axis_index("y"), lax.axis_index("c")
    me, sibling = (x, y, c), (x, y, 1 - c)
    chips = [(1 - x, y), (x, 1 - y), (1 - x, 1 - y)]

    def slot(i, px, py, pc):
        return out_refs[i].at[4 * px + 2 * py + pc]

    def copy(i, k, block, to, src=None):
        return pltpu.make_async_remote_copy(
            src_ref=slot(i, *block) if src is None else src, dst_ref=slot(i, *block),
            send_sem=send_sems.at[i, k], recv_sem=recv_sems.at[i, k], device_id=to, device_id_type=MESH)

    def own():
        mine = [pltpu.make_async_copy(x_refs[i], slot(i, *me), local_sems.at[i]) for i in range(n)]
        first = [copy(i, 0, me, sibling, src=x_refs[i]) for i in range(n)]
        first += [copy(i, 1 + j, me, (*chip, c), src=x_refs[i]) for j, chip in enumerate(chips) for i in range(n)]
        return mine, first

    def start():
        mine, first = own()
        for cp in mine + first:
            cp.start()

    def finish():
        mine, first = own()
        passed = []
        for j, chip in enumerate(chips):
            for i in range(n):
                copy(i, 1 + j, (*chip, c), me).wait_recv()
                passed.append(copy(i, 4 + j, (*chip, c), sibling))
                passed[-1].start()
        for i in range(n):
            copy(i, 0, sibling, me).wait_recv()
        for j, chip in enumerate(chips):
            for i in range(n):
                copy(i, 4 + j, (*chip, 1 - c), me).wait_recv()
        for cp in first + passed:
            cp.wait_send()
        for cp in mine:
            cp.wait()

    return start, finish


def _exchange_ops(p_refs, out_refs, send_sems, recv_sems, local_sems):
    n = len(p_refs)
    x, y, c = lax.axis_index("x"), lax.axis_index("y"), lax.axis_index("c")
    my = 4 * x + 2 * y + c

    def peer(k):
        fx, fy, fc = (k >> 2) & 1, (k >> 1) & 1, k & 1
        px, py, pc = (1 - x if fx else x), (1 - y if fy else y), (1 - c if fc else c)
        return (px, py, pc), 4 * px + 2 * py + pc

    def mine():
        return [pltpu.make_async_copy(p_refs[i].at[my], out_refs[i].at[my], local_sems.at[i]) for i in range(n)]

    def start():
        for cp in mine():
            cp.start()
        for k in range(1, N_DEV):
            to, pid = peer(k)
            for i in range(n):
                pltpu.make_async_remote_copy(
                    src_ref=p_refs[i].at[pid], dst_ref=out_refs[i].at[my], send_sem=send_sems.at[i, k - 1],
                    recv_sem=recv_sems.at[i, k - 1], device_id=to, device_id_type=MESH).start()

    def finish():
        for k in range(1, N_DEV):
            to, pid = peer(k)
            for i in range(n):
                pltpu.make_async_remote_copy(
                    src_ref=p_refs[i].at[pid], dst_ref=out_refs[i].at[pid], send_sem=send_sems.at[i, k - 1],
                    recv_sem=recv_sems.at[i, k - 1], device_id=to, device_id_type=MESH).wait()
        for cp in mine():
            cp.wait()

    return start, finish


def _all_gather(xs, *, name):
    n = len(xs)

    def body(*refs):
        start, finish = _gather_ops(refs[:n], refs[n:2 * n], *refs[2 * n:])
        start()
        finish()

    return pl.pallas_call(
        body,
        out_shape=[jax.ShapeDtypeStruct((N_DEV,) + a.shape, a.dtype) for a in xs],
        in_specs=[ANY] * n, out_specs=[ANY] * n, scratch_shapes=_comm_sems(n), name=name,
    )(*xs)


def _adamw(parts, w, m, v, *, name):
    a, b = w.shape
    tr = _pick(a, (256, 128, 64, 32, 16, 8))

    def body(p_ref, w_ref, m_ref, v_ref, g_ref, d_ref, mo_ref, vo_ref):
        g = p_ref[0]
        for s in range(1, N_DEV):
            g = g + p_ref[s]
        mn = ADAM_B1 * m_ref[...] + (1.0 - ADAM_B1) * g
        vn = ADAM_B2 * v_ref[...] + (1.0 - ADAM_B2) * jnp.square(g)
        m_hat = mn / (1.0 - ADAM_B1 ** ADAM_STEP)
        v_hat = vn / (1.0 - ADAM_B2 ** ADAM_STEP)
        g_ref[...] = g
        d_ref[...] = -ADAM_LR * (m_hat / (jnp.sqrt(v_hat) + ADAM_EPS) + ADAM_WD * w_ref[...])
        mo_ref[...] = mn
        vo_ref[...] = vn

    blk = pl.BlockSpec((tr, b), lambda i: (i, 0))
    shp = jax.ShapeDtypeStruct((a, b), F32)
    return pl.pallas_call(
        body, grid=(a // tr,),
        in_specs=[pl.BlockSpec((N_DEV, tr, b), lambda i: (0, i, 0)), blk, blk, blk],
        out_specs=[blk, blk, blk, blk],
        out_shape=[shp, shp, shp, shp],
        compiler_params=_cparams(("parallel",)), name=name,
    )(parts, w, m, v)


def _col_shards(a, n):
    return a.reshape(a.shape[0], N_DEV, n).transpose(1, 0, 2)


def _from_col_shards(g, cols):
    r = g.shape[1]
    full = g.transpose(1, 0, 2).reshape(r, -1)
    return jnp.pad(full, ((0, 0), (0, cols - full.shape[1])))


def kernel(x, norm_w, gla_in_proj, gla_gate_up, gla_gate_bias, gla_head_norm, gla_out_proj, ssd_in_proj, ssd_conv_w, ssd_conv_b, ssd_dt_bias, ssd_a_log, ssd_d, ssd_gate_norm, ssd_out_proj, final_norm, loss_target, m_norm_w, m_gla_in_proj, m_gla_gate_up, m_gla_gate_bias, m_gla_head_norm, m_gla_out_proj, m_ssd_in_proj, m_ssd_conv_w, m_ssd_conv_b, m_ssd_dt_bias, m_ssd_a_log, m_ssd_d, m_ssd_gate_norm, m_ssd_out_proj, m_final_norm, v_norm_w, v_gla_in_proj, v_gla_gate_up, v_gla_gate_bias, v_gla_head_norm, v_gla_out_proj, v_ssd_in_proj, v_ssd_conv_w, v_ssd_conv_b, v_ssd_dt_bias, v_ssd_a_log, v_ssd_d, v_ssd_gate_norm, v_ssd_out_proj, v_final_norm):
    x0 = x[0]
    tgt = loss_target[0]
    n_gin = GLA_PROJ // N_DEV
    n_sin = SSD_PROJ // N_DEV
    n_up = GLA_DK // N_DEV
    n_cv = SSD_CONV_DIM // N_DEV

    g_gin, g_up, g_gout = _all_gather(
        [gla_in_proj[0].astype(BF16), gla_gate_up[0].astype(BF16), gla_out_proj[0].astype(BF16)], name="gather_weights")
    w_gin = _from_col_shards(g_gin, GLA_PROJ_PAD)
    wup = jnp.pad(_from_col_shards(g_up, GLA_DK), ((0, 128 - GLA_RANK), (0, 0))).astype(F32)
    w_gout = g_gout.reshape(D_INNER, D_MODEL)
    vec128 = lambda a: jnp.pad(a.reshape(1, -1), ((0, 0), (0, 128 - a.size)))
    dtb, alog, dsk = vec128(ssd_dt_bias), vec128(ssd_a_log), vec128(ssd_d)
    nw0, nw1 = norm_w[0:1], norm_w[1:2]

    hn1 = _rms_fwd(x0, nw0, name="rms1_fwd")
    proj1 = _mm(hn1, w_gin, name="gla_in_proj")
    (o, og, s_saved), (g_sin, g_sout, g_cw, g_cb, g_gn) = _gla_layer_fwd(
        proj1, wup, gla_gate_bias, gla_head_norm,
        [ssd_in_proj[0].astype(BF16), ssd_out_proj[0].astype(BF16), ssd_conv_w[0], ssd_conv_b, ssd_gate_norm])
    w_sin = _from_col_shards(g_sin, SSD_PROJ_PAD)
    w_sout = g_sout.reshape(D_INNER, D_MODEL)
    conv_w = _from_col_shards(g_cw, SSD_CONV_DIM)
    conv_b = g_cb.reshape(1, SSD_CONV_DIM)
    gate_norm = g_gn.reshape(1, D_INNER)
    x1 = _mm(og, w_gout, add=x0, name="gla_out_proj")
    hn2 = _rms_fwd(x1, nw1, name="rms2_fwd")
    proj2 = _mm(hn2, w_sin, name="ssd_in_proj")
    xbc, conv_sp = _conv_fwd(proj2, conv_w, conv_b)
    yn, h_saved = _ssd_layer_fwd(xbc, proj2, dtb, alog, dsk, gate_norm)
    x2 = _mm(yn, w_sout, add=x1, name="ssd_out_proj")
    lsum, dx2, d_final = _loss_bwd(x2, tgt, final_norm.reshape(1, D_MODEL))
    loss = lax.psum(lsum[0, 0], ("x", "y", "c"))

    d_sout = _mm_tn(yn, dx2, name="ssd_out_proj_dw")
    dyn = _mm(dx2, w_sout.T, name="ssd_out_proj_dx")
    dproj2, dxbc, d_dtb, d_alog, d_dsk, d_gate_norm = _ssd_layer_bwd(xbc, proj2, dtb, alog, dsk, gate_norm, h_saved, dyn)
    dproj2, d_conv_w, d_conv_b = _conv_bwd(proj2, conv_w, conv_sp, dxbc, dproj2)
    d_sin = _mm_tn(hn2, dproj2, name="ssd_in_proj_dw")
    dhn2 = _mm(dproj2, w_sin.T, name="ssd_in_proj_dx")
    dx1, d_nw1 = _rms_bwd(x1, nw1, dhn2, dx2, name="rms2_bwd")
    d_gout = _mm_tn(og, dx1, name="gla_out_proj_dw")
    dog = _mm(dx1, w_gout.T, name="gla_out_proj_dx")
    early = {
        "gla_out_proj": ((gla_out_proj[0], m_gla_out_proj[0], v_gla_out_proj[0]), d_gout.reshape(N_DEV, -1, D_MODEL)),
        "ssd_in_proj": ((ssd_in_proj[0], m_ssd_in_proj[0], v_ssd_in_proj[0]), _col_shards(d_sin[:, :SSD_PROJ], n_sin)),
        "ssd_conv_w": ((ssd_conv_w[0], m_ssd_conv_w[0], v_ssd_conv_w[0]), _col_shards(d_conv_w, n_cv)),
        "ssd_conv_b": ((ssd_conv_b, m_ssd_conv_b, v_ssd_conv_b), d_conv_b.reshape(N_DEV, 1, n_cv)),
        "ssd_gate_norm": ((ssd_gate_norm, m_ssd_gate_norm, v_ssd_gate_norm), d_gate_norm.reshape(N_DEV, 1, -1)),
        "ssd_out_proj": ((ssd_out_proj[0], m_ssd_out_proj[0], v_ssd_out_proj[0]), d_sout.reshape(N_DEV, -1, D_MODEL)),
    }
    (dproj1, d_wup, d_gbias, d_head_norm), early_recv = _gla_layer_bwd(
        proj1, wup, gla_gate_bias, gla_head_norm, o, s_saved, dog, [p for _, p in early.values()])
    d_gin = _mm_tn(hn1, dproj1, name="gla_in_proj_dw")
    late = {
        "gla_in_proj": ((gla_in_proj[0], m_gla_in_proj[0], v_gla_in_proj[0]), _col_shards(d_gin[:, :GLA_PROJ], n_gin)),
        "gla_gate_up": ((gla_gate_up[0], m_gla_gate_up[0], v_gla_gate_up[0]), _col_shards(d_wup[:GLA_RANK], n_up)),
    }
    dhn1, late_recv = _mm(dproj1, w_gin.T, name="gla_in_proj_dx", exchange=[p for _, p in late.values()])
    dx0, d_nw0 = _rms_bwd(x0, nw0, dhn1, dx1, name="rms1_bwd")
    heads = SSD_HEADS
    replicated = {
        "norm_w": ((norm_w, m_norm_w, v_norm_w), jnp.concatenate([d_nw0, d_nw1], axis=0)),
        "gla_gate_bias": ((gla_gate_bias, m_gla_gate_bias, v_gla_gate_bias), d_gbias),
        "gla_head_norm": ((gla_head_norm, m_gla_head_norm, v_gla_head_norm), d_head_norm),
        "ssd_dt_bias": ((ssd_dt_bias, m_ssd_dt_bias, v_ssd_dt_bias), d_dtb[:, :heads]),
        "ssd_a_log": ((ssd_a_log, m_ssd_a_log, v_ssd_a_log), d_alog[:, :heads]),
        "ssd_d": ((ssd_d, m_ssd_d, v_ssd_d), d_dsk[:, :heads]),
        "final_norm": (tuple(t.reshape(1, D_MODEL) for t in (final_norm, m_final_norm, v_final_norm)), d_final),
    }
    results = {}
    for group, recv in ((early, early_recv),
                        (late, late_recv),
                        (replicated, _all_gather([p for _, p in replicated.values()], name="replicated_gather"))):
        for (nm, ((w, m, v), _)), r in zip(group.items(), recv):
            results[nm] = _adamw(r, w, m, v, name=nm + "_adamw")

    order = [("norm_w", norm_w), ("gla_in_proj", gla_in_proj), ("gla_gate_up", gla_gate_up), ("gla_gate_bias", gla_gate_bias),
             ("gla_head_norm", gla_head_norm), ("gla_out_proj", gla_out_proj), ("ssd_in_proj", ssd_in_proj),
             ("ssd_conv_w", ssd_conv_w), ("ssd_conv_b", ssd_conv_b), ("ssd_dt_bias", ssd_dt_bias), ("ssd_a_log", ssd_a_log),
             ("ssd_d", ssd_d), ("ssd_gate_norm", ssd_gate_norm), ("ssd_out_proj", ssd_out_proj), ("final_norm", final_norm)]
    out = [loss, dx0[None]]
    for i in range(4):
        out += [results[nm][i].reshape(ref.shape) for nm, ref in order]
    return tuple(out)
```

```python
import jax
import jax.numpy as jnp
from jax import lax
from jax.experimental import pallas as pl
from jax.experimental.pallas import tpu as pltpu

F32 = jnp.float32
BF16 = jnp.bfloat16

D_MODEL = 1024
D_INNER = 2048
RMS_EPS = 1e-6
GLA_HEADS = 4
GLA_DK = 512
GLA_HEAD_K = 128
GLA_HEAD_V = 512
GLA_RANK = 16
GLA_NORMALIZER = 16.0
CHUNK = 64
SUB = 16
STEP_CHUNKS = 4
BWD_STEP_CHUNKS = 1
GLA_PROJ = 5136
GLA_PROJ_PAD = 5376
GLA_GK_COL = 5120
SSD_HEADS = 32
SSD_GROUPS = 8
SSD_HPG = 4
SSD_P = 64
SSD_N = 128
SSD_CONV = 4
SSD_CONV_DIM = 4096
SSD_PROJ = 6176
SSD_PROJ_PAD = 6400
SSD_DT_COL = 6144
N_DEV = 8

ADAM_LR = 0.001
ADAM_B1 = 0.9
ADAM_B2 = 0.999
ADAM_EPS = 1e-08
ADAM_WD = 0.01
ADAM_STEP = 10

VMEM_LIMIT = 56 * 1024 * 1024
TOKEN_TILE = 512
MM_TOKEN_TILE = 2048
MM_VMEM_BUDGET = 44 * 1024 * 1024
MM_TILES = (2048, 1792, 1280, 1024, 768, 512, 256, 128)


def _dot(a, b):
    return jnp.dot(a, b, preferred_element_type=F32)


def _dot_nt(a, b):
    return lax.dot_general(a, b, (((1,), (1,)), ((), ())), preferred_element_type=F32)


def _dot_tn(a, b):
    return lax.dot_general(a, b, (((0,), (0,)), ((), ())), preferred_element_type=F32)


def _bf(a):
    return a.astype(BF16)


@jax.custom_vjp
def _mxu(a, b):
    return _dot(_bf(a), _bf(b))


def _mxu_fwd(a, b):
    return _mxu(a, b), (a, b)


def _mxu_bwd(res, g):
    a, b = res
    return _dot_nt(_bf(g), _bf(b)), _dot_tn(_bf(a), _bf(g))


_mxu.defvjp(_mxu_fwd, _mxu_bwd)


@jax.custom_vjp
def _mxu_nt(a, b):
    return _dot_nt(_bf(a), _bf(b))


def _mxu_nt_fwd(a, b):
    return _mxu_nt(a, b), (a, b)


def _mxu_nt_bwd(res, g):
    a, b = res
    return _dot(_bf(g), _bf(b)), _dot_tn(_bf(g), _bf(a))


_mxu_nt.defvjp(_mxu_nt_fwd, _mxu_nt_bwd)


@jax.custom_vjp
def _mxu_tn(a, b):
    return _dot_tn(_bf(a), _bf(b))


def _mxu_tn_fwd(a, b):
    return _mxu_tn(a, b), (a, b)


def _mxu_tn_bwd(res, g):
    a, b = res
    return _dot_nt(_bf(b), _bf(g)), _dot(_bf(a), _bf(g))


_mxu_tn.defvjp(_mxu_tn_fwd, _mxu_tn_bwd)


def _split2(a):
    hi = _bf(a)
    return hi, _bf(a - hi.astype(F32))


def _three_pass(dot, a, b):
    ah, al = _split2(a)
    bh, bl = _split2(b)
    return dot(ah, bh) + (dot(ah, bl) + dot(al, bh))


@jax.custom_vjp
def _dot3_nt(a, b):
    return _three_pass(_dot_nt, a, b)


def _dot3_nt_fwd(a, b):
    return _dot3_nt(a, b), (a, b)


def _dot3_nt_bwd(res, g):
    a, b = res
    return _three_pass(_dot, g, b), _three_pass(_dot_tn, g, a)


_dot3_nt.defvjp(_dot3_nt_fwd, _dot3_nt_bwd)


def _silu(x):
    return x / (1.0 + jnp.exp(-x))


def _log_sigmoid(z):
    return jnp.minimum(z, 0.0) - jnp.log(1.0 + jnp.exp(-jnp.abs(z)))


def _softplus(z):
    return jnp.maximum(z, 0.0) + jnp.log(1.0 + jnp.exp(-jnp.abs(z)))


def _iota(shape, dim):
    return lax.broadcasted_iota(jnp.int32, shape, dim)


def _rms(x, w):
    return x * lax.rsqrt(jnp.mean(x * x, axis=-1, keepdims=True) + RMS_EPS) * w


def _scan_rows(a, reverse, seg):
    n = a.shape[0]
    pos = _iota(a.shape, 0) & (seg - 1)
    sh = 1
    while sh < seg:
        if reverse:
            a = a + jnp.where(pos < seg - sh, pltpu.roll(a, n - sh, 0), 0.0)
        else:
            a = a + jnp.where(pos >= sh, pltpu.roll(a, sh, 0), 0.0)
        sh *= 2
    return a


def _make_cumsum(seg):
    @jax.custom_vjp
    def cumsum(a):
        return _scan_rows(a, False, seg)

    cumsum.defvjp(lambda a: (_scan_rows(a, False, seg), None), lambda _, g: (_scan_rows(g, True, seg),))
    return cumsum


_cumsum_sub = _make_cumsum(SUB)
_cumsum_rows = _make_cumsum(CHUNK)


def _cparams(sem):
    return pltpu.CompilerParams(dimension_semantics=sem, vmem_limit_bytes=VMEM_LIMIT)


def _acc_out(ref, val, first):
    @pl.when(first)
    def _():
        ref[...] = val

    @pl.when(jnp.logical_not(first))
    def _():
        ref[...] += val


def _gla_chunk(q, k, va, vb, gk, wup, bias, sts):
    nb = CHUNK // SUB
    heads = range(GLA_HEADS)
    hc = lambda a, h: a[:, h * GLA_HEAD_K:(h + 1) * GLA_HEAD_K]
    v = [(va if h < 2 else vb)[:, (h % 2) * GLA_HEAD_V:(h % 2 + 1) * GLA_HEAD_V] for h in heads]
    z = _mxu(gk, wup) + bias
    la = _log_sigmoid(z) * (1.0 / GLA_NORMALIZER)
    qs = q * (GLA_HEAD_K ** -0.5)
    bl = _cumsum_sub(la)
    tot = [jnp.sum(la[i * SUB:(i + 1) * SUB], axis=0, keepdims=True) for i in range(nb)]
    pre = [jnp.zeros((1, GLA_DK), F32)]
    for i in range(nb):
        pre.append(pre[i] + tot[i])
    b_last = pre[nb]
    rows_of = lambda vals: jnp.concatenate([jnp.broadcast_to(t, (SUB, GLA_DK)) for t in vals], axis=0)
    suf = rows_of(tot) - bl
    nxt = rows_of(pre[1:])
    q_in = qs * jnp.exp(bl + rows_of(pre[:nb]))
    k_end = k * jnp.exp(suf + (b_last - nxt))
    dec = jnp.exp(b_last)
    qa = qs * jnp.exp(bl)
    o_inter = [_mxu_nt(hc(q_in, h), sts[h]) for h in heads]
    sts_new = tuple(sts[h] * hc(dec, h) + _mxu_tn(v[h], hc(k_end, h)) for h in heads)
    half = SUB // 2
    rs = _iota((SUB, GLA_HEAD_K), 0)
    cs = _iota((half, CHUNK), 1)
    a_rows = [[] for _ in heads]
    for i in range(nb):
        sl = slice(i * SUB, (i + 1) * SUB)
        n = i * SUB
        if i > 0:
            kp = jnp.concatenate([k[:n] * jnp.exp(suf[:n] + (pre[i] - nxt[:n])), jnp.zeros((CHUNK - n, GLA_DK), F32)], axis=0)
        for h in heads:
            q_i, k_i, bl_i = hc(qs[sl], h), hc(k[sl], h), hc(bl[sl], h)
            if i > 0:
                a_i = _dot3_nt(hc(qa[sl], h), hc(kp, h))
                a_top, a_bot = a_i[:half], a_i[half:]
            else:
                a_top = a_bot = jnp.zeros((half, CHUNK), F32)
            for j in range(SUB):
                lo = 0 if j < half else half
                e = jnp.exp(jnp.minimum(bl_i[lo:] - bl_i[j:j + 1], 0.0))
                t = jnp.where(rs[lo:] >= j, q_i[lo:] * e * k_i[j:j + 1], 0.0)
                rsum = jnp.sum(t, axis=-1, keepdims=True)
                hit = cs == i * SUB + j
                if lo == 0:
                    a_top = a_top + jnp.where(hit, rsum[:half], 0.0)
                a_bot = a_bot + jnp.where(hit, rsum[half - lo:], 0.0)
            a_rows[h] += [a_top, a_bot]
    o = [o_inter[h] + _mxu(jnp.concatenate(a_rows[h], axis=0), v[h]) for h in heads]
    return jnp.concatenate(o, axis=1), sts_new


def _gla_post(o, g, wn):
    return _rms(o, wn) * _silu(g)


GLA_HALF = 2 * GLA_HEAD_V


def _gla_specs(rows, nc, rev):
    ci = (lambda c: nc - 1 - c) if rev else (lambda c: c)
    v0 = 2 * GLA_DK // GLA_HALF
    g0 = (2 * GLA_DK + D_INNER) // GLA_HALF
    return [
        pl.BlockSpec((rows, GLA_DK), lambda c: (ci(c), 0)),
        pl.BlockSpec((rows, GLA_DK), lambda c: (ci(c), 1)),
        pl.BlockSpec((rows, GLA_HALF), lambda c: (ci(c), v0)),
        pl.BlockSpec((rows, GLA_HALF), lambda c: (ci(c), v0 + 1)),
        pl.BlockSpec((rows, GLA_HALF), lambda c: (ci(c), g0)),
        pl.BlockSpec((rows, GLA_HALF), lambda c: (ci(c), g0 + 1)),
        pl.BlockSpec((rows, 128), lambda c: (ci(c), GLA_GK_COL // 128)),
        pl.BlockSpec((128, GLA_DK), lambda c: (0, 0)),
        pl.BlockSpec((1, GLA_DK), lambda c: (0, 0)),
        pl.BlockSpec((1, GLA_HEAD_V), lambda c: (0, 0)),
    ]


def _head_cols(ref_a, ref_b, h, rows=slice(None)):
    ref = ref_a if h < 2 else ref_b
    return ref[rows, (h % 2) * GLA_HEAD_V:(h % 2 + 1) * GLA_HEAD_V]


def _gla_layer_fwd(proj, wup, bias, wn, gather):
    L = proj.shape[0]
    sc = min(STEP_CHUNKS, L // CHUNK)
    rows = sc * CHUNK
    nc = L // rows
    n = len(gather)

    def body(*refs):
        q_ref, k_ref, va_ref, vb_ref, ga_ref, gb_ref, gk_ref, wup_ref, b_ref, wn_ref = refs[:10]
        x_refs = refs[10:10 + n]
        o_ref, og_ref, s_ref = refs[10 + n:13 + n]
        out_refs = refs[13 + n:13 + 2 * n]
        st, send_sems, recv_sems, local_sems = refs[13 + 2 * n:]
        start, finish = _gather_ops(x_refs, out_refs, send_sems, recv_sems, local_sems)

        @pl.when(pl.program_id(0) == 0)
        def _():
            st[...] = jnp.zeros(st.shape, F32)
            start()

        s_cur = tuple(st[h] for h in range(GLA_HEADS))
        for u in range(sc):
            r = slice(u * CHUNK, (u + 1) * CHUNK)
            for h in range(GLA_HEADS):
                s_ref[u, h] = s_cur[h]
            o, s_cur = _gla_chunk(q_ref[r], k_ref[r], va_ref[r], vb_ref[r], gk_ref[r], wup_ref[...], b_ref[...], s_cur)
            o_ref[r] = o
            for h in range(GLA_HEADS):
                vc = slice(h * GLA_HEAD_V, (h + 1) * GLA_HEAD_V)
                og_ref[r, vc] = _gla_post(o[:, vc], _head_cols(ga_ref, gb_ref, h, r), wn_ref[...]).astype(BF16)
        for h in range(GLA_HEADS):
            st[h] = s_cur[h]

        @pl.when(pl.program_id(0) == nc - 1)
        def _():
            finish()

    res = pl.pallas_call(
        body,
        grid=(nc,),
        in_specs=_gla_specs(rows, nc, False) + [ANY] * n,
        out_specs=[
            pl.BlockSpec((rows, D_INNER), lambda c: (c, 0)),
            pl.BlockSpec((rows, D_INNER), lambda c: (c, 0)),
            pl.BlockSpec((sc, GLA_HEADS, GLA_HEAD_V, GLA_HEAD_K), lambda c: (c, 0, 0, 0)),
        ] + [ANY] * n,
        out_shape=[
            jax.ShapeDtypeStruct((L, D_INNER), F32),
            jax.ShapeDtypeStruct((L, D_INNER), BF16),
            jax.ShapeDtypeStruct((L // CHUNK, GLA_HEADS, GLA_HEAD_V, GLA_HEAD_K), F32),
        ] + [jax.ShapeDtypeStruct((N_DEV,) + a.shape, a.dtype) for a in gather],
        scratch_shapes=[pltpu.VMEM((GLA_HEADS, GLA_HEAD_V, GLA_HEAD_K), F32)] + _comm_sems(n),
        compiler_params=_cparams(("arbitrary",)),
        name="gla_layer_fwd",
    )(proj, proj, proj, proj, proj, proj, proj, wup, bias, wn, *gather)
    return res[:3], res[3:]


def _gla_layer_bwd(proj, wup, bias, wn, o, s_in, dog, exchange):
    L = proj.shape[0]
    nc = L // CHUNK
    n = len(exchange)

    def body(*refs):
        (q_ref, k_ref, va_ref, vb_ref, ga_ref, gb_ref, gk_ref, wup_ref, b_ref, wn_ref, o_ref, s_ref, dog_ref) = refs[:13]
        p_refs = refs[13:13 + n]
        dp_ref, dwup_ref, db_ref, dwn_ref = refs[13 + n:17 + n]
        out_refs = refs[17 + n:17 + 2 * n]
        dst, send_sems, recv_sems, local_sems = refs[17 + 2 * n:]
        start, finish = _exchange_ops(p_refs, out_refs, send_sems, recv_sems, local_sems)

        @pl.when(pl.program_id(0) == 0)
        def _():
            dst[...] = jnp.zeros(dst.shape, F32)
            dwup_ref[...] = jnp.zeros(dwup_ref.shape, F32)
            db_ref[...] = jnp.zeros(db_ref.shape, F32)
            dwn_ref[...] = jnp.zeros(dwn_ref.shape, F32)
            start()

        dos = []
        for h in range(GLA_HEADS):
            vc = slice(h * GLA_HEAD_V, (h + 1) * GLA_HEAD_V)
            _, post_vjp = jax.vjp(_gla_post, o_ref[:, vc], _head_cols(ga_ref, gb_ref, h), wn_ref[...])
            do, dg, dwn = post_vjp(dog_ref[:, vc])
            dos.append(do)
            dp_ref[:, 2 * GLA_DK + D_INNER + h * GLA_HEAD_V:2 * GLA_DK + D_INNER + (h + 1) * GLA_HEAD_V] = dg.astype(BF16)
            dwn_ref[...] += dwn
        _, vjp = jax.vjp(_gla_chunk, q_ref[...], k_ref[...], va_ref[...], vb_ref[...], gk_ref[...], wup_ref[...], b_ref[...],
                         tuple(s_ref[h] for h in range(GLA_HEADS)))
        dq, dk, dva, dvb, dgk, dwup, db, ds = vjp((jnp.concatenate(dos, axis=1), tuple(dst[h] for h in range(GLA_HEADS))))
        for h in range(GLA_HEADS):
            dst[h] = ds[h]
        dp_ref[:, :GLA_DK] = dq.astype(BF16)
        dp_ref[:, GLA_DK:2 * GLA_DK] = dk.astype(BF16)
        dp_ref[:, 2 * GLA_DK:2 * GLA_DK + GLA_HALF] = dva.astype(BF16)
        dp_ref[:, 2 * GLA_DK + GLA_HALF:2 * GLA_DK + D_INNER] = dvb.astype(BF16)
        dwup_ref[...] += dwup
        db_ref[...] += db
        dp_ref[:, GLA_GK_COL:GLA_GK_COL + 128] = dgk.astype(BF16)
        dp_ref[:, GLA_GK_COL + 128:] = jnp.zeros((CHUNK, GLA_PROJ_PAD - GLA_GK_COL - 128), BF16)

        @pl.when(pl.program_id(0) == nc - 1)
        def _():
            finish()

    rc = lambda c: nc - 1 - c
    res = pl.pallas_call(
        body,
        grid=(nc,),
        in_specs=_gla_specs(CHUNK, nc, True) + [
            pl.BlockSpec((CHUNK, D_INNER), lambda c: (rc(c), 0)),
            pl.BlockSpec((None, GLA_HEADS, GLA_HEAD_V, GLA_HEAD_K), lambda c: (rc(c), 0, 0, 0)),
            pl.BlockSpec((CHUNK, D_INNER), lambda c: (rc(c), 0)),
        ] + [ANY] * n,
        out_specs=[
            pl.BlockSpec((CHUNK, GLA_PROJ_PAD), lambda c: (rc(c), 0)),
            pl.BlockSpec((128, GLA_DK), lambda c: (0, 0)),
            pl.BlockSpec((1, GLA_DK), lambda c: (0, 0)),
            pl.BlockSpec((1, GLA_HEAD_V), lambda c: (0, 0)),
        ] + [ANY] * n,
        out_shape=[
            jax.ShapeDtypeStruct((L, GLA_PROJ_PAD), BF16),
            jax.ShapeDtypeStruct((128, GLA_DK), F32),
            jax.ShapeDtypeStruct((1, GLA_DK), F32),
            jax.ShapeDtypeStruct((1, GLA_HEAD_V), F32),
        ] + [jax.ShapeDtypeStruct(a.shape, a.dtype) for a in exchange],
        scratch_shapes=[pltpu.VMEM((GLA_HEADS, GLA_HEAD_V, GLA_HEAD_K), F32)] + _comm_sems(n),
        compiler_params=_cparams(("arbitrary",)),
        name="gla_layer_bwd",
    )(proj, proj, proj, proj, proj, proj, proj, wup, bias, wn, o, s_in, dog, *exchange)
    return res[:4], res[4:]


@jax.custom_vjp
def _expand(v):
    r = v.shape[0]
    left = _iota((r, 128), 1) < SSD_P
    slabs = []
    for p in range(SSD_HEADS // 2):
        a = jnp.broadcast_to(v[:, 2 * p:2 * p + 1], (r, 128))
        b = jnp.broadcast_to(v[:, 2 * p + 1:2 * p + 2], (r, 128))
        slabs.append(jnp.where(left, a, b))
    return jnp.concatenate(slabs, axis=1)


def _expand_fwd(v):
    return _expand(v), None


def _expand_bwd(_, g):
    r = g.shape[0]
    lane = _iota((r, 128), 1)
    left = lane < SSD_P
    dv = jnp.zeros((r, 128), F32)
    for p in range(SSD_HEADS // 2):
        gs = g[:, 128 * p:128 * (p + 1)]
        sa = jnp.sum(jnp.where(left, gs, 0.0), axis=-1, keepdims=True)
        sb = jnp.sum(jnp.where(left, 0.0, gs), axis=-1, keepdims=True)
        dv = dv + jnp.where(lane == 2 * p, sa, 0.0) + jnp.where(lane == 2 * p + 1, sb, 0.0)
    return (dv,)


_expand.defvjp(_expand_fwd, _expand_bwd)

SSD_GW = SSD_HPG * SSD_P
SSD_BC = SSD_GROUPS * SSD_N
SSD_PHASE = 4


def _ssd_chunk(xs, Bm, Cm, dtp, dtb, alog, dsk, h_in):
    dt = _softplus(dtp + dtb)
    acum = _cumsum_rows(dt * (-jnp.exp(alog)))
    a_last = acum[CHUNK - 1:CHUNK]
    acum_b = _expand(acum)
    w_end = _expand(dt * jnp.exp(a_last - acum))
    d_b = _expand(jnp.broadcast_to(dsk, (8, 128)))[0:1]
    ac_t = jnp.concatenate([acum, acum], axis=0).T
    dt_t = jnp.concatenate([dt, dt], axis=0).T
    lane = _iota((CHUNK, 128), 1)
    left = lane < SSD_P
    causal = (lane & (SSD_P - 1)) <= _iota((CHUNK, 128), 0)
    cd = jnp.exp(ac_t[:, CHUNK - 1:CHUNK])
    ys, h_out = [], []
    for g0 in range(0, SSD_GROUPS, SSD_PHASE):
        cb2, y_off = {}, {}
        for g in range(g0, g0 + SSD_PHASE):
            Bg = Bm[:, g * SSD_N:(g + 1) * SSD_N]
            Cg = Cm[:, g * SSD_N:(g + 1) * SSD_N]
            gs = slice(g * SSD_GW, (g + 1) * SSD_GW)
            cb2[g] = _mxu_nt(Cg, jnp.concatenate([Bg, Bg], axis=0))
            y_off[g] = _mxu_nt(Cg, h_in[gs])
            st = _mxu_tn(xs[:, gs] * w_end[:, gs], Bg)
            hs = [h_in[h * SSD_P:(h + 1) * SSD_P] * cd[h:h + 1] for h in range(g * SSD_HPG, (g + 1) * SSD_HPG)]
            h_out.append(jnp.concatenate(hs, axis=0) + st)
        for p in range(g0 * (SSD_HPG // 2), (g0 + SSD_PHASE) * (SSD_HPG // 2)):
            g, k = divmod(p, SSD_HPG // 2)
            sl = slice(128 * p, 128 * (p + 1))
            ac_c = acum_b[:, sl]
            ac_r = jnp.where(left, ac_t[2 * p:2 * p + 1], ac_t[2 * p + 1:2 * p + 2])
            dt_r = jnp.where(left, dt_t[2 * p:2 * p + 1], dt_t[2 * p + 1:2 * p + 2])
            m2 = cb2[g] * jnp.where(causal, jnp.exp(jnp.minimum(ac_c - ac_r, 0.0)), 0.0) * dt_r
            xsl = xs[:, sl]
            x2 = jnp.concatenate([jnp.where(left, xsl, 0.0), jnp.where(left, 0.0, xsl)], axis=0)
            ys.append(_mxu(m2, x2) + y_off[g][:, 128 * k:128 * (k + 1)] * jnp.exp(ac_c) + xsl * d_b[:, sl])
    return jnp.concatenate(ys, axis=1), jnp.concatenate(h_out, axis=0)


def _ssd_step(xs, Bm, Cm, dtp, dtb, alog, dsk, h_in, z, wn):
    ys, h = [], h_in
    for u in range(xs.shape[0] // CHUNK):
        r = slice(u * CHUNK, (u + 1) * CHUNK)
        y, h = _ssd_chunk(xs[r], Bm[r], Cm[r], dtp[r], dtb, alog, dsk, h)
        ys.append(_rms(y * _silu(z[r]), wn))
    return jnp.concatenate(ys, axis=0), h


def _ssd_specs(rows, nc, rev):
    ci = (lambda c: nc - 1 - c) if rev else (lambda c: c)
    vec = pl.BlockSpec((1, 128), lambda c: (0, 0))
    return [
        pl.BlockSpec((rows, D_INNER), lambda c: (ci(c), 0)),
        pl.BlockSpec((rows, SSD_BC), lambda c: (ci(c), D_INNER // SSD_BC)),
        pl.BlockSpec((rows, SSD_BC), lambda c: (ci(c), D_INNER // SSD_BC + 1)),
        pl.BlockSpec((rows, 128), lambda c: (ci(c), SSD_DT_COL // 128)),
        vec, vec, vec,
        pl.BlockSpec((rows, D_INNER), lambda c: (ci(c), 0)),
        pl.BlockSpec((1, D_INNER), lambda c: (0, 0)),
    ]


def _ssd_layer_fwd(xbc, proj, dtb, alog, dsk, wn):
    L = xbc.shape[0]
    sc = min(STEP_CHUNKS, L // CHUNK)
    rows = sc * CHUNK
    nc = L // rows

    def body(xs_ref, b_ref, c_ref, dt_ref, dtb_ref, alog_ref, dsk_ref, z_ref, wn_ref, y_ref, hs_ref, hst):
        @pl.when(pl.program_id(0) == 0)
        def _():
            hst[...] = jnp.zeros(hst.shape, F32)

        h = hst[...]
        for u in range(sc):
            r = slice(u * CHUNK, (u + 1) * CHUNK)
            hs_ref[u] = h
            yn, h = _ssd_step(xs_ref[r], b_ref[r], c_ref[r], dt_ref[r], dtb_ref[...], alog_ref[...], dsk_ref[...], h,
                              z_ref[r], wn_ref[...])
            y_ref[r] = yn.astype(BF16)
        hst[...] = h

    return pl.pallas_call(
        body,
        grid=(nc,),
        in_specs=_ssd_specs(rows, nc, False),
        out_specs=[
            pl.BlockSpec((rows, D_INNER), lambda c: (c, 0)),
            pl.BlockSpec((sc, D_INNER, SSD_N), lambda c: (c, 0, 0)),
        ],
        out_shape=[
            jax.ShapeDtypeStruct((L, D_INNER), BF16),
            jax.ShapeDtypeStruct((L // CHUNK, D_INNER, SSD_N), F32),
        ],
        scratch_shapes=[pltpu.VMEM((D_INNER, SSD_N), F32)],
        compiler_params=_cparams(("arbitrary",)),
        name="ssd_layer_fwd",
    )(xbc, xbc, xbc, proj, dtb, alog, dsk, proj, wn)


def _ssd_layer_bwd(xbc, proj, dtb, alog, dsk, wn, h_saved, dyn):
    L = xbc.shape[0]
    sc = min(BWD_STEP_CHUNKS, L // CHUNK)
    rows = sc * CHUNK
    nc = L // rows

    def body(xs_ref, b_ref, c_ref, dt_ref, dtb_ref, alog_ref, dsk_ref, z_ref, wn_ref, hs_ref, dyn_ref,
             dp_ref, dx_ref, ddtb_ref, dalog_ref, ddsk_ref, dwn_ref, dhst):
        @pl.when(pl.program_id(0) == 0)
        def _():
            dhst[...] = jnp.zeros(dhst.shape, F32)
            ddtb_ref[...] = jnp.zeros((1, 128), F32)
            dalog_ref[...] = jnp.zeros((1, 128), F32)
            ddsk_ref[...] = jnp.zeros((1, 128), F32)
            dwn_ref[...] = jnp.zeros((1, D_INNER), F32)

        dh = dhst[...]
        for u in reversed(range(sc)):
            r = slice(u * CHUNK, (u + 1) * CHUNK)
            _, vjp = jax.vjp(_ssd_step, xs_ref[r], b_ref[r], c_ref[r], dt_ref[r], dtb_ref[...], alog_ref[...],
                             dsk_ref[...], hs_ref[u], z_ref[r], wn_ref[...])
            dxs, db, dc, ddt, ddtb, dalog, ddsk, dh, dz, dwn = vjp((dyn_ref[r], dh))
            dx_ref[r, :D_INNER] = dxs
            dx_ref[r, D_INNER:D_INNER + SSD_BC] = db
            dx_ref[r, D_INNER + SSD_BC:] = dc
            dp_ref[r, :D_INNER] = dz.astype(BF16)
            dp_ref[r, SSD_DT_COL:SSD_DT_COL + 128] = ddt.astype(BF16)
            ddtb_ref[...] += ddtb
            dalog_ref[...] += dalog
            ddsk_ref[...] += ddsk
            dwn_ref[...] += dwn
        dp_ref[:, D_INNER:SSD_DT_COL] = jnp.zeros((rows, SSD_CONV_DIM), BF16)
        dp_ref[:, SSD_DT_COL + 128:] = jnp.zeros((rows, SSD_PROJ_PAD - SSD_DT_COL - 128), BF16)
        dhst[...] = dh

    rc = lambda c: nc - 1 - c
    vec = pl.BlockSpec((1, 128), lambda c: (0, 0))
    vshape = jax.ShapeDtypeStruct((1, 128), F32)
    return pl.pallas_call(
        body,
        grid=(nc,),
        in_specs=_ssd_specs(rows, nc, True) + [
            pl.BlockSpec((sc, D_INNER, SSD_N), lambda c: (rc(c), 0, 0)),
            pl.BlockSpec((rows, D_INNER), lambda c: (rc(c), 0)),
        ],
        out_specs=[
            pl.BlockSpec((rows, SSD_PROJ_PAD), lambda c: (rc(c), 0)),
            pl.BlockSpec((rows, SSD_CONV_DIM), lambda c: (rc(c), 0)),
            vec, vec, vec,
            pl.BlockSpec((1, D_INNER), lambda c: (0, 0)),
        ],
        out_shape=[
            jax.ShapeDtypeStruct((L, SSD_PROJ_PAD), BF16),
            jax.ShapeDtypeStruct((L, SSD_CONV_DIM), F32),
            vshape, vshape, vshape,
            jax.ShapeDtypeStruct((1, D_INNER), F32),
        ],
        scratch_shapes=[pltpu.VMEM((D_INNER, SSD_N), F32)],
        compiler_params=_cparams(("arbitrary",)),
        name="ssd_layer_bwd",
    )(xbc, xbc, xbc, proj, dtb, alog, dsk, proj, wn, h_saved, dyn)


def _pick(n, options):
    for t in options:
        if n % t == 0:
            return t
    return n


def _token_tile(m, row_bytes, fixed_bytes):
    for t in (MM_TOKEN_TILE, MM_TOKEN_TILE // 2, MM_TOKEN_TILE // 4):
        if m % t == 0 and t * row_bytes + fixed_bytes <= MM_VMEM_BUDGET:
            return t
    return min(m, MM_TOKEN_TILE // 4)


def _comm_out(kind, arrays):
    return [jax.ShapeDtypeStruct(((N_DEV,) + p.shape) if kind == "gather" else p.shape, p.dtype) for p in arrays]


def _mm(a, b, *, name, out_dtype=F32, add=None, comm=("exchange", ())):
    kind, exchange = comm
    M, K = a.shape
    N = b.shape[1]
    tn = _pick(N, MM_TILES)
    tk = _pick(K, MM_TILES)
    nk = K // tk
    row_bytes = 2 * (tk * a.dtype.itemsize + tn * jnp.dtype(out_dtype).itemsize + (tn * 4 if add is not None else 0)) \
        + (tn * 4 if nk > 1 else 0)
    tm = _token_tile(M, row_bytes, 2 * tk * tn * b.dtype.itemsize)
    grid = (M // tm, N // tn, nk)
    n = len(exchange)
    n_in = 2 + (add is not None)

    def body(*refs):
        a_ref, b_ref = refs[:2]
        add_ref = refs[2] if add is not None else None
        p_refs = refs[n_in:n_in + n]
        o_ref = refs[n_in + n]
        out_refs = refs[n_in + n + 1:n_in + 2 * n + 1]
        acc = refs[n_in + 2 * n + 1]
        ids = [pl.program_id(d) for d in range(3)]
        k = ids[2]
        if n:
            start, finish_exchange = (_gather_ops if kind == "gather" else _exchange_ops)(
                p_refs, out_refs, *refs[n_in + 2 * n + 2:])

            @pl.when((ids[0] == 0) & (ids[1] == 0) & (k == 0))
            def _():
                start()

        p = _dot(_bf(a_ref[...]), _bf(b_ref[...]))

        def finish(r):
            if add is not None:
                r = r + add_ref[...]
            o_ref[...] = r.astype(out_dtype)

        if nk == 1:
            finish(p)
        else:
            @pl.when(k == 0)
            def _():
                acc[...] = p

            @pl.when((k > 0) & (k < nk - 1))
            def _():
                acc[...] += p

            @pl.when(k == nk - 1)
            def _():
                finish(acc[...] + p)

        if n:
            @pl.when((ids[0] == grid[0] - 1) & (ids[1] == grid[1] - 1) & (k == nk - 1))
            def _():
                finish_exchange()

    in_specs = [pl.BlockSpec((tm, tk), lambda i, j, k: (i, k)), pl.BlockSpec((tk, tn), lambda i, j, k: (k, j))]
    args = [a, b]
    if add is not None:
        in_specs.append(pl.BlockSpec((tm, tn), lambda i, j, k: (i, j)))
        args.append(add)
    res = pl.pallas_call(
        body,
        grid=grid,
        in_specs=in_specs + [ANY] * n,
        out_specs=[pl.BlockSpec((tm, tn), lambda i, j, k: (i, j))] + [ANY] * n,
        out_shape=[jax.ShapeDtypeStruct((M, N), out_dtype)] + _comm_out(kind, exchange),
        scratch_shapes=[pltpu.VMEM((tm, tn) if nk > 1 else (8, 128), F32)] + (_comm_sems(n) if n else []),
        compiler_params=_cparams(("arbitrary",) * 3 if n else ("parallel", "parallel", "arbitrary")),
        name=name,
    )(*args, *exchange)
    return (res[0], res[1:]) if n else res[0]


def _mm_tn(a, b, *, name):
    M, K = a.shape
    N = b.shape[1]
    tn = _pick(N, MM_TILES)
    tm = _token_tile(M, 2 * (K * a.dtype.itemsize + tn * b.dtype.itemsize), 2 * K * tn * 4)

    def body(a_ref, b_ref, o_ref):
        _acc_out(o_ref, _dot_tn(_bf(a_ref[...]), _bf(b_ref[...])), pl.program_id(1) == 0)

    return pl.pallas_call(
        body,
        grid=(N // tn, M // tm),
        in_specs=[pl.BlockSpec((tm, K), lambda j, i: (i, 0)), pl.BlockSpec((tm, tn), lambda j, i: (i, j))],
        out_specs=pl.BlockSpec((K, tn), lambda j, i: (0, j)),
        out_shape=jax.ShapeDtypeStruct((K, N), F32),
        compiler_params=_cparams(("parallel", "arbitrary")),
        name=name,
    )(a, b)


def _rms_fwd(x, w, *, name, gather=()):
    L, D = x.shape
    tm = min(TOKEN_TILE, L)
    nt = L // tm
    n = len(gather)

    def body(*refs):
        x_ref, w_ref = refs[:2]
        o_ref = refs[2 + n]
        if n:
            start, finish = _gather_ops(refs[2:2 + n], refs[3 + n:3 + 2 * n], *refs[3 + 2 * n:])

            @pl.when(pl.program_id(0) == 0)
            def _():
                start()

        o_ref[...] = _rms(x_ref[...], w_ref[...]).astype(BF16)
        if n:
            @pl.when(pl.program_id(0) == nt - 1)
            def _():
                finish()

    res = pl.pallas_call(
        body, grid=(nt,),
        in_specs=[pl.BlockSpec((tm, D), lambda i: (i, 0)), pl.BlockSpec((1, D), lambda i: (0, 0))] + [ANY] * n,
        out_specs=[pl.BlockSpec((tm, D), lambda i: (i, 0))] + [ANY] * n,
        out_shape=[jax.ShapeDtypeStruct((L, D), BF16)] + _comm_out("gather", gather),
        scratch_shapes=_comm_sems(n) if n else [],
        compiler_params=_cparams(("arbitrary",) if n else ("parallel",)), name=name,
    )(x, w, *gather)
    return (res[0], res[1:]) if n else res[0]


def _rms_bwd(x, w, dhn, dres, *, name):
    L, D = x.shape
    tm = min(TOKEN_TILE, L)

    def body(x_ref, w_ref, dhn_ref, dres_ref, dx_ref, dw_ref):
        _, vjp = jax.vjp(_rms, x_ref[...], w_ref[...])
        dx, dw = vjp(dhn_ref[...])
        dx_ref[...] = dx + dres_ref[...]
        _acc_out(dw_ref, dw, pl.program_id(0) == 0)

    row = pl.BlockSpec((tm, D), lambda i: (i, 0))
    vec = pl.BlockSpec((1, D), lambda i: (0, 0))
    return pl.pallas_call(
        body, grid=(L // tm,),
        in_specs=[row, vec, row, row],
        out_specs=[row, vec],
        out_shape=[jax.ShapeDtypeStruct((L, D), F32), jax.ShapeDtypeStruct((1, D), F32)],
        compiler_params=_cparams(("arbitrary",)), name=name,
    )(x, w, dhn, dres)


CONV_HALO = 8
CONV_COLS = 1024


CONV_RB = 64
CONV_CB = 256


def _conv_pieces(tm):
    return [(r0, c0) for c0 in range(0, CONV_COLS, CONV_CB) for r0 in range(0, tm, min(CONV_RB, tm))]


def _conv_fwd(proj, w, b):
    L = proj.shape[0]
    tm = min(TOKEN_TILE, L)
    rb = min(CONV_RB, tm)
    c0 = D_INNER // CONV_COLS
    hb = tm // CONV_HALO

    def body(x_ref, h_ref, w_ref, b_ref, o_ref, sp_ref, xx):
        xx[0:CONV_HALO] = jnp.where(pl.program_id(1) == 0, 0.0, h_ref[...])
        xx[CONV_HALO:CONV_HALO + tm] = x_ref[...]
        for r0, cc in _conv_pieces(tm):
            cs = slice(cc, cc + CONV_CB)
            u = b_ref[:, cs]
            for k in range(SSD_CONV):
                off = CONV_HALO - (SSD_CONV - 1) + k + r0
                u = u + w_ref[k:k + 1, cs] * xx[off:off + rb, cs]
            s = 1.0 / (1.0 + jnp.exp(-u))
            o_ref[r0:r0 + rb, cs] = u * s
            sp_ref[r0:r0 + rb, cs] = s * (1.0 + u * (1.0 - s))

    blk = pl.BlockSpec((tm, CONV_COLS), lambda j, i: (i, j))
    shp = jax.ShapeDtypeStruct((L, SSD_CONV_DIM), F32)
    return pl.pallas_call(
        body, grid=(SSD_CONV_DIM // CONV_COLS, L // tm),
        in_specs=[
            pl.BlockSpec((tm, CONV_COLS), lambda j, i: (i, c0 + j)),
            pl.BlockSpec((CONV_HALO, CONV_COLS), lambda j, i: (jnp.maximum(i * hb - 1, 0), c0 + j)),
            pl.BlockSpec((SSD_CONV, CONV_COLS), lambda j, i: (0, j)),
            pl.BlockSpec((1, CONV_COLS), lambda j, i: (0, j)),
        ],
        out_specs=[blk, blk],
        out_shape=[shp, shp],
        scratch_shapes=[pltpu.VMEM((CONV_HALO + tm, CONV_COLS), F32)],
        compiler_params=_cparams(("parallel", "parallel")), name="conv_fwd",
    )(proj, proj, w, b)


def _conv_bwd(proj, w, sp, dxbc, dproj):
    L = proj.shape[0]
    tm = min(TOKEN_TILE, L)
    rb = min(CONV_RB, tm)
    nt = L // tm
    c0 = D_INNER // CONV_COLS
    hb = tm // CONV_HALO

    def fold(a):
        return jnp.sum(a.reshape(rb // 8, 8, CONV_CB), axis=0)

    def body(x_ref, h_ref, w_ref, sp_ref, dy_ref, dp_in_ref, dp_ref, dw_ref, db_ref, xx, dd):
        del dp_in_ref
        i = pl.program_id(1)
        first = i == 0

        @pl.when(first)
        def _():
            dd[tm:tm + CONV_HALO] = jnp.zeros((CONV_HALO, CONV_COLS), F32)

        xx[0:CONV_HALO] = jnp.where(i == nt - 1, 0.0, h_ref[...])
        xx[CONV_HALO:CONV_HALO + tm] = x_ref[...]
        dws, dbs = [], []
        for cc in range(0, CONV_COLS, CONV_CB):
            cs = slice(cc, cc + CONV_CB)
            acc = [jnp.zeros((8, CONV_CB), F32) for _ in range(SSD_CONV + 1)]
            for r0 in range(0, tm, rb):
                du = dy_ref[r0:r0 + rb, cs] * sp_ref[r0:r0 + rb, cs]
                dd[r0:r0 + rb, cs] = du
                for k in range(SSD_CONV):
                    off = CONV_HALO - (SSD_CONV - 1) + k + r0
                    acc[k] = acc[k] + fold(du * xx[off:off + rb, cs])
                acc[SSD_CONV] = acc[SSD_CONV] + fold(du)
            dws.append(jnp.concatenate([jnp.sum(a, axis=0, keepdims=True) for a in acc[:SSD_CONV]], axis=0))
            dbs.append(jnp.sum(acc[SSD_CONV], axis=0, keepdims=True))
        for r0, cc in _conv_pieces(tm):
            cs = slice(cc, cc + CONV_CB)
            dx = jnp.zeros((rb, CONV_CB), F32)
            for k in range(SSD_CONV):
                off = SSD_CONV - 1 - k + r0
                dx = dx + w_ref[k:k + 1, cs] * dd[off:off + rb, cs]
            dp_ref[r0:r0 + rb, cs] = dx.astype(BF16)
        _acc_out(dw_ref, jnp.concatenate(dws, axis=1), first)
        _acc_out(db_ref, jnp.concatenate(dbs, axis=1), first)
        dd[tm:tm + CONV_HALO] = dd[0:CONV_HALO]

    rt = lambda i: nt - 1 - i
    return pl.pallas_call(
        body, grid=(SSD_CONV_DIM // CONV_COLS, nt),
        in_specs=[
            pl.BlockSpec((tm, CONV_COLS), lambda j, i: (rt(i), c0 + j)),
            pl.BlockSpec((CONV_HALO, CONV_COLS), lambda j, i: (jnp.maximum(rt(i) * hb - 1, 0), c0 + j)),
            pl.BlockSpec((SSD_CONV, CONV_COLS), lambda j, i: (0, j)),
            pl.BlockSpec((tm, CONV_COLS), lambda j, i: (rt(i), j)),
            pl.BlockSpec((tm, CONV_COLS), lambda j, i: (rt(i), j)),
            pl.BlockSpec(memory_space=pl.ANY),
        ],
        out_specs=[
            pl.BlockSpec((tm, CONV_COLS), lambda j, i: (rt(i), c0 + j)),
            pl.BlockSpec((SSD_CONV, CONV_COLS), lambda j, i: (0, j)),
            pl.BlockSpec((1, CONV_COLS), lambda j, i: (0, j)),
        ],
        out_shape=[jax.ShapeDtypeStruct((L, SSD_PROJ_PAD), BF16), jax.ShapeDtypeStruct((SSD_CONV, SSD_CONV_DIM), F32),
                   jax.ShapeDtypeStruct((1, SSD_CONV_DIM), F32)],
        scratch_shapes=[pltpu.VMEM((CONV_HALO + tm, CONV_COLS), F32), pltpu.VMEM((tm + CONV_HALO, CONV_COLS), F32)],
        input_output_aliases={5: 0},
        compiler_params=_cparams(("arbitrary", "arbitrary")), name="conv_bwd",
    )(proj, proj, w, sp, dxbc, dproj)


def _loss_bwd(x, tgt, w):
    L, D = x.shape
    tm = min(TOKEN_TILE, L)

    def body(x_ref, t_ref, w_ref, l_ref, dx_ref, dw_ref):
        xv = x_ref[...]
        wv = w_ref[...]
        r = lax.rsqrt(jnp.mean(xv * xv, axis=-1, keepdims=True) + RMS_EPS)
        xh = xv * r
        e = xh * wv - t_ref[...]
        lsum = 0.5 * jnp.sum(jnp.mean(e * e, axis=-1, keepdims=True), axis=0, keepdims=True)
        dout = e * (1.0 / D)
        gx = dout * wv
        dx_ref[...] = r * (gx - xh * jnp.mean(gx * xh, axis=-1, keepdims=True))
        first = pl.program_id(0) == 0
        _acc_out(dw_ref, jnp.sum(dout * xh, axis=0, keepdims=True), first)
        _acc_out(l_ref, jnp.broadcast_to(lsum, (8, 128)), first)

    row = pl.BlockSpec((tm, D), lambda i: (i, 0))
    vec = pl.BlockSpec((1, D), lambda i: (0, 0))
    return pl.pallas_call(
        body, grid=(L // tm,),
        in_specs=[row, row, vec],
        out_specs=[pl.BlockSpec((8, 128), lambda i: (0, 0)), row, vec],
        out_shape=[jax.ShapeDtypeStruct((8, 128), F32), jax.ShapeDtypeStruct((L, D), F32), jax.ShapeDtypeStruct((1, D), F32)],
        compiler_params=_cparams(("arbitrary",)), name="loss_bwd",
    )(x, tgt, w)


MESH = pl.DeviceIdType.MESH
ANY = pl.BlockSpec(memory_space=pl.ANY)


def _comm_sems(n):
    return [pltpu.SemaphoreType.DMA((n, 7)), pltpu.SemaphoreType.DMA((n, 7)), pltpu.SemaphoreType.DMA((n,))]


def _gather_ops(x_refs, out_refs, send_sems, recv_sems, local_sems):
    n = len(x_refs)
    x, y, c = lax.axis_index("x"), lax.axis_index("y"), lax.axis_index("c")
    me, sibling = (x, y, c), (x, y, 1 - c)
    chips = [(1 - x, y), (x, 1 - y), (1 - x, 1 - y)]

    def slot(i, px, py, pc):
        return out_refs[i].at[4 * px + 2 * py + pc]

    def copy(i, k, block, to, src=None):
        return pltpu.make_async_remote_copy(
            src_ref=slot(i, *block) if src is None else src, dst_ref=slot(i, *block),
            send_sem=send_sems.at[i, k], recv_sem=recv_sems.at[i, k], device_id=to, device_id_type=MESH)

    def own():
        mine = [pltpu.make_async_copy(x_refs[i], slot(i, *me), local_sems.at[i]) for i in range(n)]
        first = [copy(i, 0, me, sibling, src=x_refs[i]) for i in range(n)]
        first += [copy(i, 1 + j, me, (*chip, c), src=x_refs[i]) for j, chip in enumerate(chips) for i in range(n)]
        return mine, first

    def start():
        mine, first = own()
        for cp in mine + first:
            cp.start()

    def finish():
        mine, first = own()
        passed = []
        for j, chip in enumerate(chips):
            for i in range(n):
                copy(i, 1 + j, (*chip, c), me).wait_recv()
                passed.append(copy(i, 4 + j, (*chip, c), sibling))
                passed[-1].start()
        for i in range(n):
            copy(i, 0, sibling, me).wait_recv()
        for j, chip in enumerate(chips):
            for i in range(n):
                copy(i, 4 + j, (*chip, 1 - c), me).wait_recv()
        for cp in first + passed:
            cp.wait_send()
        for cp in mine:
            cp.wait()

    return start, finish


def _exchange_ops(p_refs, out_refs, send_sems, recv_sems, local_sems):
    n = len(p_refs)
    x, y, c = lax.axis_index("x"), lax.axis_index("y"), lax.axis_index("c")
    my = 4 * x + 2 * y + c

    def peer(k):
        fx, fy, fc = (k >> 2) & 1, (k >> 1) & 1, k & 1
        px, py, pc = (1 - x if fx else x), (1 - y if fy else y), (1 - c if fc else c)
        return (px, py, pc), 4 * px + 2 * py + pc

    def mine():
        return [pltpu.make_async_copy(p_refs[i].at[my], out_refs[i].at[my], local_sems.at[i]) for i in range(n)]

    def start():
        for cp in mine():
            cp.start()
        for k in range(1, N_DEV):
            to, pid = peer(k)
            for i in range(n):
                pltpu.make_async_remote_copy(
                    src_ref=p_refs[i].at[pid], dst_ref=out_refs[i].at[my], send_sem=send_sems.at[i, k - 1],
                    recv_sem=recv_sems.at[i, k - 1], device_id=to, device_id_type=MESH).start()

    def finish():
        for k in range(1, N_DEV):
            to, pid = peer(k)
            for i in range(n):
                pltpu.make_async_remote_copy(
                    src_ref=p_refs[i].at[pid], dst_ref=out_refs[i].at[pid], send_sem=send_sems.at[i, k - 1],
                    recv_sem=recv_sems.at[i, k - 1], device_id=to, device_id_type=MESH).wait()
        for cp in mine():
            cp.wait()

    return start, finish


def _all_gather(xs, *, name):
    n = len(xs)

    def body(*refs):
        start, finish = _gather_ops(refs[:n], refs[n:2 * n], *refs[2 * n:])
        start()
        finish()

    return pl.pallas_call(
        body,
        out_shape=[jax.ShapeDtypeStruct((N_DEV,) + a.shape, a.dtype) for a in xs],
        in_specs=[ANY] * n, out_specs=[ANY] * n, scratch_shapes=_comm_sems(n), name=name,
    )(*xs)


def _adamw(parts, w, m, v, *, name):
    a, b = w.shape
    tr = _pick(a, (256, 128, 64, 32, 16, 8))

    def body(p_ref, w_ref, m_ref, v_ref, g_ref, d_ref, mo_ref, vo_ref):
        g = p_ref[0]
        for s in range(1, N_DEV):
            g = g + p_ref[s]
        mn = ADAM_B1 * m_ref[...] + (1.0 - ADAM_B1) * g
        vn = ADAM_B2 * v_ref[...] + (1.0 - ADAM_B2) * jnp.square(g)
        m_hat = mn / (1.0 - ADAM_B1 ** ADAM_STEP)
        v_hat = vn / (1.0 - ADAM_B2 ** ADAM_STEP)
        g_ref[...] = g
        d_ref[...] = -ADAM_LR * (m_hat / (jnp.sqrt(v_hat) + ADAM_EPS) + ADAM_WD * w_ref[...])
        mo_ref[...] = mn
        vo_ref[...] = vn

    blk = pl.BlockSpec((tr, b), lambda i: (i, 0))
    shp = jax.ShapeDtypeStruct((a, b), F32)
    return pl.pallas_call(
        body, grid=(a // tr,),
        in_specs=[pl.BlockSpec((N_DEV, tr, b), lambda i: (0, i, 0)), blk, blk, blk],
        out_specs=[blk, blk, blk, blk],
        out_shape=[shp, shp, shp, shp],
        compiler_params=_cparams(("parallel",)), name=name,
    )(parts, w, m, v)


def _col_shards(a, n):
    return a.reshape(a.shape[0], N_DEV, n).transpose(1, 0, 2)


def _from_col_shards(g, cols):
    r = g.shape[1]
    full = g.transpose(1, 0, 2).reshape(r, -1)
    return jnp.pad(full, ((0, 0), (0, cols - full.shape[1])))


def kernel(x, norm_w, gla_in_proj, gla_gate_up, gla_gate_bias, gla_head_norm, gla_out_proj, ssd_in_proj, ssd_conv_w, ssd_conv_b, ssd_dt_bias, ssd_a_log, ssd_d, ssd_gate_norm, ssd_out_proj, final_norm, loss_target, m_norm_w, m_gla_in_proj, m_gla_gate_up, m_gla_gate_bias, m_gla_head_norm, m_gla_out_proj, m_ssd_in_proj, m_ssd_conv_w, m_ssd_conv_b, m_ssd_dt_bias, m_ssd_a_log, m_ssd_d, m_ssd_gate_norm, m_ssd_out_proj, m_final_norm, v_norm_w, v_gla_in_proj, v_gla_gate_up, v_gla_gate_bias, v_gla_head_norm, v_gla_out_proj, v_ssd_in_proj, v_ssd_conv_w, v_ssd_conv_b, v_ssd_dt_bias, v_ssd_a_log, v_ssd_d, v_ssd_gate_norm, v_ssd_out_proj, v_final_norm):
    x0 = x[0]
    tgt = loss_target[0]
    n_gin = GLA_PROJ // N_DEV
    n_sin = SSD_PROJ // N_DEV
    n_up = GLA_DK // N_DEV
    n_cv = SSD_CONV_DIM // N_DEV

    vec128 = lambda a: jnp.pad(a.reshape(1, -1), ((0, 0), (0, 128 - a.size)))
    dtb, alog, dsk = vec128(ssd_dt_bias), vec128(ssd_a_log), vec128(ssd_d)
    nw0, nw1 = norm_w[0:1], norm_w[1:2]

    hn1, (g_gin,) = _rms_fwd(x0, nw0, name="rms1_fwd", gather=[gla_in_proj[0].astype(BF16)])
    w_gin = _from_col_shards(g_gin, GLA_PROJ_PAD)
    proj1, (g_up, g_gout) = _mm(hn1, w_gin, name="gla_in_proj",
                                comm=("gather", [gla_gate_up[0].astype(BF16), gla_out_proj[0].astype(BF16)]))
    wup = jnp.pad(_from_col_shards(g_up, GLA_DK), ((0, 128 - GLA_RANK), (0, 0))).astype(F32)
    w_gout = g_gout.reshape(D_INNER, D_MODEL)
    (o, og, s_saved), (g_sin, g_sout, g_cw, g_cb, g_gn) = _gla_layer_fwd(
        proj1, wup, gla_gate_bias, gla_head_norm,
        [ssd_in_proj[0].astype(BF16), ssd_out_proj[0].astype(BF16), ssd_conv_w[0], ssd_conv_b, ssd_gate_norm])
    w_sin = _from_col_shards(g_sin, SSD_PROJ_PAD)
    w_sout = g_sout.reshape(D_INNER, D_MODEL)
    conv_w = _from_col_shards(g_cw, SSD_CONV_DIM)
    conv_b = g_cb.reshape(1, SSD_CONV_DIM)
    gate_norm = g_gn.reshape(1, D_INNER)
    x1 = _mm(og, w_gout, add=x0, name="gla_out_proj")
    hn2 = _rms_fwd(x1, nw1, name="rms2_fwd")
    proj2 = _mm(hn2, w_sin, name="ssd_in_proj")
    xbc, conv_sp = _conv_fwd(proj2, conv_w, conv_b)
    yn, h_saved = _ssd_layer_fwd(xbc, proj2, dtb, alog, dsk, gate_norm)
    x2 = _mm(yn, w_sout, add=x1, name="ssd_out_proj")
    lsum, dx2, d_final = _loss_bwd(x2, tgt, final_norm.reshape(1, D_MODEL))
    loss = lax.psum(lsum[0, 0], ("x", "y", "c"))

    d_sout = _mm_tn(yn, dx2, name="ssd_out_proj_dw")
    dyn = _mm(dx2, w_sout.T, name="ssd_out_proj_dx")
    dproj2, dxbc, d_dtb, d_alog, d_dsk, d_gate_norm = _ssd_layer_bwd(xbc, proj2, dtb, alog, dsk, gate_norm, h_saved, dyn)
    dproj2, d_conv_w, d_conv_b = _conv_bwd(proj2, conv_w, conv_sp, dxbc, dproj2)
    d_sin = _mm_tn(hn2, dproj2, name="ssd_in_proj_dw")
    dhn2 = _mm(dproj2, w_sin.T, name="ssd_in_proj_dx")
    dx1, d_nw1 = _rms_bwd(x1, nw1, dhn2, dx2, name="rms2_bwd")
    d_gout = _mm_tn(og, dx1, name="gla_out_proj_dw")
    dog = _mm(dx1, w_gout.T, name="gla_out_proj_dx")
    early = {
        "gla_out_proj": ((gla_out_proj[0], m_gla_out_proj[0], v_gla_out_proj[0]), d_gout.reshape(N_DEV, -1, D_MODEL)),
        "ssd_in_proj": ((ssd_in_proj[0], m_ssd_in_proj[0], v_ssd_in_proj[0]), _col_shards(d_sin[:, :SSD_PROJ], n_sin)),
        "ssd_conv_w": ((ssd_conv_w[0], m_ssd_conv_w[0], v_ssd_conv_w[0]), _col_shards(d_conv_w, n_cv)),
        "ssd_conv_b": ((ssd_conv_b, m_ssd_conv_b, v_ssd_conv_b), d_conv_b.reshape(N_DEV, 1, n_cv)),
        "ssd_gate_norm": ((ssd_gate_norm, m_ssd_gate_norm, v_ssd_gate_norm), d_gate_norm.reshape(N_DEV, 1, -1)),
        "ssd_out_proj": ((ssd_out_proj[0], m_ssd_out_proj[0], v_ssd_out_proj[0]), d_sout.reshape(N_DEV, -1, D_MODEL)),
    }
    (dproj1, d_wup, d_gbias, d_head_norm), early_recv = _gla_layer_bwd(
        proj1, wup, gla_gate_bias, gla_head_norm, o, s_saved, dog, [p for _, p in early.values()])
    d_gin = _mm_tn(hn1, dproj1, name="gla_in_proj_dw")
    late = {
        "gla_in_proj": ((gla_in_proj[0], m_gla_in_proj[0], v_gla_in_proj[0]), _col_shards(d_gin[:, :GLA_PROJ], n_gin)),
        "gla_gate_up": ((gla_gate_up[0], m_gla_gate_up[0], v_gla_gate_up[0]), _col_shards(d_wup[:GLA_RANK], n_up)),
    }
    dhn1, late_recv = _mm(dproj1, w_gin.T, name="gla_in_proj_dx", comm=("exchange", [p for _, p in late.values()]))
    dx0, d_nw0 = _rms_bwd(x0, nw0, dhn1, dx1, name="rms1_bwd")
    heads = SSD_HEADS
    replicated = {
        "norm_w": ((norm_w, m_norm_w, v_norm_w), jnp.concatenate([d_nw0, d_nw1], axis=0)),
        "gla_gate_bias": ((gla_gate_bias, m_gla_gate_bias, v_gla_gate_bias), d_gbias),
        "gla_head_norm": ((gla_head_norm, m_gla_head_norm, v_gla_head_norm), d_head_norm),
        "ssd_dt_bias": ((ssd_dt_bias, m_ssd_dt_bias, v_ssd_dt_bias), d_dtb[:, :heads]),
        "ssd_a_log": ((ssd_a_log, m_ssd_a_log, v_ssd_a_log), d_alog[:, :heads]),
        "ssd_d": ((ssd_d, m_ssd_d, v_ssd_d), d_dsk[:, :heads]),
        "final_norm": (tuple(t.reshape(1, D_MODEL) for t in (final_norm, m_final_norm, v_final_norm)), d_final),
    }
    results = {}
    for group, recv in ((early, early_recv),
                        (late, late_recv),
                        (replicated, _all_gather([p for _, p in replicated.values()], name="replicated_gather"))):
        for (nm, ((w, m, v), _)), r in zip(group.items(), recv):
            results[nm] = _adamw(r, w, m, v, name=nm + "_adamw")

    order = [("norm_w", norm_w), ("gla_in_proj", gla_in_proj), ("gla_gate_up", gla_gate_up), ("gla_gate_bias", gla_gate_bias),
             ("gla_head_norm", gla_head_norm), ("gla_out_proj", gla_out_proj), ("ssd_in_proj", ssd_in_proj),
             ("ssd_conv_w", ssd_conv_w), ("ssd_conv_b", ssd_conv_b), ("ssd_dt_bias", ssd_dt_bias), ("ssd_a_log", ssd_a_log),
             ("ssd_d", ssd_d), ("ssd_gate_norm", ssd_gate_norm), ("ssd_out_proj", ssd_out_proj), ("final_norm", final_norm)]
    out = [loss, dx0[None]]
    for i in range(4):
        out += [results[nm][i].reshape(ref.shape) for nm, ref in order]
    return tuple(out)
```

```python
import jax
import jax.numpy as jnp
from jax import lax
from jax.experimental import pallas as pl
from jax.experimental.pallas import tpu as pltpu

F32 = jnp.float32
BF16 = jnp.bfloat16

D_MODEL = 1024
D_INNER = 2048
RMS_EPS = 1e-6
GLA_HEADS = 4
GLA_DK = 512
GLA_HEAD_K = 128
GLA_HEAD_V = 512
GLA_RANK = 16
GLA_NORMALIZER = 16.0
CHUNK = 64
SUB = 16
STEP_CHUNKS = 4
BWD_STEP_CHUNKS = 1
GLA_PROJ = 5136
GLA_PROJ_PAD = 5376
GLA_GK_COL = 5120
SSD_HEADS = 32
SSD_GROUPS = 8
SSD_HPG = 4
SSD_P = 64
SSD_N = 128
SSD_CONV = 4
SSD_CONV_DIM = 4096
SSD_PROJ = 6176
SSD_PROJ_PAD = 6400
SSD_DT_COL = 6144
N_DEV = 8

ADAM_LR = 0.001
ADAM_B1 = 0.9
ADAM_B2 = 0.999
ADAM_EPS = 1e-08
ADAM_WD = 0.01
ADAM_STEP = 10

VMEM_LIMIT = 56 * 1024 * 1024
TOKEN_TILE = 512
MM_TOKEN_TILE = 2048
MM_VMEM_BUDGET = 44 * 1024 * 1024
MM_TILES = (2048, 1792, 1280, 1024, 768, 512, 256, 128)


def _dot(a, b):
    return jnp.dot(a, b, preferred_element_type=F32)


def _dot_nt(a, b):
    return lax.dot_general(a, b, (((1,), (1,)), ((), ())), preferred_element_type=F32)


def _dot_tn(a, b):
    return lax.dot_general(a, b, (((0,), (0,)), ((), ())), preferred_element_type=F32)


def _bf(a):
    return a.astype(BF16)


@jax.custom_vjp
def _mxu(a, b):
    return _dot(_bf(a), _bf(b))


def _mxu_fwd(a, b):
    return _mxu(a, b), (a, b)


def _mxu_bwd(res, g):
    a, b = res
    return _dot_nt(_bf(g), _bf(b)), _dot_tn(_bf(a), _bf(g))


_mxu.defvjp(_mxu_fwd, _mxu_bwd)


@jax.custom_vjp
def _mxu_nt(a, b):
    return _dot_nt(_bf(a), _bf(b))


def _mxu_nt_fwd(a, b):
    return _mxu_nt(a, b), (a, b)


def _mxu_nt_bwd(res, g):
    a, b = res
    return _dot(_bf(g), _bf(b)), _dot_tn(_bf(g), _bf(a))


_mxu_nt.defvjp(_mxu_nt_fwd, _mxu_nt_bwd)


@jax.custom_vjp
def _mxu_tn(a, b):
    return _dot_tn(_bf(a), _bf(b))


def _mxu_tn_fwd(a, b):
    return _mxu_tn(a, b), (a, b)


def _mxu_tn_bwd(res, g):
    a, b = res
    return _dot_nt(_bf(b), _bf(g)), _dot(_bf(a), _bf(g))


_mxu_tn.defvjp(_mxu_tn_fwd, _mxu_tn_bwd)


def _split2(a):
    hi = _bf(a)
    return hi, _bf(a - hi.astype(F32))


def _three_pass(dot, a, b):
    ah, al = _split2(a)
    bh, bl = _split2(b)
    return dot(ah, bh) + (dot(ah, bl) + dot(al, bh))


@jax.custom_vjp
def _dot3_nt(a, b):
    return _three_pass(_dot_nt, a, b)


def _dot3_nt_fwd(a, b):
    return _dot3_nt(a, b), (a, b)


def _dot3_nt_bwd(res, g):
    a, b = res
    return _three_pass(_dot, g, b), _three_pass(_dot_tn, g, a)


_dot3_nt.defvjp(_dot3_nt_fwd, _dot3_nt_bwd)


def _silu(x):
    return x / (1.0 + jnp.exp(-x))


def _log_sigmoid(z):
    return jnp.minimum(z, 0.0) - jnp.log(1.0 + jnp.exp(-jnp.abs(z)))


def _softplus(z):
    return jnp.maximum(z, 0.0) + jnp.log(1.0 + jnp.exp(-jnp.abs(z)))


def _iota(shape, dim):
    return lax.broadcasted_iota(jnp.int32, shape, dim)


def _rms(x, w):
    return x * lax.rsqrt(jnp.mean(x * x, axis=-1, keepdims=True) + RMS_EPS) * w


def _scan_rows(a, reverse, seg):
    n = a.shape[0]
    pos = _iota(a.shape, 0) & (seg - 1)
    sh = 1
    while sh < seg:
        if reverse:
            a = a + jnp.where(pos < seg - sh, pltpu.roll(a, n - sh, 0), 0.0)
        else:
            a = a + jnp.where(pos >= sh, pltpu.roll(a, sh, 0), 0.0)
        sh *= 2
    return a


def _make_cumsum(seg):
    @jax.custom_vjp
    def cumsum(a):
        return _scan_rows(a, False, seg)

    cumsum.defvjp(lambda a: (_scan_rows(a, False, seg), None), lambda _, g: (_scan_rows(g, True, seg),))
    return cumsum


_cumsum_sub = _make_cumsum(SUB)
_cumsum_rows = _make_cumsum(CHUNK)


def _cparams(sem):
    return pltpu.CompilerParams(dimension_semantics=sem, vmem_limit_bytes=VMEM_LIMIT)


def _acc_out(ref, val, first):
    @pl.when(first)
    def _():
        ref[...] = val

    @pl.when(jnp.logical_not(first))
    def _():
        ref[...] += val


def _gla_chunk(q, k, va, vb, gk, wup, bias, sts):
    nb = CHUNK // SUB
    heads = range(GLA_HEADS)
    hc = lambda a, h: a[:, h * GLA_HEAD_K:(h + 1) * GLA_HEAD_K]
    v = [(va if h < 2 else vb)[:, (h % 2) * GLA_HEAD_V:(h % 2 + 1) * GLA_HEAD_V] for h in heads]
    z = _mxu(gk, wup) + bias
    la = _log_sigmoid(z) * (1.0 / GLA_NORMALIZER)
    qs = q * (GLA_HEAD_K ** -0.5)
    bl = _cumsum_sub(la)
    tot = [jnp.sum(la[i * SUB:(i + 1) * SUB], axis=0, keepdims=True) for i in range(nb)]
    pre = [jnp.zeros((1, GLA_DK), F32)]
    for i in range(nb):
        pre.append(pre[i] + tot[i])
    b_last = pre[nb]
    rows_of = lambda vals: jnp.concatenate([jnp.broadcast_to(t, (SUB, GLA_DK)) for t in vals], axis=0)
    suf = rows_of(tot) - bl
    nxt = rows_of(pre[1:])
    q_in = qs * jnp.exp(bl + rows_of(pre[:nb]))
    k_end = k * jnp.exp(suf + (b_last - nxt))
    dec = jnp.exp(b_last)
    qa = qs * jnp.exp(bl)
    o_inter = [_mxu_nt(hc(q_in, h), sts[h]) for h in heads]
    sts_new = tuple(sts[h] * hc(dec, h) + _mxu_tn(v[h], hc(k_end, h)) for h in heads)
    half = SUB // 2
    rs = _iota((SUB, GLA_HEAD_K), 0)
    cs = _iota((half, CHUNK), 1)
    a_rows = [[] for _ in heads]
    for i in range(nb):
        sl = slice(i * SUB, (i + 1) * SUB)
        n = i * SUB
        if i > 0:
            kp = jnp.concatenate([k[:n] * jnp.exp(suf[:n] + (pre[i] - nxt[:n])), jnp.zeros((CHUNK - n, GLA_DK), F32)], axis=0)
        for h in heads:
            q_i, k_i, bl_i = hc(qs[sl], h), hc(k[sl], h), hc(bl[sl], h)
            if i > 0:
                a_i = _dot3_nt(hc(qa[sl], h), hc(kp, h))
                a_top, a_bot = a_i[:half], a_i[half:]
            else:
                a_top = a_bot = jnp.zeros((half, CHUNK), F32)
            for j in range(SUB):
                lo = 0 if j < half else half
                e = jnp.exp(jnp.minimum(bl_i[lo:] - bl_i[j:j + 1], 0.0))
                t = jnp.where(rs[lo:] >= j, q_i[lo:] * e * k_i[j:j + 1], 0.0)
                rsum = jnp.sum(t, axis=-1, keepdims=True)
                hit = cs == i * SUB + j
                if lo == 0:
                    a_top = a_top + jnp.where(hit, rsum[:half], 0.0)
                a_bot = a_bot + jnp.where(hit, rsum[half - lo:], 0.0)
            a_rows[h] += [a_top, a_bot]
    o = [o_inter[h] + _mxu(jnp.concatenate(a_rows[h], axis=0), v[h]) for h in heads]
    return jnp.concatenate(o, axis=1), sts_new


def _gla_post(o, g, wn):
    return _rms(o, wn) * _silu(g)


GLA_HALF = 2 * GLA_HEAD_V


def _gla_specs(rows, nc, rev):
    ci = (lambda c: nc - 1 - c) if rev else (lambda c: c)
    v0 = 2 * GLA_DK // GLA_HALF
    g0 = (2 * GLA_DK + D_INNER) // GLA_HALF
    return [
        pl.BlockSpec((rows, GLA_DK), lambda c: (ci(c), 0)),
        pl.BlockSpec((rows, GLA_DK), lambda c: (ci(c), 1)),
        pl.BlockSpec((rows, GLA_HALF), lambda c: (ci(c), v0)),
        pl.BlockSpec((rows, GLA_HALF), lambda c: (ci(c), v0 + 1)),
        pl.BlockSpec((rows, GLA_HALF), lambda c: (ci(c), g0)),
        pl.BlockSpec((rows, GLA_HALF), lambda c: (ci(c), g0 + 1)),
        pl.BlockSpec((rows, 128), lambda c: (ci(c), GLA_GK_COL // 128)),
        pl.BlockSpec((128, GLA_DK), lambda c: (0, 0)),
        pl.BlockSpec((1, GLA_DK), lambda c: (0, 0)),
        pl.BlockSpec((1, GLA_HEAD_V), lambda c: (0, 0)),
    ]


def _head_cols(ref_a, ref_b, h, rows=slice(None)):
    ref = ref_a if h < 2 else ref_b
    return ref[rows, (h % 2) * GLA_HEAD_V:(h % 2 + 1) * GLA_HEAD_V]


def _gla_layer_fwd(proj, wup, bias, wn, gather):
    L = proj.shape[0]
    sc = min(STEP_CHUNKS, L // CHUNK)
    rows = sc * CHUNK
    nc = L // rows
    n = len(gather)

    def body(*refs):
        q_ref, k_ref, va_ref, vb_ref, ga_ref, gb_ref, gk_ref, wup_ref, b_ref, wn_ref = refs[:10]
        x_refs = refs[10:10 + n]
        o_ref, og_ref, s_ref = refs[10 + n:13 + n]
        out_refs = refs[13 + n:13 + 2 * n]
        st, send_sems, recv_sems, local_sems = refs[13 + 2 * n:]
        start, finish = _gather_ops(x_refs, out_refs, send_sems, recv_sems, local_sems)

        @pl.when(pl.program_id(0) == 0)
        def _():
            st[...] = jnp.zeros(st.shape, F32)
            start()

        s_cur = tuple(st[h] for h in range(GLA_HEADS))
        for u in range(sc):
            r = slice(u * CHUNK, (u + 1) * CHUNK)
            for h in range(GLA_HEADS):
                s_ref[u, h] = s_cur[h]
            o, s_cur = _gla_chunk(q_ref[r], k_ref[r], va_ref[r], vb_ref[r], gk_ref[r], wup_ref[...], b_ref[...], s_cur)
            o_ref[r] = o
            for h in range(GLA_HEADS):
                vc = slice(h * GLA_HEAD_V, (h + 1) * GLA_HEAD_V)
                og_ref[r, vc] = _gla_post(o[:, vc], _head_cols(ga_ref, gb_ref, h, r), wn_ref[...]).astype(BF16)
        for h in range(GLA_HEADS):
            st[h] = s_cur[h]

        @pl.when(pl.program_id(0) == nc - 1)
        def _():
            finish()

    res = pl.pallas_call(
        body,
        grid=(nc,),
        in_specs=_gla_specs(rows, nc, False) + [ANY] * n,
        out_specs=[
            pl.BlockSpec((rows, D_INNER), lambda c: (c, 0)),
            pl.BlockSpec((rows, D_INNER), lambda c: (c, 0)),
            pl.BlockSpec((sc, GLA_HEADS, GLA_HEAD_V, GLA_HEAD_K), lambda c: (c, 0, 0, 0)),
        ] + [ANY] * n,
        out_shape=[
            jax.ShapeDtypeStruct((L, D_INNER), F32),
            jax.ShapeDtypeStruct((L, D_INNER), BF16),
            jax.ShapeDtypeStruct((L // CHUNK, GLA_HEADS, GLA_HEAD_V, GLA_HEAD_K), F32),
        ] + [jax.ShapeDtypeStruct((N_DEV,) + a.shape, a.dtype) for a in gather],
        scratch_shapes=[pltpu.VMEM((GLA_HEADS, GLA_HEAD_V, GLA_HEAD_K), F32)] + _comm_sems(n),
        compiler_params=_cparams(("arbitrary",)),
        name="gla_layer_fwd",
    )(proj, proj, proj, proj, proj, proj, proj, wup, bias, wn, *gather)
    return res[:3], res[3:]


def _gla_layer_bwd(proj, wup, bias, wn, o, s_in, dog, exchange):
    L = proj.shape[0]
    nc = L // CHUNK
    n = len(exchange)

    def body(*refs):
        (q_ref, k_ref, va_ref, vb_ref, ga_ref, gb_ref, gk_ref, wup_ref, b_ref, wn_ref, o_ref, s_ref, dog_ref) = refs[:13]
        p_refs = refs[13:13 + n]
        dp_ref, dwup_ref, db_ref, dwn_ref = refs[13 + n:17 + n]
        out_refs = refs[17 + n:17 + 2 * n]
        dst, send_sems, recv_sems, local_sems = refs[17 + 2 * n:]
        start, finish = _exchange_ops(p_refs, out_refs, send_sems, recv_sems, local_sems)

        @pl.when(pl.program_id(0) == 0)
        def _():
            dst[...] = jnp.zeros(dst.shape, F32)
            dwup_ref[...] = jnp.zeros(dwup_ref.shape, F32)
            db_ref[...] = jnp.zeros(db_ref.shape, F32)
            dwn_ref[...] = jnp.zeros(dwn_ref.shape, F32)
            start()

        dos = []
        for h in range(GLA_HEADS):
            vc = slice(h * GLA_HEAD_V, (h + 1) * GLA_HEAD_V)
            _, post_vjp = jax.vjp(_gla_post, o_ref[:, vc], _head_cols(ga_ref, gb_ref, h), wn_ref[...])
            do, dg, dwn = post_vjp(dog_ref[:, vc])
            dos.append(do)
            dp_ref[:, 2 * GLA_DK + D_INNER + h * GLA_HEAD_V:2 * GLA_DK + D_INNER + (h + 1) * GLA_HEAD_V] = dg.astype(BF16)
            dwn_ref[...] += dwn
        _, vjp = jax.vjp(_gla_chunk, q_ref[...], k_ref[...], va_ref[...], vb_ref[...], gk_ref[...], wup_ref[...], b_ref[...],
                         tuple(s_ref[h] for h in range(GLA_HEADS)))
        dq, dk, dva, dvb, dgk, dwup, db, ds = vjp((jnp.concatenate(dos, axis=1), tuple(dst[h] for h in range(GLA_HEADS))))
        for h in range(GLA_HEADS):
            dst[h] = ds[h]
        dp_ref[:, :GLA_DK] = dq.astype(BF16)
        dp_ref[:, GLA_DK:2 * GLA_DK] = dk.astype(BF16)
        dp_ref[:, 2 * GLA_DK:2 * GLA_DK + GLA_HALF] = dva.astype(BF16)
        dp_ref[:, 2 * GLA_DK + GLA_HALF:2 * GLA_DK + D_INNER] = dvb.astype(BF16)
        dwup_ref[...] += dwup
        db_ref[...] += db
        dp_ref[:, GLA_GK_COL:GLA_GK_COL + 128] = dgk.astype(BF16)
        dp_ref[:, GLA_GK_COL + 128:] = jnp.zeros((CHUNK, GLA_PROJ_PAD - GLA_GK_COL - 128), BF16)

        @pl.when(pl.program_id(0) == nc - 1)
        def _():
            finish()

    rc = lambda c: nc - 1 - c
    res = pl.pallas_call(
        body,
        grid=(nc,),
        in_specs=_gla_specs(CHUNK, nc, True) + [
            pl.BlockSpec((CHUNK, D_INNER), lambda c: (rc(c), 0)),
            pl.BlockSpec((None, GLA_HEADS, GLA_HEAD_V, GLA_HEAD_K), lambda c: (rc(c), 0, 0, 0)),
            pl.BlockSpec((CHUNK, D_INNER), lambda c: (rc(c), 0)),
        ] + [ANY] * n,
        out_specs=[
            pl.BlockSpec((CHUNK, GLA_PROJ_PAD), lambda c: (rc(c), 0)),
            pl.BlockSpec((128, GLA_DK), lambda c: (0, 0)),
            pl.BlockSpec((1, GLA_DK), lambda c: (0, 0)),
            pl.BlockSpec((1, GLA_HEAD_V), lambda c: (0, 0)),
        ] + [ANY] * n,
        out_shape=[
            jax.ShapeDtypeStruct((L, GLA_PROJ_PAD), BF16),
            jax.ShapeDtypeStruct((128, GLA_DK), F32),
            jax.ShapeDtypeStruct((1, GLA_DK), F32),
            jax.ShapeDtypeStruct((1, GLA_HEAD_V), F32),
        ] + [jax.ShapeDtypeStruct(a.shape, a.dtype) for a in exchange],
        scratch_shapes=[pltpu.VMEM((GLA_HEADS, GLA_HEAD_V, GLA_HEAD_K), F32)] + _comm_sems(n),
        compiler_params=_cparams(("arbitrary",)),
        name="gla_layer_bwd",
    )(proj, proj, proj, proj, proj, proj, proj, wup, bias, wn, o, s_in, dog, *exchange)
    return res[:4], res[4:]


@jax.custom_vjp
def _expand(v):
    r = v.shape[0]
    left = _iota((r, 128), 1) < SSD_P
    slabs = []
    for p in range(SSD_HEADS // 2):
        a = jnp.broadcast_to(v[:, 2 * p:2 * p + 1], (r, 128))
        b = jnp.broadcast_to(v[:, 2 * p + 1:2 * p + 2], (r, 128))
        slabs.append(jnp.where(left, a, b))
    return jnp.concatenate(slabs, axis=1)


def _expand_fwd(v):
    return _expand(v), None


def _expand_bwd(_, g):
    r = g.shape[0]
    lane = _iota((r, 128), 1)
    left = lane < SSD_P
    dv = jnp.zeros((r, 128), F32)
    for p in range(SSD_HEADS // 2):
        gs = g[:, 128 * p:128 * (p + 1)]
        sa = jnp.sum(jnp.where(left, gs, 0.0), axis=-1, keepdims=True)
        sb = jnp.sum(jnp.where(left, 0.0, gs), axis=-1, keepdims=True)
        dv = dv + jnp.where(lane == 2 * p, sa, 0.0) + jnp.where(lane == 2 * p + 1, sb, 0.0)
    return (dv,)


_expand.defvjp(_expand_fwd, _expand_bwd)

SSD_GW = SSD_HPG * SSD_P
SSD_BC = SSD_GROUPS * SSD_N
SSD_PHASE = 4


def _ssd_chunk(xs, Bm, Cm, dtp, dtb, alog, dsk, h_in):
    dt = _softplus(dtp + dtb)
    acum = _cumsum_rows(dt * (-jnp.exp(alog)))
    a_last = acum[CHUNK - 1:CHUNK]
    acum_b = _expand(acum)
    w_end = _expand(dt * jnp.exp(a_last - acum))
    d_b = _expand(jnp.broadcast_to(dsk, (8, 128)))[0:1]
    ac_t = jnp.concatenate([acum, acum], axis=0).T
    dt_t = jnp.concatenate([dt, dt], axis=0).T
    lane = _iota((CHUNK, 128), 1)
    left = lane < SSD_P
    causal = (lane & (SSD_P - 1)) <= _iota((CHUNK, 128), 0)
    cd = jnp.exp(ac_t[:, CHUNK - 1:CHUNK])
    ys, h_out = [], []
    for g0 in range(0, SSD_GROUPS, SSD_PHASE):
        cb2, y_off = {}, {}
        for g in range(g0, g0 + SSD_PHASE):
            Bg = Bm[:, g * SSD_N:(g + 1) * SSD_N]
            Cg = Cm[:, g * SSD_N:(g + 1) * SSD_N]
            gs = slice(g * SSD_GW, (g + 1) * SSD_GW)
            cb2[g] = _mxu_nt(Cg, jnp.concatenate([Bg, Bg], axis=0))
            y_off[g] = _mxu_nt(Cg, h_in[gs])
            st = _mxu_tn(xs[:, gs] * w_end[:, gs], Bg)
            hs = [h_in[h * SSD_P:(h + 1) * SSD_P] * cd[h:h + 1] for h in range(g * SSD_HPG, (g + 1) * SSD_HPG)]
            h_out.append(jnp.concatenate(hs, axis=0) + st)
        for p in range(g0 * (SSD_HPG // 2), (g0 + SSD_PHASE) * (SSD_HPG // 2)):
            g, k = divmod(p, SSD_HPG // 2)
            sl = slice(128 * p, 128 * (p + 1))
            ac_c = acum_b[:, sl]
            ac_r = jnp.where(left, ac_t[2 * p:2 * p + 1], ac_t[2 * p + 1:2 * p + 2])
            dt_r = jnp.where(left, dt_t[2 * p:2 * p + 1], dt_t[2 * p + 1:2 * p + 2])
            m2 = cb2[g] * jnp.where(causal, jnp.exp(jnp.minimum(ac_c - ac_r, 0.0)), 0.0) * dt_r
            xsl = xs[:, sl]
            x2 = jnp.concatenate([jnp.where(left, xsl, 0.0), jnp.where(left, 0.0, xsl)], axis=0)
            ys.append(_mxu(m2, x2) + y_off[g][:, 128 * k:128 * (k + 1)] * jnp.exp(ac_c) + xsl * d_b[:, sl])
    return jnp.concatenate(ys, axis=1), jnp.concatenate(h_out, axis=0)


def _ssd_step(xs, Bm, Cm, dtp, dtb, alog, dsk, h_in, z, wn):
    ys, h = [], h_in
    for u in range(xs.shape[0] // CHUNK):
        r = slice(u * CHUNK, (u + 1) * CHUNK)
        y, h = _ssd_chunk(xs[r], Bm[r], Cm[r], dtp[r], dtb, alog, dsk, h)
        ys.append(_rms(y * _silu(z[r]), wn))
    return jnp.concatenate(ys, axis=0), h


def _ssd_specs(rows, nc, rev):
    ci = (lambda c: nc - 1 - c) if rev else (lambda c: c)
    vec = pl.BlockSpec((1, 128), lambda c: (0, 0))
    return [
        pl.BlockSpec((rows, D_INNER), lambda c: (ci(c), 0)),
        pl.BlockSpec((rows, SSD_BC), lambda c: (ci(c), D_INNER // SSD_BC)),
        pl.BlockSpec((rows, SSD_BC), lambda c: (ci(c), D_INNER // SSD_BC + 1)),
        pl.BlockSpec((rows, 128), lambda c: (ci(c), SSD_DT_COL // 128)),
        vec, vec, vec,
        pl.BlockSpec((rows, D_INNER), lambda c: (ci(c), 0)),
        pl.BlockSpec((1, D_INNER), lambda c: (0, 0)),
    ]


def _ssd_layer_fwd(xbc, proj, dtb, alog, dsk, wn):
    L = xbc.shape[0]
    sc = min(STEP_CHUNKS, L // CHUNK)
    rows = sc * CHUNK
    nc = L // rows

    def body(xs_ref, b_ref, c_ref, dt_ref, dtb_ref, alog_ref, dsk_ref, z_ref, wn_ref, y_ref, hs_ref, hst):
        @pl.when(pl.program_id(0) == 0)
        def _():
            hst[...] = jnp.zeros(hst.shape, F32)

        h = hst[...]
        for u in range(sc):
            r = slice(u * CHUNK, (u + 1) * CHUNK)
            hs_ref[u] = h
            yn, h = _ssd_step(xs_ref[r], b_ref[r], c_ref[r], dt_ref[r], dtb_ref[...], alog_ref[...], dsk_ref[...], h,
                              z_ref[r], wn_ref[...])
            y_ref[r] = yn.astype(BF16)
        hst[...] = h

    return pl.pallas_call(
        body,
        grid=(nc,),
        in_specs=_ssd_specs(rows, nc, False),
        out_specs=[
            pl.BlockSpec((rows, D_INNER), lambda c: (c, 0)),
            pl.BlockSpec((sc, D_INNER, SSD_N), lambda c: (c, 0, 0)),
        ],
        out_shape=[
            jax.ShapeDtypeStruct((L, D_INNER), BF16),
            jax.ShapeDtypeStruct((L // CHUNK, D_INNER, SSD_N), F32),
        ],
        scratch_shapes=[pltpu.VMEM((D_INNER, SSD_N), F32)],
        compiler_params=_cparams(("arbitrary",)),
        name="ssd_layer_fwd",
    )(xbc, xbc, xbc, proj, dtb, alog, dsk, proj, wn)


def _ssd_layer_bwd(xbc, proj, dtb, alog, dsk, wn, h_saved, dyn):
    L = xbc.shape[0]
    sc = min(BWD_STEP_CHUNKS, L // CHUNK)
    rows = sc * CHUNK
    nc = L // rows

    def body(xs_ref, b_ref, c_ref, dt_ref, dtb_ref, alog_ref, dsk_ref, z_ref, wn_ref, hs_ref, dyn_ref,
             dp_ref, dx_ref, ddtb_ref, dalog_ref, ddsk_ref, dwn_ref, dhst):
        @pl.when(pl.program_id(0) == 0)
        def _():
            dhst[...] = jnp.zeros(dhst.shape, F32)
            ddtb_ref[...] = jnp.zeros((1, 128), F32)
            dalog_ref[...] = jnp.zeros((1, 128), F32)
            ddsk_ref[...] = jnp.zeros((1, 128), F32)
            dwn_ref[...] = jnp.zeros((1, D_INNER), F32)

        dh = dhst[...]
        for u in reversed(range(sc)):
            r = slice(u * CHUNK, (u + 1) * CHUNK)
            _, vjp = jax.vjp(_ssd_step, xs_ref[r], b_ref[r], c_ref[r], dt_ref[r], dtb_ref[...], alog_ref[...],
                             dsk_ref[...], hs_ref[u], z_ref[r], wn_ref[...])
            dxs, db, dc, ddt, ddtb, dalog, ddsk, dh, dz, dwn = vjp((dyn_ref[r], dh))
            dx_ref[r, :D_INNER] = dxs
            dx_ref[r, D_INNER:D_INNER + SSD_BC] = db
            dx_ref[r, D_INNER + SSD_BC:] = dc
            dp_ref[r, :D_INNER] = dz.astype(BF16)
            dp_ref[r, SSD_DT_COL:SSD_DT_COL + 128] = ddt.astype(BF16)
            ddtb_ref[...] += ddtb
            dalog_ref[...] += dalog
            ddsk_ref[...] += ddsk
            dwn_ref[...] += dwn
        dp_ref[:, D_INNER:SSD_DT_COL] = jnp.zeros((rows, SSD_CONV_DIM), BF16)
        dp_ref[:, SSD_DT_COL + 128:] = jnp.zeros((rows, SSD_PROJ_PAD - SSD_DT_COL - 128), BF16)
        dhst[...] = dh

    rc = lambda c: nc - 1 - c
    vec = pl.BlockSpec((1, 128), lambda c: (0, 0))
    vshape = jax.ShapeDtypeStruct((1, 128), F32)
    return pl.pallas_call(
        body,
        grid=(nc,),
        in_specs=_ssd_specs(rows, nc, True) + [
            pl.BlockSpec((sc, D_INNER, SSD_N), lambda c: (rc(c), 0, 0)),
            pl.BlockSpec((rows, D_INNER), lambda c: (rc(c), 0)),
        ],
        out_specs=[
            pl.BlockSpec((rows, SSD_PROJ_PAD), lambda c: (rc(c), 0)),
            pl.BlockSpec((rows, SSD_CONV_DIM), lambda c: (rc(c), 0)),
            vec, vec, vec,
            pl.BlockSpec((1, D_INNER), lambda c: (0, 0)),
        ],
        out_shape=[
            jax.ShapeDtypeStruct((L, SSD_PROJ_PAD), BF16),
            jax.ShapeDtypeStruct((L, SSD_CONV_DIM), F32),
            vshape, vshape, vshape,
            jax.ShapeDtypeStruct((1, D_INNER), F32),
        ],
        scratch_shapes=[pltpu.VMEM((D_INNER, SSD_N), F32)],
        compiler_params=_cparams(("arbitrary",)),
        name="ssd_layer_bwd",
    )(xbc, xbc, xbc, proj, dtb, alog, dsk, proj, wn, h_saved, dyn)


def _pick(n, options):
    for t in options:
        if n % t == 0:
            return t
    return n


def _token_tile(m, row_bytes, fixed_bytes):
    for t in (MM_TOKEN_TILE, MM_TOKEN_TILE // 2, MM_TOKEN_TILE // 4):
        if m % t == 0 and t * row_bytes + fixed_bytes <= MM_VMEM_BUDGET:
            return t
    return min(m, MM_TOKEN_TILE // 4)


def _comm_out(kind, arrays):
    return [jax.ShapeDtypeStruct(((N_DEV,) + p.shape) if kind == "gather" else p.shape, p.dtype) for p in arrays]


def _mm(a, b, *, name, out_dtype=F32, add=None, rms_bwd=None, comm=("exchange", ())):
    kind, exchange = comm
    M, K = a.shape
    N = b.shape[1]
    tn = _pick(N, MM_TILES)
    tk = _pick(K, MM_TILES)
    nk = K // tk
    n_side = (add is not None) + (3 if rms_bwd is not None else 0)
    row_bytes = 2 * (tk * a.dtype.itemsize + tn * jnp.dtype(out_dtype).itemsize + n_side * tn * 4) \
        + (tn * 4 if nk > 1 else 0)
    tm = _token_tile(M, row_bytes, 2 * tk * tn * b.dtype.itemsize)
    grid = (M // tm, N // tn, nk)
    n = len(exchange)
    if rms_bwd is not None:
        assert tn == N and add is None, "the rmsnorm epilogue needs whole rows and takes the residual gradient itself"
    n_in = 2 + (add is not None) + (3 if rms_bwd is not None else 0)
    n_out = 1 + (rms_bwd is not None)

    def body(*refs):
        a_ref, b_ref = refs[:2]
        add_ref = refs[2] if add is not None else None
        p_refs = refs[n_in:n_in + n]
        o_ref = refs[n_in + n]
        out_refs = refs[n_in + n + n_out:n_in + 2 * n + n_out]
        acc = refs[n_in + 2 * n + n_out]
        ids = [pl.program_id(d) for d in range(3)]
        k = ids[2]
        if n:
            start, finish_exchange = (_gather_ops if kind == "gather" else _exchange_ops)(
                p_refs, out_refs, *refs[n_in + 2 * n + n_out + 1:])

            @pl.when((ids[0] == 0) & (ids[1] == 0) & (k == 0))
            def _():
                start()

        p = _dot(_bf(a_ref[...]), _bf(b_ref[...]))

        def finish(r):
            if rms_bwd is not None:
                x_ref, w_ref, dres_ref = refs[2:5]
                _, vjp = jax.vjp(_rms, x_ref[...], w_ref[...])
                dx, dw = vjp(r)
                o_ref[...] = dx + dres_ref[...]
                _acc_out(refs[n_in + n + 1], dw, ids[0] == 0)
                return
            if add is not None:
                r = r + add_ref[...]
            o_ref[...] = r.astype(out_dtype)

        if nk == 1:
            finish(p)
        else:
            @pl.when(k == 0)
            def _():
                acc[...] = p

            @pl.when((k > 0) & (k < nk - 1))
            def _():
                acc[...] += p

            @pl.when(k == nk - 1)
            def _():
                finish(acc[...] + p)

        if n:
            @pl.when((ids[0] == grid[0] - 1) & (ids[1] == grid[1] - 1) & (k == nk - 1))
            def _():
                finish_exchange()

    tile = pl.BlockSpec((tm, tn), lambda i, j, k: (i, j))
    vec = pl.BlockSpec((1, tn), lambda i, j, k: (0, j))
    in_specs = [pl.BlockSpec((tm, tk), lambda i, j, k: (i, k)), pl.BlockSpec((tk, tn), lambda i, j, k: (k, j))]
    args = [a, b]
    out_specs, out_shape = [tile], [jax.ShapeDtypeStruct((M, N), out_dtype)]
    if add is not None:
        in_specs.append(tile)
        args.append(add)
    if rms_bwd is not None:
        in_specs += [tile, vec, tile]
        args += list(rms_bwd)
        out_specs.append(vec)
        out_shape.append(jax.ShapeDtypeStruct((1, N), F32))
    sequential = n or rms_bwd is not None
    res = pl.pallas_call(
        body,
        grid=grid,
        in_specs=in_specs + [ANY] * n,
        out_specs=out_specs + [ANY] * n,
        out_shape=out_shape + _comm_out(kind, exchange),
        scratch_shapes=[pltpu.VMEM((tm, tn) if nk > 1 else (8, 128), F32)] + (_comm_sems(n) if n else []),
        compiler_params=_cparams(("arbitrary",) * 3 if sequential else ("parallel", "parallel", "arbitrary")),
        name=name,
    )(*args, *exchange)
    main = res[0] if rms_bwd is None else (res[0], res[1])
    return (main, res[n_out:]) if n else main


def _mm_tn(a, b, *, name):
    M, K = a.shape
    N = b.shape[1]
    tn = _pick(N, MM_TILES)
    tm = _token_tile(M, 2 * (K * a.dtype.itemsize + tn * b.dtype.itemsize), 2 * K * tn * 4)

    def body(a_ref, b_ref, o_ref):
        _acc_out(o_ref, _dot_tn(_bf(a_ref[...]), _bf(b_ref[...])), pl.program_id(1) == 0)

    return pl.pallas_call(
        body,
        grid=(N // tn, M // tm),
        in_specs=[pl.BlockSpec((tm, K), lambda j, i: (i, 0)), pl.BlockSpec((tm, tn), lambda j, i: (i, j))],
        out_specs=pl.BlockSpec((K, tn), lambda j, i: (0, j)),
        out_shape=jax.ShapeDtypeStruct((K, N), F32),
        compiler_params=_cparams(("parallel", "arbitrary")),
        name=name,
    )(a, b)


def _rms_fwd(x, w, *, name, gather=()):
    L, D = x.shape
    tm = min(TOKEN_TILE, L)
    nt = L // tm
    n = len(gather)

    def body(*refs):
        x_ref, w_ref = refs[:2]
        o_ref = refs[2 + n]
        if n:
            start, finish = _gather_ops(refs[2:2 + n], refs[3 + n:3 + 2 * n], *refs[3 + 2 * n:])

            @pl.when(pl.program_id(0) == 0)
            def _():
                start()

        o_ref[...] = _rms(x_ref[...], w_ref[...]).astype(BF16)
        if n:
            @pl.when(pl.program_id(0) == nt - 1)
            def _():
                finish()

    res = pl.pallas_call(
        body, grid=(nt,),
        in_specs=[pl.BlockSpec((tm, D), lambda i: (i, 0)), pl.BlockSpec((1, D), lambda i: (0, 0))] + [ANY] * n,
        out_specs=[pl.BlockSpec((tm, D), lambda i: (i, 0))] + [ANY] * n,
        out_shape=[jax.ShapeDtypeStruct((L, D), BF16)] + _comm_out("gather", gather),
        scratch_shapes=_comm_sems(n) if n else [],
        compiler_params=_cparams(("arbitrary",) if n else ("parallel",)), name=name,
    )(x, w, *gather)
    return (res[0], res[1:]) if n else res[0]


CONV_HALO = 8
CONV_COLS = 1024


CONV_RB = 64
CONV_CB = 256


def _conv_pieces(tm):
    return [(r0, c0) for c0 in range(0, CONV_COLS, CONV_CB) for r0 in range(0, tm, min(CONV_RB, tm))]


def _conv_fwd(proj, w, b):
    L = proj.shape[0]
    tm = min(TOKEN_TILE, L)
    rb = min(CONV_RB, tm)
    c0 = D_INNER // CONV_COLS
    hb = tm // CONV_HALO

    def body(x_ref, h_ref, w_ref, b_ref, o_ref, sp_ref, xx):
        xx[0:CONV_HALO] = jnp.where(pl.program_id(1) == 0, 0.0, h_ref[...])
        xx[CONV_HALO:CONV_HALO + tm] = x_ref[...]
        for r0, cc in _conv_pieces(tm):
            cs = slice(cc, cc + CONV_CB)
            u = b_ref[:, cs]
            for k in range(SSD_CONV):
                off = CONV_HALO - (SSD_CONV - 1) + k + r0
                u = u + w_ref[k:k + 1, cs] * xx[off:off + rb, cs]
            s = 1.0 / (1.0 + jnp.exp(-u))
            o_ref[r0:r0 + rb, cs] = u * s
            sp_ref[r0:r0 + rb, cs] = s * (1.0 + u * (1.0 - s))

    blk = pl.BlockSpec((tm, CONV_COLS), lambda j, i: (i, j))
    shp = jax.ShapeDtypeStruct((L, SSD_CONV_DIM), F32)
    return pl.pallas_call(
        body, grid=(SSD_CONV_DIM // CONV_COLS, L // tm),
        in_specs=[
            pl.BlockSpec((tm, CONV_COLS), lambda j, i: (i, c0 + j)),
            pl.BlockSpec((CONV_HALO, CONV_COLS), lambda j, i: (jnp.maximum(i * hb - 1, 0), c0 + j)),
            pl.BlockSpec((SSD_CONV, CONV_COLS), lambda j, i: (0, j)),
            pl.BlockSpec((1, CONV_COLS), lambda j, i: (0, j)),
        ],
        out_specs=[blk, blk],
        out_shape=[shp, shp],
        scratch_shapes=[pltpu.VMEM((CONV_HALO + tm, CONV_COLS), F32)],
        compiler_params=_cparams(("parallel", "parallel")), name="conv_fwd",
    )(proj, proj, w, b)


def _conv_bwd(proj, w, sp, dxbc, dproj):
    L = proj.shape[0]
    tm = min(TOKEN_TILE, L)
    rb = min(CONV_RB, tm)
    nt = L // tm
    c0 = D_INNER // CONV_COLS
    hb = tm // CONV_HALO

    def fold(a):
        return jnp.sum(a.reshape(rb // 8, 8, CONV_CB), axis=0)

    def body(x_ref, h_ref, w_ref, sp_ref, dy_ref, dp_in_ref, dp_ref, dw_ref, db_ref, xx, dd):
        del dp_in_ref
        i = pl.program_id(1)
        first = i == 0

        @pl.when(first)
        def _():
            dd[tm:tm + CONV_HALO] = jnp.zeros((CONV_HALO, CONV_COLS), F32)

        xx[0:CONV_HALO] = jnp.where(i == nt - 1, 0.0, h_ref[...])
        xx[CONV_HALO:CONV_HALO + tm] = x_ref[...]
        dws, dbs = [], []
        for cc in range(0, CONV_COLS, CONV_CB):
            cs = slice(cc, cc + CONV_CB)
            acc = [jnp.zeros((8, CONV_CB), F32) for _ in range(SSD_CONV + 1)]
            for r0 in range(0, tm, rb):
                du = dy_ref[r0:r0 + rb, cs] * sp_ref[r0:r0 + rb, cs]
                dd[r0:r0 + rb, cs] = du
                for k in range(SSD_CONV):
                    off = CONV_HALO - (SSD_CONV - 1) + k + r0
                    acc[k] = acc[k] + fold(du * xx[off:off + rb, cs])
                acc[SSD_CONV] = acc[SSD_CONV] + fold(du)
            dws.append(jnp.concatenate([jnp.sum(a, axis=0, keepdims=True) for a in acc[:SSD_CONV]], axis=0))
            dbs.append(jnp.sum(acc[SSD_CONV], axis=0, keepdims=True))
        for r0, cc in _conv_pieces(tm):
            cs = slice(cc, cc + CONV_CB)
            dx = jnp.zeros((rb, CONV_CB), F32)
            for k in range(SSD_CONV):
                off = SSD_CONV - 1 - k + r0
                dx = dx + w_ref[k:k + 1, cs] * dd[off:off + rb, cs]
            dp_ref[r0:r0 + rb, cs] = dx.astype(BF16)
        _acc_out(dw_ref, jnp.concatenate(dws, axis=1), first)
        _acc_out(db_ref, jnp.concatenate(dbs, axis=1), first)
        dd[tm:tm + CONV_HALO] = dd[0:CONV_HALO]

    rt = lambda i: nt - 1 - i
    return pl.pallas_call(
        body, grid=(SSD_CONV_DIM // CONV_COLS, nt),
        in_specs=[
            pl.BlockSpec((tm, CONV_COLS), lambda j, i: (rt(i), c0 + j)),
            pl.BlockSpec((CONV_HALO, CONV_COLS), lambda j, i: (jnp.maximum(rt(i) * hb - 1, 0), c0 + j)),
            pl.BlockSpec((SSD_CONV, CONV_COLS), lambda j, i: (0, j)),
            pl.BlockSpec((tm, CONV_COLS), lambda j, i: (rt(i), j)),
            pl.BlockSpec((tm, CONV_COLS), lambda j, i: (rt(i), j)),
            pl.BlockSpec(memory_space=pl.ANY),
        ],
        out_specs=[
            pl.BlockSpec((tm, CONV_COLS), lambda j, i: (rt(i), c0 + j)),
            pl.BlockSpec((SSD_CONV, CONV_COLS), lambda j, i: (0, j)),
            pl.BlockSpec((1, CONV_COLS), lambda j, i: (0, j)),
        ],
        out_shape=[jax.ShapeDtypeStruct((L, SSD_PROJ_PAD), BF16), jax.ShapeDtypeStruct((SSD_CONV, SSD_CONV_DIM), F32),
                   jax.ShapeDtypeStruct((1, SSD_CONV_DIM), F32)],
        scratch_shapes=[pltpu.VMEM((CONV_HALO + tm, CONV_COLS), F32), pltpu.VMEM((tm + CONV_HALO, CONV_COLS), F32)],
        input_output_aliases={5: 0},
        compiler_params=_cparams(("arbitrary", "arbitrary")), name="conv_bwd",
    )(proj, proj, w, sp, dxbc, dproj)


def _loss_bwd(x, tgt, w):
    L, D = x.shape
    tm = min(TOKEN_TILE, L)

    def body(x_ref, t_ref, w_ref, l_ref, dx_ref, dw_ref):
        xv = x_ref[...]
        wv = w_ref[...]
        r = lax.rsqrt(jnp.mean(xv * xv, axis=-1, keepdims=True) + RMS_EPS)
        xh = xv * r
        e = xh * wv - t_ref[...]
        lsum = 0.5 * jnp.sum(jnp.mean(e * e, axis=-1, keepdims=True), axis=0, keepdims=True)
        dout = e * (1.0 / D)
        gx = dout * wv
        dx_ref[...] = r * (gx - xh * jnp.mean(gx * xh, axis=-1, keepdims=True))
        first = pl.program_id(0) == 0
        _acc_out(dw_ref, jnp.sum(dout * xh, axis=0, keepdims=True), first)
        _acc_out(l_ref, jnp.broadcast_to(lsum, (8, 128)), first)

    row = pl.BlockSpec((tm, D), lambda i: (i, 0))
    vec = pl.BlockSpec((1, D), lambda i: (0, 0))
    return pl.pallas_call(
        body, grid=(L // tm,),
        in_specs=[row, row, vec],
        out_specs=[pl.BlockSpec((8, 128), lambda i: (0, 0)), row, vec],
        out_shape=[jax.ShapeDtypeStruct((8, 128), F32), jax.ShapeDtypeStruct((L, D), F32), jax.ShapeDtypeStruct((1, D), F32)],
        compiler_params=_cparams(("arbitrary",)), name="loss_bwd",
    )(x, tgt, w)


MESH = pl.DeviceIdType.MESH
ANY = pl.BlockSpec(memory_space=pl.ANY)


def _comm_sems(n):
    return [pltpu.SemaphoreType.DMA((n, 7)), pltpu.SemaphoreType.DMA((n, 7)), pltpu.SemaphoreType.DMA((n,))]


def _gather_ops(x_refs, out_refs, send_sems, recv_sems, local_sems):
    n = len(x_refs)
    x, y, c = lax.axis_index("x"), lax.axis_index("y"), lax.axis_index("c")
    me, sibling = (x, y, c), (x, y, 1 - c)
    chips = [(1 - x, y), (x, 1 - y), (1 - x, 1 - y)]

    def slot(i, px, py, pc):
        return out_refs[i].at[4 * px + 2 * py + pc]

    def copy(i, k, block, to, src=None):
        return pltpu.make_async_remote_copy(
            src_ref=slot(i, *block) if src is None else src, dst_ref=slot(i, *block),
            send_sem=send_sems.at[i, k], recv_sem=recv_sems.at[i, k], device_id=to, device_id_type=MESH)

    def own():
        mine = [pltpu.make_async_copy(x_refs[i], slot(i, *me), local_sems.at[i]) for i in range(n)]
        first = [copy(i, 0, me, sibling, src=x_refs[i]) for i in range(n)]
        first += [copy(i, 1 + j, me, (*chip, c), src=x_refs[i]) for j, chip in enumerate(chips) for i in range(n)]
        return mine, first

    def start():
        mine, first = own()
        for cp in mine + first:
            cp.start()

    def finish():
        mine, first = own()
        passed = []
        for j, chip in enumerate(chips):
            for i in range(n):
                copy(i, 1 + j, (*chip, c), me).wait_recv()
                passed.append(copy(i, 4 + j, (*chip, c), sibling))
                passed[-1].start()
        for i in range(n):
            copy(i, 0, sibling, me).wait_recv()
        for j, chip in enumerate(chips):
            for i in range(n):
                copy(i, 4 + j, (*chip, 1 - c), me).wait_recv()
        for cp in first + passed:
            cp.wait_send()
        for cp in mine:
            cp.wait()

    return start, finish


def _exchange_ops(p_refs, out_refs, send_sems, recv_sems, local_sems):
    n = len(p_refs)
    x, y, c = lax.axis_index("x"), lax.axis_index("y"), lax.axis_index("c")
    my = 4 * x + 2 * y + c

    def peer(k):
        fx, fy, fc = (k >> 2) & 1, (k >> 1) & 1, k & 1
        px, py, pc = (1 - x if fx else x), (1 - y if fy else y), (1 - c if fc else c)
        return (px, py, pc), 4 * px + 2 * py + pc

    def mine():
        return [pltpu.make_async_copy(p_refs[i].at[my], out_refs[i].at[my], local_sems.at[i]) for i in range(n)]

    def start():
        for cp in mine():
            cp.start()
        for k in range(1, N_DEV):
            to, pid = peer(k)
            for i in range(n):
                pltpu.make_async_remote_copy(
                    src_ref=p_refs[i].at[pid], dst_ref=out_refs[i].at[my], send_sem=send_sems.at[i, k - 1],
                    recv_sem=recv_sems.at[i, k - 1], device_id=to, device_id_type=MESH).start()

    def finish():
        for k in range(1, N_DEV):
            to, pid = peer(k)
            for i in range(n):
                pltpu.make_async_remote_copy(
                    src_ref=p_refs[i].at[pid], dst_ref=out_refs[i].at[pid], send_sem=send_sems.at[i, k - 1],
                    recv_sem=recv_sems.at[i, k - 1], device_id=to, device_id_type=MESH).wait()
        for cp in mine():
            cp.wait()

    return start, finish


def _all_gather(xs, *, name):
    n = len(xs)

    def body(*refs):
        start, finish = _gather_ops(refs[:n], refs[n:2 * n], *refs[2 * n:])
        start()
        finish()

    return pl.pallas_call(
        body,
        out_shape=[jax.ShapeDtypeStruct((N_DEV,) + a.shape, a.dtype) for a in xs],
        in_specs=[ANY] * n, out_specs=[ANY] * n, scratch_shapes=_comm_sems(n), name=name,
    )(*xs)


def _adamw(parts, w, m, v, *, name):
    a, b = w.shape
    tr = _pick(a, (256, 128, 64, 32, 16, 8))

    def body(p_ref, w_ref, m_ref, v_ref, g_ref, d_ref, mo_ref, vo_ref):
        g = p_ref[0]
        for s in range(1, N_DEV):
            g = g + p_ref[s]
        mn = ADAM_B1 * m_ref[...] + (1.0 - ADAM_B1) * g
        vn = ADAM_B2 * v_ref[...] + (1.0 - ADAM_B2) * jnp.square(g)
        m_hat = mn / (1.0 - ADAM_B1 ** ADAM_STEP)
        v_hat = vn / (1.0 - ADAM_B2 ** ADAM_STEP)
        g_ref[...] = g
        d_ref[...] = -ADAM_LR * (m_hat / (jnp.sqrt(v_hat) + ADAM_EPS) + ADAM_WD * w_ref[...])
        mo_ref[...] = mn
        vo_ref[...] = vn

    blk = pl.BlockSpec((tr, b), lambda i: (i, 0))
    shp = jax.ShapeDtypeStruct((a, b), F32)
    return pl.pallas_call(
        body, grid=(a // tr,),
        in_specs=[pl.BlockSpec((N_DEV, tr, b), lambda i: (0, i, 0)), blk, blk, blk],
        out_specs=[blk, blk, blk, blk],
        out_shape=[shp, shp, shp, shp],
        compiler_params=_cparams(("parallel",)), name=name,
    )(parts, w, m, v)


def _col_shards(a, n):
    return a.reshape(a.shape[0], N_DEV, n).transpose(1, 0, 2)


def _from_col_shards(g, cols):
    r = g.shape[1]
    full = g.transpose(1, 0, 2).reshape(r, -1)
    return jnp.pad(full, ((0, 0), (0, cols - full.shape[1])))


def kernel(x, norm_w, gla_in_proj, gla_gate_up, gla_gate_bias, gla_head_norm, gla_out_proj, ssd_in_proj, ssd_conv_w, ssd_conv_b, ssd_dt_bias, ssd_a_log, ssd_d, ssd_gate_norm, ssd_out_proj, final_norm, loss_target, m_norm_w, m_gla_in_proj, m_gla_gate_up, m_gla_gate_bias, m_gla_head_norm, m_gla_out_proj, m_ssd_in_proj, m_ssd_conv_w, m_ssd_conv_b, m_ssd_dt_bias, m_ssd_a_log, m_ssd_d, m_ssd_gate_norm, m_ssd_out_proj, m_final_norm, v_norm_w, v_gla_in_proj, v_gla_gate_up, v_gla_gate_bias, v_gla_head_norm, v_gla_out_proj, v_ssd_in_proj, v_ssd_conv_w, v_ssd_conv_b, v_ssd_dt_bias, v_ssd_a_log, v_ssd_d, v_ssd_gate_norm, v_ssd_out_proj, v_final_norm):
    x0 = x[0]
    tgt = loss_target[0]
    n_gin = GLA_PROJ // N_DEV
    n_sin = SSD_PROJ // N_DEV
    n_up = GLA_DK // N_DEV
    n_cv = SSD_CONV_DIM // N_DEV

    vec128 = lambda a: jnp.pad(a.reshape(1, -1), ((0, 0), (0, 128 - a.size)))
    dtb, alog, dsk = vec128(ssd_dt_bias), vec128(ssd_a_log), vec128(ssd_d)
    nw0, nw1 = norm_w[0:1], norm_w[1:2]

    hn1, (g_gin,) = _rms_fwd(x0, nw0, name="rms1_fwd", gather=[gla_in_proj[0].astype(BF16)])
    w_gin = _from_col_shards(g_gin, GLA_PROJ_PAD)
    proj1, (g_up, g_gout) = _mm(hn1, w_gin, name="gla_in_proj",
                                comm=("gather", [gla_gate_up[0].astype(BF16), gla_out_proj[0].astype(BF16)]))
    wup = jnp.pad(_from_col_shards(g_up, GLA_DK), ((0, 128 - GLA_RANK), (0, 0))).astype(F32)
    w_gout = g_gout.reshape(D_INNER, D_MODEL)
    (o, og, s_saved), (g_sin, g_sout, g_cw, g_cb, g_gn) = _gla_layer_fwd(
        proj1, wup, gla_gate_bias, gla_head_norm,
        [ssd_in_proj[0].astype(BF16), ssd_out_proj[0].astype(BF16), ssd_conv_w[0], ssd_conv_b, ssd_gate_norm])
    w_sin = _from_col_shards(g_sin, SSD_PROJ_PAD)
    w_sout = g_sout.reshape(D_INNER, D_MODEL)
    conv_w = _from_col_shards(g_cw, SSD_CONV_DIM)
    conv_b = g_cb.reshape(1, SSD_CONV_DIM)
    gate_norm = g_gn.reshape(1, D_INNER)
    x1 = _mm(og, w_gout, add=x0, name="gla_out_proj")
    hn2 = _rms_fwd(x1, nw1, name="rms2_fwd")
    proj2 = _mm(hn2, w_sin, name="ssd_in_proj")
    xbc, conv_sp = _conv_fwd(proj2, conv_w, conv_b)
    yn, h_saved = _ssd_layer_fwd(xbc, proj2, dtb, alog, dsk, gate_norm)
    x2 = _mm(yn, w_sout, add=x1, name="ssd_out_proj")
    lsum, dx2, d_final = _loss_bwd(x2, tgt, final_norm.reshape(1, D_MODEL))
    loss = lax.psum(lsum[0, 0], ("x", "y", "c"))

    d_sout = _mm_tn(yn, dx2, name="ssd_out_proj_dw")
    dyn = _mm(dx2, w_sout.T, name="ssd_out_proj_dx")
    dproj2, dxbc, d_dtb, d_alog, d_dsk, d_gate_norm = _ssd_layer_bwd(xbc, proj2, dtb, alog, dsk, gate_norm, h_saved, dyn)
    dproj2, d_conv_w, d_conv_b = _conv_bwd(proj2, conv_w, conv_sp, dxbc, dproj2)
    d_sin = _mm_tn(hn2, dproj2, name="ssd_in_proj_dw")
    dx1, d_nw1 = _mm(dproj2, w_sin.T, name="ssd_in_proj_dx", rms_bwd=(x1, nw1, dx2))
    d_gout = _mm_tn(og, dx1, name="gla_out_proj_dw")
    dog = _mm(dx1, w_gout.T, name="gla_out_proj_dx")
    early = {
        "gla_out_proj": ((gla_out_proj[0], m_gla_out_proj[0], v_gla_out_proj[0]), d_gout.reshape(N_DEV, -1, D_MODEL)),
        "ssd_in_proj": ((ssd_in_proj[0], m_ssd_in_proj[0], v_ssd_in_proj[0]), _col_shards(d_sin[:, :SSD_PROJ], n_sin)),
        "ssd_conv_w": ((ssd_conv_w[0], m_ssd_conv_w[0], v_ssd_conv_w[0]), _col_shards(d_conv_w, n_cv)),
        "ssd_conv_b": ((ssd_conv_b, m_ssd_conv_b, v_ssd_conv_b), d_conv_b.reshape(N_DEV, 1, n_cv)),
        "ssd_gate_norm": ((ssd_gate_norm, m_ssd_gate_norm, v_ssd_gate_norm), d_gate_norm.reshape(N_DEV, 1, -1)),
        "ssd_out_proj": ((ssd_out_proj[0], m_ssd_out_proj[0], v_ssd_out_proj[0]), d_sout.reshape(N_DEV, -1, D_MODEL)),
    }
    (dproj1, d_wup, d_gbias, d_head_norm), early_recv = _gla_layer_bwd(
        proj1, wup, gla_gate_bias, gla_head_norm, o, s_saved, dog, [p for _, p in early.values()])
    d_gin = _mm_tn(hn1, dproj1, name="gla_in_proj_dw")
    late = {
        "gla_in_proj": ((gla_in_proj[0], m_gla_in_proj[0], v_gla_in_proj[0]), _col_shards(d_gin[:, :GLA_PROJ], n_gin)),
        "gla_gate_up": ((gla_gate_up[0], m_gla_gate_up[0], v_gla_gate_up[0]), _col_shards(d_wup[:GLA_RANK], n_up)),
    }
    (dx0, d_nw0), late_recv = _mm(dproj1, w_gin.T, name="gla_in_proj_dx", rms_bwd=(x0, nw0, dx1),
                                  comm=("exchange", [p for _, p in late.values()]))
    heads = SSD_HEADS
    replicated = {
        "norm_w": ((norm_w, m_norm_w, v_norm_w), jnp.concatenate([d_nw0, d_nw1], axis=0)),
        "gla_gate_bias": ((gla_gate_bias, m_gla_gate_bias, v_gla_gate_bias), d_gbias),
        "gla_head_norm": ((gla_head_norm, m_gla_head_norm, v_gla_head_norm), d_head_norm),
        "ssd_dt_bias": ((ssd_dt_bias, m_ssd_dt_bias, v_ssd_dt_bias), d_dtb[:, :heads]),
        "ssd_a_log": ((ssd_a_log, m_ssd_a_log, v_ssd_a_log), d_alog[:, :heads]),
        "ssd_d": ((ssd_d, m_ssd_d, v_ssd_d), d_dsk[:, :heads]),
        "final_norm": (tuple(t.reshape(1, D_MODEL) for t in (final_norm, m_final_norm, v_final_norm)), d_final),
    }
    results = {}
    for group, recv in ((early, early_recv),
                        (late, late_recv),
                        (replicated, _all_gather([p for _, p in replicated.values()], name="replicated_gather"))):
        for (nm, ((w, m, v), _)), r in zip(group.items(), recv):
            results[nm] = _adamw(r, w, m, v, name=nm + "_adamw")

    order = [("norm_w", norm_w), ("gla_in_proj", gla_in_proj), ("gla_gate_up", gla_gate_up), ("gla_gate_bias", gla_gate_bias),
             ("gla_head_norm", gla_head_norm), ("gla_out_proj", gla_out_proj), ("ssd_in_proj", ssd_in_proj),
             ("ssd_conv_w", ssd_conv_w), ("ssd_conv_b", ssd_conv_b), ("ssd_dt_bias", ssd_dt_bias), ("ssd_a_log", ssd_a_log),
             ("ssd_d", ssd_d), ("ssd_gate_norm", ssd_gate_norm), ("ssd_out_proj", ssd_out_proj), ("final_norm", final_norm)]
    out = [loss, dx0[None]]
    for i in range(4):
        out += [results[nm][i].reshape(ref.shape) for nm, ref in order]
    return tuple(out)
```

```python
import jax
import jax.numpy as jnp
from jax import lax
from jax.experimental import pallas as pl
from jax.experimental.pallas import tpu as pltpu

F32 = jnp.float32
BF16 = jnp.bfloat16

D_MODEL = 1024
D_INNER = 2048
RMS_EPS = 1e-6
GLA_HEADS = 4
GLA_DK = 512
GLA_HEAD_K = 128
GLA_HEAD_V = 512
GLA_RANK = 16
GLA_NORMALIZER = 16.0
CHUNK = 64
SUB = 16
STEP_CHUNKS = 8
BWD_STEP_CHUNKS = 1
GLA_PROJ = 5136
GLA_PROJ_PAD = 5376
GLA_GK_COL = 5120
SSD_HEADS = 32
SSD_GROUPS = 8
SSD_HPG = 4
SSD_P = 64
SSD_N = 128
SSD_CONV = 4
SSD_CONV_DIM = 4096
SSD_PROJ = 6176
SSD_PROJ_PAD = 6400
SSD_DT_COL = 6144
N_DEV = 8

ADAM_LR = 0.001
ADAM_B1 = 0.9
ADAM_B2 = 0.999
ADAM_EPS = 1e-08
ADAM_WD = 0.01
ADAM_STEP = 10

VMEM_LIMIT = 56 * 1024 * 1024
TOKEN_TILE = 512
MM_TOKEN_TILE = 2048
MM_VMEM_BUDGET = 44 * 1024 * 1024
MM_TILES = (2048, 1792, 1280, 1024, 768, 512, 256, 128)


def _dot(a, b):
    return jnp.dot(a, b, preferred_element_type=F32)


def _dot_nt(a, b):
    return lax.dot_general(a, b, (((1,), (1,)), ((), ())), preferred_element_type=F32)


def _dot_tn(a, b):
    return lax.dot_general(a, b, (((0,), (0,)), ((), ())), preferred_element_type=F32)


def _bf(a):
    return a.astype(BF16)


@jax.custom_vjp
def _mxu(a, b):
    return _dot(_bf(a), _bf(b))


def _mxu_fwd(a, b):
    return _mxu(a, b), (a, b)


def _mxu_bwd(res, g):
    a, b = res
    return _dot_nt(_bf(g), _bf(b)), _dot_tn(_bf(a), _bf(g))


_mxu.defvjp(_mxu_fwd, _mxu_bwd)


@jax.custom_vjp
def _mxu_nt(a, b):
    return _dot_nt(_bf(a), _bf(b))


def _mxu_nt_fwd(a, b):
    return _mxu_nt(a, b), (a, b)


def _mxu_nt_bwd(res, g):
    a, b = res
    return _dot(_bf(g), _bf(b)), _dot_tn(_bf(g), _bf(a))


_mxu_nt.defvjp(_mxu_nt_fwd, _mxu_nt_bwd)


@jax.custom_vjp
def _mxu_tn(a, b):
    return _dot_tn(_bf(a), _bf(b))


def _mxu_tn_fwd(a, b):
    return _mxu_tn(a, b), (a, b)


def _mxu_tn_bwd(res, g):
    a, b = res
    return _dot_nt(_bf(b), _bf(g)), _dot(_bf(a), _bf(g))


_mxu_tn.defvjp(_mxu_tn_fwd, _mxu_tn_bwd)


def _split2(a):
    hi = _bf(a)
    return hi, _bf(a - hi.astype(F32))


def _three_pass(dot, a, b):
    ah, al = _split2(a)
    bh, bl = _split2(b)
    return dot(ah, bh) + (dot(ah, bl) + dot(al, bh))


@jax.custom_vjp
def _dot3_nt(a, b):
    return _three_pass(_dot_nt, a, b)


def _dot3_nt_fwd(a, b):
    return _dot3_nt(a, b), (a, b)


def _dot3_nt_bwd(res, g):
    a, b = res
    return _three_pass(_dot, g, b), _three_pass(_dot_tn, g, a)


_dot3_nt.defvjp(_dot3_nt_fwd, _dot3_nt_bwd)


def _silu(x):
    return x / (1.0 + jnp.exp(-x))


def _log_sigmoid(z):
    return jnp.minimum(z, 0.0) - jnp.log(1.0 + jnp.exp(-jnp.abs(z)))


def _softplus(z):
    return jnp.maximum(z, 0.0) + jnp.log(1.0 + jnp.exp(-jnp.abs(z)))


def _iota(shape, dim):
    return lax.broadcasted_iota(jnp.int32, shape, dim)


def _rms(x, w):
    return x * lax.rsqrt(jnp.mean(x * x, axis=-1, keepdims=True) + RMS_EPS) * w


def _scan_rows(a, reverse, seg):
    n = a.shape[0]
    pos = _iota(a.shape, 0) & (seg - 1)
    sh = 1
    while sh < seg:
        if reverse:
            a = a + jnp.where(pos < seg - sh, pltpu.roll(a, n - sh, 0), 0.0)
        else:
            a = a + jnp.where(pos >= sh, pltpu.roll(a, sh, 0), 0.0)
        sh *= 2
    return a


def _make_cumsum(seg):
    @jax.custom_vjp
    def cumsum(a):
        return _scan_rows(a, False, seg)

    cumsum.defvjp(lambda a: (_scan_rows(a, False, seg), None), lambda _, g: (_scan_rows(g, True, seg),))
    return cumsum


_cumsum_sub = _make_cumsum(SUB)
_cumsum_rows = _make_cumsum(CHUNK)


def _cparams(sem):
    return pltpu.CompilerParams(dimension_semantics=sem, vmem_limit_bytes=VMEM_LIMIT)


def _acc_out(ref, val, first):
    @pl.when(first)
    def _():
        ref[...] = val

    @pl.when(jnp.logical_not(first))
    def _():
        ref[...] += val


def _gla_chunk(q, k, va, vb, gk, wup, bias, sts):
    nb = CHUNK // SUB
    heads = range(GLA_HEADS)
    hc = lambda a, h: a[:, h * GLA_HEAD_K:(h + 1) * GLA_HEAD_K]
    v = [(va if h < 2 else vb)[:, (h % 2) * GLA_HEAD_V:(h % 2 + 1) * GLA_HEAD_V] for h in heads]
    z = _mxu(gk, wup) + bias
    la = _log_sigmoid(z) * (1.0 / GLA_NORMALIZER)
    qs = q * (GLA_HEAD_K ** -0.5)
    bl = _cumsum_sub(la)
    tot = [jnp.sum(la[i * SUB:(i + 1) * SUB], axis=0, keepdims=True) for i in range(nb)]
    pre = [jnp.zeros((1, GLA_DK), F32)]
    for i in range(nb):
        pre.append(pre[i] + tot[i])
    b_last = pre[nb]
    rows_of = lambda vals: jnp.concatenate([jnp.broadcast_to(t, (SUB, GLA_DK)) for t in vals], axis=0)
    suf = rows_of(tot) - bl
    nxt = rows_of(pre[1:])
    q_in = qs * jnp.exp(bl + rows_of(pre[:nb]))
    k_end = k * jnp.exp(suf + (b_last - nxt))
    dec = jnp.exp(b_last)
    qa = qs * jnp.exp(bl)
    o_inter = [_mxu_nt(hc(q_in, h), sts[h]) for h in heads]
    sts_new = tuple(sts[h] * hc(dec, h) + _mxu_tn(v[h], hc(k_end, h)) for h in heads)
    half = SUB // 2
    rs = _iota((SUB, GLA_HEAD_K), 0)
    cs = _iota((half, CHUNK), 1)
    a_rows = [[] for _ in heads]
    for i in range(nb):
        sl = slice(i * SUB, (i + 1) * SUB)
        n = i * SUB
        if i > 0:
            kp = jnp.concatenate([k[:n] * jnp.exp(suf[:n] + (pre[i] - nxt[:n])), jnp.zeros((CHUNK - n, GLA_DK), F32)], axis=0)
        for h in heads:
            q_i, k_i, bl_i = hc(qs[sl], h), hc(k[sl], h), hc(bl[sl], h)
            if i > 0:
                a_i = _dot3_nt(hc(qa[sl], h), hc(kp, h))
                a_top, a_bot = a_i[:half], a_i[half:]
            else:
                a_top = a_bot = jnp.zeros((half, CHUNK), F32)
            for j in range(SUB):
                lo = 0 if j < half else half
                e = jnp.exp(jnp.minimum(bl_i[lo:] - bl_i[j:j + 1], 0.0))
                t = jnp.where(rs[lo:] >= j, q_i[lo:] * e * k_i[j:j + 1], 0.0)
                rsum = jnp.sum(t, axis=-1, keepdims=True)
                hit = cs == i * SUB + j
                if lo == 0:
                    a_top = a_top + jnp.where(hit, rsum[:half], 0.0)
                a_bot = a_bot + jnp.where(hit, rsum[half - lo:], 0.0)
            a_rows[h] += [a_top, a_bot]
    o = [o_inter[h] + _mxu(jnp.concatenate(a_rows[h], axis=0), v[h]) for h in heads]
    return jnp.concatenate(o, axis=1), sts_new


def _gla_post(o, g, wn):
    return _rms(o, wn) * _silu(g)


GLA_HALF = 2 * GLA_HEAD_V


def _gla_specs(rows, nc, rev):
    ci = (lambda c: nc - 1 - c) if rev else (lambda c: c)
    v0 = 2 * GLA_DK // GLA_HALF
    g0 = (2 * GLA_DK + D_INNER) // GLA_HALF
    return [
        pl.BlockSpec((rows, GLA_DK), lambda c: (ci(c), 0)),
        pl.BlockSpec((rows, GLA_DK), lambda c: (ci(c), 1)),
        pl.BlockSpec((rows, GLA_HALF), lambda c: (ci(c), v0)),
        pl.BlockSpec((rows, GLA_HALF), lambda c: (ci(c), v0 + 1)),
        pl.BlockSpec((rows, GLA_HALF), lambda c: (ci(c), g0)),
        pl.BlockSpec((rows, GLA_HALF), lambda c: (ci(c), g0 + 1)),
        pl.BlockSpec((rows, 128), lambda c: (ci(c), GLA_GK_COL // 128)),
        pl.BlockSpec((128, GLA_DK), lambda c: (0, 0)),
        pl.BlockSpec((1, GLA_DK), lambda c: (0, 0)),
        pl.BlockSpec((1, GLA_HEAD_V), lambda c: (0, 0)),
    ]


def _head_cols(ref_a, ref_b, h, rows=slice(None)):
    ref = ref_a if h < 2 else ref_b
    return ref[rows, (h % 2) * GLA_HEAD_V:(h % 2 + 1) * GLA_HEAD_V]


def _gla_layer_fwd(proj, wup, bias, wn, gather):
    L = proj.shape[0]
    sc = min(STEP_CHUNKS, L // CHUNK)
    rows = sc * CHUNK
    nc = L // rows
    n = len(gather)

    def body(*refs):
        q_ref, k_ref, va_ref, vb_ref, ga_ref, gb_ref, gk_ref, wup_ref, b_ref, wn_ref = refs[:10]
        x_refs = refs[10:10 + n]
        o_ref, og_ref, s_ref = refs[10 + n:13 + n]
        out_refs = refs[13 + n:13 + 2 * n]
        st, send_sems, recv_sems, local_sems = refs[13 + 2 * n:]
        start, finish = _gather_ops(x_refs, out_refs, send_sems, recv_sems, local_sems)

        @pl.when(pl.program_id(0) == 0)
        def _():
            st[...] = jnp.zeros(st.shape, F32)
            start()

        s_cur = tuple(st[h] for h in range(GLA_HEADS))
        for u in range(sc):
            r = slice(u * CHUNK, (u + 1) * CHUNK)
            for h in range(GLA_HEADS):
                s_ref[u, h] = s_cur[h]
            o, s_cur = _gla_chunk(q_ref[r], k_ref[r], va_ref[r], vb_ref[r], gk_ref[r], wup_ref[...], b_ref[...], s_cur)
            o_ref[r] = o
            for h in range(GLA_HEADS):
                vc = slice(h * GLA_HEAD_V, (h + 1) * GLA_HEAD_V)
                og_ref[r, vc] = _gla_post(o[:, vc], _head_cols(ga_ref, gb_ref, h, r), wn_ref[...]).astype(BF16)
        for h in range(GLA_HEADS):
            st[h] = s_cur[h]

        @pl.when(pl.program_id(0) == nc - 1)
        def _():
            finish()

    res = pl.pallas_call(
        body,
        grid=(nc,),
        in_specs=_gla_specs(rows, nc, False) + [ANY] * n,
        out_specs=[
            pl.BlockSpec((rows, D_INNER), lambda c: (c, 0)),
            pl.BlockSpec((rows, D_INNER), lambda c: (c, 0)),
            pl.BlockSpec((sc, GLA_HEADS, GLA_HEAD_V, GLA_HEAD_K), lambda c: (c, 0, 0, 0)),
        ] + [ANY] * n,
        out_shape=[
            jax.ShapeDtypeStruct((L, D_INNER), F32),
            jax.ShapeDtypeStruct((L, D_INNER), BF16),
            jax.ShapeDtypeStruct((L // CHUNK, GLA_HEADS, GLA_HEAD_V, GLA_HEAD_K), F32),
        ] + [jax.ShapeDtypeStruct((N_DEV,) + a.shape, a.dtype) for a in gather],
        scratch_shapes=[pltpu.VMEM((GLA_HEADS, GLA_HEAD_V, GLA_HEAD_K), F32)] + _comm_sems(n),
        compiler_params=_cparams(("arbitrary",)),
        name="gla_layer_fwd",
    )(proj, proj, proj, proj, proj, proj, proj, wup, bias, wn, *gather)
    return res[:3], res[3:]


def _gla_layer_bwd(proj, wup, bias, wn, o, s_in, dog, exchange):
    L = proj.shape[0]
    nc = L // CHUNK
    n = len(exchange)

    def body(*refs):
        (q_ref, k_ref, va_ref, vb_ref, ga_ref, gb_ref, gk_ref, wup_ref, b_ref, wn_ref, o_ref, s_ref, dog_ref) = refs[:13]
        p_refs = refs[13:13 + n]
        dp_ref, dwup_ref, db_ref, dwn_ref = refs[13 + n:17 + n]
        out_refs = refs[17 + n:17 + 2 * n]
        dst, send_sems, recv_sems, local_sems = refs[17 + 2 * n:]
        start, finish = _exchange_ops(p_refs, out_refs, send_sems, recv_sems, local_sems)

        @pl.when(pl.program_id(0) == 0)
        def _():
            dst[...] = jnp.zeros(dst.shape, F32)
            dwup_ref[...] = jnp.zeros(dwup_ref.shape, F32)
            db_ref[...] = jnp.zeros(db_ref.shape, F32)
            dwn_ref[...] = jnp.zeros(dwn_ref.shape, F32)
            start()

        dos = []
        for h in range(GLA_HEADS):
            vc = slice(h * GLA_HEAD_V, (h + 1) * GLA_HEAD_V)
            _, post_vjp = jax.vjp(_gla_post, o_ref[:, vc], _head_cols(ga_ref, gb_ref, h), wn_ref[...])
            do, dg, dwn = post_vjp(dog_ref[:, vc])
            dos.append(do)
            dp_ref[:, 2 * GLA_DK + D_INNER + h * GLA_HEAD_V:2 * GLA_DK + D_INNER + (h + 1) * GLA_HEAD_V] = dg.astype(BF16)
            dwn_ref[...] += dwn
        _, vjp = jax.vjp(_gla_chunk, q_ref[...], k_ref[...], va_ref[...], vb_ref[...], gk_ref[...], wup_ref[...], b_ref[...],
                         tuple(s_ref[h] for h in range(GLA_HEADS)))
        dq, dk, dva, dvb, dgk, dwup, db, ds = vjp((jnp.concatenate(dos, axis=1), tuple(dst[h] for h in range(GLA_HEADS))))
        for h in range(GLA_HEADS):
            dst[h] = ds[h]
        dp_ref[:, :GLA_DK] = dq.astype(BF16)
        dp_ref[:, GLA_DK:2 * GLA_DK] = dk.astype(BF16)
        dp_ref[:, 2 * GLA_DK:2 * GLA_DK + GLA_HALF] = dva.astype(BF16)
        dp_ref[:, 2 * GLA_DK + GLA_HALF:2 * GLA_DK + D_INNER] = dvb.astype(BF16)
        dwup_ref[...] += dwup
        db_ref[...] += db
        dp_ref[:, GLA_GK_COL:GLA_GK_COL + 128] = dgk.astype(BF16)
        dp_ref[:, GLA_GK_COL + 128:] = jnp.zeros((CHUNK, GLA_PROJ_PAD - GLA_GK_COL - 128), BF16)

        @pl.when(pl.program_id(0) == nc - 1)
        def _():
            finish()

    rc = lambda c: nc - 1 - c
    res = pl.pallas_call(
        body,
        grid=(nc,),
        in_specs=_gla_specs(CHUNK, nc, True) + [
            pl.BlockSpec((CHUNK, D_INNER), lambda c: (rc(c), 0)),
            pl.BlockSpec((None, GLA_HEADS, GLA_HEAD_V, GLA_HEAD_K), lambda c: (rc(c), 0, 0, 0)),
            pl.BlockSpec((CHUNK, D_INNER), lambda c: (rc(c), 0)),
        ] + [ANY] * n,
        out_specs=[
            pl.BlockSpec((CHUNK, GLA_PROJ_PAD), lambda c: (rc(c), 0)),
            pl.BlockSpec((128, GLA_DK), lambda c: (0, 0)),
            pl.BlockSpec((1, GLA_DK), lambda c: (0, 0)),
            pl.BlockSpec((1, GLA_HEAD_V), lambda c: (0, 0)),
        ] + [ANY] * n,
        out_shape=[
            jax.ShapeDtypeStruct((L, GLA_PROJ_PAD), BF16),
            jax.ShapeDtypeStruct((128, GLA_DK), F32),
            jax.ShapeDtypeStruct((1, GLA_DK), F32),
            jax.ShapeDtypeStruct((1, GLA_HEAD_V), F32),
        ] + [jax.ShapeDtypeStruct(a.shape, a.dtype) for a in exchange],
        scratch_shapes=[pltpu.VMEM((GLA_HEADS, GLA_HEAD_V, GLA_HEAD_K), F32)] + _comm_sems(n),
        compiler_params=_cparams(("arbitrary",)),
        name="gla_layer_bwd",
    )(proj, proj, proj, proj, proj, proj, proj, wup, bias, wn, o, s_in, dog, *exchange)
    return res[:4], res[4:]


@jax.custom_vjp
def _expand(v):
    r = v.shape[0]
    left = _iota((r, 128), 1) < SSD_P
    slabs = []
    for p in range(SSD_HEADS // 2):
        a = jnp.broadcast_to(v[:, 2 * p:2 * p + 1], (r, 128))
        b = jnp.broadcast_to(v[:, 2 * p + 1:2 * p + 2], (r, 128))
        slabs.append(jnp.where(left, a, b))
    return jnp.concatenate(slabs, axis=1)


def _expand_fwd(v):
    return _expand(v), None


def _expand_bwd(_, g):
    r = g.shape[0]
    lane = _iota((r, 128), 1)
    left = lane < SSD_P
    dv = jnp.zeros((r, 128), F32)
    for p in range(SSD_HEADS // 2):
        gs = g[:, 128 * p:128 * (p + 1)]
        sa = jnp.sum(jnp.where(left, gs, 0.0), axis=-1, keepdims=True)
        sb = jnp.sum(jnp.where(left, 0.0, gs), axis=-1, keepdims=True)
        dv = dv + jnp.where(lane == 2 * p, sa, 0.0) + jnp.where(lane == 2 * p + 1, sb, 0.0)
    return (dv,)


_expand.defvjp(_expand_fwd, _expand_bwd)

SSD_GW = SSD_HPG * SSD_P
SSD_BC = SSD_GROUPS * SSD_N
SSD_PHASE = 4
SSD_PHASE_FWD = 8


def _ssd_chunk(xs, Bm, Cm, dtp, dtb, alog, dsk, h_in, phase=SSD_PHASE):
    dt = _softplus(dtp + dtb)
    acum = _cumsum_rows(dt * (-jnp.exp(alog)))
    a_last = acum[CHUNK - 1:CHUNK]
    acum_b = _expand(acum)
    w_end = _expand(dt * jnp.exp(a_last - acum))
    d_b = _expand(jnp.broadcast_to(dsk, (8, 128)))[0:1]
    ac_t = jnp.concatenate([acum, acum], axis=0).T
    dt_t = jnp.concatenate([dt, dt], axis=0).T
    lane = _iota((CHUNK, 128), 1)
    left = lane < SSD_P
    causal = (lane & (SSD_P - 1)) <= _iota((CHUNK, 128), 0)
    cd = jnp.exp(ac_t[:, CHUNK - 1:CHUNK])
    ys, h_out = [], []
    for g0 in range(0, SSD_GROUPS, phase):
        cb2, y_off = {}, {}
        for g in range(g0, g0 + phase):
            Bg = Bm[:, g * SSD_N:(g + 1) * SSD_N]
            Cg = Cm[:, g * SSD_N:(g + 1) * SSD_N]
            gs = slice(g * SSD_GW, (g + 1) * SSD_GW)
            cb2[g] = _mxu_nt(Cg, jnp.concatenate([Bg, Bg], axis=0))
            y_off[g] = _mxu_nt(Cg, h_in[gs])
            st = _mxu_tn(xs[:, gs] * w_end[:, gs], Bg)
            hs = [h_in[h * SSD_P:(h + 1) * SSD_P] * cd[h:h + 1] for h in range(g * SSD_HPG, (g + 1) * SSD_HPG)]
            h_out.append(jnp.concatenate(hs, axis=0) + st)
        for p in range(g0 * (SSD_HPG // 2), (g0 + phase) * (SSD_HPG // 2)):
            g, k = divmod(p, SSD_HPG // 2)
            sl = slice(128 * p, 128 * (p + 1))
            ac_c = acum_b[:, sl]
            ac_r = jnp.where(left, ac_t[2 * p:2 * p + 1], ac_t[2 * p + 1:2 * p + 2])
            dt_r = jnp.where(left, dt_t[2 * p:2 * p + 1], dt_t[2 * p + 1:2 * p + 2])
            m2 = cb2[g] * jnp.where(causal, jnp.exp(jnp.minimum(ac_c - ac_r, 0.0)), 0.0) * dt_r
            xsl = xs[:, sl]
            x2 = jnp.concatenate([jnp.where(left, xsl, 0.0), jnp.where(left, 0.0, xsl)], axis=0)
            ys.append(_mxu(m2, x2) + y_off[g][:, 128 * k:128 * (k + 1)] * jnp.exp(ac_c) + xsl * d_b[:, sl])
    return jnp.concatenate(ys, axis=1), jnp.concatenate(h_out, axis=0)


def _ssd_step(xs, Bm, Cm, dtp, dtb, alog, dsk, h_in, z, wn, phase=SSD_PHASE):
    y, h_out = _ssd_chunk(xs, Bm, Cm, dtp, dtb, alog, dsk, h_in, phase)
    return _rms(y * _silu(z), wn), h_out


def _ssd_specs(rows, nc, rev):
    ci = (lambda c: nc - 1 - c) if rev else (lambda c: c)
    vec = pl.BlockSpec((1, 128), lambda c: (0, 0))
    return [
        pl.BlockSpec((rows, D_INNER), lambda c: (ci(c), 0)),
        pl.BlockSpec((rows, SSD_BC), lambda c: (ci(c), D_INNER // SSD_BC)),
        pl.BlockSpec((rows, SSD_BC), lambda c: (ci(c), D_INNER // SSD_BC + 1)),
        pl.BlockSpec((rows, 128), lambda c: (ci(c), SSD_DT_COL // 128)),
        vec, vec, vec,
        pl.BlockSpec((rows, D_INNER), lambda c: (ci(c), 0)),
        pl.BlockSpec((1, D_INNER), lambda c: (0, 0)),
    ]


def _ssd_layer_fwd(xbc, proj, dtb, alog, dsk, wn):
    L = xbc.shape[0]
    sc = min(STEP_CHUNKS, L // CHUNK)
    rows = sc * CHUNK
    nc = L // rows

    def body(xs_ref, b_ref, c_ref, dt_ref, dtb_ref, alog_ref, dsk_ref, z_ref, wn_ref, y_ref, hs_ref, hst):
        @pl.when(pl.program_id(0) == 0)
        def _():
            hst[...] = jnp.zeros(hst.shape, F32)

        h = hst[...]
        for u in range(sc):
            r = slice(u * CHUNK, (u + 1) * CHUNK)
            hs_ref[u] = h
            yn, h = _ssd_step(xs_ref[r], b_ref[r], c_ref[r], dt_ref[r], dtb_ref[...], alog_ref[...], dsk_ref[...], h,
                              z_ref[r], wn_ref[...], SSD_PHASE_FWD)
            y_ref[r] = yn.astype(BF16)
        hst[...] = h

    return pl.pallas_call(
        body,
        grid=(nc,),
        in_specs=_ssd_specs(rows, nc, False),
        out_specs=[
            pl.BlockSpec((rows, D_INNER), lambda c: (c, 0)),
            pl.BlockSpec((sc, D_INNER, SSD_N), lambda c: (c, 0, 0)),
        ],
        out_shape=[
            jax.ShapeDtypeStruct((L, D_INNER), BF16),
            jax.ShapeDtypeStruct((L // CHUNK, D_INNER, SSD_N), F32),
        ],
        scratch_shapes=[pltpu.VMEM((D_INNER, SSD_N), F32)],
        compiler_params=_cparams(("arbitrary",)),
        name="ssd_layer_fwd",
    )(xbc, xbc, xbc, proj, dtb, alog, dsk, proj, wn)


def _ssd_layer_bwd(xbc, proj, dtb, alog, dsk, wn, h_saved, dyn):
    L = xbc.shape[0]
    sc = min(BWD_STEP_CHUNKS, L // CHUNK)
    rows = sc * CHUNK
    nc = L // rows

    def body(xs_ref, b_ref, c_ref, dt_ref, dtb_ref, alog_ref, dsk_ref, z_ref, wn_ref, hs_ref, dyn_ref,
             dp_ref, dx_ref, ddtb_ref, dalog_ref, ddsk_ref, dwn_ref, dhst):
        @pl.when(pl.program_id(0) == 0)
        def _():
            dhst[...] = jnp.zeros(dhst.shape, F32)
            ddtb_ref[...] = jnp.zeros((1, 128), F32)
            dalog_ref[...] = jnp.zeros((1, 128), F32)
            ddsk_ref[...] = jnp.zeros((1, 128), F32)
            dwn_ref[...] = jnp.zeros((1, D_INNER), F32)

        dh = dhst[...]
        for u in reversed(range(sc)):
            r = slice(u * CHUNK, (u + 1) * CHUNK)
            _, vjp = jax.vjp(_ssd_step, xs_ref[r], b_ref[r], c_ref[r], dt_ref[r], dtb_ref[...], alog_ref[...],
                             dsk_ref[...], hs_ref[u], z_ref[r], wn_ref[...])
            dxs, db, dc, ddt, ddtb, dalog, ddsk, dh, dz, dwn = vjp((dyn_ref[r], dh))
            dx_ref[r, :D_INNER] = dxs
            dx_ref[r, D_INNER:D_INNER + SSD_BC] = db
            dx_ref[r, D_INNER + SSD_BC:] = dc
            dp_ref[r, :D_INNER] = dz.astype(BF16)
            dp_ref[r, SSD_DT_COL:SSD_DT_COL + 128] = ddt.astype(BF16)
            ddtb_ref[...] += ddtb
            dalog_ref[...] += dalog
            ddsk_ref[...] += ddsk
            dwn_ref[...] += dwn
        dp_ref[:, D_INNER:SSD_DT_COL] = jnp.zeros((rows, SSD_CONV_DIM), BF16)
        dp_ref[:, SSD_DT_COL + 128:] = jnp.zeros((rows, SSD_PROJ_PAD - SSD_DT_COL - 128), BF16)
        dhst[...] = dh

    rc = lambda c: nc - 1 - c
    vec = pl.BlockSpec((1, 128), lambda c: (0, 0))
    vshape = jax.ShapeDtypeStruct((1, 128), F32)
    return pl.pallas_call(
        body,
        grid=(nc,),
        in_specs=_ssd_specs(rows, nc, True) + [
            pl.BlockSpec((sc, D_INNER, SSD_N), lambda c: (rc(c), 0, 0)),
            pl.BlockSpec((rows, D_INNER), lambda c: (rc(c), 0)),
        ],
        out_specs=[
            pl.BlockSpec((rows, SSD_PROJ_PAD), lambda c: (rc(c), 0)),
            pl.BlockSpec((rows, SSD_CONV_DIM), lambda c: (rc(c), 0)),
            vec, vec, vec,
            pl.BlockSpec((1, D_INNER), lambda c: (0, 0)),
        ],
        out_shape=[
            jax.ShapeDtypeStruct((L, SSD_PROJ_PAD), BF16),
            jax.ShapeDtypeStruct((L, SSD_CONV_DIM), F32),
            vshape, vshape, vshape,
            jax.ShapeDtypeStruct((1, D_INNER), F32),
        ],
        scratch_shapes=[pltpu.VMEM((D_INNER, SSD_N), F32)],
        compiler_params=_cparams(("arbitrary",)),
        name="ssd_layer_bwd",
    )(xbc, xbc, xbc, proj, dtb, alog, dsk, proj, wn, h_saved, dyn)


def _pick(n, options):
    for t in options:
        if n % t == 0:
            return t
    return n


def _token_tile(m, row_bytes, fixed_bytes):
    for t in (MM_TOKEN_TILE, MM_TOKEN_TILE // 2, MM_TOKEN_TILE // 4):
        if m % t == 0 and t * row_bytes + fixed_bytes <= MM_VMEM_BUDGET:
            return t
    return min(m, MM_TOKEN_TILE // 4)


def _comm_out(kind, arrays):
    return [jax.ShapeDtypeStruct(((N_DEV,) + p.shape) if kind == "gather" else p.shape, p.dtype) for p in arrays]


def _mm(a, b, *, name, out_dtype=F32, add=None, comm=("exchange", ())):
    kind, exchange = comm
    M, K = a.shape
    N = b.shape[1]
    tn = _pick(N, MM_TILES)
    tk = _pick(K, MM_TILES)
    nk = K // tk
    row_bytes = 2 * (tk * a.dtype.itemsize + tn * jnp.dtype(out_dtype).itemsize + (tn * 4 if add is not None else 0)) \
        + (tn * 4 if nk > 1 else 0)
    tm = _token_tile(M, row_bytes, 2 * tk * tn * b.dtype.itemsize)
    grid = (M // tm, N // tn, nk)
    n = len(exchange)
    n_in = 2 + (add is not None)

    def body(*refs):
        a_ref, b_ref = refs[:2]
        add_ref = refs[2] if add is not None else None
        p_refs = refs[n_in:n_in + n]
        o_ref = refs[n_in + n]
        out_refs = refs[n_in + n + 1:n_in + 2 * n + 1]
        acc = refs[n_in + 2 * n + 1]
        ids = [pl.program_id(d) for d in range(3)]
        k = ids[2]
        if n:
            start, finish_exchange = (_gather_ops if kind == "gather" else _exchange_ops)(
                p_refs, out_refs, *refs[n_in + 2 * n + 2:])

            @pl.when((ids[0] == 0) & (ids[1] == 0) & (k == 0))
            def _():
                start()

        p = _dot(_bf(a_ref[...]), _bf(b_ref[...]))

        def finish(r):
            if add is not None:
                r = r + add_ref[...]
            o_ref[...] = r.astype(out_dtype)

        if nk == 1:
            finish(p)
        else:
            @pl.when(k == 0)
            def _():
                acc[...] = p

            @pl.when((k > 0) & (k < nk - 1))
            def _():
                acc[...] += p

            @pl.when(k == nk - 1)
            def _():
                finish(acc[...] + p)

        if n:
            @pl.when((ids[0] == grid[0] - 1) & (ids[1] == grid[1] - 1) & (k == nk - 1))
            def _():
                finish_exchange()

    in_specs = [pl.BlockSpec((tm, tk), lambda i, j, k: (i, k)), pl.BlockSpec((tk, tn), lambda i, j, k: (k, j))]
    args = [a, b]
    if add is not None:
        in_specs.append(pl.BlockSpec((tm, tn), lambda i, j, k: (i, j)))
        args.append(add)
    res = pl.pallas_call(
        body,
        grid=grid,
        in_specs=in_specs + [ANY] * n,
        out_specs=[pl.BlockSpec((tm, tn), lambda i, j, k: (i, j))] + [ANY] * n,
        out_shape=[jax.ShapeDtypeStruct((M, N), out_dtype)] + _comm_out(kind, exchange),
        scratch_shapes=[pltpu.VMEM((tm, tn) if nk > 1 else (8, 128), F32)] + (_comm_sems(n) if n else []),
        compiler_params=_cparams(("arbitrary",) * 3 if n else ("parallel", "parallel", "arbitrary")),
        name=name,
    )(*args, *exchange)
    return (res[0], res[1:]) if n else res[0]


def _mm_tn(a, b, *, name):
    M, K = a.shape
    N = b.shape[1]
    tn = _pick(N, MM_TILES)
    tm = _token_tile(M, 2 * (K * a.dtype.itemsize + tn * b.dtype.itemsize), 2 * K * tn * 4)

    def body(a_ref, b_ref, o_ref):
        _acc_out(o_ref, _dot_tn(_bf(a_ref[...]), _bf(b_ref[...])), pl.program_id(1) == 0)

    return pl.pallas_call(
        body,
        grid=(N // tn, M // tm),
        in_specs=[pl.BlockSpec((tm, K), lambda j, i: (i, 0)), pl.BlockSpec((tm, tn), lambda j, i: (i, j))],
        out_specs=pl.BlockSpec((K, tn), lambda j, i: (0, j)),
        out_shape=jax.ShapeDtypeStruct((K, N), F32),
        compiler_params=_cparams(("parallel", "arbitrary")),
        name=name,
    )(a, b)


def _rms_fwd(x, w, *, name, gather=()):
    L, D = x.shape
    tm = min(TOKEN_TILE, L)
    nt = L // tm
    n = len(gather)

    def body(*refs):
        x_ref, w_ref = refs[:2]
        o_ref = refs[2 + n]
        if n:
            start, finish = _gather_ops(refs[2:2 + n], refs[3 + n:3 + 2 * n], *refs[3 + 2 * n:])

            @pl.when(pl.program_id(0) == 0)
            def _():
                start()

        o_ref[...] = _rms(x_ref[...], w_ref[...]).astype(BF16)
        if n:
            @pl.when(pl.program_id(0) == nt - 1)
            def _():
                finish()

    res = pl.pallas_call(
        body, grid=(nt,),
        in_specs=[pl.BlockSpec((tm, D), lambda i: (i, 0)), pl.BlockSpec((1, D), lambda i: (0, 0))] + [ANY] * n,
        out_specs=[pl.BlockSpec((tm, D), lambda i: (i, 0))] + [ANY] * n,
        out_shape=[jax.ShapeDtypeStruct((L, D), BF16)] + _comm_out("gather", gather),
        scratch_shapes=_comm_sems(n) if n else [],
        compiler_params=_cparams(("arbitrary",) if n else ("parallel",)), name=name,
    )(x, w, *gather)
    return (res[0], res[1:]) if n else res[0]


def _rms_bwd(x, w, dhn, dres, *, name):
    L, D = x.shape
    tm = min(TOKEN_TILE, L)

    def body(x_ref, w_ref, dhn_ref, dres_ref, dx_ref, dw_ref):
        _, vjp = jax.vjp(_rms, x_ref[...], w_ref[...])
        dx, dw = vjp(dhn_ref[...])
        dx_ref[...] = dx + dres_ref[...]
        _acc_out(dw_ref, dw, pl.program_id(0) == 0)

    row = pl.BlockSpec((tm, D), lambda i: (i, 0))
    vec = pl.BlockSpec((1, D), lambda i: (0, 0))
    return pl.pallas_call(
        body, grid=(L // tm,),
        in_specs=[row, vec, row, row],
        out_specs=[row, vec],
        out_shape=[jax.ShapeDtypeStruct((L, D), F32), jax.ShapeDtypeStruct((1, D), F32)],
        compiler_params=_cparams(("arbitrary",)), name=name,
    )(x, w, dhn, dres)


CONV_HALO = 8
CONV_COLS = 1024


CONV_RB = 64
CONV_CB = 256
CONV_FWD_PIECE = (128, 128)


def _conv_pieces(tm, rb=CONV_RB, cb=CONV_CB):
    return [(r0, c0) for c0 in range(0, CONV_COLS, cb) for r0 in range(0, tm, min(rb, tm))]


def _conv_fwd(proj, w, b):
    L = proj.shape[0]
    tm = min(TOKEN_TILE, L)
    rb, cb = min(CONV_FWD_PIECE[0], tm), CONV_FWD_PIECE[1]
    c0 = D_INNER // CONV_COLS
    hb = tm // CONV_HALO

    def body(x_ref, h_ref, w_ref, b_ref, o_ref, sp_ref, xx):
        xx[0:CONV_HALO] = jnp.where(pl.program_id(1) == 0, 0.0, h_ref[...])
        xx[CONV_HALO:CONV_HALO + tm] = x_ref[...]
        for r0, cc in _conv_pieces(tm, rb, cb):
            cs = slice(cc, cc + cb)
            u = b_ref[:, cs]
            for k in range(SSD_CONV):
                off = CONV_HALO - (SSD_CONV - 1) + k + r0
                u = u + w_ref[k:k + 1, cs] * xx[off:off + rb, cs]
            s = 1.0 / (1.0 + jnp.exp(-u))
            o_ref[r0:r0 + rb, cs] = u * s
            sp_ref[r0:r0 + rb, cs] = s * (1.0 + u * (1.0 - s))

    blk = pl.BlockSpec((tm, CONV_COLS), lambda j, i: (i, j))
    shp = jax.ShapeDtypeStruct((L, SSD_CONV_DIM), F32)
    return pl.pallas_call(
        body, grid=(SSD_CONV_DIM // CONV_COLS, L // tm),
        in_specs=[
            pl.BlockSpec((tm, CONV_COLS), lambda j, i: (i, c0 + j)),
            pl.BlockSpec((CONV_HALO, CONV_COLS), lambda j, i: (jnp.maximum(i * hb - 1, 0), c0 + j)),
            pl.BlockSpec((SSD_CONV, CONV_COLS), lambda j, i: (0, j)),
            pl.BlockSpec((1, CONV_COLS), lambda j, i: (0, j)),
        ],
        out_specs=[blk, blk],
        out_shape=[shp, shp],
        scratch_shapes=[pltpu.VMEM((CONV_HALO + tm, CONV_COLS), F32)],
        compiler_params=_cparams(("parallel", "parallel")), name="conv_fwd",
    )(proj, proj, w, b)


def _conv_bwd(proj, w, sp, dxbc, dproj):
    L = proj.shape[0]
    tm = min(TOKEN_TILE, L)
    rb = min(CONV_RB, tm)
    nt = L // tm
    c0 = D_INNER // CONV_COLS
    hb = tm // CONV_HALO

    def fold(a):
        return jnp.sum(a.reshape(rb // 8, 8, CONV_CB), axis=0)

    def body(x_ref, h_ref, w_ref, sp_ref, dy_ref, dp_in_ref, dp_ref, dw_ref, db_ref, xx, dd):
        del dp_in_ref
        i = pl.program_id(1)
        first = i == 0

        @pl.when(first)
        def _():
            dd[tm:tm + CONV_HALO] = jnp.zeros((CONV_HALO, CONV_COLS), F32)

        xx[0:CONV_HALO] = jnp.where(i == nt - 1, 0.0, h_ref[...])
        xx[CONV_HALO:CONV_HALO + tm] = x_ref[...]
        dws, dbs = [], []
        for cc in range(0, CONV_COLS, CONV_CB):
            cs = slice(cc, cc + CONV_CB)
            acc = [jnp.zeros((8, CONV_CB), F32) for _ in range(SSD_CONV + 1)]
            for r0 in range(0, tm, rb):
                du = dy_ref[r0:r0 + rb, cs] * sp_ref[r0:r0 + rb, cs]
                dd[r0:r0 + rb, cs] = du
                for k in range(SSD_CONV):
                    off = CONV_HALO - (SSD_CONV - 1) + k + r0
                    acc[k] = acc[k] + fold(du * xx[off:off + rb, cs])
                acc[SSD_CONV] = acc[SSD_CONV] + fold(du)
            dws.append(jnp.concatenate([jnp.sum(a, axis=0, keepdims=True) for a in acc[:SSD_CONV]], axis=0))
            dbs.append(jnp.sum(acc[SSD_CONV], axis=0, keepdims=True))
        for r0, cc in _conv_pieces(tm):
            cs = slice(cc, cc + CONV_CB)
            dx = jnp.zeros((rb, CONV_CB), F32)
            for k in range(SSD_CONV):
                off = SSD_CONV - 1 - k + r0
                dx = dx + w_ref[k:k + 1, cs] * dd[off:off + rb, cs]
            dp_ref[r0:r0 + rb, cs] = dx.astype(BF16)
        _acc_out(dw_ref, jnp.concatenate(dws, axis=1), first)
        _acc_out(db_ref, jnp.concatenate(dbs, axis=1), first)
        dd[tm:tm + CONV_HALO] = dd[0:CONV_HALO]

    rt = lambda i: nt - 1 - i
    return pl.pallas_call(
        body, grid=(SSD_CONV_DIM // CONV_COLS, nt),
        in_specs=[
            pl.BlockSpec((tm, CONV_COLS), lambda j, i: (rt(i), c0 + j)),
            pl.BlockSpec((CONV_HALO, CONV_COLS), lambda j, i: (jnp.maximum(rt(i) * hb - 1, 0), c0 + j)),
            pl.BlockSpec((SSD_CONV, CONV_COLS), lambda j, i: (0, j)),
            pl.BlockSpec((tm, CONV_COLS), lambda j, i: (rt(i), j)),
            pl.BlockSpec((tm, CONV_COLS), lambda j, i: (rt(i), j)),
            pl.BlockSpec(memory_space=pl.ANY),
        ],
        out_specs=[
            pl.BlockSpec((tm, CONV_COLS), lambda j, i: (rt(i), c0 + j)),
            pl.BlockSpec((SSD_CONV, CONV_COLS), lambda j, i: (0, j)),
            pl.BlockSpec((1, CONV_COLS), lambda j, i: (0, j)),
        ],
        out_shape=[jax.ShapeDtypeStruct((L, SSD_PROJ_PAD), BF16), jax.ShapeDtypeStruct((SSD_CONV, SSD_CONV_DIM), F32),
                   jax.ShapeDtypeStruct((1, SSD_CONV_DIM), F32)],
        scratch_shapes=[pltpu.VMEM((CONV_HALO + tm, CONV_COLS), F32), pltpu.VMEM((tm + CONV_HALO, CONV_COLS), F32)],
        input_output_aliases={5: 0},
        compiler_params=_cparams(("arbitrary", "arbitrary")), name="conv_bwd",
    )(proj, proj, w, sp, dxbc, dproj)


def _loss_bwd(x, tgt, w):
    L, D = x.shape
    tm = min(TOKEN_TILE, L)

    def body(x_ref, t_ref, w_ref, l_ref, dx_ref, dw_ref):
        xv = x_ref[...]
        wv = w_ref[...]
        r = lax.rsqrt(jnp.mean(xv * xv, axis=-1, keepdims=True) + RMS_EPS)
        xh = xv * r
        e = xh * wv - t_ref[...]
        lsum = 0.5 * jnp.sum(jnp.mean(e * e, axis=-1, keepdims=True), axis=0, keepdims=True)
        dout = e * (1.0 / D)
        gx = dout * wv
        dx_ref[...] = r * (gx - xh * jnp.mean(gx * xh, axis=-1, keepdims=True))
        first = pl.program_id(0) == 0
        _acc_out(dw_ref, jnp.sum(dout * xh, axis=0, keepdims=True), first)
        _acc_out(l_ref, jnp.broadcast_to(lsum, (8, 128)), first)

    row = pl.BlockSpec((tm, D), lambda i: (i, 0))
    vec = pl.BlockSpec((1, D), lambda i: (0, 0))
    return pl.pallas_call(
        body, grid=(L // tm,),
        in_specs=[row, row, vec],
        out_specs=[pl.BlockSpec((8, 128), lambda i: (0, 0)), row, vec],
        out_shape=[jax.ShapeDtypeStruct((8, 128), F32), jax.ShapeDtypeStruct((L, D), F32), jax.ShapeDtypeStruct((1, D), F32)],
        compiler_params=_cparams(("arbitrary",)), name="loss_bwd",
    )(x, tgt, w)


MESH = pl.DeviceIdType.MESH
ANY = pl.BlockSpec(memory_space=pl.ANY)


def _comm_sems(n):
    return [pltpu.SemaphoreType.DMA((n, 7)), pltpu.SemaphoreType.DMA((n, 7)), pltpu.SemaphoreType.DMA((n,))]


def _gather_ops(x_refs, out_refs, send_sems, recv_sems, local_sems):
    n = len(x_refs)
    x, y, c = lax.axis_index("x"), lax.axis_index("y"), lax.axis_index("c")
    me, sibling = (x, y, c), (x, y, 1 - c)
    chips = [(1 - x, y), (x, 1 - y), (1 - x, 1 - y)]

    def slot(i, px, py, pc):
        return out_refs[i].at[4 * px + 2 * py + pc]

    def copy(i, k, block, to, src=None):
        return pltpu.make_async_remote_copy(
            src_ref=slot(i, *block) if src is None else src, dst_ref=slot(i, *block),
            send_sem=send_sems.at[i, k], recv_sem=recv_sems.at[i, k], device_id=to, device_id_type=MESH)

    def own():
        mine = [pltpu.make_async_copy(x_refs[i], slot(i, *me), local_sems.at[i]) for i in range(n)]
        first = [copy(i, 0, me, sibling, src=x_refs[i]) for i in range(n)]
        first += [copy(i, 1 + j, me, (*chip, c), src=x_refs[i]) for j, chip in enumerate(chips) for i in range(n)]
        return mine, first

    def start():
        mine, first = own()
        for cp in mine + first:
            cp.start()

    def finish():
        mine, first = own()
        passed = []
        for j, chip in enumerate(chips):
            for i in range(n):
                copy(i, 1 + j, (*chip, c), me).wait_recv()
                passed.append(copy(i, 4 + j, (*chip, c), sibling))
                passed[-1].start()
        for i in range(n):
            copy(i, 0, sibling, me).wait_recv()
        for j, chip in enumerate(chips):
            for i in range(n):
                copy(i, 4 + j, (*chip, 1 - c), me).wait_recv()
        for cp in first + passed:
            cp.wait_send()
        for cp in mine:
            cp.wait()

    return start, finish


def _exchange_ops(p_refs, out_refs, send_sems, recv_sems, local_sems):
    n = len(p_refs)
    x, y, c = lax.axis_index("x"), lax.axis_index("y"), lax.axis_index("c")
    my = 4 * x + 2 * y + c

    def peer(k):
        fx, fy, fc = (k >> 2) & 1, (k >> 1) & 1, k & 1
        px, py, pc = (1 - x if fx else x), (1 - y if fy else y), (1 - c if fc else c)
        return (px, py, pc), 4 * px + 2 * py + pc

    def mine():
        return [pltpu.make_async_copy(p_refs[i].at[my], out_refs[i].at[my], local_sems.at[i]) for i in range(n)]

    def start():
        for cp in mine():
            cp.start()
        for k in range(1, N_DEV):
            to, pid = peer(k)
            for i in range(n):
                pltpu.make_async_remote_copy(
                    src_ref=p_refs[i].at[pid], dst_ref=out_refs[i].at[my], send_sem=send_sems.at[i, k - 1],
                    recv_sem=recv_sems.at[i, k - 1], device_id=to, device_id_type=MESH).start()

    def finish():
        for k in range(1, N_DEV):
            to, pid = peer(k)
            for i in range(n):
                pltpu.make_async_remote_copy(
                    src_ref=p_refs[i].at[pid], dst_ref=out_refs[i].at[pid], send_sem=send_sems.at[i, k - 1],
                    recv_sem=recv_sems.at[i, k - 1], device_id=to, device_id_type=MESH).wait()
        for cp in mine():
            cp.wait()

    return start, finish


def _all_gather(xs, *, name):
    n = len(xs)

    def body(*refs):
        start, finish = _gather_ops(refs[:n], refs[n:2 * n], *refs[2 * n:])
        start()
        finish()

    return pl.pallas_call(
        body,
        out_shape=[jax.ShapeDtypeStruct((N_DEV,) + a.shape, a.dtype) for a in xs],
        in_specs=[ANY] * n, out_specs=[ANY] * n, scratch_shapes=_comm_sems(n), name=name,
    )(*xs)


def _adamw(parts, w, m, v, *, name):
    a, b = w.shape
    tr = _pick(a, (256, 128, 64, 32, 16, 8))

    def body(p_ref, w_ref, m_ref, v_ref, g_ref, d_ref, mo_ref, vo_ref):
        g = p_ref[0]
        for s in range(1, N_DEV):
            g = g + p_ref[s]
        mn = ADAM_B1 * m_ref[...] + (1.0 - ADAM_B1) * g
        vn = ADAM_B2 * v_ref[...] + (1.0 - ADAM_B2) * jnp.square(g)
        m_hat = mn / (1.0 - ADAM_B1 ** ADAM_STEP)
        v_hat = vn / (1.0 - ADAM_B2 ** ADAM_STEP)
        g_ref[...] = g
        d_ref[...] = -ADAM_LR * (m_hat / (jnp.sqrt(v_hat) + ADAM_EPS) + ADAM_WD * w_ref[...])
        mo_ref[...] = mn
        vo_ref[...] = vn

    blk = pl.BlockSpec((tr, b), lambda i: (i, 0))
    shp = jax.ShapeDtypeStruct((a, b), F32)
    return pl.pallas_call(
        body, grid=(a // tr,),
        in_specs=[pl.BlockSpec((N_DEV, tr, b), lambda i: (0, i, 0)), blk, blk, blk],
        out_specs=[blk, blk, blk, blk],
        out_shape=[shp, shp, shp, shp],
        compiler_params=_cparams(("parallel",)), name=name,
    )(parts, w, m, v)


def _col_shards(a, n):
    return a.reshape(a.shape[0], N_DEV, n).transpose(1, 0, 2)


def _from_col_shards(g, cols):
    r = g.shape[1]
    full = g.transpose(1, 0, 2).reshape(r, -1)
    return jnp.pad(full, ((0, 0), (0, cols - full.shape[1])))


def kernel(x, norm_w, gla_in_proj, gla_gate_up, gla_gate_bias, gla_head_norm, gla_out_proj, ssd_in_proj, ssd_conv_w, ssd_conv_b, ssd_dt_bias, ssd_a_log, ssd_d, ssd_gate_norm, ssd_out_proj, final_norm, loss_target, m_norm_w, m_gla_in_proj, m_gla_gate_up, m_gla_gate_bias, m_gla_head_norm, m_gla_out_proj, m_ssd_in_proj, m_ssd_conv_w, m_ssd_conv_b, m_ssd_dt_bias, m_ssd_a_log, m_ssd_d, m_ssd_gate_norm, m_ssd_out_proj, m_final_norm, v_norm_w, v_gla_in_proj, v_gla_gate_up, v_gla_gate_bias, v_gla_head_norm, v_gla_out_proj, v_ssd_in_proj, v_ssd_conv_w, v_ssd_conv_b, v_ssd_dt_bias, v_ssd_a_log, v_ssd_d, v_ssd_gate_norm, v_ssd_out_proj, v_final_norm):
    x0 = x[0]
    tgt = loss_target[0]
    n_gin = GLA_PROJ // N_DEV
    n_sin = SSD_PROJ // N_DEV
    n_up = GLA_DK // N_DEV
    n_cv = SSD_CONV_DIM // N_DEV

    vec128 = lambda a: jnp.pad(a.reshape(1, -1), ((0, 0), (0, 128 - a.size)))
    dtb, alog, dsk = vec128(ssd_dt_bias), vec128(ssd_a_log), vec128(ssd_d)
    nw0, nw1 = norm_w[0:1], norm_w[1:2]

    hn1, (g_gin,) = _rms_fwd(x0, nw0, name="rms1_fwd", gather=[gla_in_proj[0].astype(BF16)])
    w_gin = _from_col_shards(g_gin, GLA_PROJ_PAD)
    proj1, (g_up, g_gout) = _mm(hn1, w_gin, name="gla_in_proj",
                                comm=("gather", [gla_gate_up[0].astype(BF16), gla_out_proj[0].astype(BF16)]))
    wup = jnp.pad(_from_col_shards(g_up, GLA_DK), ((0, 128 - GLA_RANK), (0, 0))).astype(F32)
    w_gout = g_gout.reshape(D_INNER, D_MODEL)
    (o, og, s_saved), (g_sin, g_sout, g_cw, g_cb, g_gn) = _gla_layer_fwd(
        proj1, wup, gla_gate_bias, gla_head_norm,
        [ssd_in_proj[0].astype(BF16), ssd_out_proj[0].astype(BF16), ssd_conv_w[0], ssd_conv_b, ssd_gate_norm])
    w_sin = _from_col_shards(g_sin, SSD_PROJ_PAD)
    w_sout = g_sout.reshape(D_INNER, D_MODEL)
    conv_w = _from_col_shards(g_cw, SSD_CONV_DIM)
    conv_b = g_cb.reshape(1, SSD_CONV_DIM)
    gate_norm = g_gn.reshape(1, D_INNER)
    x1 = _mm(og, w_gout, add=x0, name="gla_out_proj")
    hn2 = _rms_fwd(x1, nw1, name="rms2_fwd")
    proj2 = _mm(hn2, w_sin, name="ssd_in_proj")
    xbc, conv_sp = _conv_fwd(proj2, conv_w, conv_b)
    yn, h_saved = _ssd_layer_fwd(xbc, proj2, dtb, alog, dsk, gate_norm)
    x2 = _mm(yn, w_sout, add=x1, name="ssd_out_proj")
    lsum, dx2, d_final = _loss_bwd(x2, tgt, final_norm.reshape(1, D_MODEL))
    loss = lax.psum(lsum[0, 0], ("x", "y", "c"))

    d_sout = _mm_tn(yn, dx2, name="ssd_out_proj_dw")
    dyn = _mm(dx2, w_sout.T, name="ssd_out_proj_dx")
    dproj2, dxbc, d_dtb, d_alog, d_dsk, d_gate_norm = _ssd_layer_bwd(xbc, proj2, dtb, alog, dsk, gate_norm, h_saved, dyn)
    dproj2, d_conv_w, d_conv_b = _conv_bwd(proj2, conv_w, conv_sp, dxbc, dproj2)
    d_sin = _mm_tn(hn2, dproj2, name="ssd_in_proj_dw")
    dhn2 = _mm(dproj2, w_sin.T, name="ssd_in_proj_dx")
    dx1, d_nw1 = _rms_bwd(x1, nw1, dhn2, dx2, name="rms2_bwd")
    d_gout = _mm_tn(og, dx1, name="gla_out_proj_dw")
    dog = _mm(dx1, w_gout.T, name="gla_out_proj_dx")
    early = {
        "gla_out_proj": ((gla_out_proj[0], m_gla_out_proj[0], v_gla_out_proj[0]), d_gout.reshape(N_DEV, -1, D_MODEL)),
        "ssd_in_proj": ((ssd_in_proj[0], m_ssd_in_proj[0], v_ssd_in_proj[0]), _col_shards(d_sin[:, :SSD_PROJ], n_sin)),
        "ssd_conv_w": ((ssd_conv_w[0], m_ssd_conv_w[0], v_ssd_conv_w[0]), _col_shards(d_conv_w, n_cv)),
        "ssd_conv_b": ((ssd_conv_b, m_ssd_conv_b, v_ssd_conv_b), d_conv_b.reshape(N_DEV, 1, n_cv)),
        "ssd_gate_norm": ((ssd_gate_norm, m_ssd_gate_norm, v_ssd_gate_norm), d_gate_norm.reshape(N_DEV, 1, -1)),
        "ssd_out_proj": ((ssd_out_proj[0], m_ssd_out_proj[0], v_ssd_out_proj[0]), d_sout.reshape(N_DEV, -1, D_MODEL)),
    }
    (dproj1, d_wup, d_gbias, d_head_norm), early_recv = _gla_layer_bwd(
        proj1, wup, gla_gate_bias, gla_head_norm, o, s_saved, dog, [p for _, p in early.values()])
    d_gin = _mm_tn(hn1, dproj1, name="gla_in_proj_dw")
    late = {
        "gla_in_proj": ((gla_in_proj[0], m_gla_in_proj[0], v_gla_in_proj[0]), _col_shards(d_gin[:, :GLA_PROJ], n_gin)),
        "gla_gate_up": ((gla_gate_up[0], m_gla_gate_up[0], v_gla_gate_up[0]), _col_shards(d_wup[:GLA_RANK], n_up)),
    }
    dhn1, late_recv = _mm(dproj1, w_gin.T, name="gla_in_proj_dx", comm=("exchange", [p for _, p in late.values()]))
    dx0, d_nw0 = _rms_bwd(x0, nw0, dhn1, dx1, name="rms1_bwd")
    heads = SSD_HEADS
    replicated = {
        "norm_w": ((norm_w, m_norm_w, v_norm_w), jnp.concatenate([d_nw0, d_nw1], axis=0)),
        "gla_gate_bias": ((gla_gate_bias, m_gla_gate_bias, v_gla_gate_bias), d_gbias),
        "gla_head_norm": ((gla_head_norm, m_gla_head_norm, v_gla_head_norm), d_head_norm),
        "ssd_dt_bias": ((ssd_dt_bias, m_ssd_dt_bias, v_ssd_dt_bias), d_dtb[:, :heads]),
        "ssd_a_log": ((ssd_a_log, m_ssd_a_log, v_ssd_a_log), d_alog[:, :heads]),
        "ssd_d": ((ssd_d, m_ssd_d, v_ssd_d), d_dsk[:, :heads]),
        "final_norm": (tuple(t.reshape(1, D_MODEL) for t in (final_norm, m_final_norm, v_final_norm)), d_final),
    }
    results = {}
    for group, recv in ((early, early_recv),
                        (late, late_recv),
                        (replicated, _all_gather([p for _, p in replicated.values()], name="replicated_gather"))):
        for (nm, ((w, m, v), _)), r in zip(group.items(), recv):
            results[nm] = _adamw(r, w, m, v, name=nm + "_adamw")

    order = [("norm_w", norm_w), ("gla_in_proj", gla_in_proj), ("gla_gate_up", gla_gate_up), ("gla_gate_bias", gla_gate_bias),
             ("gla_head_norm", gla_head_norm), ("gla_out_proj", gla_out_proj), ("ssd_in_proj", ssd_in_proj),
             ("ssd_conv_w", ssd_conv_w), ("ssd_conv_b", ssd_conv_b), ("ssd_dt_bias", ssd_dt_bias), ("ssd_a_log", ssd_a_log),
             ("ssd_d", ssd_d), ("ssd_gate_norm", ssd_gate_norm), ("ssd_out_proj", ssd_out_proj), ("final_norm", final_norm)]
    out = [loss, dx0[None]]
    for i in range(4):
        out += [results[nm][i].reshape(ref.shape) for nm, ref in order]
    return tuple(out)
```

```python
import jax
import jax.numpy as jnp
from jax import lax
from jax.experimental import pallas as pl
from jax.experimental.pallas import tpu as pltpu

F32 = jnp.float32
BF16 = jnp.bfloat16

D_MODEL = 1024
D_INNER = 2048
RMS_EPS = 1e-6
GLA_HEADS = 4
GLA_DK = 512
GLA_HEAD_K = 128
GLA_HEAD_V = 512
GLA_RANK = 16
GLA_NORMALIZER = 16.0
CHUNK = 64
SUB = 16
STEP_CHUNKS = 8
BWD_STEP_CHUNKS = 1
GLA_PROJ = 5136
GLA_PROJ_PAD = 5376
GLA_GK_COL = 5120
SSD_HEADS = 32
SSD_GROUPS = 8
SSD_HPG = 4
SSD_P = 64
SSD_N = 128
SSD_CONV = 4
SSD_CONV_DIM = 4096
SSD_PROJ = 6176
SSD_PROJ_PAD = 6400
SSD_DT_COL = 6144
N_DEV = 8

ADAM_LR = 0.001
ADAM_B1 = 0.9
ADAM_B2 = 0.999
ADAM_EPS = 1e-08
ADAM_WD = 0.01
ADAM_STEP = 10

VMEM_LIMIT = 56 * 1024 * 1024
TOKEN_TILE = 512
MM_TOKEN_TILE = 2048
MM_VMEM_BUDGET = 44 * 1024 * 1024
MM_TILES = (2048, 1792, 1280, 1024, 768, 512, 256, 128)


def _dot(a, b):
    return jnp.dot(a, b, preferred_element_type=F32)


def _dot_nt(a, b):
    return lax.dot_general(a, b, (((1,), (1,)), ((), ())), preferred_element_type=F32)


def _dot_tn(a, b):
    return lax.dot_general(a, b, (((0,), (0,)), ((), ())), preferred_element_type=F32)


def _bf(a):
    return a.astype(BF16)


@jax.custom_vjp
def _mxu(a, b):
    return _dot(_bf(a), _bf(b))


def _mxu_fwd(a, b):
    return _mxu(a, b), (a, b)


def _mxu_bwd(res, g):
    a, b = res
    return _dot_nt(_bf(g), _bf(b)), _dot_tn(_bf(a), _bf(g))


_mxu.defvjp(_mxu_fwd, _mxu_bwd)


@jax.custom_vjp
def _mxu_nt(a, b):
    return _dot_nt(_bf(a), _bf(b))


def _mxu_nt_fwd(a, b):
    return _mxu_nt(a, b), (a, b)


def _mxu_nt_bwd(res, g):
    a, b = res
    return _dot(_bf(g), _bf(b)), _dot_tn(_bf(g), _bf(a))


_mxu_nt.defvjp(_mxu_nt_fwd, _mxu_nt_bwd)


@jax.custom_vjp
def _mxu_tn(a, b):
    return _dot_tn(_bf(a), _bf(b))


def _mxu_tn_fwd(a, b):
    return _mxu_tn(a, b), (a, b)


def _mxu_tn_bwd(res, g):
    a, b = res
    return _dot_nt(_bf(b), _bf(g)), _dot(_bf(a), _bf(g))


_mxu_tn.defvjp(_mxu_tn_fwd, _mxu_tn_bwd)


def _split2(a):
    hi = _bf(a)
    return hi, _bf(a - hi.astype(F32))


def _three_pass(dot, a, b):
    ah, al = _split2(a)
    bh, bl = _split2(b)
    return dot(ah, bh) + (dot(ah, bl) + dot(al, bh))


@jax.custom_vjp
def _dot3_nt(a, b):
    return _three_pass(_dot_nt, a, b)


def _dot3_nt_fwd(a, b):
    return _dot3_nt(a, b), (a, b)


def _dot3_nt_bwd(res, g):
    a, b = res
    return _three_pass(_dot, g, b), _three_pass(_dot_tn, g, a)


_dot3_nt.defvjp(_dot3_nt_fwd, _dot3_nt_bwd)


def _silu(x):
    return x / (1.0 + jnp.exp(-x))


def _log_sigmoid(z):
    return jnp.minimum(z, 0.0) - jnp.log(1.0 + jnp.exp(-jnp.abs(z)))


def _softplus(z):
    return jnp.maximum(z, 0.0) + jnp.log(1.0 + jnp.exp(-jnp.abs(z)))


def _iota(shape, dim):
    return lax.broadcasted_iota(jnp.int32, shape, dim)


def _rms(x, w):
    return x * lax.rsqrt(jnp.mean(x * x, axis=-1, keepdims=True) + RMS_EPS) * w


def _scan_rows(a, reverse, seg):
    n = a.shape[0]
    pos = _iota(a.shape, 0) & (seg - 1)
    sh = 1
    while sh < seg:
        if reverse:
            a = a + jnp.where(pos < seg - sh, pltpu.roll(a, n - sh, 0), 0.0)
        else:
            a = a + jnp.where(pos >= sh, pltpu.roll(a, sh, 0), 0.0)
        sh *= 2
    return a


def _make_cumsum(seg):
    @jax.custom_vjp
    def cumsum(a):
        return _scan_rows(a, False, seg)

    cumsum.defvjp(lambda a: (_scan_rows(a, False, seg), None), lambda _, g: (_scan_rows(g, True, seg),))
    return cumsum


_cumsum_sub = _make_cumsum(SUB)
_cumsum_rows = _make_cumsum(CHUNK)


def _cparams(sem):
    return pltpu.CompilerParams(dimension_semantics=sem, vmem_limit_bytes=VMEM_LIMIT)


def _acc_out(ref, val, first):
    @pl.when(first)
    def _():
        ref[...] = val

    @pl.when(jnp.logical_not(first))
    def _():
        ref[...] += val


def _gla_chunk(q, k, va, vb, gk, wup, bias, sts):
    nb = CHUNK // SUB
    heads = range(GLA_HEADS)
    hc = lambda a, h: a[:, h * GLA_HEAD_K:(h + 1) * GLA_HEAD_K]
    v = [(va if h < 2 else vb)[:, (h % 2) * GLA_HEAD_V:(h % 2 + 1) * GLA_HEAD_V] for h in heads]
    z = _mxu(gk, wup) + bias
    la = _log_sigmoid(z) * (1.0 / GLA_NORMALIZER)
    qs = q * (GLA_HEAD_K ** -0.5)
    bl = _cumsum_sub(la)
    tot = [jnp.sum(la[i * SUB:(i + 1) * SUB], axis=0, keepdims=True) for i in range(nb)]
    pre = [jnp.zeros((1, GLA_DK), F32)]
    for i in range(nb):
        pre.append(pre[i] + tot[i])
    b_last = pre[nb]
    rows_of = lambda vals: jnp.concatenate([jnp.broadcast_to(t, (SUB, GLA_DK)) for t in vals], axis=0)
    suf = rows_of(tot) - bl
    nxt = rows_of(pre[1:])
    q_in = qs * jnp.exp(bl + rows_of(pre[:nb]))
    k_end = k * jnp.exp(suf + (b_last - nxt))
    dec = jnp.exp(b_last)
    qa = qs * jnp.exp(bl)
    o_inter = [_mxu_nt(hc(q_in, h), sts[h]) for h in heads]
    sts_new = tuple(sts[h] * hc(dec, h) + _mxu_tn(v[h], hc(k_end, h)) for h in heads)
    half = SUB // 2
    rs = _iota((SUB, GLA_HEAD_K), 0)
    cs = _iota((half, CHUNK), 1)
    a_rows = [[] for _ in heads]
    for i in range(nb):
        sl = slice(i * SUB, (i + 1) * SUB)
        n = i * SUB
        if i > 0:
            kp = jnp.concatenate([k[:n] * jnp.exp(suf[:n] + (pre[i] - nxt[:n])), jnp.zeros((CHUNK - n, GLA_DK), F32)], axis=0)
        for h in heads:
            q_i, k_i, bl_i = hc(qs[sl], h), hc(k[sl], h), hc(bl[sl], h)
            if i > 0:
                a_i = _dot3_nt(hc(qa[sl], h), hc(kp, h))
                a_top, a_bot = a_i[:half], a_i[half:]
            else:
                a_top = a_bot = jnp.zeros((half, CHUNK), F32)
            for j in range(SUB):
                lo = 0 if j < half else half
                e = jnp.exp(jnp.minimum(bl_i[lo:] - bl_i[j:j + 1], 0.0))
                t = jnp.where(rs[lo:] >= j, q_i[lo:] * e * k_i[j:j + 1], 0.0)
                rsum = jnp.sum(t, axis=-1, keepdims=True)
                hit = cs == i * SUB + j
                if lo == 0:
                    a_top = a_top + jnp.where(hit, rsum[:half], 0.0)
                a_bot = a_bot + jnp.where(hit, rsum[half - lo:], 0.0)
            a_rows[h] += [a_top, a_bot]
    o = [o_inter[h] + _mxu(jnp.concatenate(a_rows[h], axis=0), v[h]) for h in heads]
    return jnp.concatenate(o, axis=1), sts_new


def _gla_post(o, g, wn):
    return _rms(o, wn) * _silu(g)


GLA_HALF = 2 * GLA_HEAD_V


def _gla_specs(rows, nc, rev):
    ci = (lambda c: nc - 1 - c) if rev else (lambda c: c)
    v0 = 2 * GLA_DK // GLA_HALF
    g0 = (2 * GLA_DK + D_INNER) // GLA_HALF
    return [
        pl.BlockSpec((rows, GLA_DK), lambda c: (ci(c), 0)),
        pl.BlockSpec((rows, GLA_DK), lambda c: (ci(c), 1)),
        pl.BlockSpec((rows, GLA_HALF), lambda c: (ci(c), v0)),
        pl.BlockSpec((rows, GLA_HALF), lambda c: (ci(c), v0 + 1)),
        pl.BlockSpec((rows, GLA_HALF), lambda c: (ci(c), g0)),
        pl.BlockSpec((rows, GLA_HALF), lambda c: (ci(c), g0 + 1)),
        pl.BlockSpec((rows, 128), lambda c: (ci(c), GLA_GK_COL // 128)),
        pl.BlockSpec((128, GLA_DK), lambda c: (0, 0)),
        pl.BlockSpec((1, GLA_DK), lambda c: (0, 0)),
        pl.BlockSpec((1, GLA_HEAD_V), lambda c: (0, 0)),
    ]


def _head_cols(ref_a, ref_b, h, rows=slice(None)):
    ref = ref_a if h < 2 else ref_b
    return ref[rows, (h % 2) * GLA_HEAD_V:(h % 2 + 1) * GLA_HEAD_V]


def _gla_layer_fwd(proj, wup, bias, wn, gather):
    L = proj.shape[0]
    sc = min(STEP_CHUNKS, L // CHUNK)
    rows = sc * CHUNK
    nc = L // rows
    n = len(gather)

    def body(*refs):
        q_ref, k_ref, va_ref, vb_ref, ga_ref, gb_ref, gk_ref, wup_ref, b_ref, wn_ref = refs[:10]
        x_refs = refs[10:10 + n]
        o_ref, og_ref, s_ref = refs[10 + n:13 + n]
        out_refs = refs[13 + n:13 + 2 * n]
        st, send_sems, recv_sems, local_sems = refs[13 + 2 * n:]
        start, finish = _gather_ops(x_refs, out_refs, send_sems, recv_sems, local_sems)

        @pl.when(pl.program_id(0) == 0)
        def _():
            st[...] = jnp.zeros(st.shape, F32)
            start()

        s_cur = tuple(st[h] for h in range(GLA_HEADS))
        for u in range(sc):
            r = slice(u * CHUNK, (u + 1) * CHUNK)
            for h in range(GLA_HEADS):
                s_ref[u, h] = s_cur[h]
            o, s_cur = _gla_chunk(q_ref[r], k_ref[r], va_ref[r], vb_ref[r], gk_ref[r], wup_ref[...], b_ref[...], s_cur)
            o_ref[r] = o
            for h in range(GLA_HEADS):
                vc = slice(h * GLA_HEAD_V, (h + 1) * GLA_HEAD_V)
                og_ref[r, vc] = _gla_post(o[:, vc], _head_cols(ga_ref, gb_ref, h, r), wn_ref[...]).astype(BF16)
        for h in range(GLA_HEADS):
            st[h] = s_cur[h]

        @pl.when(pl.program_id(0) == nc - 1)
        def _():
            finish()

    res = pl.pallas_call(
        body,
        grid=(nc,),
        in_specs=_gla_specs(rows, nc, False) + [ANY] * n,
        out_specs=[
            pl.BlockSpec((rows, D_INNER), lambda c: (c, 0)),
            pl.BlockSpec((rows, D_INNER), lambda c: (c, 0)),
            pl.BlockSpec((sc, GLA_HEADS, GLA_HEAD_V, GLA_HEAD_K), lambda c: (c, 0, 0, 0)),
        ] + [ANY] * n,
        out_shape=[
            jax.ShapeDtypeStruct((L, D_INNER), F32),
            jax.ShapeDtypeStruct((L, D_INNER), BF16),
            jax.ShapeDtypeStruct((L // CHUNK, GLA_HEADS, GLA_HEAD_V, GLA_HEAD_K), F32),
        ] + [jax.ShapeDtypeStruct((N_DEV,) + a.shape, a.dtype) for a in gather],
        scratch_shapes=[pltpu.VMEM((GLA_HEADS, GLA_HEAD_V, GLA_HEAD_K), F32)] + _comm_sems(n),
        compiler_params=_cparams(("arbitrary",)),
        name="gla_layer_fwd",
    )(proj, proj, proj, proj, proj, proj, proj, wup, bias, wn, *gather)
    return res[:3], res[3:]


def _gla_layer_bwd(proj, wup, bias, wn, o, s_in, dog, exchange):
    L = proj.shape[0]
    nc = L // CHUNK
    n = len(exchange)

    def body(*refs):
        (q_ref, k_ref, va_ref, vb_ref, ga_ref, gb_ref, gk_ref, wup_ref, b_ref, wn_ref, o_ref, s_ref, dog_ref) = refs[:13]
        p_refs = refs[13:13 + n]
        dp_ref, dwup_ref, db_ref, dwn_ref = refs[13 + n:17 + n]
        out_refs = refs[17 + n:17 + 2 * n]
        dst, send_sems, recv_sems, local_sems = refs[17 + 2 * n:]
        start, finish = _exchange_ops(p_refs, out_refs, send_sems, recv_sems, local_sems)

        @pl.when(pl.program_id(0) == 0)
        def _():
            dst[...] = jnp.zeros(dst.shape, F32)
            dwup_ref[...] = jnp.zeros(dwup_ref.shape, F32)
            db_ref[...] = jnp.zeros(db_ref.shape, F32)
            dwn_ref[...] = jnp.zeros(dwn_ref.shape, F32)
            start()

        dos = []
        for h in range(GLA_HEADS):
            vc = slice(h * GLA_HEAD_V, (h + 1) * GLA_HEAD_V)
            _, post_vjp = jax.vjp(_gla_post, o_ref[:, vc], _head_cols(ga_ref, gb_ref, h), wn_ref[...])
            do, dg, dwn = post_vjp(dog_ref[:, vc])
            dos.append(do)
            dp_ref[:, 2 * GLA_DK + D_INNER + h * GLA_HEAD_V:2 * GLA_DK + D_INNER + (h + 1) * GLA_HEAD_V] = dg.astype(BF16)
            dwn_ref[...] += dwn
        _, vjp = jax.vjp(_gla_chunk, q_ref[...], k_ref[...], va_ref[...], vb_ref[...], gk_ref[...], wup_ref[...], b_ref[...],
                         tuple(s_ref[h] for h in range(GLA_HEADS)))
        dq, dk, dva, dvb, dgk, dwup, db, ds = vjp((jnp.concatenate(dos, axis=1), tuple(dst[h] for h in range(GLA_HEADS))))
        for h in range(GLA_HEADS):
            dst[h] = ds[h]
        dp_ref[:, :GLA_DK] = dq.astype(BF16)
        dp_ref[:, GLA_DK:2 * GLA_DK] = dk.astype(BF16)
        dp_ref[:, 2 * GLA_DK:2 * GLA_DK + GLA_HALF] = dva.astype(BF16)
        dp_ref[:, 2 * GLA_DK + GLA_HALF:2 * GLA_DK + D_INNER] = dvb.astype(BF16)
        dwup_ref[...] += dwup
        db_ref[...] += db
        dp_ref[:, GLA_GK_COL:GLA_GK_COL + 128] = dgk.astype(BF16)
        dp_ref[:, GLA_GK_COL + 128:] = jnp.zeros((CHUNK, GLA_PROJ_PAD - GLA_GK_COL - 128), BF16)

        @pl.when(pl.program_id(0) == nc - 1)
        def _():
            finish()

    rc = lambda c: nc - 1 - c
    res = pl.pallas_call(
        body,
        grid=(nc,),
        in_specs=_gla_specs(CHUNK, nc, True) + [
            pl.BlockSpec((CHUNK, D_INNER), lambda c: (rc(c), 0)),
            pl.BlockSpec((None, GLA_HEADS, GLA_HEAD_V, GLA_HEAD_K), lambda c: (rc(c), 0, 0, 0)),
            pl.BlockSpec((CHUNK, D_INNER), lambda c: (rc(c), 0)),
        ] + [ANY] * n,
        out_specs=[
            pl.BlockSpec((CHUNK, GLA_PROJ_PAD), lambda c: (rc(c), 0)),
            pl.BlockSpec((128, GLA_DK), lambda c: (0, 0)),
            pl.BlockSpec((1, GLA_DK), lambda c: (0, 0)),
            pl.BlockSpec((1, GLA_HEAD_V), lambda c: (0, 0)),
        ] + [ANY] * n,
        out_shape=[
            jax.ShapeDtypeStruct((L, GLA_PROJ_PAD), BF16),
            jax.ShapeDtypeStruct((128, GLA_DK), F32),
            jax.ShapeDtypeStruct((1, GLA_DK), F32),
            jax.ShapeDtypeStruct((1, GLA_HEAD_V), F32),
        ] + [jax.ShapeDtypeStruct(a.shape, a.dtype) for a in exchange],
        scratch_shapes=[pltpu.VMEM((GLA_HEADS, GLA_HEAD_V, GLA_HEAD_K), F32)] + _comm_sems(n),
        compiler_params=_cparams(("arbitrary",)),
        name="gla_layer_bwd",
    )(proj, proj, proj, proj, proj, proj, proj, wup, bias, wn, o, s_in, dog, *exchange)
    return res[:4], res[4:]


@jax.custom_vjp
def _expand(v):
    r = v.shape[0]
    left = _iota((r, 128), 1) < SSD_P
    slabs = []
    for p in range(SSD_HEADS // 2):
        a = jnp.broadcast_to(v[:, 2 * p:2 * p + 1], (r, 128))
        b = jnp.broadcast_to(v[:, 2 * p + 1:2 * p + 2], (r, 128))
        slabs.append(jnp.where(left, a, b))
    return jnp.concatenate(slabs, axis=1)


def _expand_fwd(v):
    return _expand(v), None


def _expand_bwd(_, g):
    r = g.shape[0]
    lane = _iota((r, 128), 1)
    left = lane < SSD_P
    dv = jnp.zeros((r, 128), F32)
    for p in range(SSD_HEADS // 2):
        gs = g[:, 128 * p:128 * (p + 1)]
        sa = jnp.sum(jnp.where(left, gs, 0.0), axis=-1, keepdims=True)
        sb = jnp.sum(jnp.where(left, 0.0, gs), axis=-1, keepdims=True)
        dv = dv + jnp.where(lane == 2 * p, sa, 0.0) + jnp.where(lane == 2 * p + 1, sb, 0.0)
    return (dv,)


_expand.defvjp(_expand_fwd, _expand_bwd)

SSD_GW = SSD_HPG * SSD_P
SSD_BC = SSD_GROUPS * SSD_N
SSD_PHASE = 4
SSD_PHASE_FWD = 8


def _ssd_chunk(xs, Bm, Cm, dtp, dtb, alog, dsk, h_in, phase=SSD_PHASE):
    dt = _softplus(dtp + dtb)
    acum = _cumsum_rows(dt * (-jnp.exp(alog)))
    a_last = acum[CHUNK - 1:CHUNK]
    acum_b = _expand(acum)
    w_end = _expand(dt * jnp.exp(a_last - acum))
    d_b = _expand(jnp.broadcast_to(dsk, (8, 128)))[0:1]
    ac_t = jnp.concatenate([acum, acum], axis=0).T
    dt_t = jnp.concatenate([dt, dt], axis=0).T
    lane = _iota((CHUNK, 128), 1)
    left = lane < SSD_P
    causal = (lane & (SSD_P - 1)) <= _iota((CHUNK, 128), 0)
    cd = jnp.exp(ac_t[:, CHUNK - 1:CHUNK])
    ys, h_out = [], []
    for g0 in range(0, SSD_GROUPS, phase):
        cb2, y_off = {}, {}
        for g in range(g0, g0 + phase):
            Bg = Bm[:, g * SSD_N:(g + 1) * SSD_N]
            Cg = Cm[:, g * SSD_N:(g + 1) * SSD_N]
            gs = slice(g * SSD_GW, (g + 1) * SSD_GW)
            cb2[g] = _mxu_nt(Cg, jnp.concatenate([Bg, Bg], axis=0))
            y_off[g] = _mxu_nt(Cg, h_in[gs])
            st = _mxu_tn(xs[:, gs] * w_end[:, gs], Bg)
            hs = [h_in[h * SSD_P:(h + 1) * SSD_P] * cd[h:h + 1] for h in range(g * SSD_HPG, (g + 1) * SSD_HPG)]
            h_out.append(jnp.concatenate(hs, axis=0) + st)
        for p in range(g0 * (SSD_HPG // 2), (g0 + phase) * (SSD_HPG // 2)):
            g, k = divmod(p, SSD_HPG // 2)
            sl = slice(128 * p, 128 * (p + 1))
            ac_c = acum_b[:, sl]
            ac_r = jnp.where(left, ac_t[2 * p:2 * p + 1], ac_t[2 * p + 1:2 * p + 2])
            dt_r = jnp.where(left, dt_t[2 * p:2 * p + 1], dt_t[2 * p + 1:2 * p + 2])
            m2 = cb2[g] * jnp.where(causal, jnp.exp(jnp.minimum(ac_c - ac_r, 0.0)), 0.0) * dt_r
            xsl = xs[:, sl]
            x2 = jnp.concatenate([jnp.where(left, xsl, 0.0), jnp.where(left, 0.0, xsl)], axis=0)
            ys.append(_mxu(m2, x2) + y_off[g][:, 128 * k:128 * (k + 1)] * jnp.exp(ac_c) + xsl * d_b[:, sl])
    return jnp.concatenate(ys, axis=1), jnp.concatenate(h_out, axis=0)


def _ssd_step(xs, Bm, Cm, dtp, dtb, alog, dsk, h_in, z, wn, phase=SSD_PHASE):
    y, h_out = _ssd_chunk(xs, Bm, Cm, dtp, dtb, alog, dsk, h_in, phase)
    return _rms(y * _silu(z), wn), h_out


def _ssd_specs(rows, nc, rev):
    ci = (lambda c: nc - 1 - c) if rev else (lambda c: c)
    vec = pl.BlockSpec((1, 128), lambda c: (0, 0))
    return [
        pl.BlockSpec((rows, D_INNER), lambda c: (ci(c), 0)),
        pl.BlockSpec((rows, SSD_BC), lambda c: (ci(c), D_INNER // SSD_BC)),
        pl.BlockSpec((rows, SSD_BC), lambda c: (ci(c), D_INNER // SSD_BC + 1)),
        pl.BlockSpec((rows, 128), lambda c: (ci(c), SSD_DT_COL // 128)),
        vec, vec, vec,
        pl.BlockSpec((rows, D_INNER), lambda c: (ci(c), 0)),
        pl.BlockSpec((1, D_INNER), lambda c: (0, 0)),
    ]


def _ssd_layer_fwd(xbc, proj, dtb, alog, dsk, wn):
    L = xbc.shape[0]
    sc = min(STEP_CHUNKS, L // CHUNK)
    rows = sc * CHUNK
    nc = L // rows

    def body(xs_ref, b_ref, c_ref, dt_ref, dtb_ref, alog_ref, dsk_ref, z_ref, wn_ref, y_ref, hs_ref, hst):
        @pl.when(pl.program_id(0) == 0)
        def _():
            hst[...] = jnp.zeros(hst.shape, F32)

        h = hst[...]
        for u in range(sc):
            r = slice(u * CHUNK, (u + 1) * CHUNK)
            hs_ref[u] = h
            yn, h = _ssd_step(xs_ref[r], b_ref[r], c_ref[r], dt_ref[r], dtb_ref[...], alog_ref[...], dsk_ref[...], h,
                              z_ref[r], wn_ref[...], SSD_PHASE_FWD)
            y_ref[r] = yn.astype(BF16)
        hst[...] = h

    return pl.pallas_call(
        body,
        grid=(nc,),
        in_specs=_ssd_specs(rows, nc, False),
        out_specs=[
            pl.BlockSpec((rows, D_INNER), lambda c: (c, 0)),
            pl.BlockSpec((sc, D_INNER, SSD_N), lambda c: (c, 0, 0)),
        ],
        out_shape=[
            jax.ShapeDtypeStruct((L, D_INNER), BF16),
            jax.ShapeDtypeStruct((L // CHUNK, D_INNER, SSD_N), F32),
        ],
        scratch_shapes=[pltpu.VMEM((D_INNER, SSD_N), F32)],
        compiler_params=_cparams(("arbitrary",)),
        name="ssd_layer_fwd",
    )(xbc, xbc, xbc, proj, dtb, alog, dsk, proj, wn)


def _ssd_layer_bwd(xbc, proj, dtb, alog, dsk, wn, h_saved, dyn):
    L = xbc.shape[0]
    sc = min(BWD_STEP_CHUNKS, L // CHUNK)
    rows = sc * CHUNK
    nc = L // rows

    def body(xs_ref, b_ref, c_ref, dt_ref, dtb_ref, alog_ref, dsk_ref, z_ref, wn_ref, hs_ref, dyn_ref,
             dp_ref, dx_ref, ddtb_ref, dalog_ref, ddsk_ref, dwn_ref, dhst):
        @pl.when(pl.program_id(0) == 0)
        def _():
            dhst[...] = jnp.zeros(dhst.shape, F32)
            ddtb_ref[...] = jnp.zeros((1, 128), F32)
            dalog_ref[...] = jnp.zeros((1, 128), F32)
            ddsk_ref[...] = jnp.zeros((1, 128), F32)
            dwn_ref[...] = jnp.zeros((1, D_INNER), F32)

        dh = dhst[...]
        for u in reversed(range(sc)):
            r = slice(u * CHUNK, (u + 1) * CHUNK)
            _, vjp = jax.vjp(_ssd_step, xs_ref[r], b_ref[r], c_ref[r], dt_ref[r], dtb_ref[...], alog_ref[...],
                             dsk_ref[...], hs_ref[u], z_ref[r], wn_ref[...])
            dxs, db, dc, ddt, ddtb, dalog, ddsk, dh, dz, dwn = vjp((dyn_ref[r], dh))
            dx_ref[r, :D_INNER] = dxs
            dx_ref[r, D_INNER:D_INNER + SSD_BC] = db
            dx_ref[r, D_INNER + SSD_BC:] = dc
            dp_ref[r, :D_INNER] = dz.astype(BF16)
            dp_ref[r, SSD_DT_COL:SSD_DT_COL + 128] = ddt.astype(BF16)
            ddtb_ref[...] += ddtb
            dalog_ref[...] += dalog
            ddsk_ref[...] += ddsk
            dwn_ref[...] += dwn
        dp_ref[:, D_INNER:SSD_DT_COL] = jnp.zeros((rows, SSD_CONV_DIM), BF16)
        dp_ref[:, SSD_DT_COL + 128:] = jnp.zeros((rows, SSD_PROJ_PAD - SSD_DT_COL - 128), BF16)
        dhst[...] = dh

    rc = lambda c: nc - 1 - c
    vec = pl.BlockSpec((1, 128), lambda c: (0, 0))
    vshape = jax.ShapeDtypeStruct((1, 128), F32)
    return pl.pallas_call(
        body,
        grid=(nc,),
        in_specs=_ssd_specs(rows, nc, True) + [
            pl.BlockSpec((sc, D_INNER, SSD_N), lambda c: (rc(c), 0, 0)),
            pl.BlockSpec((rows, D_INNER), lambda c: (rc(c), 0)),
        ],
        out_specs=[
            pl.BlockSpec((rows, SSD_PROJ_PAD), lambda c: (rc(c), 0)),
            pl.BlockSpec((rows, SSD_CONV_DIM), lambda c: (rc(c), 0)),
            vec, vec, vec,
            pl.BlockSpec((1, D_INNER), lambda c: (0, 0)),
        ],
        out_shape=[
            jax.ShapeDtypeStruct((L, SSD_PROJ_PAD), BF16),
            jax.ShapeDtypeStruct((L, SSD_CONV_DIM), F32),
            vshape, vshape, vshape,
            jax.ShapeDtypeStruct((1, D_INNER), F32),
        ],
        scratch_shapes=[pltpu.VMEM((D_INNER, SSD_N), F32)],
        compiler_params=_cparams(("arbitrary",)),
        name="ssd_layer_bwd",
    )(xbc, xbc, xbc, proj, dtb, alog, dsk, proj, wn, h_saved, dyn)


def _pick(n, options):
    for t in options:
        if n % t == 0:
            return t
    return n


def _token_tile(m, row_bytes, fixed_bytes):
    for t in (MM_TOKEN_TILE, MM_TOKEN_TILE // 2, MM_TOKEN_TILE // 4):
        if m % t == 0 and t * row_bytes + fixed_bytes <= MM_VMEM_BUDGET:
            return t
    return min(m, MM_TOKEN_TILE // 4)


def _comm_out(kind, arrays):
    return [jax.ShapeDtypeStruct(((N_DEV,) + p.shape) if kind == "gather" else p.shape, p.dtype) for p in arrays]


def _mm(a, b, *, name, out_dtype=F32, add=None, comm=("exchange", ())):
    kind, exchange = comm
    M, K = a.shape
    N = b.shape[1]
    tn = _pick(N, MM_TILES)
    tk = _pick(K, MM_TILES)
    nk = K // tk
    row_bytes = 2 * (tk * a.dtype.itemsize + tn * jnp.dtype(out_dtype).itemsize + (tn * 4 if add is not None else 0)) \
        + (tn * 4 if nk > 1 else 0)
    tm = _token_tile(M, row_bytes, 2 * tk * tn * b.dtype.itemsize)
    grid = (M // tm, N // tn, nk)
    n = len(exchange)
    n_in = 2 + (add is not None)

    def body(*refs):
        a_ref, b_ref = refs[:2]
        add_ref = refs[2] if add is not None else None
        p_refs = refs[n_in:n_in + n]
        o_ref = refs[n_in + n]
        out_refs = refs[n_in + n + 1:n_in + 2 * n + 1]
        acc = refs[n_in + 2 * n + 1]
        ids = [pl.program_id(d) for d in range(3)]
        k = ids[2]
        if n:
            start, finish_exchange = (_gather_ops if kind == "gather" else _exchange_ops)(
                p_refs, out_refs, *refs[n_in + 2 * n + 2:])

            @pl.when((ids[0] == 0) & (ids[1] == 0) & (k == 0))
            def _():
                start()

        p = _dot(_bf(a_ref[...]), _bf(b_ref[...]))

        def finish(r):
            if add is not None:
                r = r + add_ref[...]
            o_ref[...] = r.astype(out_dtype)

        if nk == 1:
            finish(p)
        else:
            @pl.when(k == 0)
            def _():
                acc[...] = p

            @pl.when((k > 0) & (k < nk - 1))
            def _():
                acc[...] += p

            @pl.when(k == nk - 1)
            def _():
                finish(acc[...] + p)

        if n:
            @pl.when((ids[0] == grid[0] - 1) & (ids[1] == grid[1] - 1) & (k == nk - 1))
            def _():
                finish_exchange()

    in_specs = [pl.BlockSpec((tm, tk), lambda i, j, k: (i, k)), pl.BlockSpec((tk, tn), lambda i, j, k: (k, j))]
    args = [a, b]
    if add is not None:
        in_specs.append(pl.BlockSpec((tm, tn), lambda i, j, k: (i, j)))
        args.append(add)
    res = pl.pallas_call(
        body,
        grid=grid,
        in_specs=in_specs + [ANY] * n,
        out_specs=[pl.BlockSpec((tm, tn), lambda i, j, k: (i, j))] + [ANY] * n,
        out_shape=[jax.ShapeDtypeStruct((M, N), out_dtype)] + _comm_out(kind, exchange),
        scratch_shapes=[pltpu.VMEM((tm, tn) if nk > 1 else (8, 128), F32)] + (_comm_sems(n) if n else []),
        compiler_params=_cparams(("arbitrary",) * 3 if n else ("parallel", "parallel", "arbitrary")),
        name=name,
    )(*args, *exchange)
    return (res[0], res[1:]) if n else res[0]


def _mm_tn(a, b, *, name):
    M, K = a.shape
    N = b.shape[1]
    tn = _pick(N, MM_TILES)
    tm = _token_tile(M, 2 * (K * a.dtype.itemsize + tn * b.dtype.itemsize), 2 * K * tn * 4)

    def body(a_ref, b_ref, o_ref):
        _acc_out(o_ref, _dot_tn(_bf(a_ref[...]), _bf(b_ref[...])), pl.program_id(1) == 0)

    return pl.pallas_call(
        body,
        grid=(N // tn, M // tm),
        in_specs=[pl.BlockSpec((tm, K), lambda j, i: (i, 0)), pl.BlockSpec((tm, tn), lambda j, i: (i, j))],
        out_specs=pl.BlockSpec((K, tn), lambda j, i: (0, j)),
        out_shape=jax.ShapeDtypeStruct((K, N), F32),
        compiler_params=_cparams(("parallel", "arbitrary")),
        name=name,
    )(a, b)


def _rms_fwd(x, w, *, name, gather=()):
    L, D = x.shape
    tm = min(TOKEN_TILE, L)
    nt = L // tm
    n = len(gather)

    def body(*refs):
        x_ref, w_ref = refs[:2]
        o_ref = refs[2 + n]
        if n:
            start, finish = _gather_ops(refs[2:2 + n], refs[3 + n:3 + 2 * n], *refs[3 + 2 * n:])

            @pl.when(pl.program_id(0) == 0)
            def _():
                start()

        o_ref[...] = _rms(x_ref[...], w_ref[...]).astype(BF16)
        if n:
            @pl.when(pl.program_id(0) == nt - 1)
            def _():
                finish()

    res = pl.pallas_call(
        body, grid=(nt,),
        in_specs=[pl.BlockSpec((tm, D), lambda i: (i, 0)), pl.BlockSpec((1, D), lambda i: (0, 0))] + [ANY] * n,
        out_specs=[pl.BlockSpec((tm, D), lambda i: (i, 0))] + [ANY] * n,
        out_shape=[jax.ShapeDtypeStruct((L, D), BF16)] + _comm_out("gather", gather),
        scratch_shapes=_comm_sems(n) if n else [],
        compiler_params=_cparams(("arbitrary",) if n else ("parallel",)), name=name,
    )(x, w, *gather)
    return (res[0], res[1:]) if n else res[0]


def _rms_bwd(x, w, dhn, dres, *, name):
    L, D = x.shape
    tm = min(TOKEN_TILE, L)

    def body(x_ref, w_ref, dhn_ref, dres_ref, dx_ref, dw_ref):
        _, vjp = jax.vjp(_rms, x_ref[...], w_ref[...])
        dx, dw = vjp(dhn_ref[...])
        dx_ref[...] = dx + dres_ref[...]
        _acc_out(dw_ref, dw, pl.program_id(0) == 0)

    row = pl.BlockSpec((tm, D), lambda i: (i, 0))
    vec = pl.BlockSpec((1, D), lambda i: (0, 0))
    return pl.pallas_call(
        body, grid=(L // tm,),
        in_specs=[row, vec, row, row],
        out_specs=[row, vec],
        out_shape=[jax.ShapeDtypeStruct((L, D), F32), jax.ShapeDtypeStruct((1, D), F32)],
        compiler_params=_cparams(("arbitrary",)), name=name,
    )(x, w, dhn, dres)


CONV_HALO = 8
CONV_COLS = 1024


CONV_RB = 64
CONV_CB = 256
CONV_FWD_PIECE = (128, 128)


def _conv_pieces(tm, rb=CONV_RB, cb=CONV_CB):
    return [(r0, c0) for c0 in range(0, CONV_COLS, cb) for r0 in range(0, tm, min(rb, tm))]


def _conv_fwd(proj, w, b):
    L = proj.shape[0]
    tm = min(TOKEN_TILE, L)
    rb, cb = min(CONV_FWD_PIECE[0], tm), CONV_FWD_PIECE[1]
    c0 = D_INNER // CONV_COLS
    hb = tm // CONV_HALO

    def body(x_ref, h_ref, w_ref, b_ref, o_ref, sp_ref, xx):
        xx[0:CONV_HALO] = jnp.where(pl.program_id(1) == 0, 0.0, h_ref[...])
        xx[CONV_HALO:CONV_HALO + tm] = x_ref[...]
        for r0, cc in _conv_pieces(tm, rb, cb):
            cs = slice(cc, cc + cb)
            u = b_ref[:, cs]
            for k in range(SSD_CONV):
                off = CONV_HALO - (SSD_CONV - 1) + k + r0
                u = u + w_ref[k:k + 1, cs] * xx[off:off + rb, cs]
            s = 1.0 / (1.0 + jnp.exp(-u))
            o_ref[r0:r0 + rb, cs] = u * s
            sp_ref[r0:r0 + rb, cs] = s * (1.0 + u * (1.0 - s))

    blk = pl.BlockSpec((tm, CONV_COLS), lambda j, i: (i, j))
    shp = jax.ShapeDtypeStruct((L, SSD_CONV_DIM), F32)
    return pl.pallas_call(
        body, grid=(SSD_CONV_DIM // CONV_COLS, L // tm),
        in_specs=[
            pl.BlockSpec((tm, CONV_COLS), lambda j, i: (i, c0 + j)),
            pl.BlockSpec((CONV_HALO, CONV_COLS), lambda j, i: (jnp.maximum(i * hb - 1, 0), c0 + j)),
            pl.BlockSpec((SSD_CONV, CONV_COLS), lambda j, i: (0, j)),
            pl.BlockSpec((1, CONV_COLS), lambda j, i: (0, j)),
        ],
        out_specs=[blk, blk],
        out_shape=[shp, shp],
        scratch_shapes=[pltpu.VMEM((CONV_HALO + tm, CONV_COLS), F32)],
        compiler_params=_cparams(("parallel", "parallel")), name="conv_fwd",
    )(proj, proj, w, b)


def _conv_bwd(proj, w, sp, dxbc, dproj):
    L = proj.shape[0]
    tm = min(TOKEN_TILE, L)
    rb = min(CONV_RB, tm)
    nt = L // tm
    c0 = D_INNER // CONV_COLS
    hb = tm // CONV_HALO

    def fold(a):
        return jnp.sum(a.reshape(rb // 8, 8, CONV_CB), axis=0)

    def body(x_ref, h_ref, w_ref, sp_ref, dy_ref, dp_in_ref, dp_ref, dw_ref, db_ref, xx, dd):
        del dp_in_ref
        i = pl.program_id(1)
        first = i == 0

        @pl.when(first)
        def _():
            dd[tm:tm + CONV_HALO] = jnp.zeros((CONV_HALO, CONV_COLS), F32)

        xx[0:CONV_HALO] = jnp.where(i == nt - 1, 0.0, h_ref[...])
        xx[CONV_HALO:CONV_HALO + tm] = x_ref[...]
        dws, dbs = [], []
        for cc in range(0, CONV_COLS, CONV_CB):
            cs = slice(cc, cc + CONV_CB)
            acc = [jnp.zeros((8, CONV_CB), F32) for _ in range(SSD_CONV + 1)]
            for r0 in range(0, tm, rb):
                du = dy_ref[r0:r0 + rb, cs] * sp_ref[r0:r0 + rb, cs]
                dd[r0:r0 + rb, cs] = du
                for k in range(SSD_CONV):
                    off = CONV_HALO - (SSD_CONV - 1) + k + r0
                    acc[k] = acc[k] + fold(du * xx[off:off + rb, cs])
                acc[SSD_CONV] = acc[SSD_CONV] + fold(du)
            dws.append(jnp.concatenate([jnp.sum(a, axis=0, keepdims=True) for a in acc[:SSD_CONV]], axis=0))
            dbs.append(jnp.sum(acc[SSD_CONV], axis=0, keepdims=True))
        for r0, cc in _conv_pieces(tm):
            cs = slice(cc, cc + CONV_CB)
            dx = jnp.zeros((rb, CONV_CB), F32)
            for k in range(SSD_CONV):
                off = SSD_CONV - 1 - k + r0
                dx = dx + w_ref[k:k + 1, cs] * dd[off:off + rb, cs]
            dp_ref[r0:r0 + rb, cs] = dx.astype(BF16)
        _acc_out(dw_ref, jnp.concatenate(dws, axis=1), first)
        _acc_out(db_ref, jnp.concatenate(dbs, axis=1), first)
        dd[tm:tm + CONV_HALO] = dd[0:CONV_HALO]

    rt = lambda i: nt - 1 - i
    return pl.pallas_call(
        body, grid=(SSD_CONV_DIM // CONV_COLS, nt),
        in_specs=[
            pl.BlockSpec((tm, CONV_COLS), lambda j, i: (rt(i), c0 + j)),
            pl.BlockSpec((CONV_HALO, CONV_COLS), lambda j, i: (jnp.maximum(rt(i) * hb - 1, 0), c0 + j)),
            pl.BlockSpec((SSD_CONV, CONV_COLS), lambda j, i: (0, j)),
            pl.BlockSpec((tm, CONV_COLS), lambda j, i: (rt(i), j)),
            pl.BlockSpec((tm, CONV_COLS), lambda j, i: (rt(i), j)),
            pl.BlockSpec(memory_space=pl.ANY),
        ],
        out_specs=[
            pl.BlockSpec((tm, CONV_COLS), lambda j, i: (rt(i), c0 + j)),
            pl.BlockSpec((SSD_CONV, CONV_COLS), lambda j, i: (0, j)),
            pl.BlockSpec((1, CONV_COLS), lambda j, i: (0, j)),
        ],
        out_shape=[jax.ShapeDtypeStruct((L, SSD_PROJ_PAD), BF16), jax.ShapeDtypeStruct((SSD_CONV, SSD_CONV_DIM), F32),
                   jax.ShapeDtypeStruct((1, SSD_CONV_DIM), F32)],
        scratch_shapes=[pltpu.VMEM((CONV_HALO + tm, CONV_COLS), F32), pltpu.VMEM((tm + CONV_HALO, CONV_COLS), F32)],
        input_output_aliases={5: 0},
        compiler_params=_cparams(("arbitrary", "arbitrary")), name="conv_bwd",
    )(proj, proj, w, sp, dxbc, dproj)


def _loss_bwd(x, tgt, w):
    L, D = x.shape
    tm = min(TOKEN_TILE, L)

    def body(x_ref, t_ref, w_ref, l_ref, dx_ref, dw_ref):
        xv = x_ref[...]
        wv = w_ref[...]
        r = lax.rsqrt(jnp.mean(xv * xv, axis=-1, keepdims=True) + RMS_EPS)
        xh = xv * r
        e = xh * wv - t_ref[...]
        lsum = 0.5 * jnp.sum(jnp.mean(e * e, axis=-1, keepdims=True), axis=0, keepdims=True)
        dout = e * (1.0 / D)
        gx = dout * wv
        dx_ref[...] = r * (gx - xh * jnp.mean(gx * xh, axis=-1, keepdims=True))
        first = pl.program_id(0) == 0
        _acc_out(dw_ref, jnp.sum(dout * xh, axis=0, keepdims=True), first)
        _acc_out(l_ref, jnp.broadcast_to(lsum, (8, 128)), first)

    row = pl.BlockSpec((tm, D), lambda i: (i, 0))
    vec = pl.BlockSpec((1, D), lambda i: (0, 0))
    return pl.pallas_call(
        body, grid=(L // tm,),
        in_specs=[row, row, vec],
        out_specs=[pl.BlockSpec((8, 128), lambda i: (0, 0)), row, vec],
        out_shape=[jax.ShapeDtypeStruct((8, 128), F32), jax.ShapeDtypeStruct((L, D), F32), jax.ShapeDtypeStruct((1, D), F32)],
        compiler_params=_cparams(("arbitrary",)), name="loss_bwd",
    )(x, tgt, w)


MESH = pl.DeviceIdType.MESH
ANY = pl.BlockSpec(memory_space=pl.ANY)


def _comm_sems(n):
    return [pltpu.SemaphoreType.DMA((n, 7)), pltpu.SemaphoreType.DMA((n, 7)), pltpu.SemaphoreType.DMA((n,))]


def _gather_ops(x_refs, out_refs, send_sems, recv_sems, local_sems):
    n = len(x_refs)
    x, y, c = lax.axis_index("x"), lax.axis_index("y"), lax.axis_index("c")
    me, sibling = (x, y, c), (x, y, 1 - c)
    chips = [(1 - x, y), (x, 1 - y), (1 - x, 1 - y)]

    def slot(i, px, py, pc):
        return out_refs[i].at[4 * px + 2 * py + pc]

    def copy(i, k, block, to, src=None):
        return pltpu.make_async_remote_copy(
            src_ref=slot(i, *block) if src is None else src, dst_ref=slot(i, *block),
            send_sem=send_sems.at[i, k], recv_sem=recv_sems.at[i, k], device_id=to, device_id_type=MESH)

    def own():
        mine = [pltpu.make_async_copy(x_refs[i], slot(i, *me), local_sems.at[i]) for i in range(n)]
        first = [copy(i, 0, me, sibling, src=x_refs[i]) for i in range(n)]
        first += [copy(i, 1 + j, me, (*chip, c), src=x_refs[i]) for j, chip in enumerate(chips) for i in range(n)]
        return mine, first

    def start():
        mine, first = own()
        for cp in mine + first:
            cp.start()

    def finish():
        mine, first = own()
        passed = []
        for j, chip in enumerate(chips):
            for i in range(n):
                copy(i, 1 + j, (*chip, c), me).wait_recv()
                passed.append(copy(i, 4 + j, (*chip, c), sibling))
                passed[-1].start()
        for i in range(n):
            copy(i, 0, sibling, me).wait_recv()
        for j, chip in enumerate(chips):
            for i in range(n):
                copy(i, 4 + j, (*chip, 1 - c), me).wait_recv()
        for cp in first + passed:
            cp.wait_send()
        for cp in mine:
            cp.wait()

    return start, finish


def _exchange_ops(p_refs, out_refs, send_sems, recv_sems, local_sems):
    n = len(p_refs)
    x, y, c = lax.axis_index("x"), lax.axis_index("y"), lax.axis_index("c")
    my = 4 * x + 2 * y + c

    def peer(k):
        fx, fy, fc = (k >> 2) & 1, (k >> 1) & 1, k & 1
        px, py, pc = (1 - x if fx else x), (1 - y if fy else y), (1 - c if fc else c)
        return (px, py, pc), 4 * px + 2 * py + pc

    def mine():
        return [pltpu.make_async_copy(p_refs[i].at[my], out_refs[i].at[my], local_sems.at[i]) for i in range(n)]

    def start():
        for cp in mine():
            cp.start()
        for k in range(1, N_DEV):
            to, pid = peer(k)
            for i in range(n):
                pltpu.make_async_remote_copy(
                    src_ref=p_refs[i].at[pid], dst_ref=out_refs[i].at[my], send_sem=send_sems.at[i, k - 1],
                    recv_sem=recv_sems.at[i, k - 1], device_id=to, device_id_type=MESH).start()

    def finish():
        for k in range(1, N_DEV):
            to, pid = peer(k)
            for i in range(n):
                pltpu.make_async_remote_copy(
                    src_ref=p_refs[i].at[pid], dst_ref=out_refs[i].at[pid], send_sem=send_sems.at[i, k - 1],
                    recv_sem=recv_sems.at[i, k - 1], device_id=to, device_id_type=MESH).wait()
        for cp in mine():
            cp.wait()

    return start, finish


def _all_gather(xs, *, name):
    n = len(xs)

    def body(*refs):
        start, finish = _gather_ops(refs[:n], refs[n:2 * n], *refs[2 * n:])
        start()
        finish()

    return pl.pallas_call(
        body,
        out_shape=[jax.ShapeDtypeStruct((N_DEV,) + a.shape, a.dtype) for a in xs],
        in_specs=[ANY] * n, out_specs=[ANY] * n, scratch_shapes=_comm_sems(n), name=name,
    )(*xs)


def _adamw(parts, w, m, v, *, name):
    a, b = w.shape
    tr = _pick(a, (256, 128, 64, 32, 16, 8))

    def body(p_ref, w_ref, m_ref, v_ref, g_ref, d_ref, mo_ref, vo_ref):
        g = p_ref[0]
        for s in range(1, N_DEV):
            g = g + p_ref[s]
        mn = ADAM_B1 * m_ref[...] + (1.0 - ADAM_B1) * g
        vn = ADAM_B2 * v_ref[...] + (1.0 - ADAM_B2) * jnp.square(g)
        m_hat = mn / (1.0 - ADAM_B1 ** ADAM_STEP)
        v_hat = vn / (1.0 - ADAM_B2 ** ADAM_STEP)
        g_ref[...] = g
        d_ref[...] = -ADAM_LR * (m_hat / (jnp.sqrt(v_hat) + ADAM_EPS) + ADAM_WD * w_ref[...])
        mo_ref[...] = mn
        vo_ref[...] = vn

    blk = pl.BlockSpec((tr, b), lambda i: (i, 0))
    shp = jax.ShapeDtypeStruct((a, b), F32)
    return pl.pallas_call(
        body, grid=(a // tr,),
        in_specs=[pl.BlockSpec((N_DEV, tr, b), lambda i: (0, i, 0)), blk, blk, blk],
        out_specs=[blk, blk, blk, blk],
        out_shape=[shp, shp, shp, shp],
        compiler_params=_cparams(("parallel",)), name=name,
    )(parts, w, m, v)


def _sum_slots(parts, *, name):
    def body(p_ref, o_ref):
        s = p_ref[0]
        for k in range(1, N_DEV):
            s = s + p_ref[k]
        o_ref[...] = s

    return pl.pallas_call(body, out_shape=jax.ShapeDtypeStruct(parts.shape[1:], parts.dtype), name=name)(parts)


def _col_shards(a, n):
    return a.reshape(a.shape[0], N_DEV, n).transpose(1, 0, 2)


def _from_col_shards(g, cols):
    r = g.shape[1]
    full = g.transpose(1, 0, 2).reshape(r, -1)
    return jnp.pad(full, ((0, 0), (0, cols - full.shape[1])))


def kernel(x, norm_w, gla_in_proj, gla_gate_up, gla_gate_bias, gla_head_norm, gla_out_proj, ssd_in_proj, ssd_conv_w, ssd_conv_b, ssd_dt_bias, ssd_a_log, ssd_d, ssd_gate_norm, ssd_out_proj, final_norm, loss_target, m_norm_w, m_gla_in_proj, m_gla_gate_up, m_gla_gate_bias, m_gla_head_norm, m_gla_out_proj, m_ssd_in_proj, m_ssd_conv_w, m_ssd_conv_b, m_ssd_dt_bias, m_ssd_a_log, m_ssd_d, m_ssd_gate_norm, m_ssd_out_proj, m_final_norm, v_norm_w, v_gla_in_proj, v_gla_gate_up, v_gla_gate_bias, v_gla_head_norm, v_gla_out_proj, v_ssd_in_proj, v_ssd_conv_w, v_ssd_conv_b, v_ssd_dt_bias, v_ssd_a_log, v_ssd_d, v_ssd_gate_norm, v_ssd_out_proj, v_final_norm):
    x0 = x[0]
    tgt = loss_target[0]
    n_gin = GLA_PROJ // N_DEV
    n_sin = SSD_PROJ // N_DEV
    n_up = GLA_DK // N_DEV
    n_cv = SSD_CONV_DIM // N_DEV

    vec128 = lambda a: jnp.pad(a.reshape(1, -1), ((0, 0), (0, 128 - a.size)))
    dtb, alog, dsk = vec128(ssd_dt_bias), vec128(ssd_a_log), vec128(ssd_d)
    nw0, nw1 = norm_w[0:1], norm_w[1:2]

    hn1, (g_gin,) = _rms_fwd(x0, nw0, name="rms1_fwd", gather=[gla_in_proj[0].astype(BF16)])
    w_gin = _from_col_shards(g_gin, GLA_PROJ_PAD)
    proj1, (g_up, g_gout) = _mm(hn1, w_gin, name="gla_in_proj",
                                comm=("gather", [gla_gate_up[0].astype(BF16), gla_out_proj[0].astype(BF16)]))
    wup = jnp.pad(_from_col_shards(g_up, GLA_DK), ((0, 128 - GLA_RANK), (0, 0))).astype(F32)
    w_gout = g_gout.reshape(D_INNER, D_MODEL)
    (o, og, s_saved), (g_sin, g_sout, g_cw, g_cb, g_gn) = _gla_layer_fwd(
        proj1, wup, gla_gate_bias, gla_head_norm,
        [ssd_in_proj[0].astype(BF16), ssd_out_proj[0].astype(BF16), ssd_conv_w[0], ssd_conv_b, ssd_gate_norm])
    w_sin = _from_col_shards(g_sin, SSD_PROJ_PAD)
    w_sout = g_sout.reshape(D_INNER, D_MODEL)
    conv_w = _from_col_shards(g_cw, SSD_CONV_DIM)
    conv_b = g_cb.reshape(1, SSD_CONV_DIM)
    gate_norm = g_gn.reshape(1, D_INNER)
    x1 = _mm(og, w_gout, add=x0, name="gla_out_proj")
    hn2 = _rms_fwd(x1, nw1, name="rms2_fwd")
    proj2 = _mm(hn2, w_sin, name="ssd_in_proj")
    xbc, conv_sp = _conv_fwd(proj2, conv_w, conv_b)
    yn, h_saved = _ssd_layer_fwd(xbc, proj2, dtb, alog, dsk, gate_norm)
    x2 = _mm(yn, w_sout, add=x1, name="ssd_out_proj")
    lsum, dx2, d_final = _loss_bwd(x2, tgt, final_norm.reshape(1, D_MODEL))

    d_sout = _mm_tn(yn, dx2, name="ssd_out_proj_dw")
    dyn = _mm(dx2, w_sout.T, name="ssd_out_proj_dx")
    dproj2, dxbc, d_dtb, d_alog, d_dsk, d_gate_norm = _ssd_layer_bwd(xbc, proj2, dtb, alog, dsk, gate_norm, h_saved, dyn)
    dproj2, d_conv_w, d_conv_b = _conv_bwd(proj2, conv_w, conv_sp, dxbc, dproj2)
    d_sin = _mm_tn(hn2, dproj2, name="ssd_in_proj_dw")
    dhn2 = _mm(dproj2, w_sin.T, name="ssd_in_proj_dx")
    dx1, d_nw1 = _rms_bwd(x1, nw1, dhn2, dx2, name="rms2_bwd")
    d_gout = _mm_tn(og, dx1, name="gla_out_proj_dw")
    dog = _mm(dx1, w_gout.T, name="gla_out_proj_dx")
    early = {
        "gla_out_proj": ((gla_out_proj[0], m_gla_out_proj[0], v_gla_out_proj[0]), d_gout.reshape(N_DEV, -1, D_MODEL)),
        "ssd_in_proj": ((ssd_in_proj[0], m_ssd_in_proj[0], v_ssd_in_proj[0]), _col_shards(d_sin[:, :SSD_PROJ], n_sin)),
        "ssd_conv_w": ((ssd_conv_w[0], m_ssd_conv_w[0], v_ssd_conv_w[0]), _col_shards(d_conv_w, n_cv)),
        "ssd_conv_b": ((ssd_conv_b, m_ssd_conv_b, v_ssd_conv_b), d_conv_b.reshape(N_DEV, 1, n_cv)),
        "ssd_gate_norm": ((ssd_gate_norm, m_ssd_gate_norm, v_ssd_gate_norm), d_gate_norm.reshape(N_DEV, 1, -1)),
        "ssd_out_proj": ((ssd_out_proj[0], m_ssd_out_proj[0], v_ssd_out_proj[0]), d_sout.reshape(N_DEV, -1, D_MODEL)),
    }
    (dproj1, d_wup, d_gbias, d_head_norm), early_recv = _gla_layer_bwd(
        proj1, wup, gla_gate_bias, gla_head_norm, o, s_saved, dog, [p for _, p in early.values()])
    d_gin = _mm_tn(hn1, dproj1, name="gla_in_proj_dw")
    late = {
        "gla_in_proj": ((gla_in_proj[0], m_gla_in_proj[0], v_gla_in_proj[0]), _col_shards(d_gin[:, :GLA_PROJ], n_gin)),
        "gla_gate_up": ((gla_gate_up[0], m_gla_gate_up[0], v_gla_gate_up[0]), _col_shards(d_wup[:GLA_RANK], n_up)),
    }
    dhn1, late_recv = _mm(dproj1, w_gin.T, name="gla_in_proj_dx", comm=("exchange", [p for _, p in late.values()]))
    dx0, d_nw0 = _rms_bwd(x0, nw0, dhn1, dx1, name="rms1_bwd")
    heads = SSD_HEADS
    replicated = {
        "norm_w": ((norm_w, m_norm_w, v_norm_w), jnp.concatenate([d_nw0, d_nw1], axis=0)),
        "gla_gate_bias": ((gla_gate_bias, m_gla_gate_bias, v_gla_gate_bias), d_gbias),
        "gla_head_norm": ((gla_head_norm, m_gla_head_norm, v_gla_head_norm), d_head_norm),
        "ssd_dt_bias": ((ssd_dt_bias, m_ssd_dt_bias, v_ssd_dt_bias), d_dtb[:, :heads]),
        "ssd_a_log": ((ssd_a_log, m_ssd_a_log, v_ssd_a_log), d_alog[:, :heads]),
        "ssd_d": ((ssd_d, m_ssd_d, v_ssd_d), d_dsk[:, :heads]),
        "final_norm": (tuple(t.reshape(1, D_MODEL) for t in (final_norm, m_final_norm, v_final_norm)), d_final),
    }
    rep_recv = _all_gather([p for _, p in replicated.values()] + [lsum[0:1]], name="replicated_gather")
    loss = _sum_slots(rep_recv[-1], name="loss_sum")[0, 0]
    results = {}
    for group, recv in ((early, early_recv), (late, late_recv), (replicated, rep_recv)):
        for (nm, ((w, m, v), _)), r in zip(group.items(), recv):
            results[nm] = _adamw(r, w, m, v, name=nm + "_adamw")

    order = [("norm_w", norm_w), ("gla_in_proj", gla_in_proj), ("gla_gate_up", gla_gate_up), ("gla_gate_bias", gla_gate_bias),
             ("gla_head_norm", gla_head_norm), ("gla_out_proj", gla_out_proj), ("ssd_in_proj", ssd_in_proj),
             ("ssd_conv_w", ssd_conv_w), ("ssd_conv_b", ssd_conv_b), ("ssd_dt_bias", ssd_dt_bias), ("ssd_a_log", ssd_a_log),
             ("ssd_d", ssd_d), ("ssd_gate_norm", ssd_gate_norm), ("ssd_out_proj", ssd_out_proj), ("final_norm", final_norm)]
    out = [loss, dx0[None]]
    for i in range(4):
        out += [results[nm][i].reshape(ref.shape) for nm, ref in order]
    return tuple(out)
```

```python
import jax
import jax.numpy as jnp
from jax import lax
from jax.experimental import pallas as pl
from jax.experimental.pallas import tpu as pltpu

F32 = jnp.float32
BF16 = jnp.bfloat16

D_MODEL = 1024
D_INNER = 2048
RMS_EPS = 1e-6
GLA_HEADS = 4
GLA_DK = 512
GLA_HEAD_K = 128
GLA_HEAD_V = 512
GLA_RANK = 16
GLA_NORMALIZER = 16.0
CHUNK = 64
SUB = 16
STEP_CHUNKS = 8
BWD_STEP_CHUNKS = 1
GLA_BWD_STEP_CHUNKS = 2
GLA_PROJ = 5136
GLA_PROJ_PAD = 5376
GLA_GK_COL = 5120
SSD_HEADS = 32
SSD_GROUPS = 8
SSD_HPG = 4
SSD_P = 64
SSD_N = 128
SSD_CONV = 4
SSD_CONV_DIM = 4096
SSD_PROJ = 6176
SSD_PROJ_PAD = 6400
SSD_DT_COL = 6144
N_DEV = 8

ADAM_LR = 0.001
ADAM_B1 = 0.9
ADAM_B2 = 0.999
ADAM_EPS = 1e-08
ADAM_WD = 0.01
ADAM_STEP = 10

VMEM_LIMIT = 56 * 1024 * 1024
TOKEN_TILE = 512
MM_TOKEN_TILE = 2048
MM_VMEM_BUDGET = 44 * 1024 * 1024
MM_TILES = (2048, 1792, 1280, 1024, 768, 512, 256, 128)


def _dot(a, b):
    return jnp.dot(a, b, preferred_element_type=F32)


def _dot_nt(a, b):
    return lax.dot_general(a, b, (((1,), (1,)), ((), ())), preferred_element_type=F32)


def _dot_tn(a, b):
    return lax.dot_general(a, b, (((0,), (0,)), ((), ())), preferred_element_type=F32)


def _bf(a):
    return a.astype(BF16)


@jax.custom_vjp
def _mxu(a, b):
    return _dot(_bf(a), _bf(b))


def _mxu_fwd(a, b):
    return _mxu(a, b), (a, b)


def _mxu_bwd(res, g):
    a, b = res
    return _dot_nt(_bf(g), _bf(b)), _dot_tn(_bf(a), _bf(g))


_mxu.defvjp(_mxu_fwd, _mxu_bwd)


@jax.custom_vjp
def _mxu_nt(a, b):
    return _dot_nt(_bf(a), _bf(b))


def _mxu_nt_fwd(a, b):
    return _mxu_nt(a, b), (a, b)


def _mxu_nt_bwd(res, g):
    a, b = res
    return _dot(_bf(g), _bf(b)), _dot_tn(_bf(g), _bf(a))


_mxu_nt.defvjp(_mxu_nt_fwd, _mxu_nt_bwd)


@jax.custom_vjp
def _mxu_tn(a, b):
    return _dot_tn(_bf(a), _bf(b))


def _mxu_tn_fwd(a, b):
    return _mxu_tn(a, b), (a, b)


def _mxu_tn_bwd(res, g):
    a, b = res
    return _dot_nt(_bf(b), _bf(g)), _dot(_bf(a), _bf(g))


_mxu_tn.defvjp(_mxu_tn_fwd, _mxu_tn_bwd)


def _split2(a):
    hi = _bf(a)
    return hi, _bf(a - hi.astype(F32))


def _three_pass(dot, a, b):
    ah, al = _split2(a)
    bh, bl = _split2(b)
    return dot(ah, bh) + (dot(ah, bl) + dot(al, bh))


@jax.custom_vjp
def _dot3_nt(a, b):
    return _three_pass(_dot_nt, a, b)


def _dot3_nt_fwd(a, b):
    return _dot3_nt(a, b), (a, b)


def _dot3_nt_bwd(res, g):
    a, b = res
    return _three_pass(_dot, g, b), _three_pass(_dot_tn, g, a)


_dot3_nt.defvjp(_dot3_nt_fwd, _dot3_nt_bwd)


def _silu(x):
    return x / (1.0 + jnp.exp(-x))


def _log_sigmoid(z):
    return jnp.minimum(z, 0.0) - jnp.log(1.0 + jnp.exp(-jnp.abs(z)))


def _softplus(z):
    return jnp.maximum(z, 0.0) + jnp.log(1.0 + jnp.exp(-jnp.abs(z)))


def _iota(shape, dim):
    return lax.broadcasted_iota(jnp.int32, shape, dim)


def _rms(x, w):
    return x * lax.rsqrt(jnp.mean(x * x, axis=-1, keepdims=True) + RMS_EPS) * w


def _scan_rows(a, reverse, seg):
    n = a.shape[0]
    pos = _iota(a.shape, 0) & (seg - 1)
    sh = 1
    while sh < seg:
        if reverse:
            a = a + jnp.where(pos < seg - sh, pltpu.roll(a, n - sh, 0), 0.0)
        else:
            a = a + jnp.where(pos >= sh, pltpu.roll(a, sh, 0), 0.0)
        sh *= 2
    return a


def _make_cumsum(seg):
    @jax.custom_vjp
    def cumsum(a):
        return _scan_rows(a, False, seg)

    cumsum.defvjp(lambda a: (_scan_rows(a, False, seg), None), lambda _, g: (_scan_rows(g, True, seg),))
    return cumsum


_cumsum_sub = _make_cumsum(SUB)
_cumsum_rows = _make_cumsum(CHUNK)


def _cparams(sem):
    return pltpu.CompilerParams(dimension_semantics=sem, vmem_limit_bytes=VMEM_LIMIT)


def _acc_out(ref, val, first):
    @pl.when(first)
    def _():
        ref[...] = val

    @pl.when(jnp.logical_not(first))
    def _():
        ref[...] += val


def _gla_chunk(q, k, va, vb, gk, wup, bias, sts):
    nb = CHUNK // SUB
    heads = range(GLA_HEADS)
    hc = lambda a, h: a[:, h * GLA_HEAD_K:(h + 1) * GLA_HEAD_K]
    v = [(va if h < 2 else vb)[:, (h % 2) * GLA_HEAD_V:(h % 2 + 1) * GLA_HEAD_V] for h in heads]
    z = _mxu(gk, wup) + bias
    la = _log_sigmoid(z) * (1.0 / GLA_NORMALIZER)
    qs = q * (GLA_HEAD_K ** -0.5)
    bl = _cumsum_sub(la)
    tot = [jnp.sum(la[i * SUB:(i + 1) * SUB], axis=0, keepdims=True) for i in range(nb)]
    pre = [jnp.zeros((1, GLA_DK), F32)]
    for i in range(nb):
        pre.append(pre[i] + tot[i])
    b_last = pre[nb]
    rows_of = lambda vals: jnp.concatenate([jnp.broadcast_to(t, (SUB, GLA_DK)) for t in vals], axis=0)
    suf = rows_of(tot) - bl
    nxt = rows_of(pre[1:])
    q_in = qs * jnp.exp(bl + rows_of(pre[:nb]))
    k_end = k * jnp.exp(suf + (b_last - nxt))
    dec = jnp.exp(b_last)
    qa = qs * jnp.exp(bl)
    o_inter = [_mxu_nt(hc(q_in, h), sts[h]) for h in heads]
    sts_new = tuple(sts[h] * hc(dec, h) + _mxu_tn(v[h], hc(k_end, h)) for h in heads)
    half = SUB // 2
    rs = _iota((SUB, GLA_HEAD_K), 0)
    cs = _iota((half, CHUNK), 1)
    a_rows = [[] for _ in heads]
    for i in range(nb):
        sl = slice(i * SUB, (i + 1) * SUB)
        n = i * SUB
        if i > 0:
            kp = jnp.concatenate([k[:n] * jnp.exp(suf[:n] + (pre[i] - nxt[:n])), jnp.zeros((CHUNK - n, GLA_DK), F32)], axis=0)
        for h in heads:
            q_i, k_i, bl_i = hc(qs[sl], h), hc(k[sl], h), hc(bl[sl], h)
            if i > 0:
                a_i = _dot3_nt(hc(qa[sl], h), hc(kp, h))
                a_top, a_bot = a_i[:half], a_i[half:]
            else:
                a_top = a_bot = jnp.zeros((half, CHUNK), F32)
            for j in range(SUB):
                lo = 0 if j < half else half
                e = jnp.exp(jnp.minimum(bl_i[lo:] - bl_i[j:j + 1], 0.0))
                t = jnp.where(rs[lo:] >= j, q_i[lo:] * e * k_i[j:j + 1], 0.0)
                rsum = jnp.sum(t, axis=-1, keepdims=True)
                hit = cs == i * SUB + j
                if lo == 0:
                    a_top = a_top + jnp.where(hit, rsum[:half], 0.0)
                a_bot = a_bot + jnp.where(hit, rsum[half - lo:], 0.0)
            a_rows[h] += [a_top, a_bot]
    o = [o_inter[h] + _mxu(jnp.concatenate(a_rows[h], axis=0), v[h]) for h in heads]
    return jnp.concatenate(o, axis=1), sts_new


def _gla_post(o, g, wn):
    return _rms(o, wn) * _silu(g)


GLA_HALF = 2 * GLA_HEAD_V


def _gla_specs(rows, nc, rev):
    ci = (lambda c: nc - 1 - c) if rev else (lambda c: c)
    v0 = 2 * GLA_DK // GLA_HALF
    g0 = (2 * GLA_DK + D_INNER) // GLA_HALF
    return [
        pl.BlockSpec((rows, GLA_DK), lambda c: (ci(c), 0)),
        pl.BlockSpec((rows, GLA_DK), lambda c: (ci(c), 1)),
        pl.BlockSpec((rows, GLA_HALF), lambda c: (ci(c), v0)),
        pl.BlockSpec((rows, GLA_HALF), lambda c: (ci(c), v0 + 1)),
        pl.BlockSpec((rows, GLA_HALF), lambda c: (ci(c), g0)),
        pl.BlockSpec((rows, GLA_HALF), lambda c: (ci(c), g0 + 1)),
        pl.BlockSpec((rows, 128), lambda c: (ci(c), GLA_GK_COL // 128)),
        pl.BlockSpec((128, GLA_DK), lambda c: (0, 0)),
        pl.BlockSpec((1, GLA_DK), lambda c: (0, 0)),
        pl.BlockSpec((1, GLA_HEAD_V), lambda c: (0, 0)),
    ]


def _head_cols(ref_a, ref_b, h, rows=slice(None)):
    ref = ref_a if h < 2 else ref_b
    return ref[rows, (h % 2) * GLA_HEAD_V:(h % 2 + 1) * GLA_HEAD_V]


def _gla_layer_fwd(proj, wup, bias, wn, gather):
    L = proj.shape[0]
    sc = min(STEP_CHUNKS, L // CHUNK)
    rows = sc * CHUNK
    nc = L // rows
    n = len(gather)

    def body(*refs):
        q_ref, k_ref, va_ref, vb_ref, ga_ref, gb_ref, gk_ref, wup_ref, b_ref, wn_ref = refs[:10]
        x_refs = refs[10:10 + n]
        o_ref, og_ref, s_ref = refs[10 + n:13 + n]
        out_refs = refs[13 + n:13 + 2 * n]
        st, send_sems, recv_sems, local_sems = refs[13 + 2 * n:]
        start, finish = _gather_ops(x_refs, out_refs, send_sems, recv_sems, local_sems)

        @pl.when(pl.program_id(0) == 0)
        def _():
            st[...] = jnp.zeros(st.shape, F32)
            start()

        s_cur = tuple(st[h] for h in range(GLA_HEADS))
        for u in range(sc):
            r = slice(u * CHUNK, (u + 1) * CHUNK)
            for h in range(GLA_HEADS):
                s_ref[u, h] = s_cur[h]
            o, s_cur = _gla_chunk(q_ref[r], k_ref[r], va_ref[r], vb_ref[r], gk_ref[r], wup_ref[...], b_ref[...], s_cur)
            o_ref[r] = o
            for h in range(GLA_HEADS):
                vc = slice(h * GLA_HEAD_V, (h + 1) * GLA_HEAD_V)
                og_ref[r, vc] = _gla_post(o[:, vc], _head_cols(ga_ref, gb_ref, h, r), wn_ref[...]).astype(BF16)
        for h in range(GLA_HEADS):
            st[h] = s_cur[h]

        @pl.when(pl.program_id(0) == nc - 1)
        def _():
            finish()

    res = pl.pallas_call(
        body,
        grid=(nc,),
        in_specs=_gla_specs(rows, nc, False) + [ANY] * n,
        out_specs=[
            pl.BlockSpec((rows, D_INNER), lambda c: (c, 0)),
            pl.BlockSpec((rows, D_INNER), lambda c: (c, 0)),
            pl.BlockSpec((sc, GLA_HEADS, GLA_HEAD_V, GLA_HEAD_K), lambda c: (c, 0, 0, 0)),
        ] + [ANY] * n,
        out_shape=[
            jax.ShapeDtypeStruct((L, D_INNER), F32),
            jax.ShapeDtypeStruct((L, D_INNER), BF16),
            jax.ShapeDtypeStruct((L // CHUNK, GLA_HEADS, GLA_HEAD_V, GLA_HEAD_K), F32),
        ] + [jax.ShapeDtypeStruct((N_DEV,) + a.shape, a.dtype) for a in gather],
        scratch_shapes=[pltpu.VMEM((GLA_HEADS, GLA_HEAD_V, GLA_HEAD_K), F32)] + _comm_sems(n),
        compiler_params=_cparams(("arbitrary",)),
        name="gla_layer_fwd",
    )(proj, proj, proj, proj, proj, proj, proj, wup, bias, wn, *gather)
    return res[:3], res[3:]


def _gla_layer_bwd(proj, wup, bias, wn, o, s_in, dog, exchange):
    L = proj.shape[0]
    sc = min(GLA_BWD_STEP_CHUNKS, L // CHUNK)
    rows = sc * CHUNK
    nc = L // rows
    n = len(exchange)

    def body(*refs):
        (q_ref, k_ref, va_ref, vb_ref, ga_ref, gb_ref, gk_ref, wup_ref, b_ref, wn_ref, o_ref, s_ref, dog_ref) = refs[:13]
        p_refs = refs[13:13 + n]
        dp_ref, dwup_ref, db_ref, dwn_ref = refs[13 + n:17 + n]
        out_refs = refs[17 + n:17 + 2 * n]
        dst, send_sems, recv_sems, local_sems = refs[17 + 2 * n:]
        start, finish = _exchange_ops(p_refs, out_refs, send_sems, recv_sems, local_sems)

        @pl.when(pl.program_id(0) == 0)
        def _():
            dst[...] = jnp.zeros(dst.shape, F32)
            dwup_ref[...] = jnp.zeros(dwup_ref.shape, F32)
            db_ref[...] = jnp.zeros(db_ref.shape, F32)
            dwn_ref[...] = jnp.zeros(dwn_ref.shape, F32)
            start()

        ds = tuple(dst[h] for h in range(GLA_HEADS))
        for u in reversed(range(sc)):
            r = slice(u * CHUNK, (u + 1) * CHUNK)
            dos = []
            for h in range(GLA_HEADS):
                vc = slice(h * GLA_HEAD_V, (h + 1) * GLA_HEAD_V)
                _, post_vjp = jax.vjp(_gla_post, o_ref[r, vc], _head_cols(ga_ref, gb_ref, h, r), wn_ref[...])
                do, dg, dwn = post_vjp(dog_ref[r, vc])
                dos.append(do)
                dp_ref[r, 2 * GLA_DK + D_INNER + h * GLA_HEAD_V:2 * GLA_DK + D_INNER + (h + 1) * GLA_HEAD_V] = dg.astype(BF16)
                dwn_ref[...] += dwn
            _, vjp = jax.vjp(_gla_chunk, q_ref[r], k_ref[r], va_ref[r], vb_ref[r], gk_ref[r], wup_ref[...], b_ref[...],
                             tuple(s_ref[u, h] for h in range(GLA_HEADS)))
            dq, dk, dva, dvb, dgk, dwup, db, ds = vjp((jnp.concatenate(dos, axis=1), ds))
            dp_ref[r, :GLA_DK] = dq.astype(BF16)
            dp_ref[r, GLA_DK:2 * GLA_DK] = dk.astype(BF16)
            dp_ref[r, 2 * GLA_DK:2 * GLA_DK + GLA_HALF] = dva.astype(BF16)
            dp_ref[r, 2 * GLA_DK + GLA_HALF:2 * GLA_DK + D_INNER] = dvb.astype(BF16)
            dwup_ref[...] += dwup
            db_ref[...] += db
            dp_ref[r, GLA_GK_COL:GLA_GK_COL + 128] = dgk.astype(BF16)
        for h in range(GLA_HEADS):
            dst[h] = ds[h]
        dp_ref[:, GLA_GK_COL + 128:] = jnp.zeros((rows, GLA_PROJ_PAD - GLA_GK_COL - 128), BF16)

        @pl.when(pl.program_id(0) == nc - 1)
        def _():
            finish()

    rc = lambda c: nc - 1 - c
    res = pl.pallas_call(
        body,
        grid=(nc,),
        in_specs=_gla_specs(rows, nc, True) + [
            pl.BlockSpec((rows, D_INNER), lambda c: (rc(c), 0)),
            pl.BlockSpec((sc, GLA_HEADS, GLA_HEAD_V, GLA_HEAD_K), lambda c: (rc(c), 0, 0, 0)),
            pl.BlockSpec((rows, D_INNER), lambda c: (rc(c), 0)),
        ] + [ANY] * n,
        out_specs=[
            pl.BlockSpec((rows, GLA_PROJ_PAD), lambda c: (rc(c), 0)),
            pl.BlockSpec((128, GLA_DK), lambda c: (0, 0)),
            pl.BlockSpec((1, GLA_DK), lambda c: (0, 0)),
            pl.BlockSpec((1, GLA_HEAD_V), lambda c: (0, 0)),
        ] + [ANY] * n,
        out_shape=[
            jax.ShapeDtypeStruct((L, GLA_PROJ_PAD), BF16),
            jax.ShapeDtypeStruct((128, GLA_DK), F32),
            jax.ShapeDtypeStruct((1, GLA_DK), F32),
            jax.ShapeDtypeStruct((1, GLA_HEAD_V), F32),
        ] + [jax.ShapeDtypeStruct(a.shape, a.dtype) for a in exchange],
        scratch_shapes=[pltpu.VMEM((GLA_HEADS, GLA_HEAD_V, GLA_HEAD_K), F32)] + _comm_sems(n),
        compiler_params=_cparams(("arbitrary",)),
        name="gla_layer_bwd",
    )(proj, proj, proj, proj, proj, proj, proj, wup, bias, wn, o, s_in, dog, *exchange)
    return res[:4], res[4:]


@jax.custom_vjp
def _expand(v):
    r = v.shape[0]
    left = _iota((r, 128), 1) < SSD_P
    slabs = []
    for p in range(SSD_HEADS // 2):
        a = jnp.broadcast_to(v[:, 2 * p:2 * p + 1], (r, 128))
        b = jnp.broadcast_to(v[:, 2 * p + 1:2 * p + 2], (r, 128))
        slabs.append(jnp.where(left, a, b))
    return jnp.concatenate(slabs, axis=1)


def _expand_fwd(v):
    return _expand(v), None


def _expand_bwd(_, g):
    r = g.shape[0]
    lane = _iota((r, 128), 1)
    left = lane < SSD_P
    dv = jnp.zeros((r, 128), F32)
    for p in range(SSD_HEADS // 2):
        gs = g[:, 128 * p:128 * (p + 1)]
        sa = jnp.sum(jnp.where(left, gs, 0.0), axis=-1, keepdims=True)
        sb = jnp.sum(jnp.where(left, 0.0, gs), axis=-1, keepdims=True)
        dv = dv + jnp.where(lane == 2 * p, sa, 0.0) + jnp.where(lane == 2 * p + 1, sb, 0.0)
    return (dv,)


_expand.defvjp(_expand_fwd, _expand_bwd)

SSD_GW = SSD_HPG * SSD_P
SSD_BC = SSD_GROUPS * SSD_N
SSD_PHASE = 4
SSD_PHASE_FWD = 8


def _ssd_chunk(xs, Bm, Cm, dtp, dtb, alog, dsk, h_in, phase=SSD_PHASE):
    dt = _softplus(dtp + dtb)
    acum = _cumsum_rows(dt * (-jnp.exp(alog)))
    a_last = acum[CHUNK - 1:CHUNK]
    acum_b = _expand(acum)
    w_end = _expand(dt * jnp.exp(a_last - acum))
    d_b = _expand(jnp.broadcast_to(dsk, (8, 128)))[0:1]
    ac_t = jnp.concatenate([acum, acum], axis=0).T
    dt_t = jnp.concatenate([dt, dt], axis=0).T
    lane = _iota((CHUNK, 128), 1)
    left = lane < SSD_P
    causal = (lane & (SSD_P - 1)) <= _iota((CHUNK, 128), 0)
    cd = jnp.exp(ac_t[:, CHUNK - 1:CHUNK])
    ys, h_out = [], []
    for g0 in range(0, SSD_GROUPS, phase):
        cb2, y_off = {}, {}
        for g in range(g0, g0 + phase):
            Bg = Bm[:, g * SSD_N:(g + 1) * SSD_N]
            Cg = Cm[:, g * SSD_N:(g + 1) * SSD_N]
            gs = slice(g * SSD_GW, (g + 1) * SSD_GW)
            cb2[g] = _mxu_nt(Cg, jnp.concatenate([Bg, Bg], axis=0))
            y_off[g] = _mxu_nt(Cg, h_in[gs])
            st = _mxu_tn(xs[:, gs] * w_end[:, gs], Bg)
            hs = [h_in[h * SSD_P:(h + 1) * SSD_P] * cd[h:h + 1] for h in range(g * SSD_HPG, (g + 1) * SSD_HPG)]
            h_out.append(jnp.concatenate(hs, axis=0) + st)
        for p in range(g0 * (SSD_HPG // 2), (g0 + phase) * (SSD_HPG // 2)):
            g, k = divmod(p, SSD_HPG // 2)
            sl = slice(128 * p, 128 * (p + 1))
            ac_c = acum_b[:, sl]
            ac_r = jnp.where(left, ac_t[2 * p:2 * p + 1], ac_t[2 * p + 1:2 * p + 2])
            dt_r = jnp.where(left, dt_t[2 * p:2 * p + 1], dt_t[2 * p + 1:2 * p + 2])
            m2 = cb2[g] * jnp.where(causal, jnp.exp(jnp.minimum(ac_c - ac_r, 0.0)), 0.0) * dt_r
            xsl = xs[:, sl]
            x2 = jnp.concatenate([jnp.where(left, xsl, 0.0), jnp.where(left, 0.0, xsl)], axis=0)
            ys.append(_mxu(m2, x2) + y_off[g][:, 128 * k:128 * (k + 1)] * jnp.exp(ac_c) + xsl * d_b[:, sl])
    return jnp.concatenate(ys, axis=1), jnp.concatenate(h_out, axis=0)


def _ssd_step(xs, Bm, Cm, dtp, dtb, alog, dsk, h_in, z, wn, phase=SSD_PHASE):
    y, h_out = _ssd_chunk(xs, Bm, Cm, dtp, dtb, alog, dsk, h_in, phase)
    return _rms(y * _silu(z), wn), h_out


def _ssd_specs(rows, nc, rev):
    ci = (lambda c: nc - 1 - c) if rev else (lambda c: c)
    vec = pl.BlockSpec((1, 128), lambda c: (0, 0))
    return [
        pl.BlockSpec((rows, D_INNER), lambda c: (ci(c), 0)),
        pl.BlockSpec((rows, SSD_BC), lambda c: (ci(c), D_INNER // SSD_BC)),
        pl.BlockSpec((rows, SSD_BC), lambda c: (ci(c), D_INNER // SSD_BC + 1)),
        pl.BlockSpec((rows, 128), lambda c: (ci(c), SSD_DT_COL // 128)),
        vec, vec, vec,
        pl.BlockSpec((rows, D_INNER), lambda c: (ci(c), 0)),
        pl.BlockSpec((1, D_INNER), lambda c: (0, 0)),
    ]


def _ssd_layer_fwd(xbc, proj, dtb, alog, dsk, wn):
    L = xbc.shape[0]
    sc = min(STEP_CHUNKS, L // CHUNK)
    rows = sc * CHUNK
    nc = L // rows

    def body(xs_ref, b_ref, c_ref, dt_ref, dtb_ref, alog_ref, dsk_ref, z_ref, wn_ref, y_ref, hs_ref, hst):
        @pl.when(pl.program_id(0) == 0)
        def _():
            hst[...] = jnp.zeros(hst.shape, F32)

        h = hst[...]
        for u in range(sc):
            r = slice(u * CHUNK, (u + 1) * CHUNK)
            hs_ref[u] = h
            yn, h = _ssd_step(xs_ref[r], b_ref[r], c_ref[r], dt_ref[r], dtb_ref[...], alog_ref[...], dsk_ref[...], h,
                              z_ref[r], wn_ref[...], SSD_PHASE_FWD)
            y_ref[r] = yn.astype(BF16)
        hst[...] = h

    return pl.pallas_call(
        body,
        grid=(nc,),
        in_specs=_ssd_specs(rows, nc, False),
        out_specs=[
            pl.BlockSpec((rows, D_INNER), lambda c: (c, 0)),
            pl.BlockSpec((sc, D_INNER, SSD_N), lambda c: (c, 0, 0)),
        ],
        out_shape=[
            jax.ShapeDtypeStruct((L, D_INNER), BF16),
            jax.ShapeDtypeStruct((L // CHUNK, D_INNER, SSD_N), F32),
        ],
        scratch_shapes=[pltpu.VMEM((D_INNER, SSD_N), F32)],
        compiler_params=_cparams(("arbitrary",)),
        name="ssd_layer_fwd",
    )(xbc, xbc, xbc, proj, dtb, alog, dsk, proj, wn)


def _ssd_layer_bwd(xbc, proj, dtb, alog, dsk, wn, h_saved, dyn):
    L = xbc.shape[0]
    sc = min(BWD_STEP_CHUNKS, L // CHUNK)
    rows = sc * CHUNK
    nc = L // rows

    def body(xs_ref, b_ref, c_ref, dt_ref, dtb_ref, alog_ref, dsk_ref, z_ref, wn_ref, hs_ref, dyn_ref,
             dp_ref, dx_ref, ddtb_ref, dalog_ref, ddsk_ref, dwn_ref, dhst):
        @pl.when(pl.program_id(0) == 0)
        def _():
            dhst[...] = jnp.zeros(dhst.shape, F32)
            ddtb_ref[...] = jnp.zeros((1, 128), F32)
            dalog_ref[...] = jnp.zeros((1, 128), F32)
            ddsk_ref[...] = jnp.zeros((1, 128), F32)
            dwn_ref[...] = jnp.zeros((1, D_INNER), F32)

        dh = dhst[...]
        for u in reversed(range(sc)):
            r = slice(u * CHUNK, (u + 1) * CHUNK)
            _, vjp = jax.vjp(_ssd_step, xs_ref[r], b_ref[r], c_ref[r], dt_ref[r], dtb_ref[...], alog_ref[...],
                             dsk_ref[...], hs_ref[u], z_ref[r], wn_ref[...])
            dxs, db, dc, ddt, ddtb, dalog, ddsk, dh, dz, dwn = vjp((dyn_ref[r], dh))
            dx_ref[r, :D_INNER] = dxs
            dx_ref[r, D_INNER:D_INNER + SSD_BC] = db
            dx_ref[r, D_INNER + SSD_BC:] = dc
            dp_ref[r, :D_INNER] = dz.astype(BF16)
            dp_ref[r, SSD_DT_COL:SSD_DT_COL + 128] = ddt.astype(BF16)
            ddtb_ref[...] += ddtb
            dalog_ref[...] += dalog
            ddsk_ref[...] += ddsk
            dwn_ref[...] += dwn
        dp_ref[:, D_INNER:SSD_DT_COL] = jnp.zeros((rows, SSD_CONV_DIM), BF16)
        dp_ref[:, SSD_DT_COL + 128:] = jnp.zeros((rows, SSD_PROJ_PAD - SSD_DT_COL - 128), BF16)
        dhst[...] = dh

    rc = lambda c: nc - 1 - c
    vec = pl.BlockSpec((1, 128), lambda c: (0, 0))
    vshape = jax.ShapeDtypeStruct((1, 128), F32)
    return pl.pallas_call(
        body,
        grid=(nc,),
        in_specs=_ssd_specs(rows, nc, True) + [
            pl.BlockSpec((sc, D_INNER, SSD_N), lambda c: (rc(c), 0, 0)),
            pl.BlockSpec((rows, D_INNER), lambda c: (rc(c), 0)),
        ],
        out_specs=[
            pl.BlockSpec((rows, SSD_PROJ_PAD), lambda c: (rc(c), 0)),
            pl.BlockSpec((rows, SSD_CONV_DIM), lambda c: (rc(c), 0)),
            vec, vec, vec,
            pl.BlockSpec((1, D_INNER), lambda c: (0, 0)),
        ],
        out_shape=[
            jax.ShapeDtypeStruct((L, SSD_PROJ_PAD), BF16),
            jax.ShapeDtypeStruct((L, SSD_CONV_DIM), F32),
            vshape, vshape, vshape,
            jax.ShapeDtypeStruct((1, D_INNER), F32),
        ],
        scratch_shapes=[pltpu.VMEM((D_INNER, SSD_N), F32)],
        compiler_params=_cparams(("arbitrary",)),
        name="ssd_layer_bwd",
    )(xbc, xbc, xbc, proj, dtb, alog, dsk, proj, wn, h_saved, dyn)


def _pick(n, options):
    for t in options:
        if n % t == 0:
            return t
    return n


def _token_tile(m, row_bytes, fixed_bytes):
    for t in (MM_TOKEN_TILE, MM_TOKEN_TILE // 2, MM_TOKEN_TILE // 4):
        if m % t == 0 and t * row_bytes + fixed_bytes <= MM_VMEM_BUDGET:
            return t
    return min(m, MM_TOKEN_TILE // 4)


def _comm_out(kind, arrays):
    return [jax.ShapeDtypeStruct(((N_DEV,) + p.shape) if kind == "gather" else p.shape, p.dtype) for p in arrays]


def _mm(a, b, *, name, out_dtype=F32, add=None, comm=("exchange", ())):
    kind, exchange = comm
    M, K = a.shape
    N = b.shape[1]
    tn = _pick(N, MM_TILES)
    tk = _pick(K, MM_TILES)
    nk = K // tk
    row_bytes = 2 * (tk * a.dtype.itemsize + tn * jnp.dtype(out_dtype).itemsize + (tn * 4 if add is not None else 0)) \
        + (tn * 4 if nk > 1 else 0)
    tm = _token_tile(M, row_bytes, 2 * tk * tn * b.dtype.itemsize)
    grid = (M // tm, N // tn, nk)
    n = len(exchange)
    n_in = 2 + (add is not None)

    def body(*refs):
        a_ref, b_ref = refs[:2]
        add_ref = refs[2] if add is not None else None
        p_refs = refs[n_in:n_in + n]
        o_ref = refs[n_in + n]
        out_refs = refs[n_in + n + 1:n_in + 2 * n + 1]
        acc = refs[n_in + 2 * n + 1]
        ids = [pl.program_id(d) for d in range(3)]
        k = ids[2]
        if n:
            start, finish_exchange = (_gather_ops if kind == "gather" else _exchange_ops)(
                p_refs, out_refs, *refs[n_in + 2 * n + 2:])

            @pl.when((ids[0] == 0) & (ids[1] == 0) & (k == 0))
            def _():
                start()

        p = _dot(_bf(a_ref[...]), _bf(b_ref[...]))

        def finish(r):
            if add is not None:
                r = r + add_ref[...]
            o_ref[...] = r.astype(out_dtype)

        if nk == 1:
            finish(p)
        else:
            @pl.when(k == 0)
            def _():
                acc[...] = p

            @pl.when((k > 0) & (k < nk - 1))
            def _():
                acc[...] += p

            @pl.when(k == nk - 1)
            def _():
                finish(acc[...] + p)

        if n:
            @pl.when((ids[0] == grid[0] - 1) & (ids[1] == grid[1] - 1) & (k == nk - 1))
            def _():
                finish_exchange()

    in_specs = [pl.BlockSpec((tm, tk), lambda i, j, k: (i, k)), pl.BlockSpec((tk, tn), lambda i, j, k: (k, j))]
    args = [a, b]
    if add is not None:
        in_specs.append(pl.BlockSpec((tm, tn), lambda i, j, k: (i, j)))
        args.append(add)
    res = pl.pallas_call(
        body,
        grid=grid,
        in_specs=in_specs + [ANY] * n,
        out_specs=[pl.BlockSpec((tm, tn), lambda i, j, k: (i, j))] + [ANY] * n,
        out_shape=[jax.ShapeDtypeStruct((M, N), out_dtype)] + _comm_out(kind, exchange),
        scratch_shapes=[pltpu.VMEM((tm, tn) if nk > 1 else (8, 128), F32)] + (_comm_sems(n) if n else []),
        compiler_params=_cparams(("arbitrary",) * 3 if n else ("parallel", "parallel", "arbitrary")),
        name=name,
    )(*args, *exchange)
    return (res[0], res[1:]) if n else res[0]


def _mm_tn(a, b, *, name):
    M, K = a.shape
    N = b.shape[1]
    tn = _pick(N, MM_TILES)
    tm = _token_tile(M, 2 * (K * a.dtype.itemsize + tn * b.dtype.itemsize), 2 * K * tn * 4)

    def body(a_ref, b_ref, o_ref):
        _acc_out(o_ref, _dot_tn(_bf(a_ref[...]), _bf(b_ref[...])), pl.program_id(1) == 0)

    return pl.pallas_call(
        body,
        grid=(N // tn, M // tm),
        in_specs=[pl.BlockSpec((tm, K), lambda j, i: (i, 0)), pl.BlockSpec((tm, tn), lambda j, i: (i, j))],
        out_specs=pl.BlockSpec((K, tn), lambda j, i: (0, j)),
        out_shape=jax.ShapeDtypeStruct((K, N), F32),
        compiler_params=_cparams(("parallel", "arbitrary")),
        name=name,
    )(a, b)


def _rms_fwd(x, w, *, name, gather=()):
    L, D = x.shape
    tm = min(TOKEN_TILE, L)
    nt = L // tm
    n = len(gather)

    def body(*refs):
        x_ref, w_ref = refs[:2]
        o_ref = refs[2 + n]
        if n:
            start, finish = _gather_ops(refs[2:2 + n], refs[3 + n:3 + 2 * n], *refs[3 + 2 * n:])

            @pl.when(pl.program_id(0) == 0)
            def _():
                start()

        o_ref[...] = _rms(x_ref[...], w_ref[...]).astype(BF16)
        if n:
            @pl.when(pl.program_id(0) == nt - 1)
            def _():
                finish()

    res = pl.pallas_call(
        body, grid=(nt,),
        in_specs=[pl.BlockSpec((tm, D), lambda i: (i, 0)), pl.BlockSpec((1, D), lambda i: (0, 0))] + [ANY] * n,
        out_specs=[pl.BlockSpec((tm, D), lambda i: (i, 0))] + [ANY] * n,
        out_shape=[jax.ShapeDtypeStruct((L, D), BF16)] + _comm_out("gather", gather),
        scratch_shapes=_comm_sems(n) if n else [],
        compiler_params=_cparams(("arbitrary",) if n else ("parallel",)), name=name,
    )(x, w, *gather)
    return (res[0], res[1:]) if n else res[0]


def _rms_bwd(x, w, dhn, dres, *, name):
    L, D = x.shape
    tm = min(TOKEN_TILE, L)

    def body(x_ref, w_ref, dhn_ref, dres_ref, dx_ref, dw_ref):
        _, vjp = jax.vjp(_rms, x_ref[...], w_ref[...])
        dx, dw = vjp(dhn_ref[...])
        dx_ref[...] = dx + dres_ref[...]
        _acc_out(dw_ref, dw, pl.program_id(0) == 0)

    row = pl.BlockSpec((tm, D), lambda i: (i, 0))
    vec = pl.BlockSpec((1, D), lambda i: (0, 0))
    return pl.pallas_call(
        body, grid=(L // tm,),
        in_specs=[row, vec, row, row],
        out_specs=[row, vec],
        out_shape=[jax.ShapeDtypeStruct((L, D), F32), jax.ShapeDtypeStruct((1, D), F32)],
        compiler_params=_cparams(("arbitrary",)), name=name,
    )(x, w, dhn, dres)


CONV_HALO = 8
CONV_COLS = 1024


CONV_RB = 64
CONV_CB = 256
CONV_FWD_PIECE = (128, 128)


def _conv_pieces(tm, rb=CONV_RB, cb=CONV_CB):
    return [(r0, c0) for c0 in range(0, CONV_COLS, cb) for r0 in range(0, tm, min(rb, tm))]


def _conv_fwd(proj, w, b):
    L = proj.shape[0]
    tm = min(TOKEN_TILE, L)
    rb, cb = min(CONV_FWD_PIECE[0], tm), CONV_FWD_PIECE[1]
    c0 = D_INNER // CONV_COLS
    hb = tm // CONV_HALO

    def body(x_ref, h_ref, w_ref, b_ref, o_ref, sp_ref, xx):
        xx[0:CONV_HALO] = jnp.where(pl.program_id(1) == 0, 0.0, h_ref[...])
        xx[CONV_HALO:CONV_HALO + tm] = x_ref[...]
        for r0, cc in _conv_pieces(tm, rb, cb):
            cs = slice(cc, cc + cb)
            u = b_ref[:, cs]
            for k in range(SSD_CONV):
                off = CONV_HALO - (SSD_CONV - 1) + k + r0
                u = u + w_ref[k:k + 1, cs] * xx[off:off + rb, cs]
            s = 1.0 / (1.0 + jnp.exp(-u))
            o_ref[r0:r0 + rb, cs] = u * s
            sp_ref[r0:r0 + rb, cs] = s * (1.0 + u * (1.0 - s))

    blk = pl.BlockSpec((tm, CONV_COLS), lambda j, i: (i, j))
    shp = jax.ShapeDtypeStruct((L, SSD_CONV_DIM), F32)
    return pl.pallas_call(
        body, grid=(SSD_CONV_DIM // CONV_COLS, L // tm),
        in_specs=[
            pl.BlockSpec((tm, CONV_COLS), lambda j, i: (i, c0 + j)),
            pl.BlockSpec((CONV_HALO, CONV_COLS), lambda j, i: (jnp.maximum(i * hb - 1, 0), c0 + j)),
            pl.BlockSpec((SSD_CONV, CONV_COLS), lambda j, i: (0, j)),
            pl.BlockSpec((1, CONV_COLS), lambda j, i: (0, j)),
        ],
        out_specs=[blk, blk],
        out_shape=[shp, shp],
        scratch_shapes=[pltpu.VMEM((CONV_HALO + tm, CONV_COLS), F32)],
        compiler_params=_cparams(("parallel", "parallel")), name="conv_fwd",
    )(proj, proj, w, b)


def _conv_bwd(proj, w, sp, dxbc, dproj):
    L = proj.shape[0]
    tm = min(TOKEN_TILE, L)
    rb = min(CONV_RB, tm)
    nt = L // tm
    c0 = D_INNER // CONV_COLS
    hb = tm // CONV_HALO

    def fold(a):
        return jnp.sum(a.reshape(rb // 8, 8, CONV_CB), axis=0)

    def body(x_ref, h_ref, w_ref, sp_ref, dy_ref, dp_in_ref, dp_ref, dw_ref, db_ref, xx, dd):
        del dp_in_ref
        i = pl.program_id(1)
        first = i == 0

        @pl.when(first)
        def _():
            dd[tm:tm + CONV_HALO] = jnp.zeros((CONV_HALO, CONV_COLS), F32)

        xx[0:CONV_HALO] = jnp.where(i == nt - 1, 0.0, h_ref[...])
        xx[CONV_HALO:CONV_HALO + tm] = x_ref[...]
        dws, dbs = [], []
        for cc in range(0, CONV_COLS, CONV_CB):
            cs = slice(cc, cc + CONV_CB)
            acc = [jnp.zeros((8, CONV_CB), F32) for _ in range(SSD_CONV + 1)]
            for r0 in range(0, tm, rb):
                du = dy_ref[r0:r0 + rb, cs] * sp_ref[r0:r0 + rb, cs]
                dd[r0:r0 + rb, cs] = du
                for k in range(SSD_CONV):
                    off = CONV_HALO - (SSD_CONV - 1) + k + r0
                    acc[k] = acc[k] + fold(du * xx[off:off + rb, cs])
                acc[SSD_CONV] = acc[SSD_CONV] + fold(du)
            dws.append(jnp.concatenate([jnp.sum(a, axis=0, keepdims=True) for a in acc[:SSD_CONV]], axis=0))
            dbs.append(jnp.sum(acc[SSD_CONV], axis=0, keepdims=True))
        for r0, cc in _conv_pieces(tm):
            cs = slice(cc, cc + CONV_CB)
            dx = jnp.zeros((rb, CONV_CB), F32)
            for k in range(SSD_CONV):
                off = SSD_CONV - 1 - k + r0
                dx = dx + w_ref[k:k + 1, cs] * dd[off:off + rb, cs]
            dp_ref[r0:r0 + rb, cs] = dx.astype(BF16)
        _acc_out(dw_ref, jnp.concatenate(dws, axis=1), first)
        _acc_out(db_ref, jnp.concatenate(dbs, axis=1), first)
        dd[tm:tm + CONV_HALO] = dd[0:CONV_HALO]

    rt = lambda i: nt - 1 - i
    return pl.pallas_call(
        body, grid=(SSD_CONV_DIM // CONV_COLS, nt),
        in_specs=[
            pl.BlockSpec((tm, CONV_COLS), lambda j, i: (rt(i), c0 + j)),
            pl.BlockSpec((CONV_HALO, CONV_COLS), lambda j, i: (jnp.maximum(rt(i) * hb - 1, 0), c0 + j)),
            pl.BlockSpec((SSD_CONV, CONV_COLS), lambda j, i: (0, j)),
            pl.BlockSpec((tm, CONV_COLS), lambda j, i: (rt(i), j)),
            pl.BlockSpec((tm, CONV_COLS), lambda j, i: (rt(i), j)),
            pl.BlockSpec(memory_space=pl.ANY),
        ],
        out_specs=[
            pl.BlockSpec((tm, CONV_COLS), lambda j, i: (rt(i), c0 + j)),
            pl.BlockSpec((SSD_CONV, CONV_COLS), lambda j, i: (0, j)),
            pl.BlockSpec((1, CONV_COLS), lambda j, i: (0, j)),
        ],
        out_shape=[jax.ShapeDtypeStruct((L, SSD_PROJ_PAD), BF16), jax.ShapeDtypeStruct((SSD_CONV, SSD_CONV_DIM), F32),
                   jax.ShapeDtypeStruct((1, SSD_CONV_DIM), F32)],
        scratch_shapes=[pltpu.VMEM((CONV_HALO + tm, CONV_COLS), F32), pltpu.VMEM((tm + CONV_HALO, CONV_COLS), F32)],
        input_output_aliases={5: 0},
        compiler_params=_cparams(("arbitrary", "arbitrary")), name="conv_bwd",
    )(proj, proj, w, sp, dxbc, dproj)


def _loss_bwd(x, tgt, w):
    L, D = x.shape
    tm = min(TOKEN_TILE, L)

    def body(x_ref, t_ref, w_ref, l_ref, dx_ref, dw_ref):
        xv = x_ref[...]
        wv = w_ref[...]
        r = lax.rsqrt(jnp.mean(xv * xv, axis=-1, keepdims=True) + RMS_EPS)
        xh = xv * r
        e = xh * wv - t_ref[...]
        lsum = 0.5 * jnp.sum(jnp.mean(e * e, axis=-1, keepdims=True), axis=0, keepdims=True)
        dout = e * (1.0 / D)
        gx = dout * wv
        dx_ref[...] = r * (gx - xh * jnp.mean(gx * xh, axis=-1, keepdims=True))
        first = pl.program_id(0) == 0
        _acc_out(dw_ref, jnp.sum(dout * xh, axis=0, keepdims=True), first)
        _acc_out(l_ref, jnp.broadcast_to(lsum, (8, 128)), first)

    row = pl.BlockSpec((tm, D), lambda i: (i, 0))
    vec = pl.BlockSpec((1, D), lambda i: (0, 0))
    return pl.pallas_call(
        body, grid=(L // tm,),
        in_specs=[row, row, vec],
        out_specs=[pl.BlockSpec((8, 128), lambda i: (0, 0)), row, vec],
        out_shape=[jax.ShapeDtypeStruct((8, 128), F32), jax.ShapeDtypeStruct((L, D), F32), jax.ShapeDtypeStruct((1, D), F32)],
        compiler_params=_cparams(("arbitrary",)), name="loss_bwd",
    )(x, tgt, w)


MESH = pl.DeviceIdType.MESH
ANY = pl.BlockSpec(memory_space=pl.ANY)


def _comm_sems(n):
    return [pltpu.SemaphoreType.DMA((n, 7)), pltpu.SemaphoreType.DMA((n, 7)), pltpu.SemaphoreType.DMA((n,))]


def _gather_ops(x_refs, out_refs, send_sems, recv_sems, local_sems):
    n = len(x_refs)
    x, y, c = lax.axis_index("x"), lax.axis_index("y"), lax.axis_index("c")
    me, sibling = (x, y, c), (x, y, 1 - c)
    chips = [(1 - x, y), (x, 1 - y), (1 - x, 1 - y)]

    def slot(i, px, py, pc):
        return out_refs[i].at[4 * px + 2 * py + pc]

    def copy(i, k, block, to, src=None):
        return pltpu.make_async_remote_copy(
            src_ref=slot(i, *block) if src is None else src, dst_ref=slot(i, *block),
            send_sem=send_sems.at[i, k], recv_sem=recv_sems.at[i, k], device_id=to, device_id_type=MESH)

    def own():
        mine = [pltpu.make_async_copy(x_refs[i], slot(i, *me), local_sems.at[i]) for i in range(n)]
        first = [copy(i, 0, me, sibling, src=x_refs[i]) for i in range(n)]
        first += [copy(i, 1 + j, me, (*chip, c), src=x_refs[i]) for j, chip in enumerate(chips) for i in range(n)]
        return mine, first

    def start():
        mine, first = own()
        for cp in mine + first:
            cp.start()

    def finish():
        mine, first = own()
        passed = []
        for j, chip in enumerate(chips):
            for i in range(n):
                copy(i, 1 + j, (*chip, c), me).wait_recv()
                passed.append(copy(i, 4 + j, (*chip, c), sibling))
                passed[-1].start()
        for i in range(n):
            copy(i, 0, sibling, me).wait_recv()
        for j, chip in enumerate(chips):
            for i in range(n):
                copy(i, 4 + j, (*chip, 1 - c), me).wait_recv()
        for cp in first + passed:
            cp.wait_send()
        for cp in mine:
            cp.wait()

    return start, finish


def _exchange_ops(p_refs, out_refs, send_sems, recv_sems, local_sems):
    n = len(p_refs)
    x, y, c = lax.axis_index("x"), lax.axis_index("y"), lax.axis_index("c")
    my = 4 * x + 2 * y + c

    def peer(k):
        fx, fy, fc = (k >> 2) & 1, (k >> 1) & 1, k & 1
        px, py, pc = (1 - x if fx else x), (1 - y if fy else y), (1 - c if fc else c)
        return (px, py, pc), 4 * px + 2 * py + pc

    def mine():
        return [pltpu.make_async_copy(p_refs[i].at[my], out_refs[i].at[my], local_sems.at[i]) for i in range(n)]

    def start():
        for cp in mine():
            cp.start()
        for k in range(1, N_DEV):
            to, pid = peer(k)
            for i in range(n):
                pltpu.make_async_remote_copy(
                    src_ref=p_refs[i].at[pid], dst_ref=out_refs[i].at[my], send_sem=send_sems.at[i, k - 1],
                    recv_sem=recv_sems.at[i, k - 1], device_id=to, device_id_type=MESH).start()

    def finish():
        for k in range(1, N_DEV):
            to, pid = peer(k)
            for i in range(n):
                pltpu.make_async_remote_copy(
                    src_ref=p_refs[i].at[pid], dst_ref=out_refs[i].at[pid], send_sem=send_sems.at[i, k - 1],
                    recv_sem=recv_sems.at[i, k - 1], device_id=to, device_id_type=MESH).wait()
        for cp in mine():
            cp.wait()

    return start, finish


def _all_gather(xs, *, name):
    n = len(xs)

    def body(*refs):
        start, finish = _gather_ops(refs[:n], refs[n:2 * n], *refs[2 * n:])
        start()
        finish()

    return pl.pallas_call(
        body,
        out_shape=[jax.ShapeDtypeStruct((N_DEV,) + a.shape, a.dtype) for a in xs],
        in_specs=[ANY] * n, out_specs=[ANY] * n, scratch_shapes=_comm_sems(n), name=name,
    )(*xs)


def _adamw(parts, w, m, v, *, name):
    a, b = w.shape
    tr = _pick(a, (256, 128, 64, 32, 16, 8))

    def body(p_ref, w_ref, m_ref, v_ref, g_ref, d_ref, mo_ref, vo_ref):
        g = p_ref[0]
        for s in range(1, N_DEV):
            g = g + p_ref[s]
        mn = ADAM_B1 * m_ref[...] + (1.0 - ADAM_B1) * g
        vn = ADAM_B2 * v_ref[...] + (1.0 - ADAM_B2) * jnp.square(g)
        m_hat = mn / (1.0 - ADAM_B1 ** ADAM_STEP)
        v_hat = vn / (1.0 - ADAM_B2 ** ADAM_STEP)
        g_ref[...] = g
        d_ref[...] = -ADAM_LR * (m_hat / (jnp.sqrt(v_hat) + ADAM_EPS) + ADAM_WD * w_ref[...])
        mo_ref[...] = mn
        vo_ref[...] = vn

    blk = pl.BlockSpec((tr, b), lambda i: (i, 0))
    shp = jax.ShapeDtypeStruct((a, b), F32)
    return pl.pallas_call(
        body, grid=(a // tr,),
        in_specs=[pl.BlockSpec((N_DEV, tr, b), lambda i: (0, i, 0)), blk, blk, blk],
        out_specs=[blk, blk, blk, blk],
        out_shape=[shp, shp, shp, shp],
        compiler_params=_cparams(("parallel",)), name=name,
    )(parts, w, m, v)


def _sum_slots(parts, *, name):
    def body(p_ref, o_ref):
        s = p_ref[0]
        for k in range(1, N_DEV):
            s = s + p_ref[k]
        o_ref[...] = s

    return pl.pallas_call(body, out_shape=jax.ShapeDtypeStruct(parts.shape[1:], parts.dtype), name=name)(parts)


def _col_shards(a, n):
    return a.reshape(a.shape[0], N_DEV, n).transpose(1, 0, 2)


def _from_col_shards(g, cols):
    r = g.shape[1]
    full = g.transpose(1, 0, 2).reshape(r, -1)
    return jnp.pad(full, ((0, 0), (0, cols - full.shape[1])))


def kernel(x, norm_w, gla_in_proj, gla_gate_up, gla_gate_bias, gla_head_norm, gla_out_proj, ssd_in_proj, ssd_conv_w, ssd_conv_b, ssd_dt_bias, ssd_a_log, ssd_d, ssd_gate_norm, ssd_out_proj, final_norm, loss_target, m_norm_w, m_gla_in_proj, m_gla_gate_up, m_gla_gate_bias, m_gla_head_norm, m_gla_out_proj, m_ssd_in_proj, m_ssd_conv_w, m_ssd_conv_b, m_ssd_dt_bias, m_ssd_a_log, m_ssd_d, m_ssd_gate_norm, m_ssd_out_proj, m_final_norm, v_norm_w, v_gla_in_proj, v_gla_gate_up, v_gla_gate_bias, v_gla_head_norm, v_gla_out_proj, v_ssd_in_proj, v_ssd_conv_w, v_ssd_conv_b, v_ssd_dt_bias, v_ssd_a_log, v_ssd_d, v_ssd_gate_norm, v_ssd_out_proj, v_final_norm):
    x0 = x[0]
    tgt = loss_target[0]
    n_gin = GLA_PROJ // N_DEV
    n_sin = SSD_PROJ // N_DEV
    n_up = GLA_DK // N_DEV
    n_cv = SSD_CONV_DIM // N_DEV

    vec128 = lambda a: jnp.pad(a.reshape(1, -1), ((0, 0), (0, 128 - a.size)))
    dtb, alog, dsk = vec128(ssd_dt_bias), vec128(ssd_a_log), vec128(ssd_d)
    nw0, nw1 = norm_w[0:1], norm_w[1:2]

    hn1, (g_gin,) = _rms_fwd(x0, nw0, name="rms1_fwd", gather=[gla_in_proj[0].astype(BF16)])
    w_gin = _from_col_shards(g_gin, GLA_PROJ_PAD)
    proj1, (g_up, g_gout) = _mm(hn1, w_gin, name="gla_in_proj",
                                comm=("gather", [gla_gate_up[0].astype(BF16), gla_out_proj[0].astype(BF16)]))
    wup = jnp.pad(_from_col_shards(g_up, GLA_DK), ((0, 128 - GLA_RANK), (0, 0))).astype(F32)
    w_gout = g_gout.reshape(D_INNER, D_MODEL)
    (o, og, s_saved), (g_sin, g_sout, g_cw, g_cb, g_gn) = _gla_layer_fwd(
        proj1, wup, gla_gate_bias, gla_head_norm,
        [ssd_in_proj[0].astype(BF16), ssd_out_proj[0].astype(BF16), ssd_conv_w[0], ssd_conv_b, ssd_gate_norm])
    w_sin = _from_col_shards(g_sin, SSD_PROJ_PAD)
    w_sout = g_sout.reshape(D_INNER, D_MODEL)
    conv_w = _from_col_shards(g_cw, SSD_CONV_DIM)
    conv_b = g_cb.reshape(1, SSD_CONV_DIM)
    gate_norm = g_gn.reshape(1, D_INNER)
    x1 = _mm(og, w_gout, add=x0, name="gla_out_proj")
    hn2 = _rms_fwd(x1, nw1, name="rms2_fwd")
    proj2 = _mm(hn2, w_sin, name="ssd_in_proj")
    xbc, conv_sp = _conv_fwd(proj2, conv_w, conv_b)
    yn, h_saved = _ssd_layer_fwd(xbc, proj2, dtb, alog, dsk, gate_norm)
    x2 = _mm(yn, w_sout, add=x1, name="ssd_out_proj")
    lsum, dx2, d_final = _loss_bwd(x2, tgt, final_norm.reshape(1, D_MODEL))

    d_sout = _mm_tn(yn, dx2, name="ssd_out_proj_dw")
    dyn = _mm(dx2, w_sout.T, name="ssd_out_proj_dx")
    dproj2, dxbc, d_dtb, d_alog, d_dsk, d_gate_norm = _ssd_layer_bwd(xbc, proj2, dtb, alog, dsk, gate_norm, h_saved, dyn)
    dproj2, d_conv_w, d_conv_b = _conv_bwd(proj2, conv_w, conv_sp, dxbc, dproj2)
    d_sin = _mm_tn(hn2, dproj2, name="ssd_in_proj_dw")
    dhn2 = _mm(dproj2, w_sin.T, name="ssd_in_proj_dx")
    dx1, d_nw1 = _rms_bwd(x1, nw1, dhn2, dx2, name="rms2_bwd")
    d_gout = _mm_tn(og, dx1, name="gla_out_proj_dw")
    dog = _mm(dx1, w_gout.T, name="gla_out_proj_dx")
    early = {
        "gla_out_proj": ((gla_out_proj[0], m_gla_out_proj[0], v_gla_out_proj[0]), d_gout.reshape(N_DEV, -1, D_MODEL)),
        "ssd_in_proj": ((ssd_in_proj[0], m_ssd_in_proj[0], v_ssd_in_proj[0]), _col_shards(d_sin[:, :SSD_PROJ], n_sin)),
        "ssd_conv_w": ((ssd_conv_w[0], m_ssd_conv_w[0], v_ssd_conv_w[0]), _col_shards(d_conv_w, n_cv)),
        "ssd_conv_b": ((ssd_conv_b, m_ssd_conv_b, v_ssd_conv_b), d_conv_b.reshape(N_DEV, 1, n_cv)),
        "ssd_gate_norm": ((ssd_gate_norm, m_ssd_gate_norm, v_ssd_gate_norm), d_gate_norm.reshape(N_DEV, 1, -1)),
        "ssd_out_proj": ((ssd_out_proj[0], m_ssd_out_proj[0], v_ssd_out_proj[0]), d_sout.reshape(N_DEV, -1, D_MODEL)),
    }
    (dproj1, d_wup, d_gbias, d_head_norm), early_recv = _gla_layer_bwd(
        proj1, wup, gla_gate_bias, gla_head_norm, o, s_saved, dog, [p for _, p in early.values()])
    d_gin = _mm_tn(hn1, dproj1, name="gla_in_proj_dw")
    late = {
        "gla_in_proj": ((gla_in_proj[0], m_gla_in_proj[0], v_gla_in_proj[0]), _col_shards(d_gin[:, :GLA_PROJ], n_gin)),
        "gla_gate_up": ((gla_gate_up[0], m_gla_gate_up[0], v_gla_gate_up[0]), _col_shards(d_wup[:GLA_RANK], n_up)),
    }
    dhn1, late_recv = _mm(dproj1, w_gin.T, name="gla_in_proj_dx", comm=("exchange", [p for _, p in late.values()]))
    dx0, d_nw0 = _rms_bwd(x0, nw0, dhn1, dx1, name="rms1_bwd")
    heads = SSD_HEADS
    replicated = {
        "norm_w": ((norm_w, m_norm_w, v_norm_w), jnp.concatenate([d_nw0, d_nw1], axis=0)),
        "gla_gate_bias": ((gla_gate_bias, m_gla_gate_bias, v_gla_gate_bias), d_gbias),
        "gla_head_norm": ((gla_head_norm, m_gla_head_norm, v_gla_head_norm), d_head_norm),
        "ssd_dt_bias": ((ssd_dt_bias, m_ssd_dt_bias, v_ssd_dt_bias), d_dtb[:, :heads]),
        "ssd_a_log": ((ssd_a_log, m_ssd_a_log, v_ssd_a_log), d_alog[:, :heads]),
        "ssd_d": ((ssd_d, m_ssd_d, v_ssd_d), d_dsk[:, :heads]),
        "final_norm": (tuple(t.reshape(1, D_MODEL) for t in (final_norm, m_final_norm, v_final_norm)), d_final),
    }
    rep_recv = _all_gather([p for _, p in replicated.values()] + [lsum[0:1]], name="replicated_gather")
    loss = _sum_slots(rep_recv[-1], name="loss_sum")[0, 0]
    results = {}
    for group, recv in ((early, early_recv), (late, late_recv), (replicated, rep_recv)):
        for (nm, ((w, m, v), _)), r in zip(group.items(), recv):
            results[nm] = _adamw(r, w, m, v, name=nm + "_adamw")

    order = [("norm_w", norm_w), ("gla_in_proj", gla_in_proj), ("gla_gate_up", gla_gate_up), ("gla_gate_bias", gla_gate_bias),
             ("gla_head_norm", gla_head_norm), ("gla_out_proj", gla_out_proj), ("ssd_in_proj", ssd_in_proj),
             ("ssd_conv_w", ssd_conv_w), ("ssd_conv_b", ssd_conv_b), ("ssd_dt_bias", ssd_dt_bias), ("ssd_a_log", ssd_a_log),
             ("ssd_d", ssd_d), ("ssd_gate_norm", ssd_gate_norm), ("ssd_out_proj", ssd_out_proj), ("final_norm", final_norm)]
    out = [loss, dx0[None]]
    for i in range(4):
        out += [results[nm][i].reshape(ref.shape) for nm, ref in order]
    return tuple(out)
```
